```python
import math
import jax, jax.numpy as jnp
from jax import lax
import numpy as np

D_MODEL = 2048
BATCH = 8
SEQ = 4096
DEPTH = 2

HEAD_DIM = 128
A_WIDTH = D_MODEL // 2
A_GROUPS = A_WIDTH // 128
CHUNK = 128
B_HEADS = (D_MODEL // 2) // HEAD_DIM
B_WIDTH = B_HEADS * HEAD_DIM
DILATED_PAIRS = ((128, 1), (512, 4), (2048, 16))
ATT_BLOCK = 128
C_HEADS = D_MODEL // HEAD_DIM
C_WIDTH = C_HEADS * HEAD_DIM
D_FF = 4 * D_MODEL
N_EVEN = (DEPTH + 1) // 2
N_ODD = DEPTH // 2
RMS_EPS = 1e-6
LN_EPS = 1e-5

kernel_name = "hybrid_gmlp_dilated_stickbreak_trunk"


def rmsnorm(x, g):
    xf = x.astype(jnp.float32)
    y = xf * lax.rsqrt(jnp.mean(xf * xf, axis=-1, keepdims=True) + RMS_EPS)
    return (y * g.astype(jnp.float32)).astype(x.dtype)


def layernorm(x, g, b):
    xf = x.astype(jnp.float32)
    mu = jnp.mean(xf, axis=-1, keepdims=True)
    var = jnp.mean(jnp.square(xf - mu), axis=-1, keepdims=True)
    y = (xf - mu) * lax.rsqrt(var + LN_EPS)
    return (y * g.astype(jnp.float32) + b.astype(jnp.float32)).astype(x.dtype)


def alibi_slopes(n):
    return jnp.exp2(-8.0 * (jnp.arange(n, dtype=jnp.float32) + 1.0) / n)


def spatial_gating_unit(u, v, ln_g, ln_b, w_s, b_s):
    bsz, t, _ = v.shape
    v = layernorm(v, ln_g, ln_b)
    vg = v.reshape(bsz, t // CHUNK, CHUNK, A_GROUPS, A_WIDTH // A_GROUPS)
    w = jnp.tril(w_s).astype(v.dtype)
    mixed = jnp.einsum('gij,bcjgd->bcigd', w, vg) + b_s.T.astype(v.dtype)[None, None, :, :, None]
    return u * mixed.reshape(bsz, t, A_WIDTH)


def dilated_branch(q, k, v, window, dilation, slopes):
    bsz, t, h, dh = q.shape
    blk = ATT_BLOCK
    win_sub = window // dilation
    period = dilation * blk
    tp = -(-t // period) * period
    pad = tp - t
    sub_len = tp // dilation
    nb = sub_len // blk

    def to_sub(a):
        a = jnp.pad(a, ((0, 0), (0, pad), (0, 0), (0, 0)))
        a = a.reshape(bsz, sub_len, dilation, h, dh).transpose(0, 2, 3, 1, 4)
        return a.reshape(bsz, dilation, h, nb, blk, dh)

    qs, ks, vs = to_sub(q), to_sub(k), to_sub(v)

    def with_prev(a):
        prev = jnp.concatenate([jnp.zeros_like(a[:, :, :, :1]), a[:, :, :, :-1]], axis=3)
        return jnp.concatenate([prev, a], axis=4)

    kw, vw = with_prev(ks), with_prev(vs)
    s = jnp.einsum('brhnqe,brhnke->brhnqk', qs, kw).astype(jnp.float32)

    qi = jnp.arange(blk)[:, None]
    kj = jnp.arange(2 * blk)[None, :]
    dist = qi + blk - kj
    band = (dist >= 0) & (dist <= win_sub)
    first = (jnp.arange(nb) == 0)[:, None, None] & (kj < blk)[None]
    valid = band[None] & ~first
    bias = -slopes[:, None, None] * (dist * dilation).astype(jnp.float32)[None]
    s = s + bias[None, None, :, None]
    s = jnp.where(valid[None, None, None], s, -jnp.inf)
    m = jnp.max(s, axis=-1, keepdims=True)
    p = jnp.exp(s - m)
    den = jnp.sum(p, axis=-1, keepdims=True)
    o = jnp.einsum('brhnqk,brhnke->brhnqe', p, vw.astype(jnp.float32)) / den
    lse = (m + jnp.log(den))[..., 0]

    o = o.reshape(bsz, dilation, h, sub_len, dh).transpose(0, 3, 1, 2, 4).reshape(bsz, tp, h, dh)[:, :t]
    lse = lse.reshape(bsz, dilation, h, sub_len).transpose(0, 3, 1, 2).reshape(bsz, tp, h)[:, :t]
    return o, lse


def dilated_mixture(q, k, v):
    slopes = alibi_slopes(q.shape[2])
    outs, lses = [], []
    for window, dilation in DILATED_PAIRS:
        o, lse = dilated_branch(q, k, v, window, dilation, slopes)
        outs.append(o)
        lses.append(lse)
    wts = jax.nn.softmax(jnp.stack(lses, axis=0), axis=0)
    o = jnp.sum(wts[..., None] * jnp.stack(outs, axis=0), axis=0)
    return o.astype(q.dtype)


def gmlp_dilated_layer(h, w_in, ln_g, ln_b, w_s, b_s, w_out):
    bsz, t, _ = h.shape
    z = h @ w_in
    u, va, q, k, vb = jnp.split(z, [A_WIDTH, 2 * A_WIDTH, 2 * A_WIDTH + B_WIDTH, 2 * A_WIDTH + 2 * B_WIDTH], axis=-1)
    a_out = spatial_gating_unit(jax.nn.gelu(u, approximate=False), jax.nn.gelu(va, approximate=False), ln_g, ln_b, w_s, b_s)
    q = q.reshape(bsz, t, B_HEADS, HEAD_DIM) * (HEAD_DIM ** -0.5)
    k = k.reshape(bsz, t, B_HEADS, HEAD_DIM)
    vb = vb.reshape(bsz, t, B_HEADS, HEAD_DIM)
    b_out = dilated_mixture(q, k, vb).reshape(bsz, t, B_WIDTH)
    return jnp.concatenate([a_out, b_out], axis=-1) @ w_out


def stick_breaking_attention(q, k, v):
    bsz, h, t, dh = q.shape
    nb = t // ATT_BLOCK
    qb = q.reshape(bsz, h, nb, ATT_BLOCK, dh).transpose(2, 0, 1, 3, 4)
    kpos = jnp.arange(t)

    def block(args):
        qblk, start = args
        z = jnp.einsum('bhqe,bhke->bhqk', qblk, k).astype(jnp.float32)
        qpos = start + jnp.arange(ATT_BLOCK)
        causal = kpos[None, :] < qpos[:, None]
        log_1m_beta = jnp.where(causal, jax.nn.log_sigmoid(-z), 0.0)
        rev = lax.cumsum(log_1m_beta, axis=3, reverse=True)
        excl = jnp.concatenate([rev[..., 1:], jnp.zeros_like(rev[..., :1])], axis=-1)
        a = jnp.where(causal, jnp.exp(jax.nn.log_sigmoid(z) + excl), 0.0)
        return jnp.einsum('bhqk,bhke->bhqe', a.astype(v.dtype), v)

    o = lax.map(block, (qb, jnp.arange(nb) * ATT_BLOCK))
    return o.transpose(1, 0, 3, 2, 4).reshape(bsz, t, h * dh)


def stick_breaking_layer(h, w_in, w_out):
    bsz, t, _ = h.shape
    q, k, v = jnp.split(h @ w_in, 3, axis=-1)
    def heads(a):
        return a.reshape(bsz, t, C_HEADS, HEAD_DIM).transpose(0, 2, 1, 3)
    o = stick_breaking_attention(heads(q) * (HEAD_DIM ** -0.5), heads(k), heads(v))
    return o @ w_out


def squared_relu_mlp(h, w1, w2):
    return jnp.square(jax.nn.relu(h @ w1)) @ w2


def _fwd_setup_inputs(seed: int = 0) -> dict:
    key = jax.random.key(seed)
    ks = jax.random.split(key, 16)
    f32 = jnp.float32
    def nrm(k, shape, scale):
        return jax.random.normal(k, shape, f32) * scale
    def gain(k, shape):
        return 1.0 + 0.02 * jax.random.normal(k, shape, f32)
    return {
        "x": jax.random.normal(ks[0], (BATCH, SEQ, D_MODEL), f32),
        "norm_pre_mix": gain(ks[1], (DEPTH, D_MODEL)),
        "norm_post_mix": gain(ks[2], (DEPTH, D_MODEL)),
        "norm_pre_ffn": gain(ks[3], (DEPTH, D_MODEL)),
        "norm_post_ffn": gain(ks[4], (DEPTH, D_MODEL)),
        "ab_w_in": nrm(ks[5], (N_EVEN, D_MODEL, 2 * A_WIDTH + 3 * B_WIDTH), D_MODEL ** -0.5),
        "sgu_ln_g": gain(ks[6], (N_EVEN, A_WIDTH)),
        "sgu_ln_b": nrm(ks[7], (N_EVEN, A_WIDTH), 0.02),
        "sgu_w": nrm(ks[8], (N_EVEN, A_GROUPS, CHUNK, CHUNK), CHUNK ** -0.5),
        "sgu_b": 1.0 + nrm(ks[9], (N_EVEN, A_GROUPS, CHUNK), 0.1),
        "ab_w_out": nrm(ks[10], (N_EVEN, A_WIDTH + B_WIDTH, D_MODEL), (A_WIDTH + B_WIDTH) ** -0.5),
        "sb_w_in": nrm(ks[11], (N_ODD, D_MODEL, 3 * C_WIDTH), D_MODEL ** -0.5),
        "sb_w_out": nrm(ks[12], (N_ODD, C_WIDTH, D_MODEL), C_WIDTH ** -0.5),
        "ffn_w1": nrm(ks[13], (DEPTH, D_MODEL, D_FF), D_MODEL ** -0.5),
        "ffn_w2": nrm(ks[14], (DEPTH, D_FF, D_MODEL), D_FF ** -0.5),
    }


def _fwd_reference(x, norm_pre_mix, norm_post_mix, norm_pre_ffn, norm_post_ffn,
              ab_w_in, sgu_ln_g, sgu_ln_b, sgu_w, sgu_b, ab_w_out,
              sb_w_in, sb_w_out, ffn_w1, ffn_w2):
    for layer in range(DEPTH):
        h = rmsnorm(x, norm_pre_mix[layer])
        i = layer // 2
        if layer % 2 == 0:
            y = gmlp_dilated_layer(h, ab_w_in[i], sgu_ln_g[i], sgu_ln_b[i], sgu_w[i], sgu_b[i], ab_w_out[i])
        else:
            y = stick_breaking_layer(h, sb_w_in[i], sb_w_out[i])
        x = x + rmsnorm(y, norm_post_mix[layer])
        h = rmsnorm(x, norm_pre_ffn[layer])
        y = squared_relu_mlp(h, ffn_w1[layer], ffn_w2[layer])
        x = x + rmsnorm(y, norm_post_ffn[layer])
    return x


import jax as _jax
import jax.numpy as _jnp

TWIN_FORMAT = 'train_step'
FWD_PARAMS = ['x', 'norm_pre_mix', 'norm_post_mix', 'norm_pre_ffn', 'norm_post_ffn', 'ab_w_in', 'sgu_ln_g', 'sgu_ln_b', 'sgu_w', 'sgu_b', 'ab_w_out', 'sb_w_in', 'sb_w_out', 'ffn_w1', 'ffn_w2']
TWIN_WEIGHTS = ['norm_pre_mix', 'norm_post_mix', 'norm_pre_ffn', 'norm_post_ffn', 'ab_w_in', 'sgu_ln_g', 'sgu_ln_b', 'sgu_w', 'sgu_b', 'ab_w_out', 'sb_w_in', 'sb_w_out', 'ffn_w1', 'ffn_w2']
TWIN_DIFF_INPUT = 'x'
TWIN_INPUTS = ['x', 'norm_pre_mix', 'norm_post_mix', 'norm_pre_ffn', 'norm_post_ffn', 'ab_w_in', 'sgu_ln_g', 'sgu_ln_b', 'sgu_w', 'sgu_b', 'ab_w_out', 'sb_w_in', 'sb_w_out', 'ffn_w1', 'ffn_w2', 'loss_target', 'm_norm_pre_mix', 'm_norm_post_mix', 'm_norm_pre_ffn', 'm_norm_post_ffn', 'm_ab_w_in', 'm_sgu_ln_g', 'm_sgu_ln_b', 'm_sgu_w', 'm_sgu_b', 'm_ab_w_out', 'm_sb_w_in', 'm_sb_w_out', 'm_ffn_w1', 'm_ffn_w2', 'v_norm_pre_mix', 'v_norm_post_mix', 'v_norm_pre_ffn', 'v_norm_post_ffn', 'v_ab_w_in', 'v_sgu_ln_g', 'v_sgu_ln_b', 'v_sgu_w', 'v_sgu_b', 'v_ab_w_out', 'v_sb_w_in', 'v_sb_w_out', 'v_ffn_w1', 'v_ffn_w2']
TWIN_OUTPUTS = ['loss', 'grad_x', 'grad_norm_pre_mix', 'grad_norm_post_mix', 'grad_norm_pre_ffn', 'grad_norm_post_ffn', 'grad_ab_w_in', 'grad_sgu_ln_g', 'grad_sgu_ln_b', 'grad_sgu_w', 'grad_sgu_b', 'grad_ab_w_out', 'grad_sb_w_in', 'grad_sb_w_out', 'grad_ffn_w1', 'grad_ffn_w2', 'delta_norm_pre_mix', 'delta_norm_post_mix', 'delta_norm_pre_ffn', 'delta_norm_post_ffn', 'delta_ab_w_in', 'delta_sgu_ln_g', 'delta_sgu_ln_b', 'delta_sgu_w', 'delta_sgu_b', 'delta_ab_w_out', 'delta_sb_w_in', 'delta_sb_w_out', 'delta_ffn_w1', 'delta_ffn_w2', 'new_m_norm_pre_mix', 'new_m_norm_post_mix', 'new_m_norm_pre_ffn', 'new_m_norm_post_ffn', 'new_m_ab_w_in', 'new_m_sgu_ln_g', 'new_m_sgu_ln_b', 'new_m_sgu_w', 'new_m_sgu_b', 'new_m_ab_w_out', 'new_m_sb_w_in', 'new_m_sb_w_out', 'new_m_ffn_w1', 'new_m_ffn_w2', 'new_v_norm_pre_mix', 'new_v_norm_post_mix', 'new_v_norm_pre_ffn', 'new_v_norm_post_ffn', 'new_v_ab_w_in', 'new_v_sgu_ln_g', 'new_v_sgu_ln_b', 'new_v_sgu_w', 'new_v_sgu_b', 'new_v_ab_w_out', 'new_v_sb_w_in', 'new_v_sb_w_out', 'new_v_ffn_w1', 'new_v_ffn_w2']
TWIN_LEAF_KINDS = {'loss': 'loss', 'grad_x': 'grad_x', 'grad_norm_pre_mix': 'grad_w', 'grad_norm_post_mix': 'grad_w', 'grad_norm_pre_ffn': 'grad_w', 'grad_norm_post_ffn': 'grad_w', 'grad_ab_w_in': 'grad_w', 'grad_sgu_ln_g': 'grad_w', 'grad_sgu_ln_b': 'grad_w', 'grad_sgu_w': 'grad_w', 'grad_sgu_b': 'grad_w', 'grad_ab_w_out': 'grad_w', 'grad_sb_w_in': 'grad_w', 'grad_sb_w_out': 'grad_w', 'grad_ffn_w1': 'grad_w', 'grad_ffn_w2': 'grad_w', 'delta_norm_pre_mix': 'delta_w', 'delta_norm_post_mix': 'delta_w', 'delta_norm_pre_ffn': 'delta_w', 'delta_norm_post_ffn': 'delta_w', 'delta_ab_w_in': 'delta_w', 'delta_sgu_ln_g': 'delta_w', 'delta_sgu_ln_b': 'delta_w', 'delta_sgu_w': 'delta_w', 'delta_sgu_b': 'delta_w', 'delta_ab_w_out': 'delta_w', 'delta_sb_w_in': 'delta_w', 'delta_sb_w_out': 'delta_w', 'delta_ffn_w1': 'delta_w', 'delta_ffn_w2': 'delta_w', 'new_m_norm_pre_mix': 'new_m', 'new_m_norm_post_mix': 'new_m', 'new_m_norm_pre_ffn': 'new_m', 'new_m_norm_post_ffn': 'new_m', 'new_m_ab_w_in': 'new_m', 'new_m_sgu_ln_g': 'new_m', 'new_m_sgu_ln_b': 'new_m', 'new_m_sgu_w': 'new_m', 'new_m_sgu_b': 'new_m', 'new_m_ab_w_out': 'new_m', 'new_m_sb_w_in': 'new_m', 'new_m_sb_w_out': 'new_m', 'new_m_ffn_w1': 'new_m', 'new_m_ffn_w2': 'new_m', 'new_v_norm_pre_mix': 'new_v', 'new_v_norm_post_mix': 'new_v', 'new_v_norm_pre_ffn': 'new_v', 'new_v_norm_post_ffn': 'new_v', 'new_v_ab_w_in': 'new_v', 'new_v_sgu_ln_g': 'new_v', 'new_v_sgu_ln_b': 'new_v', 'new_v_sgu_w': 'new_v', 'new_v_sgu_b': 'new_v', 'new_v_ab_w_out': 'new_v', 'new_v_sb_w_in': 'new_v', 'new_v_sb_w_out': 'new_v', 'new_v_ffn_w1': 'new_v', 'new_v_ffn_w2': 'new_v'}


def _forward(args):
    return _fwd_reference(*[args[k] for k in FWD_PARAMS])


def _output_shape():
    def fwd():
        inp = _fwd_setup_inputs(0)
        return _fwd_reference(*[inp[k] for k in FWD_PARAMS])
    out = _jax.eval_shape(fwd)
    return out.shape, out.dtype

N_MICROBATCH = 1
ADAM_LR = 0.001
ADAM_B1 = 0.9
ADAM_B2 = 0.999
ADAM_EPS = 1e-08
ADAM_WD = 0.01
ADAM_STEP = 10
PER_EXAMPLE_BATCH_AXIS = {'x': 0, 'loss_target': 0}
SHARED_INPUTS = []
_WEIGHT_DTYPES = {'norm_pre_mix': _jnp.float32, 'norm_post_mix': _jnp.float32, 'norm_pre_ffn': _jnp.float32, 'norm_post_ffn': _jnp.float32, 'ab_w_in': _jnp.float32, 'sgu_ln_g': _jnp.float32, 'sgu_ln_b': _jnp.float32, 'sgu_w': _jnp.float32, 'sgu_b': _jnp.float32, 'ab_w_out': _jnp.float32, 'sb_w_in': _jnp.float32, 'sb_w_out': _jnp.float32, 'ffn_w1': _jnp.float32, 'ffn_w2': _jnp.float32}
MOMENT_SCALE = {'norm_pre_mix': 3.790305e+00, 'norm_post_mix': 1.711040e+01, 'norm_pre_ffn': 2.530939e+00, 'norm_post_ffn': 1.749969e+01, 'ab_w_in': 3.401114e-01, 'sgu_ln_g': 3.045926e-01, 'sgu_ln_b': 3.618797e-01, 'sgu_w': 2.574660e-01, 'sgu_b': 5.648661e-01, 'ab_w_out': 5.888389e+00, 'sb_w_in': 3.208320e+00, 'sb_w_out': 5.333265e+00, 'ffn_w1': 1.284415e+00, 'ffn_w2': 6.130340e+00}


def _to_microbatches(a, axis):
    t = _jnp.moveaxis(a, axis, 0)
    t = t.reshape((N_MICROBATCH, t.shape[0] // N_MICROBATCH) + t.shape[1:])
    return _jnp.moveaxis(t, 1, axis + 1)


def setup_inputs(seed: int = 0) -> dict:
    inp = _fwd_setup_inputs(seed)
    key = _jax.random.fold_in(_jax.random.key(seed), 7919)
    shape, _ = _output_shape()
    out = dict(inp)
    out["loss_target"] = _jax.random.normal(_jax.random.fold_in(key, 0), shape, _jnp.float32)
    for i, name in enumerate(TWIN_WEIGHTS):
        w = inp[name].astype(_jnp.float32)
        if MOMENT_SCALE is None:
            s = _jnp.sqrt(_jnp.mean(_jnp.square(w)) + 1e-30)
        else:
            s = MOMENT_SCALE[name]
        km, kv = _jax.random.split(_jax.random.fold_in(key, i + 1))
        out[name] = w
        out["m_" + name] = s * _jax.random.normal(km, w.shape, _jnp.float32)
        out["v_" + name] = (s * s) * _jax.random.uniform(kv, w.shape, _jnp.float32, 0.5, 1.5)
    if N_MICROBATCH > 1:
        for name, axis in PER_EXAMPLE_BATCH_AXIS.items():
            out[name] = _to_microbatches(out[name], axis)
    return {'x': out['x'], 'norm_pre_mix': out['norm_pre_mix'], 'norm_post_mix': out['norm_post_mix'], 'norm_pre_ffn': out['norm_pre_ffn'], 'norm_post_ffn': out['norm_post_ffn'], 'ab_w_in': out['ab_w_in'], 'sgu_ln_g': out['sgu_ln_g'], 'sgu_ln_b': out['sgu_ln_b'], 'sgu_w': out['sgu_w'], 'sgu_b': out['sgu_b'], 'ab_w_out': out['ab_w_out'], 'sb_w_in': out['sb_w_in'], 'sb_w_out': out['sb_w_out'], 'ffn_w1': out['ffn_w1'], 'ffn_w2': out['ffn_w2'], 'loss_target': out['loss_target'], 'm_norm_pre_mix': out['m_norm_pre_mix'], 'm_norm_post_mix': out['m_norm_post_mix'], 'm_norm_pre_ffn': out['m_norm_pre_ffn'], 'm_norm_post_ffn': out['m_norm_post_ffn'], 'm_ab_w_in': out['m_ab_w_in'], 'm_sgu_ln_g': out['m_sgu_ln_g'], 'm_sgu_ln_b': out['m_sgu_ln_b'], 'm_sgu_w': out['m_sgu_w'], 'm_sgu_b': out['m_sgu_b'], 'm_ab_w_out': out['m_ab_w_out'], 'm_sb_w_in': out['m_sb_w_in'], 'm_sb_w_out': out['m_sb_w_out'], 'm_ffn_w1': out['m_ffn_w1'], 'm_ffn_w2': out['m_ffn_w2'], 'v_norm_pre_mix': out['v_norm_pre_mix'], 'v_norm_post_mix': out['v_norm_post_mix'], 'v_norm_pre_ffn': out['v_norm_pre_ffn'], 'v_norm_post_ffn': out['v_norm_post_ffn'], 'v_ab_w_in': out['v_ab_w_in'], 'v_sgu_ln_g': out['v_sgu_ln_g'], 'v_sgu_ln_b': out['v_sgu_ln_b'], 'v_sgu_w': out['v_sgu_w'], 'v_sgu_b': out['v_sgu_b'], 'v_ab_w_out': out['v_ab_w_out'], 'v_sb_w_in': out['v_sb_w_in'], 'v_sb_w_out': out['v_sb_w_out'], 'v_ffn_w1': out['v_ffn_w1'], 'v_ffn_w2': out['v_ffn_w2']}


def _loss(weights, diff, rest, loss_target):
    with _jax.named_scope("forward"):
        args = {**rest, TWIN_DIFF_INPUT: diff, **{k: w.astype(_WEIGHT_DTYPES[k]) for k, w in weights.items()}}
        y = _forward(args)
    with _jax.named_scope("loss_head"):
        err = _jnp.square(y.astype(_jnp.float32) - loss_target)
        return 0.5 * _jnp.sum(_jnp.mean(err, axis=-1)) if err.ndim else 0.5 * err


def _adamw(w, g, m, v):
    m = ADAM_B1 * m + (1.0 - ADAM_B1) * g
    v = ADAM_B2 * v + (1.0 - ADAM_B2) * _jnp.square(g)
    m_hat = m / (1.0 - ADAM_B1 ** ADAM_STEP)
    v_hat = v / (1.0 - ADAM_B2 ** ADAM_STEP)
    delta = -ADAM_LR * (m_hat / (_jnp.sqrt(v_hat) + ADAM_EPS) + ADAM_WD * w)
    return delta, m, v


def reference(x, norm_pre_mix, norm_post_mix, norm_pre_ffn, norm_post_ffn, ab_w_in, sgu_ln_g, sgu_ln_b, sgu_w, sgu_b, ab_w_out, sb_w_in, sb_w_out, ffn_w1, ffn_w2, loss_target, m_norm_pre_mix, m_norm_post_mix, m_norm_pre_ffn, m_norm_post_ffn, m_ab_w_in, m_sgu_ln_g, m_sgu_ln_b, m_sgu_w, m_sgu_b, m_ab_w_out, m_sb_w_in, m_sb_w_out, m_ffn_w1, m_ffn_w2, v_norm_pre_mix, v_norm_post_mix, v_norm_pre_ffn, v_norm_post_ffn, v_ab_w_in, v_sgu_ln_g, v_sgu_ln_b, v_sgu_w, v_sgu_b, v_ab_w_out, v_sb_w_in, v_sb_w_out, v_ffn_w1, v_ffn_w2):
    given = dict(x=x, norm_pre_mix=norm_pre_mix, norm_post_mix=norm_post_mix, norm_pre_ffn=norm_pre_ffn, norm_post_ffn=norm_post_ffn, ab_w_in=ab_w_in, sgu_ln_g=sgu_ln_g, sgu_ln_b=sgu_ln_b, sgu_w=sgu_w, sgu_b=sgu_b, ab_w_out=ab_w_out, sb_w_in=sb_w_in, sb_w_out=sb_w_out, ffn_w1=ffn_w1, ffn_w2=ffn_w2, loss_target=loss_target, m_norm_pre_mix=m_norm_pre_mix, m_norm_post_mix=m_norm_post_mix, m_norm_pre_ffn=m_norm_pre_ffn, m_norm_post_ffn=m_norm_post_ffn, m_ab_w_in=m_ab_w_in, m_sgu_ln_g=m_sgu_ln_g, m_sgu_ln_b=m_sgu_ln_b, m_sgu_w=m_sgu_w, m_sgu_b=m_sgu_b, m_ab_w_out=m_ab_w_out, m_sb_w_in=m_sb_w_in, m_sb_w_out=m_sb_w_out, m_ffn_w1=m_ffn_w1, m_ffn_w2=m_ffn_w2, v_norm_pre_mix=v_norm_pre_mix, v_norm_post_mix=v_norm_post_mix, v_norm_pre_ffn=v_norm_pre_ffn, v_norm_post_ffn=v_norm_post_ffn, v_ab_w_in=v_ab_w_in, v_sgu_ln_g=v_sgu_ln_g, v_sgu_ln_b=v_sgu_ln_b, v_sgu_w=v_sgu_w, v_sgu_b=v_sgu_b, v_ab_w_out=v_ab_w_out, v_sb_w_in=v_sb_w_in, v_sb_w_out=v_sb_w_out, v_ffn_w1=v_ffn_w1, v_ffn_w2=v_ffn_w2)
    weights = {n: given[n] for n in TWIN_WEIGHTS}
    shared = {n: given[n] for n in SHARED_INPUTS}
    per_example = {n: given[n] for n in ['x']}
    grad_fn = _jax.value_and_grad(_loss, argnums=(0, 1))

    def one_microbatch(ex, loss_target):
        ex = dict(ex)
        diff = ex.pop(TWIN_DIFF_INPUT)
        return grad_fn(weights, diff, {**shared, **ex}, loss_target)

    if N_MICROBATCH == 1:
        loss, (grad_w, grad_x) = one_microbatch(per_example, given["loss_target"])
    else:
        def body(carry, xs):
            loss_sum, grad_sum = carry
            l_k, (gw_k, gx_k) = one_microbatch(xs[0], xs[1])
            with _jax.named_scope("update"):
                return (loss_sum + l_k, _jax.tree.map(_jnp.add, grad_sum, gw_k)), gx_k

        init = (_jnp.zeros((), _jnp.float32), _jax.tree.map(_jnp.zeros_like, weights))
        (loss, grad_w), grad_x = _jax.lax.scan(body, init, (per_example, given["loss_target"]))
    with _jax.named_scope("update"):
        delta_w, new_m, new_v = {}, {}, {}
        for n in TWIN_WEIGHTS:
            delta_w[n], new_m[n], new_v[n] = _adamw(weights[n], grad_w[n], given["m_" + n], given["v_" + n])
    return (loss, grad_x, *[grad_w[n] for n in TWIN_WEIGHTS], *[delta_w[n] for n in TWIN_WEIGHTS],
            *[new_m[n] for n in TWIN_WEIGHTS], *[new_v[n] for n in TWIN_WEIGHTS])
```

```python
import functools

import jax
import jax.numpy as jnp
from jax import lax
from jax.experimental import pallas as pl
from jax.experimental.pallas import tpu as pltpu

F32 = jnp.float32
BF16 = jnp.bfloat16
MESH = pl.DeviceIdType.MESH

HEAD_DIM = 128
CHUNK = 128
DILATIONS = (1, 4, 16)
SB_BLOCK = 256
RMS_EPS = 1e-6
LN_EPS = 1e-5
ADAM_LR, ADAM_B1, ADAM_B2, ADAM_EPS, ADAM_WD, ADAM_STEP = 0.001, 0.9, 0.999, 1e-08, 0.01, 10
NEG = -1e30
V7X_VMEM_LIMIT = 48 * 1024 * 1024
ANY = pl.BlockSpec(memory_space=pl.ANY)


def _params(*sem):
    return pltpu.CompilerParams(dimension_semantics=sem if sem else None, vmem_limit_bytes=V7X_VMEM_LIMIT)


def _tile(n, pref):
    if n <= pref:
        return n
    t = pref
    while n % t:
        t -= 128
    return t


def _dot(a, b, dims):
    return lax.dot_general(a, b, (dims, ((), ())), preferred_element_type=F32)


NN = ((1,), (0,))
NT = ((1,), (1,))
TN = ((0,), (0,))


def _matmul(a, b, mode, out_dtype, name, a_square=False, relu_out=False, mul2=None):
    if mode == "nn":
        (m, k), n = a.shape, b.shape[1]
    elif mode == "nt":
        (m, k), n = a.shape, b.shape[0]
    else:
        (k, m), n = a.shape, b.shape[1]
    tm, tn, tk = _tile(m, 1024), _tile(n, 1024), _tile(k, 512)
    nk = k // tk
    dims = {"nn": NN, "nt": NT, "tn": TN}[mode]
    a_spec = pl.BlockSpec((tk, tm), lambda i, j, kk: (kk, i)) if mode == "tn" else pl.BlockSpec((tm, tk), lambda i, j, kk: (i, kk))
    b_spec = pl.BlockSpec((tn, tk), lambda i, j, kk: (j, kk)) if mode == "nt" else pl.BlockSpec((tk, tn), lambda i, j, kk: (kk, j))
    o_spec = pl.BlockSpec((tm, tn), lambda i, j, kk: (i, j))

    def body(a_ref, b_ref, *rest):
        if mul2 is None:
            o_ref, acc_ref = rest
        else:
            m_ref, o_ref, acc_ref = rest
        kk = pl.program_id(2)

        @pl.when(kk == 0)
        def _():
            acc_ref[...] = jnp.zeros_like(acc_ref)

        av = a_ref[...]
        if a_square:
            av = av * av
        acc_ref[...] += _dot(av, b_ref[...], dims)

        @pl.when(kk == nk - 1)
        def _():
            r = acc_ref[...]
            if relu_out:
                r = jnp.maximum(r, 0.0)
            if mul2 is not None:
                r = r * (2.0 * m_ref[...].astype(F32))
            o_ref[...] = r.astype(out_dtype)

    args = (a, b) if mul2 is None else (a, b, mul2)
    specs = [a_spec, b_spec] + ([] if mul2 is None else [o_spec])
    return pl.pallas_call(
        body, name=name, grid=(m // tm, n // tn, nk), in_specs=specs, out_specs=o_spec,
        out_shape=jax.ShapeDtypeStruct((m, n), out_dtype), scratch_shapes=[pltpu.VMEM((tm, tn), F32)],
        compiler_params=_params("parallel", "parallel", "arbitrary"),
    )(*args)


NORM_ROWS = 256


def _rms(x, g):
    rstd = lax.rsqrt(jnp.mean(x * x, axis=-1, keepdims=True) + RMS_EPS)
    n = x * rstd
    return n * g, n, rstd


def _rms_bwd(n, rstd, g, dout):
    dn = dout * g
    return rstd * (dn - n * jnp.mean(dn * n, axis=-1, keepdims=True))


def _row_spec(d):
    return pl.BlockSpec((NORM_ROWS, d), lambda i: (i, 0))


def _vec_spec(d):
    return pl.BlockSpec((1, d), lambda i: (0, 0))


def _accumulate(ref, val):
    @pl.when(pl.program_id(0) == 0)
    def _():
        ref[...] = jnp.zeros_like(ref)

    ref[...] += val


def _rms_fwd(x, g, name):
    t, d = x.shape

    def body(x_ref, g_ref, h_ref):
        h_ref[...] = _rms(x_ref[...], g_ref[...])[0].astype(BF16)

    return pl.pallas_call(
        body, name=name, grid=(t // NORM_ROWS,), in_specs=[_row_spec(d), _vec_spec(d)], out_specs=_row_spec(d),
        out_shape=jax.ShapeDtypeStruct((t, d), BF16), compiler_params=_params("parallel"),
    )(x, g)


def _post_pre_fwd(y, g_post, x, g_pre, name):
    t, d = x.shape

    def body(y_ref, gp_ref, x_ref, gn_ref, xn_ref, h_ref):
        xn = x_ref[...] + _rms(y_ref[...], gp_ref[...])[0]
        xn_ref[...] = xn
        h_ref[...] = _rms(xn, gn_ref[...])[0].astype(BF16)

    return pl.pallas_call(
        body, name=name, grid=(t // NORM_ROWS,),
        in_specs=[_row_spec(d), _vec_spec(d), _row_spec(d), _vec_spec(d)], out_specs=[_row_spec(d), _row_spec(d)],
        out_shape=[jax.ShapeDtypeStruct((t, d), F32), jax.ShapeDtypeStruct((t, d), BF16)], compiler_params=_params("parallel"),
    )(y, g_post, x, g_pre)


def _final_fwd_bwd(y, g_post, x, target, name):
    t, d = x.shape

    def body(y_ref, g_ref, x_ref, t_ref, loss_ref, dx_ref, dy_ref, dg_ref):
        g = g_ref[...]
        out, n, rstd = _rms(y_ref[...], g)
        e = x_ref[...] + out - t_ref[...]
        _accumulate(loss_ref, jnp.full(loss_ref.shape, 0.5 / d, F32) * jnp.sum(e * e))
        dx = e * (1.0 / d)
        dx_ref[...] = dx
        dy_ref[...] = _rms_bwd(n, rstd, g, dx).astype(BF16)
        _accumulate(dg_ref, jnp.sum(dx * n, axis=0, keepdims=True))

    return pl.pallas_call(
        body, name=name, grid=(t // NORM_ROWS,),
        in_specs=[_row_spec(d), _vec_spec(d), _row_spec(d), _row_spec(d)],
        out_specs=[pl.BlockSpec((8, 128), lambda i: (0, 0)), _row_spec(d), _row_spec(d), _vec_spec(d)],
        out_shape=[jax.ShapeDtypeStruct((8, 128), F32), jax.ShapeDtypeStruct((t, d), F32),
                   jax.ShapeDtypeStruct((t, d), BF16), jax.ShapeDtypeStruct((1, d), F32)],
        compiler_params=_params("arbitrary"),
    )(y, g_post, x, target)


def _pre_post_bwd(x, g_pre, dh, dx_in, y, g_post, name):
    t, d = x.shape
    both = y is not None

    def body(x_ref, gp_ref, dh_ref, dxi_ref, *rest):
        if both:
            y_ref, gq_ref, dx_ref, dy_ref, dgp_ref, dgq_ref = rest
        else:
            dx_ref, dgp_ref = rest
        gp = gp_ref[...]
        _, n, rstd = _rms(x_ref[...], gp)
        dh_v = dh_ref[...]
        dx = dxi_ref[...] + _rms_bwd(n, rstd, gp, dh_v)
        dx_ref[...] = dx
        _accumulate(dgp_ref, jnp.sum(dh_v * n, axis=0, keepdims=True))
        if both:
            gq = gq_ref[...]
            _, ny, rstdy = _rms(y_ref[...], gq)
            dy_ref[...] = _rms_bwd(ny, rstdy, gq, dx).astype(BF16)
            _accumulate(dgq_ref, jnp.sum(dx * ny, axis=0, keepdims=True))

    in_specs = [_row_spec(d), _vec_spec(d), _row_spec(d), _row_spec(d)]
    args = [x, g_pre, dh, dx_in]
    if both:
        in_specs += [_row_spec(d), _vec_spec(d)]
        args += [y, g_post]
        out_specs = [_row_spec(d), _row_spec(d), _vec_spec(d), _vec_spec(d)]
        out_shape = [jax.ShapeDtypeStruct((t, d), F32), jax.ShapeDtypeStruct((t, d), BF16),
                     jax.ShapeDtypeStruct((1, d), F32), jax.ShapeDtypeStruct((1, d), F32)]
    else:
        out_specs = [_row_spec(d), _vec_spec(d)]
        out_shape = [jax.ShapeDtypeStruct((t, d), F32), jax.ShapeDtypeStruct((1, d), F32)]
    return pl.pallas_call(
        body, name=name, grid=(t // NORM_ROWS,), in_specs=in_specs, out_specs=out_specs, out_shape=out_shape,
        compiler_params=_params("arbitrary"),
    )(*args)


def _gelu(x):
    return 0.5 * x * (1.0 + lax.erf(x * 0.7071067811865476))


def _gelu_grad(x):
    return 0.5 * (1.0 + lax.erf(x * 0.7071067811865476)) + x * jnp.exp(-0.5 * x * x) * 0.3989422804014327


def _layernorm(v, g, b):
    mu = jnp.mean(v, axis=-1, keepdims=True)
    vc = v - mu
    rs = lax.rsqrt(jnp.mean(vc * vc, axis=-1, keepdims=True) + LN_EPS)
    vhat = vc * rs
    return vhat * g + b, vhat, rs


def _tril_mask():
    return lax.broadcasted_iota(jnp.int32, (CHUNK, CHUNK), 0) >= lax.broadcasted_iota(jnp.int32, (CHUNK, CHUNK), 1)


def _sgu_fwd(z, ln_g, ln_b, w16, bias_b, name):
    t = z.shape[0]
    groups = w16.shape[0]
    a = groups * CHUNK

    def body(u_ref, v_ref, g_ref, b_ref, w_ref, bb_ref, o_ref):
        u = _gelu(u_ref[...].astype(F32))
        vn = _layernorm(_gelu(v_ref[...].astype(F32)), g_ref[...], b_ref[...])[0].astype(BF16)
        tril = _tril_mask()
        for g in range(groups):
            sl = slice(g * CHUNK, (g + 1) * CHUNK)
            w = jnp.where(tril, w_ref[g], jnp.zeros((), BF16))
            mixed = _dot(w, vn[:, sl], NN) + bb_ref[g]
            o_ref[:, sl] = (u[:, sl] * mixed).astype(BF16)

    full3 = pl.BlockSpec((groups, CHUNK, CHUNK), lambda c: (0, 0, 0))
    return pl.pallas_call(
        body, name=name, grid=(t // CHUNK,),
        in_specs=[pl.BlockSpec((CHUNK, a), lambda c: (c, 0)), pl.BlockSpec((CHUNK, a), lambda c: (c, 1)),
                  _vec_spec(a), _vec_spec(a), full3, full3],
        out_specs=pl.BlockSpec((CHUNK, a), lambda c: (c, 0)),
        out_shape=jax.ShapeDtypeStruct((t, a), BF16), compiler_params=_params("parallel"),
    )(z, z, ln_g, ln_b, w16, bias_b)


def _sgu_bwd(z, dab, ln_g, ln_b, w16, bias_b, name):
    t = z.shape[0]
    groups = w16.shape[0]
    a = groups * CHUNK

    def body(u_ref, v_ref, da_ref, g_ref, b_ref, w_ref, bb_ref, duv_ref, dg_ref, db_ref, dw_ref, dbs_ref, dvn_ref):
        up = u_ref[...].astype(F32)
        vp = v_ref[...].astype(F32)
        u = _gelu(up)
        ln_gain = g_ref[...]
        vn32, vhat, rs = _layernorm(_gelu(vp), ln_gain, b_ref[...])
        vn = vn32.astype(BF16)
        da = da_ref[...].astype(F32)
        tril = _tril_mask()
        ones = jnp.ones((8, CHUNK), F32)

        @pl.when(pl.program_id(0) == 0)
        def _():
            dw_ref[...] = jnp.zeros_like(dw_ref)
            dbs_ref[...] = jnp.zeros_like(dbs_ref)

        for g in range(groups):
            sl = slice(g * CHUNK, (g + 1) * CHUNK)
            w = jnp.where(tril, w_ref[g], jnp.zeros((), BF16))
            mixed = _dot(w, vn[:, sl], NN) + bb_ref[g]
            dmix = da[:, sl] * u[:, sl]
            dmix16 = dmix.astype(BF16)
            duv_ref[:, sl] = (da[:, sl] * mixed * _gelu_grad(up[:, sl])).astype(BF16)
            dvn_ref[:, sl] = _dot(w, dmix16, TN)
            dw_ref[g] += jnp.where(tril, _dot(dmix16, vn[:, sl], NT), 0.0)
            dbs_ref[g:g + 1, :] += lax.dot_general(ones, dmix, (NT, ((), ())), precision=lax.Precision.HIGHEST,
                                                   preferred_element_type=F32)[0:1]
        dvn = dvn_ref[...]
        dvhat = dvn * ln_gain
        dva = rs * (dvhat - jnp.mean(dvhat, axis=-1, keepdims=True) - vhat * jnp.mean(dvhat * vhat, axis=-1, keepdims=True))
        duv_ref[:, a:] = (dva * _gelu_grad(vp)).astype(BF16)
        _accumulate(dg_ref, jnp.sum(dvn * vhat, axis=0, keepdims=True))
        _accumulate(db_ref, jnp.sum(dvn, axis=0, keepdims=True))

    full3 = pl.BlockSpec((groups, CHUNK, CHUNK), lambda c: (0, 0, 0))
    return pl.pallas_call(
        body, name=name, grid=(t // CHUNK,),
        in_specs=[pl.BlockSpec((CHUNK, a), lambda c: (c, 0)), pl.BlockSpec((CHUNK, a), lambda c: (c, 1)),
                  pl.BlockSpec((CHUNK, a), lambda c: (c, 0)), _vec_spec(a), _vec_spec(a), full3, full3],
        out_specs=[pl.BlockSpec((CHUNK, 2 * a), lambda c: (c, 0)), _vec_spec(a), _vec_spec(a), full3,
                   pl.BlockSpec((groups, CHUNK), lambda c: (0, 0))],
        out_shape=[jax.ShapeDtypeStruct((t, 2 * a), BF16), jax.ShapeDtypeStruct((1, a), F32), jax.ShapeDtypeStruct((1, a), F32),
                   jax.ShapeDtypeStruct((groups, CHUNK, CHUNK), F32), jax.ShapeDtypeStruct((groups, CHUNK), F32)],
        scratch_shapes=[pltpu.VMEM((CHUNK, a), F32)], compiler_params=_params("arbitrary"),
    )(z, z, dab, ln_g, ln_b, w16, bias_b)


def _dil_masks(d):
    qi = lax.broadcasted_iota(jnp.int32, (CHUNK, CHUNK), 0)
    kj = lax.broadcasted_iota(jnp.int32, (CHUNK, CHUNK), 1)
    dist_c = qi - kj
    return dist_c >= 0, dist_c <= 0, (dist_c * d).astype(F32), ((dist_c + CHUNK) * d).astype(F32)


def _alibi_slope(h, heads):
    return 2.0 ** (-8.0 * (h + 1) / heads)


def _dil_fwd(z, d, name):
    t = z.shape[0]
    w = z.shape[1] // 5
    heads = w // HEAD_DIM
    nb = t // d // CHUNK
    scale = HEAD_DIM ** -0.5

    def body(q_ref, kp_ref, kc_ref, vp_ref, vc_ref, o_ref, l_ref):
        ok_c, ok_p0, bias_c, bias_p = _dil_masks(d)
        ok_p = ok_p0 & (pl.program_id(1) > 0)
        for h in range(heads):
            sl = slice(h * HEAD_DIM, (h + 1) * HEAD_DIM)
            slope = _alibi_slope(h, heads)
            q = q_ref[:, sl]
            s_c = jnp.where(ok_c, _dot(q, kc_ref[:, sl], NT) * scale - slope * bias_c, NEG)
            s_p = jnp.where(ok_p, _dot(q, kp_ref[:, sl], NT) * scale - slope * bias_p, NEG)
            m = jnp.maximum(jnp.max(s_c, axis=1, keepdims=True), jnp.max(s_p, axis=1, keepdims=True))
            p_c = jnp.exp(s_c - m)
            p_p = jnp.exp(s_p - m)
            den = jnp.sum(p_c, axis=1, keepdims=True) + jnp.sum(p_p, axis=1, keepdims=True)
            o = _dot(p_c.astype(BF16), vc_ref[:, sl], NN) + _dot(p_p.astype(BF16), vp_ref[:, sl], NN)
            o_ref[:, sl] = o / den
            l_ref[:, sl] = jnp.broadcast_to(m + jnp.log(den), (CHUNK, HEAD_DIM))

    def zspec(col, prev):
        if prev:
            return pl.BlockSpec((CHUNK, w), lambda r, n: (jnp.maximum(n - 1, 0), r * 5 + col))
        return pl.BlockSpec((CHUNK, w), lambda r, n: (n, r * 5 + col))

    ospec = pl.BlockSpec((CHUNK, w), lambda r, n: (n, r))
    zv = z.reshape(t // d, d * 5 * w)
    o, lse = pl.pallas_call(
        body, name=name, grid=(d, nb),
        in_specs=[zspec(2, False), zspec(3, True), zspec(3, False), zspec(4, True), zspec(4, False)],
        out_specs=[ospec, ospec],
        out_shape=[jax.ShapeDtypeStruct((t // d, d * w), F32), jax.ShapeDtypeStruct((t // d, d * w), F32)],
        compiler_params=_params("parallel", "parallel"),
    )(zv, zv, zv, zv, zv)
    return o.reshape(t, w), lse.reshape(t, w)


def _dil_merge(a_out, outs, lses, name):
    t, a = a_out.shape
    w = outs[0].shape[1]
    nbr = len(outs)

    def body(a_ref, *rest):
        o_refs, l_refs, (ab_ref, lt_ref) = rest[:nbr], rest[nbr:2 * nbr], rest[2 * nbr:]
        ls = [r[...] for r in l_refs]
        m = functools.reduce(jnp.maximum, ls)
        ws = [jnp.exp(l - m) for l in ls]
        tot = functools.reduce(jnp.add, ws)
        mix = functools.reduce(jnp.add, [wt * r[...] for wt, r in zip(ws, o_refs)]) / tot
        ab_ref[:, :a] = a_ref[...]
        ab_ref[:, a:] = mix.astype(BF16)
        lt_ref[...] = m + jnp.log(tot)

    return pl.pallas_call(
        body, name=name, grid=(t // NORM_ROWS,),
        in_specs=[_row_spec(a)] + [_row_spec(w)] * (2 * nbr), out_specs=[_row_spec(a + w), _row_spec(w)],
        out_shape=[jax.ShapeDtypeStruct((t, a + w), BF16), jax.ShapeDtypeStruct((t, w), F32)],
        compiler_params=_params("parallel"),
    )(a_out, *outs, *lses)


def _dil_bwd(z, ab, dab, ltot, d, name):
    t = z.shape[0]
    w = z.shape[1] // 5
    heads = w // HEAD_DIM
    nb = t // d // CHUNK
    scale = HEAD_DIM ** -0.5

    def body(q_ref, qn_ref, kp_ref, kc_ref, vp_ref, vc_ref, o_ref, on_ref, do_ref, don_ref, l_ref, ln_ref,
             dq_ref, dk_ref, dv_ref):
        n = pl.program_id(1)
        ok_c, ok_p0, bias_c, bias_p = _dil_masks(d)
        ok_p = ok_p0 & (n > 0)
        ok_n = ok_p0 & (n < nb - 1)
        for h in range(heads):
            sl = slice(h * HEAD_DIM, (h + 1) * HEAD_DIM)
            slope = _alibi_slope(h, heads)
            q, qn, kp, kc, vp, vc = q_ref[:, sl], qn_ref[:, sl], kp_ref[:, sl], kc_ref[:, sl], vp_ref[:, sl], vc_ref[:, sl]
            do, don = do_ref[:, sl], don_ref[:, sl]
            delta = jnp.sum(do.astype(F32) * o_ref[:, sl].astype(F32), axis=1, keepdims=True)
            delta_n = jnp.sum(don.astype(F32) * on_ref[:, sl].astype(F32), axis=1, keepdims=True)
            lt, lt_n = l_ref[:, sl], ln_ref[:, sl]
            p_c = jnp.exp(jnp.where(ok_c, _dot(q, kc, NT) * scale - slope * bias_c, NEG) - lt)
            p_p = jnp.exp(jnp.where(ok_p, _dot(q, kp, NT) * scale - slope * bias_p, NEG) - lt)
            p_n = jnp.exp(jnp.where(ok_n, _dot(qn, kc, NT) * scale - slope * bias_p, NEG) - lt_n)
            ds_c = (p_c * (_dot(do, vc, NT) - delta)).astype(BF16)
            ds_p = (p_p * (_dot(do, vp, NT) - delta)).astype(BF16)
            ds_n = (p_n * (_dot(don, vc, NT) - delta_n)).astype(BF16)
            dq_ref[:, sl] = (_dot(ds_c, kc, NN) + _dot(ds_p, kp, NN)) * scale
            dk_ref[:, sl] = (_dot(ds_c, q, TN) + _dot(ds_n, qn, TN)) * scale
            dv_ref[:, sl] = _dot(p_c.astype(BF16), do, TN) + _dot(p_n.astype(BF16), don, TN)

    def spec(mult, col, shift):
        if shift < 0:
            return pl.BlockSpec((CHUNK, w), lambda r, n: (jnp.maximum(n - 1, 0), r * mult + col))
        if shift > 0:
            return pl.BlockSpec((CHUNK, w), lambda r, n: (jnp.minimum(n + 1, nb - 1), r * mult + col))
        return pl.BlockSpec((CHUNK, w), lambda r, n: (n, r * mult + col))

    zv = z.reshape(t // d, d * 5 * w)
    abv = ab.reshape(t // d, d * 2 * w)
    dabv = dab.reshape(t // d, d * 2 * w)
    lv = ltot.reshape(t // d, d * w)
    ospec = spec(1, 0, 0)
    outs = pl.pallas_call(
        body, name=name, grid=(d, nb),
        in_specs=[spec(5, 2, 0), spec(5, 2, 1), spec(5, 3, -1), spec(5, 3, 0), spec(5, 4, -1), spec(5, 4, 0),
                  spec(2, 1, 0), spec(2, 1, 1), spec(2, 1, 0), spec(2, 1, 1), spec(1, 0, 0), spec(1, 0, 1)],
        out_specs=[ospec, ospec, ospec],
        out_shape=[jax.ShapeDtypeStruct((t // d, d * w), F32)] * 3,
        compiler_params=_params("parallel", "parallel"),
    )(zv, zv, zv, zv, zv, zv, abv, abv, dabv, dabv, lv, lv)
    return [o.reshape(t, w) for o in outs]


def _dz_assemble(duv, parts, name):
    t, a2 = duv.shape
    w = parts[0][0].shape[1]
    nbr = len(parts)

    def body(duv_ref, *rest):
        refs, dz_ref = rest[:-1], rest[-1]
        dz_ref[:, :a2] = duv_ref[...]
        for i in range(3):
            tot = functools.reduce(jnp.add, [refs[b * 3 + i][...] for b in range(nbr)])
            dz_ref[:, a2 + i * w:a2 + (i + 1) * w] = tot.astype(BF16)

    flat = [p for branch in parts for p in branch]
    return pl.pallas_call(
        body, name=name, grid=(t // NORM_ROWS,), in_specs=[_row_spec(a2)] + [_row_spec(w)] * len(flat),
        out_specs=_row_spec(a2 + 3 * w), out_shape=jax.ShapeDtypeStruct((t, a2 + 3 * w), BF16),
        compiler_params=_params("parallel"),
    )(duv, *flat)


def _split_dot(x, m16):
    hi = x.astype(BF16)
    lo = (x - hi.astype(F32)).astype(BF16)
    return _dot(hi, m16, NN) + _dot(lo, m16, NN)


def _sb_tile(q, kj, i, j):
    blk = q.shape[0]
    zt = _dot(q, kj, NT) * (HEAD_DIM ** -0.5)
    e = jnp.exp(-jnp.abs(zt))
    sp = jnp.maximum(zt, 0.0) + jnp.log1p(e)
    rows = lax.broadcasted_iota(jnp.int32, (blk, blk), 0)
    cols = lax.broadcasted_iota(jnp.int32, (blk, blk), 1)
    causal = (j * blk + cols) < (i * blk + rows)
    l = jnp.where(causal, -sp, 0.0)
    r = 1.0 / (1.0 + e)
    beta = jnp.where(zt >= 0.0, r, e * r)
    return beta, l, causal


def _sb_fwd(zc, name):
    t = zc.shape[0]
    c = zc.shape[1] // 3
    heads = c // HEAD_DIM
    blk = min(SB_BLOCK, t)

    def body(q_ref, k_ref, v_ref, o_ref, ct_ref):
        i = pl.program_id(1)
        q = q_ref[...]
        rows = lax.broadcasted_iota(jnp.int32, (blk, blk), 0)
        cols = lax.broadcasted_iota(jnp.int32, (blk, blk), 1)
        m_right = (rows > cols).astype(BF16)

        def step(s, carry):
            acc, c_run = carry
            j = i - s
            off = pl.multiple_of(j * blk, blk)
            beta, l, causal = _sb_tile(q, k_ref[pl.ds(off, blk), :], i, j)
            a = jnp.where(causal, beta * jnp.exp(c_run + _split_dot(l, m_right)), 0.0)
            acc = acc + _dot(a.astype(BF16), v_ref[pl.ds(off, blk), :], NN)
            return acc, c_run + jnp.sum(l, axis=1, keepdims=True)

        acc, c_tot = lax.fori_loop(0, i + 1, step, (jnp.zeros((blk, HEAD_DIM), F32), jnp.zeros((blk, 1), F32)))
        o_ref[...] = acc.astype(BF16)
        ct_ref[...] = jnp.broadcast_to(c_tot, (blk, HEAD_DIM))

    qspec = pl.BlockSpec((blk, HEAD_DIM), lambda h, i: (i, h))
    return pl.pallas_call(
        body, name=name, grid=(heads, t // blk),
        in_specs=[qspec, pl.BlockSpec((t, HEAD_DIM), lambda h, i: (0, heads + h)),
                  pl.BlockSpec((t, HEAD_DIM), lambda h, i: (0, 2 * heads + h))],
        out_specs=[qspec, qspec],
        out_shape=[jax.ShapeDtypeStruct((t, c), BF16), jax.ShapeDtypeStruct((t, c), F32)],
        compiler_params=_params("parallel", "parallel"),
    )(zc, zc, zc)


def _sb_bwd(zc, ctot, do, name):
    t = zc.shape[0]
    c = zc.shape[1] // 3
    heads = c // HEAD_DIM
    blk = min(SB_BLOCK, t)
    scale = HEAD_DIM ** -0.5

    def body(q_ref, k_ref, v_ref, ct_ref, do_ref, dq_ref, dk_ref, dv_ref):
        i = pl.program_id(1)

        @pl.when(i == 0)
        def _():
            dk_ref[...] = jnp.zeros_like(dk_ref)
            dv_ref[...] = jnp.zeros_like(dv_ref)

        q = q_ref[...]
        dov = do_ref[...]
        c_tot = ct_ref[:, 0:1]
        rows = lax.broadcasted_iota(jnp.int32, (blk, blk), 0)
        cols = lax.broadcasted_iota(jnp.int32, (blk, blk), 1)
        m_upto = (rows <= cols).astype(BF16)
        m_left = (rows < cols).astype(BF16)

        def step(j, carry):
            dq, l_run, w_run = carry
            off = pl.multiple_of(j * blk, blk)
            kj = k_ref[pl.ds(off, blk), :]
            vj = v_ref[pl.ds(off, blk), :]
            beta, l, causal = _sb_tile(q, kj, i, j)
            a = jnp.where(causal, beta * jnp.exp(c_tot - l_run - _split_dot(l, m_upto)), 0.0)
            wgt = a * _dot(dov, vj, NT)
            before = w_run + _split_dot(wgt, m_left)
            dz = jnp.where(causal, wgt * (1.0 - beta) - beta * before, 0.0) * scale
            dz16 = dz.astype(BF16)
            dk_ref[pl.ds(off, blk), :] += _dot(dz16, q, TN)
            dv_ref[pl.ds(off, blk), :] += _dot(a.astype(BF16), dov, TN)
            return (dq + _dot(dz16, kj, NN), l_run + jnp.sum(l, axis=1, keepdims=True),
                    w_run + jnp.sum(wgt, axis=1, keepdims=True))

        zero = jnp.zeros((blk, 1), F32)
        dq, _, _ = lax.fori_loop(0, i + 1, step, (jnp.zeros((blk, HEAD_DIM), F32), zero, zero))
        dq_ref[...] = dq

    qspec = pl.BlockSpec((blk, HEAD_DIM), lambda h, i: (i, h))
    full = pl.BlockSpec((t, HEAD_DIM), lambda h, i: (0, h))
    return pl.pallas_call(
        body, name=name, grid=(heads, t // blk),
        in_specs=[qspec, pl.BlockSpec((t, HEAD_DIM), lambda h, i: (0, heads + h)),
                  pl.BlockSpec((t, HEAD_DIM), lambda h, i: (0, 2 * heads + h)), qspec, qspec],
        out_specs=[qspec, full, full], out_shape=[jax.ShapeDtypeStruct((t, c), F32)] * 3,
        compiler_params=_params("arbitrary", "arbitrary"),
    )(zc, zc, zc, ctot, do)


def _ffn_fwd(h, w1, w2, tag):
    r = _matmul(h, w1, "nn", BF16, f"ffn_up_{tag}", relu_out=True)
    return r, _matmul(r, w2, "nn", F32, f"ffn_down_{tag}", a_square=True)


def _ffn_bwd(dy, h, r, w1, w2, tag):
    da = _matmul(dy, w2, "nt", BF16, f"ffn_da_{tag}", mul2=r)
    dw2 = _matmul(r, dy, "tn", F32, f"ffn_dw2_{tag}", a_square=True)
    dh = _matmul(da, w1, "nt", F32, f"ffn_dh_{tag}")
    dw1 = _matmul(h, da, "tn", F32, f"ffn_dw1_{tag}")
    return dh, dw1, dw2


def _local_step(x, target, norms, sgu, big):
    g = {k: [v[l:l + 1] for l in range(2)] for k, v in norms.items()}
    ln_g, ln_b, sgu_w, sgu_b = sgu
    groups = sgu_w.shape[0]
    w16 = sgu_w.astype(BF16)
    bias_b = jnp.broadcast_to(sgu_b[:, :, None], (groups, CHUNK, CHUNK))

    h1_0 = _rms_fwd(x, g["pre_mix"][0], "rms_in")
    z0 = _matmul(h1_0, big["ab_w_in"], "nn", BF16, "ab_in")
    a_out = _sgu_fwd(z0, ln_g, ln_b, w16, bias_b, "sgu_fwd")
    branches = [_dil_fwd(z0, d, f"dil_fwd_{d}") for d in DILATIONS]
    ab, ltot = _dil_merge(a_out, [b[0] for b in branches], [b[1] for b in branches], "dil_merge")
    y_0 = _matmul(ab, big["ab_w_out"], "nn", F32, "ab_out")
    x1, h2_0 = _post_pre_fwd(y_0, g["post_mix"][0], x, g["pre_ffn"][0], "norm_mix0")
    r_0, y2_0 = _ffn_fwd(h2_0, big["ffn_w1"][0], big["ffn_w2"][0], "0")
    x2, h1_1 = _post_pre_fwd(y2_0, g["post_ffn"][0], x1, g["pre_mix"][1], "norm_ffn0")
    zc = _matmul(h1_1, big["sb_w_in"], "nn", BF16, "sb_in")
    o_sb, ct_sb = _sb_fwd(zc, "sb_fwd")
    y_1 = _matmul(o_sb, big["sb_w_out"], "nn", F32, "sb_out")
    x3, h2_1 = _post_pre_fwd(y_1, g["post_mix"][1], x2, g["pre_ffn"][1], "norm_mix1")
    r_1, y2_1 = _ffn_fwd(h2_1, big["ffn_w1"][1], big["ffn_w2"][1], "1")
    loss, dx4, dy2_1, dg_post_ffn1 = _final_fwd_bwd(y2_1, g["post_ffn"][1], x3, target, "loss")

    dh2, dw1_1, dw2_1 = _ffn_bwd(dy2_1, h2_1, r_1, big["ffn_w1"][1], big["ffn_w2"][1], "1")
    dx3, dy_1, dg_pre_ffn1, dg_post_mix1 = _pre_post_bwd(x3, g["pre_ffn"][1], dh2, dx4, y_1, g["post_mix"][1], "norm_bwd_mix1")
    do_sb = _matmul(dy_1, big["sb_w_out"], "nt", BF16, "sb_out_dx")
    dw_sb_out = _matmul(o_sb, dy_1, "tn", F32, "sb_out_dw")
    dzc = jnp.concatenate(_sb_bwd(zc, ct_sb, do_sb, "sb_bwd"), axis=1).astype(BF16)
    dh1 = _matmul(dzc, big["sb_w_in"], "nt", F32, "sb_in_dx")
    dw_sb_in = _matmul(h1_1, dzc, "tn", F32, "sb_in_dw")
    dx2, dy2_0, dg_pre_mix1, dg_post_ffn0 = _pre_post_bwd(x2, g["pre_mix"][1], dh1, dx3, y2_0, g["post_ffn"][0], "norm_bwd_ffn0")
    dh2, dw1_0, dw2_0 = _ffn_bwd(dy2_0, h2_0, r_0, big["ffn_w1"][0], big["ffn_w2"][0], "0")
    dx1, dy_0, dg_pre_ffn0, dg_post_mix0 = _pre_post_bwd(x1, g["pre_ffn"][0], dh2, dx2, y_0, g["post_mix"][0], "norm_bwd_mix0")
    dab = _matmul(dy_0, big["ab_w_out"], "nt", BF16, "ab_out_dx")
    dw_ab_out = _matmul(ab, dy_0, "tn", F32, "ab_out_dw")
    duv, d_ln_g, d_ln_b, d_sgu_w, d_sgu_b = _sgu_bwd(z0, dab, ln_g, ln_b, w16, bias_b, "sgu_bwd")
    parts = [_dil_bwd(z0, ab, dab, ltot, d, f"dil_bwd_{d}") for d in DILATIONS]
    dz0 = _dz_assemble(duv, parts, "dz_assemble")
    dh1 = _matmul(dz0, big["ab_w_in"], "nt", F32, "ab_in_dx")
    dw_ab_in = _matmul(h1_0, dz0, "tn", F32, "ab_in_dw")
    grad_x, dg_pre_mix0 = _pre_post_bwd(x, g["pre_mix"][0], dh1, dx1, None, None, "norm_bwd_in")

    d_norms = {
        "pre_mix": jnp.concatenate([dg_pre_mix0, dg_pre_mix1]), "post_mix": jnp.concatenate([dg_post_mix0, dg_post_mix1]),
        "pre_ffn": jnp.concatenate([dg_pre_ffn0, dg_pre_ffn1]), "post_ffn": jnp.concatenate([dg_post_ffn0, dg_post_ffn1]),
    }
    d_big = {"ab_w_in": dw_ab_in, "ab_w_out": dw_ab_out, "sb_w_in": dw_sb_in, "sb_w_out": dw_sb_out,
             "ffn_w1": [dw1_0, dw1_1], "ffn_w2": [dw2_0, dw2_1]}
    return loss, grad_x, d_norms, (d_ln_g, d_ln_b, d_sgu_w, d_sgu_b), d_big


def _place():
    x, y, c = lax.axis_index("x"), lax.axis_index("y"), lax.axis_index("c")
    return x, y, c, 2 * x + y


def _other_chip(x, y, j):
    px = 1 - x if j & 2 else x
    py = 1 - y if j & 1 else y
    return px, py, 2 * px + py


def _half_shape(full_shape, kind):
    rows, cols = full_shape
    return (rows // 2, cols // 4) if kind == "col" else (rows // 8, cols)


def _half(ref, kind, s, h):
    rows, cols = ref.shape
    rh, cs = _half_shape((rows, cols), kind)
    if kind == "col":
        return ref.at[pl.ds(h * rh, rh), pl.ds(s * cs, cs)]
    return ref.at[pl.ds((2 * s + h) * rh, rh), :]


def _shard(ref, kind, s):
    rows, cols = ref.shape
    if kind == "col":
        return ref.at[:, pl.ds(s * (cols // 4), cols // 4)]
    return ref.at[pl.ds(s * (rows // 4), rows // 4), :]


def _to_bf16(w, name):
    rows, cols = w.shape
    tr = _tile(rows, 512)

    def body(w_ref, o_ref):
        o_ref[...] = w_ref[...].astype(BF16)

    spec = pl.BlockSpec((tr, cols), lambda i: (i, 0))
    return pl.pallas_call(body, name=name, grid=(rows // tr,), in_specs=[spec], out_specs=spec,
                          out_shape=jax.ShapeDtypeStruct((rows, cols), BF16), compiler_params=_params("parallel"))(w)


def _all_gather(w16, kind, name):
    rows, cols = w16.shape
    full = (rows, cols * 4) if kind == "col" else (rows * 4, cols)

    def body(w_ref, o_ref, send, recv, local_sem):
        x, y, c, mine = _place()
        rh = _half_shape(full, kind)[0]
        local = pltpu.make_async_copy(w_ref, _shard(o_ref, kind, mine), local_sem)
        local.start()

        def copy(k, src, s, h, to):
            return pltpu.make_async_remote_copy(src_ref=src, dst_ref=_half(o_ref, kind, s, h), send_sem=send.at[k],
                                                recv_sem=recv.at[k], device_id=to, device_id_type=MESH)

        out = []
        for j in (1, 2, 3):
            px, py, _ = _other_chip(x, y, j)
            out.append(copy(j - 1, w_ref.at[pl.ds(c * rh, rh), :], mine, c, (px, py, c)))
            out[-1].start()
        for j in (1, 2, 3):
            _, _, peer = _other_chip(x, y, j)
            copy(j - 1, _half(o_ref, kind, peer, c), peer, c, (x, y, c)).wait_recv()
            out.append(copy(2 + j, _half(o_ref, kind, peer, c), peer, c, (x, y, 1 - c)))
            out[-1].start()
        for j in (1, 2, 3):
            _, _, peer = _other_chip(x, y, j)
            copy(2 + j, _half(o_ref, kind, peer, 1 - c), peer, 1 - c, (x, y, c)).wait_recv()
        for cp in out:
            cp.wait_send()
        local.wait()

    return pl.pallas_call(
        body, name=name, in_specs=[ANY], out_specs=ANY, out_shape=jax.ShapeDtypeStruct(full, BF16),
        scratch_shapes=[pltpu.SemaphoreType.DMA((6,)), pltpu.SemaphoreType.DMA((6,)), pltpu.SemaphoreType.DMA(())],
    )(w16)


def _rs_pair(dw, kind, name):
    rh, cs = _half_shape(dw.shape, kind)

    def body(dw_ref, got_ref, send, recv):
        x, y, c, _ = _place()
        copies = [pltpu.make_async_remote_copy(src_ref=_half(dw_ref, kind, s, 1 - c), dst_ref=got_ref.at[s], send_sem=send.at[s],
                                               recv_sem=recv.at[s], device_id=(x, y, 1 - c), device_id_type=MESH) for s in range(4)]
        for cp in copies:
            cp.start()
        for cp in copies:
            cp.wait()

    return pl.pallas_call(
        body, name=name, in_specs=[ANY], out_specs=ANY, out_shape=jax.ShapeDtypeStruct((4, rh, cs), F32),
        scratch_shapes=[pltpu.SemaphoreType.DMA((4,)), pltpu.SemaphoreType.DMA((4,))],
    )(dw)


def _half_index(kind, nblk):
    if kind == "col":
        return lambda shard, half, i: (half * nblk + i, shard)
    return lambda shard, half, i: ((2 * shard + half) * nblk + i, 0)


def _pair_sum(dw, got, place, kind, name):
    rh, cs = _half_shape(dw.shape, kind)
    tr = _tile(rh, 256)
    idx = _half_index(kind, rh // tr)

    def body(p_ref, dw_ref, got_ref, o_ref):
        o_ref[...] = (dw_ref[...] + got_ref[...]).astype(BF16)

    spec3 = pl.BlockSpec((None, tr, cs), lambda s, i, p: (s, i, 0))
    return pl.pallas_call(
        body, name=name, out_shape=jax.ShapeDtypeStruct((4, rh, cs), BF16),
        grid_spec=pltpu.PrefetchScalarGridSpec(
            num_scalar_prefetch=1, grid=(4, rh // tr),
            in_specs=[pl.BlockSpec((tr, cs), lambda s, i, p: idx(s, p[0], i)), spec3], out_specs=spec3),
        compiler_params=_params("parallel", "parallel"),
    )(place, dw, got)


def _rs_chips(part, name):
    _, rh, cs = part.shape

    def body(p_ref, got_ref, send, recv):
        x, y, c, _ = _place()
        copies = []
        for j in (1, 2, 3):
            px, py, peer = _other_chip(x, y, j)
            copies.append(pltpu.make_async_remote_copy(src_ref=p_ref.at[peer], dst_ref=got_ref.at[j - 1], send_sem=send.at[j - 1],
                                                       recv_sem=recv.at[j - 1], device_id=(px, py, c), device_id_type=MESH))
        for cp in copies:
            cp.start()
        for cp in copies:
            cp.wait()

    return pl.pallas_call(
        body, name=name, in_specs=[ANY], out_specs=ANY, out_shape=jax.ShapeDtypeStruct((3, rh, cs), BF16),
        scratch_shapes=[pltpu.SemaphoreType.DMA((3,)), pltpu.SemaphoreType.DMA((3,))],
    )(part)


def _chip_sum(dw, got_pair, got_chips, place, kind, name):
    rh, cs = _half_shape(dw.shape, kind)
    tr = _tile(rh, 256)
    idx = _half_index(kind, rh // tr)

    def body(p_ref, dw_ref, pair_ref, chips_ref, o_ref):
        tot = dw_ref[...] + pair_ref[...]
        for j in range(3):
            tot = tot + chips_ref[j].astype(F32)
        o_ref[...] = tot

    return pl.pallas_call(
        body, name=name, out_shape=jax.ShapeDtypeStruct((rh, cs), F32),
        grid_spec=pltpu.PrefetchScalarGridSpec(
            num_scalar_prefetch=1, grid=(rh // tr,),
            in_specs=[pl.BlockSpec((tr, cs), lambda i, p: idx(p[1], p[0], i)),
                      pl.BlockSpec((None, tr, cs), lambda i, p: (p[1], i, 0)),
                      pl.BlockSpec((3, tr, cs), lambda i, p: (0, i, 0))],
            out_specs=pl.BlockSpec((tr, cs), lambda i, p: (i, 0))),
        compiler_params=_params("parallel"),
    )(place, dw, got_pair, got_chips)


def _rs_join(half, name):
    rh, cs = half.shape

    def body(h_ref, o_ref, send, recv, local_sem):
        x, y, c, _ = _place()
        local = pltpu.make_async_copy(h_ref, o_ref.at[c], local_sem)
        local.start()
        cp = pltpu.make_async_remote_copy(src_ref=h_ref, dst_ref=o_ref.at[c], send_sem=send, recv_sem=recv,
                                          device_id=(x, y, 1 - c), device_id_type=MESH)
        cp.start()
        pltpu.make_async_remote_copy(src_ref=h_ref, dst_ref=o_ref.at[1 - c], send_sem=send, recv_sem=recv,
                                     device_id=(x, y, 1 - c), device_id_type=MESH).wait_recv()
        cp.wait_send()
        local.wait()

    return pl.pallas_call(
        body, name=name, in_specs=[ANY], out_specs=ANY, out_shape=jax.ShapeDtypeStruct((2, rh, cs), F32),
        scratch_shapes=[pltpu.SemaphoreType.DMA(()), pltpu.SemaphoreType.DMA(()), pltpu.SemaphoreType.DMA(())],
    )(half)


def _reduce_scatter(dw, kind, place, tag):
    got_pair = _rs_pair(dw, kind, f"rs_pair_{tag}")
    part = _pair_sum(dw, got_pair, place, kind, f"rs_pair_sum_{tag}")
    got_chips = _rs_chips(part, f"rs_chips_{tag}")
    half = _chip_sum(dw, got_pair, got_chips, place, kind, f"rs_chip_sum_{tag}")
    both = _rs_join(half, f"rs_join_{tag}")
    rows, cols = dw.shape
    return both.reshape((rows, cols // 4) if kind == "col" else (rows // 4, cols))


def _adamw_math(w, g, m, v):
    m = ADAM_B1 * m + (1.0 - ADAM_B1) * g
    v = ADAM_B2 * v + (1.0 - ADAM_B2) * (g * g)
    m_hat = m / (1.0 - ADAM_B1 ** ADAM_STEP)
    v_hat = v / (1.0 - ADAM_B2 ** ADAM_STEP)
    return -ADAM_LR * (m_hat / (jnp.sqrt(v_hat) + ADAM_EPS) + ADAM_WD * w), m, v


def _adamw(w, g, m, v, name):
    rows, cols = w.shape
    tr = _tile(rows, 256)

    def body(w_ref, g_ref, m_ref, v_ref, d_ref, mo_ref, vo_ref):
        d_ref[...], mo_ref[...], vo_ref[...] = _adamw_math(w_ref[...], g_ref[...], m_ref[...], v_ref[...])

    spec = pl.BlockSpec((tr, cols), lambda i: (i, 0))
    return pl.pallas_call(body, name=name, grid=(rows // tr,), in_specs=[spec] * 4, out_specs=[spec] * 3,
                          out_shape=[jax.ShapeDtypeStruct((rows, cols), F32)] * 3, compiler_params=_params("parallel"))(w, g, m, v)


def _pack(arrays):
    flat = jnp.concatenate([a.reshape(-1) for a in arrays])
    pad = (-flat.shape[0]) % 1024
    return jnp.pad(flat, (0, pad)).reshape(-1, 128)


def _unpack(packed, like):
    flat = packed.reshape(-1)
    out, off = [], 0
    for a in like:
        out.append(flat[off:off + a.size].reshape(a.shape))
        off += a.size
    return out


def _gather_small(g, name):
    rows = g.shape[0]

    def body(g_ref, o_ref, send, recv, local_sem):
        x, y, c, _ = _place()
        me = 4 * x + 2 * y + c
        local = pltpu.make_async_copy(g_ref, o_ref.at[me], local_sem)
        local.start()
        copies = []
        for j in range(1, 8):
            px = 1 - x if j & 4 else x
            py = 1 - y if j & 2 else y
            pc = 1 - c if j & 1 else c
            copies.append(pltpu.make_async_remote_copy(src_ref=g_ref, dst_ref=o_ref.at[me], send_sem=send.at[j - 1],
                                                       recv_sem=recv.at[j - 1], device_id=(px, py, pc), device_id_type=MESH))
        for cp in copies:
            cp.start()
        for j in range(1, 8):
            px = 1 - x if j & 4 else x
            py = 1 - y if j & 2 else y
            pc = 1 - c if j & 1 else c
            pltpu.make_async_remote_copy(src_ref=g_ref, dst_ref=o_ref.at[4 * px + 2 * py + pc], send_sem=send.at[j - 1],
                                         recv_sem=recv.at[j - 1], device_id=(px, py, pc), device_id_type=MESH).wait_recv()
        for cp in copies:
            cp.wait_send()
        local.wait()

    vmem = pl.BlockSpec(memory_space=pltpu.VMEM)
    return pl.pallas_call(
        body, name=name, in_specs=[vmem], out_specs=vmem, out_shape=jax.ShapeDtypeStruct((8, rows, 128), F32),
        scratch_shapes=[pltpu.SemaphoreType.DMA((7,)), pltpu.SemaphoreType.DMA((7,)), pltpu.SemaphoreType.DMA(())],
        compiler_params=_params(),
    )(g)


def _small_update(parts, w, m, v, name):
    rows = w.shape[0]

    def body(p_ref, w_ref, m_ref, v_ref, g_ref, d_ref, mo_ref, vo_ref):
        g = p_ref[0]
        for k in range(1, 8):
            g = g + p_ref[k]
        g_ref[...] = g
        d_ref[...], mo_ref[...], vo_ref[...] = _adamw_math(w_ref[...], g, m_ref[...], v_ref[...])

    return pl.pallas_call(body, name=name, out_shape=[jax.ShapeDtypeStruct((rows, 128), F32)] * 4, compiler_params=_params())(parts, w, m, v)


SMALL = ("norm_pre_mix", "norm_post_mix", "norm_pre_ffn", "norm_post_ffn", "sgu_ln_g", "sgu_ln_b", "sgu_w", "sgu_b")
BIG = (("ab_w_in", "col"), ("ab_w_out", "row"), ("sb_w_in", "col"), ("sb_w_out", "row"), ("ffn_w1", "col"), ("ffn_w2", "row"))
WEIGHTS = ("norm_pre_mix", "norm_post_mix", "norm_pre_ffn", "norm_post_ffn", "ab_w_in", "sgu_ln_g", "sgu_ln_b", "sgu_w", "sgu_b",
           "ab_w_out", "sb_w_in", "sb_w_out", "ffn_w1", "ffn_w2")


def kernel(x, norm_pre_mix, norm_post_mix, norm_pre_ffn, norm_post_ffn, ab_w_in, sgu_ln_g, sgu_ln_b, sgu_w, sgu_b, ab_w_out, sb_w_in, sb_w_out, ffn_w1, ffn_w2, loss_target, m_norm_pre_mix, m_norm_post_mix, m_norm_pre_ffn, m_norm_post_ffn, m_ab_w_in, m_sgu_ln_g, m_sgu_ln_b, m_sgu_w, m_sgu_b, m_ab_w_out, m_sb_w_in, m_sb_w_out, m_ffn_w1, m_ffn_w2, v_norm_pre_mix, v_norm_post_mix, v_norm_pre_ffn, v_norm_post_ffn, v_ab_w_in, v_sgu_ln_g, v_sgu_ln_b, v_sgu_w, v_sgu_b, v_ab_w_out, v_sb_w_in, v_sb_w_out, v_ffn_w1, v_ffn_w2):
    w = dict(norm_pre_mix=norm_pre_mix, norm_post_mix=norm_post_mix, norm_pre_ffn=norm_pre_ffn, norm_post_ffn=norm_post_ffn,
             ab_w_in=ab_w_in, sgu_ln_g=sgu_ln_g, sgu_ln_b=sgu_ln_b, sgu_w=sgu_w, sgu_b=sgu_b, ab_w_out=ab_w_out, sb_w_in=sb_w_in,
             sb_w_out=sb_w_out, ffn_w1=ffn_w1, ffn_w2=ffn_w2)
    m = dict(norm_pre_mix=m_norm_pre_mix, norm_post_mix=m_norm_post_mix, norm_pre_ffn=m_norm_pre_ffn, norm_post_ffn=m_norm_post_ffn,
             ab_w_in=m_ab_w_in, sgu_ln_g=m_sgu_ln_g, sgu_ln_b=m_sgu_ln_b, sgu_w=m_sgu_w, sgu_b=m_sgu_b, ab_w_out=m_ab_w_out,
             sb_w_in=m_sb_w_in, sb_w_out=m_sb_w_out, ffn_w1=m_ffn_w1, ffn_w2=m_ffn_w2)
    v = dict(norm_pre_mix=v_norm_pre_mix, norm_post_mix=v_norm_post_mix, norm_pre_ffn=v_norm_pre_ffn, norm_post_ffn=v_norm_post_ffn,
             ab_w_in=v_ab_w_in, sgu_ln_g=v_sgu_ln_g, sgu_ln_b=v_sgu_ln_b, sgu_w=v_sgu_w, sgu_b=v_sgu_b, ab_w_out=v_ab_w_out,
             sb_w_in=v_sb_w_in, sb_w_out=v_sb_w_out, ffn_w1=v_ffn_w1, ffn_w2=v_ffn_w2)
    xi, yi, ci = lax.axis_index("x"), lax.axis_index("y"), lax.axis_index("c")
    place = jnp.stack([ci, 2 * xi + yi]).astype(jnp.int32)

    big = {}
    for name, kind in BIG:
        layers = [_all_gather(_to_bf16(w[name][l], f"bf16_{name}{l}"), kind, f"gather_{name}{l}") for l in range(w[name].shape[0])]
        big[name] = layers if len(layers) > 1 else layers[0]

    norms = {k: w["norm_" + k] for k in ("pre_mix", "post_mix", "pre_ffn", "post_ffn")}
    sgu = (sgu_ln_g, sgu_ln_b, sgu_w[0], sgu_b[0])
    loss_blk, grad_x, d_norms, d_sgu, d_big = _local_step(x[0], loss_target[0], norms, sgu, big)
    loss = lax.psum(loss_blk[0, 0], ("x", "y", "c"))

    grads, deltas, new_m, new_v = {}, {}, {}, {}
    small_g = [d_norms["pre_mix"], d_norms["post_mix"], d_norms["pre_ffn"], d_norms["post_ffn"],
               d_sgu[0], d_sgu[1], d_sgu[2][None], d_sgu[3][None]]
    gathered = _gather_small(_pack(small_g), "gather_small")
    outs = _small_update(gathered, _pack([w[k] for k in SMALL]), _pack([m[k] for k in SMALL]), _pack([v[k] for k in SMALL]), "small_update")
    like = [w[k] for k in SMALL]
    for dst, packed in zip((grads, deltas, new_m, new_v), outs):
        for k, a in zip(SMALL, _unpack(packed, like)):
            dst[k] = a

    for name, kind in BIG:
        per_layer = []
        for l in range(w[name].shape[0]):
            dw = d_big[name][l] if isinstance(d_big[name], list) else d_big[name]
            g = _reduce_scatter(dw, kind, place, f"{name}{l}")
            per_layer.append((g,) + tuple(_adamw(w[name][l], g, m[name][l], v[name][l], f"adamw_{name}{l}")))
        for i, dst in enumerate((grads, deltas, new_m, new_v)):
            dst[name] = jnp.stack([p[i] for p in per_layer])

    return (loss, grad_x[None], *[grads[k] for k in WEIGHTS], *[deltas[k] for k in WEIGHTS],
            *[new_m[k] for k in WEIGHTS], *[new_v[k] for k in WEIGHTS])
```

```python
import functools

import jax
import jax.numpy as jnp
from jax import lax
from jax.experimental import pallas as pl
from jax.experimental.pallas import tpu as pltpu

F32 = jnp.float32
BF16 = jnp.bfloat16
MESH = pl.DeviceIdType.MESH

HEAD_DIM = 128
CHUNK = 128
DILATIONS = (1, 4, 16)
SB_BLOCK = 256
RMS_EPS = 1e-6
LN_EPS = 1e-5
ADAM_LR, ADAM_B1, ADAM_B2, ADAM_EPS, ADAM_WD, ADAM_STEP = 0.001, 0.9, 0.999, 1e-08, 0.01, 10
NEG = -1e30
V7X_VMEM_LIMIT = 48 * 1024 * 1024
ANY = pl.BlockSpec(memory_space=pl.ANY)


def _params(*sem):
    return pltpu.CompilerParams(dimension_semantics=sem if sem else None, vmem_limit_bytes=V7X_VMEM_LIMIT)


def _tile(n, pref):
    if n <= pref:
        return n
    t = pref
    while n % t:
        t -= 128
    return t


def _dot(a, b, dims):
    return lax.dot_general(a, b, (dims, ((), ())), preferred_element_type=F32)


NN = ((1,), (0,))
NT = ((1,), (1,))
TN = ((0,), (0,))


def _place():
    x, y, c = lax.axis_index("x"), lax.axis_index("y"), lax.axis_index("c")
    return x, y, c, 2 * x + y


def _flip(x, y, c, j):
    return (1 - x if j & 4 else x), (1 - y if j & 2 else y), (1 - c if j & 1 else c)


def _half_shape(full_shape, kind):
    rows, cols = full_shape
    return (rows // 2, cols // 4) if kind == "col" else (rows // 8, cols)


def _half(ref, kind, s, h):
    rh, cs = _half_shape(ref.shape, kind)
    if kind == "col":
        return ref.at[pl.ds(h * rh, rh), pl.ds(s * cs, cs)]
    return ref.at[pl.ds((2 * s + h) * rh, rh), :]


def _remote(src, dst, send, recv, to):
    return pltpu.make_async_remote_copy(src_ref=src, dst_ref=dst, send_sem=send, recv_sem=recv, device_id=to, device_id_type=MESH)


class _Gather:
    n_sems = 6

    def __init__(self, full, kind):
        self.ro, self.rw, self.kind = [], [full], kind

    def start(self, ro, rw, send, recv):
        x, y, c, mine = _place()
        own = _half(rw[0], self.kind, mine, c)
        for k, j in enumerate((2, 4, 6)):
            px, py, _ = _flip(x, y, c, j)
            _remote(own, own, send(k), recv(k), (px, py, c)).start()

    def finish(self, ro, rw, send, recv):
        x, y, c, mine = _place()
        own = _half(rw[0], self.kind, mine, c)
        for k, j in enumerate((2, 4, 6)):
            px, py, _ = _flip(x, y, c, j)
            got = _half(rw[0], self.kind, 2 * px + py, c)
            _remote(got, got, send(k), recv(k), (x, y, c)).wait_recv()
            _remote(got, got, send(3 + k), recv(3 + k), (x, y, 1 - c)).start()
        for k, j in enumerate((2, 4, 6)):
            px, py, _ = _flip(x, y, c, j)
            got = _half(rw[0], self.kind, 2 * px + py, 1 - c)
            _remote(got, got, send(3 + k), recv(3 + k), (x, y, c)).wait_recv()
        for k in range(6):
            _remote(own, own, send(k), recv(k), (x, y, c)).wait_send()


class _Scatter:
    def __init__(self, dw16, got, kind, patterns):
        self.ro, self.rw, self.kind, self.patterns, self.n_sems = [dw16], [got], kind, patterns, len(patterns)

    def start(self, ro, rw, send, recv):
        x, y, c, _ = _place()
        for k, j in enumerate(self.patterns):
            px, py, pc = _flip(x, y, c, j)
            _remote(_half(ro[0], self.kind, 2 * px + py, pc), rw[0].at[j - 1], send(k), recv(k), (px, py, pc)).start()

    def finish(self, ro, rw, send, recv):
        x, y, c, _ = _place()
        for k, j in enumerate(self.patterns):
            slot = rw[0].at[j - 1]
            cp = _remote(slot, slot, send(k), recv(k), (x, y, c))
            cp.wait_recv()
            cp.wait_send()


class _Join:
    def __init__(self, bufs):
        self.ro, self.rw, self.n_sems = [], list(bufs), sum(b.shape[0] for b in bufs)

    def _copies(self, rw, send, recv, slot):
        x, y, c, _ = _place()
        k = 0
        for ref in rw:
            for l in range(ref.shape[0]):
                yield _remote(ref.at[l, c], ref.at[l, slot(c)], send(k), recv(k), (x, y, 1 - c))
                k += 1

    def start(self, ro, rw, send, recv):
        for cp in self._copies(rw, send, recv, lambda c: c):
            cp.start()

    def finish(self, ro, rw, send, recv):
        for cp in self._copies(rw, send, recv, lambda c: 1 - c):
            cp.wait_recv()
        for cp in self._copies(rw, send, recv, lambda c: c):
            cp.wait_send()


def _comm_layout(comms):
    ro = [a for c in comms for a in c.ro]
    rw = [a for c in comms for a in c.rw]
    return ro, rw, sum(c.n_sems for c in comms)


def _comm_each(comms, method, ro_refs, rw_refs, send, recv):
    i_ro = i_rw = i_sem = 0
    for c in comms:
        getattr(c, method)(ro_refs[i_ro:i_ro + len(c.ro)], rw_refs[i_rw:i_rw + len(c.rw)],
                           lambda k, b=i_sem: send.at[b + k], lambda k, b=i_sem: recv.at[b + k])
        i_ro, i_rw, i_sem = i_ro + len(c.ro), i_rw + len(c.rw), i_sem + c.n_sems


def _split_results(comms, rws):
    out, i = [], 0
    for c in comms:
        out.append(list(rws[i:i + len(c.rw)]))
        i += len(c.rw)
    return out


def _comm_call(comms, name):
    ro, rw, n_sems = _comm_layout(comms)

    def body(*refs):
        ro_refs = refs[:len(ro)]
        rw_refs = refs[len(ro) + len(rw):len(ro) + 2 * len(rw)]
        send, recv = refs[len(ro) + 2 * len(rw):]
        _comm_each(comms, "start", ro_refs, rw_refs, send, recv)
        _comm_each(comms, "finish", ro_refs, rw_refs, send, recv)

    rws = pl.pallas_call(
        body, name=name, in_specs=[ANY] * (len(ro) + len(rw)), out_specs=[ANY] * len(rw),
        out_shape=[jax.ShapeDtypeStruct(a.shape, a.dtype) for a in rw],
        input_output_aliases={len(ro) + k: k for k in range(len(rw))},
        scratch_shapes=[pltpu.SemaphoreType.DMA((n_sems,)), pltpu.SemaphoreType.DMA((n_sems,))],
    )(*ro, *rw)
    return _split_results(comms, rws)


def _pcall(body, args, *, name, grid, in_specs, out_specs, out_shape, scratch=(), sem=(), comms=()):
    n_in, n_out, n_scr = len(in_specs), len(out_specs), len(scratch)
    if not comms:
        return pl.pallas_call(body, name=name, grid=grid, in_specs=list(in_specs), out_specs=list(out_specs),
                              out_shape=list(out_shape), scratch_shapes=list(scratch), compiler_params=_params(*sem))(*args)
    ro, rw, n_sems = _comm_layout(comms)

    def carrier(*refs):
        ins = refs[:n_in]
        ro_refs = refs[n_in:n_in + len(ro)]
        o0 = n_in + len(ro) + len(rw)
        outs = refs[o0:o0 + n_out]
        rw_refs = refs[o0 + n_out:o0 + n_out + len(rw)]
        s0 = o0 + n_out + len(rw)
        send, recv = refs[s0 + n_scr], refs[s0 + n_scr + 1]
        ids = [pl.program_id(a) for a in range(len(grid))]
        first = functools.reduce(jnp.logical_and, [i == 0 for i in ids])
        last = functools.reduce(jnp.logical_and, [i == g - 1 for i, g in zip(ids, grid)])

        @pl.when(first)
        def _():
            _comm_each(comms, "start", ro_refs, rw_refs, send, recv)

        body(*ins, *outs, *refs[s0:s0 + n_scr])

        @pl.when(last)
        def _():
            _comm_each(comms, "finish", ro_refs, rw_refs, send, recv)

    res = pl.pallas_call(
        carrier, name=name, grid=grid, in_specs=list(in_specs) + [ANY] * (len(ro) + len(rw)),
        out_specs=list(out_specs) + [ANY] * len(rw),
        out_shape=list(out_shape) + [jax.ShapeDtypeStruct(a.shape, a.dtype) for a in rw],
        input_output_aliases={n_in + len(ro) + k: n_out + k for k in range(len(rw))},
        scratch_shapes=list(scratch) + [pltpu.SemaphoreType.DMA((n_sems,)), pltpu.SemaphoreType.DMA((n_sems,))],
        compiler_params=_params(*["arbitrary"] * len(grid)),
    )(*args, *ro, *rw)
    return list(res[:n_out]), _split_results(comms, res[n_out:])


def _matmul(a, b, mode, out_dtype, name, a_square=False, relu_out=False, mul2=None, comms=()):
    if mode == "nn":
        (m, k), n = a.shape, b.shape[1]
    elif mode == "nt":
        (m, k), n = a.shape, b.shape[0]
    else:
        (k, m), n = a.shape, b.shape[1]
    tm, tn, tk = _tile(m, 1024), _tile(n, 1024), _tile(k, 512)
    nk = k // tk
    dims = {"nn": NN, "nt": NT, "tn": TN}[mode]
    a_spec = pl.BlockSpec((tk, tm), lambda i, j, kk: (kk, i)) if mode == "tn" else pl.BlockSpec((tm, tk), lambda i, j, kk: (i, kk))
    b_spec = pl.BlockSpec((tn, tk), lambda i, j, kk: (j, kk)) if mode == "nt" else pl.BlockSpec((tk, tn), lambda i, j, kk: (kk, j))
    o_spec = pl.BlockSpec((tm, tn), lambda i, j, kk: (i, j))

    def body(a_ref, b_ref, *rest):
        if mul2 is None:
            o_ref, acc_ref = rest
        else:
            m_ref, o_ref, acc_ref = rest
        kk = pl.program_id(2)

        @pl.when(kk == 0)
        def _():
            acc_ref[...] = jnp.zeros_like(acc_ref)

        av = a_ref[...]
        if a_square:
            av = av * av
        acc_ref[...] += _dot(av, b_ref[...], dims)

        @pl.when(kk == nk - 1)
        def _():
            r = acc_ref[...]
            if relu_out:
                r = jnp.maximum(r, 0.0)
            if mul2 is not None:
                r = r * (2.0 * m_ref[...].astype(F32))
            o_ref[...] = r.astype(out_dtype)

    args = (a, b) if mul2 is None else (a, b, mul2)
    specs = [a_spec, b_spec] + ([] if mul2 is None else [o_spec])
    res = _pcall(body, args, name=name, grid=(m // tm, n // tn, nk), in_specs=specs, out_specs=[o_spec],
                 out_shape=[jax.ShapeDtypeStruct((m, n), out_dtype)], scratch=[pltpu.VMEM((tm, tn), F32)],
                 sem=("parallel", "parallel", "arbitrary"), comms=comms)
    return (res[0][0], res[1]) if comms else res[0]


NORM_ROWS = 256


def _rms(x, g):
    rstd = lax.rsqrt(jnp.mean(x * x, axis=-1, keepdims=True) + RMS_EPS)
    n = x * rstd
    return n * g, n, rstd


def _rms_bwd(n, rstd, g, dout):
    dn = dout * g
    return rstd * (dn - n * jnp.mean(dn * n, axis=-1, keepdims=True))


def _row_spec(d):
    return pl.BlockSpec((NORM_ROWS, d), lambda i: (i, 0))


def _vec_spec(d):
    return pl.BlockSpec((1, d), lambda i: (0, 0))


def _accumulate(ref, val):
    @pl.when(pl.program_id(0) == 0)
    def _():
        ref[...] = jnp.zeros_like(ref)

    ref[...] += val


def _rms_fwd(x, g, name):
    t, d = x.shape

    def body(x_ref, g_ref, h_ref):
        h_ref[...] = _rms(x_ref[...], g_ref[...])[0].astype(BF16)

    return pl.pallas_call(
        body, name=name, grid=(t // NORM_ROWS,), in_specs=[_row_spec(d), _vec_spec(d)], out_specs=_row_spec(d),
        out_shape=jax.ShapeDtypeStruct((t, d), BF16), compiler_params=_params("parallel"),
    )(x, g)


def _post_pre_fwd(y, g_post, x, g_pre, name):
    t, d = x.shape

    def body(y_ref, gp_ref, x_ref, gn_ref, xn_ref, h_ref):
        xn = x_ref[...] + _rms(y_ref[...], gp_ref[...])[0]
        xn_ref[...] = xn
        h_ref[...] = _rms(xn, gn_ref[...])[0].astype(BF16)

    return pl.pallas_call(
        body, name=name, grid=(t // NORM_ROWS,),
        in_specs=[_row_spec(d), _vec_spec(d), _row_spec(d), _vec_spec(d)], out_specs=[_row_spec(d), _row_spec(d)],
        out_shape=[jax.ShapeDtypeStruct((t, d), F32), jax.ShapeDtypeStruct((t, d), BF16)], compiler_params=_params("parallel"),
    )(y, g_post, x, g_pre)


def _final_fwd_bwd(y, g_post, x, target, name):
    t, d = x.shape

    def body(y_ref, g_ref, x_ref, t_ref, loss_ref, dx_ref, dy_ref, dg_ref):
        g = g_ref[...]
        out, n, rstd = _rms(y_ref[...], g)
        e = x_ref[...] + out - t_ref[...]
        _accumulate(loss_ref, jnp.full(loss_ref.shape, 0.5 / d, F32) * jnp.sum(e * e))
        dx = e * (1.0 / d)
        dx_ref[...] = dx
        dy_ref[...] = _rms_bwd(n, rstd, g, dx).astype(BF16)
        _accumulate(dg_ref, jnp.sum(dx * n, axis=0, keepdims=True))

    return pl.pallas_call(
        body, name=name, grid=(t // NORM_ROWS,),
        in_specs=[_row_spec(d), _vec_spec(d), _row_spec(d), _row_spec(d)],
        out_specs=[pl.BlockSpec((8, 128), lambda i: (0, 0)), _row_spec(d), _row_spec(d), _vec_spec(d)],
        out_shape=[jax.ShapeDtypeStruct((8, 128), F32), jax.ShapeDtypeStruct((t, d), F32),
                   jax.ShapeDtypeStruct((t, d), BF16), jax.ShapeDtypeStruct((1, d), F32)],
        compiler_params=_params("arbitrary"),
    )(y, g_post, x, target)


def _pre_post_bwd(x, g_pre, dh, dx_in, y, g_post, name, comms=()):
    t, d = x.shape
    both = y is not None

    def body(x_ref, gp_ref, dh_ref, dxi_ref, *rest):
        if both:
            y_ref, gq_ref, dx_ref, dy_ref, dgp_ref, dgq_ref = rest
        else:
            dx_ref, dgp_ref = rest
        gp = gp_ref[...]
        _, n, rstd = _rms(x_ref[...], gp)
        dh_v = dh_ref[...]
        dx = dxi_ref[...] + _rms_bwd(n, rstd, gp, dh_v)
        dx_ref[...] = dx
        _accumulate(dgp_ref, jnp.sum(dh_v * n, axis=0, keepdims=True))
        if both:
            gq = gq_ref[...]
            _, ny, rstdy = _rms(y_ref[...], gq)
            dy_ref[...] = _rms_bwd(ny, rstdy, gq, dx).astype(BF16)
            _accumulate(dgq_ref, jnp.sum(dx * ny, axis=0, keepdims=True))

    in_specs = [_row_spec(d), _vec_spec(d), _row_spec(d), _row_spec(d)]
    args = [x, g_pre, dh, dx_in]
    if both:
        in_specs += [_row_spec(d), _vec_spec(d)]
        args += [y, g_post]
        out_specs = [_row_spec(d), _row_spec(d), _vec_spec(d), _vec_spec(d)]
        out_shape = [jax.ShapeDtypeStruct((t, d), F32), jax.ShapeDtypeStruct((t, d), BF16),
                     jax.ShapeDtypeStruct((1, d), F32), jax.ShapeDtypeStruct((1, d), F32)]
    else:
        out_specs = [_row_spec(d), _vec_spec(d)]
        out_shape = [jax.ShapeDtypeStruct((t, d), F32), jax.ShapeDtypeStruct((1, d), F32)]
    return _pcall(body, args, name=name, grid=(t // NORM_ROWS,), in_specs=in_specs, out_specs=out_specs, out_shape=out_shape,
                  sem=("arbitrary",), comms=comms)


def _gelu(x):
    return 0.5 * x * (1.0 + lax.erf(x * 0.7071067811865476))


def _gelu_grad(x):
    return 0.5 * (1.0 + lax.erf(x * 0.7071067811865476)) + x * jnp.exp(-0.5 * x * x) * 0.3989422804014327


def _layernorm(v, g, b):
    mu = jnp.mean(v, axis=-1, keepdims=True)
    vc = v - mu
    rs = lax.rsqrt(jnp.mean(vc * vc, axis=-1, keepdims=True) + LN_EPS)
    vhat = vc * rs
    return vhat * g + b, vhat, rs


def _tril_mask():
    return lax.broadcasted_iota(jnp.int32, (CHUNK, CHUNK), 0) >= lax.broadcasted_iota(jnp.int32, (CHUNK, CHUNK), 1)


def _sgu_fwd(z, ln_g, ln_b, w16, bias_b, name):
    t = z.shape[0]
    groups = w16.shape[0]
    a = groups * CHUNK

    def body(u_ref, v_ref, g_ref, b_ref, w_ref, bb_ref, o_ref):
        u = _gelu(u_ref[...].astype(F32))
        vn = _layernorm(_gelu(v_ref[...].astype(F32)), g_ref[...], b_ref[...])[0].astype(BF16)
        tril = _tril_mask()
        for g in range(groups):
            sl = slice(g * CHUNK, (g + 1) * CHUNK)
            w = jnp.where(tril, w_ref[g], jnp.zeros((), BF16))
            mixed = _dot(w, vn[:, sl], NN) + bb_ref[g]
            o_ref[:, sl] = (u[:, sl] * mixed).astype(BF16)

    full3 = pl.BlockSpec((groups, CHUNK, CHUNK), lambda c: (0, 0, 0))
    return pl.pallas_call(
        body, name=name, grid=(t // CHUNK,),
        in_specs=[pl.BlockSpec((CHUNK, a), lambda c: (c, 0)), pl.BlockSpec((CHUNK, a), lambda c: (c, 1)),
                  _vec_spec(a), _vec_spec(a), full3, full3],
        out_specs=pl.BlockSpec((CHUNK, a), lambda c: (c, 0)),
        out_shape=jax.ShapeDtypeStruct((t, a), BF16), compiler_params=_params("parallel"),
    )(z, z, ln_g, ln_b, w16, bias_b)


def _sgu_bwd(z, dab, ln_g, ln_b, w16, bias_b, name):
    t = z.shape[0]
    groups = w16.shape[0]
    a = groups * CHUNK

    def body(u_ref, v_ref, da_ref, g_ref, b_ref, w_ref, bb_ref, duv_ref, dg_ref, db_ref, dw_ref, dbs_ref, dvn_ref):
        up = u_ref[...].astype(F32)
        vp = v_ref[...].astype(F32)
        u = _gelu(up)
        ln_gain = g_ref[...]
        vn32, vhat, rs = _layernorm(_gelu(vp), ln_gain, b_ref[...])
        vn = vn32.astype(BF16)
        da = da_ref[...].astype(F32)
        tril = _tril_mask()
        ones = jnp.ones((8, CHUNK), F32)

        @pl.when(pl.program_id(0) == 0)
        def _():
            dw_ref[...] = jnp.zeros_like(dw_ref)
            dbs_ref[...] = jnp.zeros_like(dbs_ref)

        for g in range(groups):
            sl = slice(g * CHUNK, (g + 1) * CHUNK)
            w = jnp.where(tril, w_ref[g], jnp.zeros((), BF16))
            mixed = _dot(w, vn[:, sl], NN) + bb_ref[g]
            dmix = da[:, sl] * u[:, sl]
            dmix16 = dmix.astype(BF16)
            duv_ref[:, sl] = (da[:, sl] * mixed * _gelu_grad(up[:, sl])).astype(BF16)
            dvn_ref[:, sl] = _dot(w, dmix16, TN)
            dw_ref[g] += jnp.where(tril, _dot(dmix16, vn[:, sl], NT), 0.0)
            dbs_ref[g:g + 1, :] += lax.dot_general(ones, dmix, (NT, ((), ())), precision=lax.Precision.HIGHEST,
                                                   preferred_element_type=F32)[0:1]
        dvn = dvn_ref[...]
        dvhat = dvn * ln_gain
        dva = rs * (dvhat - jnp.mean(dvhat, axis=-1, keepdims=True) - vhat * jnp.mean(dvhat * vhat, axis=-1, keepdims=True))
        duv_ref[:, a:] = (dva * _gelu_grad(vp)).astype(BF16)
        _accumulate(dg_ref, jnp.sum(dvn * vhat, axis=0, keepdims=True))
        _accumulate(db_ref, jnp.sum(dvn, axis=0, keepdims=True))

    full3 = pl.BlockSpec((groups, CHUNK, CHUNK), lambda c: (0, 0, 0))
    return pl.pallas_call(
        body, name=name, grid=(t // CHUNK,),
        in_specs=[pl.BlockSpec((CHUNK, a), lambda c: (c, 0)), pl.BlockSpec((CHUNK, a), lambda c: (c, 1)),
                  pl.BlockSpec((CHUNK, a), lambda c: (c, 0)), _vec_spec(a), _vec_spec(a), full3, full3],
        out_specs=[pl.BlockSpec((CHUNK, 2 * a), lambda c: (c, 0)), _vec_spec(a), _vec_spec(a), full3,
                   pl.BlockSpec((groups, CHUNK), lambda c: (0, 0))],
        out_shape=[jax.ShapeDtypeStruct((t, 2 * a), BF16), jax.ShapeDtypeStruct((1, a), F32), jax.ShapeDtypeStruct((1, a), F32),
                   jax.ShapeDtypeStruct((groups, CHUNK, CHUNK), F32), jax.ShapeDtypeStruct((groups, CHUNK), F32)],
        scratch_shapes=[pltpu.VMEM((CHUNK, a), F32)], compiler_params=_params("arbitrary"),
    )(z, z, dab, ln_g, ln_b, w16, bias_b)


def _dil_masks(d):
    qi = lax.broadcasted_iota(jnp.int32, (CHUNK, CHUNK), 0)
    kj = lax.broadcasted_iota(jnp.int32, (CHUNK, CHUNK), 1)
    dist_c = qi - kj
    return dist_c >= 0, dist_c <= 0, (dist_c * d).astype(F32), ((dist_c + CHUNK) * d).astype(F32)


def _alibi_slope(h, heads):
    return 2.0 ** (-8.0 * (h + 1) / heads)


def _dil_fwd(z, d, name):
    t = z.shape[0]
    w = z.shape[1] // 5
    heads = w // HEAD_DIM
    nb = t // d // CHUNK
    scale = HEAD_DIM ** -0.5

    def body(q_ref, kp_ref, kc_ref, vp_ref, vc_ref, o_ref, l_ref):
        ok_c, ok_p0, bias_c, bias_p = _dil_masks(d)
        ok_p = ok_p0 & (pl.program_id(1) > 0)
        for h in range(heads):
            sl = slice(h * HEAD_DIM, (h + 1) * HEAD_DIM)
            slope = _alibi_slope(h, heads)
            q = q_ref[:, sl]
            s_c = jnp.where(ok_c, _dot(q, kc_ref[:, sl], NT) * scale - slope * bias_c, NEG)
            s_p = jnp.where(ok_p, _dot(q, kp_ref[:, sl], NT) * scale - slope * bias_p, NEG)
            m = jnp.maximum(jnp.max(s_c, axis=1, keepdims=True), jnp.max(s_p, axis=1, keepdims=True))
            p_c = jnp.exp(s_c - m)
            p_p = jnp.exp(s_p - m)
            den = jnp.sum(p_c, axis=1, keepdims=True) + jnp.sum(p_p, axis=1, keepdims=True)
            o = _dot(p_c.astype(BF16), vc_ref[:, sl], NN) + _dot(p_p.astype(BF16), vp_ref[:, sl], NN)
            o_ref[:, sl] = o / den
            l_ref[:, sl] = jnp.broadcast_to(m + jnp.log(den), (CHUNK, HEAD_DIM))

    def zspec(col, prev):
        if prev:
            return pl.BlockSpec((CHUNK, w), lambda r, n: (jnp.maximum(n - 1, 0), r * 5 + col))
        return pl.BlockSpec((CHUNK, w), lambda r, n: (n, r * 5 + col))

    ospec = pl.BlockSpec((CHUNK, w), lambda r, n: (n, r))
    zv = z.reshape(t // d, d * 5 * w)
    o, lse = pl.pallas_call(
        body, name=name, grid=(d, nb),
        in_specs=[zspec(2, False), zspec(3, True), zspec(3, False), zspec(4, True), zspec(4, False)],
        out_specs=[ospec, ospec],
        out_shape=[jax.ShapeDtypeStruct((t // d, d * w), F32), jax.ShapeDtypeStruct((t // d, d * w), F32)],
        compiler_params=_params("parallel", "parallel"),
    )(zv, zv, zv, zv, zv)
    return o.reshape(t, w), lse.reshape(t, w)


def _dil_merge(a_out, outs, lses, name):
    t, a = a_out.shape
    w = outs[0].shape[1]
    nbr = len(outs)

    def body(a_ref, *rest):
        o_refs, l_refs, (ab_ref, lt_ref) = rest[:nbr], rest[nbr:2 * nbr], rest[2 * nbr:]
        ls = [r[...] for r in l_refs]
        m = functools.reduce(jnp.maximum, ls)
        ws = [jnp.exp(l - m) for l in ls]
        tot = functools.reduce(jnp.add, ws)
        mix = functools.reduce(jnp.add, [wt * r[...] for wt, r in zip(ws, o_refs)]) / tot
        ab_ref[:, :a] = a_ref[...]
        ab_ref[:, a:] = mix.astype(BF16)
        lt_ref[...] = m + jnp.log(tot)

    return pl.pallas_call(
        body, name=name, grid=(t // NORM_ROWS,),
        in_specs=[_row_spec(a)] + [_row_spec(w)] * (2 * nbr), out_specs=[_row_spec(a + w), _row_spec(w)],
        out_shape=[jax.ShapeDtypeStruct((t, a + w), BF16), jax.ShapeDtypeStruct((t, w), F32)],
        compiler_params=_params("parallel"),
    )(a_out, *outs, *lses)


def _dil_bwd(z, ab, dab, ltot, d, name, comms=()):
    t = z.shape[0]
    w = z.shape[1] // 5
    heads = w // HEAD_DIM
    nb = t // d // CHUNK
    scale = HEAD_DIM ** -0.5

    def body(q_ref, qn_ref, kp_ref, kc_ref, vp_ref, vc_ref, o_ref, on_ref, do_ref, don_ref, l_ref, ln_ref,
             dq_ref, dk_ref, dv_ref):
        n = pl.program_id(1)
        ok_c, ok_p0, bias_c, bias_p = _dil_masks(d)
        ok_p = ok_p0 & (n > 0)
        ok_n = ok_p0 & (n < nb - 1)
        for h in range(heads):
            sl = slice(h * HEAD_DIM, (h + 1) * HEAD_DIM)
            slope = _alibi_slope(h, heads)
            q, qn, kp, kc, vp, vc = q_ref[:, sl], qn_ref[:, sl], kp_ref[:, sl], kc_ref[:, sl], vp_ref[:, sl], vc_ref[:, sl]
            do, don = do_ref[:, sl], don_ref[:, sl]
            delta = jnp.sum(do.astype(F32) * o_ref[:, sl].astype(F32), axis=1, keepdims=True)
            delta_n = jnp.sum(don.astype(F32) * on_ref[:, sl].astype(F32), axis=1, keepdims=True)
            lt, lt_n = l_ref[:, sl], ln_ref[:, sl]
            p_c = jnp.exp(jnp.where(ok_c, _dot(q, kc, NT) * scale - slope * bias_c, NEG) - lt)
            p_p = jnp.exp(jnp.where(ok_p, _dot(q, kp, NT) * scale - slope * bias_p, NEG) - lt)
            p_n = jnp.exp(jnp.where(ok_n, _dot(qn, kc, NT) * scale - slope * bias_p, NEG) - lt_n)
            ds_c = (p_c * (_dot(do, vc, NT) - delta)).astype(BF16)
            ds_p = (p_p * (_dot(do, vp, NT) - delta)).astype(BF16)
            ds_n = (p_n * (_dot(don, vc, NT) - delta_n)).astype(BF16)
            dq_ref[:, sl] = (_dot(ds_c, kc, NN) + _dot(ds_p, kp, NN)) * scale
            dk_ref[:, sl] = (_dot(ds_c, q, TN) + _dot(ds_n, qn, TN)) * scale
            dv_ref[:, sl] = _dot(p_c.astype(BF16), do, TN) + _dot(p_n.astype(BF16), don, TN)

    def spec(mult, col, shift):
        if shift < 0:
            return pl.BlockSpec((CHUNK, w), lambda r, n: (jnp.maximum(n - 1, 0), r * mult + col))
        if shift > 0:
            return pl.BlockSpec((CHUNK, w), lambda r, n: (jnp.minimum(n + 1, nb - 1), r * mult + col))
        return pl.BlockSpec((CHUNK, w), lambda r, n: (n, r * mult + col))

    zv = z.reshape(t // d, d * 5 * w)
    abv = ab.reshape(t // d, d * 2 * w)
    dabv = dab.reshape(t // d, d * 2 * w)
    lv = ltot.reshape(t // d, d * w)
    ospec = spec(1, 0, 0)
    res = _pcall(
        body, (zv, zv, zv, zv, zv, zv, abv, abv, dabv, dabv, lv, lv), name=name, grid=(d, nb),
        in_specs=[spec(5, 2, 0), spec(5, 2, 1), spec(5, 3, -1), spec(5, 3, 0), spec(5, 4, -1), spec(5, 4, 0),
                  spec(2, 1, 0), spec(2, 1, 1), spec(2, 1, 0), spec(2, 1, 1), spec(1, 0, 0), spec(1, 0, 1)],
        out_specs=[ospec, ospec, ospec], out_shape=[jax.ShapeDtypeStruct((t // d, d * w), F32)] * 3,
        sem=("parallel", "parallel"), comms=comms)
    outs, rws = res if comms else (res, None)
    outs = [o.reshape(t, w) for o in outs]
    return (outs, rws) if comms else outs


def _dz_assemble(duv, parts, name):
    t, a2 = duv.shape
    w = parts[0][0].shape[1]
    nbr = len(parts)

    def body(duv_ref, *rest):
        refs, dz_ref = rest[:-1], rest[-1]
        dz_ref[:, :a2] = duv_ref[...]
        for i in range(3):
            tot = functools.reduce(jnp.add, [refs[b * 3 + i][...] for b in range(nbr)])
            dz_ref[:, a2 + i * w:a2 + (i + 1) * w] = tot.astype(BF16)

    flat = [p for branch in parts for p in branch]
    return pl.pallas_call(
        body, name=name, grid=(t // NORM_ROWS,), in_specs=[_row_spec(a2)] + [_row_spec(w)] * len(flat),
        out_specs=_row_spec(a2 + 3 * w), out_shape=jax.ShapeDtypeStruct((t, a2 + 3 * w), BF16),
        compiler_params=_params("parallel"),
    )(duv, *flat)


def _split_dot(x, m16):
    hi = x.astype(BF16)
    lo = (x - hi.astype(F32)).astype(BF16)
    return _dot(hi, m16, NN) + _dot(lo, m16, NN)


SB_DEAD = -110.0


def _sb_log1m(q, kj, i, j):
    blk = q.shape[0]
    zt = _dot(q, kj, NT) * (HEAD_DIM ** -0.5)
    e = jnp.exp(-jnp.abs(zt))
    rows = lax.broadcasted_iota(jnp.int32, (blk, blk), 0)
    cols = lax.broadcasted_iota(jnp.int32, (blk, blk), 1)
    causal = (j * blk + cols) < (i * blk + rows)
    return zt, e, jnp.where(causal, -(jnp.maximum(zt, 0.0) + jnp.log1p(e)), 0.0), causal


def _sb_beta(zt, e):
    r = 1.0 / (1.0 + e)
    return jnp.where(zt >= 0.0, r, e * r)


def _sb_alive(s, i, c_run):
    return (s <= i) & (jnp.max(c_run) > SB_DEAD)


def _sb_fwd(zc, name, comms=()):
    t = zc.shape[0]
    c = zc.shape[1] // 3
    heads = c // HEAD_DIM
    blk = min(SB_BLOCK, t)

    def body(q_ref, k_ref, v_ref, o_ref):
        i = pl.program_id(1)
        q = q_ref[...]
        rows = lax.broadcasted_iota(jnp.int32, (blk, blk), 0)
        cols = lax.broadcasted_iota(jnp.int32, (blk, blk), 1)
        m_right = (rows > cols).astype(BF16)

        def step(carry):
            s, acc, c_run = carry
            j = i - s
            off = pl.multiple_of(j * blk, blk)
            zt, e, l, causal = _sb_log1m(q, k_ref[pl.ds(off, blk), :], i, j)
            a = jnp.where(causal, _sb_beta(zt, e) * jnp.exp(c_run + _split_dot(l, m_right)), 0.0)
            acc = acc + _dot(a.astype(BF16), v_ref[pl.ds(off, blk), :], NN)
            return s + 1, acc, c_run + jnp.sum(l, axis=1, keepdims=True)

        _, acc, _ = lax.while_loop(lambda carry: _sb_alive(carry[0], i, carry[2]), step,
                                   (jnp.int32(0), jnp.zeros((blk, HEAD_DIM), F32), jnp.zeros((blk, 1), F32)))
        o_ref[...] = acc.astype(BF16)

    qspec = pl.BlockSpec((blk, HEAD_DIM), lambda h, i: (i, h))
    res = _pcall(body, (zc, zc, zc), name=name, grid=(heads, t // blk),
                 in_specs=[qspec, pl.BlockSpec((t, HEAD_DIM), lambda h, i: (0, heads + h)),
                           pl.BlockSpec((t, HEAD_DIM), lambda h, i: (0, 2 * heads + h))],
                 out_specs=[qspec], out_shape=[jax.ShapeDtypeStruct((t, c), BF16)], sem=("parallel", "parallel"), comms=comms)
    return (res[0][0], res[1]) if comms else res[0]


def _sb_bwd(zc, do, name, comms=()):
    t = zc.shape[0]
    c = zc.shape[1] // 3
    heads = c // HEAD_DIM
    blk = min(SB_BLOCK, t)
    scale = HEAD_DIM ** -0.5

    def body(q_ref, k_ref, v_ref, do_ref, dq_ref, dk_ref, dv_ref):
        i = pl.program_id(1)

        @pl.when(i == 0)
        def _():
            dk_ref[...] = jnp.zeros_like(dk_ref)
            dv_ref[...] = jnp.zeros_like(dv_ref)

        q = q_ref[...]
        dov = do_ref[...]
        rows = lax.broadcasted_iota(jnp.int32, (blk, blk), 0)
        cols = lax.broadcasted_iota(jnp.int32, (blk, blk), 1)
        m_upto = (rows <= cols).astype(BF16)
        m_left = (rows < cols).astype(BF16)

        def scan(carry):
            s, c_run = carry
            j = i - s
            l = _sb_log1m(q, k_ref[pl.ds(pl.multiple_of(j * blk, blk), blk), :], i, j)[2]
            return s + 1, c_run + jnp.sum(l, axis=1, keepdims=True)

        n_blocks, c_tot = lax.while_loop(lambda carry: _sb_alive(carry[0], i, carry[1]), scan,
                                         (jnp.int32(0), jnp.zeros((blk, 1), F32)))

        def step(j, carry):
            dq, l_run, w_run = carry
            off = pl.multiple_of(j * blk, blk)
            kj = k_ref[pl.ds(off, blk), :]
            vj = v_ref[pl.ds(off, blk), :]
            zt, e, l, causal = _sb_log1m(q, kj, i, j)
            beta = _sb_beta(zt, e)
            a = jnp.where(causal, beta * jnp.exp(c_tot - l_run - _split_dot(l, m_upto)), 0.0)
            wgt = a * _dot(dov, vj, NT)
            before = w_run + _split_dot(wgt, m_left)
            dz = jnp.where(causal, wgt * (1.0 - beta) - beta * before, 0.0) * scale
            dz16 = dz.astype(BF16)
            dk_ref[pl.ds(off, blk), :] += _dot(dz16, q, TN)
            dv_ref[pl.ds(off, blk), :] += _dot(a.astype(BF16), dov, TN)
            return (dq + _dot(dz16, kj, NN), l_run + jnp.sum(l, axis=1, keepdims=True),
                    w_run + jnp.sum(wgt, axis=1, keepdims=True))

        zero = jnp.zeros((blk, 1), F32)
        dq, _, _ = lax.fori_loop(i + 1 - n_blocks, i + 1, step, (jnp.zeros((blk, HEAD_DIM), F32), zero, zero))
        dq_ref[...] = dq

    qspec = pl.BlockSpec((blk, HEAD_DIM), lambda h, i: (i, h))
    full = pl.BlockSpec((t, HEAD_DIM), lambda h, i: (0, h))
    return _pcall(body, (zc, zc, zc, do), name=name, grid=(heads, t // blk),
                  in_specs=[qspec, pl.BlockSpec((t, HEAD_DIM), lambda h, i: (0, heads + h)),
                            pl.BlockSpec((t, HEAD_DIM), lambda h, i: (0, 2 * heads + h)), qspec],
                  out_specs=[qspec, full, full], out_shape=[jax.ShapeDtypeStruct((t, c), F32)] * 3,
                  sem=("arbitrary", "arbitrary"), comms=comms)


def _concat_bf16(parts, name):
    t, c = parts[0].shape

    def body(*refs):
        for k, r in enumerate(refs[:-1]):
            refs[-1][:, k * c:(k + 1) * c] = r[...].astype(BF16)

    return pl.pallas_call(
        body, name=name, grid=(t // NORM_ROWS,), in_specs=[_row_spec(c)] * len(parts), out_specs=_row_spec(c * len(parts)),
        out_shape=jax.ShapeDtypeStruct((t, c * len(parts)), BF16), compiler_params=_params("parallel"),
    )(*parts)


KIND = {"ab_w_in": "col", "ab_w_out": "row", "sb_w_in": "col", "sb_w_out": "row",
        "ffn_w1_0": "col", "ffn_w1_1": "col", "ffn_w2_0": "row", "ffn_w2_1": "row"}
PAT_A, PAT_B, PAT_ALL = (1, 2, 4, 6), (3, 5, 7), (1, 2, 3, 4, 5, 6, 7)


def _local_step(x, target, norms, sgu, big, got=None):
    g = {k: [v[l:l + 1] for l in range(2)] for k, v in norms.items()}
    ln_g, ln_b, sgu_w, sgu_b = sgu
    groups = sgu_w.shape[0]
    w16 = sgu_w.astype(BF16)
    bias_b = jnp.broadcast_to(sgu_b[:, :, None], (groups, CHUNK, CHUNK))
    big, dws, dist = dict(big), {}, got is not None
    got = dict(got) if dist else {}

    def run(fn, *args, gather=(), scatter=(), **kw):
        if not dist or not (gather or scatter):
            return fn(*args, **kw)
        comms = [_Gather(big[k], KIND[k]) for k in gather] + [_Scatter(dws[k], got[k], KIND[k], pat) for k, pat in scatter]
        out, rws = fn(*args, comms=comms, **kw)
        for k, r in zip(gather, rws):
            big[k] = r[0]
        for (k, _), r in zip(scatter, rws[len(gather):]):
            got[k] = r[0]
        return out

    h1_0 = _rms_fwd(x, g["pre_mix"][0], "rms_in")
    z0 = run(_matmul, h1_0, big["ab_w_in"], "nn", BF16, "ab_in", gather=("ffn_w1_0",))
    a_out = _sgu_fwd(z0, ln_g, ln_b, w16, bias_b, "sgu_fwd")
    branches = [_dil_fwd(z0, d, f"dil_fwd_{d}") for d in DILATIONS]
    ab, ltot = _dil_merge(a_out, [b[0] for b in branches], [b[1] for b in branches], "dil_merge")
    y_0 = _matmul(ab, big["ab_w_out"], "nn", F32, "ab_out")
    x1, h2_0 = _post_pre_fwd(y_0, g["post_mix"][0], x, g["pre_ffn"][0], "norm_mix0")
    r_0 = run(_matmul, h2_0, big["ffn_w1_0"], "nn", BF16, "ffn_up_0", relu_out=True, gather=("ffn_w2_0",))
    y2_0 = run(_matmul, r_0, big["ffn_w2_0"], "nn", F32, "ffn_down_0", a_square=True, gather=("sb_w_in",))
    x2, h1_1 = _post_pre_fwd(y2_0, g["post_ffn"][0], x1, g["pre_mix"][1], "norm_ffn0")
    zc = run(_matmul, h1_1, big["sb_w_in"], "nn", BF16, "sb_in", gather=("sb_w_out",))
    o_sb = run(_sb_fwd, zc, "sb_fwd", gather=("ffn_w1_1",))
    y_1 = _matmul(o_sb, big["sb_w_out"], "nn", F32, "sb_out")
    x3, h2_1 = _post_pre_fwd(y_1, g["post_mix"][1], x2, g["pre_ffn"][1], "norm_mix1")
    r_1 = run(_matmul, h2_1, big["ffn_w1_1"], "nn", BF16, "ffn_up_1", relu_out=True, gather=("ffn_w2_1",))
    y2_1 = _matmul(r_1, big["ffn_w2_1"], "nn", F32, "ffn_down_1", a_square=True)
    loss, dx4, dy2_1, dg_post_ffn1 = _final_fwd_bwd(y2_1, g["post_ffn"][1], x3, target, "loss")

    da = _matmul(dy2_1, big["ffn_w2_1"], "nt", BF16, "ffn_da_1", mul2=r_1)
    dws["ffn_w2_1"] = _matmul(r_1, dy2_1, "tn", BF16, "ffn_dw2_1", a_square=True)
    dh2 = run(_matmul, da, big["ffn_w1_1"], "nt", F32, "ffn_dh_1", scatter=(("ffn_w2_1", PAT_A),))
    dws["ffn_w1_1"] = run(_matmul, h2_1, da, "tn", BF16, "ffn_dw1_1", scatter=(("ffn_w2_1", PAT_B),))
    dx3, dy_1, dg_pre_ffn1, dg_post_mix1 = _pre_post_bwd(x3, g["pre_ffn"][1], dh2, dx4, y_1, g["post_mix"][1], "norm_bwd_mix1")
    do_sb = _matmul(dy_1, big["sb_w_out"], "nt", BF16, "sb_out_dx")
    dws["sb_w_out"] = _matmul(o_sb, dy_1, "tn", BF16, "sb_out_dw")
    dqkv = run(_sb_bwd, zc, do_sb, "sb_bwd", scatter=(("ffn_w1_1", PAT_ALL), ("sb_w_out", PAT_ALL)))
    dzc = _concat_bf16(dqkv, "sb_dz")
    dh1 = _matmul(dzc, big["sb_w_in"], "nt", F32, "sb_in_dx")
    dws["sb_w_in"] = _matmul(h1_1, dzc, "tn", BF16, "sb_in_dw")
    dx2, dy2_0, dg_pre_mix1, dg_post_ffn0 = _pre_post_bwd(x2, g["pre_mix"][1], dh1, dx3, y2_0, g["post_ffn"][0], "norm_bwd_ffn0")
    da = run(_matmul, dy2_0, big["ffn_w2_0"], "nt", BF16, "ffn_da_0", mul2=r_0, scatter=(("sb_w_in", PAT_A),))
    dws["ffn_w2_0"] = run(_matmul, r_0, dy2_0, "tn", BF16, "ffn_dw2_0", a_square=True, scatter=(("sb_w_in", PAT_B),))
    dh2 = run(_matmul, da, big["ffn_w1_0"], "nt", F32, "ffn_dh_0", scatter=(("ffn_w2_0", PAT_A),))
    dws["ffn_w1_0"] = run(_matmul, h2_0, da, "tn", BF16, "ffn_dw1_0", scatter=(("ffn_w2_0", PAT_B),))
    dx1, dy_0, dg_pre_ffn0, dg_post_mix0 = _pre_post_bwd(x1, g["pre_ffn"][0], dh2, dx2, y_0, g["post_mix"][0], "norm_bwd_mix0")
    dab = _matmul(dy_0, big["ab_w_out"], "nt", BF16, "ab_out_dx")
    dws["ab_w_out"] = _matmul(ab, dy_0, "tn", BF16, "ab_out_dw")
    duv, d_ln_g, d_ln_b, d_sgu_w, d_sgu_b = _sgu_bwd(z0, dab, ln_g, ln_b, w16, bias_b, "sgu_bwd")
    parts = [run(_dil_bwd, z0, ab, dab, ltot, 1, "dil_bwd_1", scatter=(("ffn_w1_0", PAT_A),)),
             run(_dil_bwd, z0, ab, dab, ltot, 4, "dil_bwd_4", scatter=(("ffn_w1_0", PAT_B),)),
             run(_dil_bwd, z0, ab, dab, ltot, 16, "dil_bwd_16", scatter=(("ab_w_out", PAT_ALL),))]
    dz0 = _dz_assemble(duv, parts, "dz_assemble")
    dws["ab_w_in"] = _matmul(h1_0, dz0, "tn", BF16, "ab_in_dw")
    dh1 = run(_matmul, dz0, big["ab_w_in"], "nt", F32, "ab_in_dx", scatter=(("ab_w_in", PAT_A),))
    grad_x, dg_pre_mix0 = run(_pre_post_bwd, x, g["pre_mix"][0], dh1, dx1, None, None, "norm_bwd_in", scatter=(("ab_w_in", PAT_B),))

    d_norms = {
        "pre_mix": jnp.concatenate([dg_pre_mix0, dg_pre_mix1]), "post_mix": jnp.concatenate([dg_post_mix0, dg_post_mix1]),
        "pre_ffn": jnp.concatenate([dg_pre_ffn0, dg_pre_ffn1]), "post_ffn": jnp.concatenate([dg_post_ffn0, dg_post_ffn1]),
    }
    return loss, grad_x, d_norms, (d_ln_g, d_ln_b, d_sgu_w, d_sgu_b), dws, got


def _to_bf16_full(w, layer, kind, name):
    _, rows, cols = w.shape
    tr = _tile(rows, 512)
    nblk = rows // tr
    full = (rows, 4 * cols) if kind == "col" else (4 * rows, cols)

    def body(w_ref, o_ref):
        o_ref[...] = w_ref[...].astype(BF16)

    def place(i):
        mine = 2 * lax.axis_index("x") + lax.axis_index("y")
        return (i, mine) if kind == "col" else (mine * nblk + i, 0)

    return pl.pallas_call(
        body, name=name, grid=(nblk,), in_specs=[pl.BlockSpec((None, tr, cols), lambda i: (layer, i, 0))],
        out_specs=pl.BlockSpec((tr, cols), place), out_shape=jax.ShapeDtypeStruct(full, BF16), compiler_params=_params("parallel"),
    )(w)


def _owner_sum(dw16, got, buf, layer, kind, name):
    rh, cs = _half_shape(dw16.shape, kind)
    tr = _tile(rh, 128)
    nblk = rh // tr

    def body(dw_ref, got_ref, buf_ref, o_ref):
        tot = dw_ref[...].astype(F32)
        for j in range(7):
            tot = tot + got_ref[j].astype(F32)
        o_ref[...] = tot

    def own(i):
        x, y, c, mine = _place()
        return (c * nblk + i, mine) if kind == "col" else ((2 * mine + c) * nblk + i, 0)

    return pl.pallas_call(
        body, name=name, grid=(nblk,),
        in_specs=[pl.BlockSpec((tr, cs), own), pl.BlockSpec((7, tr, cs), lambda i: (0, i, 0)), ANY],
        out_specs=pl.BlockSpec((None, None, tr, cs), lambda i: (layer, lax.axis_index("c"), i, 0)),
        out_shape=jax.ShapeDtypeStruct(buf.shape, F32), input_output_aliases={2: 0}, compiler_params=_params("parallel"),
    )(dw16, got, buf)


def _adamw_math(w, g, m, v):
    m = ADAM_B1 * m + (1.0 - ADAM_B1) * g
    v = ADAM_B2 * v + (1.0 - ADAM_B2) * (g * g)
    m_hat = m / (1.0 - ADAM_B1 ** ADAM_STEP)
    v_hat = v / (1.0 - ADAM_B2 ** ADAM_STEP)
    return -ADAM_LR * (m_hat / (jnp.sqrt(v_hat) + ADAM_EPS) + ADAM_WD * w), m, v


def _adamw(w, g, m, v, name):
    layers, rows, cols = w.shape
    tr = _tile(rows, 256)

    def body(w_ref, g_ref, m_ref, v_ref, d_ref, mo_ref, vo_ref):
        d_ref[...], mo_ref[...], vo_ref[...] = _adamw_math(w_ref[...], g_ref[...], m_ref[...], v_ref[...])

    spec = pl.BlockSpec((None, tr, cols), lambda l, i: (l, i, 0))
    return pl.pallas_call(body, name=name, grid=(layers, rows // tr), in_specs=[spec] * 4, out_specs=[spec] * 3,
                          out_shape=[jax.ShapeDtypeStruct(w.shape, F32)] * 3, compiler_params=_params("parallel", "parallel"))(w, g, m, v)


def _pack(arrays):
    flat = jnp.concatenate([a.reshape(-1) for a in arrays])
    pad = (-flat.shape[0]) % 1024
    return jnp.pad(flat, (0, pad)).reshape(-1, 128)


def _unpack(packed, like):
    flat = packed.reshape(-1)
    out, off = [], 0
    for a in like:
        out.append(flat[off:off + a.size].reshape(a.shape))
        off += a.size
    return out


def _gather_small(g, name):
    rows = g.shape[0]

    def body(g_ref, o_ref, send, recv, local_sem):
        x, y, c, _ = _place()
        me = 4 * x + 2 * y + c
        local = pltpu.make_async_copy(g_ref, o_ref.at[me], local_sem)
        local.start()
        copies = []
        for j in range(1, 8):
            px = 1 - x if j & 4 else x
            py = 1 - y if j & 2 else y
            pc = 1 - c if j & 1 else c
            copies.append(pltpu.make_async_remote_copy(src_ref=g_ref, dst_ref=o_ref.at[me], send_sem=send.at[j - 1],
                                                       recv_sem=recv.at[j - 1], device_id=(px, py, pc), device_id_type=MESH))
        for cp in copies:
            cp.start()
        for j in range(1, 8):
            px = 1 - x if j & 4 else x
            py = 1 - y if j & 2 else y
            pc = 1 - c if j & 1 else c
            pltpu.make_async_remote_copy(src_ref=g_ref, dst_ref=o_ref.at[4 * px + 2 * py + pc], send_sem=send.at[j - 1],
                                         recv_sem=recv.at[j - 1], device_id=(px, py, pc), device_id_type=MESH).wait_recv()
        for cp in copies:
            cp.wait_send()
        local.wait()

    vmem = pl.BlockSpec(memory_space=pltpu.VMEM)
    return pl.pallas_call(
        body, name=name, in_specs=[vmem], out_specs=vmem, out_shape=jax.ShapeDtypeStruct((8, rows, 128), F32),
        scratch_shapes=[pltpu.SemaphoreType.DMA((7,)), pltpu.SemaphoreType.DMA((7,)), pltpu.SemaphoreType.DMA(())],
        compiler_params=_params(),
    )(g)


def _small_update(parts, w, m, v, name):
    rows = w.shape[0]

    def body(p_ref, w_ref, m_ref, v_ref, g_ref, d_ref, mo_ref, vo_ref):
        g = p_ref[0]
        for k in range(1, 8):
            g = g + p_ref[k]
        g_ref[...] = g
        d_ref[...], mo_ref[...], vo_ref[...] = _adamw_math(w_ref[...], g, m_ref[...], v_ref[...])

    return pl.pallas_call(body, name=name, out_shape=[jax.ShapeDtypeStruct((rows, 128), F32)] * 4, compiler_params=_params())(parts, w, m, v)


SMALL = ("norm_pre_mix", "norm_post_mix", "norm_pre_ffn", "norm_post_ffn", "sgu_ln_g", "sgu_ln_b", "sgu_w", "sgu_b")
BIG = (("ab_w_in", ("ab_w_in",)), ("ab_w_out", ("ab_w_out",)), ("sb_w_in", ("sb_w_in",)), ("sb_w_out", ("sb_w_out",)),
       ("ffn_w1", ("ffn_w1_0", "ffn_w1_1")), ("ffn_w2", ("ffn_w2_0", "ffn_w2_1")))
WEIGHTS = ("norm_pre_mix", "norm_post_mix", "norm_pre_ffn", "norm_post_ffn", "ab_w_in", "sgu_ln_g", "sgu_ln_b", "sgu_w", "sgu_b",
           "ab_w_out", "sb_w_in", "sb_w_out", "ffn_w1", "ffn_w2")


def kernel(x, norm_pre_mix, norm_post_mix, norm_pre_ffn, norm_post_ffn, ab_w_in, sgu_ln_g, sgu_ln_b, sgu_w, sgu_b, ab_w_out, sb_w_in, sb_w_out, ffn_w1, ffn_w2, loss_target, m_norm_pre_mix, m_norm_post_mix, m_norm_pre_ffn, m_norm_post_ffn, m_ab_w_in, m_sgu_ln_g, m_sgu_ln_b, m_sgu_w, m_sgu_b, m_ab_w_out, m_sb_w_in, m_sb_w_out, m_ffn_w1, m_ffn_w2, v_norm_pre_mix, v_norm_post_mix, v_norm_pre_ffn, v_norm_post_ffn, v_ab_w_in, v_sgu_ln_g, v_sgu_ln_b, v_sgu_w, v_sgu_b, v_ab_w_out, v_sb_w_in, v_sb_w_out, v_ffn_w1, v_ffn_w2):
    w = dict(norm_pre_mix=norm_pre_mix, norm_post_mix=norm_post_mix, norm_pre_ffn=norm_pre_ffn, norm_post_ffn=norm_post_ffn,
             ab_w_in=ab_w_in, sgu_ln_g=sgu_ln_g, sgu_ln_b=sgu_ln_b, sgu_w=sgu_w, sgu_b=sgu_b, ab_w_out=ab_w_out, sb_w_in=sb_w_in,
             sb_w_out=sb_w_out, ffn_w1=ffn_w1, ffn_w2=ffn_w2)
    m = dict(norm_pre_mix=m_norm_pre_mix, norm_post_mix=m_norm_post_mix, norm_pre_ffn=m_norm_pre_ffn, norm_post_ffn=m_norm_post_ffn,
             ab_w_in=m_ab_w_in, sgu_ln_g=m_sgu_ln_g, sgu_ln_b=m_sgu_ln_b, sgu_w=m_sgu_w, sgu_b=m_sgu_b, ab_w_out=m_ab_w_out,
             sb_w_in=m_sb_w_in, sb_w_out=m_sb_w_out, ffn_w1=m_ffn_w1, ffn_w2=m_ffn_w2)
    v = dict(norm_pre_mix=v_norm_pre_mix, norm_post_mix=v_norm_post_mix, norm_pre_ffn=v_norm_pre_ffn, norm_post_ffn=v_norm_post_ffn,
             ab_w_in=v_ab_w_in, sgu_ln_g=v_sgu_ln_g, sgu_ln_b=v_sgu_ln_b, sgu_w=v_sgu_w, sgu_b=v_sgu_b, ab_w_out=v_ab_w_out,
             sb_w_in=v_sb_w_in, sb_w_out=v_sb_w_out, ffn_w1=v_ffn_w1, ffn_w2=v_ffn_w2)
    big, got = {}, {}
    for name, keys in BIG:
        for layer, key in enumerate(keys):
            big[key] = _to_bf16_full(w[name], layer, KIND[key], f"bf16_{key}")
            got[key] = lax.empty((7,) + _half_shape(big[key].shape, KIND[key]), BF16)
    first = _comm_call([_Gather(big["ab_w_in"], KIND["ab_w_in"]), _Gather(big["ab_w_out"], KIND["ab_w_out"])], "gather_first")
    big["ab_w_in"], big["ab_w_out"] = first[0][0], first[1][0]

    norms = {k: w["norm_" + k] for k in ("pre_mix", "post_mix", "pre_ffn", "post_ffn")}
    sgu = (sgu_ln_g, sgu_ln_b, sgu_w[0], sgu_b[0])
    loss_blk, grad_x, d_norms, d_sgu, dws, got = _local_step(x[0], loss_target[0], norms, sgu, big, got)
    loss = lax.psum(loss_blk[0, 0], ("x", "y", "c"))

    grads, deltas, new_m, new_v = {}, {}, {}, {}
    small_g = [d_norms["pre_mix"], d_norms["post_mix"], d_norms["pre_ffn"], d_norms["post_ffn"],
               d_sgu[0], d_sgu[1], d_sgu[2][None], d_sgu[3][None]]
    gathered = _gather_small(_pack(small_g), "gather_small")
    outs = _small_update(gathered, _pack([w[k] for k in SMALL]), _pack([m[k] for k in SMALL]), _pack([v[k] for k in SMALL]), "small_update")
    like = [w[k] for k in SMALL]
    for dst, packed in zip((grads, deltas, new_m, new_v), outs):
        for k, a in zip(SMALL, _unpack(packed, like)):
            dst[k] = a

    bufs = []
    for name, keys in BIG:
        buf = lax.empty((len(keys), 2) + _half_shape(dws[keys[0]].shape, KIND[keys[0]]), F32)
        for layer, key in enumerate(keys):
            buf = _owner_sum(dws[key], got[key], buf, layer, KIND[key], f"sum_{key}")
        bufs.append(buf)
    joined = _comm_call([_Join(bufs)], "join")[0]
    for (name, _), buf in zip(BIG, joined):
        grads[name] = buf.reshape(w[name].shape)
        deltas[name], new_m[name], new_v[name] = _adamw(w[name], grads[name], m[name], v[name], f"adamw_{name}")

    return (loss, grad_x[None], *[grads[k] for k in WEIGHTS], *[deltas[k] for k in WEIGHTS],
            *[new_m[k] for k in WEIGHTS], *[new_v[k] for k in WEIGHTS])
```

```python
import functools

import jax
import jax.numpy as jnp
from jax import lax
from jax.experimental import pallas as pl
from jax.experimental.pallas import tpu as pltpu

F32 = jnp.float32
BF16 = jnp.bfloat16
MESH = pl.DeviceIdType.MESH

HEAD_DIM = 128
CHUNK = 128
DILATIONS = (1, 4, 16)
SB_BLOCK = 256
RMS_EPS = 1e-6
LN_EPS = 1e-5
ADAM_LR, ADAM_B1, ADAM_B2, ADAM_EPS, ADAM_WD, ADAM_STEP = 0.001, 0.9, 0.999, 1e-08, 0.01, 10
NEG = -1e30
V7X_VMEM_LIMIT = 48 * 1024 * 1024
ANY = pl.BlockSpec(memory_space=pl.ANY)


def _params(*sem):
    return pltpu.CompilerParams(dimension_semantics=sem if sem else None, vmem_limit_bytes=V7X_VMEM_LIMIT)


def _tile(n, pref):
    if n <= pref:
        return n
    t = pref
    while n % t:
        t -= 128
    return t


def _dot(a, b, dims):
    return lax.dot_general(a, b, (dims, ((), ())), preferred_element_type=F32)


NN = ((1,), (0,))
NT = ((1,), (1,))
TN = ((0,), (0,))


def _place():
    x, y, c = lax.axis_index("x"), lax.axis_index("y"), lax.axis_index("c")
    return x, y, c, 2 * x + y


def _flip(x, y, c, j):
    return (1 - x if j & 4 else x), (1 - y if j & 2 else y), (1 - c if j & 1 else c)


def _half_shape(full_shape, kind):
    rows, cols = full_shape
    return (rows // 2, cols // 4) if kind == "col" else (rows // 8, cols)


def _half(ref, kind, s, h):
    rh, cs = _half_shape(ref.shape, kind)
    if kind == "col":
        return ref.at[pl.ds(h * rh, rh), pl.ds(s * cs, cs)]
    return ref.at[pl.ds((2 * s + h) * rh, rh), :]


def _remote(src, dst, send, recv, to):
    return pltpu.make_async_remote_copy(src_ref=src, dst_ref=dst, send_sem=send, recv_sem=recv, device_id=to, device_id_type=MESH)


class _Gather:
    n_sems = 6

    def __init__(self, full, kind):
        self.ro, self.rw, self.kind = [], [full], kind

    def start(self, ro, rw, send, recv):
        x, y, c, mine = _place()
        own = _half(rw[0], self.kind, mine, c)
        for k, j in enumerate((2, 4, 6)):
            px, py, _ = _flip(x, y, c, j)
            _remote(own, own, send(k), recv(k), (px, py, c)).start()

    def finish(self, ro, rw, send, recv):
        x, y, c, mine = _place()
        own = _half(rw[0], self.kind, mine, c)
        for k, j in enumerate((2, 4, 6)):
            px, py, _ = _flip(x, y, c, j)
            got = _half(rw[0], self.kind, 2 * px + py, c)
            _remote(got, got, send(k), recv(k), (x, y, c)).wait_recv()
            _remote(got, got, send(3 + k), recv(3 + k), (x, y, 1 - c)).start()
        for k, j in enumerate((2, 4, 6)):
            px, py, _ = _flip(x, y, c, j)
            got = _half(rw[0], self.kind, 2 * px + py, 1 - c)
            _remote(got, got, send(3 + k), recv(3 + k), (x, y, c)).wait_recv()
        for k in range(6):
            _remote(own, own, send(k), recv(k), (x, y, c)).wait_send()


class _Scatter:
    def __init__(self, dw16, got, kind, patterns):
        self.ro, self.rw, self.kind, self.patterns, self.n_sems = [dw16], [got], kind, patterns, len(patterns)

    def start(self, ro, rw, send, recv):
        x, y, c, _ = _place()
        for k, j in enumerate(self.patterns):
            px, py, pc = _flip(x, y, c, j)
            _remote(_half(ro[0], self.kind, 2 * px + py, pc), rw[0].at[j - 1], send(k), recv(k), (px, py, pc)).start()

    def finish(self, ro, rw, send, recv):
        x, y, c, _ = _place()
        for k, j in enumerate(self.patterns):
            slot = rw[0].at[j - 1]
            cp = _remote(slot, slot, send(k), recv(k), (x, y, c))
            cp.wait_recv()
            cp.wait_send()


class _Join:
    def __init__(self, bufs):
        self.ro, self.rw, self.n_sems = [], list(bufs), sum(b.shape[0] for b in bufs)

    def _copies(self, rw, send, recv, slot):
        x, y, c, _ = _place()
        k = 0
        for ref in rw:
            for l in range(ref.shape[0]):
                yield _remote(ref.at[l, c], ref.at[l, slot(c)], send(k), recv(k), (x, y, 1 - c))
                k += 1

    def start(self, ro, rw, send, recv):
        for cp in self._copies(rw, send, recv, lambda c: c):
            cp.start()

    def finish(self, ro, rw, send, recv):
        for cp in self._copies(rw, send, recv, lambda c: 1 - c):
            cp.wait_recv()
        for cp in self._copies(rw, send, recv, lambda c: c):
            cp.wait_send()


def _comm_layout(comms):
    ro = [a for c in comms for a in c.ro]
    rw = [a for c in comms for a in c.rw]
    return ro, rw, sum(c.n_sems for c in comms)


def _comm_each(comms, method, ro_refs, rw_refs, send, recv):
    i_ro = i_rw = i_sem = 0
    for c in comms:
        getattr(c, method)(ro_refs[i_ro:i_ro + len(c.ro)], rw_refs[i_rw:i_rw + len(c.rw)],
                           lambda k, b=i_sem: send.at[b + k], lambda k, b=i_sem: recv.at[b + k])
        i_ro, i_rw, i_sem = i_ro + len(c.ro), i_rw + len(c.rw), i_sem + c.n_sems


def _split_results(comms, rws):
    out, i = [], 0
    for c in comms:
        out.append(list(rws[i:i + len(c.rw)]))
        i += len(c.rw)
    return out


def _comm_call(comms, name):
    ro, rw, n_sems = _comm_layout(comms)

    def body(*refs):
        ro_refs = refs[:len(ro)]
        rw_refs = refs[len(ro) + len(rw):len(ro) + 2 * len(rw)]
        send, recv = refs[len(ro) + 2 * len(rw):]
        _comm_each(comms, "start", ro_refs, rw_refs, send, recv)
        _comm_each(comms, "finish", ro_refs, rw_refs, send, recv)

    rws = pl.pallas_call(
        body, name=name, in_specs=[ANY] * (len(ro) + len(rw)), out_specs=[ANY] * len(rw),
        out_shape=[jax.ShapeDtypeStruct(a.shape, a.dtype) for a in rw],
        input_output_aliases={len(ro) + k: k for k in range(len(rw))},
        scratch_shapes=[pltpu.SemaphoreType.DMA((n_sems,)), pltpu.SemaphoreType.DMA((n_sems,))],
    )(*ro, *rw)
    return _split_results(comms, rws)


def _pcall(body, args, *, name, grid, in_specs, out_specs, out_shape, scratch=(), sem=(), comms=()):
    n_in, n_out, n_scr = len(in_specs), len(out_specs), len(scratch)
    if not comms:
        return pl.pallas_call(body, name=name, grid=grid, in_specs=list(in_specs), out_specs=list(out_specs),
                              out_shape=list(out_shape), scratch_shapes=list(scratch), compiler_params=_params(*sem))(*args)
    ro, rw, n_sems = _comm_layout(comms)

    def carrier(*refs):
        ins = refs[:n_in]
        ro_refs = refs[n_in:n_in + len(ro)]
        o0 = n_in + len(ro) + len(rw)
        outs = refs[o0:o0 + n_out]
        rw_refs = refs[o0 + n_out:o0 + n_out + len(rw)]
        s0 = o0 + n_out + len(rw)
        send, recv = refs[s0 + n_scr], refs[s0 + n_scr + 1]
        ids = [pl.program_id(a) for a in range(len(grid))]
        first = functools.reduce(jnp.logical_and, [i == 0 for i in ids])
        last = functools.reduce(jnp.logical_and, [i == g - 1 for i, g in zip(ids, grid)])

        @pl.when(first)
        def _():
            _comm_each(comms, "start", ro_refs, rw_refs, send, recv)

        body(*ins, *outs, *refs[s0:s0 + n_scr])

        @pl.when(last)
        def _():
            _comm_each(comms, "finish", ro_refs, rw_refs, send, recv)

    res = pl.pallas_call(
        carrier, name=name, grid=grid, in_specs=list(in_specs) + [ANY] * (len(ro) + len(rw)),
        out_specs=list(out_specs) + [ANY] * len(rw),
        out_shape=list(out_shape) + [jax.ShapeDtypeStruct(a.shape, a.dtype) for a in rw],
        input_output_aliases={n_in + len(ro) + k: n_out + k for k in range(len(rw))},
        scratch_shapes=list(scratch) + [pltpu.SemaphoreType.DMA((n_sems,)), pltpu.SemaphoreType.DMA((n_sems,))],
        compiler_params=_params(*["arbitrary"] * len(grid)),
    )(*args, *ro, *rw)
    return list(res[:n_out]), _split_results(comms, res[n_out:])


def _matmul(a, b, mode, out_dtype, name, a_square=False, relu_out=False, mul2=None, comms=()):
    if mode == "nn":
        (m, k), n = a.shape, b.shape[1]
    elif mode == "nt":
        (m, k), n = a.shape, b.shape[0]
    else:
        (k, m), n = a.shape, b.shape[1]
    tm, tn, tk = _tile(m, 1024), _tile(n, 1024), _tile(k, 2048)
    nk = k // tk
    dims = {"nn": NN, "nt": NT, "tn": TN}[mode]
    a_spec = pl.BlockSpec((tk, tm), lambda i, j, kk: (kk, i)) if mode == "tn" else pl.BlockSpec((tm, tk), lambda i, j, kk: (i, kk))
    b_spec = pl.BlockSpec((tn, tk), lambda i, j, kk: (j, kk)) if mode == "nt" else pl.BlockSpec((tk, tn), lambda i, j, kk: (kk, j))
    o_spec = pl.BlockSpec((tm, tn), lambda i, j, kk: (i, j))

    def body(a_ref, b_ref, *rest):
        m_ref = None if mul2 is None else rest[0]
        o_ref = rest[0 if mul2 is None else 1]
        kk = pl.program_id(2)

        def partial():
            av = a_ref[...]
            if a_square:
                av = av * av
            return _dot(av, b_ref[...], dims)

        def finish(r):
            if relu_out:
                r = jnp.maximum(r, 0.0)
            if mul2 is not None:
                r = r * (2.0 * m_ref[...].astype(F32))
            o_ref[...] = r.astype(out_dtype)

        if nk == 1:
            finish(partial())
            return
        acc_ref = rest[-1]

        @pl.when(kk == 0)
        def _():
            acc_ref[...] = partial()

        @pl.when(kk > 0)
        def _():
            acc_ref[...] += partial()

        @pl.when(kk == nk - 1)
        def _():
            finish(acc_ref[...])

    args = (a, b) if mul2 is None else (a, b, mul2)
    specs = [a_spec, b_spec] + ([] if mul2 is None else [o_spec])
    res = _pcall(body, args, name=name, grid=(m // tm, n // tn, nk), in_specs=specs, out_specs=[o_spec],
                 out_shape=[jax.ShapeDtypeStruct((m, n), out_dtype)], scratch=[pltpu.VMEM((tm, tn), F32)] if nk > 1 else [],
                 sem=("parallel", "parallel", "arbitrary"), comms=comms)
    return (res[0][0], res[1]) if comms else res[0]


NORM_ROWS = 256


def _rms(x, g):
    rstd = lax.rsqrt(jnp.mean(x * x, axis=-1, keepdims=True) + RMS_EPS)
    n = x * rstd
    return n * g, n, rstd


def _rms_bwd(n, rstd, g, dout):
    dn = dout * g
    return rstd * (dn - n * jnp.mean(dn * n, axis=-1, keepdims=True))


def _row_spec(d):
    return pl.BlockSpec((NORM_ROWS, d), lambda i: (i, 0))


def _vec_spec(d):
    return pl.BlockSpec((1, d), lambda i: (0, 0))


def _accumulate(ref, val):
    @pl.when(pl.program_id(0) == 0)
    def _():
        ref[...] = jnp.zeros_like(ref)

    ref[...] += val


def _rms_fwd(x, g, name):
    t, d = x.shape

    def body(x_ref, g_ref, h_ref):
        h_ref[...] = _rms(x_ref[...], g_ref[...])[0].astype(BF16)

    return pl.pallas_call(
        body, name=name, grid=(t // NORM_ROWS,), in_specs=[_row_spec(d), _vec_spec(d)], out_specs=_row_spec(d),
        out_shape=jax.ShapeDtypeStruct((t, d), BF16), compiler_params=_params("parallel"),
    )(x, g)


def _post_pre_fwd(y, g_post, x, g_pre, name):
    t, d = x.shape

    def body(y_ref, gp_ref, x_ref, gn_ref, xn_ref, h_ref):
        xn = x_ref[...] + _rms(y_ref[...], gp_ref[...])[0]
        xn_ref[...] = xn
        h_ref[...] = _rms(xn, gn_ref[...])[0].astype(BF16)

    return pl.pallas_call(
        body, name=name, grid=(t // NORM_ROWS,),
        in_specs=[_row_spec(d), _vec_spec(d), _row_spec(d), _vec_spec(d)], out_specs=[_row_spec(d), _row_spec(d)],
        out_shape=[jax.ShapeDtypeStruct((t, d), F32), jax.ShapeDtypeStruct((t, d), BF16)], compiler_params=_params("parallel"),
    )(y, g_post, x, g_pre)


def _final_fwd_bwd(y, g_post, x, target, name):
    t, d = x.shape

    def body(y_ref, g_ref, x_ref, t_ref, loss_ref, dx_ref, dy_ref, dg_ref):
        g = g_ref[...]
        out, n, rstd = _rms(y_ref[...], g)
        e = x_ref[...] + out - t_ref[...]
        _accumulate(loss_ref, jnp.full(loss_ref.shape, 0.5 / d, F32) * jnp.sum(e * e))
        dx = e * (1.0 / d)
        dx_ref[...] = dx
        dy_ref[...] = _rms_bwd(n, rstd, g, dx).astype(BF16)
        _accumulate(dg_ref, jnp.sum(dx * n, axis=0, keepdims=True))

    return pl.pallas_call(
        body, name=name, grid=(t // NORM_ROWS,),
        in_specs=[_row_spec(d), _vec_spec(d), _row_spec(d), _row_spec(d)],
        out_specs=[pl.BlockSpec((8, 128), lambda i: (0, 0)), _row_spec(d), _row_spec(d), _vec_spec(d)],
        out_shape=[jax.ShapeDtypeStruct((8, 128), F32), jax.ShapeDtypeStruct((t, d), F32),
                   jax.ShapeDtypeStruct((t, d), BF16), jax.ShapeDtypeStruct((1, d), F32)],
        compiler_params=_params("arbitrary"),
    )(y, g_post, x, target)


def _pre_post_bwd(x, g_pre, dh, dx_in, y, g_post, name, comms=()):
    t, d = x.shape
    both = y is not None

    def body(x_ref, gp_ref, dh_ref, dxi_ref, *rest):
        if both:
            y_ref, gq_ref, dx_ref, dy_ref, dgp_ref, dgq_ref = rest
        else:
            dx_ref, dgp_ref = rest
        gp = gp_ref[...]
        _, n, rstd = _rms(x_ref[...], gp)
        dh_v = dh_ref[...]
        dx = dxi_ref[...] + _rms_bwd(n, rstd, gp, dh_v)
        dx_ref[...] = dx
        _accumulate(dgp_ref, jnp.sum(dh_v * n, axis=0, keepdims=True))
        if both:
            gq = gq_ref[...]
            _, ny, rstdy = _rms(y_ref[...], gq)
            dy_ref[...] = _rms_bwd(ny, rstdy, gq, dx).astype(BF16)
            _accumulate(dgq_ref, jnp.sum(dx * ny, axis=0, keepdims=True))

    in_specs = [_row_spec(d), _vec_spec(d), _row_spec(d), _row_spec(d)]
    args = [x, g_pre, dh, dx_in]
    if both:
        in_specs += [_row_spec(d), _vec_spec(d)]
        args += [y, g_post]
        out_specs = [_row_spec(d), _row_spec(d), _vec_spec(d), _vec_spec(d)]
        out_shape = [jax.ShapeDtypeStruct((t, d), F32), jax.ShapeDtypeStruct((t, d), BF16),
                     jax.ShapeDtypeStruct((1, d), F32), jax.ShapeDtypeStruct((1, d), F32)]
    else:
        out_specs = [_row_spec(d), _vec_spec(d)]
        out_shape = [jax.ShapeDtypeStruct((t, d), F32), jax.ShapeDtypeStruct((1, d), F32)]
    return _pcall(body, args, name=name, grid=(t // NORM_ROWS,), in_specs=in_specs, out_specs=out_specs, out_shape=out_shape,
                  sem=("arbitrary",), comms=comms)


def _gelu(x):
    return 0.5 * x * (1.0 + lax.erf(x * 0.7071067811865476))


def _gelu_grad(x):
    return 0.5 * (1.0 + lax.erf(x * 0.7071067811865476)) + x * jnp.exp(-0.5 * x * x) * 0.3989422804014327


def _layernorm(v, g, b):
    mu = jnp.mean(v, axis=-1, keepdims=True)
    vc = v - mu
    rs = lax.rsqrt(jnp.mean(vc * vc, axis=-1, keepdims=True) + LN_EPS)
    vhat = vc * rs
    return vhat * g + b, vhat, rs


def _tril_mask():
    return lax.broadcasted_iota(jnp.int32, (CHUNK, CHUNK), 0) >= lax.broadcasted_iota(jnp.int32, (CHUNK, CHUNK), 1)


def _sgu_fwd(z, ln_g, ln_b, w16, bias_b, name):
    t = z.shape[0]
    groups = w16.shape[0]
    a = groups * CHUNK

    def body(u_ref, v_ref, g_ref, b_ref, w_ref, bb_ref, o_ref):
        u = _gelu(u_ref[...].astype(F32))
        vn = _layernorm(_gelu(v_ref[...].astype(F32)), g_ref[...], b_ref[...])[0].astype(BF16)
        tril = _tril_mask()
        for g in range(groups):
            sl = slice(g * CHUNK, (g + 1) * CHUNK)
            w = jnp.where(tril, w_ref[g], jnp.zeros((), BF16))
            mixed = _dot(w, vn[:, sl], NN) + bb_ref[g]
            o_ref[:, sl] = (u[:, sl] * mixed).astype(BF16)

    full3 = pl.BlockSpec((groups, CHUNK, CHUNK), lambda c: (0, 0, 0))
    return pl.pallas_call(
        body, name=name, grid=(t // CHUNK,),
        in_specs=[pl.BlockSpec((CHUNK, a), lambda c: (c, 0)), pl.BlockSpec((CHUNK, a), lambda c: (c, 1)),
                  _vec_spec(a), _vec_spec(a), full3, full3],
        out_specs=pl.BlockSpec((CHUNK, a), lambda c: (c, 0)),
        out_shape=jax.ShapeDtypeStruct((t, a), BF16), compiler_params=_params("parallel"),
    )(z, z, ln_g, ln_b, w16, bias_b)


def _sgu_bwd(z, dab, ln_g, ln_b, w16, bias_b, name, comms=()):
    t = z.shape[0]
    groups = w16.shape[0]
    a = groups * CHUNK

    def body(u_ref, v_ref, da_ref, g_ref, b_ref, w_ref, bb_ref, duv_ref, dg_ref, db_ref, dw_ref, dbs_ref, dvn_ref):
        up = u_ref[...].astype(F32)
        vp = v_ref[...].astype(F32)
        u = _gelu(up)
        ln_gain = g_ref[...]
        vn32, vhat, rs = _layernorm(_gelu(vp), ln_gain, b_ref[...])
        vn = vn32.astype(BF16)
        da = da_ref[...].astype(F32)
        tril = _tril_mask()
        ones = jnp.ones((8, CHUNK), F32)

        @pl.when(pl.program_id(0) == 0)
        def _():
            dw_ref[...] = jnp.zeros_like(dw_ref)
            dbs_ref[...] = jnp.zeros_like(dbs_ref)

        for g in range(groups):
            sl = slice(g * CHUNK, (g + 1) * CHUNK)
            w = jnp.where(tril, w_ref[g], jnp.zeros((), BF16))
            mixed = _dot(w, vn[:, sl], NN) + bb_ref[g]
            dmix = da[:, sl] * u[:, sl]
            dmix16 = dmix.astype(BF16)
            duv_ref[:, sl] = (da[:, sl] * mixed * _gelu_grad(up[:, sl])).astype(BF16)
            dvn_ref[:, sl] = _dot(w, dmix16, TN)
            dw_ref[g] += jnp.where(tril, _dot(dmix16, vn[:, sl], NT), 0.0)
            dbs_ref[g:g + 1, :] += lax.dot_general(ones, dmix, (NT, ((), ())), precision=lax.Precision.HIGHEST,
                                                   preferred_element_type=F32)[0:1]
        dvn = dvn_ref[...]
        dvhat = dvn * ln_gain
        dva = rs * (dvhat - jnp.mean(dvhat, axis=-1, keepdims=True) - vhat * jnp.mean(dvhat * vhat, axis=-1, keepdims=True))
        duv_ref[:, a:] = (dva * _gelu_grad(vp)).astype(BF16)
        _accumulate(dg_ref, jnp.sum(dvn * vhat, axis=0, keepdims=True))
        _accumulate(db_ref, jnp.sum(dvn, axis=0, keepdims=True))

    full3 = pl.BlockSpec((groups, CHUNK, CHUNK), lambda c: (0, 0, 0))
    return _pcall(
        body, (z, z, dab, ln_g, ln_b, w16, bias_b), name=name, grid=(t // CHUNK,),
        in_specs=[pl.BlockSpec((CHUNK, a), lambda c: (c, 0)), pl.BlockSpec((CHUNK, a), lambda c: (c, 1)),
                  pl.BlockSpec((CHUNK, a), lambda c: (c, 0)), _vec_spec(a), _vec_spec(a), full3, full3],
        out_specs=[pl.BlockSpec((CHUNK, 2 * a), lambda c: (c, 0)), _vec_spec(a), _vec_spec(a), full3,
                   pl.BlockSpec((groups, CHUNK), lambda c: (0, 0))],
        out_shape=[jax.ShapeDtypeStruct((t, 2 * a), BF16), jax.ShapeDtypeStruct((1, a), F32), jax.ShapeDtypeStruct((1, a), F32),
                   jax.ShapeDtypeStruct((groups, CHUNK, CHUNK), F32), jax.ShapeDtypeStruct((groups, CHUNK), F32)],
        scratch=[pltpu.VMEM((CHUNK, a), F32)], sem=("arbitrary",), comms=comms)


def _dil_masks(d):
    qi = lax.broadcasted_iota(jnp.int32, (CHUNK, CHUNK), 0)
    kj = lax.broadcasted_iota(jnp.int32, (CHUNK, CHUNK), 1)
    dist_c = qi - kj
    return dist_c >= 0, dist_c <= 0, (dist_c * d).astype(F32), ((dist_c + CHUNK) * d).astype(F32)


def _alibi_slope(h, heads):
    return 2.0 ** (-8.0 * (h + 1) / heads)


def _dil_fwd(z, d, name, comms=()):
    t = z.shape[0]
    w = z.shape[1] // 5
    heads = w // HEAD_DIM
    nb = t // d // CHUNK
    scale = HEAD_DIM ** -0.5

    def body(q_ref, kp_ref, kc_ref, vp_ref, vc_ref, o_ref, l_ref):
        ok_c, ok_p0, bias_c, bias_p = _dil_masks(d)
        ok_p = ok_p0 & (pl.program_id(1) > 0)
        for h in range(heads):
            sl = slice(h * HEAD_DIM, (h + 1) * HEAD_DIM)
            slope = _alibi_slope(h, heads)
            q = q_ref[:, sl]
            s_c = jnp.where(ok_c, _dot(q, kc_ref[:, sl], NT) * scale - slope * bias_c, NEG)
            s_p = jnp.where(ok_p, _dot(q, kp_ref[:, sl], NT) * scale - slope * bias_p, NEG)
            m = jnp.maximum(jnp.max(s_c, axis=1, keepdims=True), jnp.max(s_p, axis=1, keepdims=True))
            p_c = jnp.exp(s_c - m)
            p_p = jnp.exp(s_p - m)
            den = jnp.sum(p_c, axis=1, keepdims=True) + jnp.sum(p_p, axis=1, keepdims=True)
            o = _dot(p_c.astype(BF16), vc_ref[:, sl], NN) + _dot(p_p.astype(BF16), vp_ref[:, sl], NN)
            o_ref[:, sl] = o / den
            l_ref[:, sl] = jnp.broadcast_to(m + jnp.log(den), (CHUNK, HEAD_DIM))

    def zspec(col, prev):
        if prev:
            return pl.BlockSpec((CHUNK, w), lambda r, n: (jnp.maximum(n - 1, 0), r * 5 + col))
        return pl.BlockSpec((CHUNK, w), lambda r, n: (n, r * 5 + col))

    ospec = pl.BlockSpec((CHUNK, w), lambda r, n: (n, r))
    zv = z.reshape(t // d, d * 5 * w)
    res = _pcall(
        body, (zv, zv, zv, zv, zv), name=name, grid=(d, nb),
        in_specs=[zspec(2, False), zspec(3, True), zspec(3, False), zspec(4, True), zspec(4, False)],
        out_specs=[ospec, ospec],
        out_shape=[jax.ShapeDtypeStruct((t // d, d * w), F32), jax.ShapeDtypeStruct((t // d, d * w), F32)],
        sem=("parallel", "parallel"), comms=comms)
    (o, lse), rws = res if comms else (res, None)
    outs = (o.reshape(t, w), lse.reshape(t, w))
    return (outs, rws) if comms else outs


def _dil_merge(a_out, outs, lses, name):
    t, a = a_out.shape
    w = outs[0].shape[1]
    nbr = len(outs)

    def body(a_ref, *rest):
        o_refs, l_refs, (ab_ref, lt_ref) = rest[:nbr], rest[nbr:2 * nbr], rest[2 * nbr:]
        ls = [r[...] for r in l_refs]
        m = functools.reduce(jnp.maximum, ls)
        ws = [jnp.exp(l - m) for l in ls]
        tot = functools.reduce(jnp.add, ws)
        mix = functools.reduce(jnp.add, [wt * r[...] for wt, r in zip(ws, o_refs)]) / tot
        ab_ref[:, :a] = a_ref[...]
        ab_ref[:, a:] = mix.astype(BF16)
        lt_ref[...] = m + jnp.log(tot)

    return pl.pallas_call(
        body, name=name, grid=(t // NORM_ROWS,),
        in_specs=[_row_spec(a)] + [_row_spec(w)] * (2 * nbr), out_specs=[_row_spec(a + w), _row_spec(w)],
        out_shape=[jax.ShapeDtypeStruct((t, a + w), BF16), jax.ShapeDtypeStruct((t, w), F32)],
        compiler_params=_params("parallel"),
    )(a_out, *outs, *lses)


def _dil_bwd(z, ab, dab, ltot, d, name, comms=()):
    t = z.shape[0]
    w = z.shape[1] // 5
    heads = w // HEAD_DIM
    nb = t // d // CHUNK
    scale = HEAD_DIM ** -0.5

    def body(q_ref, qn_ref, kp_ref, kc_ref, vp_ref, vc_ref, o_ref, on_ref, do_ref, don_ref, l_ref, ln_ref,
             dq_ref, dk_ref, dv_ref):
        n = pl.program_id(1)
        ok_c, ok_p0, bias_c, bias_p = _dil_masks(d)
        ok_p = ok_p0 & (n > 0)
        ok_n = ok_p0 & (n < nb - 1)
        for h in range(heads):
            sl = slice(h * HEAD_DIM, (h + 1) * HEAD_DIM)
            slope = _alibi_slope(h, heads)
            q, qn, kp, kc, vp, vc = q_ref[:, sl], qn_ref[:, sl], kp_ref[:, sl], kc_ref[:, sl], vp_ref[:, sl], vc_ref[:, sl]
            do, don = do_ref[:, sl], don_ref[:, sl]
            delta = jnp.sum(do.astype(F32) * o_ref[:, sl].astype(F32), axis=1, keepdims=True)
            delta_n = jnp.sum(don.astype(F32) * on_ref[:, sl].astype(F32), axis=1, keepdims=True)
            lt, lt_n = l_ref[:, sl], ln_ref[:, sl]
            p_c = jnp.exp(jnp.where(ok_c, _dot(q, kc, NT) * scale - slope * bias_c, NEG) - lt)
            p_p = jnp.exp(jnp.where(ok_p, _dot(q, kp, NT) * scale - slope * bias_p, NEG) - lt)
            p_n = jnp.exp(jnp.where(ok_n, _dot(qn, kc, NT) * scale - slope * bias_p, NEG) - lt_n)
            ds_c = (p_c * (_dot(do, vc, NT) - delta)).astype(BF16)
            ds_p = (p_p * (_dot(do, vp, NT) - delta)).astype(BF16)
            ds_n = (p_n * (_dot(don, vc, NT) - delta_n)).astype(BF16)
            dq_ref[:, sl] = (_dot(ds_c, kc, NN) + _dot(ds_p, kp, NN)) * scale
            dk_ref[:, sl] = (_dot(ds_c, q, TN) + _dot(ds_n, qn, TN)) * scale
            dv_ref[:, sl] = _dot(p_c.astype(BF16), do, TN) + _dot(p_n.astype(BF16), don, TN)

    def spec(mult, col, shift):
        if shift < 0:
            return pl.BlockSpec((CHUNK, w), lambda r, n: (jnp.maximum(n - 1, 0), r * mult + col))
        if shift > 0:
            return pl.BlockSpec((CHUNK, w), lambda r, n: (jnp.minimum(n + 1, nb - 1), r * mult + col))
        return pl.BlockSpec((CHUNK, w), lambda r, n: (n, r * mult + col))

    zv = z.reshape(t // d, d * 5 * w)
    abv = ab.reshape(t // d, d * 2 * w)
    dabv = dab.reshape(t // d, d * 2 * w)
    lv = ltot.reshape(t // d, d * w)
    ospec = spec(1, 0, 0)
    res = _pcall(
        body, (zv, zv, zv, zv, zv, zv, abv, abv, dabv, dabv, lv, lv), name=name, grid=(d, nb),
        in_specs=[spec(5, 2, 0), spec(5, 2, 1), spec(5, 3, -1), spec(5, 3, 0), spec(5, 4, -1), spec(5, 4, 0),
                  spec(2, 1, 0), spec(2, 1, 1), spec(2, 1, 0), spec(2, 1, 1), spec(1, 0, 0), spec(1, 0, 1)],
        out_specs=[ospec, ospec, ospec], out_shape=[jax.ShapeDtypeStruct((t // d, d * w), F32)] * 3,
        sem=("parallel", "parallel"), comms=comms)
    outs, rws = res if comms else (res, None)
    outs = [o.reshape(t, w) for o in outs]
    return (outs, rws) if comms else outs


def _dz_assemble(duv, parts, name):
    t, a2 = duv.shape
    w = parts[0][0].shape[1]
    nbr = len(parts)

    def body(duv_ref, *rest):
        refs, dz_ref = rest[:-1], rest[-1]
        dz_ref[:, :a2] = duv_ref[...]
        for i in range(3):
            tot = functools.reduce(jnp.add, [refs[b * 3 + i][...] for b in range(nbr)])
            dz_ref[:, a2 + i * w:a2 + (i + 1) * w] = tot.astype(BF16)

    flat = [p for branch in parts for p in branch]
    return pl.pallas_call(
        body, name=name, grid=(t // NORM_ROWS,), in_specs=[_row_spec(a2)] + [_row_spec(w)] * len(flat),
        out_specs=_row_spec(a2 + 3 * w), out_shape=jax.ShapeDtypeStruct((t, a2 + 3 * w), BF16),
        compiler_params=_params("parallel"),
    )(duv, *flat)


def _split_dot(x, m16):
    hi = x.astype(BF16)
    lo = (x - hi.astype(F32)).astype(BF16)
    return _dot(hi, m16, NN) + _dot(lo, m16, NN)


SB_DEAD = -110.0


def _sb_log1m(q, kj, i, j):
    blk = q.shape[0]
    zt = _dot(q, kj, NT) * (HEAD_DIM ** -0.5)
    e = jnp.exp(-jnp.abs(zt))
    rows = lax.broadcasted_iota(jnp.int32, (blk, blk), 0)
    cols = lax.broadcasted_iota(jnp.int32, (blk, blk), 1)
    causal = (j * blk + cols) < (i * blk + rows)
    return zt, e, jnp.where(causal, -(jnp.maximum(zt, 0.0) + jnp.log1p(e)), 0.0), causal


def _sb_beta(zt, e):
    r = 1.0 / (1.0 + e)
    return jnp.where(zt >= 0.0, r, e * r)


def _sb_alive(s, i, c_run):
    return (s <= i) & (jnp.max(c_run) > SB_DEAD)


def _sb_fwd(zc, name, comms=()):
    t = zc.shape[0]
    c = zc.shape[1] // 3
    heads = c // HEAD_DIM
    blk = min(SB_BLOCK, t)

    def body(q_ref, k_ref, v_ref, o_ref, ct_ref, nb_ref):
        i = pl.program_id(1)
        q = q_ref[...]
        rows = lax.broadcasted_iota(jnp.int32, (blk, blk), 0)
        cols = lax.broadcasted_iota(jnp.int32, (blk, blk), 1)
        m_right = (rows > cols).astype(BF16)

        def step(carry):
            s, acc, c_run = carry
            j = i - s
            off = pl.multiple_of(j * blk, blk)
            zt, e, l, causal = _sb_log1m(q, k_ref[pl.ds(off, blk), :], i, j)
            a = jnp.where(causal, _sb_beta(zt, e) * jnp.exp(c_run + _split_dot(l, m_right)), 0.0)
            acc = acc + _dot(a.astype(BF16), v_ref[pl.ds(off, blk), :], NN)
            return s + 1, acc, c_run + jnp.sum(l, axis=1, keepdims=True)

        swept, acc, c_tot = lax.while_loop(lambda carry: _sb_alive(carry[0], i, carry[2]), step,
                                           (jnp.int32(0), jnp.zeros((blk, HEAD_DIM), F32), jnp.zeros((blk, 1), F32)))
        o_ref[...] = acc.astype(BF16)
        ct_ref[...] = jnp.broadcast_to(c_tot, (blk, HEAD_DIM))
        nb_ref[...] = jnp.zeros((blk, HEAD_DIM), F32) + swept.astype(F32)

    qspec = pl.BlockSpec((blk, HEAD_DIM), lambda h, i: (i, h))
    return _pcall(body, (zc, zc, zc), name=name, grid=(heads, t // blk),
                  in_specs=[qspec, pl.BlockSpec((t, HEAD_DIM), lambda h, i: (0, heads + h)),
                            pl.BlockSpec((t, HEAD_DIM), lambda h, i: (0, 2 * heads + h))],
                  out_specs=[qspec, qspec, qspec],
                  out_shape=[jax.ShapeDtypeStruct((t, c), BF16), jax.ShapeDtypeStruct((t, c), F32), jax.ShapeDtypeStruct((t, c), F32)],
                  sem=("parallel", "parallel"), comms=comms)


def _sb_bwd(zc, ctot, swept, do, name, comms=()):
    t = zc.shape[0]
    c = zc.shape[1] // 3
    heads = c // HEAD_DIM
    blk = min(SB_BLOCK, t)
    scale = HEAD_DIM ** -0.5

    def body(q_ref, k_ref, v_ref, ct_ref, nb_ref, do_ref, dq_ref, dk_ref, dv_ref):
        i = pl.program_id(1)

        @pl.when(i == 0)
        def _():
            dk_ref[...] = jnp.zeros_like(dk_ref)
            dv_ref[...] = jnp.zeros_like(dv_ref)

        q = q_ref[...]
        dov = do_ref[...]
        c_tot = ct_ref[:, 0:1]
        n_blocks = jnp.clip(jnp.max(nb_ref[0:8, :]).astype(jnp.int32), 1, i + 1)
        rows = lax.broadcasted_iota(jnp.int32, (blk, blk), 0)
        cols = lax.broadcasted_iota(jnp.int32, (blk, blk), 1)
        m_upto = (rows <= cols).astype(BF16)
        m_left = (rows < cols).astype(BF16)

        def step(j, carry):
            dq, l_run, w_run = carry
            off = pl.multiple_of(j * blk, blk)
            kj = k_ref[pl.ds(off, blk), :]
            vj = v_ref[pl.ds(off, blk), :]
            zt, e, l, causal = _sb_log1m(q, kj, i, j)
            beta = _sb_beta(zt, e)
            a = jnp.where(causal, beta * jnp.exp(c_tot - l_run - _split_dot(l, m_upto)), 0.0)
            wgt = a * _dot(dov, vj, NT)
            before = w_run + _split_dot(wgt, m_left)
            dz = jnp.where(causal, wgt * (1.0 - beta) - beta * before, 0.0) * scale
            dz16 = dz.astype(BF16)
            dk_ref[pl.ds(off, blk), :] += _dot(dz16, q, TN)
            dv_ref[pl.ds(off, blk), :] += _dot(a.astype(BF16), dov, TN)
            return (dq + _dot(dz16, kj, NN), l_run + jnp.sum(l, axis=1, keepdims=True),
                    w_run + jnp.sum(wgt, axis=1, keepdims=True))

        zero = jnp.zeros((blk, 1), F32)
        dq, _, _ = lax.fori_loop(i + 1 - n_blocks, i + 1, step, (jnp.zeros((blk, HEAD_DIM), F32), zero, zero))
        dq_ref[...] = dq

    qspec = pl.BlockSpec((blk, HEAD_DIM), lambda h, i: (i, h))
    full = pl.BlockSpec((t, HEAD_DIM), lambda h, i: (0, h))
    return _pcall(body, (zc, zc, zc, ctot, swept, do), name=name, grid=(heads, t // blk),
                  in_specs=[qspec, pl.BlockSpec((t, HEAD_DIM), lambda h, i: (0, heads + h)),
                            pl.BlockSpec((t, HEAD_DIM), lambda h, i: (0, 2 * heads + h)), qspec, qspec, qspec],
                  out_specs=[qspec, full, full], out_shape=[jax.ShapeDtypeStruct((t, c), F32)] * 3,
                  sem=("arbitrary", "arbitrary"), comms=comms)


def _concat_bf16(parts, name):
    t, c = parts[0].shape

    def body(*refs):
        for k, r in enumerate(refs[:-1]):
            refs[-1][:, k * c:(k + 1) * c] = r[...].astype(BF16)

    return pl.pallas_call(
        body, name=name, grid=(t // NORM_ROWS,), in_specs=[_row_spec(c)] * len(parts), out_specs=_row_spec(c * len(parts)),
        out_shape=jax.ShapeDtypeStruct((t, c * len(parts)), BF16), compiler_params=_params("parallel"),
    )(*parts)


KIND = {"ab_w_in": "col", "ab_w_out": "row", "sb_w_in": "col", "sb_w_out": "row",
        "ffn_w1_0": "col", "ffn_w1_1": "col", "ffn_w2_0": "row", "ffn_w2_1": "row"}
PAT_A, PAT_B, PAT_ALL = (1, 2, 4, 6), (3, 5, 7), (1, 2, 3, 4, 5, 6, 7)


def _local_step(x, target, norms, sgu, big, got=None):
    g = {k: [v[l:l + 1] for l in range(2)] for k, v in norms.items()}
    ln_g, ln_b, sgu_w, sgu_b = sgu
    groups = sgu_w.shape[0]
    w16 = sgu_w.astype(BF16)
    bias_b = jnp.broadcast_to(sgu_b[:, :, None], (groups, CHUNK, CHUNK))
    big, dws, dist = dict(big), {}, got is not None
    got = dict(got) if dist else {}

    def run(fn, *args, gather=(), scatter=(), **kw):
        if not dist or not (gather or scatter):
            return fn(*args, **kw)
        comms = [_Gather(big[k], KIND[k]) for k in gather] + [_Scatter(dws[k], got[k], KIND[k], pat) for k, pat in scatter]
        out, rws = fn(*args, comms=comms, **kw)
        for k, r in zip(gather, rws):
            big[k] = r[0]
        for (k, _), r in zip(scatter, rws[len(gather):]):
            got[k] = r[0]
        return out

    h1_0 = _rms_fwd(x, g["pre_mix"][0], "rms_in")
    z0 = run(_matmul, h1_0, big["ab_w_in"], "nn", BF16, "ab_in", gather=("ffn_w1_0",))
    a_out = _sgu_fwd(z0, ln_g, ln_b, w16, bias_b, "sgu_fwd")
    branches = [run(_dil_fwd, z0, 1, "dil_fwd_1", gather=("ab_w_out",)), _dil_fwd(z0, 4, "dil_fwd_4"), _dil_fwd(z0, 16, "dil_fwd_16")]
    ab, ltot = _dil_merge(a_out, [b[0] for b in branches], [b[1] for b in branches], "dil_merge")
    y_0 = _matmul(ab, big["ab_w_out"], "nn", F32, "ab_out")
    x1, h2_0 = _post_pre_fwd(y_0, g["post_mix"][0], x, g["pre_ffn"][0], "norm_mix0")
    r_0 = run(_matmul, h2_0, big["ffn_w1_0"], "nn", BF16, "ffn_up_0", relu_out=True, gather=("ffn_w2_0",))
    y2_0 = run(_matmul, r_0, big["ffn_w2_0"], "nn", F32, "ffn_down_0", a_square=True, gather=("sb_w_in",))
    x2, h1_1 = _post_pre_fwd(y2_0, g["post_ffn"][0], x1, g["pre_mix"][1], "norm_ffn0")
    zc = run(_matmul, h1_1, big["sb_w_in"], "nn", BF16, "sb_in", gather=("sb_w_out",))
    o_sb, ct_sb, nb_sb = run(_sb_fwd, zc, "sb_fwd", gather=("ffn_w1_1", "ffn_w2_1"))
    y_1 = _matmul(o_sb, big["sb_w_out"], "nn", F32, "sb_out")
    x3, h2_1 = _post_pre_fwd(y_1, g["post_mix"][1], x2, g["pre_ffn"][1], "norm_mix1")
    r_1 = _matmul(h2_1, big["ffn_w1_1"], "nn", BF16, "ffn_up_1", relu_out=True)
    y2_1 = _matmul(r_1, big["ffn_w2_1"], "nn", F32, "ffn_down_1", a_square=True)
    loss, dx4, dy2_1, dg_post_ffn1 = _final_fwd_bwd(y2_1, g["post_ffn"][1], x3, target, "loss")

    da = _matmul(dy2_1, big["ffn_w2_1"], "nt", BF16, "ffn_da_1", mul2=r_1)
    dws["ffn_w2_1"] = _matmul(r_1, dy2_1, "tn", BF16, "ffn_dw2_1", a_square=True)
    dh2 = run(_matmul, da, big["ffn_w1_1"], "nt", F32, "ffn_dh_1", scatter=(("ffn_w2_1", PAT_A),))
    dws["ffn_w1_1"] = run(_matmul, h2_1, da, "tn", BF16, "ffn_dw1_1", scatter=(("ffn_w2_1", PAT_B),))
    dx3, dy_1, dg_pre_ffn1, dg_post_mix1 = _pre_post_bwd(x3, g["pre_ffn"][1], dh2, dx4, y_1, g["post_mix"][1], "norm_bwd_mix1")
    do_sb = _matmul(dy_1, big["sb_w_out"], "nt", BF16, "sb_out_dx")
    dws["sb_w_out"] = _matmul(o_sb, dy_1, "tn", BF16, "sb_out_dw")
    dqkv = run(_sb_bwd, zc, ct_sb, nb_sb, do_sb, "sb_bwd", scatter=(("ffn_w1_1", PAT_ALL), ("sb_w_out", PAT_ALL)))
    dzc = _concat_bf16(dqkv, "sb_dz")
    dh1 = _matmul(dzc, big["sb_w_in"], "nt", F32, "sb_in_dx")
    dws["sb_w_in"] = _matmul(h1_1, dzc, "tn", BF16, "sb_in_dw")
    dx2, dy2_0, dg_pre_mix1, dg_post_ffn0 = _pre_post_bwd(x2, g["pre_mix"][1], dh1, dx3, y2_0, g["post_ffn"][0], "norm_bwd_ffn0")
    da = run(_matmul, dy2_0, big["ffn_w2_0"], "nt", BF16, "ffn_da_0", mul2=r_0, scatter=(("sb_w_in", PAT_A),))
    dws["ffn_w2_0"] = run(_matmul, r_0, dy2_0, "tn", BF16, "ffn_dw2_0", a_square=True, scatter=(("sb_w_in", PAT_B),))
    dh2 = run(_matmul, da, big["ffn_w1_0"], "nt", F32, "ffn_dh_0", scatter=(("ffn_w2_0", PAT_A),))
    dws["ffn_w1_0"] = run(_matmul, h2_0, da, "tn", BF16, "ffn_dw1_0", scatter=(("ffn_w2_0", PAT_B),))
    dx1, dy_0, dg_pre_ffn0, dg_post_mix0 = run(_pre_post_bwd, x1, g["pre_ffn"][0], dh2, dx2, y_0, g["post_mix"][0], "norm_bwd_mix0",
                                               scatter=(("ffn_w1_0", (1, 2)),))
    dab = run(_matmul, dy_0, big["ab_w_out"], "nt", BF16, "ab_out_dx", scatter=(("ffn_w1_0", (4,)),))
    dws["ab_w_out"] = run(_matmul, ab, dy_0, "tn", BF16, "ab_out_dw", scatter=(("ffn_w1_0", (6,)),))
    duv, d_ln_g, d_ln_b, d_sgu_w, d_sgu_b = run(_sgu_bwd, z0, dab, ln_g, ln_b, w16, bias_b, "sgu_bwd", scatter=(("ffn_w1_0", (3,)),))
    parts = [run(_dil_bwd, z0, ab, dab, ltot, 1, "dil_bwd_1", scatter=(("ffn_w1_0", (5,)), ("ab_w_out", (1, 2, 4)))),
             run(_dil_bwd, z0, ab, dab, ltot, 4, "dil_bwd_4", scatter=(("ffn_w1_0", (7,)), ("ab_w_out", (6, 3, 5)))),
             run(_dil_bwd, z0, ab, dab, ltot, 16, "dil_bwd_16", scatter=(("ab_w_out", (7,)),))]
    dz0 = _dz_assemble(duv, parts, "dz_assemble")
    dws["ab_w_in"] = _matmul(h1_0, dz0, "tn", BF16, "ab_in_dw")
    dh1 = run(_matmul, dz0, big["ab_w_in"], "nt", F32, "ab_in_dx", scatter=(("ab_w_in", (1, 2, 4, 6, 3)),))
    grad_x, dg_pre_mix0 = run(_pre_post_bwd, x, g["pre_mix"][0], dh1, dx1, None, None, "norm_bwd_in", scatter=(("ab_w_in", (5, 7)),))

    d_norms = {
        "pre_mix": jnp.concatenate([dg_pre_mix0, dg_pre_mix1]), "post_mix": jnp.concatenate([dg_post_mix0, dg_post_mix1]),
        "pre_ffn": jnp.concatenate([dg_pre_ffn0, dg_pre_ffn1]), "post_ffn": jnp.concatenate([dg_post_ffn0, dg_post_ffn1]),
    }
    return loss, grad_x, d_norms, (d_ln_g, d_ln_b, d_sgu_w, d_sgu_b), dws, got


def _to_bf16_full(w, layer, kind, name):
    _, rows, cols = w.shape
    tr = _tile(rows, 512)
    nblk = rows // tr
    full = (rows, 4 * cols) if kind == "col" else (4 * rows, cols)

    def body(w_ref, o_ref):
        o_ref[...] = w_ref[...].astype(BF16)

    def place(i):
        mine = 2 * lax.axis_index("x") + lax.axis_index("y")
        return (i, mine) if kind == "col" else (mine * nblk + i, 0)

    return pl.pallas_call(
        body, name=name, grid=(nblk,), in_specs=[pl.BlockSpec((None, tr, cols), lambda i: (layer, i, 0))],
        out_specs=pl.BlockSpec((tr, cols), place), out_shape=jax.ShapeDtypeStruct(full, BF16), compiler_params=_params("parallel"),
    )(w)


def _owner_sum(dw16, got, buf, layer, kind, name):
    rh, cs = _half_shape(dw16.shape, kind)
    tr = _tile(rh, 128)
    nblk = rh // tr

    def body(dw_ref, got_ref, buf_ref, o_ref):
        tot = dw_ref[...].astype(F32)
        for j in range(7):
            tot = tot + got_ref[j].astype(F32)
        o_ref[...] = tot

    def own(i):
        x, y, c, mine = _place()
        return (c * nblk + i, mine) if kind == "col" else ((2 * mine + c) * nblk + i, 0)

    return pl.pallas_call(
        body, name=name, grid=(nblk,),
        in_specs=[pl.BlockSpec((tr, cs), own), pl.BlockSpec((7, tr, cs), lambda i: (0, i, 0)), ANY],
        out_specs=pl.BlockSpec((None, None, tr, cs), lambda i: (layer, lax.axis_index("c"), i, 0)),
        out_shape=jax.ShapeDtypeStruct(buf.shape, F32), input_output_aliases={2: 0}, compiler_params=_params("parallel"),
    )(dw16, got, buf)


def _adamw_math(w, g, m, v):
    m = ADAM_B1 * m + (1.0 - ADAM_B1) * g
    v = ADAM_B2 * v + (1.0 - ADAM_B2) * (g * g)
    m_hat = m / (1.0 - ADAM_B1 ** ADAM_STEP)
    v_hat = v / (1.0 - ADAM_B2 ** ADAM_STEP)
    return -ADAM_LR * (m_hat / (jnp.sqrt(v_hat) + ADAM_EPS) + ADAM_WD * w), m, v


def _adamw(w, g, m, v, name):
    layers, rows, cols = w.shape
    tr = _tile(rows, 256)

    def body(w_ref, g_ref, m_ref, v_ref, d_ref, mo_ref, vo_ref):
        d_ref[...], mo_ref[...], vo_ref[...] = _adamw_math(w_ref[...], g_ref[...], m_ref[...], v_ref[...])

    spec = pl.BlockSpec((None, tr, cols), lambda l, i: (l, i, 0))
    return pl.pallas_call(body, name=name, grid=(layers, rows // tr), in_specs=[spec] * 4, out_specs=[spec] * 3,
                          out_shape=[jax.ShapeDtypeStruct(w.shape, F32)] * 3, compiler_params=_params("parallel", "parallel"))(w, g, m, v)


def _pack(arrays):
    flat = jnp.concatenate([a.reshape(-1) for a in arrays])
    pad = (-flat.shape[0]) % 1024
    return jnp.pad(flat, (0, pad)).reshape(-1, 128)


def _unpack(packed, like):
    flat = packed.reshape(-1)
    out, off = [], 0
    for a in like:
        out.append(flat[off:off + a.size].reshape(a.shape))
        off += a.size
    return out


def _gather_small(g, name):
    rows = g.shape[0]

    def body(g_ref, o_ref, send, recv, local_sem):
        x, y, c, _ = _place()
        me = 4 * x + 2 * y + c
        local = pltpu.make_async_copy(g_ref, o_ref.at[me], local_sem)
        local.start()
        copies = []
        for j in range(1, 8):
            px = 1 - x if j & 4 else x
            py = 1 - y if j & 2 else y
            pc = 1 - c if j & 1 else c
            copies.append(pltpu.make_async_remote_copy(src_ref=g_ref, dst_ref=o_ref.at[me], send_sem=send.at[j - 1],
                                                       recv_sem=recv.at[j - 1], device_id=(px, py, pc), device_id_type=MESH))
        for cp in copies:
            cp.start()
        for j in range(1, 8):
            px = 1 - x if j & 4 else x
            py = 1 - y if j & 2 else y
            pc = 1 - c if j & 1 else c
            pltpu.make_async_remote_copy(src_ref=g_ref, dst_ref=o_ref.at[4 * px + 2 * py + pc], send_sem=send.at[j - 1],
                                         recv_sem=recv.at[j - 1], device_id=(px, py, pc), device_id_type=MESH).wait_recv()
        for cp in copies:
            cp.wait_send()
        local.wait()

    vmem = pl.BlockSpec(memory_space=pltpu.VMEM)
    return pl.pallas_call(
        body, name=name, in_specs=[vmem], out_specs=vmem, out_shape=jax.ShapeDtypeStruct((8, rows, 128), F32),
        scratch_shapes=[pltpu.SemaphoreType.DMA((7,)), pltpu.SemaphoreType.DMA((7,)), pltpu.SemaphoreType.DMA(())],
        compiler_params=_params(),
    )(g)


def _small_update(parts, w, m, v, name):
    rows = w.shape[0]

    def body(p_ref, w_ref, m_ref, v_ref, g_ref, d_ref, mo_ref, vo_ref):
        g = p_ref[0]
        for k in range(1, 8):
            g = g + p_ref[k]
        g_ref[...] = g
        d_ref[...], mo_ref[...], vo_ref[...] = _adamw_math(w_ref[...], g, m_ref[...], v_ref[...])

    return pl.pallas_call(body, name=name, out_shape=[jax.ShapeDtypeStruct((rows, 128), F32)] * 4, compiler_params=_params())(parts, w, m, v)


SMALL = ("norm_pre_mix", "norm_post_mix", "norm_pre_ffn", "norm_post_ffn", "sgu_ln_g", "sgu_ln_b", "sgu_w", "sgu_b")
BIG = (("ab_w_in", ("ab_w_in",)), ("ab_w_out", ("ab_w_out",)), ("sb_w_in", ("sb_w_in",)), ("sb_w_out", ("sb_w_out",)),
       ("ffn_w1", ("ffn_w1_0", "ffn_w1_1")), ("ffn_w2", ("ffn_w2_0", "ffn_w2_1")))
WEIGHTS = ("norm_pre_mix", "norm_post_mix", "norm_pre_ffn", "norm_post_ffn", "ab_w_in", "sgu_ln_g", "sgu_ln_b", "sgu_w", "sgu_b",
           "ab_w_out", "sb_w_in", "sb_w_out", "ffn_w1", "ffn_w2")


def kernel(x, norm_pre_mix, norm_post_mix, norm_pre_ffn, norm_post_ffn, ab_w_in, sgu_ln_g, sgu_ln_b, sgu_w, sgu_b, ab_w_out, sb_w_in, sb_w_out, ffn_w1, ffn_w2, loss_target, m_norm_pre_mix, m_norm_post_mix, m_norm_pre_ffn, m_norm_post_ffn, m_ab_w_in, m_sgu_ln_g, m_sgu_ln_b, m_sgu_w, m_sgu_b, m_ab_w_out, m_sb_w_in, m_sb_w_out, m_ffn_w1, m_ffn_w2, v_norm_pre_mix, v_norm_post_mix, v_norm_pre_ffn, v_norm_post_ffn, v_ab_w_in, v_sgu_ln_g, v_sgu_ln_b, v_sgu_w, v_sgu_b, v_ab_w_out, v_sb_w_in, v_sb_w_out, v_ffn_w1, v_ffn_w2):
    w = dict(norm_pre_mix=norm_pre_mix, norm_post_mix=norm_post_mix, norm_pre_ffn=norm_pre_ffn, norm_post_ffn=norm_post_ffn,
             ab_w_in=ab_w_in, sgu_ln_g=sgu_ln_g, sgu_ln_b=sgu_ln_b, sgu_w=sgu_w, sgu_b=sgu_b, ab_w_out=ab_w_out, sb_w_in=sb_w_in,
             sb_w_out=sb_w_out, ffn_w1=ffn_w1, ffn_w2=ffn_w2)
    m = dict(norm_pre_mix=m_norm_pre_mix, norm_post_mix=m_norm_post_mix, norm_pre_ffn=m_norm_pre_ffn, norm_post_ffn=m_norm_post_ffn,
             ab_w_in=m_ab_w_in, sgu_ln_g=m_sgu_ln_g, sgu_ln_b=m_sgu_ln_b, sgu_w=m_sgu_w, sgu_b=m_sgu_b, ab_w_out=m_ab_w_out,
             sb_w_in=m_sb_w_in, sb_w_out=m_sb_w_out, ffn_w1=m_ffn_w1, ffn_w2=m_ffn_w2)
    v = dict(norm_pre_mix=v_norm_pre_mix, norm_post_mix=v_norm_post_mix, norm_pre_ffn=v_norm_pre_ffn, norm_post_ffn=v_norm_post_ffn,
             ab_w_in=v_ab_w_in, sgu_ln_g=v_sgu_ln_g, sgu_ln_b=v_sgu_ln_b, sgu_w=v_sgu_w, sgu_b=v_sgu_b, ab_w_out=v_ab_w_out,
             sb_w_in=v_sb_w_in, sb_w_out=v_sb_w_out, ffn_w1=v_ffn_w1, ffn_w2=v_ffn_w2)
    big, got = {}, {}
    for name, keys in BIG:
        for layer, key in enumerate(keys):
            big[key] = _to_bf16_full(w[name], layer, KIND[key], f"bf16_{key}")
            got[key] = lax.empty((7,) + _half_shape(big[key].shape, KIND[key]), BF16)
    big["ab_w_in"] = _comm_call([_Gather(big["ab_w_in"], KIND["ab_w_in"])], "gather_first")[0][0]

    norms = {k: w["norm_" + k] for k in ("pre_mix", "post_mix", "pre_ffn", "post_ffn")}
    sgu = (sgu_ln_g, sgu_ln_b, sgu_w[0], sgu_b[0])
    loss_blk, grad_x, d_norms, d_sgu, dws, got = _local_step(x[0], loss_target[0], norms, sgu, big, got)
    loss = lax.psum(loss_blk[0, 0], ("x", "y", "c"))

    grads, deltas, new_m, new_v = {}, {}, {}, {}
    small_g = [d_norms["pre_mix"], d_norms["post_mix"], d_norms["pre_ffn"], d_norms["post_ffn"],
               d_sgu[0], d_sgu[1], d_sgu[2][None], d_sgu[3][None]]
    gathered = _gather_small(_pack(small_g), "gather_small")
    outs = _small_update(gathered, _pack([w[k] for k in SMALL]), _pack([m[k] for k in SMALL]), _pack([v[k] for k in SMALL]), "small_update")
    like = [w[k] for k in SMALL]
    for dst, packed in zip((grads, deltas, new_m, new_v), outs):
        for k, a in zip(SMALL, _unpack(packed, like)):
            dst[k] = a

    bufs = []
    for name, keys in BIG:
        buf = lax.empty((len(keys), 2) + _half_shape(dws[keys[0]].shape, KIND[keys[0]]), F32)
        for layer, key in enumerate(keys):
            buf = _owner_sum(dws[key], got[key], buf, layer, KIND[key], f"sum_{key}")
        bufs.append(buf)
    joined = _comm_call([_Join(bufs)], "join")[0]
    for (name, _), buf in zip(BIG, joined):
        grads[name] = buf.reshape(w[name].shape)
        deltas[name], new_m[name], new_v[name] = _adamw(w[name], grads[name], m[name], v[name], f"adamw_{name}")

    return (loss, grad_x[None], *[grads[k] for k in WEIGHTS], *[deltas[k] for k in WEIGHTS],
            *[new_m[k] for k in WEIGHTS], *[new_v[k] for k in WEIGHTS])
```

```python
import functools

import jax
import jax.numpy as jnp
from jax import lax
from jax.experimental import pallas as pl
from jax.experimental.pallas import tpu as pltpu

F32 = jnp.float32
BF16 = jnp.bfloat16
MESH = pl.DeviceIdType.MESH

HEAD_DIM = 128
CHUNK = 128
DILATIONS = (1, 4, 16)
SB_BLOCK = 256
RMS_EPS = 1e-6
LN_EPS = 1e-5
ADAM_LR, ADAM_B1, ADAM_B2, ADAM_EPS, ADAM_WD, ADAM_STEP = 0.001, 0.9, 0.999, 1e-08, 0.01, 10
NEG = -1e30
V7X_VMEM_LIMIT = 48 * 1024 * 1024
ANY = pl.BlockSpec(memory_space=pl.ANY)


def _params(*sem):
    return pltpu.CompilerParams(dimension_semantics=sem if sem else None, vmem_limit_bytes=V7X_VMEM_LIMIT)


def _tile(n, pref):
    if n <= pref:
        return n
    t = pref
    while n % t:
        t -= 128
    return t


def _dot(a, b, dims):
    return lax.dot_general(a, b, (dims, ((), ())), preferred_element_type=F32)


NN = ((1,), (0,))
NT = ((1,), (1,))
TN = ((0,), (0,))


def _place():
    x, y, c = lax.axis_index("x"), lax.axis_index("y"), lax.axis_index("c")
    return x, y, c, 2 * x + y


def _flip(x, y, c, j):
    return (1 - x if j & 4 else x), (1 - y if j & 2 else y), (1 - c if j & 1 else c)


def _half_shape(full_shape, kind):
    rows, cols = full_shape
    return (rows // 2, cols // 4) if kind == "col" else (rows // 8, cols)


def _half(ref, kind, s, h):
    rh, cs = _half_shape(ref.shape, kind)
    if kind == "col":
        return ref.at[pl.ds(h * rh, rh), pl.ds(s * cs, cs)]
    return ref.at[pl.ds((2 * s + h) * rh, rh), :]


def _remote(src, dst, send, recv, to):
    return pltpu.make_async_remote_copy(src_ref=src, dst_ref=dst, send_sem=send, recv_sem=recv, device_id=to, device_id_type=MESH)


class _Gather:
    n_sems = 6

    def __init__(self, full, kind):
        self.ro, self.rw, self.kind = [], [full], kind

    def start(self, ro, rw, send, recv):
        x, y, c, mine = _place()
        own = _half(rw[0], self.kind, mine, c)
        for k, j in enumerate((2, 4, 6)):
            px, py, _ = _flip(x, y, c, j)
            _remote(own, own, send(k), recv(k), (px, py, c)).start()

    def finish(self, ro, rw, send, recv):
        x, y, c, mine = _place()
        own = _half(rw[0], self.kind, mine, c)
        for k, j in enumerate((2, 4, 6)):
            px, py, _ = _flip(x, y, c, j)
            got = _half(rw[0], self.kind, 2 * px + py, c)
            _remote(got, got, send(k), recv(k), (x, y, c)).wait_recv()
            _remote(got, got, send(3 + k), recv(3 + k), (x, y, 1 - c)).start()
        for k, j in enumerate((2, 4, 6)):
            px, py, _ = _flip(x, y, c, j)
            got = _half(rw[0], self.kind, 2 * px + py, 1 - c)
            _remote(got, got, send(3 + k), recv(3 + k), (x, y, c)).wait_recv()
        for k in range(6):
            _remote(own, own, send(k), recv(k), (x, y, c)).wait_send()


class _GatherSend:
    def __init__(self, full, kind, patterns):
        self.ro, self.rw, self.kind, self.patterns, self.n_sems = [], [full], kind, patterns, len(patterns)

    def start(self, ro, rw, send, recv):
        x, y, c, mine = _place()
        own = _half(rw[0], self.kind, mine, c)
        for k, j in enumerate(self.patterns):
            px, py, _ = _flip(x, y, c, j)
            _remote(own, own, send(k), recv(k), (px, py, c)).start()

    def finish(self, ro, rw, send, recv):
        x, y, c, _ = _place()
        for k, j in enumerate(self.patterns):
            px, py, _ = _flip(x, y, c, j)
            got = _half(rw[0], self.kind, 2 * px + py, c)
            cp = _remote(got, got, send(k), recv(k), (x, y, c))
            cp.wait_recv()
            cp.wait_send()


class _GatherFwd:
    def __init__(self, full, kind, patterns):
        self.ro, self.rw, self.kind, self.patterns, self.n_sems = [], [full], kind, patterns, len(patterns)

    def start(self, ro, rw, send, recv):
        x, y, c, _ = _place()
        for k, j in enumerate(self.patterns):
            px, py, _ = _flip(x, y, c, j)
            got = _half(rw[0], self.kind, 2 * px + py, c)
            _remote(got, got, send(k), recv(k), (x, y, 1 - c)).start()

    def finish(self, ro, rw, send, recv):
        x, y, c, _ = _place()
        for k, j in enumerate(self.patterns):
            px, py, _ = _flip(x, y, c, j)
            got = _half(rw[0], self.kind, 2 * px + py, 1 - c)
            cp = _remote(got, got, send(k), recv(k), (x, y, c))
            cp.wait_recv()
            cp.wait_send()


class _PairSwap:
    n_sems = 4

    def __init__(self, dw16, pair, kind):
        self.ro, self.rw, self.kind = [dw16], [pair], kind

    def start(self, ro, rw, send, recv):
        x, y, c, _ = _place()
        for s in range(4):
            _remote(_half(ro[0], self.kind, s, 1 - c), rw[0].at[s], send(s), recv(s), (x, y, 1 - c)).start()

    def finish(self, ro, rw, send, recv):
        x, y, c, _ = _place()
        for s in range(4):
            cp = _remote(rw[0].at[s], rw[0].at[s], send(s), recv(s), (x, y, c))
            cp.wait_recv()
            cp.wait_send()


class _ChipScatter:
    def __init__(self, psum, got, patterns):
        self.ro, self.rw, self.patterns, self.n_sems = [psum], [got], patterns, len(patterns)

    def start(self, ro, rw, send, recv):
        x, y, c, _ = _place()
        for k, j in enumerate(self.patterns):
            px, py, _ = _flip(x, y, c, j)
            _remote(ro[0].at[2 * px + py], rw[0].at[j // 2 - 1], send(k), recv(k), (px, py, c)).start()

    def finish(self, ro, rw, send, recv):
        x, y, c, _ = _place()
        for k, j in enumerate(self.patterns):
            slot = rw[0].at[j // 2 - 1]
            cp = _remote(slot, slot, send(k), recv(k), (x, y, c))
            cp.wait_recv()
            cp.wait_send()


class _Join:
    def __init__(self, bufs):
        self.ro, self.rw, self.n_sems = [], list(bufs), sum(b.shape[0] for b in bufs)

    def _copies(self, rw, send, recv, slot):
        x, y, c, _ = _place()
        k = 0
        for ref in rw:
            for l in range(ref.shape[0]):
                yield _remote(ref.at[l, c], ref.at[l, slot(c)], send(k), recv(k), (x, y, 1 - c))
                k += 1

    def start(self, ro, rw, send, recv):
        for cp in self._copies(rw, send, recv, lambda c: c):
            cp.start()

    def finish(self, ro, rw, send, recv):
        for cp in self._copies(rw, send, recv, lambda c: 1 - c):
            cp.wait_recv()
        for cp in self._copies(rw, send, recv, lambda c: c):
            cp.wait_send()


def _comm_layout(comms):
    ro = [a for c in comms for a in c.ro]
    rw = [a for c in comms for a in c.rw]
    return ro, rw, sum(c.n_sems for c in comms)


def _comm_each(comms, method, ro_refs, rw_refs, send, recv):
    i_ro = i_rw = i_sem = 0
    for c in comms:
        getattr(c, method)(ro_refs[i_ro:i_ro + len(c.ro)], rw_refs[i_rw:i_rw + len(c.rw)],
                           lambda k, b=i_sem: send.at[b + k], lambda k, b=i_sem: recv.at[b + k])
        i_ro, i_rw, i_sem = i_ro + len(c.ro), i_rw + len(c.rw), i_sem + c.n_sems


def _split_results(comms, rws):
    out, i = [], 0
    for c in comms:
        out.append(list(rws[i:i + len(c.rw)]))
        i += len(c.rw)
    return out


def _comm_call(comms, name):
    ro, rw, n_sems = _comm_layout(comms)

    def body(*refs):
        ro_refs = refs[:len(ro)]
        rw_refs = refs[len(ro) + len(rw):len(ro) + 2 * len(rw)]
        send, recv = refs[len(ro) + 2 * len(rw):]
        _comm_each(comms, "start", ro_refs, rw_refs, send, recv)
        _comm_each(comms, "finish", ro_refs, rw_refs, send, recv)

    rws = pl.pallas_call(
        body, name=name, in_specs=[ANY] * (len(ro) + len(rw)), out_specs=[ANY] * len(rw),
        out_shape=[jax.ShapeDtypeStruct(a.shape, a.dtype) for a in rw],
        input_output_aliases={len(ro) + k: k for k in range(len(rw))},
        scratch_shapes=[pltpu.SemaphoreType.DMA((n_sems,)), pltpu.SemaphoreType.DMA((n_sems,))],
    )(*ro, *rw)
    return _split_results(comms, rws)


def _pcall(body, args, *, name, grid, in_specs, out_specs, out_shape, scratch=(), sem=(), comms=()):
    n_in, n_out, n_scr = len(in_specs), len(out_specs), len(scratch)
    if not comms:
        return pl.pallas_call(body, name=name, grid=grid, in_specs=list(in_specs), out_specs=list(out_specs),
                              out_shape=list(out_shape), scratch_shapes=list(scratch), compiler_params=_params(*sem))(*args)
    ro, rw, n_sems = _comm_layout(comms)

    def carrier(*refs):
        ins = refs[:n_in]
        ro_refs = refs[n_in:n_in + len(ro)]
        o0 = n_in + len(ro) + len(rw)
        outs = refs[o0:o0 + n_out]
        rw_refs = refs[o0 + n_out:o0 + n_out + len(rw)]
        s0 = o0 + n_out + len(rw)
        send, recv = refs[s0 + n_scr], refs[s0 + n_scr + 1]
        ids = [pl.program_id(a) for a in range(len(grid))]
        first = functools.reduce(jnp.logical_and, [i == 0 for i in ids])
        last = functools.reduce(jnp.logical_and, [i == g - 1 for i, g in zip(ids, grid)])

        @pl.when(first)
        def _():
            _comm_each(comms, "start", ro_refs, rw_refs, send, recv)

        body(*ins, *outs, *refs[s0:s0 + n_scr])

        @pl.when(last)
        def _():
            _comm_each(comms, "finish", ro_refs, rw_refs, send, recv)

    res = pl.pallas_call(
        carrier, name=name, grid=grid, in_specs=list(in_specs) + [ANY] * (len(ro) + len(rw)),
        out_specs=list(out_specs) + [ANY] * len(rw),
        out_shape=list(out_shape) + [jax.ShapeDtypeStruct(a.shape, a.dtype) for a in rw],
        input_output_aliases={n_in + len(ro) + k: n_out + k for k in range(len(rw))},
        scratch_shapes=list(scratch) + [pltpu.SemaphoreType.DMA((n_sems,)), pltpu.SemaphoreType.DMA((n_sems,))],
        compiler_params=_params(*["arbitrary"] * len(grid)),
    )(*args, *ro, *rw)
    return list(res[:n_out]), _split_results(comms, res[n_out:])


def _matmul(a, b, mode, out_dtype, name, a_square=False, relu_out=False, mul2=None, comms=()):
    if mode == "nn":
        (m, k), n = a.shape, b.shape[1]
    elif mode == "nt":
        (m, k), n = a.shape, b.shape[0]
    else:
        (k, m), n = a.shape, b.shape[1]
    tm, tn, tk = _tile(m, 1024), _tile(n, 1024), _tile(k, 2048)
    nk = k // tk
    dims = {"nn": NN, "nt": NT, "tn": TN}[mode]
    a_spec = pl.BlockSpec((tk, tm), lambda i, j, kk: (kk, i)) if mode == "tn" else pl.BlockSpec((tm, tk), lambda i, j, kk: (i, kk))
    b_spec = pl.BlockSpec((tn, tk), lambda i, j, kk: (j, kk)) if mode == "nt" else pl.BlockSpec((tk, tn), lambda i, j, kk: (kk, j))
    o_spec = pl.BlockSpec((tm, tn), lambda i, j, kk: (i, j))

    def body(a_ref, b_ref, *rest):
        m_ref = None if mul2 is None else rest[0]
        o_ref = rest[0 if mul2 is None else 1]
        kk = pl.program_id(2)

        def partial():
            av = a_ref[...]
            if a_square:
                av = av * av
            return _dot(av, b_ref[...], dims)

        def finish(r):
            if relu_out:
                r = jnp.maximum(r, 0.0)
            if mul2 is not None:
                r = r * (2.0 * m_ref[...].astype(F32))
            o_ref[...] = r.astype(out_dtype)

        if nk == 1:
            finish(partial())
            return
        acc_ref = rest[-1]

        @pl.when(kk == 0)
        def _():
            acc_ref[...] = partial()

        @pl.when(kk > 0)
        def _():
            acc_ref[...] += partial()

        @pl.when(kk == nk - 1)
        def _():
            finish(acc_ref[...])

    args = (a, b) if mul2 is None else (a, b, mul2)
    specs = [a_spec, b_spec] + ([] if mul2 is None else [o_spec])
    res = _pcall(body, args, name=name, grid=(m // tm, n // tn, nk), in_specs=specs, out_specs=[o_spec],
                 out_shape=[jax.ShapeDtypeStruct((m, n), out_dtype)], scratch=[pltpu.VMEM((tm, tn), F32)] if nk > 1 else [],
                 sem=("parallel", "parallel", "arbitrary"), comms=comms)
    return (res[0][0], res[1]) if comms else res[0]


NORM_ROWS = 256


def _rms(x, g):
    rstd = lax.rsqrt(jnp.mean(x * x, axis=-1, keepdims=True) + RMS_EPS)
    n = x * rstd
    return n * g, n, rstd


def _rms_bwd(n, rstd, g, dout):
    dn = dout * g
    return rstd * (dn - n * jnp.mean(dn * n, axis=-1, keepdims=True))


def _row_spec(d):
    return pl.BlockSpec((NORM_ROWS, d), lambda i: (i, 0))


def _vec_spec(d):
    return pl.BlockSpec((1, d), lambda i: (0, 0))


def _accumulate(ref, val):
    @pl.when(pl.program_id(0) == 0)
    def _():
        ref[...] = jnp.zeros_like(ref)

    ref[...] += val


def _rms_fwd(x, g, name):
    t, d = x.shape

    def body(x_ref, g_ref, h_ref):
        h_ref[...] = _rms(x_ref[...], g_ref[...])[0].astype(BF16)

    return pl.pallas_call(
        body, name=name, grid=(t // NORM_ROWS,), in_specs=[_row_spec(d), _vec_spec(d)], out_specs=_row_spec(d),
        out_shape=jax.ShapeDtypeStruct((t, d), BF16), compiler_params=_params("parallel"),
    )(x, g)


def _post_pre_fwd(y, g_post, x, g_pre, name, comms=()):
    t, d = x.shape

    def body(y_ref, gp_ref, x_ref, gn_ref, xn_ref, h_ref):
        xn = x_ref[...] + _rms(y_ref[...], gp_ref[...])[0]
        xn_ref[...] = xn
        h_ref[...] = _rms(xn, gn_ref[...])[0].astype(BF16)

    return _pcall(
        body, (y, g_post, x, g_pre), name=name, grid=(t // NORM_ROWS,),
        in_specs=[_row_spec(d), _vec_spec(d), _row_spec(d), _vec_spec(d)], out_specs=[_row_spec(d), _row_spec(d)],
        out_shape=[jax.ShapeDtypeStruct((t, d), F32), jax.ShapeDtypeStruct((t, d), BF16)], sem=("parallel",), comms=comms)


def _final_fwd_bwd(y, g_post, x, target, name):
    t, d = x.shape

    def body(y_ref, g_ref, x_ref, t_ref, loss_ref, dx_ref, dy_ref, dg_ref):
        g = g_ref[...]
        out, n, rstd = _rms(y_ref[...], g)
        e = x_ref[...] + out - t_ref[...]
        _accumulate(loss_ref, jnp.full(loss_ref.shape, 0.5 / d, F32) * jnp.sum(e * e))
        dx = e * (1.0 / d)
        dx_ref[...] = dx
        dy_ref[...] = _rms_bwd(n, rstd, g, dx).astype(BF16)
        _accumulate(dg_ref, jnp.sum(dx * n, axis=0, keepdims=True))

    return pl.pallas_call(
        body, name=name, grid=(t // NORM_ROWS,),
        in_specs=[_row_spec(d), _vec_spec(d), _row_spec(d), _row_spec(d)],
        out_specs=[pl.BlockSpec((8, 128), lambda i: (0, 0)), _row_spec(d), _row_spec(d), _vec_spec(d)],
        out_shape=[jax.ShapeDtypeStruct((8, 128), F32), jax.ShapeDtypeStruct((t, d), F32),
                   jax.ShapeDtypeStruct((t, d), BF16), jax.ShapeDtypeStruct((1, d), F32)],
        compiler_params=_params("arbitrary"),
    )(y, g_post, x, target)


def _pre_post_bwd(x, g_pre, dh, dx_in, y, g_post, name, comms=()):
    t, d = x.shape
    both = y is not None

    def body(x_ref, gp_ref, dh_ref, dxi_ref, *rest):
        if both:
            y_ref, gq_ref, dx_ref, dy_ref, dgp_ref, dgq_ref = rest
        else:
            dx_ref, dgp_ref = rest
        gp = gp_ref[...]
        _, n, rstd = _rms(x_ref[...], gp)
        dh_v = dh_ref[...]
        dx = dxi_ref[...] + _rms_bwd(n, rstd, gp, dh_v)
        dx_ref[...] = dx
        _accumulate(dgp_ref, jnp.sum(dh_v * n, axis=0, keepdims=True))
        if both:
            gq = gq_ref[...]
            _, ny, rstdy = _rms(y_ref[...], gq)
            dy_ref[...] = _rms_bwd(ny, rstdy, gq, dx).astype(BF16)
            _accumulate(dgq_ref, jnp.sum(dx * ny, axis=0, keepdims=True))

    in_specs = [_row_spec(d), _vec_spec(d), _row_spec(d), _row_spec(d)]
    args = [x, g_pre, dh, dx_in]
    if both:
        in_specs += [_row_spec(d), _vec_spec(d)]
        args += [y, g_post]
        out_specs = [_row_spec(d), _row_spec(d), _vec_spec(d), _vec_spec(d)]
        out_shape = [jax.ShapeDtypeStruct((t, d), F32), jax.ShapeDtypeStruct((t, d), BF16),
                     jax.ShapeDtypeStruct((1, d), F32), jax.ShapeDtypeStruct((1, d), F32)]
    else:
        out_specs = [_row_spec(d), _vec_spec(d)]
        out_shape = [jax.ShapeDtypeStruct((t, d), F32), jax.ShapeDtypeStruct((1, d), F32)]
    return _pcall(body, args, name=name, grid=(t // NORM_ROWS,), in_specs=in_specs, out_specs=out_specs, out_shape=out_shape,
                  sem=("arbitrary",), comms=comms)


def _gelu(x):
    return 0.5 * x * (1.0 + lax.erf(x * 0.7071067811865476))


def _gelu_grad(x):
    return 0.5 * (1.0 + lax.erf(x * 0.7071067811865476)) + x * jnp.exp(-0.5 * x * x) * 0.3989422804014327


def _layernorm(v, g, b):
    mu = jnp.mean(v, axis=-1, keepdims=True)
    vc = v - mu
    rs = lax.rsqrt(jnp.mean(vc * vc, axis=-1, keepdims=True) + LN_EPS)
    vhat = vc * rs
    return vhat * g + b, vhat, rs


def _tril_mask():
    return lax.broadcasted_iota(jnp.int32, (CHUNK, CHUNK), 0) >= lax.broadcasted_iota(jnp.int32, (CHUNK, CHUNK), 1)


def _sgu_fwd(z, ln_g, ln_b, w16, bias_b, name, comms=()):
    t = z.shape[0]
    groups = w16.shape[0]
    a = groups * CHUNK

    def body(u_ref, v_ref, g_ref, b_ref, w_ref, bb_ref, o_ref):
        u = _gelu(u_ref[...].astype(F32))
        vn = _layernorm(_gelu(v_ref[...].astype(F32)), g_ref[...], b_ref[...])[0].astype(BF16)
        tril = _tril_mask()
        for g in range(groups):
            sl = slice(g * CHUNK, (g + 1) * CHUNK)
            w = jnp.where(tril, w_ref[g], jnp.zeros((), BF16))
            mixed = _dot(w, vn[:, sl], NN) + bb_ref[g]
            o_ref[:, sl] = (u[:, sl] * mixed).astype(BF16)

    full3 = pl.BlockSpec((groups, CHUNK, CHUNK), lambda c: (0, 0, 0))
    res = _pcall(
        body, (z, z, ln_g, ln_b, w16, bias_b), name=name, grid=(t // CHUNK,),
        in_specs=[pl.BlockSpec((CHUNK, a), lambda c: (c, 0)), pl.BlockSpec((CHUNK, a), lambda c: (c, 1)),
                  _vec_spec(a), _vec_spec(a), full3, full3],
        out_specs=[pl.BlockSpec((CHUNK, a), lambda c: (c, 0))], out_shape=[jax.ShapeDtypeStruct((t, a), BF16)],
        sem=("parallel",), comms=comms)
    return (res[0][0], res[1]) if comms else res[0]


def _sgu_bwd(z, dab, ln_g, ln_b, w16, bias_b, name, comms=()):
    t = z.shape[0]
    groups = w16.shape[0]
    a = groups * CHUNK

    def body(u_ref, v_ref, da_ref, g_ref, b_ref, w_ref, bb_ref, duv_ref, dg_ref, db_ref, dw_ref, dbs_ref, dvn_ref):
        up = u_ref[...].astype(F32)
        vp = v_ref[...].astype(F32)
        u = _gelu(up)
        ln_gain = g_ref[...]
        vn32, vhat, rs = _layernorm(_gelu(vp), ln_gain, b_ref[...])
        vn = vn32.astype(BF16)
        da = da_ref[...].astype(F32)
        tril = _tril_mask()
        ones = jnp.ones((8, CHUNK), F32)

        @pl.when(pl.program_id(0) == 0)
        def _():
            dw_ref[...] = jnp.zeros_like(dw_ref)
            dbs_ref[...] = jnp.zeros_like(dbs_ref)

        for g in range(groups):
            sl = slice(g * CHUNK, (g + 1) * CHUNK)
            w = jnp.where(tril, w_ref[g], jnp.zeros((), BF16))
            mixed = _dot(w, vn[:, sl], NN) + bb_ref[g]
            dmix = da[:, sl] * u[:, sl]
            dmix16 = dmix.astype(BF16)
            duv_ref[:, sl] = (da[:, sl] * mixed * _gelu_grad(up[:, sl])).astype(BF16)
            dvn_ref[:, sl] = _dot(w, dmix16, TN)
            dw_ref[g] += jnp.where(tril, _dot(dmix16, vn[:, sl], NT), 0.0)
            dbs_ref[g:g + 1, :] += lax.dot_general(ones, dmix, (NT, ((), ())), precision=lax.Precision.HIGHEST,
                                                   preferred_element_type=F32)[0:1]
        dvn = dvn_ref[...]
        dvhat = dvn * ln_gain
        dva = rs * (dvhat - jnp.mean(dvhat, axis=-1, keepdims=True) - vhat * jnp.mean(dvhat * vhat, axis=-1, keepdims=True))
        duv_ref[:, a:] = (dva * _gelu_grad(vp)).astype(BF16)
        _accumulate(dg_ref, jnp.sum(dvn * vhat, axis=0, keepdims=True))
        _accumulate(db_ref, jnp.sum(dvn, axis=0, keepdims=True))

    full3 = pl.BlockSpec((groups, CHUNK, CHUNK), lambda c: (0, 0, 0))
    return _pcall(
        body, (z, z, dab, ln_g, ln_b, w16, bias_b), name=name, grid=(t // CHUNK,),
        in_specs=[pl.BlockSpec((CHUNK, a), lambda c: (c, 0)), pl.BlockSpec((CHUNK, a), lambda c: (c, 1)),
                  pl.BlockSpec((CHUNK, a), lambda c: (c, 0)), _vec_spec(a), _vec_spec(a), full3, full3],
        out_specs=[pl.BlockSpec((CHUNK, 2 * a), lambda c: (c, 0)), _vec_spec(a), _vec_spec(a), full3,
                   pl.BlockSpec((groups, CHUNK), lambda c: (0, 0))],
        out_shape=[jax.ShapeDtypeStruct((t, 2 * a), BF16), jax.ShapeDtypeStruct((1, a), F32), jax.ShapeDtypeStruct((1, a), F32),
                   jax.ShapeDtypeStruct((groups, CHUNK, CHUNK), F32), jax.ShapeDtypeStruct((groups, CHUNK), F32)],
        scratch=[pltpu.VMEM((CHUNK, a), F32)], sem=("arbitrary",), comms=comms)


def _dil_masks(d):
    qi = lax.broadcasted_iota(jnp.int32, (CHUNK, CHUNK), 0)
    kj = lax.broadcasted_iota(jnp.int32, (CHUNK, CHUNK), 1)
    dist_c = qi - kj
    return dist_c >= 0, dist_c <= 0, (dist_c * d).astype(F32), ((dist_c + CHUNK) * d).astype(F32)


def _alibi_slope(h, heads):
    return 2.0 ** (-8.0 * (h + 1) / heads)


def _dil_fwd(z, d, name, comms=()):
    t = z.shape[0]
    w = z.shape[1] // 5
    heads = w // HEAD_DIM
    nb = t // d // CHUNK
    scale = HEAD_DIM ** -0.5

    def body(q_ref, kp_ref, kc_ref, vp_ref, vc_ref, o_ref, l_ref):
        ok_c, ok_p0, bias_c, bias_p = _dil_masks(d)
        ok_p = ok_p0 & (pl.program_id(1) > 0)
        for h in range(heads):
            sl = slice(h * HEAD_DIM, (h + 1) * HEAD_DIM)
            slope = _alibi_slope(h, heads)
            q = q_ref[:, sl]
            s_c = jnp.where(ok_c, _dot(q, kc_ref[:, sl], NT) * scale - slope * bias_c, NEG)
            s_p = jnp.where(ok_p, _dot(q, kp_ref[:, sl], NT) * scale - slope * bias_p, NEG)
            m = jnp.maximum(jnp.max(s_c, axis=1, keepdims=True), jnp.max(s_p, axis=1, keepdims=True))
            p_c = jnp.exp(s_c - m)
            p_p = jnp.exp(s_p - m)
            den = jnp.sum(p_c, axis=1, keepdims=True) + jnp.sum(p_p, axis=1, keepdims=True)
            o = _dot(p_c.astype(BF16), vc_ref[:, sl], NN) + _dot(p_p.astype(BF16), vp_ref[:, sl], NN)
            o_ref[:, sl] = o / den
            l_ref[:, sl] = jnp.broadcast_to(m + jnp.log(den), (CHUNK, HEAD_DIM))

    def zspec(col, prev):
        if prev:
            return pl.BlockSpec((CHUNK, w), lambda r, n: (jnp.maximum(n - 1, 0), r * 5 + col))
        return pl.BlockSpec((CHUNK, w), lambda r, n: (n, r * 5 + col))

    ospec = pl.BlockSpec((CHUNK, w), lambda r, n: (n, r))
    zv = z.reshape(t // d, d * 5 * w)
    res = _pcall(
        body, (zv, zv, zv, zv, zv), name=name, grid=(d, nb),
        in_specs=[zspec(2, False), zspec(3, True), zspec(3, False), zspec(4, True), zspec(4, False)],
        out_specs=[ospec, ospec],
        out_shape=[jax.ShapeDtypeStruct((t // d, d * w), F32), jax.ShapeDtypeStruct((t // d, d * w), F32)],
        sem=("parallel", "parallel"), comms=comms)
    (o, lse), rws = res if comms else (res, None)
    outs = (o.reshape(t, w), lse.reshape(t, w))
    return (outs, rws) if comms else outs


def _dil_merge(a_out, outs, lses, name, comms=()):
    t, a = a_out.shape
    w = outs[0].shape[1]
    nbr = len(outs)

    def body(a_ref, *rest):
        o_refs, l_refs, (ab_ref, lt_ref) = rest[:nbr], rest[nbr:2 * nbr], rest[2 * nbr:]
        ls = [r[...] for r in l_refs]
        m = functools.reduce(jnp.maximum, ls)
        ws = [jnp.exp(l - m) for l in ls]
        tot = functools.reduce(jnp.add, ws)
        mix = functools.reduce(jnp.add, [wt * r[...] for wt, r in zip(ws, o_refs)]) / tot
        ab_ref[:, :a] = a_ref[...]
        ab_ref[:, a:] = mix.astype(BF16)
        lt_ref[...] = m + jnp.log(tot)

    return _pcall(
        body, (a_out, *outs, *lses), name=name, grid=(t // NORM_ROWS,),
        in_specs=[_row_spec(a)] + [_row_spec(w)] * (2 * nbr), out_specs=[_row_spec(a + w), _row_spec(w)],
        out_shape=[jax.ShapeDtypeStruct((t, a + w), BF16), jax.ShapeDtypeStruct((t, w), F32)],
        sem=("parallel",), comms=comms)


def _dil_bwd(z, ab, dab, ltot, d, name, comms=()):
    t = z.shape[0]
    w = z.shape[1] // 5
    heads = w // HEAD_DIM
    nb = t // d // CHUNK
    scale = HEAD_DIM ** -0.5

    def body(q_ref, qn_ref, kp_ref, kc_ref, vp_ref, vc_ref, o_ref, on_ref, do_ref, don_ref, l_ref, ln_ref,
             dq_ref, dk_ref, dv_ref):
        n = pl.program_id(1)
        ok_c, ok_p0, bias_c, bias_p = _dil_masks(d)
        ok_p = ok_p0 & (n > 0)
        ok_n = ok_p0 & (n < nb - 1)
        for h in range(heads):
            sl = slice(h * HEAD_DIM, (h + 1) * HEAD_DIM)
            slope = _alibi_slope(h, heads)
            q, qn, kp, kc, vp, vc = q_ref[:, sl], qn_ref[:, sl], kp_ref[:, sl], kc_ref[:, sl], vp_ref[:, sl], vc_ref[:, sl]
            do, don = do_ref[:, sl], don_ref[:, sl]
            delta = jnp.sum(do.astype(F32) * o_ref[:, sl].astype(F32), axis=1, keepdims=True)
            delta_n = jnp.sum(don.astype(F32) * on_ref[:, sl].astype(F32), axis=1, keepdims=True)
            lt, lt_n = l_ref[:, sl], ln_ref[:, sl]
            p_c = jnp.exp(jnp.where(ok_c, _dot(q, kc, NT) * scale - slope * bias_c, NEG) - lt)
            p_p = jnp.exp(jnp.where(ok_p, _dot(q, kp, NT) * scale - slope * bias_p, NEG) - lt)
            p_n = jnp.exp(jnp.where(ok_n, _dot(qn, kc, NT) * scale - slope * bias_p, NEG) - lt_n)
            ds_c = (p_c * (_dot(do, vc, NT) - delta)).astype(BF16)
            ds_p = (p_p * (_dot(do, vp, NT) - delta)).astype(BF16)
            ds_n = (p_n * (_dot(don, vc, NT) - delta_n)).astype(BF16)
            dq_ref[:, sl] = (_dot(ds_c, kc, NN) + _dot(ds_p, kp, NN)) * scale
            dk_ref[:, sl] = (_dot(ds_c, q, TN) + _dot(ds_n, qn, TN)) * scale
            dv_ref[:, sl] = _dot(p_c.astype(BF16), do, TN) + _dot(p_n.astype(BF16), don, TN)

    def spec(mult, col, shift):
        if shift < 0:
            return pl.BlockSpec((CHUNK, w), lambda r, n: (jnp.maximum(n - 1, 0), r * mult + col))
        if shift > 0:
            return pl.BlockSpec((CHUNK, w), lambda r, n: (jnp.minimum(n + 1, nb - 1), r * mult + col))
        return pl.BlockSpec((CHUNK, w), lambda r, n: (n, r * mult + col))

    zv = z.reshape(t // d, d * 5 * w)
    abv = ab.reshape(t // d, d * 2 * w)
    dabv = dab.reshape(t // d, d * 2 * w)
    lv = ltot.reshape(t // d, d * w)
    ospec = spec(1, 0, 0)
    res = _pcall(
        body, (zv, zv, zv, zv, zv, zv, abv, abv, dabv, dabv, lv, lv), name=name, grid=(d, nb),
        in_specs=[spec(5, 2, 0), spec(5, 2, 1), spec(5, 3, -1), spec(5, 3, 0), spec(5, 4, -1), spec(5, 4, 0),
                  spec(2, 1, 0), spec(2, 1, 1), spec(2, 1, 0), spec(2, 1, 1), spec(1, 0, 0), spec(1, 0, 1)],
        out_specs=[ospec, ospec, ospec], out_shape=[jax.ShapeDtypeStruct((t // d, d * w), F32)] * 3,
        sem=("parallel", "parallel"), comms=comms)
    outs, rws = res if comms else (res, None)
    outs = [o.reshape(t, w) for o in outs]
    return (outs, rws) if comms else outs


def _dz_assemble(duv, parts, name):
    t, a2 = duv.shape
    w = parts[0][0].shape[1]
    nbr = len(parts)

    def body(duv_ref, *rest):
        refs, dz_ref = rest[:-1], rest[-1]
        dz_ref[:, :a2] = duv_ref[...]
        for i in range(3):
            tot = functools.reduce(jnp.add, [refs[b * 3 + i][...] for b in range(nbr)])
            dz_ref[:, a2 + i * w:a2 + (i + 1) * w] = tot.astype(BF16)

    flat = [p for branch in parts for p in branch]
    return pl.pallas_call(
        body, name=name, grid=(t // NORM_ROWS,), in_specs=[_row_spec(a2)] + [_row_spec(w)] * len(flat),
        out_specs=_row_spec(a2 + 3 * w), out_shape=jax.ShapeDtypeStruct((t, a2 + 3 * w), BF16),
        compiler_params=_params("parallel"),
    )(duv, *flat)


def _split_dot(x, m16):
    hi = x.astype(BF16)
    lo = (x - hi.astype(F32)).astype(BF16)
    return _dot(hi, m16, NN) + _dot(lo, m16, NN)


SB_DEAD = -110.0


def _sb_log1m(q, kj, i, j):
    blk = q.shape[0]
    zt = _dot(q, kj, NT) * (HEAD_DIM ** -0.5)
    e = jnp.exp(-jnp.abs(zt))
    rows = lax.broadcasted_iota(jnp.int32, (blk, blk), 0)
    cols = lax.broadcasted_iota(jnp.int32, (blk, blk), 1)
    causal = (j * blk + cols) < (i * blk + rows)
    return zt, e, jnp.where(causal, -(jnp.maximum(zt, 0.0) + jnp.log1p(e)), 0.0), causal


def _sb_beta(zt, e):
    r = 1.0 / (1.0 + e)
    return jnp.where(zt >= 0.0, r, e * r)


def _sb_alive(s, i, c_run):
    return (s <= i) & (jnp.max(c_run) > SB_DEAD)


def _sb_fwd(zc, name, comms=()):
    t = zc.shape[0]
    c = zc.shape[1] // 3
    heads = c // HEAD_DIM
    blk = min(SB_BLOCK, t)

    def body(q_ref, k_ref, v_ref, o_ref, ct_ref, nb_ref):
        i = pl.program_id(1)
        q = q_ref[...]
        rows = lax.broadcasted_iota(jnp.int32, (blk, blk), 0)
        cols = lax.broadcasted_iota(jnp.int32, (blk, blk), 1)
        m_right = (rows > cols).astype(BF16)

        def step(carry):
            s, acc, c_run = carry
            j = i - s
            off = pl.multiple_of(j * blk, blk)
            zt, e, l, causal = _sb_log1m(q, k_ref[pl.ds(off, blk), :], i, j)
            a = jnp.where(causal, _sb_beta(zt, e) * jnp.exp(c_run + _split_dot(l, m_right)), 0.0)
            acc = acc + _dot(a.astype(BF16), v_ref[pl.ds(off, blk), :], NN)
            return s + 1, acc, c_run + jnp.sum(l, axis=1, keepdims=True)

        swept, acc, c_tot = lax.while_loop(lambda carry: _sb_alive(carry[0], i, carry[2]), step,
                                           (jnp.int32(0), jnp.zeros((blk, HEAD_DIM), F32), jnp.zeros((blk, 1), F32)))
        o_ref[...] = acc.astype(BF16)
        ct_ref[...] = jnp.broadcast_to(c_tot, (blk, HEAD_DIM))
        nb_ref[...] = jnp.zeros((blk, HEAD_DIM), F32) + swept.astype(F32)

    qspec = pl.BlockSpec((blk, HEAD_DIM), lambda h, i: (i, h))
    return _pcall(body, (zc, zc, zc), name=name, grid=(heads, t // blk),
                  in_specs=[qspec, pl.BlockSpec((t, HEAD_DIM), lambda h, i: (0, heads + h)),
                            pl.BlockSpec((t, HEAD_DIM), lambda h, i: (0, 2 * heads + h))],
                  out_specs=[qspec, qspec, qspec],
                  out_shape=[jax.ShapeDtypeStruct((t, c), BF16), jax.ShapeDtypeStruct((t, c), F32), jax.ShapeDtypeStruct((t, c), F32)],
                  sem=("parallel", "parallel"), comms=comms)


def _sb_bwd(zc, ctot, swept, do, name, comms=()):
    t = zc.shape[0]
    c = zc.shape[1] // 3
    heads = c // HEAD_DIM
    blk = min(SB_BLOCK, t)
    scale = HEAD_DIM ** -0.5

    def body(q_ref, k_ref, v_ref, ct_ref, nb_ref, do_ref, dq_ref, dk_ref, dv_ref):
        i = pl.program_id(1)

        @pl.when(i == 0)
        def _():
            dk_ref[...] = jnp.zeros_like(dk_ref)
            dv_ref[...] = jnp.zeros_like(dv_ref)

        q = q_ref[...]
        dov = do_ref[...]
        c_tot = ct_ref[:, 0:1]
        n_blocks = jnp.clip(jnp.max(nb_ref[0:8, :]).astype(jnp.int32), 1, i + 1)
        rows = lax.broadcasted_iota(jnp.int32, (blk, blk), 0)
        cols = lax.broadcasted_iota(jnp.int32, (blk, blk), 1)
        m_upto = (rows <= cols).astype(BF16)
        m_left = (rows < cols).astype(BF16)

        def step(j, carry):
            dq, l_run, w_run = carry
            off = pl.multiple_of(j * blk, blk)
            kj = k_ref[pl.ds(off, blk), :]
            vj = v_ref[pl.ds(off, blk), :]
            zt, e, l, causal = _sb_log1m(q, kj, i, j)
            beta = _sb_beta(zt, e)
            a = jnp.where(causal, beta * jnp.exp(c_tot - l_run - _split_dot(l, m_upto)), 0.0)
            wgt = a * _dot(dov, vj, NT)
            before = w_run + _split_dot(wgt, m_left)
            dz = jnp.where(causal, wgt * (1.0 - beta) - beta * before, 0.0) * scale
            dz16 = dz.astype(BF16)
            dk_ref[pl.ds(off, blk), :] += _dot(dz16, q, TN)
            dv_ref[pl.ds(off, blk), :] += _dot(a.astype(BF16), dov, TN)
            return (dq + _dot(dz16, kj, NN), l_run + jnp.sum(l, axis=1, keepdims=True),
                    w_run + jnp.sum(wgt, axis=1, keepdims=True))

        zero = jnp.zeros((blk, 1), F32)
        dq, _, _ = lax.fori_loop(i + 1 - n_blocks, i + 1, step, (jnp.zeros((blk, HEAD_DIM), F32), zero, zero))
        dq_ref[...] = dq

    qspec = pl.BlockSpec((blk, HEAD_DIM), lambda h, i: (i, h))
    full = pl.BlockSpec((t, HEAD_DIM), lambda h, i: (0, h))
    return _pcall(body, (zc, zc, zc, ctot, swept, do), name=name, grid=(heads, t // blk),
                  in_specs=[qspec, pl.BlockSpec((t, HEAD_DIM), lambda h, i: (0, heads + h)),
                            pl.BlockSpec((t, HEAD_DIM), lambda h, i: (0, 2 * heads + h)), qspec, qspec, qspec],
                  out_specs=[qspec, full, full], out_shape=[jax.ShapeDtypeStruct((t, c), F32)] * 3,
                  sem=("arbitrary", "arbitrary"), comms=comms)


def _concat_bf16(parts, name, comms=()):
    t, c = parts[0].shape

    def body(*refs):
        for k, r in enumerate(refs[:-1]):
            refs[-1][:, k * c:(k + 1) * c] = r[...].astype(BF16)

    res = _pcall(body, tuple(parts), name=name, grid=(t // NORM_ROWS,), in_specs=[_row_spec(c)] * len(parts),
                 out_specs=[_row_spec(c * len(parts))], out_shape=[jax.ShapeDtypeStruct((t, c * len(parts)), BF16)],
                 sem=("parallel",), comms=comms)
    return (res[0][0], res[1]) if comms else res[0]


KIND = {"ab_w_in": "col", "ab_w_out": "row", "sb_w_in": "col", "sb_w_out": "row",
        "ffn_w1_0": "col", "ffn_w1_1": "col", "ffn_w2_0": "row", "ffn_w2_1": "row"}
X_Y, DIAG, CHIPS = (2, 4), (6,), (2, 4, 6)


def _local_step(x, target, norms, sgu, big, bufs=None):
    g = {k: [v[l:l + 1] for l in range(2)] for k, v in norms.items()}
    ln_g, ln_b, sgu_w, sgu_b = sgu
    groups = sgu_w.shape[0]
    w16 = sgu_w.astype(BF16)
    bias_b = jnp.broadcast_to(sgu_b[:, :, None], (groups, CHUNK, CHUNK))
    big, dws, psum, dist = dict(big), {}, {}, bufs is not None
    pair, got = (dict(bufs[0]), dict(bufs[1])) if dist else ({}, {})

    def run(fn, *args, ops=(), **kw):
        if not dist or not ops:
            return fn(*args, **kw)
        make = {"gs": lambda k, p: _GatherSend(big[k], KIND[k], p), "gf": lambda k, p: _GatherFwd(big[k], KIND[k], p),
                "swap": lambda k, p: _PairSwap(dws[k], pair[k], KIND[k]), "chips": lambda k, p: _ChipScatter(psum[k], got[k], p)}
        out, rws = fn(*args, comms=[make[op](k, p) for op, k, p in ops], **kw)
        for (op, k, _), r in zip(ops, rws):
            if op in ("gs", "gf"):
                big[k] = r[0]
            elif op == "swap":
                psum[k] = _pair_sum(dws[k], r[0], KIND[k], f"pair_sum_{k}")
            else:
                got[k] = r[0]
        return out

    h1_0 = _rms_fwd(x, g["pre_mix"][0], "rms_in")
    z0 = run(_matmul, h1_0, big["ab_w_in"], "nn", BF16, "ab_in", ops=[("gs", "ffn_w1_0", X_Y)])
    a_out = run(_sgu_fwd, z0, ln_g, ln_b, w16, bias_b, "sgu_fwd", ops=[("gf", "ffn_w1_0", X_Y), ("gs", "ab_w_out", X_Y)])
    branches = [run(_dil_fwd, z0, 1, "dil_fwd_1", ops=[("gs", "ffn_w1_0", DIAG), ("gf", "ab_w_out", X_Y)]),
                run(_dil_fwd, z0, 4, "dil_fwd_4", ops=[("gf", "ffn_w1_0", DIAG), ("gs", "ab_w_out", DIAG), ("gs", "ffn_w2_0", X_Y)]),
                run(_dil_fwd, z0, 16, "dil_fwd_16", ops=[("gf", "ab_w_out", DIAG), ("gs", "ffn_w2_0", DIAG)])]
    ab, ltot = run(_dil_merge, a_out, [b[0] for b in branches], [b[1] for b in branches], "dil_merge", ops=[("gf", "ffn_w2_0", CHIPS)])
    y_0 = _matmul(ab, big["ab_w_out"], "nn", F32, "ab_out")
    x1, h2_0 = _post_pre_fwd(y_0, g["post_mix"][0], x, g["pre_ffn"][0], "norm_mix0")
    r_0 = run(_matmul, h2_0, big["ffn_w1_0"], "nn", BF16, "ffn_up_0", relu_out=True,
              ops=[("gs", "sb_w_in", CHIPS), ("gs", "sb_w_out", CHIPS)])
    y2_0 = run(_matmul, r_0, big["ffn_w2_0"], "nn", F32, "ffn_down_0", a_square=True,
               ops=[("gf", "sb_w_in", CHIPS), ("gf", "sb_w_out", CHIPS), ("gs", "ffn_w1_1", X_Y)])
    x2, h1_1 = run(_post_pre_fwd, y2_0, g["post_ffn"][0], x1, g["pre_mix"][1], "norm_ffn0", ops=[("gf", "ffn_w1_1", X_Y)])
    zc = run(_matmul, h1_1, big["sb_w_in"], "nn", BF16, "sb_in", ops=[("gs", "ffn_w1_1", DIAG)])
    o_sb, ct_sb, nb_sb = run(_sb_fwd, zc, "sb_fwd", ops=[("gf", "ffn_w1_1", DIAG), ("gs", "ffn_w2_1", CHIPS)])
    y_1 = run(_matmul, o_sb, big["sb_w_out"], "nn", F32, "sb_out", ops=[("gf", "ffn_w2_1", CHIPS)])
    x3, h2_1 = _post_pre_fwd(y_1, g["post_mix"][1], x2, g["pre_ffn"][1], "norm_mix1")
    r_1 = _matmul(h2_1, big["ffn_w1_1"], "nn", BF16, "ffn_up_1", relu_out=True)
    y2_1 = _matmul(r_1, big["ffn_w2_1"], "nn", F32, "ffn_down_1", a_square=True)
    loss, dx4, dy2_1, dg_post_ffn1 = _final_fwd_bwd(y2_1, g["post_ffn"][1], x3, target, "loss")

    da = _matmul(dy2_1, big["ffn_w2_1"], "nt", BF16, "ffn_da_1", mul2=r_1)
    dws["ffn_w2_1"] = _matmul(r_1, dy2_1, "tn", BF16, "ffn_dw2_1", a_square=True)
    dh2 = run(_matmul, da, big["ffn_w1_1"], "nt", F32, "ffn_dh_1", ops=[("swap", "ffn_w2_1", None)])
    dws["ffn_w1_1"] = run(_matmul, h2_1, da, "tn", BF16, "ffn_dw1_1", ops=[("chips", "ffn_w2_1", X_Y)])
    dx3, dy_1, dg_pre_ffn1, dg_post_mix1 = run(_pre_post_bwd, x3, g["pre_ffn"][1], dh2, dx4, y_1, g["post_mix"][1], "norm_bwd_mix1",
                                               ops=[("swap", "ffn_w1_1", None)])
    do_sb = _matmul(dy_1, big["sb_w_out"], "nt", BF16, "sb_out_dx")
    dws["sb_w_out"] = _matmul(o_sb, dy_1, "tn", BF16, "sb_out_dw")
    dqkv = run(_sb_bwd, zc, ct_sb, nb_sb, do_sb, "sb_bwd",
               ops=[("chips", "ffn_w2_1", DIAG), ("chips", "ffn_w1_1", CHIPS), ("swap", "sb_w_out", None)])
    dzc = run(_concat_bf16, dqkv, "sb_dz", ops=[("chips", "sb_w_out", X_Y)])
    dh1 = run(_matmul, dzc, big["sb_w_in"], "nt", F32, "sb_in_dx", ops=[("chips", "sb_w_out", DIAG)])
    dws["sb_w_in"] = _matmul(h1_1, dzc, "tn", BF16, "sb_in_dw")
    dx2, dy2_0, dg_pre_mix1, dg_post_ffn0 = run(_pre_post_bwd, x2, g["pre_mix"][1], dh1, dx3, y2_0, g["post_ffn"][0], "norm_bwd_ffn0",
                                                ops=[("swap", "sb_w_in", None)])
    da = run(_matmul, dy2_0, big["ffn_w2_0"], "nt", BF16, "ffn_da_0", mul2=r_0, ops=[("chips", "sb_w_in", X_Y)])
    dws["ffn_w2_0"] = run(_matmul, r_0, dy2_0, "tn", BF16, "ffn_dw2_0", a_square=True, ops=[("chips", "sb_w_in", DIAG)])
    dws["ffn_w1_0"] = run(_matmul, h2_0, da, "tn", BF16, "ffn_dw1_0", ops=[("swap", "ffn_w2_0", None)])
    dh2 = run(_matmul, da, big["ffn_w1_0"], "nt", F32, "ffn_dh_0", ops=[("chips", "ffn_w2_0", X_Y), ("swap", "ffn_w1_0", None)])
    dx1, dy_0, dg_pre_ffn0, dg_post_mix0 = run(_pre_post_bwd, x1, g["pre_ffn"][0], dh2, dx2, y_0, g["post_mix"][0], "norm_bwd_mix0",
                                               ops=[("chips", "ffn_w2_0", DIAG)])
    dab = run(_matmul, dy_0, big["ab_w_out"], "nt", BF16, "ab_out_dx", ops=[("chips", "ffn_w1_0", (2,))])
    dws["ab_w_out"] = run(_matmul, ab, dy_0, "tn", BF16, "ab_out_dw", ops=[("chips", "ffn_w1_0", (4,))])
    duv, d_ln_g, d_ln_b, d_sgu_w, d_sgu_b = run(_sgu_bwd, z0, dab, ln_g, ln_b, w16, bias_b, "sgu_bwd", ops=[("swap", "ab_w_out", None)])
    parts = [run(_dil_bwd, z0, ab, dab, ltot, 1, "dil_bwd_1", ops=[("chips", "ffn_w1_0", DIAG)]),
             run(_dil_bwd, z0, ab, dab, ltot, 4, "dil_bwd_4", ops=[("chips", "ab_w_out", CHIPS)]),
             _dil_bwd(z0, ab, dab, ltot, 16, "dil_bwd_16")]
    dz0 = _dz_assemble(duv, parts, "dz_assemble")
    dws["ab_w_in"] = _matmul(h1_0, dz0, "tn", BF16, "ab_in_dw")
    dh1 = run(_matmul, dz0, big["ab_w_in"], "nt", F32, "ab_in_dx", ops=[("swap", "ab_w_in", None)])
    grad_x, dg_pre_mix0 = run(_pre_post_bwd, x, g["pre_mix"][0], dh1, dx1, None, None, "norm_bwd_in", ops=[("chips", "ab_w_in", X_Y)])
    if dist:
        got["ab_w_in"] = _comm_call([_ChipScatter(psum["ab_w_in"], got["ab_w_in"], DIAG)], "scatter_last")[0][0]

    d_norms = {
        "pre_mix": jnp.concatenate([dg_pre_mix0, dg_pre_mix1]), "post_mix": jnp.concatenate([dg_post_mix0, dg_post_mix1]),
        "pre_ffn": jnp.concatenate([dg_pre_ffn0, dg_pre_ffn1]), "post_ffn": jnp.concatenate([dg_post_ffn0, dg_post_ffn1]),
    }
    return loss, grad_x, d_norms, (d_ln_g, d_ln_b, d_sgu_w, d_sgu_b), (psum, got) if dist else dws


def _to_bf16_full(w, layer, kind, name):
    _, rows, cols = w.shape
    tr = _tile(rows, 512)
    nblk = rows // tr
    full = (rows, 4 * cols) if kind == "col" else (4 * rows, cols)

    def body(w_ref, o_ref):
        o_ref[...] = w_ref[...].astype(BF16)

    def place(i):
        mine = 2 * lax.axis_index("x") + lax.axis_index("y")
        return (i, mine) if kind == "col" else (mine * nblk + i, 0)

    return pl.pallas_call(
        body, name=name, grid=(nblk,), in_specs=[pl.BlockSpec((None, tr, cols), lambda i: (layer, i, 0))],
        out_specs=pl.BlockSpec((tr, cols), place), out_shape=jax.ShapeDtypeStruct(full, BF16), compiler_params=_params("parallel"),
    )(w)


def _pair_sum(dw16, pair, kind, name):
    rh, cs = _half_shape(dw16.shape, kind)
    tr = _tile(rh, 256)
    nblk = rh // tr

    def body(dw_ref, pair_ref, o_ref):
        o_ref[...] = (dw_ref[...].astype(F32) + pair_ref[...].astype(F32)).astype(BF16)

    def own(s, i):
        c = lax.axis_index("c")
        return (c * nblk + i, s) if kind == "col" else ((2 * s + c) * nblk + i, 0)

    spec3 = pl.BlockSpec((None, tr, cs), lambda s, i: (s, i, 0))
    return pl.pallas_call(
        body, name=name, grid=(4, nblk), in_specs=[pl.BlockSpec((tr, cs), own), spec3], out_specs=spec3,
        out_shape=jax.ShapeDtypeStruct((4, rh, cs), BF16), compiler_params=_params("parallel", "parallel"),
    )(dw16, pair)


def _owner_sum(psum, got, buf, layer, name):
    _, rh, cs = psum.shape
    tr = _tile(rh, 256)

    def body(p_ref, got_ref, buf_ref, o_ref):
        tot = p_ref[...].astype(F32)
        for j in range(3):
            tot = tot + got_ref[j].astype(F32)
        o_ref[...] = tot

    return pl.pallas_call(
        body, name=name, grid=(rh // tr,),
        in_specs=[pl.BlockSpec((None, tr, cs), lambda i: (2 * lax.axis_index("x") + lax.axis_index("y"), i, 0)),
                  pl.BlockSpec((3, tr, cs), lambda i: (0, i, 0)), ANY],
        out_specs=pl.BlockSpec((None, None, tr, cs), lambda i: (layer, lax.axis_index("c"), i, 0)),
        out_shape=jax.ShapeDtypeStruct(buf.shape, F32), input_output_aliases={2: 0}, compiler_params=_params("parallel"),
    )(psum, got, buf)


def _adamw_math(w, g, m, v):
    m = ADAM_B1 * m + (1.0 - ADAM_B1) * g
    v = ADAM_B2 * v + (1.0 - ADAM_B2) * (g * g)
    m_hat = m / (1.0 - ADAM_B1 ** ADAM_STEP)
    v_hat = v / (1.0 - ADAM_B2 ** ADAM_STEP)
    return -ADAM_LR * (m_hat / (jnp.sqrt(v_hat) + ADAM_EPS) + ADAM_WD * w), m, v


def _adamw(w, g, m, v, name):
    layers, rows, cols = w.shape
    tr = _tile(rows, 256)

    def body(w_ref, g_ref, m_ref, v_ref, d_ref, mo_ref, vo_ref):
        d_ref[...], mo_ref[...], vo_ref[...] = _adamw_math(w_ref[...], g_ref[...], m_ref[...], v_ref[...])

    spec = pl.BlockSpec((None, tr, cols), lambda l, i: (l, i, 0))
    return pl.pallas_call(body, name=name, grid=(layers, rows // tr), in_specs=[spec] * 4, out_specs=[spec] * 3,
                          out_shape=[jax.ShapeDtypeStruct(w.shape, F32)] * 3, compiler_params=_params("parallel", "parallel"))(w, g, m, v)


def _pack(arrays):
    flat = jnp.concatenate([a.reshape(-1) for a in arrays])
    pad = (-flat.shape[0]) % 1024
    return jnp.pad(flat, (0, pad)).reshape(-1, 128)


def _unpack(packed, like):
    flat = packed.reshape(-1)
    out, off = [], 0
    for a in like:
        out.append(flat[off:off + a.size].reshape(a.shape))
        off += a.size
    return out


def _gather_small(g, name):
    rows = g.shape[0]

    def body(g_ref, o_ref, send, recv, local_sem):
        x, y, c, _ = _place()
        me = 4 * x + 2 * y + c
        local = pltpu.make_async_copy(g_ref, o_ref.at[me], local_sem)
        local.start()
        copies = []
        for j in range(1, 8):
            px = 1 - x if j & 4 else x
            py = 1 - y if j & 2 else y
            pc = 1 - c if j & 1 else c
            copies.append(pltpu.make_async_remote_copy(src_ref=g_ref, dst_ref=o_ref.at[me], send_sem=send.at[j - 1],
                                                       recv_sem=recv.at[j - 1], device_id=(px, py, pc), device_id_type=MESH))
        for cp in copies:
            cp.start()
        for j in range(1, 8):
            px = 1 - x if j & 4 else x
            py = 1 - y if j & 2 else y
            pc = 1 - c if j & 1 else c
            pltpu.make_async_remote_copy(src_ref=g_ref, dst_ref=o_ref.at[4 * px + 2 * py + pc], send_sem=send.at[j - 1],
                                         recv_sem=recv.at[j - 1], device_id=(px, py, pc), device_id_type=MESH).wait_recv()
        for cp in copies:
            cp.wait_send()
        local.wait()

    vmem = pl.BlockSpec(memory_space=pltpu.VMEM)
    return pl.pallas_call(
        body, name=name, in_specs=[vmem], out_specs=vmem, out_shape=jax.ShapeDtypeStruct((8, rows, 128), F32),
        scratch_shapes=[pltpu.SemaphoreType.DMA((7,)), pltpu.SemaphoreType.DMA((7,)), pltpu.SemaphoreType.DMA(())],
        compiler_params=_params(),
    )(g)


def _small_update(parts, w, m, v, name):
    rows = w.shape[0]

    def body(p_ref, w_ref, m_ref, v_ref, g_ref, d_ref, mo_ref, vo_ref):
        g = p_ref[0]
        for k in range(1, 8):
            g = g + p_ref[k]
        g_ref[...] = g
        d_ref[...], mo_ref[...], vo_ref[...] = _adamw_math(w_ref[...], g, m_ref[...], v_ref[...])

    return pl.pallas_call(body, name=name, out_shape=[jax.ShapeDtypeStruct((rows, 128), F32)] * 4, compiler_params=_params())(parts, w, m, v)


SMALL = ("norm_pre_mix", "norm_post_mix", "norm_pre_ffn", "norm_post_ffn", "sgu_ln_g", "sgu_ln_b", "sgu_w", "sgu_b")
BIG = (("ab_w_in", ("ab_w_in",)), ("ab_w_out", ("ab_w_out",)), ("sb_w_in", ("sb_w_in",)), ("sb_w_out", ("sb_w_out",)),
       ("ffn_w1", ("ffn_w1_0", "ffn_w1_1")), ("ffn_w2", ("ffn_w2_0", "ffn_w2_1")))
WEIGHTS = ("norm_pre_mix", "norm_post_mix", "norm_pre_ffn", "norm_post_ffn", "ab_w_in", "sgu_ln_g", "sgu_ln_b", "sgu_w", "sgu_b",
           "ab_w_out", "sb_w_in", "sb_w_out", "ffn_w1", "ffn_w2")


def kernel(x, norm_pre_mix, norm_post_mix, norm_pre_ffn, norm_post_ffn, ab_w_in, sgu_ln_g, sgu_ln_b, sgu_w, sgu_b, ab_w_out, sb_w_in, sb_w_out, ffn_w1, ffn_w2, loss_target, m_norm_pre_mix, m_norm_post_mix, m_norm_pre_ffn, m_norm_post_ffn, m_ab_w_in, m_sgu_ln_g, m_sgu_ln_b, m_sgu_w, m_sgu_b, m_ab_w_out, m_sb_w_in, m_sb_w_out, m_ffn_w1, m_ffn_w2, v_norm_pre_mix, v_norm_post_mix, v_norm_pre_ffn, v_norm_post_ffn, v_ab_w_in, v_sgu_ln_g, v_sgu_ln_b, v_sgu_w, v_sgu_b, v_ab_w_out, v_sb_w_in, v_sb_w_out, v_ffn_w1, v_ffn_w2):
    w = dict(norm_pre_mix=norm_pre_mix, norm_post_mix=norm_post_mix, norm_pre_ffn=norm_pre_ffn, norm_post_ffn=norm_post_ffn,
             ab_w_in=ab_w_in, sgu_ln_g=sgu_ln_g, sgu_ln_b=sgu_ln_b, sgu_w=sgu_w, sgu_b=sgu_b, ab_w_out=ab_w_out, sb_w_in=sb_w_in,
             sb_w_out=sb_w_out, ffn_w1=ffn_w1, ffn_w2=ffn_w2)
    m = dict(norm_pre_mix=m_norm_pre_mix, norm_post_mix=m_norm_post_mix, norm_pre_ffn=m_norm_pre_ffn, norm_post_ffn=m_norm_post_ffn,
             ab_w_in=m_ab_w_in, sgu_ln_g=m_sgu_ln_g, sgu_ln_b=m_sgu_ln_b, sgu_w=m_sgu_w, sgu_b=m_sgu_b, ab_w_out=m_ab_w_out,
             sb_w_in=m_sb_w_in, sb_w_out=m_sb_w_out, ffn_w1=m_ffn_w1, ffn_w2=m_ffn_w2)
    v = dict(norm_pre_mix=v_norm_pre_mix, norm_post_mix=v_norm_post_mix, norm_pre_ffn=v_norm_pre_ffn, norm_post_ffn=v_norm_post_ffn,
             ab_w_in=v_ab_w_in, sgu_ln_g=v_sgu_ln_g, sgu_ln_b=v_sgu_ln_b, sgu_w=v_sgu_w, sgu_b=v_sgu_b, ab_w_out=v_ab_w_out,
             sb_w_in=v_sb_w_in, sb_w_out=v_sb_w_out, ffn_w1=v_ffn_w1, ffn_w2=v_ffn_w2)
    big, pair, got = {}, {}, {}
    for name, keys in BIG:
        for layer, key in enumerate(keys):
            big[key] = _to_bf16_full(w[name], layer, KIND[key], f"bf16_{key}")
            half = _half_shape(big[key].shape, KIND[key])
            pair[key], got[key] = lax.empty((4,) + half, BF16), lax.empty((3,) + half, BF16)
    big["ab_w_in"] = _comm_call([_Gather(big["ab_w_in"], KIND["ab_w_in"])], "gather_first")[0][0]

    norms = {k: w["norm_" + k] for k in ("pre_mix", "post_mix", "pre_ffn", "post_ffn")}
    sgu = (sgu_ln_g, sgu_ln_b, sgu_w[0], sgu_b[0])
    loss_blk, grad_x, d_norms, d_sgu, (psum, got) = _local_step(x[0], loss_target[0], norms, sgu, big, (pair, got))
    loss = lax.psum(loss_blk[0, 0], ("x", "y", "c"))

    grads, deltas, new_m, new_v = {}, {}, {}, {}
    small_g = [d_norms["pre_mix"], d_norms["post_mix"], d_norms["pre_ffn"], d_norms["post_ffn"],
               d_sgu[0], d_sgu[1], d_sgu[2][None], d_sgu[3][None]]
    gathered = _gather_small(_pack(small_g), "gather_small")
    outs = _small_update(gathered, _pack([w[k] for k in SMALL]), _pack([m[k] for k in SMALL]), _pack([v[k] for k in SMALL]), "small_update")
    like = [w[k] for k in SMALL]
    for dst, packed in zip((grads, deltas, new_m, new_v), outs):
        for k, a in zip(SMALL, _unpack(packed, like)):
            dst[k] = a

    bufs = []
    for name, keys in BIG:
        buf = lax.empty((len(keys), 2) + psum[keys[0]].shape[1:], F32)
        for layer, key in enumerate(keys):
            buf = _owner_sum(psum[key], got[key], buf, layer, f"sum_{key}")
        bufs.append(buf)
    joined = _comm_call([_Join(bufs)], "join")[0]
    for (name, _), buf in zip(BIG, joined):
        grads[name] = buf.reshape(w[name].shape)
        deltas[name], new_m[name], new_v[name] = _adamw(w[name], grads[name], m[name], v[name], f"adamw_{name}")

    return (loss, grad_x[None], *[grads[k] for k in WEIGHTS], *[deltas[k] for k in WEIGHTS],
            *[new_m[k] for k in WEIGHTS], *[new_v[k] for k in WEIGHTS])
```

```python
import functools

import jax
import jax.numpy as jnp
from jax import lax
from jax.experimental import pallas as pl
from jax.experimental.pallas import tpu as pltpu

F32 = jnp.float32
BF16 = jnp.bfloat16
MESH = pl.DeviceIdType.MESH

HEAD_DIM = 128
CHUNK = 128
DILATIONS = (1, 4, 16)
SB_BLOCK = 256
RMS_EPS = 1e-6
LN_EPS = 1e-5
ADAM_LR, ADAM_B1, ADAM_B2, ADAM_EPS, ADAM_WD, ADAM_STEP = 0.001, 0.9, 0.999, 1e-08, 0.01, 10
NEG = -1e30
V7X_VMEM_LIMIT = 48 * 1024 * 1024
ANY = pl.BlockSpec(memory_space=pl.ANY)


def _params(*sem):
    return pltpu.CompilerParams(dimension_semantics=sem if sem else None, vmem_limit_bytes=V7X_VMEM_LIMIT)


def _tile(n, pref):
    if n <= pref:
        return n
    t = pref
    while n % t:
        t -= 128
    return t


def _dot(a, b, dims):
    return lax.dot_general(a, b, (dims, ((), ())), preferred_element_type=F32)


NN = ((1,), (0,))
NT = ((1,), (1,))
TN = ((0,), (0,))


def _place():
    x, y, c = lax.axis_index("x"), lax.axis_index("y"), lax.axis_index("c")
    return x, y, c, 2 * x + y


def _flip(x, y, c, j):
    return (1 - x if j & 4 else x), (1 - y if j & 2 else y), (1 - c if j & 1 else c)


def _half_shape(full_shape, kind):
    rows, cols = full_shape
    return (rows // 2, cols // 4) if kind == "col" else (rows // 8, cols)


def _half(ref, kind, s, h):
    rh, cs = _half_shape(ref.shape, kind)
    if kind == "col":
        return ref.at[pl.ds(h * rh, rh), pl.ds(s * cs, cs)]
    return ref.at[pl.ds((2 * s + h) * rh, rh), :]


def _remote(src, dst, send, recv, to):
    return pltpu.make_async_remote_copy(src_ref=src, dst_ref=dst, send_sem=send, recv_sem=recv, device_id=to, device_id_type=MESH)


class _Gather:
    n_sems = 6

    def __init__(self, full, kind):
        self.ro, self.rw, self.kind = [], [full], kind

    def start(self, ro, rw, send, recv):
        x, y, c, mine = _place()
        own = _half(rw[0], self.kind, mine, c)
        for k, j in enumerate((2, 4, 6)):
            px, py, _ = _flip(x, y, c, j)
            _remote(own, own, send(k), recv(k), (px, py, c)).start()

    def finish(self, ro, rw, send, recv):
        x, y, c, mine = _place()
        own = _half(rw[0], self.kind, mine, c)
        for k, j in enumerate((2, 4, 6)):
            px, py, _ = _flip(x, y, c, j)
            got = _half(rw[0], self.kind, 2 * px + py, c)
            _remote(got, got, send(k), recv(k), (x, y, c)).wait_recv()
            _remote(got, got, send(3 + k), recv(3 + k), (x, y, 1 - c)).start()
        for k, j in enumerate((2, 4, 6)):
            px, py, _ = _flip(x, y, c, j)
            got = _half(rw[0], self.kind, 2 * px + py, 1 - c)
            _remote(got, got, send(3 + k), recv(3 + k), (x, y, c)).wait_recv()
        for k in range(6):
            _remote(own, own, send(k), recv(k), (x, y, c)).wait_send()


class _GatherSend:
    def __init__(self, full, kind, patterns):
        self.ro, self.rw, self.kind, self.patterns, self.n_sems = [], [full], kind, patterns, len(patterns)

    def start(self, ro, rw, send, recv):
        x, y, c, mine = _place()
        own = _half(rw[0], self.kind, mine, c)
        for k, j in enumerate(self.patterns):
            px, py, _ = _flip(x, y, c, j)
            _remote(own, own, send(k), recv(k), (px, py, c)).start()

    def finish(self, ro, rw, send, recv):
        x, y, c, _ = _place()
        for k, j in enumerate(self.patterns):
            px, py, _ = _flip(x, y, c, j)
            got = _half(rw[0], self.kind, 2 * px + py, c)
            cp = _remote(got, got, send(k), recv(k), (x, y, c))
            cp.wait_recv()
            cp.wait_send()


class _GatherFwd:
    def __init__(self, full, kind, patterns):
        self.ro, self.rw, self.kind, self.patterns, self.n_sems = [], [full], kind, patterns, len(patterns)

    def start(self, ro, rw, send, recv):
        x, y, c, _ = _place()
        for k, j in enumerate(self.patterns):
            px, py, _ = _flip(x, y, c, j)
            got = _half(rw[0], self.kind, 2 * px + py, c)
            _remote(got, got, send(k), recv(k), (x, y, 1 - c)).start()

    def finish(self, ro, rw, send, recv):
        x, y, c, _ = _place()
        for k, j in enumerate(self.patterns):
            px, py, _ = _flip(x, y, c, j)
            got = _half(rw[0], self.kind, 2 * px + py, 1 - c)
            cp = _remote(got, got, send(k), recv(k), (x, y, c))
            cp.wait_recv()
            cp.wait_send()


class _PairSwap:
    n_sems = 4

    def __init__(self, dw16, pair, kind):
        self.ro, self.rw, self.kind = [dw16], [pair], kind

    def start(self, ro, rw, send, recv):
        x, y, c, _ = _place()
        for s in range(4):
            _remote(_half(ro[0], self.kind, s, 1 - c), rw[0].at[s], send(s), recv(s), (x, y, 1 - c)).start()

    def finish(self, ro, rw, send, recv):
        x, y, c, _ = _place()
        for s in range(4):
            cp = _remote(rw[0].at[s], rw[0].at[s], send(s), recv(s), (x, y, c))
            cp.wait_recv()
            cp.wait_send()


class _ChipScatter:
    def __init__(self, psum, got, patterns):
        self.ro, self.rw, self.patterns, self.n_sems = [psum], [got], patterns, len(patterns)

    def start(self, ro, rw, send, recv):
        x, y, c, _ = _place()
        for k, j in enumerate(self.patterns):
            px, py, _ = _flip(x, y, c, j)
            _remote(ro[0].at[2 * px + py], rw[0].at[j // 2 - 1], send(k), recv(k), (px, py, c)).start()

    def finish(self, ro, rw, send, recv):
        x, y, c, _ = _place()
        for k, j in enumerate(self.patterns):
            slot = rw[0].at[j // 2 - 1]
            cp = _remote(slot, slot, send(k), recv(k), (x, y, c))
            cp.wait_recv()
            cp.wait_send()


class _Join:
    def __init__(self, bufs):
        self.ro, self.rw, self.n_sems = [], list(bufs), sum(b.shape[0] for b in bufs)

    def _copies(self, rw, send, recv, slot):
        x, y, c, _ = _place()
        k = 0
        for ref in rw:
            for l in range(ref.shape[0]):
                yield _remote(ref.at[l, c], ref.at[l, slot(c)], send(k), recv(k), (x, y, 1 - c))
                k += 1

    def start(self, ro, rw, send, recv):
        for cp in self._copies(rw, send, recv, lambda c: c):
            cp.start()

    def finish(self, ro, rw, send, recv):
        for cp in self._copies(rw, send, recv, lambda c: 1 - c):
            cp.wait_recv()
        for cp in self._copies(rw, send, recv, lambda c: c):
            cp.wait_send()


def _comm_layout(comms):
    ro = [a for c in comms for a in c.ro]
    rw = [a for c in comms for a in c.rw]
    return ro, rw, sum(c.n_sems for c in comms)


def _comm_each(comms, method, ro_refs, rw_refs, send, recv):
    i_ro = i_rw = i_sem = 0
    for c in comms:
        getattr(c, method)(ro_refs[i_ro:i_ro + len(c.ro)], rw_refs[i_rw:i_rw + len(c.rw)],
                           lambda k, b=i_sem: send.at[b + k], lambda k, b=i_sem: recv.at[b + k])
        i_ro, i_rw, i_sem = i_ro + len(c.ro), i_rw + len(c.rw), i_sem + c.n_sems


def _split_results(comms, rws):
    out, i = [], 0
    for c in comms:
        out.append(list(rws[i:i + len(c.rw)]))
        i += len(c.rw)
    return out


def _comm_call(comms, name):
    ro, rw, n_sems = _comm_layout(comms)

    def body(*refs):
        ro_refs = refs[:len(ro)]
        rw_refs = refs[len(ro) + len(rw):len(ro) + 2 * len(rw)]
        send, recv = refs[len(ro) + 2 * len(rw):]
        _comm_each(comms, "start", ro_refs, rw_refs, send, recv)
        _comm_each(comms, "finish", ro_refs, rw_refs, send, recv)

    rws = pl.pallas_call(
        body, name=name, in_specs=[ANY] * (len(ro) + len(rw)), out_specs=[ANY] * len(rw),
        out_shape=[jax.ShapeDtypeStruct(a.shape, a.dtype) for a in rw],
        input_output_aliases={len(ro) + k: k for k in range(len(rw))},
        scratch_shapes=[pltpu.SemaphoreType.DMA((n_sems,)), pltpu.SemaphoreType.DMA((n_sems,))],
    )(*ro, *rw)
    return _split_results(comms, rws)


def _pcall(body, args, *, name, grid, in_specs, out_specs, out_shape, scratch=(), sem=(), comms=()):
    n_in, n_out, n_scr = len(in_specs), len(out_specs), len(scratch)
    if not comms:
        return pl.pallas_call(body, name=name, grid=grid, in_specs=list(in_specs), out_specs=list(out_specs),
                              out_shape=list(out_shape), scratch_shapes=list(scratch), compiler_params=_params(*sem))(*args)
    ro, rw, n_sems = _comm_layout(comms)

    def carrier(*refs):
        ins = refs[:n_in]
        ro_refs = refs[n_in:n_in + len(ro)]
        o0 = n_in + len(ro) + len(rw)
        outs = refs[o0:o0 + n_out]
        rw_refs = refs[o0 + n_out:o0 + n_out + len(rw)]
        s0 = o0 + n_out + len(rw)
        send, recv = refs[s0 + n_scr], refs[s0 + n_scr + 1]
        ids = [pl.program_id(a) for a in range(len(grid))]
        first = functools.reduce(jnp.logical_and, [i == 0 for i in ids])
        last = functools.reduce(jnp.logical_and, [i == g - 1 for i, g in zip(ids, grid)])

        @pl.when(first)
        def _():
            _comm_each(comms, "start", ro_refs, rw_refs, send, recv)

        body(*ins, *outs, *refs[s0:s0 + n_scr])

        @pl.when(last)
        def _():
            _comm_each(comms, "finish", ro_refs, rw_refs, send, recv)

    res = pl.pallas_call(
        carrier, name=name, grid=grid, in_specs=list(in_specs) + [ANY] * (len(ro) + len(rw)),
        out_specs=list(out_specs) + [ANY] * len(rw),
        out_shape=list(out_shape) + [jax.ShapeDtypeStruct(a.shape, a.dtype) for a in rw],
        input_output_aliases={n_in + len(ro) + k: n_out + k for k in range(len(rw))},
        scratch_shapes=list(scratch) + [pltpu.SemaphoreType.DMA((n_sems,)), pltpu.SemaphoreType.DMA((n_sems,))],
        compiler_params=_params(*["arbitrary"] * len(grid)),
    )(*args, *ro, *rw)
    return list(res[:n_out]), _split_results(comms, res[n_out:])


def _matmul(a, b, mode, out_dtype, name, a_square=False, relu_out=False, mul2=None, comms=()):
    if mode == "nn":
        (m, k), n = a.shape, b.shape[1]
    elif mode == "nt":
        (m, k), n = a.shape, b.shape[0]
    else:
        (k, m), n = a.shape, b.shape[1]
    tm, tn, tk = _tile(m, 1024), _tile(n, 1024), _tile(k, 2048)
    nk = k // tk
    dims = {"nn": NN, "nt": NT, "tn": TN}[mode]
    a_spec = pl.BlockSpec((tk, tm), lambda i, j, kk: (kk, i)) if mode == "tn" else pl.BlockSpec((tm, tk), lambda i, j, kk: (i, kk))
    b_spec = pl.BlockSpec((tn, tk), lambda i, j, kk: (j, kk)) if mode == "nt" else pl.BlockSpec((tk, tn), lambda i, j, kk: (kk, j))
    o_spec = pl.BlockSpec((tm, tn), lambda i, j, kk: (i, j))

    def body(a_ref, b_ref, *rest):
        m_ref = None if mul2 is None else rest[0]
        o_ref = rest[0 if mul2 is None else 1]
        kk = pl.program_id(2)

        def partial():
            av = a_ref[...]
            if a_square:
                av = av * av
            return _dot(av, b_ref[...], dims)

        def finish(r):
            if relu_out:
                r = jnp.maximum(r, 0.0)
            if mul2 is not None:
                r = r * (2.0 * m_ref[...].astype(F32))
            o_ref[...] = r.astype(out_dtype)

        if nk == 1:
            finish(partial())
            return
        acc_ref = rest[-1]

        @pl.when(kk == 0)
        def _():
            acc_ref[...] = partial()

        @pl.when(kk > 0)
        def _():
            acc_ref[...] += partial()

        @pl.when(kk == nk - 1)
        def _():
            finish(acc_ref[...])

    args = (a, b) if mul2 is None else (a, b, mul2)
    specs = [a_spec, b_spec] + ([] if mul2 is None else [o_spec])
    res = _pcall(body, args, name=name, grid=(m // tm, n // tn, nk), in_specs=specs, out_specs=[o_spec],
                 out_shape=[jax.ShapeDtypeStruct((m, n), out_dtype)], scratch=[pltpu.VMEM((tm, tn), F32)] if nk > 1 else [],
                 sem=("parallel", "parallel", "arbitrary"), comms=comms)
    return (res[0][0], res[1]) if comms else res[0]


NORM_ROWS = 256


def _rms(x, g):
    rstd = lax.rsqrt(jnp.mean(x * x, axis=-1, keepdims=True) + RMS_EPS)
    n = x * rstd
    return n * g, n, rstd


def _rms_bwd(n, rstd, g, dout):
    dn = dout * g
    return rstd * (dn - n * jnp.mean(dn * n, axis=-1, keepdims=True))


def _row_spec(d):
    return pl.BlockSpec((NORM_ROWS, d), lambda i: (i, 0))


def _vec_spec(d):
    return pl.BlockSpec((1, d), lambda i: (0, 0))


def _accumulate(ref, val):
    @pl.when(pl.program_id(0) == 0)
    def _():
        ref[...] = jnp.zeros_like(ref)

    ref[...] += val


def _rms_fwd(x, g, name):
    t, d = x.shape

    def body(x_ref, g_ref, h_ref):
        h_ref[...] = _rms(x_ref[...], g_ref[...])[0].astype(BF16)

    return pl.pallas_call(
        body, name=name, grid=(t // NORM_ROWS,), in_specs=[_row_spec(d), _vec_spec(d)], out_specs=_row_spec(d),
        out_shape=jax.ShapeDtypeStruct((t, d), BF16), compiler_params=_params("parallel"),
    )(x, g)


def _post_pre_fwd(y, g_post, x, g_pre, name, comms=()):
    t, d = x.shape

    def body(y_ref, gp_ref, x_ref, gn_ref, xn_ref, h_ref):
        xn = x_ref[...] + _rms(y_ref[...], gp_ref[...])[0]
        xn_ref[...] = xn
        h_ref[...] = _rms(xn, gn_ref[...])[0].astype(BF16)

    return _pcall(
        body, (y, g_post, x, g_pre), name=name, grid=(t // NORM_ROWS,),
        in_specs=[_row_spec(d), _vec_spec(d), _row_spec(d), _vec_spec(d)], out_specs=[_row_spec(d), _row_spec(d)],
        out_shape=[jax.ShapeDtypeStruct((t, d), F32), jax.ShapeDtypeStruct((t, d), BF16)], sem=("parallel",), comms=comms)


def _final_fwd_bwd(y, g_post, x, target, name):
    t, d = x.shape

    def body(y_ref, g_ref, x_ref, t_ref, loss_ref, dx_ref, dy_ref, dg_ref):
        g = g_ref[...]
        out, n, rstd = _rms(y_ref[...], g)
        e = x_ref[...] + out - t_ref[...]
        _accumulate(loss_ref, jnp.full(loss_ref.shape, 0.5 / d, F32) * jnp.sum(e * e))
        dx = e * (1.0 / d)
        dx_ref[...] = dx
        dy_ref[...] = _rms_bwd(n, rstd, g, dx).astype(BF16)
        _accumulate(dg_ref, jnp.sum(dx * n, axis=0, keepdims=True))

    return pl.pallas_call(
        body, name=name, grid=(t // NORM_ROWS,),
        in_specs=[_row_spec(d), _vec_spec(d), _row_spec(d), _row_spec(d)],
        out_specs=[pl.BlockSpec((8, 128), lambda i: (0, 0)), _row_spec(d), _row_spec(d), _vec_spec(d)],
        out_shape=[jax.ShapeDtypeStruct((8, 128), F32), jax.ShapeDtypeStruct((t, d), F32),
                   jax.ShapeDtypeStruct((t, d), BF16), jax.ShapeDtypeStruct((1, d), F32)],
        compiler_params=_params("arbitrary"),
    )(y, g_post, x, target)


def _pre_post_bwd(x, g_pre, dh, dx_in, y, g_post, name, comms=()):
    t, d = x.shape
    both = y is not None

    def body(x_ref, gp_ref, dh_ref, dxi_ref, *rest):
        if both:
            y_ref, gq_ref, dx_ref, dy_ref, dgp_ref, dgq_ref = rest
        else:
            dx_ref, dgp_ref = rest
        gp = gp_ref[...]
        _, n, rstd = _rms(x_ref[...], gp)
        dh_v = dh_ref[...]
        dx = dxi_ref[...] + _rms_bwd(n, rstd, gp, dh_v)
        dx_ref[...] = dx
        _accumulate(dgp_ref, jnp.sum(dh_v * n, axis=0, keepdims=True))
        if both:
            gq = gq_ref[...]
            _, ny, rstdy = _rms(y_ref[...], gq)
            dy_ref[...] = _rms_bwd(ny, rstdy, gq, dx).astype(BF16)
            _accumulate(dgq_ref, jnp.sum(dx * ny, axis=0, keepdims=True))

    in_specs = [_row_spec(d), _vec_spec(d), _row_spec(d), _row_spec(d)]
    args = [x, g_pre, dh, dx_in]
    if both:
        in_specs += [_row_spec(d), _vec_spec(d)]
        args += [y, g_post]
        out_specs = [_row_spec(d), _row_spec(d), _vec_spec(d), _vec_spec(d)]
        out_shape = [jax.ShapeDtypeStruct((t, d), F32), jax.ShapeDtypeStruct((t, d), BF16),
                     jax.ShapeDtypeStruct((1, d), F32), jax.ShapeDtypeStruct((1, d), F32)]
    else:
        out_specs = [_row_spec(d), _vec_spec(d)]
        out_shape = [jax.ShapeDtypeStruct((t, d), F32), jax.ShapeDtypeStruct((1, d), F32)]
    return _pcall(body, args, name=name, grid=(t // NORM_ROWS,), in_specs=in_specs, out_specs=out_specs, out_shape=out_shape,
                  sem=("arbitrary",), comms=comms)


def _gelu(x):
    return 0.5 * x * (1.0 + lax.erf(x * 0.7071067811865476))


def _gelu_grad(x):
    return 0.5 * (1.0 + lax.erf(x * 0.7071067811865476)) + x * jnp.exp(-0.5 * x * x) * 0.3989422804014327


def _layernorm(v, g, b):
    mu = jnp.mean(v, axis=-1, keepdims=True)
    vc = v - mu
    rs = lax.rsqrt(jnp.mean(vc * vc, axis=-1, keepdims=True) + LN_EPS)
    vhat = vc * rs
    return vhat * g + b, vhat, rs


def _tril_mask():
    return lax.broadcasted_iota(jnp.int32, (CHUNK, CHUNK), 0) >= lax.broadcasted_iota(jnp.int32, (CHUNK, CHUNK), 1)


def _sgu_fwd(z, ln_g, ln_b, w16, bias_b, name, comms=()):
    t = z.shape[0]
    groups = w16.shape[0]
    a = groups * CHUNK

    def body(u_ref, v_ref, g_ref, b_ref, w_ref, bb_ref, o_ref):
        u = _gelu(u_ref[...].astype(F32))
        vn = _layernorm(_gelu(v_ref[...].astype(F32)), g_ref[...], b_ref[...])[0].astype(BF16)
        tril = _tril_mask()
        for g in range(groups):
            sl = slice(g * CHUNK, (g + 1) * CHUNK)
            w = jnp.where(tril, w_ref[g], jnp.zeros((), BF16))
            mixed = _dot(w, vn[:, sl], NN) + bb_ref[g]
            o_ref[:, sl] = (u[:, sl] * mixed).astype(BF16)

    full3 = pl.BlockSpec((groups, CHUNK, CHUNK), lambda c: (0, 0, 0))
    res = _pcall(
        body, (z, z, ln_g, ln_b, w16, bias_b), name=name, grid=(t // CHUNK,),
        in_specs=[pl.BlockSpec((CHUNK, a), lambda c: (c, 0)), pl.BlockSpec((CHUNK, a), lambda c: (c, 1)),
                  _vec_spec(a), _vec_spec(a), full3, full3],
        out_specs=[pl.BlockSpec((CHUNK, a), lambda c: (c, 0))], out_shape=[jax.ShapeDtypeStruct((t, a), BF16)],
        sem=("parallel",), comms=comms)
    return (res[0][0], res[1]) if comms else res[0]


def _sgu_bwd(z, dab, ln_g, ln_b, w16, bias_b, name, comms=()):
    t = z.shape[0]
    groups = w16.shape[0]
    a = groups * CHUNK

    def body(u_ref, v_ref, da_ref, g_ref, b_ref, w_ref, bb_ref, duv_ref, dg_ref, db_ref, dw_ref, dbs_ref, dvn_ref):
        up = u_ref[...].astype(F32)
        vp = v_ref[...].astype(F32)
        u = _gelu(up)
        ln_gain = g_ref[...]
        vn32, vhat, rs = _layernorm(_gelu(vp), ln_gain, b_ref[...])
        vn = vn32.astype(BF16)
        da = da_ref[...].astype(F32)
        tril = _tril_mask()
        ones = jnp.ones((8, CHUNK), F32)

        @pl.when(pl.program_id(0) == 0)
        def _():
            dw_ref[...] = jnp.zeros_like(dw_ref)
            dbs_ref[...] = jnp.zeros_like(dbs_ref)

        for g in range(groups):
            sl = slice(g * CHUNK, (g + 1) * CHUNK)
            w = jnp.where(tril, w_ref[g], jnp.zeros((), BF16))
            mixed = _dot(w, vn[:, sl], NN) + bb_ref[g]
            dmix = da[:, sl] * u[:, sl]
            dmix16 = dmix.astype(BF16)
            duv_ref[:, sl] = (da[:, sl] * mixed * _gelu_grad(up[:, sl])).astype(BF16)
            dvn_ref[:, sl] = _dot(w, dmix16, TN)
            dw_ref[g] += jnp.where(tril, _dot(dmix16, vn[:, sl], NT), 0.0)
            dbs_ref[g:g + 1, :] += lax.dot_general(ones, dmix, (NT, ((), ())), precision=lax.Precision.HIGHEST,
                                                   preferred_element_type=F32)[0:1]
        dvn = dvn_ref[...]
        dvhat = dvn * ln_gain
        dva = rs * (dvhat - jnp.mean(dvhat, axis=-1, keepdims=True) - vhat * jnp.mean(dvhat * vhat, axis=-1, keepdims=True))
        duv_ref[:, a:] = (dva * _gelu_grad(vp)).astype(BF16)
        _accumulate(dg_ref, jnp.sum(dvn * vhat, axis=0, keepdims=True))
        _accumulate(db_ref, jnp.sum(dvn, axis=0, keepdims=True))

    full3 = pl.BlockSpec((groups, CHUNK, CHUNK), lambda c: (0, 0, 0))
    return _pcall(
        body, (z, z, dab, ln_g, ln_b, w16, bias_b), name=name, grid=(t // CHUNK,),
        in_specs=[pl.BlockSpec((CHUNK, a), lambda c: (c, 0)), pl.BlockSpec((CHUNK, a), lambda c: (c, 1)),
                  pl.BlockSpec((CHUNK, a), lambda c: (c, 0)), _vec_spec(a), _vec_spec(a), full3, full3],
        out_specs=[pl.BlockSpec((CHUNK, 2 * a), lambda c: (c, 0)), _vec_spec(a), _vec_spec(a), full3,
                   pl.BlockSpec((groups, CHUNK), lambda c: (0, 0))],
        out_shape=[jax.ShapeDtypeStruct((t, 2 * a), BF16), jax.ShapeDtypeStruct((1, a), F32), jax.ShapeDtypeStruct((1, a), F32),
                   jax.ShapeDtypeStruct((groups, CHUNK, CHUNK), F32), jax.ShapeDtypeStruct((groups, CHUNK), F32)],
        scratch=[pltpu.VMEM((CHUNK, a), F32)], sem=("arbitrary",), comms=comms)


def _dil_masks(d):
    qi = lax.broadcasted_iota(jnp.int32, (CHUNK, CHUNK), 0)
    kj = lax.broadcasted_iota(jnp.int32, (CHUNK, CHUNK), 1)
    dist_c = qi - kj
    return dist_c >= 0, dist_c <= 0, (dist_c * d).astype(F32), ((dist_c + CHUNK) * d).astype(F32)


def _alibi_slope(h, heads):
    return 2.0 ** (-8.0 * (h + 1) / heads)


def _dil_fwd(z, d, name, comms=()):
    t = z.shape[0]
    w = z.shape[1] // 5
    heads = w // HEAD_DIM
    nb = t // d // CHUNK
    scale = HEAD_DIM ** -0.5

    def body(q_ref, kp_ref, kc_ref, vp_ref, vc_ref, o_ref, l_ref):
        ok_c, ok_p0, bias_c, bias_p = _dil_masks(d)
        ok_p = ok_p0 & (pl.program_id(1) > 0)
        for h in range(heads):
            sl = slice(h * HEAD_DIM, (h + 1) * HEAD_DIM)
            slope = _alibi_slope(h, heads)
            q = q_ref[:, sl]
            s_c = jnp.where(ok_c, _dot(q, kc_ref[:, sl], NT) * scale - slope * bias_c, NEG)
            s_p = jnp.where(ok_p, _dot(q, kp_ref[:, sl], NT) * scale - slope * bias_p, NEG)
            m = jnp.maximum(jnp.max(s_c, axis=1, keepdims=True), jnp.max(s_p, axis=1, keepdims=True))
            p_c = jnp.exp(s_c - m)
            p_p = jnp.exp(s_p - m)
            den = jnp.sum(p_c, axis=1, keepdims=True) + jnp.sum(p_p, axis=1, keepdims=True)
            o = _dot(p_c.astype(BF16), vc_ref[:, sl], NN) + _dot(p_p.astype(BF16), vp_ref[:, sl], NN)
            o_ref[:, sl] = o / den
            l_ref[:, sl] = jnp.broadcast_to(m + jnp.log(den), (CHUNK, HEAD_DIM))

    def zspec(col, prev):
        if prev:
            return pl.BlockSpec((CHUNK, w), lambda r, n: (jnp.maximum(n - 1, 0), r * 5 + col))
        return pl.BlockSpec((CHUNK, w), lambda r, n: (n, r * 5 + col))

    ospec = pl.BlockSpec((CHUNK, w), lambda r, n: (n, r))
    zv = z.reshape(t // d, d * 5 * w)
    res = _pcall(
        body, (zv, zv, zv, zv, zv), name=name, grid=(d, nb),
        in_specs=[zspec(2, False), zspec(3, True), zspec(3, False), zspec(4, True), zspec(4, False)],
        out_specs=[ospec, ospec],
        out_shape=[jax.ShapeDtypeStruct((t // d, d * w), F32), jax.ShapeDtypeStruct((t // d, d * w), F32)],
        sem=("parallel", "parallel"), comms=comms)
    (o, lse), rws = res if comms else (res, None)
    outs = (o.reshape(t, w), lse.reshape(t, w))
    return (outs, rws) if comms else outs


def _dil_merge(a_out, outs, lses, name, comms=()):
    t, a = a_out.shape
    w = outs[0].shape[1]
    nbr = len(outs)

    def body(a_ref, *rest):
        o_refs, l_refs, (ab_ref, lt_ref) = rest[:nbr], rest[nbr:2 * nbr], rest[2 * nbr:]
        ls = [r[...] for r in l_refs]
        m = functools.reduce(jnp.maximum, ls)
        ws = [jnp.exp(l - m) for l in ls]
        tot = functools.reduce(jnp.add, ws)
        mix = functools.reduce(jnp.add, [wt * r[...] for wt, r in zip(ws, o_refs)]) / tot
        ab_ref[:, :a] = a_ref[...]
        ab_ref[:, a:] = mix.astype(BF16)
        lt_ref[...] = m + jnp.log(tot)

    return _pcall(
        body, (a_out, *outs, *lses), name=name, grid=(t // NORM_ROWS,),
        in_specs=[_row_spec(a)] + [_row_spec(w)] * (2 * nbr), out_specs=[_row_spec(a + w), _row_spec(w)],
        out_shape=[jax.ShapeDtypeStruct((t, a + w), BF16), jax.ShapeDtypeStruct((t, w), F32)],
        sem=("parallel",), comms=comms)


def _dil_bwd(z, ab, dab, ltot, d, name, comms=()):
    t = z.shape[0]
    w = z.shape[1] // 5
    heads = w // HEAD_DIM
    nb = t // d // CHUNK
    scale = HEAD_DIM ** -0.5

    def body(q_ref, qn_ref, kp_ref, kc_ref, vp_ref, vc_ref, o_ref, on_ref, do_ref, don_ref, l_ref, ln_ref,
             dq_ref, dk_ref, dv_ref):
        n = pl.program_id(1)
        ok_c, ok_p0, bias_c, bias_p = _dil_masks(d)
        ok_p = ok_p0 & (n > 0)
        ok_n = ok_p0 & (n < nb - 1)
        for h in range(heads):
            sl = slice(h * HEAD_DIM, (h + 1) * HEAD_DIM)
            slope = _alibi_slope(h, heads)
            q, qn, kp, kc, vp, vc = q_ref[:, sl], qn_ref[:, sl], kp_ref[:, sl], kc_ref[:, sl], vp_ref[:, sl], vc_ref[:, sl]
            do, don = do_ref[:, sl], don_ref[:, sl]
            delta = jnp.sum(do.astype(F32) * o_ref[:, sl].astype(F32), axis=1, keepdims=True)
            delta_n = jnp.sum(don.astype(F32) * on_ref[:, sl].astype(F32), axis=1, keepdims=True)
            lt, lt_n = l_ref[:, sl], ln_ref[:, sl]
            p_c = jnp.exp(jnp.where(ok_c, _dot(q, kc, NT) * scale - slope * bias_c, NEG) - lt)
            p_p = jnp.exp(jnp.where(ok_p, _dot(q, kp, NT) * scale - slope * bias_p, NEG) - lt)
            p_n = jnp.exp(jnp.where(ok_n, _dot(qn, kc, NT) * scale - slope * bias_p, NEG) - lt_n)
            ds_c = (p_c * (_dot(do, vc, NT) - delta)).astype(BF16)
            ds_p = (p_p * (_dot(do, vp, NT) - delta)).astype(BF16)
            ds_n = (p_n * (_dot(don, vc, NT) - delta_n)).astype(BF16)
            dq_ref[:, sl] = (_dot(ds_c, kc, NN) + _dot(ds_p, kp, NN)) * scale
            dk_ref[:, sl] = (_dot(ds_c, q, TN) + _dot(ds_n, qn, TN)) * scale
            dv_ref[:, sl] = _dot(p_c.astype(BF16), do, TN) + _dot(p_n.astype(BF16), don, TN)

    def spec(mult, col, shift):
        if shift < 0:
            return pl.BlockSpec((CHUNK, w), lambda r, n: (jnp.maximum(n - 1, 0), r * mult + col))
        if shift > 0:
            return pl.BlockSpec((CHUNK, w), lambda r, n: (jnp.minimum(n + 1, nb - 1), r * mult + col))
        return pl.BlockSpec((CHUNK, w), lambda r, n: (n, r * mult + col))

    zv = z.reshape(t // d, d * 5 * w)
    abv = ab.reshape(t // d, d * 2 * w)
    dabv = dab.reshape(t // d, d * 2 * w)
    lv = ltot.reshape(t // d, d * w)
    ospec = spec(1, 0, 0)
    res = _pcall(
        body, (zv, zv, zv, zv, zv, zv, abv, abv, dabv, dabv, lv, lv), name=name, grid=(d, nb),
        in_specs=[spec(5, 2, 0), spec(5, 2, 1), spec(5, 3, -1), spec(5, 3, 0), spec(5, 4, -1), spec(5, 4, 0),
                  spec(2, 1, 0), spec(2, 1, 1), spec(2, 1, 0), spec(2, 1, 1), spec(1, 0, 0), spec(1, 0, 1)],
        out_specs=[ospec, ospec, ospec], out_shape=[jax.ShapeDtypeStruct((t // d, d * w), F32)] * 3,
        sem=("parallel", "parallel"), comms=comms)
    outs, rws = res if comms else (res, None)
    outs = [o.reshape(t, w) for o in outs]
    return (outs, rws) if comms else outs


def _dz_assemble(duv, parts, name):
    t, a2 = duv.shape
    w = parts[0][0].shape[1]
    nbr = len(parts)

    def body(duv_ref, *rest):
        refs, dz_ref = rest[:-1], rest[-1]
        dz_ref[:, :a2] = duv_ref[...]
        for i in range(3):
            tot = functools.reduce(jnp.add, [refs[b * 3 + i][...] for b in range(nbr)])
            dz_ref[:, a2 + i * w:a2 + (i + 1) * w] = tot.astype(BF16)

    flat = [p for branch in parts for p in branch]
    return pl.pallas_call(
        body, name=name, grid=(t // NORM_ROWS,), in_specs=[_row_spec(a2)] + [_row_spec(w)] * len(flat),
        out_specs=_row_spec(a2 + 3 * w), out_shape=jax.ShapeDtypeStruct((t, a2 + 3 * w), BF16),
        compiler_params=_params("parallel"),
    )(duv, *flat)


def _split_dot(x, m16):
    hi = x.astype(BF16)
    lo = (x - hi.astype(F32)).astype(BF16)
    return _dot(hi, m16, NN) + _dot(lo, m16, NN)


SB_DEAD = -110.0


def _sb_scaled(q):
    return (q.astype(F32) * (HEAD_DIM ** -0.5)).astype(BF16)


def _sb_log(qs, kj, below):
    zt = _dot(qs, kj, NT)
    sp = jnp.maximum(zt, 0.0) + jnp.log(1.0 + jnp.exp(-jnp.abs(zt)))
    return zt - sp, (-sp if below is None else jnp.where(below, -sp, 0.0))


def _sb_alive(s, i, c_run):
    return (s <= i) & (jnp.max(c_run) > SB_DEAD)


def _sb_fwd(zc, name, comms=()):
    t = zc.shape[0]
    c = zc.shape[1] // 3
    heads = c // HEAD_DIM
    blk = min(SB_BLOCK, t)

    def body(q_ref, k_ref, v_ref, o_ref, ct_ref, nb_ref):
        i = pl.program_id(1)
        qs = _sb_scaled(q_ref[...])
        rows = lax.broadcasted_iota(jnp.int32, (blk, blk), 0)
        cols = lax.broadcasted_iota(jnp.int32, (blk, blk), 1)
        below = rows > cols
        m_right = below.astype(BF16)

        def tile(carry, diagonal):
            s, acc, c_run = carry
            off = pl.multiple_of((i - s) * blk, blk)
            log_beta, l = _sb_log(qs, k_ref[pl.ds(off, blk), :], below if diagonal else None)
            a = jnp.exp(log_beta + (c_run + _split_dot(l, m_right)))
            if diagonal:
                a = jnp.where(below, a, 0.0)
            acc = acc + _dot(a.astype(BF16), v_ref[pl.ds(off, blk), :], NN)
            return s + 1, acc, c_run + jnp.sum(l, axis=1, keepdims=True)

        first = tile((jnp.int32(0), jnp.zeros((blk, HEAD_DIM), F32), jnp.zeros((blk, 1), F32)), True)
        swept, acc, c_tot = lax.while_loop(lambda carry: _sb_alive(carry[0], i, carry[2]), lambda carry: tile(carry, False), first)
        o_ref[...] = acc.astype(BF16)
        ct_ref[...] = jnp.broadcast_to(c_tot, (blk, HEAD_DIM))
        nb_ref[...] = jnp.zeros((blk, HEAD_DIM), F32) + swept.astype(F32)

    qspec = pl.BlockSpec((blk, HEAD_DIM), lambda h, i: (i, h))
    return _pcall(body, (zc, zc, zc), name=name, grid=(heads, t // blk),
                  in_specs=[qspec, pl.BlockSpec((t, HEAD_DIM), lambda h, i: (0, heads + h)),
                            pl.BlockSpec((t, HEAD_DIM), lambda h, i: (0, 2 * heads + h))],
                  out_specs=[qspec, qspec, qspec],
                  out_shape=[jax.ShapeDtypeStruct((t, c), BF16), jax.ShapeDtypeStruct((t, c), F32), jax.ShapeDtypeStruct((t, c), F32)],
                  sem=("parallel", "parallel"), comms=comms)


def _sb_bwd(zc, ctot, swept, do, name, comms=()):
    t = zc.shape[0]
    c = zc.shape[1] // 3
    heads = c // HEAD_DIM
    blk = min(SB_BLOCK, t)
    scale = HEAD_DIM ** -0.5

    def body(q_ref, k_ref, v_ref, ct_ref, nb_ref, do_ref, dq_ref, dk_ref, dv_ref):
        i = pl.program_id(1)

        @pl.when(i == 0)
        def _():
            dk_ref[...] = jnp.zeros_like(dk_ref)
            dv_ref[...] = jnp.zeros_like(dv_ref)

        qs = _sb_scaled(q_ref[...])
        dov = do_ref[...]
        c_tot = ct_ref[:, 0:1]
        n_blocks = jnp.clip(jnp.max(nb_ref[0:8, :]).astype(jnp.int32), 1, i + 1)
        rows = lax.broadcasted_iota(jnp.int32, (blk, blk), 0)
        cols = lax.broadcasted_iota(jnp.int32, (blk, blk), 1)
        below = rows > cols
        m_upto = (rows <= cols).astype(BF16)
        m_left = (rows < cols).astype(BF16)

        def tile(j, carry, diagonal):
            dq, l_run, w_run = carry
            off = pl.multiple_of(j * blk, blk)
            kj = k_ref[pl.ds(off, blk), :]
            vj = v_ref[pl.ds(off, blk), :]
            log_beta, l = _sb_log(qs, kj, below if diagonal else None)
            a = jnp.exp(log_beta + (c_tot - l_run - _split_dot(l, m_upto)))
            if diagonal:
                a = jnp.where(below, a, 0.0)
            wgt = a * _dot(dov, vj, NT)
            before = w_run + _split_dot(wgt, m_left)
            dz = wgt * jnp.exp(l) - jnp.exp(log_beta) * before
            if diagonal:
                dz = jnp.where(below, dz, 0.0)
            dz16 = dz.astype(BF16)
            dk_ref[pl.ds(off, blk), :] += _dot(dz16, qs, TN)
            dv_ref[pl.ds(off, blk), :] += _dot(a.astype(BF16), dov, TN)
            return (dq + _dot(dz16, kj, NN), l_run + jnp.sum(l, axis=1, keepdims=True),
                    w_run + jnp.sum(wgt, axis=1, keepdims=True))

        zero = jnp.zeros((blk, 1), F32)
        carry = lax.fori_loop(i + 1 - n_blocks, i, lambda j, carry: tile(j, carry, False),
                              (jnp.zeros((blk, HEAD_DIM), F32), zero, zero))
        dq_ref[...] = tile(i, carry, True)[0] * scale

    qspec = pl.BlockSpec((blk, HEAD_DIM), lambda h, i: (i, h))
    full = pl.BlockSpec((t, HEAD_DIM), lambda h, i: (0, h))
    return _pcall(body, (zc, zc, zc, ctot, swept, do), name=name, grid=(heads, t // blk),
                  in_specs=[qspec, pl.BlockSpec((t, HEAD_DIM), lambda h, i: (0, heads + h)),
                            pl.BlockSpec((t, HEAD_DIM), lambda h, i: (0, 2 * heads + h)), qspec, qspec, qspec],
                  out_specs=[qspec, full, full], out_shape=[jax.ShapeDtypeStruct((t, c), F32)] * 3,
                  sem=("arbitrary", "arbitrary"), comms=comms)


def _concat_bf16(parts, name, comms=()):
    t, c = parts[0].shape

    def body(*refs):
        for k, r in enumerate(refs[:-1]):
            refs[-1][:, k * c:(k + 1) * c] = r[...].astype(BF16)

    res = _pcall(body, tuple(parts), name=name, grid=(t // NORM_ROWS,), in_specs=[_row_spec(c)] * len(parts),
                 out_specs=[_row_spec(c * len(parts))], out_shape=[jax.ShapeDtypeStruct((t, c * len(parts)), BF16)],
                 sem=("parallel",), comms=comms)
    return (res[0][0], res[1]) if comms else res[0]


KIND = {"ab_w_in": "col", "ab_w_out": "row", "sb_w_in": "col", "sb_w_out": "row",
        "ffn_w1_0": "col", "ffn_w1_1": "col", "ffn_w2_0": "row", "ffn_w2_1": "row"}
X_Y, DIAG, CHIPS = (2, 4), (6,), (2, 4, 6)


def _local_step(x, target, norms, sgu, big, bufs=None):
    g = {k: [v[l:l + 1] for l in range(2)] for k, v in norms.items()}
    ln_g, ln_b, sgu_w, sgu_b = sgu
    groups = sgu_w.shape[0]
    w16 = sgu_w.astype(BF16)
    bias_b = jnp.broadcast_to(sgu_b[:, :, None], (groups, CHUNK, CHUNK))
    big, dws, psum, dist = dict(big), {}, {}, bufs is not None
    pair, got = (dict(bufs[0]), dict(bufs[1])) if dist else ({}, {})

    def run(fn, *args, ops=(), **kw):
        if not dist or not ops:
            return fn(*args, **kw)
        make = {"gs": lambda k, p: _GatherSend(big[k], KIND[k], p), "gf": lambda k, p: _GatherFwd(big[k], KIND[k], p),
                "swap": lambda k, p: _PairSwap(dws[k], pair[k], KIND[k]), "chips": lambda k, p: _ChipScatter(psum[k], got[k], p)}
        out, rws = fn(*args, comms=[make[op](k, p) for op, k, p in ops], **kw)
        for (op, k, _), r in zip(ops, rws):
            if op in ("gs", "gf"):
                big[k] = r[0]
            elif op == "swap":
                psum[k] = _pair_sum(dws[k], r[0], KIND[k], f"pair_sum_{k}")
            else:
                got[k] = r[0]
        return out

    h1_0 = _rms_fwd(x, g["pre_mix"][0], "rms_in")
    z0 = run(_matmul, h1_0, big["ab_w_in"], "nn", BF16, "ab_in", ops=[("gs", "ffn_w1_0", X_Y)])
    a_out = run(_sgu_fwd, z0, ln_g, ln_b, w16, bias_b, "sgu_fwd", ops=[("gf", "ffn_w1_0", X_Y), ("gs", "ab_w_out", CHIPS)])
    branches = [run(_dil_fwd, z0, 1, "dil_fwd_1", ops=[("gs", "ffn_w1_0", DIAG), ("gf", "ab_w_out", CHIPS)]),
                run(_dil_fwd, z0, 4, "dil_fwd_4", ops=[("gf", "ffn_w1_0", DIAG), ("gs", "ffn_w2_0", X_Y)]),
                run(_dil_fwd, z0, 16, "dil_fwd_16", ops=[("gs", "ffn_w2_0", DIAG)])]
    ab, ltot = run(_dil_merge, a_out, [b[0] for b in branches], [b[1] for b in branches], "dil_merge", ops=[("gf", "ffn_w2_0", CHIPS)])
    y_0 = _matmul(ab, big["ab_w_out"], "nn", F32, "ab_out")
    x1, h2_0 = run(_post_pre_fwd, y_0, g["post_mix"][0], x, g["pre_ffn"][0], "norm_mix0", ops=[("gs", "sb_w_out", CHIPS)])
    r_0 = run(_matmul, h2_0, big["ffn_w1_0"], "nn", BF16, "ffn_up_0", relu_out=True,
              ops=[("gs", "sb_w_in", CHIPS), ("gf", "sb_w_out", CHIPS)])
    y2_0 = run(_matmul, r_0, big["ffn_w2_0"], "nn", F32, "ffn_down_0", a_square=True,
               ops=[("gf", "sb_w_in", CHIPS), ("gs", "ffn_w1_1", X_Y)])
    x2, h1_1 = run(_post_pre_fwd, y2_0, g["post_ffn"][0], x1, g["pre_mix"][1], "norm_ffn0", ops=[("gf", "ffn_w1_1", X_Y)])
    zc = run(_matmul, h1_1, big["sb_w_in"], "nn", BF16, "sb_in", ops=[("gs", "ffn_w1_1", DIAG)])
    o_sb, ct_sb, nb_sb = run(_sb_fwd, zc, "sb_fwd", ops=[("gf", "ffn_w1_1", DIAG), ("gs", "ffn_w2_1", CHIPS)])
    y_1 = run(_matmul, o_sb, big["sb_w_out"], "nn", F32, "sb_out", ops=[("gf", "ffn_w2_1", CHIPS)])
    x3, h2_1 = _post_pre_fwd(y_1, g["post_mix"][1], x2, g["pre_ffn"][1], "norm_mix1")
    r_1 = _matmul(h2_1, big["ffn_w1_1"], "nn", BF16, "ffn_up_1", relu_out=True)
    y2_1 = _matmul(r_1, big["ffn_w2_1"], "nn", F32, "ffn_down_1", a_square=True)
    loss, dx4, dy2_1, dg_post_ffn1 = _final_fwd_bwd(y2_1, g["post_ffn"][1], x3, target, "loss")

    da = _matmul(dy2_1, big["ffn_w2_1"], "nt", BF16, "ffn_da_1", mul2=r_1)
    dws["ffn_w2_1"] = _matmul(r_1, dy2_1, "tn", BF16, "ffn_dw2_1", a_square=True)
    dh2 = run(_matmul, da, big["ffn_w1_1"], "nt", F32, "ffn_dh_1", ops=[("swap", "ffn_w2_1", None)])
    dws["ffn_w1_1"] = run(_matmul, h2_1, da, "tn", BF16, "ffn_dw1_1", ops=[("chips", "ffn_w2_1", X_Y)])
    dx3, dy_1, dg_pre_ffn1, dg_post_mix1 = run(_pre_post_bwd, x3, g["pre_ffn"][1], dh2, dx4, y_1, g["post_mix"][1], "norm_bwd_mix1",
                                               ops=[("swap", "ffn_w1_1", None)])
    do_sb = _matmul(dy_1, big["sb_w_out"], "nt", BF16, "sb_out_dx")
    dws["sb_w_out"] = _matmul(o_sb, dy_1, "tn", BF16, "sb_out_dw")
    dqkv = run(_sb_bwd, zc, ct_sb, nb_sb, do_sb, "sb_bwd",
               ops=[("chips", "ffn_w2_1", DIAG), ("chips", "ffn_w1_1", CHIPS), ("swap", "sb_w_out", None)])
    dzc = run(_concat_bf16, dqkv, "sb_dz", ops=[("chips", "sb_w_out", X_Y)])
    dh1 = run(_matmul, dzc, big["sb_w_in"], "nt", F32, "sb_in_dx", ops=[("chips", "sb_w_out", DIAG)])
    dws["sb_w_in"] = _matmul(h1_1, dzc, "tn", BF16, "sb_in_dw")
    dx2, dy2_0, dg_pre_mix1, dg_post_ffn0 = run(_pre_post_bwd, x2, g["pre_mix"][1], dh1, dx3, y2_0, g["post_ffn"][0], "norm_bwd_ffn0",
                                                ops=[("swap", "sb_w_in", None)])
    da = run(_matmul, dy2_0, big["ffn_w2_0"], "nt", BF16, "ffn_da_0", mul2=r_0, ops=[("chips", "sb_w_in", X_Y)])
    dws["ffn_w2_0"] = run(_matmul, r_0, dy2_0, "tn", BF16, "ffn_dw2_0", a_square=True, ops=[("chips", "sb_w_in", DIAG)])
    dws["ffn_w1_0"] = run(_matmul, h2_0, da, "tn", BF16, "ffn_dw1_0", ops=[("swap", "ffn_w2_0", None)])
    dh2 = run(_matmul, da, big["ffn_w1_0"], "nt", F32, "ffn_dh_0", ops=[("chips", "ffn_w2_0", X_Y), ("swap", "ffn_w1_0", None)])
    dx1, dy_0, dg_pre_ffn0, dg_post_mix0 = _pre_post_bwd(x1, g["pre_ffn"][0], dh2, dx2, y_0, g["post_mix"][0], "norm_bwd_mix0")
    dab = _matmul(dy_0, big["ab_w_out"], "nt", BF16, "ab_out_dx")
    dws["ab_w_out"] = _matmul(ab, dy_0, "tn", BF16, "ab_out_dw")
    duv, d_ln_g, d_ln_b, d_sgu_w, d_sgu_b = run(_sgu_bwd, z0, dab, ln_g, ln_b, w16, bias_b, "sgu_bwd", ops=[("chips", "ffn_w2_0", DIAG)])
    parts = [run(_dil_bwd, z0, ab, dab, ltot, 1, "dil_bwd_1", ops=[("chips", "ffn_w1_0", X_Y), ("swap", "ab_w_out", None)]),
             run(_dil_bwd, z0, ab, dab, ltot, 4, "dil_bwd_4", ops=[("chips", "ffn_w1_0", DIAG)]),
             run(_dil_bwd, z0, ab, dab, ltot, 16, "dil_bwd_16", ops=[("chips", "ab_w_out", CHIPS)])]
    dz0 = _dz_assemble(duv, parts, "dz_assemble")
    dws["ab_w_in"] = _matmul(h1_0, dz0, "tn", BF16, "ab_in_dw")
    dh1 = run(_matmul, dz0, big["ab_w_in"], "nt", F32, "ab_in_dx", ops=[("swap", "ab_w_in", None)])
    grad_x, dg_pre_mix0 = run(_pre_post_bwd, x, g["pre_mix"][0], dh1, dx1, None, None, "norm_bwd_in", ops=[("chips", "ab_w_in", X_Y)])
    if dist:
        got["ab_w_in"] = _comm_call([_ChipScatter(psum["ab_w_in"], got["ab_w_in"], DIAG)], "scatter_last")[0][0]

    d_norms = {
        "pre_mix": jnp.concatenate([dg_pre_mix0, dg_pre_mix1]), "post_mix": jnp.concatenate([dg_post_mix0, dg_post_mix1]),
        "pre_ffn": jnp.concatenate([dg_pre_ffn0, dg_pre_ffn1]), "post_ffn": jnp.concatenate([dg_post_ffn0, dg_post_ffn1]),
    }
    return loss, grad_x, d_norms, (d_ln_g, d_ln_b, d_sgu_w, d_sgu_b), (psum, got) if dist else dws


def _to_bf16_full(w, layer, kind, name):
    _, rows, cols = w.shape
    tr = _tile(rows, 512)
    nblk = rows // tr
    full = (rows, 4 * cols) if kind == "col" else (4 * rows, cols)

    def body(w_ref, o_ref):
        o_ref[...] = w_ref[...].astype(BF16)

    def place(i):
        mine = 2 * lax.axis_index("x") + lax.axis_index("y")
        return (i, mine) if kind == "col" else (mine * nblk + i, 0)

    return pl.pallas_call(
        body, name=name, grid=(nblk,), in_specs=[pl.BlockSpec((None, tr, cols), lambda i: (layer, i, 0))],
        out_specs=pl.BlockSpec((tr, cols), place), out_shape=jax.ShapeDtypeStruct(full, BF16), compiler_params=_params("parallel"),
    )(w)


def _pair_sum(dw16, pair, kind, name):
    rh, cs = _half_shape(dw16.shape, kind)
    tr = _tile(rh, 256)
    nblk = rh // tr

    def body(dw_ref, pair_ref, o_ref):
        o_ref[...] = (dw_ref[...].astype(F32) + pair_ref[...].astype(F32)).astype(BF16)

    def own(s, i):
        c = lax.axis_index("c")
        return (c * nblk + i, s) if kind == "col" else ((2 * s + c) * nblk + i, 0)

    spec3 = pl.BlockSpec((None, tr, cs), lambda s, i: (s, i, 0))
    return pl.pallas_call(
        body, name=name, grid=(4, nblk), in_specs=[pl.BlockSpec((tr, cs), own), spec3], out_specs=spec3,
        out_shape=jax.ShapeDtypeStruct((4, rh, cs), BF16), compiler_params=_params("parallel", "parallel"),
    )(dw16, pair)


def _owner_sum(psum, got, buf, layer, name):
    _, rh, cs = psum.shape
    tr = _tile(rh, 256)

    def body(p_ref, got_ref, buf_ref, o_ref):
        tot = p_ref[...].astype(F32)
        for j in range(3):
            tot = tot + got_ref[j].astype(F32)
        o_ref[...] = tot

    return pl.pallas_call(
        body, name=name, grid=(rh // tr,),
        in_specs=[pl.BlockSpec((None, tr, cs), lambda i: (2 * lax.axis_index("x") + lax.axis_index("y"), i, 0)),
                  pl.BlockSpec((3, tr, cs), lambda i: (0, i, 0)), ANY],
        out_specs=pl.BlockSpec((None, None, tr, cs), lambda i: (layer, lax.axis_index("c"), i, 0)),
        out_shape=jax.ShapeDtypeStruct(buf.shape, F32), input_output_aliases={2: 0}, compiler_params=_params("parallel"),
    )(psum, got, buf)


def _adamw_math(w, g, m, v):
    m = ADAM_B1 * m + (1.0 - ADAM_B1) * g
    v = ADAM_B2 * v + (1.0 - ADAM_B2) * (g * g)
    m_hat = m / (1.0 - ADAM_B1 ** ADAM_STEP)
    v_hat = v / (1.0 - ADAM_B2 ** ADAM_STEP)
    return -ADAM_LR * (m_hat / (jnp.sqrt(v_hat) + ADAM_EPS) + ADAM_WD * w), m, v


def _adamw(w, g, m, v, name):
    layers, rows, cols = w.shape
    tr = _tile(rows, 256)

    def body(w_ref, g_ref, m_ref, v_ref, d_ref, mo_ref, vo_ref):
        d_ref[...], mo_ref[...], vo_ref[...] = _adamw_math(w_ref[...], g_ref[...], m_ref[...], v_ref[...])

    spec = pl.BlockSpec((None, tr, cols), lambda l, i: (l, i, 0))
    return pl.pallas_call(body, name=name, grid=(layers, rows // tr), in_specs=[spec] * 4, out_specs=[spec] * 3,
                          out_shape=[jax.ShapeDtypeStruct(w.shape, F32)] * 3, compiler_params=_params("parallel", "parallel"))(w, g, m, v)


def _pack(arrays):
    flat = jnp.concatenate([a.reshape(-1) for a in arrays])
    pad = (-flat.shape[0]) % 1024
    return jnp.pad(flat, (0, pad)).reshape(-1, 128)


def _unpack(packed, like):
    flat = packed.reshape(-1)
    out, off = [], 0
    for a in like:
        out.append(flat[off:off + a.size].reshape(a.shape))
        off += a.size
    return out


def _gather_small(g, name):
    rows = g.shape[0]

    def body(g_ref, o_ref, send, recv, local_sem):
        x, y, c, _ = _place()
        me = 4 * x + 2 * y + c
        local = pltpu.make_async_copy(g_ref, o_ref.at[me], local_sem)
        local.start()
        copies = []
        for j in range(1, 8):
            px = 1 - x if j & 4 else x
            py = 1 - y if j & 2 else y
            pc = 1 - c if j & 1 else c
            copies.append(pltpu.make_async_remote_copy(src_ref=g_ref, dst_ref=o_ref.at[me], send_sem=send.at[j - 1],
                                                       recv_sem=recv.at[j - 1], device_id=(px, py, pc), device_id_type=MESH))
        for cp in copies:
            cp.start()
        for j in range(1, 8):
            px = 1 - x if j & 4 else x
            py = 1 - y if j & 2 else y
            pc = 1 - c if j & 1 else c
            pltpu.make_async_remote_copy(src_ref=g_ref, dst_ref=o_ref.at[4 * px + 2 * py + pc], send_sem=send.at[j - 1],
                                         recv_sem=recv.at[j - 1], device_id=(px, py, pc), device_id_type=MESH).wait_recv()
        for cp in copies:
            cp.wait_send()
        local.wait()

    vmem = pl.BlockSpec(memory_space=pltpu.VMEM)
    return pl.pallas_call(
        body, name=name, in_specs=[vmem], out_specs=vmem, out_shape=jax.ShapeDtypeStruct((8, rows, 128), F32),
        scratch_shapes=[pltpu.SemaphoreType.DMA((7,)), pltpu.SemaphoreType.DMA((7,)), pltpu.SemaphoreType.DMA(())],
        compiler_params=_params(),
    )(g)


def _small_update(parts, w, m, v, name):
    rows = w.shape[0]

    def body(p_ref, w_ref, m_ref, v_ref, g_ref, d_ref, mo_ref, vo_ref):
        g = p_ref[0]
        for k in range(1, 8):
            g = g + p_ref[k]
        g_ref[...] = g
        d_ref[...], mo_ref[...], vo_ref[...] = _adamw_math(w_ref[...], g, m_ref[...], v_ref[...])

    return pl.pallas_call(body, name=name, out_shape=[jax.ShapeDtypeStruct((rows, 128), F32)] * 4, compiler_params=_params())(parts, w, m, v)


SMALL = ("norm_pre_mix", "norm_post_mix", "norm_pre_ffn", "norm_post_ffn", "sgu_ln_g", "sgu_ln_b", "sgu_w", "sgu_b")
BIG = (("ab_w_in", ("ab_w_in",)), ("ab_w_out", ("ab_w_out",)), ("sb_w_in", ("sb_w_in",)), ("sb_w_out", ("sb_w_out",)),
       ("ffn_w1", ("ffn_w1_0", "ffn_w1_1")), ("ffn_w2", ("ffn_w2_0", "ffn_w2_1")))
WEIGHTS = ("norm_pre_mix", "norm_post_mix", "norm_pre_ffn", "norm_post_ffn", "ab_w_in", "sgu_ln_g", "sgu_ln_b", "sgu_w", "sgu_b",
           "ab_w_out", "sb_w_in", "sb_w_out", "ffn_w1", "ffn_w2")


def kernel(x, norm_pre_mix, norm_post_mix, norm_pre_ffn, norm_post_ffn, ab_w_in, sgu_ln_g, sgu_ln_b, sgu_w, sgu_b, ab_w_out, sb_w_in, sb_w_out, ffn_w1, ffn_w2, loss_target, m_norm_pre_mix, m_norm_post_mix, m_norm_pre_ffn, m_norm_post_ffn, m_ab_w_in, m_sgu_ln_g, m_sgu_ln_b, m_sgu_w, m_sgu_b, m_ab_w_out, m_sb_w_in, m_sb_w_out, m_ffn_w1, m_ffn_w2, v_norm_pre_mix, v_norm_post_mix, v_norm_pre_ffn, v_norm_post_ffn, v_ab_w_in, v_sgu_ln_g, v_sgu_ln_b, v_sgu_w, v_sgu_b, v_ab_w_out, v_sb_w_in, v_sb_w_out, v_ffn_w1, v_ffn_w2):
    w = dict(norm_pre_mix=norm_pre_mix, norm_post_mix=norm_post_mix, norm_pre_ffn=norm_pre_ffn, norm_post_ffn=norm_post_ffn,
             ab_w_in=ab_w_in, sgu_ln_g=sgu_ln_g, sgu_ln_b=sgu_ln_b, sgu_w=sgu_w, sgu_b=sgu_b, ab_w_out=ab_w_out, sb_w_in=sb_w_in,
             sb_w_out=sb_w_out, ffn_w1=ffn_w1, ffn_w2=ffn_w2)
    m = dict(norm_pre_mix=m_norm_pre_mix, norm_post_mix=m_norm_post_mix, norm_pre_ffn=m_norm_pre_ffn, norm_post_ffn=m_norm_post_ffn,
             ab_w_in=m_ab_w_in, sgu_ln_g=m_sgu_ln_g, sgu_ln_b=m_sgu_ln_b, sgu_w=m_sgu_w, sgu_b=m_sgu_b, ab_w_out=m_ab_w_out,
             sb_w_in=m_sb_w_in, sb_w_out=m_sb_w_out, ffn_w1=m_ffn_w1, ffn_w2=m_ffn_w2)
    v = dict(norm_pre_mix=v_norm_pre_mix, norm_post_mix=v_norm_post_mix, norm_pre_ffn=v_norm_pre_ffn, norm_post_ffn=v_norm_post_ffn,
             ab_w_in=v_ab_w_in, sgu_ln_g=v_sgu_ln_g, sgu_ln_b=v_sgu_ln_b, sgu_w=v_sgu_w, sgu_b=v_sgu_b, ab_w_out=v_ab_w_out,
             sb_w_in=v_sb_w_in, sb_w_out=v_sb_w_out, ffn_w1=v_ffn_w1, ffn_w2=v_ffn_w2)
    big, pair, got = {}, {}, {}
    for name, keys in BIG:
        for layer, key in enumerate(keys):
            big[key] = _to_bf16_full(w[name], layer, KIND[key], f"bf16_{key}")
            half = _half_shape(big[key].shape, KIND[key])
            pair[key], got[key] = lax.empty((4,) + half, BF16), lax.empty((3,) + half, BF16)
    big["ab_w_in"] = _comm_call([_Gather(big["ab_w_in"], KIND["ab_w_in"])], "gather_first")[0][0]

    norms = {k: w["norm_" + k] for k in ("pre_mix", "post_mix", "pre_ffn", "post_ffn")}
    sgu = (sgu_ln_g, sgu_ln_b, sgu_w[0], sgu_b[0])
    loss_blk, grad_x, d_norms, d_sgu, (psum, got) = _local_step(x[0], loss_target[0], norms, sgu, big, (pair, got))
    loss = lax.psum(loss_blk[0, 0], ("x", "y", "c"))

    grads, deltas, new_m, new_v = {}, {}, {}, {}
    small_g = [d_norms["pre_mix"], d_norms["post_mix"], d_norms["pre_ffn"], d_norms["post_ffn"],
               d_sgu[0], d_sgu[1], d_sgu[2][None], d_sgu[3][None]]
    gathered = _gather_small(_pack(small_g), "gather_small")
    outs = _small_update(gathered, _pack([w[k] for k in SMALL]), _pack([m[k] for k in SMALL]), _pack([v[k] for k in SMALL]), "small_update")
    like = [w[k] for k in SMALL]
    for dst, packed in zip((grads, deltas, new_m, new_v), outs):
        for k, a in zip(SMALL, _unpack(packed, like)):
            dst[k] = a

    bufs = []
    for name, keys in BIG:
        buf = lax.empty((len(keys), 2) + psum[keys[0]].shape[1:], F32)
        for layer, key in enumerate(keys):
            buf = _owner_sum(psum[key], got[key], buf, layer, f"sum_{key}")
        bufs.append(buf)
    joined = _comm_call([_Join(bufs)], "join")[0]
    for (name, _), buf in zip(BIG, joined):
        grads[name] = buf.reshape(w[name].shape)
        deltas[name], new_m[name], new_v[name] = _adamw(w[name], grads[name], m[name], v[name], f"adamw_{name}")

    return (loss, grad_x[None], *[grads[k] for k in WEIGHTS], *[deltas[k] for k in WEIGHTS],
            *[new_m[k] for k in WEIGHTS], *[new_v[k] for k in WEIGHTS])
```

```python
import functools

import jax
import jax.numpy as jnp
from jax import lax
from jax.experimental import pallas as pl
from jax.experimental.pallas import tpu as pltpu

F32 = jnp.float32
BF16 = jnp.bfloat16
MESH = pl.DeviceIdType.MESH

HEAD_DIM = 128
CHUNK = 128
DILATIONS = (1, 4, 16)
SB_BLOCK = 256
RMS_EPS = 1e-6
LN_EPS = 1e-5
ADAM_LR, ADAM_B1, ADAM_B2, ADAM_EPS, ADAM_WD, ADAM_STEP = 0.001, 0.9, 0.999, 1e-08, 0.01, 10
NEG = -1e30
V7X_VMEM_LIMIT = 48 * 1024 * 1024
ANY = pl.BlockSpec(memory_space=pl.ANY)


def _params(*sem):
    return pltpu.CompilerParams(dimension_semantics=sem if sem else None, vmem_limit_bytes=V7X_VMEM_LIMIT)


def _tile(n, pref):
    if n <= pref:
        return n
    t = pref
    while n % t:
        t -= 128
    return t


def _dot(a, b, dims):
    return lax.dot_general(a, b, (dims, ((), ())), preferred_element_type=F32)


NN = ((1,), (0,))
NT = ((1,), (1,))
TN = ((0,), (0,))


def _place():
    x, y, c = lax.axis_index("x"), lax.axis_index("y"), lax.axis_index("c")
    return x, y, c, 2 * x + y


def _flip(x, y, c, j):
    return (1 - x if j & 4 else x), (1 - y if j & 2 else y), (1 - c if j & 1 else c)


def _half_shape(full_shape, kind):
    rows, cols = full_shape
    return (rows // 2, cols // 4) if kind == "col" else (rows // 8, cols)


def _half(ref, kind, s, h):
    rh, cs = _half_shape(ref.shape, kind)
    if kind == "col":
        return ref.at[pl.ds(h * rh, rh), pl.ds(s * cs, cs)]
    return ref.at[pl.ds((2 * s + h) * rh, rh), :]


def _remote(src, dst, send, recv, to):
    return pltpu.make_async_remote_copy(src_ref=src, dst_ref=dst, send_sem=send, recv_sem=recv, device_id=to, device_id_type=MESH)


class _Gather:
    n_sems = 6

    def __init__(self, full, kind):
        self.ro, self.rw, self.kind = [], [full], kind

    def start(self, ro, rw, send, recv):
        x, y, c, mine = _place()
        own = _half(rw[0], self.kind, mine, c)
        for k, j in enumerate((2, 4, 6)):
            px, py, _ = _flip(x, y, c, j)
            _remote(own, own, send(k), recv(k), (px, py, c)).start()

    def finish(self, ro, rw, send, recv):
        x, y, c, mine = _place()
        own = _half(rw[0], self.kind, mine, c)
        for k, j in enumerate((2, 4, 6)):
            px, py, _ = _flip(x, y, c, j)
            got = _half(rw[0], self.kind, 2 * px + py, c)
            _remote(got, got, send(k), recv(k), (x, y, c)).wait_recv()
            _remote(got, got, send(3 + k), recv(3 + k), (x, y, 1 - c)).start()
        for k, j in enumerate((2, 4, 6)):
            px, py, _ = _flip(x, y, c, j)
            got = _half(rw[0], self.kind, 2 * px + py, 1 - c)
            _remote(got, got, send(3 + k), recv(3 + k), (x, y, c)).wait_recv()
        for k in range(6):
            _remote(own, own, send(k), recv(k), (x, y, c)).wait_send()


class _GatherSend:
    def __init__(self, full, kind, patterns):
        self.ro, self.rw, self.kind, self.patterns, self.n_sems = [], [full], kind, patterns, len(patterns)

    def start(self, ro, rw, send, recv):
        x, y, c, mine = _place()
        own = _half(rw[0], self.kind, mine, c)
        for k, j in enumerate(self.patterns):
            px, py, _ = _flip(x, y, c, j)
            _remote(own, own, send(k), recv(k), (px, py, c)).start()

    def finish(self, ro, rw, send, recv):
        x, y, c, _ = _place()
        for k, j in enumerate(self.patterns):
            px, py, _ = _flip(x, y, c, j)
            got = _half(rw[0], self.kind, 2 * px + py, c)
            cp = _remote(got, got, send(k), recv(k), (x, y, c))
            cp.wait_recv()
            cp.wait_send()


class _GatherFwd:
    def __init__(self, full, kind, patterns):
        self.ro, self.rw, self.kind, self.patterns, self.n_sems = [], [full], kind, patterns, len(patterns)

    def start(self, ro, rw, send, recv):
        x, y, c, _ = _place()
        for k, j in enumerate(self.patterns):
            px, py, _ = _flip(x, y, c, j)
            got = _half(rw[0], self.kind, 2 * px + py, c)
            _remote(got, got, send(k), recv(k), (x, y, 1 - c)).start()

    def finish(self, ro, rw, send, recv):
        x, y, c, _ = _place()
        for k, j in enumerate(self.patterns):
            px, py, _ = _flip(x, y, c, j)
            got = _half(rw[0], self.kind, 2 * px + py, 1 - c)
            cp = _remote(got, got, send(k), recv(k), (x, y, c))
            cp.wait_recv()
            cp.wait_send()


class _PairSwap:
    n_sems = 4

    def __init__(self, dw16, pair, kind):
        self.ro, self.rw, self.kind = [dw16], [pair], kind

    def start(self, ro, rw, send, recv):
        x, y, c, _ = _place()
        for s in range(4):
            _remote(_half(ro[0], self.kind, s, 1 - c), rw[0].at[s], send(s), recv(s), (x, y, 1 - c)).start()

    def finish(self, ro, rw, send, recv):
        x, y, c, _ = _place()
        for s in range(4):
            cp = _remote(rw[0].at[s], rw[0].at[s], send(s), recv(s), (x, y, c))
            cp.wait_recv()
            cp.wait_send()


class _ChipScatter:
    def __init__(self, psum, got, patterns):
        self.ro, self.rw, self.patterns, self.n_sems = [psum], [got], patterns, len(patterns)

    def start(self, ro, rw, send, recv):
        x, y, c, _ = _place()
        for k, j in enumerate(self.patterns):
            px, py, _ = _flip(x, y, c, j)
            _remote(ro[0].at[2 * px + py], rw[0].at[j // 2 - 1], send(k), recv(k), (px, py, c)).start()

    def finish(self, ro, rw, send, recv):
        x, y, c, _ = _place()
        for k, j in enumerate(self.patterns):
            slot = rw[0].at[j // 2 - 1]
            cp = _remote(slot, slot, send(k), recv(k), (x, y, c))
            cp.wait_recv()
            cp.wait_send()


class _Join:
    def __init__(self, bufs):
        self.ro, self.rw, self.n_sems = [], list(bufs), sum(b.shape[0] for b in bufs)

    def _copies(self, rw, send, recv, slot):
        x, y, c, _ = _place()
        k = 0
        for ref in rw:
            for l in range(ref.shape[0]):
                yield _remote(ref.at[l, c], ref.at[l, slot(c)], send(k), recv(k), (x, y, 1 - c))
                k += 1

    def start(self, ro, rw, send, recv):
        for cp in self._copies(rw, send, recv, lambda c: c):
            cp.start()

    def finish(self, ro, rw, send, recv):
        for cp in self._copies(rw, send, recv, lambda c: 1 - c):
            cp.wait_recv()
        for cp in self._copies(rw, send, recv, lambda c: c):
            cp.wait_send()


def _comm_layout(comms):
    ro = [a for c in comms for a in c.ro]
    rw = [a for c in comms for a in c.rw]
    return ro, rw, sum(c.n_sems for c in comms)


def _comm_each(comms, method, ro_refs, rw_refs, send, recv):
    i_ro = i_rw = i_sem = 0
    for c in comms:
        getattr(c, method)(ro_refs[i_ro:i_ro + len(c.ro)], rw_refs[i_rw:i_rw + len(c.rw)],
                           lambda k, b=i_sem: send.at[b + k], lambda k, b=i_sem: recv.at[b + k])
        i_ro, i_rw, i_sem = i_ro + len(c.ro), i_rw + len(c.rw), i_sem + c.n_sems


def _split_results(comms, rws):
    out, i = [], 0
    for c in comms:
        out.append(list(rws[i:i + len(c.rw)]))
        i += len(c.rw)
    return out


def _comm_call(comms, name):
    ro, rw, n_sems = _comm_layout(comms)

    def body(*refs):
        ro_refs = refs[:len(ro)]
        rw_refs = refs[len(ro) + len(rw):len(ro) + 2 * len(rw)]
        send, recv = refs[len(ro) + 2 * len(rw):]
        _comm_each(comms, "start", ro_refs, rw_refs, send, recv)
        _comm_each(comms, "finish", ro_refs, rw_refs, send, recv)

    rws = pl.pallas_call(
        body, name=name, in_specs=[ANY] * (len(ro) + len(rw)), out_specs=[ANY] * len(rw),
        out_shape=[jax.ShapeDtypeStruct(a.shape, a.dtype) for a in rw],
        input_output_aliases={len(ro) + k: k for k in range(len(rw))},
        scratch_shapes=[pltpu.SemaphoreType.DMA((n_sems,)), pltpu.SemaphoreType.DMA((n_sems,))],
    )(*ro, *rw)
    return _split_results(comms, rws)


def _pcall(body, args, *, name, grid, in_specs, out_specs, out_shape, scratch=(), sem=(), comms=(), aliases=None):
    n_in, n_out, n_scr = len(in_specs), len(out_specs), len(scratch)
    aliases = dict(aliases or {})
    if not comms:
        return pl.pallas_call(body, name=name, grid=grid, in_specs=list(in_specs), out_specs=list(out_specs),
                              out_shape=list(out_shape), scratch_shapes=list(scratch), input_output_aliases=aliases,
                              compiler_params=_params(*sem))(*args)
    ro, rw, n_sems = _comm_layout(comms)

    def carrier(*refs):
        ins = refs[:n_in]
        ro_refs = refs[n_in:n_in + len(ro)]
        o0 = n_in + len(ro) + len(rw)
        outs = refs[o0:o0 + n_out]
        rw_refs = refs[o0 + n_out:o0 + n_out + len(rw)]
        s0 = o0 + n_out + len(rw)
        send, recv = refs[s0 + n_scr], refs[s0 + n_scr + 1]
        ids = [pl.program_id(a) for a in range(len(grid))]
        first = functools.reduce(jnp.logical_and, [i == 0 for i in ids])
        last = functools.reduce(jnp.logical_and, [i == g - 1 for i, g in zip(ids, grid)])

        @pl.when(first)
        def _():
            _comm_each(comms, "start", ro_refs, rw_refs, send, recv)

        body(*ins, *outs, *refs[s0:s0 + n_scr])

        @pl.when(last)
        def _():
            _comm_each(comms, "finish", ro_refs, rw_refs, send, recv)

    res = pl.pallas_call(
        carrier, name=name, grid=grid, in_specs=list(in_specs) + [ANY] * (len(ro) + len(rw)),
        out_specs=list(out_specs) + [ANY] * len(rw),
        out_shape=list(out_shape) + [jax.ShapeDtypeStruct(a.shape, a.dtype) for a in rw],
        input_output_aliases={**aliases, **{n_in + len(ro) + k: n_out + k for k in range(len(rw))}},
        scratch_shapes=list(scratch) + [pltpu.SemaphoreType.DMA((n_sems,)), pltpu.SemaphoreType.DMA((n_sems,))],
        compiler_params=_params(*["arbitrary"] * len(grid)),
    )(*args, *ro, *rw)
    return list(res[:n_out]), _split_results(comms, res[n_out:])


def _matmul(a, b, mode, out_dtype, name, a_square=False, relu_out=False, mul2=None, comms=()):
    if mode == "nn":
        (m, k), n = a.shape, b.shape[1]
    elif mode == "nt":
        (m, k), n = a.shape, b.shape[0]
    else:
        (k, m), n = a.shape, b.shape[1]
    tm, tn, tk = _tile(m, 1024), _tile(n, 1024), _tile(k, 2048)
    nk = k // tk
    dims = {"nn": NN, "nt": NT, "tn": TN}[mode]
    a_spec = pl.BlockSpec((tk, tm), lambda i, j, kk: (kk, i)) if mode == "tn" else pl.BlockSpec((tm, tk), lambda i, j, kk: (i, kk))
    b_spec = pl.BlockSpec((tn, tk), lambda i, j, kk: (j, kk)) if mode == "nt" else pl.BlockSpec((tk, tn), lambda i, j, kk: (kk, j))
    o_spec = pl.BlockSpec((tm, tn), lambda i, j, kk: (i, j))

    def body(a_ref, b_ref, *rest):
        m_ref = None if mul2 is None else rest[0]
        o_ref = rest[0 if mul2 is None else 1]
        kk = pl.program_id(2)

        def partial():
            av = a_ref[...]
            if a_square:
                av = av * av
            return _dot(av, b_ref[...], dims)

        def finish(r):
            if relu_out:
                r = jnp.maximum(r, 0.0)
            if mul2 is not None:
                r = r * (2.0 * m_ref[...].astype(F32))
            o_ref[...] = r.astype(out_dtype)

        if nk == 1:
            finish(partial())
            return
        acc_ref = rest[-1]

        @pl.when(kk == 0)
        def _():
            acc_ref[...] = partial()

        @pl.when(kk > 0)
        def _():
            acc_ref[...] += partial()

        @pl.when(kk == nk - 1)
        def _():
            finish(acc_ref[...])

    args = (a, b) if mul2 is None else (a, b, mul2)
    specs = [a_spec, b_spec] + ([] if mul2 is None else [o_spec])
    res = _pcall(body, args, name=name, grid=(m // tm, n // tn, nk), in_specs=specs, out_specs=[o_spec],
                 out_shape=[jax.ShapeDtypeStruct((m, n), out_dtype)], scratch=[pltpu.VMEM((tm, tn), F32)] if nk > 1 else [],
                 sem=("parallel", "parallel", "arbitrary"), comms=comms)
    return (res[0][0], res[1]) if comms else res[0]


NORM_ROWS = 256


def _rms(x, g):
    rstd = lax.rsqrt(jnp.mean(x * x, axis=-1, keepdims=True) + RMS_EPS)
    n = x * rstd
    return n * g, n, rstd


def _rms_bwd(n, rstd, g, dout):
    dn = dout * g
    return rstd * (dn - n * jnp.mean(dn * n, axis=-1, keepdims=True))


def _row_spec(d):
    return pl.BlockSpec((NORM_ROWS, d), lambda i: (i, 0))


def _vec_spec(d):
    return pl.BlockSpec((1, d), lambda i: (0, 0))


def _accumulate(ref, val):
    @pl.when(pl.program_id(0) == 0)
    def _():
        ref[...] = jnp.zeros_like(ref)

    ref[...] += val


def _rms_fwd(x, g, name):
    t, d = x.shape

    def body(x_ref, g_ref, h_ref):
        h_ref[...] = _rms(x_ref[...], g_ref[...])[0].astype(BF16)

    return pl.pallas_call(
        body, name=name, grid=(t // NORM_ROWS,), in_specs=[_row_spec(d), _vec_spec(d)], out_specs=_row_spec(d),
        out_shape=jax.ShapeDtypeStruct((t, d), BF16), compiler_params=_params("parallel"),
    )(x, g)


def _post_pre_fwd(y, g_post, x, g_pre, name, comms=()):
    t, d = x.shape

    def body(y_ref, gp_ref, x_ref, gn_ref, xn_ref, h_ref):
        xn = x_ref[...] + _rms(y_ref[...], gp_ref[...])[0]
        xn_ref[...] = xn
        h_ref[...] = _rms(xn, gn_ref[...])[0].astype(BF16)

    return _pcall(
        body, (y, g_post, x, g_pre), name=name, grid=(t // NORM_ROWS,),
        in_specs=[_row_spec(d), _vec_spec(d), _row_spec(d), _vec_spec(d)], out_specs=[_row_spec(d), _row_spec(d)],
        out_shape=[jax.ShapeDtypeStruct((t, d), F32), jax.ShapeDtypeStruct((t, d), BF16)], sem=("parallel",), comms=comms)


def _final_fwd_bwd(y, g_post, x, target, name):
    t, d = x.shape

    def body(y_ref, g_ref, x_ref, t_ref, loss_ref, dx_ref, dy_ref, dg_ref):
        g = g_ref[...]
        out, n, rstd = _rms(y_ref[...], g)
        e = x_ref[...] + out - t_ref[...]
        _accumulate(loss_ref, jnp.full(loss_ref.shape, 0.5 / d, F32) * jnp.sum(e * e))
        dx = e * (1.0 / d)
        dx_ref[...] = dx
        dy_ref[...] = _rms_bwd(n, rstd, g, dx).astype(BF16)
        _accumulate(dg_ref, jnp.sum(dx * n, axis=0, keepdims=True))

    return pl.pallas_call(
        body, name=name, grid=(t // NORM_ROWS,),
        in_specs=[_row_spec(d), _vec_spec(d), _row_spec(d), _row_spec(d)],
        out_specs=[pl.BlockSpec((8, 128), lambda i: (0, 0)), _row_spec(d), _row_spec(d), _vec_spec(d)],
        out_shape=[jax.ShapeDtypeStruct((8, 128), F32), jax.ShapeDtypeStruct((t, d), F32),
                   jax.ShapeDtypeStruct((t, d), BF16), jax.ShapeDtypeStruct((1, d), F32)],
        compiler_params=_params("arbitrary"),
    )(y, g_post, x, target)


def _pre_post_bwd(x, g_pre, dh, dx_in, y, g_post, name, comms=()):
    t, d = x.shape
    both = y is not None

    def body(x_ref, gp_ref, dh_ref, dxi_ref, *rest):
        if both:
            y_ref, gq_ref, dx_ref, dy_ref, dgp_ref, dgq_ref = rest
        else:
            dx_ref, dgp_ref = rest
        gp = gp_ref[...]
        _, n, rstd = _rms(x_ref[...], gp)
        dh_v = dh_ref[...]
        dx = dxi_ref[...] + _rms_bwd(n, rstd, gp, dh_v)
        dx_ref[...] = dx
        _accumulate(dgp_ref, jnp.sum(dh_v * n, axis=0, keepdims=True))
        if both:
            gq = gq_ref[...]
            _, ny, rstdy = _rms(y_ref[...], gq)
            dy_ref[...] = _rms_bwd(ny, rstdy, gq, dx).astype(BF16)
            _accumulate(dgq_ref, jnp.sum(dx * ny, axis=0, keepdims=True))

    in_specs = [_row_spec(d), _vec_spec(d), _row_spec(d), _row_spec(d)]
    args = [x, g_pre, dh, dx_in]
    if both:
        in_specs += [_row_spec(d), _vec_spec(d)]
        args += [y, g_post]
        out_specs = [_row_spec(d), _row_spec(d), _vec_spec(d), _vec_spec(d)]
        out_shape = [jax.ShapeDtypeStruct((t, d), F32), jax.ShapeDtypeStruct((t, d), BF16),
                     jax.ShapeDtypeStruct((1, d), F32), jax.ShapeDtypeStruct((1, d), F32)]
    else:
        out_specs = [_row_spec(d), _vec_spec(d)]
        out_shape = [jax.ShapeDtypeStruct((t, d), F32), jax.ShapeDtypeStruct((1, d), F32)]
    return _pcall(body, args, name=name, grid=(t // NORM_ROWS,), in_specs=in_specs, out_specs=out_specs, out_shape=out_shape,
                  sem=("arbitrary",), comms=comms)


def _gelu(x):
    return 0.5 * x * (1.0 + lax.erf(x * 0.7071067811865476))


def _gelu_grad(x):
    return 0.5 * (1.0 + lax.erf(x * 0.7071067811865476)) + x * jnp.exp(-0.5 * x * x) * 0.3989422804014327


def _layernorm(v, g, b):
    mu = jnp.mean(v, axis=-1, keepdims=True)
    vc = v - mu
    rs = lax.rsqrt(jnp.mean(vc * vc, axis=-1, keepdims=True) + LN_EPS)
    vhat = vc * rs
    return vhat * g + b, vhat, rs


def _tril_mask():
    return lax.broadcasted_iota(jnp.int32, (CHUNK, CHUNK), 0) >= lax.broadcasted_iota(jnp.int32, (CHUNK, CHUNK), 1)


def _sgu_fwd(z, ln_g, ln_b, w16, bias_b, name, comms=()):
    t = z.shape[0]
    groups = w16.shape[0]
    a = groups * CHUNK

    def body(u_ref, v_ref, g_ref, b_ref, w_ref, bb_ref, o_ref):
        u = _gelu(u_ref[...].astype(F32))
        vn = _layernorm(_gelu(v_ref[...].astype(F32)), g_ref[...], b_ref[...])[0].astype(BF16)
        tril = _tril_mask()
        for g in range(groups):
            sl = slice(g * CHUNK, (g + 1) * CHUNK)
            w = jnp.where(tril, w_ref[g], jnp.zeros((), BF16))
            mixed = _dot(w, vn[:, sl], NN) + bb_ref[g]
            o_ref[:, sl] = (u[:, sl] * mixed).astype(BF16)

    full3 = pl.BlockSpec((groups, CHUNK, CHUNK), lambda c: (0, 0, 0))
    res = _pcall(
        body, (z, z, ln_g, ln_b, w16, bias_b), name=name, grid=(t // CHUNK,),
        in_specs=[pl.BlockSpec((CHUNK, a), lambda c: (c, 0)), pl.BlockSpec((CHUNK, a), lambda c: (c, 1)),
                  _vec_spec(a), _vec_spec(a), full3, full3],
        out_specs=[pl.BlockSpec((CHUNK, a), lambda c: (c, 0))], out_shape=[jax.ShapeDtypeStruct((t, a), BF16)],
        sem=("parallel",), comms=comms)
    return (res[0][0], res[1]) if comms else res[0]


def _sgu_bwd(z, dab, ln_g, ln_b, w16, bias_b, name, comms=()):
    t = z.shape[0]
    groups = w16.shape[0]
    a = groups * CHUNK

    def body(u_ref, v_ref, da_ref, g_ref, b_ref, w_ref, bb_ref, duv_ref, dg_ref, db_ref, dw_ref, dbs_ref, dvn_ref):
        up = u_ref[...].astype(F32)
        vp = v_ref[...].astype(F32)
        u = _gelu(up)
        ln_gain = g_ref[...]
        vn32, vhat, rs = _layernorm(_gelu(vp), ln_gain, b_ref[...])
        vn = vn32.astype(BF16)
        da = da_ref[...].astype(F32)
        tril = _tril_mask()
        ones = jnp.ones((8, CHUNK), F32)

        @pl.when(pl.program_id(0) == 0)
        def _():
            dw_ref[...] = jnp.zeros_like(dw_ref)
            dbs_ref[...] = jnp.zeros_like(dbs_ref)

        for g in range(groups):
            sl = slice(g * CHUNK, (g + 1) * CHUNK)
            w = jnp.where(tril, w_ref[g], jnp.zeros((), BF16))
            mixed = _dot(w, vn[:, sl], NN) + bb_ref[g]
            dmix = da[:, sl] * u[:, sl]
            dmix16 = dmix.astype(BF16)
            duv_ref[:, sl] = (da[:, sl] * mixed * _gelu_grad(up[:, sl])).astype(BF16)
            dvn_ref[:, sl] = _dot(w, dmix16, TN)
            dw_ref[g] += jnp.where(tril, _dot(dmix16, vn[:, sl], NT), 0.0)
            dbs_ref[g:g + 1, :] += lax.dot_general(ones, dmix, (NT, ((), ())), precision=lax.Precision.HIGHEST,
                                                   preferred_element_type=F32)[0:1]
        dvn = dvn_ref[...]
        dvhat = dvn * ln_gain
        dva = rs * (dvhat - jnp.mean(dvhat, axis=-1, keepdims=True) - vhat * jnp.mean(dvhat * vhat, axis=-1, keepdims=True))
        duv_ref[:, a:] = (dva * _gelu_grad(vp)).astype(BF16)
        _accumulate(dg_ref, jnp.sum(dvn * vhat, axis=0, keepdims=True))
        _accumulate(db_ref, jnp.sum(dvn, axis=0, keepdims=True))

    full3 = pl.BlockSpec((groups, CHUNK, CHUNK), lambda c: (0, 0, 0))
    return _pcall(
        body, (z, z, dab, ln_g, ln_b, w16, bias_b), name=name, grid=(t // CHUNK,),
        in_specs=[pl.BlockSpec((CHUNK, a), lambda c: (c, 0)), pl.BlockSpec((CHUNK, a), lambda c: (c, 1)),
                  pl.BlockSpec((CHUNK, a), lambda c: (c, 0)), _vec_spec(a), _vec_spec(a), full3, full3],
        out_specs=[pl.BlockSpec((CHUNK, 2 * a), lambda c: (c, 0)), _vec_spec(a), _vec_spec(a), full3,
                   pl.BlockSpec((groups, CHUNK), lambda c: (0, 0))],
        out_shape=[jax.ShapeDtypeStruct((t, 2 * a), BF16), jax.ShapeDtypeStruct((1, a), F32), jax.ShapeDtypeStruct((1, a), F32),
                   jax.ShapeDtypeStruct((groups, CHUNK, CHUNK), F32), jax.ShapeDtypeStruct((groups, CHUNK), F32)],
        scratch=[pltpu.VMEM((CHUNK, a), F32)], sem=("arbitrary",), comms=comms)


def _dil_masks(d):
    qi = lax.broadcasted_iota(jnp.int32, (CHUNK, CHUNK), 0)
    kj = lax.broadcasted_iota(jnp.int32, (CHUNK, CHUNK), 1)
    dist_c = qi - kj
    return dist_c >= 0, dist_c <= 0, (dist_c * d).astype(F32), ((dist_c + CHUNK) * d).astype(F32)


def _alibi_slope(h, heads):
    return 2.0 ** (-8.0 * (h + 1) / heads)


def _dil_fwd(z, d, name, comms=()):
    t = z.shape[0]
    w = z.shape[1] // 5
    heads = w // HEAD_DIM
    nb = t // d // CHUNK
    scale = HEAD_DIM ** -0.5

    def body(q_ref, kp_ref, kc_ref, vp_ref, vc_ref, o_ref, l_ref):
        ok_c, ok_p0, bias_c, bias_p = _dil_masks(d)
        ok_p = ok_p0 & (pl.program_id(1) > 0)
        hs = range(heads)
        sl = [slice(h * HEAD_DIM, (h + 1) * HEAD_DIM) for h in hs]
        slope = [_alibi_slope(h, heads) for h in hs]
        ones = jnp.ones((CHUNK, HEAD_DIM), BF16)
        s_c = [_dot(q_ref[:, sl[h]], kc_ref[:, sl[h]], NT) for h in hs]
        s_p = [_dot(q_ref[:, sl[h]], kp_ref[:, sl[h]], NT) for h in hs]
        s_c = [jnp.where(ok_c, s_c[h] * scale - slope[h] * bias_c, NEG) for h in hs]
        s_p = [jnp.where(ok_p, s_p[h] * scale - slope[h] * bias_p, NEG) for h in hs]
        m = [jnp.max(jnp.maximum(s_c[h], s_p[h]), axis=1, keepdims=True) for h in hs]
        p_c = [jnp.exp(s_c[h] - m[h]).astype(BF16) for h in hs]
        p_p = [jnp.exp(s_p[h] - m[h]).astype(BF16) for h in hs]
        den = [_dot(p_c[h], ones, NN) + _dot(p_p[h], ones, NN) for h in hs]
        o = [_dot(p_c[h], vc_ref[:, sl[h]], NN) + _dot(p_p[h], vp_ref[:, sl[h]], NN) for h in hs]
        for h in hs:
            o_ref[:, sl[h]] = o[h] / den[h]
            l_ref[:, sl[h]] = m[h] + jnp.log(den[h])

    def zspec(col, prev):
        if prev:
            return pl.BlockSpec((CHUNK, w), lambda r, n: (jnp.maximum(n - 1, 0), r * 5 + col))
        return pl.BlockSpec((CHUNK, w), lambda r, n: (n, r * 5 + col))

    ospec = pl.BlockSpec((CHUNK, w), lambda r, n: (n, r))
    zv = z.reshape(t // d, d * 5 * w)
    res = _pcall(
        body, (zv, zv, zv, zv, zv), name=name, grid=(d, nb),
        in_specs=[zspec(2, False), zspec(3, True), zspec(3, False), zspec(4, True), zspec(4, False)],
        out_specs=[ospec, ospec],
        out_shape=[jax.ShapeDtypeStruct((t // d, d * w), F32), jax.ShapeDtypeStruct((t // d, d * w), F32)],
        sem=("parallel", "parallel"), comms=comms)
    (o, lse), rws = res if comms else (res, None)
    outs = (o.reshape(t, w), lse.reshape(t, w))
    return (outs, rws) if comms else outs


def _dil_merge(a_out, outs, lses, name, comms=()):
    t, a = a_out.shape
    w = outs[0].shape[1]
    nbr = len(outs)

    def body(a_ref, *rest):
        o_refs, l_refs, (ab_ref, lt_ref) = rest[:nbr], rest[nbr:2 * nbr], rest[2 * nbr:]
        ls = [r[...] for r in l_refs]
        m = functools.reduce(jnp.maximum, ls)
        ws = [jnp.exp(l - m) for l in ls]
        tot = functools.reduce(jnp.add, ws)
        mix = functools.reduce(jnp.add, [wt * r[...] for wt, r in zip(ws, o_refs)]) / tot
        ab_ref[:, :a] = a_ref[...]
        ab_ref[:, a:] = mix.astype(BF16)
        lt_ref[...] = m + jnp.log(tot)

    return _pcall(
        body, (a_out, *outs, *lses), name=name, grid=(t // NORM_ROWS,),
        in_specs=[_row_spec(a)] + [_row_spec(w)] * (2 * nbr), out_specs=[_row_spec(a + w), _row_spec(w)],
        out_shape=[jax.ShapeDtypeStruct((t, a + w), BF16), jax.ShapeDtypeStruct((t, w), F32)],
        sem=("parallel",), comms=comms)


def _dil_bwd(z, ab, dab, ltot, d, name, comms=()):
    t = z.shape[0]
    w = z.shape[1] // 5
    heads = w // HEAD_DIM
    nb = t // d // CHUNK
    scale = HEAD_DIM ** -0.5

    def body(q_ref, qn_ref, kp_ref, kc_ref, vp_ref, vc_ref, o_ref, on_ref, do_ref, don_ref, l_ref, ln_ref,
             dq_ref, dk_ref, dv_ref):
        n = pl.program_id(1)
        ok_c, ok_p0, bias_c, bias_p = _dil_masks(d)
        ok_p = ok_p0 & (n > 0)
        ok_n = ok_p0 & (n < nb - 1)
        hs = range(heads)
        sl = [slice(h * HEAD_DIM, (h + 1) * HEAD_DIM) for h in hs]
        slope = [_alibi_slope(h, heads) for h in hs]
        q, qn = [q_ref[:, s] for s in sl], [qn_ref[:, s] for s in sl]
        kp, kc = [kp_ref[:, s] for s in sl], [kc_ref[:, s] for s in sl]
        vp, vc = [vp_ref[:, s] for s in sl], [vc_ref[:, s] for s in sl]
        do, don = [do_ref[:, s] for s in sl], [don_ref[:, s] for s in sl]
        s_c = [_dot(q[h], kc[h], NT) for h in hs]
        s_p = [_dot(q[h], kp[h], NT) for h in hs]
        s_n = [_dot(qn[h], kc[h], NT) for h in hs]
        dp_c = [_dot(do[h], vc[h], NT) for h in hs]
        dp_p = [_dot(do[h], vp[h], NT) for h in hs]
        dp_n = [_dot(don[h], vc[h], NT) for h in hs]
        delta = [jnp.sum(do[h].astype(F32) * o_ref[:, sl[h]].astype(F32), axis=1, keepdims=True) for h in hs]
        delta_n = [jnp.sum(don[h].astype(F32) * on_ref[:, sl[h]].astype(F32), axis=1, keepdims=True) for h in hs]
        p_c = [jnp.exp(jnp.where(ok_c, s_c[h] * scale - slope[h] * bias_c, NEG) - l_ref[:, sl[h]]) for h in hs]
        p_p = [jnp.exp(jnp.where(ok_p, s_p[h] * scale - slope[h] * bias_p, NEG) - l_ref[:, sl[h]]) for h in hs]
        p_n = [jnp.exp(jnp.where(ok_n, s_n[h] * scale - slope[h] * bias_p, NEG) - ln_ref[:, sl[h]]) for h in hs]
        ds_c = [(p_c[h] * (dp_c[h] - delta[h])).astype(BF16) for h in hs]
        ds_p = [(p_p[h] * (dp_p[h] - delta[h])).astype(BF16) for h in hs]
        ds_n = [(p_n[h] * (dp_n[h] - delta_n[h])).astype(BF16) for h in hs]
        dq = [_dot(ds_c[h], kc[h], NN) + _dot(ds_p[h], kp[h], NN) for h in hs]
        dk = [_dot(ds_c[h], q[h], TN) + _dot(ds_n[h], qn[h], TN) for h in hs]
        dv = [_dot(p_c[h].astype(BF16), do[h], TN) + _dot(p_n[h].astype(BF16), don[h], TN) for h in hs]
        for h in hs:
            dq_ref[:, sl[h]] = dq[h] * scale
            dk_ref[:, sl[h]] = dk[h] * scale
            dv_ref[:, sl[h]] = dv[h]

    def spec(mult, col, shift):
        if shift < 0:
            return pl.BlockSpec((CHUNK, w), lambda r, n: (jnp.maximum(n - 1, 0), r * mult + col))
        if shift > 0:
            return pl.BlockSpec((CHUNK, w), lambda r, n: (jnp.minimum(n + 1, nb - 1), r * mult + col))
        return pl.BlockSpec((CHUNK, w), lambda r, n: (n, r * mult + col))

    zv = z.reshape(t // d, d * 5 * w)
    abv = ab.reshape(t // d, d * 2 * w)
    dabv = dab.reshape(t // d, d * 2 * w)
    lv = ltot.reshape(t // d, d * w)
    ospec = spec(1, 0, 0)
    res = _pcall(
        body, (zv, zv, zv, zv, zv, zv, abv, abv, dabv, dabv, lv, lv), name=name, grid=(d, nb),
        in_specs=[spec(5, 2, 0), spec(5, 2, 1), spec(5, 3, -1), spec(5, 3, 0), spec(5, 4, -1), spec(5, 4, 0),
                  spec(2, 1, 0), spec(2, 1, 1), spec(2, 1, 0), spec(2, 1, 1), spec(1, 0, 0), spec(1, 0, 1)],
        out_specs=[ospec, ospec, ospec], out_shape=[jax.ShapeDtypeStruct((t // d, d * w), F32)] * 3,
        sem=("parallel", "parallel"), comms=comms)
    outs, rws = res if comms else (res, None)
    outs = [o.reshape(t, w) for o in outs]
    return (outs, rws) if comms else outs


def _dz_assemble(duv, parts, name):
    t, a2 = duv.shape
    w = parts[0][0].shape[1]
    nbr = len(parts)

    def body(duv_ref, *rest):
        refs, dz_ref = rest[:-1], rest[-1]
        dz_ref[:, :a2] = duv_ref[...]
        for i in range(3):
            tot = functools.reduce(jnp.add, [refs[b * 3 + i][...] for b in range(nbr)])
            dz_ref[:, a2 + i * w:a2 + (i + 1) * w] = tot.astype(BF16)

    flat = [p for branch in parts for p in branch]
    return pl.pallas_call(
        body, name=name, grid=(t // NORM_ROWS,), in_specs=[_row_spec(a2)] + [_row_spec(w)] * len(flat),
        out_specs=_row_spec(a2 + 3 * w), out_shape=jax.ShapeDtypeStruct((t, a2 + 3 * w), BF16),
        compiler_params=_params("parallel"),
    )(duv, *flat)


def _split_dot(x, m16):
    hi = x.astype(BF16)
    lo = (x - hi.astype(F32)).astype(BF16)
    return _dot(hi, m16, NN) + _dot(lo, m16, NN)


SB_DEAD = -110.0


def _sb_scaled(q):
    return (q.astype(F32) * (HEAD_DIM ** -0.5)).astype(BF16)


def _sb_log(qs, kj, below):
    zt = _dot(qs, kj, NT)
    sp = jnp.maximum(zt, 0.0) + jnp.log(1.0 + jnp.exp(-jnp.abs(zt)))
    return zt - sp, (-sp if below is None else jnp.where(below, -sp, 0.0))


def _sb_alive(s, i, c_run):
    return (s <= i) & (jnp.max(c_run) > SB_DEAD)


def _sb_fwd(zc, name, comms=()):
    t = zc.shape[0]
    c = zc.shape[1] // 3
    heads = c // HEAD_DIM
    blk = min(SB_BLOCK, t)

    def body(q_ref, k_ref, v_ref, o_ref, ct_ref, nb_ref):
        i = pl.program_id(1)
        qs = _sb_scaled(q_ref[...])
        rows = lax.broadcasted_iota(jnp.int32, (blk, blk), 0)
        cols = lax.broadcasted_iota(jnp.int32, (blk, blk), 1)
        below = rows > cols
        m_right = below.astype(BF16)

        def tile(carry, diagonal):
            s, acc, c_run = carry
            off = pl.multiple_of((i - s) * blk, blk)
            log_beta, l = _sb_log(qs, k_ref[pl.ds(off, blk), :], below if diagonal else None)
            a = jnp.exp(log_beta + (c_run + _split_dot(l, m_right)))
            if diagonal:
                a = jnp.where(below, a, 0.0)
            acc = acc + _dot(a.astype(BF16), v_ref[pl.ds(off, blk), :], NN)
            return s + 1, acc, c_run + jnp.sum(l, axis=1, keepdims=True)

        first = tile((jnp.int32(0), jnp.zeros((blk, HEAD_DIM), F32), jnp.zeros((blk, 1), F32)), True)
        swept, acc, c_tot = lax.while_loop(lambda carry: _sb_alive(carry[0], i, carry[2]), lambda carry: tile(carry, False), first)
        o_ref[...] = acc.astype(BF16)
        ct_ref[...] = jnp.broadcast_to(c_tot, (blk, HEAD_DIM))
        nb_ref[...] = jnp.zeros((blk, HEAD_DIM), F32) + swept.astype(F32)

    qspec = pl.BlockSpec((blk, HEAD_DIM), lambda h, i: (i, h))
    return _pcall(body, (zc, zc, zc), name=name, grid=(heads, t // blk),
                  in_specs=[qspec, pl.BlockSpec((t, HEAD_DIM), lambda h, i: (0, heads + h)),
                            pl.BlockSpec((t, HEAD_DIM), lambda h, i: (0, 2 * heads + h))],
                  out_specs=[qspec, qspec, qspec],
                  out_shape=[jax.ShapeDtypeStruct((t, c), BF16), jax.ShapeDtypeStruct((t, c), F32), jax.ShapeDtypeStruct((t, c), F32)],
                  sem=("parallel", "parallel"), comms=comms)


def _sb_bwd(zc, ctot, swept, do, name, comms=()):
    t = zc.shape[0]
    c = zc.shape[1] // 3
    heads = c // HEAD_DIM
    blk = min(SB_BLOCK, t)
    scale = HEAD_DIM ** -0.5

    def body(q_ref, k_ref, v_ref, ct_ref, nb_ref, do_ref, dq_ref, dk_ref, dv_ref):
        i = pl.program_id(1)

        @pl.when(i == 0)
        def _():
            dk_ref[...] = jnp.zeros_like(dk_ref)
            dv_ref[...] = jnp.zeros_like(dv_ref)

        qs = _sb_scaled(q_ref[...])
        dov = do_ref[...]
        c_tot = ct_ref[:, 0:1]
        n_blocks = jnp.clip(jnp.max(nb_ref[0:8, :]).astype(jnp.int32), 1, i + 1)
        rows = lax.broadcasted_iota(jnp.int32, (blk, blk), 0)
        cols = lax.broadcasted_iota(jnp.int32, (blk, blk), 1)
        below = rows > cols
        m_upto = (rows <= cols).astype(BF16)
        m_left = (rows < cols).astype(BF16)

        def tile(j, carry, diagonal):
            dq, l_run, w_run = carry
            off = pl.multiple_of(j * blk, blk)
            kj = k_ref[pl.ds(off, blk), :]
            vj = v_ref[pl.ds(off, blk), :]
            log_beta, l = _sb_log(qs, kj, below if diagonal else None)
            a = jnp.exp(log_beta + (c_tot - l_run - _split_dot(l, m_upto)))
            if diagonal:
                a = jnp.where(below, a, 0.0)
            wgt = a * _dot(dov, vj, NT)
            before = w_run + _split_dot(wgt, m_left)
            dz = wgt * jnp.exp(l) - jnp.exp(log_beta) * before
            if diagonal:
                dz = jnp.where(below, dz, 0.0)
            dz16 = dz.astype(BF16)
            dk_ref[pl.ds(off, blk), :] += _dot(dz16, qs, TN)
            dv_ref[pl.ds(off, blk), :] += _dot(a.astype(BF16), dov, TN)
            return (dq + _dot(dz16, kj, NN), l_run + jnp.sum(l, axis=1, keepdims=True),
                    w_run + jnp.sum(wgt, axis=1, keepdims=True))

        zero = jnp.zeros((blk, 1), F32)
        carry = lax.fori_loop(i + 1 - n_blocks, i, lambda j, carry: tile(j, carry, False),
                              (jnp.zeros((blk, HEAD_DIM), F32), zero, zero))
        dq_ref[...] = tile(i, carry, True)[0] * scale

    qspec = pl.BlockSpec((blk, HEAD_DIM), lambda h, i: (i, h))
    full = pl.BlockSpec((t, HEAD_DIM), lambda h, i: (0, h))
    return _pcall(body, (zc, zc, zc, ctot, swept, do), name=name, grid=(heads, t // blk),
                  in_specs=[qspec, pl.BlockSpec((t, HEAD_DIM), lambda h, i: (0, heads + h)),
                            pl.BlockSpec((t, HEAD_DIM), lambda h, i: (0, 2 * heads + h)), qspec, qspec, qspec],
                  out_specs=[qspec, full, full], out_shape=[jax.ShapeDtypeStruct((t, c), F32)] * 3,
                  sem=("arbitrary", "arbitrary"), comms=comms)


def _concat_bf16(parts, name, comms=()):
    t, c = parts[0].shape

    def body(*refs):
        for k, r in enumerate(refs[:-1]):
            refs[-1][:, k * c:(k + 1) * c] = r[...].astype(BF16)

    res = _pcall(body, tuple(parts), name=name, grid=(t // NORM_ROWS,), in_specs=[_row_spec(c)] * len(parts),
                 out_specs=[_row_spec(c * len(parts))], out_shape=[jax.ShapeDtypeStruct((t, c * len(parts)), BF16)],
                 sem=("parallel",), comms=comms)
    return (res[0][0], res[1]) if comms else res[0]


KIND = {"ab_w_in": "col", "ab_w_out": "row", "sb_w_in": "col", "sb_w_out": "row",
        "ffn_w1_0": "col", "ffn_w1_1": "col", "ffn_w2_0": "row", "ffn_w2_1": "row"}
X_Y, DIAG, CHIPS = (2, 4), (6,), (2, 4, 6)


def _local_step(x, target, norms, sgu, big, bufs=None):
    g = {k: [v[l:l + 1] for l in range(2)] for k, v in norms.items()}
    ln_g, ln_b, sgu_w, sgu_b = sgu
    groups = sgu_w.shape[0]
    w16 = sgu_w.astype(BF16)
    bias_b = jnp.broadcast_to(sgu_b[:, :, None], (groups, CHUNK, CHUNK))
    big, dws, psum, dist = dict(big), {}, {}, bufs is not None
    pair, got = (dict(bufs[0]), dict(bufs[1])) if dist else ({}, {})

    def run(fn, *args, ops=(), **kw):
        if not dist or not ops:
            return fn(*args, **kw)
        make = {"gs": lambda k, p: _GatherSend(big[k], KIND[k], p), "gf": lambda k, p: _GatherFwd(big[k], KIND[k], p),
                "swap": lambda k, p: _PairSwap(dws[k], pair[k], KIND[k]), "chips": lambda k, p: _ChipScatter(psum[k], got[k], p)}
        out, rws = fn(*args, comms=[make[op](k, p) for op, k, p in ops], **kw)
        for (op, k, _), r in zip(ops, rws):
            if op in ("gs", "gf"):
                big[k] = r[0]
            elif op == "swap":
                psum[k] = _pair_sum(dws[k], r[0], KIND[k], f"pair_sum_{k}")
            else:
                got[k] = r[0]
        return out

    h1_0 = _rms_fwd(x, g["pre_mix"][0], "rms_in")
    z0 = run(_matmul, h1_0, big["ab_w_in"], "nn", BF16, "ab_in", ops=[("gs", "ffn_w1_0", X_Y)])
    a_out = run(_sgu_fwd, z0, ln_g, ln_b, w16, bias_b, "sgu_fwd", ops=[("gf", "ffn_w1_0", X_Y), ("gs", "ab_w_out", CHIPS)])
    branches = [run(_dil_fwd, z0, 1, "dil_fwd_1", ops=[("gs", "ffn_w1_0", DIAG), ("gf", "ab_w_out", CHIPS)]),
                run(_dil_fwd, z0, 4, "dil_fwd_4", ops=[("gf", "ffn_w1_0", DIAG), ("gs", "ffn_w2_0", X_Y)]),
                run(_dil_fwd, z0, 16, "dil_fwd_16", ops=[("gs", "ffn_w2_0", DIAG)])]
    ab, ltot = run(_dil_merge, a_out, [b[0] for b in branches], [b[1] for b in branches], "dil_merge", ops=[("gf", "ffn_w2_0", CHIPS)])
    y_0 = _matmul(ab, big["ab_w_out"], "nn", F32, "ab_out")
    x1, h2_0 = run(_post_pre_fwd, y_0, g["post_mix"][0], x, g["pre_ffn"][0], "norm_mix0", ops=[("gs", "sb_w_out", CHIPS)])
    r_0 = run(_matmul, h2_0, big["ffn_w1_0"], "nn", BF16, "ffn_up_0", relu_out=True,
              ops=[("gs", "sb_w_in", CHIPS), ("gf", "sb_w_out", CHIPS)])
    y2_0 = run(_matmul, r_0, big["ffn_w2_0"], "nn", F32, "ffn_down_0", a_square=True,
               ops=[("gf", "sb_w_in", CHIPS), ("gs", "ffn_w1_1", X_Y)])
    x2, h1_1 = run(_post_pre_fwd, y2_0, g["post_ffn"][0], x1, g["pre_mix"][1], "norm_ffn0", ops=[("gf", "ffn_w1_1", X_Y)])
    zc = run(_matmul, h1_1, big["sb_w_in"], "nn", BF16, "sb_in", ops=[("gs", "ffn_w1_1", DIAG)])
    o_sb, ct_sb, nb_sb = run(_sb_fwd, zc, "sb_fwd", ops=[("gf", "ffn_w1_1", DIAG), ("gs", "ffn_w2_1", CHIPS)])
    y_1 = run(_matmul, o_sb, big["sb_w_out"], "nn", F32, "sb_out", ops=[("gf", "ffn_w2_1", CHIPS)])
    x3, h2_1 = _post_pre_fwd(y_1, g["post_mix"][1], x2, g["pre_ffn"][1], "norm_mix1")
    r_1 = _matmul(h2_1, big["ffn_w1_1"], "nn", BF16, "ffn_up_1", relu_out=True)
    y2_1 = _matmul(r_1, big["ffn_w2_1"], "nn", F32, "ffn_down_1", a_square=True)
    loss, dx4, dy2_1, dg_post_ffn1 = _final_fwd_bwd(y2_1, g["post_ffn"][1], x3, target, "loss")

    da = _matmul(dy2_1, big["ffn_w2_1"], "nt", BF16, "ffn_da_1", mul2=r_1)
    dws["ffn_w2_1"] = _matmul(r_1, dy2_1, "tn", BF16, "ffn_dw2_1", a_square=True)
    dh2 = run(_matmul, da, big["ffn_w1_1"], "nt", F32, "ffn_dh_1", ops=[("swap", "ffn_w2_1", None)])
    dws["ffn_w1_1"] = run(_matmul, h2_1, da, "tn", BF16, "ffn_dw1_1", ops=[("chips", "ffn_w2_1", X_Y)])
    dx3, dy_1, dg_pre_ffn1, dg_post_mix1 = run(_pre_post_bwd, x3, g["pre_ffn"][1], dh2, dx4, y_1, g["post_mix"][1], "norm_bwd_mix1",
                                               ops=[("swap", "ffn_w1_1", None)])
    do_sb = _matmul(dy_1, big["sb_w_out"], "nt", BF16, "sb_out_dx")
    dws["sb_w_out"] = _matmul(o_sb, dy_1, "tn", BF16, "sb_out_dw")
    dqkv = run(_sb_bwd, zc, ct_sb, nb_sb, do_sb, "sb_bwd",
               ops=[("chips", "ffn_w2_1", DIAG), ("chips", "ffn_w1_1", CHIPS), ("swap", "sb_w_out", None)])
    dzc = run(_concat_bf16, dqkv, "sb_dz", ops=[("chips", "sb_w_out", X_Y)])
    dh1 = run(_matmul, dzc, big["sb_w_in"], "nt", F32, "sb_in_dx", ops=[("chips", "sb_w_out", DIAG)])
    dws["sb_w_in"] = _matmul(h1_1, dzc, "tn", BF16, "sb_in_dw")
    dx2, dy2_0, dg_pre_mix1, dg_post_ffn0 = run(_pre_post_bwd, x2, g["pre_mix"][1], dh1, dx3, y2_0, g["post_ffn"][0], "norm_bwd_ffn0",
                                                ops=[("swap", "sb_w_in", None)])
    da = run(_matmul, dy2_0, big["ffn_w2_0"], "nt", BF16, "ffn_da_0", mul2=r_0, ops=[("chips", "sb_w_in", X_Y)])
    dws["ffn_w2_0"] = run(_matmul, r_0, dy2_0, "tn", BF16, "ffn_dw2_0", a_square=True, ops=[("chips", "sb_w_in", DIAG)])
    dws["ffn_w1_0"] = run(_matmul, h2_0, da, "tn", BF16, "ffn_dw1_0", ops=[("swap", "ffn_w2_0", None)])
    dh2 = run(_matmul, da, big["ffn_w1_0"], "nt", F32, "ffn_dh_0", ops=[("chips", "ffn_w2_0", X_Y), ("swap", "ffn_w1_0", None)])
    dx1, dy_0, dg_pre_ffn0, dg_post_mix0 = _pre_post_bwd(x1, g["pre_ffn"][0], dh2, dx2, y_0, g["post_mix"][0], "norm_bwd_mix0")
    dab = _matmul(dy_0, big["ab_w_out"], "nt", BF16, "ab_out_dx")
    dws["ab_w_out"] = _matmul(ab, dy_0, "tn", BF16, "ab_out_dw")
    duv, d_ln_g, d_ln_b, d_sgu_w, d_sgu_b = run(_sgu_bwd, z0, dab, ln_g, ln_b, w16, bias_b, "sgu_bwd", ops=[("chips", "ffn_w2_0", DIAG)])
    parts = [run(_dil_bwd, z0, ab, dab, ltot, 1, "dil_bwd_1", ops=[("chips", "ffn_w1_0", X_Y), ("swap", "ab_w_out", None)]),
             run(_dil_bwd, z0, ab, dab, ltot, 4, "dil_bwd_4", ops=[("chips", "ffn_w1_0", DIAG)]),
             run(_dil_bwd, z0, ab, dab, ltot, 16, "dil_bwd_16", ops=[("chips", "ab_w_out", CHIPS)])]
    dz0 = _dz_assemble(duv, parts, "dz_assemble")
    dws["ab_w_in"] = _matmul(h1_0, dz0, "tn", BF16, "ab_in_dw")
    dh1 = run(_matmul, dz0, big["ab_w_in"], "nt", F32, "ab_in_dx", ops=[("swap", "ab_w_in", None)])
    grad_x, dg_pre_mix0 = run(_pre_post_bwd, x, g["pre_mix"][0], dh1, dx1, None, None, "norm_bwd_in", ops=[("chips", "ab_w_in", X_Y)])

    d_norms = {
        "pre_mix": jnp.concatenate([dg_pre_mix0, dg_pre_mix1]), "post_mix": jnp.concatenate([dg_post_mix0, dg_post_mix1]),
        "pre_ffn": jnp.concatenate([dg_pre_ffn0, dg_pre_ffn1]), "post_ffn": jnp.concatenate([dg_post_ffn0, dg_post_ffn1]),
    }
    return loss, grad_x, d_norms, (d_ln_g, d_ln_b, d_sgu_w, d_sgu_b), (psum, got) if dist else dws


def _to_bf16_full(w, layer, kind, name):
    _, rows, cols = w.shape
    tr = _tile(rows, 512)
    nblk = rows // tr
    full = (rows, 4 * cols) if kind == "col" else (4 * rows, cols)

    def body(w_ref, o_ref):
        o_ref[...] = w_ref[...].astype(BF16)

    def place(i):
        mine = 2 * lax.axis_index("x") + lax.axis_index("y")
        return (i, mine) if kind == "col" else (mine * nblk + i, 0)

    return pl.pallas_call(
        body, name=name, grid=(nblk,), in_specs=[pl.BlockSpec((None, tr, cols), lambda i: (layer, i, 0))],
        out_specs=pl.BlockSpec((tr, cols), place), out_shape=jax.ShapeDtypeStruct(full, BF16), compiler_params=_params("parallel"),
    )(w)


def _pair_sum(dw16, pair, kind, name):
    rh, cs = _half_shape(dw16.shape, kind)
    tr = _tile(rh, 256)
    nblk = rh // tr

    def body(dw_ref, pair_ref, o_ref):
        o_ref[...] = (dw_ref[...].astype(F32) + pair_ref[...].astype(F32)).astype(BF16)

    def own(s, i):
        c = lax.axis_index("c")
        return (c * nblk + i, s) if kind == "col" else ((2 * s + c) * nblk + i, 0)

    spec3 = pl.BlockSpec((None, tr, cs), lambda s, i: (s, i, 0))
    return pl.pallas_call(
        body, name=name, grid=(4, nblk), in_specs=[pl.BlockSpec((tr, cs), own), spec3], out_specs=spec3,
        out_shape=jax.ShapeDtypeStruct((4, rh, cs), BF16), compiler_params=_params("parallel", "parallel"),
    )(dw16, pair)


def _owner_sum(psum, got, buf, layer, name, comms=()):
    _, rh, cs = psum.shape
    tr = _tile(rh, 256)

    def body(p_ref, got_ref, buf_ref, o_ref):
        tot = p_ref[...].astype(F32)
        for j in range(3):
            tot = tot + got_ref[j].astype(F32)
        o_ref[...] = tot

    res = _pcall(
        body, (psum, got, buf), name=name, grid=(rh // tr,),
        in_specs=[pl.BlockSpec((None, tr, cs), lambda i: (2 * lax.axis_index("x") + lax.axis_index("y"), i, 0)),
                  pl.BlockSpec((3, tr, cs), lambda i: (0, i, 0)), ANY],
        out_specs=[pl.BlockSpec((None, None, tr, cs), lambda i: (layer, lax.axis_index("c"), i, 0))],
        out_shape=[jax.ShapeDtypeStruct(buf.shape, F32)], sem=("parallel",), comms=comms, aliases={2: 0})
    return (res[0][0], res[1]) if comms else res[0]


def _adamw_math(w, g, m, v):
    m = ADAM_B1 * m + (1.0 - ADAM_B1) * g
    v = ADAM_B2 * v + (1.0 - ADAM_B2) * (g * g)
    m_hat = m / (1.0 - ADAM_B1 ** ADAM_STEP)
    v_hat = v / (1.0 - ADAM_B2 ** ADAM_STEP)
    return -ADAM_LR * (m_hat / (jnp.sqrt(v_hat) + ADAM_EPS) + ADAM_WD * w), m, v


def _adamw(w, g, m, v, name, comms=()):
    layers, rows, cols = w.shape
    tr = _tile(rows, 256)

    def body(w_ref, g_ref, m_ref, v_ref, d_ref, mo_ref, vo_ref):
        d_ref[...], mo_ref[...], vo_ref[...] = _adamw_math(w_ref[...], g_ref[...], m_ref[...], v_ref[...])

    spec = pl.BlockSpec((None, tr, cols), lambda l, i: (l, i, 0))
    return _pcall(body, (w, g, m, v), name=name, grid=(layers, rows // tr), in_specs=[spec] * 4, out_specs=[spec] * 3,
                  out_shape=[jax.ShapeDtypeStruct(w.shape, F32)] * 3, sem=("parallel", "parallel"), comms=comms)


def _pack(arrays):
    flat = jnp.concatenate([a.reshape(-1) for a in arrays])
    pad = (-flat.shape[0]) % 1024
    return jnp.pad(flat, (0, pad)).reshape(-1, 128)


def _unpack(packed, like):
    flat = packed.reshape(-1)
    out, off = [], 0
    for a in like:
        out.append(flat[off:off + a.size].reshape(a.shape))
        off += a.size
    return out


class _SmallGather:
    n_sems = 7

    def __init__(self, g, parts):
        self.ro, self.rw = [g], [parts]

    def start(self, ro, rw, send, recv):
        x, y, c, _ = _place()
        for j in range(1, 8):
            _remote(ro[0], rw[0].at[4 * x + 2 * y + c], send(j - 1), recv(j - 1), _flip(x, y, c, j)).start()

    def finish(self, ro, rw, send, recv):
        x, y, c, _ = _place()
        for j in range(1, 8):
            px, py, pc = _flip(x, y, c, j)
            slot = rw[0].at[4 * px + 2 * py + pc]
            cp = _remote(slot, slot, send(j - 1), recv(j - 1), (x, y, c))
            cp.wait_recv()
            cp.wait_send()


def _small_update(own, parts, w, m, v, name):
    rows = w.shape[0]

    def body(own_ref, p_ref, w_ref, m_ref, v_ref, g_ref, d_ref, mo_ref, vo_ref):
        me = 4 * lax.axis_index("x") + 2 * lax.axis_index("y") + lax.axis_index("c")
        g = jnp.where(me == 0, own_ref[...], p_ref[0])
        for k in range(1, 8):
            g = g + jnp.where(me == k, own_ref[...], p_ref[k])
        g_ref[...] = g
        d_ref[...], mo_ref[...], vo_ref[...] = _adamw_math(w_ref[...], g, m_ref[...], v_ref[...])

    return pl.pallas_call(body, name=name, out_shape=[jax.ShapeDtypeStruct((rows, 128), F32)] * 4,
                          compiler_params=_params())(own, parts, w, m, v)


SMALL = ("norm_pre_mix", "norm_post_mix", "norm_pre_ffn", "norm_post_ffn", "sgu_ln_g", "sgu_ln_b", "sgu_w", "sgu_b")
BIG = (("ab_w_in", ("ab_w_in",)), ("ab_w_out", ("ab_w_out",)), ("sb_w_in", ("sb_w_in",)), ("sb_w_out", ("sb_w_out",)),
       ("ffn_w1", ("ffn_w1_0", "ffn_w1_1")), ("ffn_w2", ("ffn_w2_0", "ffn_w2_1")))
WEIGHTS = ("norm_pre_mix", "norm_post_mix", "norm_pre_ffn", "norm_post_ffn", "ab_w_in", "sgu_ln_g", "sgu_ln_b", "sgu_w", "sgu_b",
           "ab_w_out", "sb_w_in", "sb_w_out", "ffn_w1", "ffn_w2")


def kernel(x, norm_pre_mix, norm_post_mix, norm_pre_ffn, norm_post_ffn, ab_w_in, sgu_ln_g, sgu_ln_b, sgu_w, sgu_b, ab_w_out, sb_w_in, sb_w_out, ffn_w1, ffn_w2, loss_target, m_norm_pre_mix, m_norm_post_mix, m_norm_pre_ffn, m_norm_post_ffn, m_ab_w_in, m_sgu_ln_g, m_sgu_ln_b, m_sgu_w, m_sgu_b, m_ab_w_out, m_sb_w_in, m_sb_w_out, m_ffn_w1, m_ffn_w2, v_norm_pre_mix, v_norm_post_mix, v_norm_pre_ffn, v_norm_post_ffn, v_ab_w_in, v_sgu_ln_g, v_sgu_ln_b, v_sgu_w, v_sgu_b, v_ab_w_out, v_sb_w_in, v_sb_w_out, v_ffn_w1, v_ffn_w2):
    w = dict(norm_pre_mix=norm_pre_mix, norm_post_mix=norm_post_mix, norm_pre_ffn=norm_pre_ffn, norm_post_ffn=norm_post_ffn,
             ab_w_in=ab_w_in, sgu_ln_g=sgu_ln_g, sgu_ln_b=sgu_ln_b, sgu_w=sgu_w, sgu_b=sgu_b, ab_w_out=ab_w_out, sb_w_in=sb_w_in,
             sb_w_out=sb_w_out, ffn_w1=ffn_w1, ffn_w2=ffn_w2)
    m = dict(norm_pre_mix=m_norm_pre_mix, norm_post_mix=m_norm_post_mix, norm_pre_ffn=m_norm_pre_ffn, norm_post_ffn=m_norm_post_ffn,
             ab_w_in=m_ab_w_in, sgu_ln_g=m_sgu_ln_g, sgu_ln_b=m_sgu_ln_b, sgu_w=m_sgu_w, sgu_b=m_sgu_b, ab_w_out=m_ab_w_out,
             sb_w_in=m_sb_w_in, sb_w_out=m_sb_w_out, ffn_w1=m_ffn_w1, ffn_w2=m_ffn_w2)
    v = dict(norm_pre_mix=v_norm_pre_mix, norm_post_mix=v_norm_post_mix, norm_pre_ffn=v_norm_pre_ffn, norm_post_ffn=v_norm_post_ffn,
             ab_w_in=v_ab_w_in, sgu_ln_g=v_sgu_ln_g, sgu_ln_b=v_sgu_ln_b, sgu_w=v_sgu_w, sgu_b=v_sgu_b, ab_w_out=v_ab_w_out,
             sb_w_in=v_sb_w_in, sb_w_out=v_sb_w_out, ffn_w1=v_ffn_w1, ffn_w2=v_ffn_w2)
    big, pair, got = {}, {}, {}
    for name, keys in BIG:
        for layer, key in enumerate(keys):
            big[key] = _to_bf16_full(w[name], layer, KIND[key], f"bf16_{key}")
            half = _half_shape(big[key].shape, KIND[key])
            pair[key], got[key] = lax.empty((4,) + half, BF16), lax.empty((3,) + half, BF16)
    big["ab_w_in"] = _comm_call([_Gather(big["ab_w_in"], KIND["ab_w_in"])], "gather_first")[0][0]

    norms = {k: w["norm_" + k] for k in ("pre_mix", "post_mix", "pre_ffn", "post_ffn")}
    sgu = (sgu_ln_g, sgu_ln_b, sgu_w[0], sgu_b[0])
    loss_blk, grad_x, d_norms, d_sgu, (psum, got) = _local_step(x[0], loss_target[0], norms, sgu, big, (pair, got))
    loss = lax.psum(loss_blk[0, 0], ("x", "y", "c"))

    grads, deltas, new_m, new_v = {}, {}, {}, {}
    keys_of = dict(BIG)
    bufs, pending = {}, None
    for name in ("ffn_w2", "ffn_w1", "sb_w_in", "sb_w_out", "ab_w_out"):
        buf = lax.empty((len(keys_of[name]), 2) + psum[keys_of[name][0]].shape[1:], F32)
        for layer, key in enumerate(keys_of[name]):
            if pending is not None:
                buf, rws = _owner_sum(psum[key], got[key], buf, layer, f"sum_{key}", comms=[_Join([bufs[pending]])])
                bufs[pending], pending = rws[0][0], None
            else:
                buf = _owner_sum(psum[key], got[key], buf, layer, f"sum_{key}")
        bufs[name], pending = buf, name

    small_g = _pack([d_norms["pre_mix"], d_norms["post_mix"], d_norms["pre_ffn"], d_norms["post_ffn"],
                     d_sgu[0], d_sgu[1], d_sgu[2][None], d_sgu[3][None]])
    parts = lax.empty((8,) + small_g.shape, F32)

    def adamw(name, comms=()):
        grads[name] = bufs[name].reshape(w[name].shape)
        res = _adamw(w[name], grads[name], m[name], v[name], f"adamw_{name}", comms=comms)
        (deltas[name], new_m[name], new_v[name]), rws = res if comms else (res, None)
        return rws

    rws = adamw("ffn_w2", [_Join([bufs["ab_w_out"]]), _ChipScatter(psum["ab_w_in"], got["ab_w_in"], DIAG), _SmallGather(small_g, parts)])
    bufs["ab_w_out"], got["ab_w_in"], parts = rws[0][0], rws[1][0], rws[2][0]
    bufs["ab_w_in"] = _owner_sum(psum["ab_w_in"], got["ab_w_in"], lax.empty((1, 2) + psum["ab_w_in"].shape[1:], F32), 0, "sum_ab_w_in")
    bufs["ab_w_in"] = adamw("ffn_w1", [_Join([bufs["ab_w_in"]])])[0][0]
    for name in ("sb_w_in", "sb_w_out", "ab_w_out", "ab_w_in"):
        adamw(name)

    outs = _small_update(small_g, parts, _pack([w[k] for k in SMALL]), _pack([m[k] for k in SMALL]), _pack([v[k] for k in SMALL]), "small_update")
    like = [w[k] for k in SMALL]
    for dst, packed in zip((grads, deltas, new_m, new_v), outs):
        for k, a in zip(SMALL, _unpack(packed, like)):
            dst[k] = a

    return (loss, grad_x[None], *[grads[k] for k in WEIGHTS], *[deltas[k] for k in WEIGHTS],
            *[new_m[k] for k in WEIGHTS], *[new_v[k] for k in WEIGHTS])
```

```python
import functools

import jax
import jax.numpy as jnp
from jax import lax
from jax.experimental import pallas as pl
from jax.experimental.pallas import tpu as pltpu

F32 = jnp.float32
BF16 = jnp.bfloat16
MESH = pl.DeviceIdType.MESH

HEAD_DIM = 128
CHUNK = 128
DILATIONS = (1, 4, 16)
SB_BLOCK = 256
RMS_EPS = 1e-6
LN_EPS = 1e-5
ADAM_LR, ADAM_B1, ADAM_B2, ADAM_EPS, ADAM_WD, ADAM_STEP = 0.001, 0.9, 0.999, 1e-08, 0.01, 10
NEG = -1e30
V7X_VMEM_LIMIT = 48 * 1024 * 1024
ANY = pl.BlockSpec(memory_space=pl.ANY)


def _params(*sem):
    return pltpu.CompilerParams(dimension_semantics=sem if sem else None, vmem_limit_bytes=V7X_VMEM_LIMIT)


def _tile(n, pref):
    if n <= pref:
        return n
    t = pref
    while n % t:
        t -= 128
    return t


def _dot(a, b, dims):
    return lax.dot_general(a, b, (dims, ((), ())), preferred_element_type=F32)


NN = ((1,), (0,))
NT = ((1,), (1,))
TN = ((0,), (0,))


def _place():
    x, y, c = lax.axis_index("x"), lax.axis_index("y"), lax.axis_index("c")
    return x, y, c, 2 * x + y


def _flip(x, y, c, j):
    return (1 - x if j & 4 else x), (1 - y if j & 2 else y), (1 - c if j & 1 else c)


def _half_shape(full_shape, kind):
    rows, cols = full_shape
    return (rows // 2, cols // 4) if kind == "col" else (rows // 8, cols)


def _half(ref, kind, s, h):
    rh, cs = _half_shape(ref.shape, kind)
    if kind == "col":
        return ref.at[pl.ds(h * rh, rh), pl.ds(s * cs, cs)]
    return ref.at[pl.ds((2 * s + h) * rh, rh), :]


def _remote(src, dst, send, recv, to):
    return pltpu.make_async_remote_copy(src_ref=src, dst_ref=dst, send_sem=send, recv_sem=recv, device_id=to, device_id_type=MESH)


class _Gather:
    n_sems = 6

    def __init__(self, full, kind):
        self.ro, self.rw, self.kind = [], [full], kind

    def start(self, ro, rw, send, recv):
        x, y, c, mine = _place()
        own = _half(rw[0], self.kind, mine, c)
        for k, j in enumerate((2, 4, 6)):
            px, py, _ = _flip(x, y, c, j)
            _remote(own, own, send(k), recv(k), (px, py, c)).start()

    def finish(self, ro, rw, send, recv):
        x, y, c, mine = _place()
        own = _half(rw[0], self.kind, mine, c)
        for k, j in enumerate((2, 4, 6)):
            px, py, _ = _flip(x, y, c, j)
            got = _half(rw[0], self.kind, 2 * px + py, c)
            _remote(got, got, send(k), recv(k), (x, y, c)).wait_recv()
            _remote(got, got, send(3 + k), recv(3 + k), (x, y, 1 - c)).start()
        for k, j in enumerate((2, 4, 6)):
            px, py, _ = _flip(x, y, c, j)
            got = _half(rw[0], self.kind, 2 * px + py, 1 - c)
            _remote(got, got, send(3 + k), recv(3 + k), (x, y, c)).wait_recv()
        for k in range(6):
            _remote(own, own, send(k), recv(k), (x, y, c)).wait_send()


class _GatherSend:
    def __init__(self, full, kind, patterns):
        self.ro, self.rw, self.kind, self.patterns, self.n_sems = [], [full], kind, patterns, len(patterns)

    def start(self, ro, rw, send, recv):
        x, y, c, mine = _place()
        own = _half(rw[0], self.kind, mine, c)
        for k, j in enumerate(self.patterns):
            px, py, _ = _flip(x, y, c, j)
            _remote(own, own, send(k), recv(k), (px, py, c)).start()

    def finish(self, ro, rw, send, recv):
        x, y, c, _ = _place()
        for k, j in enumerate(self.patterns):
            px, py, _ = _flip(x, y, c, j)
            got = _half(rw[0], self.kind, 2 * px + py, c)
            cp = _remote(got, got, send(k), recv(k), (x, y, c))
            cp.wait_recv()
            cp.wait_send()


class _GatherFwd:
    def __init__(self, full, kind, patterns):
        self.ro, self.rw, self.kind, self.patterns, self.n_sems = [], [full], kind, patterns, len(patterns)

    def start(self, ro, rw, send, recv):
        x, y, c, _ = _place()
        for k, j in enumerate(self.patterns):
            px, py, _ = _flip(x, y, c, j)
            got = _half(rw[0], self.kind, 2 * px + py, c)
            _remote(got, got, send(k), recv(k), (x, y, 1 - c)).start()

    def finish(self, ro, rw, send, recv):
        x, y, c, _ = _place()
        for k, j in enumerate(self.patterns):
            px, py, _ = _flip(x, y, c, j)
            got = _half(rw[0], self.kind, 2 * px + py, 1 - c)
            cp = _remote(got, got, send(k), recv(k), (x, y, c))
            cp.wait_recv()
            cp.wait_send()


class _PairSwap:
    n_sems = 4

    def __init__(self, dw16, pair, kind):
        self.ro, self.rw, self.kind = [dw16], [pair], kind

    def start(self, ro, rw, send, recv):
        x, y, c, _ = _place()
        for s in range(4):
            _remote(_half(ro[0], self.kind, s, 1 - c), rw[0].at[s], send(s), recv(s), (x, y, 1 - c)).start()

    def finish(self, ro, rw, send, recv):
        x, y, c, _ = _place()
        for s in range(4):
            cp = _remote(rw[0].at[s], rw[0].at[s], send(s), recv(s), (x, y, c))
            cp.wait_recv()
            cp.wait_send()


class _ChipScatter:
    def __init__(self, psum, got, patterns):
        self.ro, self.rw, self.patterns, self.n_sems = [psum], [got], patterns, len(patterns)

    def start(self, ro, rw, send, recv):
        x, y, c, _ = _place()
        for k, j in enumerate(self.patterns):
            px, py, _ = _flip(x, y, c, j)
            _remote(ro[0].at[2 * px + py], rw[0].at[j // 2 - 1], send(k), recv(k), (px, py, c)).start()

    def finish(self, ro, rw, send, recv):
        x, y, c, _ = _place()
        for k, j in enumerate(self.patterns):
            slot = rw[0].at[j // 2 - 1]
            cp = _remote(slot, slot, send(k), recv(k), (x, y, c))
            cp.wait_recv()
            cp.wait_send()


class _Join:
    def __init__(self, bufs):
        self.ro, self.rw, self.n_sems = [], list(bufs), sum(b.shape[0] for b in bufs)

    def _copies(self, rw, send, recv, slot):
        x, y, c, _ = _place()
        k = 0
        for ref in rw:
            for l in range(ref.shape[0]):
                yield _remote(ref.at[l, c], ref.at[l, slot(c)], send(k), recv(k), (x, y, 1 - c))
                k += 1

    def start(self, ro, rw, send, recv):
        for cp in self._copies(rw, send, recv, lambda c: c):
            cp.start()

    def finish(self, ro, rw, send, recv):
        for cp in self._copies(rw, send, recv, lambda c: 1 - c):
            cp.wait_recv()
        for cp in self._copies(rw, send, recv, lambda c: c):
            cp.wait_send()


def _comm_layout(comms):
    ro = [a for c in comms for a in c.ro]
    rw = [a for c in comms for a in c.rw]
    return ro, rw, sum(c.n_sems for c in comms)


def _comm_each(comms, method, ro_refs, rw_refs, send, recv):
    i_ro = i_rw = i_sem = 0
    for c in comms:
        getattr(c, method)(ro_refs[i_ro:i_ro + len(c.ro)], rw_refs[i_rw:i_rw + len(c.rw)],
                           lambda k, b=i_sem: send.at[b + k], lambda k, b=i_sem: recv.at[b + k])
        i_ro, i_rw, i_sem = i_ro + len(c.ro), i_rw + len(c.rw), i_sem + c.n_sems


def _split_results(comms, rws):
    out, i = [], 0
    for c in comms:
        out.append(list(rws[i:i + len(c.rw)]))
        i += len(c.rw)
    return out


def _comm_call(comms, name):
    ro, rw, n_sems = _comm_layout(comms)

    def body(*refs):
        ro_refs = refs[:len(ro)]
        rw_refs = refs[len(ro) + len(rw):len(ro) + 2 * len(rw)]
        send, recv = refs[len(ro) + 2 * len(rw):]
        _comm_each(comms, "start", ro_refs, rw_refs, send, recv)
        _comm_each(comms, "finish", ro_refs, rw_refs, send, recv)

    rws = pl.pallas_call(
        body, name=name, in_specs=[ANY] * (len(ro) + len(rw)), out_specs=[ANY] * len(rw),
        out_shape=[jax.ShapeDtypeStruct(a.shape, a.dtype) for a in rw],
        input_output_aliases={len(ro) + k: k for k in range(len(rw))},
        scratch_shapes=[pltpu.SemaphoreType.DMA((n_sems,)), pltpu.SemaphoreType.DMA((n_sems,))],
    )(*ro, *rw)
    return _split_results(comms, rws)


def _pcall(body, args, *, name, grid, in_specs, out_specs, out_shape, scratch=(), sem=(), comms=(), aliases=None):
    n_in, n_out, n_scr = len(in_specs), len(out_specs), len(scratch)
    aliases = dict(aliases or {})
    if not comms:
        return pl.pallas_call(body, name=name, grid=grid, in_specs=list(in_specs), out_specs=list(out_specs),
                              out_shape=list(out_shape), scratch_shapes=list(scratch), input_output_aliases=aliases,
                              compiler_params=_params(*sem))(*args)
    ro, rw, n_sems = _comm_layout(comms)

    def carrier(*refs):
        ins = refs[:n_in]
        ro_refs = refs[n_in:n_in + len(ro)]
        o0 = n_in + len(ro) + len(rw)
        outs = refs[o0:o0 + n_out]
        rw_refs = refs[o0 + n_out:o0 + n_out + len(rw)]
        s0 = o0 + n_out + len(rw)
        send, recv = refs[s0 + n_scr], refs[s0 + n_scr + 1]
        ids = [pl.program_id(a) for a in range(len(grid))]
        first = functools.reduce(jnp.logical_and, [i == 0 for i in ids])
        last = functools.reduce(jnp.logical_and, [i == g - 1 for i, g in zip(ids, grid)])

        @pl.when(first)
        def _():
            _comm_each(comms, "start", ro_refs, rw_refs, send, recv)

        body(*ins, *outs, *refs[s0:s0 + n_scr])

        @pl.when(last)
        def _():
            _comm_each(comms, "finish", ro_refs, rw_refs, send, recv)

    res = pl.pallas_call(
        carrier, name=name, grid=grid, in_specs=list(in_specs) + [ANY] * (len(ro) + len(rw)),
        out_specs=list(out_specs) + [ANY] * len(rw),
        out_shape=list(out_shape) + [jax.ShapeDtypeStruct(a.shape, a.dtype) for a in rw],
        input_output_aliases={**aliases, **{n_in + len(ro) + k: n_out + k for k in range(len(rw))}},
        scratch_shapes=list(scratch) + [pltpu.SemaphoreType.DMA((n_sems,)), pltpu.SemaphoreType.DMA((n_sems,))],
        compiler_params=_params(*["arbitrary"] * len(grid)),
    )(*args, *ro, *rw)
    return list(res[:n_out]), _split_results(comms, res[n_out:])


def _matmul(a, b, mode, out_dtype, name, a_square=False, relu_out=False, mul2=None, comms=()):
    if mode == "nn":
        (m, k), n = a.shape, b.shape[1]
    elif mode == "nt":
        (m, k), n = a.shape, b.shape[0]
    else:
        (k, m), n = a.shape, b.shape[1]
    tm, tn, tk = _tile(m, 1024), _tile(n, 1024), _tile(k, 2048)
    nk = k // tk
    dims = {"nn": NN, "nt": NT, "tn": TN}[mode]
    a_spec = pl.BlockSpec((tk, tm), lambda i, j, kk: (kk, i)) if mode == "tn" else pl.BlockSpec((tm, tk), lambda i, j, kk: (i, kk))
    b_spec = pl.BlockSpec((tn, tk), lambda i, j, kk: (j, kk)) if mode == "nt" else pl.BlockSpec((tk, tn), lambda i, j, kk: (kk, j))
    o_spec = pl.BlockSpec((tm, tn), lambda i, j, kk: (i, j))

    def body(a_ref, b_ref, *rest):
        m_ref = None if mul2 is None else rest[0]
        o_ref = rest[0 if mul2 is None else 1]
        kk = pl.program_id(2)

        def partial():
            av = a_ref[...]
            if a_square:
                av = av * av
            return _dot(av, b_ref[...], dims)

        def finish(r):
            if relu_out:
                r = jnp.maximum(r, 0.0)
            if mul2 is not None:
                r = r * (2.0 * m_ref[...].astype(F32))
            o_ref[...] = r.astype(out_dtype)

        if nk == 1:
            finish(partial())
            return
        acc_ref = rest[-1]

        @pl.when(kk == 0)
        def _():
            acc_ref[...] = partial()

        @pl.when(kk > 0)
        def _():
            acc_ref[...] += partial()

        @pl.when(kk == nk - 1)
        def _():
            finish(acc_ref[...])

    args = (a, b) if mul2 is None else (a, b, mul2)
    specs = [a_spec, b_spec] + ([] if mul2 is None else [o_spec])
    res = _pcall(body, args, name=name, grid=(m // tm, n // tn, nk), in_specs=specs, out_specs=[o_spec],
                 out_shape=[jax.ShapeDtypeStruct((m, n), out_dtype)], scratch=[pltpu.VMEM((tm, tn), F32)] if nk > 1 else [],
                 sem=("parallel", "parallel", "arbitrary"), comms=comms)
    return (res[0][0], res[1]) if comms else res[0]


NORM_ROWS = 256


def _rms(x, g):
    rstd = lax.rsqrt(jnp.mean(x * x, axis=-1, keepdims=True) + RMS_EPS)
    n = x * rstd
    return n * g, n, rstd


def _rms_bwd(n, rstd, g, dout):
    dn = dout * g
    return rstd * (dn - n * jnp.mean(dn * n, axis=-1, keepdims=True))


def _row_spec(d):
    return pl.BlockSpec((NORM_ROWS, d), lambda i: (i, 0))


def _vec_spec(d):
    return pl.BlockSpec((1, d), lambda i: (0, 0))


def _accumulate(ref, val):
    @pl.when(pl.program_id(0) == 0)
    def _():
        ref[...] = jnp.zeros_like(ref)

    ref[...] += val


def _rms_fwd(x, g, name):
    t, d = x.shape

    def body(x_ref, g_ref, h_ref):
        h_ref[...] = _rms(x_ref[...], g_ref[...])[0].astype(BF16)

    return pl.pallas_call(
        body, name=name, grid=(t // NORM_ROWS,), in_specs=[_row_spec(d), _vec_spec(d)], out_specs=_row_spec(d),
        out_shape=jax.ShapeDtypeStruct((t, d), BF16), compiler_params=_params("parallel"),
    )(x, g)


def _post_pre_fwd(y, g_post, x, g_pre, name, comms=()):
    t, d = x.shape

    def body(y_ref, gp_ref, x_ref, gn_ref, xn_ref, h_ref):
        xn = x_ref[...] + _rms(y_ref[...], gp_ref[...])[0]
        xn_ref[...] = xn
        h_ref[...] = _rms(xn, gn_ref[...])[0].astype(BF16)

    return _pcall(
        body, (y, g_post, x, g_pre), name=name, grid=(t // NORM_ROWS,),
        in_specs=[_row_spec(d), _vec_spec(d), _row_spec(d), _vec_spec(d)], out_specs=[_row_spec(d), _row_spec(d)],
        out_shape=[jax.ShapeDtypeStruct((t, d), F32), jax.ShapeDtypeStruct((t, d), BF16)], sem=("parallel",), comms=comms)


def _final_fwd_bwd(y, g_post, x, target, name):
    t, d = x.shape

    def body(y_ref, g_ref, x_ref, t_ref, loss_ref, dx_ref, dy_ref, dg_ref):
        g = g_ref[...]
        out, n, rstd = _rms(y_ref[...], g)
        e = x_ref[...] + out - t_ref[...]
        _accumulate(loss_ref, jnp.full(loss_ref.shape, 0.5 / d, F32) * jnp.sum(e * e))
        dx = e * (1.0 / d)
        dx_ref[...] = dx
        dy_ref[...] = _rms_bwd(n, rstd, g, dx).astype(BF16)
        _accumulate(dg_ref, jnp.sum(dx * n, axis=0, keepdims=True))

    return pl.pallas_call(
        body, name=name, grid=(t // NORM_ROWS,),
        in_specs=[_row_spec(d), _vec_spec(d), _row_spec(d), _row_spec(d)],
        out_specs=[pl.BlockSpec((8, 128), lambda i: (0, 0)), _row_spec(d), _row_spec(d), _vec_spec(d)],
        out_shape=[jax.ShapeDtypeStruct((8, 128), F32), jax.ShapeDtypeStruct((t, d), F32),
                   jax.ShapeDtypeStruct((t, d), BF16), jax.ShapeDtypeStruct((1, d), F32)],
        compiler_params=_params("arbitrary"),
    )(y, g_post, x, target)


def _pre_post_bwd(x, g_pre, dh, dx_in, y, g_post, name, comms=()):
    t, d = x.shape
    both = y is not None

    def body(x_ref, gp_ref, dh_ref, dxi_ref, *rest):
        if both:
            y_ref, gq_ref, dx_ref, dy_ref, dgp_ref, dgq_ref = rest
        else:
            dx_ref, dgp_ref = rest
        gp = gp_ref[...]
        _, n, rstd = _rms(x_ref[...], gp)
        dh_v = dh_ref[...]
        dx = dxi_ref[...] + _rms_bwd(n, rstd, gp, dh_v)
        dx_ref[...] = dx
        _accumulate(dgp_ref, jnp.sum(dh_v * n, axis=0, keepdims=True))
        if both:
            gq = gq_ref[...]
            _, ny, rstdy = _rms(y_ref[...], gq)
            dy_ref[...] = _rms_bwd(ny, rstdy, gq, dx).astype(BF16)
            _accumulate(dgq_ref, jnp.sum(dx * ny, axis=0, keepdims=True))

    in_specs = [_row_spec(d), _vec_spec(d), _row_spec(d), _row_spec(d)]
    args = [x, g_pre, dh, dx_in]
    if both:
        in_specs += [_row_spec(d), _vec_spec(d)]
        args += [y, g_post]
        out_specs = [_row_spec(d), _row_spec(d), _vec_spec(d), _vec_spec(d)]
        out_shape = [jax.ShapeDtypeStruct((t, d), F32), jax.ShapeDtypeStruct((t, d), BF16),
                     jax.ShapeDtypeStruct((1, d), F32), jax.ShapeDtypeStruct((1, d), F32)]
    else:
        out_specs = [_row_spec(d), _vec_spec(d)]
        out_shape = [jax.ShapeDtypeStruct((t, d), F32), jax.ShapeDtypeStruct((1, d), F32)]
    return _pcall(body, args, name=name, grid=(t // NORM_ROWS,), in_specs=in_specs, out_specs=out_specs, out_shape=out_shape,
                  sem=("arbitrary",), comms=comms)


def _gelu(x):
    return 0.5 * x * (1.0 + lax.erf(x * 0.7071067811865476))


def _gelu_grad(x):
    return 0.5 * (1.0 + lax.erf(x * 0.7071067811865476)) + x * jnp.exp(-0.5 * x * x) * 0.3989422804014327


def _layernorm(v, g, b):
    mu = jnp.mean(v, axis=-1, keepdims=True)
    vc = v - mu
    rs = lax.rsqrt(jnp.mean(vc * vc, axis=-1, keepdims=True) + LN_EPS)
    vhat = vc * rs
    return vhat * g + b, vhat, rs


def _tril_mask():
    return lax.broadcasted_iota(jnp.int32, (CHUNK, CHUNK), 0) >= lax.broadcasted_iota(jnp.int32, (CHUNK, CHUNK), 1)


def _sgu_fwd(z, ln_g, ln_b, w16, bias_b, name, comms=()):
    t = z.shape[0]
    groups = w16.shape[0]
    a = groups * CHUNK

    def body(u_ref, v_ref, g_ref, b_ref, w_ref, bb_ref, o_ref):
        u = _gelu(u_ref[...].astype(F32))
        vn = _layernorm(_gelu(v_ref[...].astype(F32)), g_ref[...], b_ref[...])[0].astype(BF16)
        tril = _tril_mask()
        for g in range(groups):
            sl = slice(g * CHUNK, (g + 1) * CHUNK)
            w = jnp.where(tril, w_ref[g], jnp.zeros((), BF16))
            mixed = _dot(w, vn[:, sl], NN) + bb_ref[g]
            o_ref[:, sl] = (u[:, sl] * mixed).astype(BF16)

    full3 = pl.BlockSpec((groups, CHUNK, CHUNK), lambda c: (0, 0, 0))
    res = _pcall(
        body, (z, z, ln_g, ln_b, w16, bias_b), name=name, grid=(t // CHUNK,),
        in_specs=[pl.BlockSpec((CHUNK, a), lambda c: (c, 0)), pl.BlockSpec((CHUNK, a), lambda c: (c, 1)),
                  _vec_spec(a), _vec_spec(a), full3, full3],
        out_specs=[pl.BlockSpec((CHUNK, a), lambda c: (c, 0))], out_shape=[jax.ShapeDtypeStruct((t, a), BF16)],
        sem=("parallel",), comms=comms)
    return (res[0][0], res[1]) if comms else res[0]


def _sgu_bwd(z, dab, ln_g, ln_b, w16, bias_b, name, comms=()):
    t = z.shape[0]
    groups = w16.shape[0]
    a = groups * CHUNK

    def body(u_ref, v_ref, da_ref, g_ref, b_ref, w_ref, bb_ref, duv_ref, dg_ref, db_ref, dw_ref, dbs_ref, dvn_ref):
        up = u_ref[...].astype(F32)
        vp = v_ref[...].astype(F32)
        u = _gelu(up)
        ln_gain = g_ref[...]
        vn32, vhat, rs = _layernorm(_gelu(vp), ln_gain, b_ref[...])
        vn = vn32.astype(BF16)
        da = da_ref[...].astype(F32)
        tril = _tril_mask()
        ones = jnp.ones((8, CHUNK), F32)

        @pl.when(pl.program_id(0) == 0)
        def _():
            dw_ref[...] = jnp.zeros_like(dw_ref)
            dbs_ref[...] = jnp.zeros_like(dbs_ref)

        for g in range(groups):
            sl = slice(g * CHUNK, (g + 1) * CHUNK)
            w = jnp.where(tril, w_ref[g], jnp.zeros((), BF16))
            mixed = _dot(w, vn[:, sl], NN) + bb_ref[g]
            dmix = da[:, sl] * u[:, sl]
            dmix16 = dmix.astype(BF16)
            duv_ref[:, sl] = (da[:, sl] * mixed * _gelu_grad(up[:, sl])).astype(BF16)
            dvn_ref[:, sl] = _dot(w, dmix16, TN)
            dw_ref[g] += jnp.where(tril, _dot(dmix16, vn[:, sl], NT), 0.0)
            dbs_ref[g:g + 1, :] += lax.dot_general(ones, dmix, (NT, ((), ())), precision=lax.Precision.HIGHEST,
                                                   preferred_element_type=F32)[0:1]
        dvn = dvn_ref[...]
        dvhat = dvn * ln_gain
        dva = rs * (dvhat - jnp.mean(dvhat, axis=-1, keepdims=True) - vhat * jnp.mean(dvhat * vhat, axis=-1, keepdims=True))
        duv_ref[:, a:] = (dva * _gelu_grad(vp)).astype(BF16)
        _accumulate(dg_ref, jnp.sum(dvn * vhat, axis=0, keepdims=True))
        _accumulate(db_ref, jnp.sum(dvn, axis=0, keepdims=True))

    full3 = pl.BlockSpec((groups, CHUNK, CHUNK), lambda c: (0, 0, 0))
    return _pcall(
        body, (z, z, dab, ln_g, ln_b, w16, bias_b), name=name, grid=(t // CHUNK,),
        in_specs=[pl.BlockSpec((CHUNK, a), lambda c: (c, 0)), pl.BlockSpec((CHUNK, a), lambda c: (c, 1)),
                  pl.BlockSpec((CHUNK, a), lambda c: (c, 0)), _vec_spec(a), _vec_spec(a), full3, full3],
        out_specs=[pl.BlockSpec((CHUNK, 2 * a), lambda c: (c, 0)), _vec_spec(a), _vec_spec(a), full3,
                   pl.BlockSpec((groups, CHUNK), lambda c: (0, 0))],
        out_shape=[jax.ShapeDtypeStruct((t, 2 * a), BF16), jax.ShapeDtypeStruct((1, a), F32), jax.ShapeDtypeStruct((1, a), F32),
                   jax.ShapeDtypeStruct((groups, CHUNK, CHUNK), F32), jax.ShapeDtypeStruct((groups, CHUNK), F32)],
        scratch=[pltpu.VMEM((CHUNK, a), F32)], sem=("arbitrary",), comms=comms)


def _dil_masks(d):
    qi = lax.broadcasted_iota(jnp.int32, (CHUNK, CHUNK), 0)
    kj = lax.broadcasted_iota(jnp.int32, (CHUNK, CHUNK), 1)
    dist_c = qi - kj
    return dist_c >= 0, dist_c <= 0, (dist_c * d).astype(F32), ((dist_c + CHUNK) * d).astype(F32)


def _alibi_slope(h, heads):
    return 2.0 ** (-8.0 * (h + 1) / heads)


def _dil_view(z, d):
    t, w = z.shape[0], z.shape[1] // 5
    if d == 1:
        return z, 5, 2
    return z[:, 2 * w:].reshape(t // d, d * 3 * w), 3, 0


def _dil_fwd(z, d, name, comms=()):
    t = z.shape[0]
    w = z.shape[1] // 5
    heads = w // HEAD_DIM
    nb = t // d // CHUNK
    scale = HEAD_DIM ** -0.5
    zv, mult, col_q = _dil_view(z, d)

    def body(q_ref, kp_ref, kc_ref, vp_ref, vc_ref, o_ref, l_ref):
        ok_c, ok_p0, bias_c, bias_p = _dil_masks(d)
        ok_p = ok_p0 & (pl.program_id(1) > 0)
        hs = range(heads)
        sl = [slice(h * HEAD_DIM, (h + 1) * HEAD_DIM) for h in hs]
        slope = [_alibi_slope(h, heads) for h in hs]
        ones = jnp.ones((CHUNK, HEAD_DIM), BF16)
        s_c = [_dot(q_ref[:, sl[h]], kc_ref[:, sl[h]], NT) for h in hs]
        s_p = [_dot(q_ref[:, sl[h]], kp_ref[:, sl[h]], NT) for h in hs]
        s_c = [jnp.where(ok_c, s_c[h] * scale - slope[h] * bias_c, NEG) for h in hs]
        s_p = [jnp.where(ok_p, s_p[h] * scale - slope[h] * bias_p, NEG) for h in hs]
        m = [jnp.max(jnp.maximum(s_c[h], s_p[h]), axis=1, keepdims=True) for h in hs]
        p_c = [jnp.exp(s_c[h] - m[h]).astype(BF16) for h in hs]
        p_p = [jnp.exp(s_p[h] - m[h]).astype(BF16) for h in hs]
        den = [_dot(p_c[h], ones, NN) + _dot(p_p[h], ones, NN) for h in hs]
        o = [_dot(p_c[h], vc_ref[:, sl[h]], NN) + _dot(p_p[h], vp_ref[:, sl[h]], NN) for h in hs]
        l_ref[...] = jnp.zeros_like(l_ref)
        for h in hs:
            o_ref[:, sl[h]] = (o[h] / den[h]).astype(BF16)
            l_ref[:, h:h + 1] = m[h] + jnp.log(den[h][:, 0:1])

    def zspec(col, prev):
        if prev:
            return pl.BlockSpec((CHUNK, w), lambda r, n: (jnp.maximum(n - 1, 0), r * mult + col_q + col))
        return pl.BlockSpec((CHUNK, w), lambda r, n: (n, r * mult + col_q + col))

    res = _pcall(
        body, (zv, zv, zv, zv, zv), name=name, grid=(d, nb),
        in_specs=[zspec(0, False), zspec(1, True), zspec(1, False), zspec(2, True), zspec(2, False)],
        out_specs=[pl.BlockSpec((CHUNK, w), lambda r, n: (n, r)), pl.BlockSpec((CHUNK, HEAD_DIM), lambda r, n: (n, r))],
        out_shape=[jax.ShapeDtypeStruct((t // d, d * w), BF16), jax.ShapeDtypeStruct((t // d, d * HEAD_DIM), F32)],
        sem=("parallel", "parallel"), comms=comms)
    (o, lse), rws = res if comms else (res, None)
    outs = (o.reshape(t, w), lse.reshape(t, HEAD_DIM))
    return (outs, rws) if comms else outs


def _dil_merge(a_out, outs, lses, name, comms=()):
    t, a = a_out.shape
    w = outs[0].shape[1]
    heads = w // HEAD_DIM
    nbr = len(outs)

    def body(a_ref, *rest):
        o_refs, l_refs, (ab_ref, lt_ref) = rest[:nbr], rest[nbr:2 * nbr], rest[2 * nbr:]
        ls = [r[...] for r in l_refs]
        m = functools.reduce(jnp.maximum, ls)
        ws = [jnp.exp(l - m) for l in ls]
        tot = functools.reduce(jnp.add, ws)
        ws = [wt / tot for wt in ws]
        ab_ref[:, :a] = a_ref[...]
        for h in range(heads):
            sl = slice(h * HEAD_DIM, (h + 1) * HEAD_DIM)
            mix = functools.reduce(jnp.add, [wt[:, h:h + 1] * r[:, sl].astype(F32) for wt, r in zip(ws, o_refs)])
            ab_ref[:, a + h * HEAD_DIM:a + (h + 1) * HEAD_DIM] = mix.astype(BF16)
        lt_ref[...] = m + jnp.log(tot)

    return _pcall(
        body, (a_out, *outs, *lses), name=name, grid=(t // NORM_ROWS,),
        in_specs=[_row_spec(a)] + [_row_spec(w)] * nbr + [_row_spec(HEAD_DIM)] * nbr,
        out_specs=[_row_spec(a + w), _row_spec(HEAD_DIM)],
        out_shape=[jax.ShapeDtypeStruct((t, a + w), BF16), jax.ShapeDtypeStruct((t, HEAD_DIM), F32)],
        sem=("parallel",), comms=comms)


def _dil_delta(ab, dab, name):
    t, aw = ab.shape
    w = aw // 2
    heads = w // HEAD_DIM

    def body(o_ref, do_ref, dl_ref):
        dl_ref[...] = jnp.zeros_like(dl_ref)
        for h in range(heads):
            sl = slice(h * HEAD_DIM, (h + 1) * HEAD_DIM)
            dl_ref[:, h:h + 1] = jnp.sum(do_ref[:, sl].astype(F32) * o_ref[:, sl].astype(F32), axis=1, keepdims=True)

    half = pl.BlockSpec((NORM_ROWS, w), lambda i: (i, 1))
    return pl.pallas_call(body, name=name, grid=(t // NORM_ROWS,), in_specs=[half, half], out_specs=_row_spec(HEAD_DIM),
                          out_shape=jax.ShapeDtypeStruct((t, HEAD_DIM), F32), compiler_params=_params("parallel"))(ab, dab)


def _dil_bwd(z, dab, ltot, delta, d, name, comms=()):
    t = z.shape[0]
    w = z.shape[1] // 5
    heads = w // HEAD_DIM
    nb = t // d // CHUNK
    scale = HEAD_DIM ** -0.5

    def body(q_ref, qn_ref, kp_ref, kc_ref, vp_ref, vc_ref, do_ref, don_ref, l_ref, ln_ref, dl_ref, dln_ref,
             dq_ref, dk_ref, dv_ref):
        n = pl.program_id(1)
        ok_c, ok_p0, bias_c, bias_p = _dil_masks(d)
        ok_p = ok_p0 & (n > 0)
        ok_n = ok_p0 & (n < nb - 1)
        hs = range(heads)
        sl = [slice(h * HEAD_DIM, (h + 1) * HEAD_DIM) for h in hs]
        slope = [_alibi_slope(h, heads) for h in hs]
        q, qn = [q_ref[:, s] for s in sl], [qn_ref[:, s] for s in sl]
        kp, kc = [kp_ref[:, s] for s in sl], [kc_ref[:, s] for s in sl]
        vp, vc = [vp_ref[:, s] for s in sl], [vc_ref[:, s] for s in sl]
        do, don = [do_ref[:, s] for s in sl], [don_ref[:, s] for s in sl]
        s_c = [_dot(q[h], kc[h], NT) for h in hs]
        s_p = [_dot(q[h], kp[h], NT) for h in hs]
        s_n = [_dot(qn[h], kc[h], NT) for h in hs]
        dp_c = [_dot(do[h], vc[h], NT) for h in hs]
        dp_p = [_dot(do[h], vp[h], NT) for h in hs]
        dp_n = [_dot(don[h], vc[h], NT) for h in hs]
        delta = [dl_ref[:, h:h + 1] for h in hs]
        delta_n = [dln_ref[:, h:h + 1] for h in hs]
        p_c = [jnp.exp(jnp.where(ok_c, s_c[h] * scale - slope[h] * bias_c, NEG) - l_ref[:, h:h + 1]) for h in hs]
        p_p = [jnp.exp(jnp.where(ok_p, s_p[h] * scale - slope[h] * bias_p, NEG) - l_ref[:, h:h + 1]) for h in hs]
        p_n = [jnp.exp(jnp.where(ok_n, s_n[h] * scale - slope[h] * bias_p, NEG) - ln_ref[:, h:h + 1]) for h in hs]
        ds_c = [(p_c[h] * (dp_c[h] - delta[h])).astype(BF16) for h in hs]
        ds_p = [(p_p[h] * (dp_p[h] - delta[h])).astype(BF16) for h in hs]
        ds_n = [(p_n[h] * (dp_n[h] - delta_n[h])).astype(BF16) for h in hs]
        dq = [_dot(ds_c[h], kc[h], NN) + _dot(ds_p[h], kp[h], NN) for h in hs]
        dk = [_dot(ds_c[h], q[h], TN) + _dot(ds_n[h], qn[h], TN) for h in hs]
        dv = [_dot(p_c[h].astype(BF16), do[h], TN) + _dot(p_n[h].astype(BF16), don[h], TN) for h in hs]
        for h in hs:
            dq_ref[:, sl[h]] = (dq[h] * scale).astype(BF16)
            dk_ref[:, sl[h]] = (dk[h] * scale).astype(BF16)
            dv_ref[:, sl[h]] = dv[h].astype(BF16)

    def spec(mult, col, shift, width=w):
        if shift < 0:
            return pl.BlockSpec((CHUNK, width), lambda r, n: (jnp.maximum(n - 1, 0), r * mult + col))
        if shift > 0:
            return pl.BlockSpec((CHUNK, width), lambda r, n: (jnp.minimum(n + 1, nb - 1), r * mult + col))
        return pl.BlockSpec((CHUNK, width), lambda r, n: (n, r * mult + col))

    zv, mult, cq = _dil_view(z, d)
    dov = dab[:, w:].reshape(t // d, d * w)
    lv = ltot.reshape(t // d, d * HEAD_DIM)
    dlv = delta.reshape(t // d, d * HEAD_DIM)
    ospec = spec(1, 0, 0)
    res = _pcall(
        body, (zv, zv, zv, zv, zv, zv, dov, dov, lv, lv, dlv, dlv), name=name, grid=(d, nb),
        in_specs=[spec(mult, cq, 0), spec(mult, cq, 1), spec(mult, cq + 1, -1), spec(mult, cq + 1, 0),
                  spec(mult, cq + 2, -1), spec(mult, cq + 2, 0), spec(1, 0, 0), spec(1, 0, 1),
                  spec(1, 0, 0, HEAD_DIM), spec(1, 0, 1, HEAD_DIM), spec(1, 0, 0, HEAD_DIM), spec(1, 0, 1, HEAD_DIM)],
        out_specs=[ospec, ospec, ospec], out_shape=[jax.ShapeDtypeStruct((t // d, d * w), BF16)] * 3,
        sem=("parallel", "parallel"), comms=comms)
    outs, rws = res if comms else (res, None)
    outs = [o.reshape(t, w) for o in outs]
    return (outs, rws) if comms else outs


def _dz_assemble(duv, parts, name):
    t, a2 = duv.shape
    w = parts[0][0].shape[1]
    nbr = len(parts)

    def body(duv_ref, *rest):
        refs, dz_ref = rest[:-1], rest[-1]
        dz_ref[:, :a2] = duv_ref[...]
        for i in range(3):
            tot = functools.reduce(jnp.add, [refs[b * 3 + i][...].astype(F32) for b in range(nbr)])
            dz_ref[:, a2 + i * w:a2 + (i + 1) * w] = tot.astype(BF16)

    flat = [p for branch in parts for p in branch]
    return pl.pallas_call(
        body, name=name, grid=(t // NORM_ROWS,), in_specs=[_row_spec(a2)] + [_row_spec(w)] * len(flat),
        out_specs=_row_spec(a2 + 3 * w), out_shape=jax.ShapeDtypeStruct((t, a2 + 3 * w), BF16),
        compiler_params=_params("parallel"),
    )(duv, *flat)


def _split_dot(x, m16):
    hi = x.astype(BF16)
    lo = (x - hi.astype(F32)).astype(BF16)
    return _dot(hi, m16, NN) + _dot(lo, m16, NN)


SB_DEAD = -110.0


def _sb_scaled(q):
    return (q.astype(F32) * (HEAD_DIM ** -0.5)).astype(BF16)


def _sb_log(qs, kj, below):
    zt = _dot(qs, kj, NT)
    sp = jnp.maximum(zt, 0.0) + jnp.log(1.0 + jnp.exp(-jnp.abs(zt)))
    return zt - sp, (-sp if below is None else jnp.where(below, -sp, 0.0))


def _sb_alive(s, i, c_run):
    return (s <= i) & (jnp.max(c_run) > SB_DEAD)


def _sb_fwd(zc, name, comms=()):
    t = zc.shape[0]
    c = zc.shape[1] // 3
    heads = c // HEAD_DIM
    blk = min(SB_BLOCK, t)

    def body(q_ref, k_ref, v_ref, o_ref, ct_ref, nb_ref):
        i = pl.program_id(1)
        qs = _sb_scaled(q_ref[...])
        rows = lax.broadcasted_iota(jnp.int32, (blk, blk), 0)
        cols = lax.broadcasted_iota(jnp.int32, (blk, blk), 1)
        below = rows > cols
        m_right = below.astype(BF16)

        def tile(carry, diagonal):
            s, acc, c_run = carry
            off = pl.multiple_of((i - s) * blk, blk)
            log_beta, l = _sb_log(qs, k_ref[pl.ds(off, blk), :], below if diagonal else None)
            a = jnp.exp(log_beta + (c_run + _split_dot(l, m_right)))
            if diagonal:
                a = jnp.where(below, a, 0.0)
            acc = acc + _dot(a.astype(BF16), v_ref[pl.ds(off, blk), :], NN)
            return s + 1, acc, c_run + jnp.sum(l, axis=1, keepdims=True)

        first = tile((jnp.int32(0), jnp.zeros((blk, HEAD_DIM), F32), jnp.zeros((blk, 1), F32)), True)
        swept, acc, c_tot = lax.while_loop(lambda carry: _sb_alive(carry[0], i, carry[2]), lambda carry: tile(carry, False), first)
        o_ref[...] = acc.astype(BF16)
        ct_ref[...] = jnp.broadcast_to(c_tot, (blk, HEAD_DIM))
        nb_ref[...] = jnp.zeros((blk, HEAD_DIM), F32) + swept.astype(F32)

    qspec = pl.BlockSpec((blk, HEAD_DIM), lambda h, i: (i, h))
    return _pcall(body, (zc, zc, zc), name=name, grid=(heads, t // blk),
                  in_specs=[qspec, pl.BlockSpec((t, HEAD_DIM), lambda h, i: (0, heads + h)),
                            pl.BlockSpec((t, HEAD_DIM), lambda h, i: (0, 2 * heads + h))],
                  out_specs=[qspec, qspec, qspec],
                  out_shape=[jax.ShapeDtypeStruct((t, c), BF16), jax.ShapeDtypeStruct((t, c), F32), jax.ShapeDtypeStruct((t, c), F32)],
                  sem=("parallel", "parallel"), comms=comms)


def _sb_bwd(zc, ctot, swept, do, name, comms=()):
    t = zc.shape[0]
    c = zc.shape[1] // 3
    heads = c // HEAD_DIM
    blk = min(SB_BLOCK, t)
    scale = HEAD_DIM ** -0.5

    def body(q_ref, k_ref, v_ref, ct_ref, nb_ref, do_ref, dq_ref, dk_ref, dv_ref):
        i = pl.program_id(1)

        @pl.when(i == 0)
        def _():
            dk_ref[...] = jnp.zeros_like(dk_ref)
            dv_ref[...] = jnp.zeros_like(dv_ref)

        qs = _sb_scaled(q_ref[...])
        dov = do_ref[...]
        c_tot = ct_ref[:, 0:1]
        n_blocks = jnp.clip(jnp.max(nb_ref[0:8, :]).astype(jnp.int32), 1, i + 1)
        rows = lax.broadcasted_iota(jnp.int32, (blk, blk), 0)
        cols = lax.broadcasted_iota(jnp.int32, (blk, blk), 1)
        below = rows > cols
        m_upto = (rows <= cols).astype(BF16)
        m_left = (rows < cols).astype(BF16)

        def tile(j, carry, diagonal):
            dq, l_run, w_run = carry
            off = pl.multiple_of(j * blk, blk)
            kj = k_ref[pl.ds(off, blk), :]
            vj = v_ref[pl.ds(off, blk), :]
            log_beta, l = _sb_log(qs, kj, below if diagonal else None)
            a = jnp.exp(log_beta + (c_tot - l_run - _split_dot(l, m_upto)))
            if diagonal:
                a = jnp.where(below, a, 0.0)
            wgt = a * _dot(dov, vj, NT)
            before = w_run + _split_dot(wgt, m_left)
            dz = wgt * jnp.exp(l) - jnp.exp(log_beta) * before
            if diagonal:
                dz = jnp.where(below, dz, 0.0)
            dz16 = dz.astype(BF16)
            dk_ref[pl.ds(off, blk), :] += _dot(dz16, qs, TN)
            dv_ref[pl.ds(off, blk), :] += _dot(a.astype(BF16), dov, TN)
            return (dq + _dot(dz16, kj, NN), l_run + jnp.sum(l, axis=1, keepdims=True),
                    w_run + jnp.sum(wgt, axis=1, keepdims=True))

        zero = jnp.zeros((blk, 1), F32)
        carry = lax.fori_loop(i + 1 - n_blocks, i, lambda j, carry: tile(j, carry, False),
                              (jnp.zeros((blk, HEAD_DIM), F32), zero, zero))
        dq_ref[...] = tile(i, carry, True)[0] * scale

    qspec = pl.BlockSpec((blk, HEAD_DIM), lambda h, i: (i, h))
    full = pl.BlockSpec((t, HEAD_DIM), lambda h, i: (0, h))
    return _pcall(body, (zc, zc, zc, ctot, swept, do), name=name, grid=(heads, t // blk),
                  in_specs=[qspec, pl.BlockSpec((t, HEAD_DIM), lambda h, i: (0, heads + h)),
                            pl.BlockSpec((t, HEAD_DIM), lambda h, i: (0, 2 * heads + h)), qspec, qspec, qspec],
                  out_specs=[qspec, full, full], out_shape=[jax.ShapeDtypeStruct((t, c), F32)] * 3,
                  sem=("arbitrary", "arbitrary"), comms=comms)


def _concat_bf16(parts, name, comms=()):
    t, c = parts[0].shape

    def body(*refs):
        for k, r in enumerate(refs[:-1]):
            refs[-1][:, k * c:(k + 1) * c] = r[...].astype(BF16)

    res = _pcall(body, tuple(parts), name=name, grid=(t // NORM_ROWS,), in_specs=[_row_spec(c)] * len(parts),
                 out_specs=[_row_spec(c * len(parts))], out_shape=[jax.ShapeDtypeStruct((t, c * len(parts)), BF16)],
                 sem=("parallel",), comms=comms)
    return (res[0][0], res[1]) if comms else res[0]


KIND = {"ab_w_in": "col", "ab_w_out": "row", "sb_w_in": "col", "sb_w_out": "row",
        "ffn_w1_0": "col", "ffn_w1_1": "col", "ffn_w2_0": "row", "ffn_w2_1": "row"}
X_Y, DIAG, CHIPS = (2, 4), (6,), (2, 4, 6)


def _local_step(x, target, norms, sgu, big, bufs=None):
    g = {k: [v[l:l + 1] for l in range(2)] for k, v in norms.items()}
    ln_g, ln_b, sgu_w, sgu_b = sgu
    groups = sgu_w.shape[0]
    w16 = sgu_w.astype(BF16)
    bias_b = jnp.broadcast_to(sgu_b[:, :, None], (groups, CHUNK, CHUNK))
    big, dws, psum, dist = dict(big), {}, {}, bufs is not None
    pair, got = (dict(bufs[0]), dict(bufs[1])) if dist else ({}, {})

    def run(fn, *args, ops=(), **kw):
        if not dist or not ops:
            return fn(*args, **kw)
        make = {"gs": lambda k, p: _GatherSend(big[k], KIND[k], p), "gf": lambda k, p: _GatherFwd(big[k], KIND[k], p),
                "swap": lambda k, p: _PairSwap(dws[k], pair[k], KIND[k]), "chips": lambda k, p: _ChipScatter(psum[k], got[k], p)}
        out, rws = fn(*args, comms=[make[op](k, p) for op, k, p in ops], **kw)
        for (op, k, _), r in zip(ops, rws):
            if op in ("gs", "gf"):
                big[k] = r[0]
            elif op == "swap":
                psum[k] = _pair_sum(dws[k], r[0], KIND[k], f"pair_sum_{k}")
            else:
                got[k] = r[0]
        return out

    h1_0 = _rms_fwd(x, g["pre_mix"][0], "rms_in")
    z0 = run(_matmul, h1_0, big["ab_w_in"], "nn", BF16, "ab_in", ops=[("gs", "ffn_w1_0", X_Y)])
    a_out = run(_sgu_fwd, z0, ln_g, ln_b, w16, bias_b, "sgu_fwd", ops=[("gf", "ffn_w1_0", X_Y), ("gs", "ab_w_out", CHIPS)])
    branches = [run(_dil_fwd, z0, 1, "dil_fwd_1", ops=[("gs", "ffn_w1_0", DIAG), ("gf", "ab_w_out", CHIPS)]),
                run(_dil_fwd, z0, 4, "dil_fwd_4", ops=[("gf", "ffn_w1_0", DIAG), ("gs", "ffn_w2_0", X_Y)]),
                run(_dil_fwd, z0, 16, "dil_fwd_16", ops=[("gs", "ffn_w2_0", DIAG)])]
    ab, ltot = run(_dil_merge, a_out, [b[0] for b in branches], [b[1] for b in branches], "dil_merge", ops=[("gf", "ffn_w2_0", CHIPS)])
    y_0 = _matmul(ab, big["ab_w_out"], "nn", F32, "ab_out")
    x1, h2_0 = run(_post_pre_fwd, y_0, g["post_mix"][0], x, g["pre_ffn"][0], "norm_mix0", ops=[("gs", "sb_w_out", CHIPS)])
    r_0 = run(_matmul, h2_0, big["ffn_w1_0"], "nn", BF16, "ffn_up_0", relu_out=True,
              ops=[("gs", "sb_w_in", CHIPS), ("gf", "sb_w_out", CHIPS)])
    y2_0 = run(_matmul, r_0, big["ffn_w2_0"], "nn", F32, "ffn_down_0", a_square=True,
               ops=[("gf", "sb_w_in", CHIPS), ("gs", "ffn_w1_1", X_Y)])
    x2, h1_1 = run(_post_pre_fwd, y2_0, g["post_ffn"][0], x1, g["pre_mix"][1], "norm_ffn0", ops=[("gf", "ffn_w1_1", X_Y)])
    zc = run(_matmul, h1_1, big["sb_w_in"], "nn", BF16, "sb_in", ops=[("gs", "ffn_w1_1", DIAG)])
    o_sb, ct_sb, nb_sb = run(_sb_fwd, zc, "sb_fwd", ops=[("gf", "ffn_w1_1", DIAG), ("gs", "ffn_w2_1", CHIPS)])
    y_1 = run(_matmul, o_sb, big["sb_w_out"], "nn", F32, "sb_out", ops=[("gf", "ffn_w2_1", CHIPS)])
    x3, h2_1 = _post_pre_fwd(y_1, g["post_mix"][1], x2, g["pre_ffn"][1], "norm_mix1")
    r_1 = _matmul(h2_1, big["ffn_w1_1"], "nn", BF16, "ffn_up_1", relu_out=True)
    y2_1 = _matmul(r_1, big["ffn_w2_1"], "nn", F32, "ffn_down_1", a_square=True)
    loss, dx4, dy2_1, dg_post_ffn1 = _final_fwd_bwd(y2_1, g["post_ffn"][1], x3, target, "loss")

    da = _matmul(dy2_1, big["ffn_w2_1"], "nt", BF16, "ffn_da_1", mul2=r_1)
    dws["ffn_w2_1"] = _matmul(r_1, dy2_1, "tn", BF16, "ffn_dw2_1", a_square=True)
    dh2 = run(_matmul, da, big["ffn_w1_1"], "nt", F32, "ffn_dh_1", ops=[("swap", "ffn_w2_1", None)])
    dws["ffn_w1_1"] = run(_matmul, h2_1, da, "tn", BF16, "ffn_dw1_1", ops=[("chips", "ffn_w2_1", X_Y)])
    dx3, dy_1, dg_pre_ffn1, dg_post_mix1 = run(_pre_post_bwd, x3, g["pre_ffn"][1], dh2, dx4, y_1, g["post_mix"][1], "norm_bwd_mix1",
                                               ops=[("swap", "ffn_w1_1", None)])
    do_sb = _matmul(dy_1, big["sb_w_out"], "nt", BF16, "sb_out_dx")
    dws["sb_w_out"] = _matmul(o_sb, dy_1, "tn", BF16, "sb_out_dw")
    dqkv = run(_sb_bwd, zc, ct_sb, nb_sb, do_sb, "sb_bwd",
               ops=[("chips", "ffn_w2_1", DIAG), ("chips", "ffn_w1_1", CHIPS), ("swap", "sb_w_out", None)])
    dzc = run(_concat_bf16, dqkv, "sb_dz", ops=[("chips", "sb_w_out", X_Y)])
    dh1 = run(_matmul, dzc, big["sb_w_in"], "nt", F32, "sb_in_dx", ops=[("chips", "sb_w_out", DIAG)])
    dws["sb_w_in"] = _matmul(h1_1, dzc, "tn", BF16, "sb_in_dw")
    dx2, dy2_0, dg_pre_mix1, dg_post_ffn0 = run(_pre_post_bwd, x2, g["pre_mix"][1], dh1, dx3, y2_0, g["post_ffn"][0], "norm_bwd_ffn0",
                                                ops=[("swap", "sb_w_in", None)])
    da = run(_matmul, dy2_0, big["ffn_w2_0"], "nt", BF16, "ffn_da_0", mul2=r_0, ops=[("chips", "sb_w_in", X_Y)])
    dws["ffn_w2_0"] = run(_matmul, r_0, dy2_0, "tn", BF16, "ffn_dw2_0", a_square=True, ops=[("chips", "sb_w_in", DIAG)])
    dws["ffn_w1_0"] = run(_matmul, h2_0, da, "tn", BF16, "ffn_dw1_0", ops=[("swap", "ffn_w2_0", None)])
    dh2 = run(_matmul, da, big["ffn_w1_0"], "nt", F32, "ffn_dh_0", ops=[("chips", "ffn_w2_0", X_Y), ("swap", "ffn_w1_0", None)])
    dx1, dy_0, dg_pre_ffn0, dg_post_mix0 = _pre_post_bwd(x1, g["pre_ffn"][0], dh2, dx2, y_0, g["post_mix"][0], "norm_bwd_mix0")
    dab = _matmul(dy_0, big["ab_w_out"], "nt", BF16, "ab_out_dx")
    dws["ab_w_out"] = _matmul(ab, dy_0, "tn", BF16, "ab_out_dw")
    duv, d_ln_g, d_ln_b, d_sgu_w, d_sgu_b = run(_sgu_bwd, z0, dab, ln_g, ln_b, w16, bias_b, "sgu_bwd", ops=[("chips", "ffn_w2_0", DIAG)])
    delta = _dil_delta(ab, dab, "dil_delta")
    parts = [run(_dil_bwd, z0, dab, ltot, delta, 1, "dil_bwd_1", ops=[("chips", "ffn_w1_0", X_Y), ("swap", "ab_w_out", None)]),
             run(_dil_bwd, z0, dab, ltot, delta, 4, "dil_bwd_4", ops=[("chips", "ffn_w1_0", DIAG)]),
             run(_dil_bwd, z0, dab, ltot, delta, 16, "dil_bwd_16", ops=[("chips", "ab_w_out", CHIPS)])]
    dz0 = _dz_assemble(duv, parts, "dz_assemble")
    dws["ab_w_in"] = _matmul(h1_0, dz0, "tn", BF16, "ab_in_dw")
    dh1 = run(_matmul, dz0, big["ab_w_in"], "nt", F32, "ab_in_dx", ops=[("swap", "ab_w_in", None)])
    grad_x, dg_pre_mix0 = run(_pre_post_bwd, x, g["pre_mix"][0], dh1, dx1, None, None, "norm_bwd_in", ops=[("chips", "ab_w_in", X_Y)])

    d_norms = {
        "pre_mix": jnp.concatenate([dg_pre_mix0, dg_pre_mix1]), "post_mix": jnp.concatenate([dg_post_mix0, dg_post_mix1]),
        "pre_ffn": jnp.concatenate([dg_pre_ffn0, dg_pre_ffn1]), "post_ffn": jnp.concatenate([dg_post_ffn0, dg_post_ffn1]),
    }
    return loss, grad_x, d_norms, (d_ln_g, d_ln_b, d_sgu_w, d_sgu_b), (psum, got) if dist else dws


def _to_bf16_full(w, layer, kind, name):
    _, rows, cols = w.shape
    tr = _tile(rows, 512)
    nblk = rows // tr
    full = (rows, 4 * cols) if kind == "col" else (4 * rows, cols)

    def body(w_ref, o_ref):
        o_ref[...] = w_ref[...].astype(BF16)

    def place(i):
        mine = 2 * lax.axis_index("x") + lax.axis_index("y")
        return (i, mine) if kind == "col" else (mine * nblk + i, 0)

    return pl.pallas_call(
        body, name=name, grid=(nblk,), in_specs=[pl.BlockSpec((None, tr, cols), lambda i: (layer, i, 0))],
        out_specs=pl.BlockSpec((tr, cols), place), out_shape=jax.ShapeDtypeStruct(full, BF16), compiler_params=_params("parallel"),
    )(w)


def _pair_sum(dw16, pair, kind, name):
    rh, cs = _half_shape(dw16.shape, kind)
    tr = _tile(rh, 256)
    nblk = rh // tr

    def body(dw_ref, pair_ref, o_ref):
        o_ref[...] = (dw_ref[...].astype(F32) + pair_ref[...].astype(F32)).astype(BF16)

    def own(s, i):
        c = lax.axis_index("c")
        return (c * nblk + i, s) if kind == "col" else ((2 * s + c) * nblk + i, 0)

    spec3 = pl.BlockSpec((None, tr, cs), lambda s, i: (s, i, 0))
    return pl.pallas_call(
        body, name=name, grid=(4, nblk), in_specs=[pl.BlockSpec((tr, cs), own), spec3], out_specs=spec3,
        out_shape=jax.ShapeDtypeStruct((4, rh, cs), BF16), compiler_params=_params("parallel", "parallel"),
    )(dw16, pair)


def _owner_sum(psum, got, buf, layer, name, comms=()):
    _, rh, cs = psum.shape
    tr = _tile(rh, 256)

    def body(p_ref, got_ref, buf_ref, o_ref):
        tot = p_ref[...].astype(F32)
        for j in range(3):
            tot = tot + got_ref[j].astype(F32)
        o_ref[...] = tot

    res = _pcall(
        body, (psum, got, buf), name=name, grid=(rh // tr,),
        in_specs=[pl.BlockSpec((None, tr, cs), lambda i: (2 * lax.axis_index("x") + lax.axis_index("y"), i, 0)),
                  pl.BlockSpec((3, tr, cs), lambda i: (0, i, 0)), ANY],
        out_specs=[pl.BlockSpec((None, None, tr, cs), lambda i: (layer, lax.axis_index("c"), i, 0))],
        out_shape=[jax.ShapeDtypeStruct(buf.shape, F32)], sem=("parallel",), comms=comms, aliases={2: 0})
    return (res[0][0], res[1]) if comms else res[0]


def _adamw_math(w, g, m, v):
    m = ADAM_B1 * m + (1.0 - ADAM_B1) * g
    v = ADAM_B2 * v + (1.0 - ADAM_B2) * (g * g)
    m_hat = m / (1.0 - ADAM_B1 ** ADAM_STEP)
    v_hat = v / (1.0 - ADAM_B2 ** ADAM_STEP)
    return -ADAM_LR * (m_hat / (jnp.sqrt(v_hat) + ADAM_EPS) + ADAM_WD * w), m, v


def _adamw(w, g, m, v, name, comms=()):
    layers, rows, cols = w.shape
    tr = _tile(rows, 256)

    def body(w_ref, g_ref, m_ref, v_ref, d_ref, mo_ref, vo_ref):
        d_ref[...], mo_ref[...], vo_ref[...] = _adamw_math(w_ref[...], g_ref[...], m_ref[...], v_ref[...])

    spec = pl.BlockSpec((None, tr, cols), lambda l, i: (l, i, 0))
    return _pcall(body, (w, g, m, v), name=name, grid=(layers, rows // tr), in_specs=[spec] * 4, out_specs=[spec] * 3,
                  out_shape=[jax.ShapeDtypeStruct(w.shape, F32)] * 3, sem=("parallel", "parallel"), comms=comms)


def _pack(arrays):
    flat = jnp.concatenate([a.reshape(-1) for a in arrays])
    pad = (-flat.shape[0]) % 1024
    return jnp.pad(flat, (0, pad)).reshape(-1, 128)


def _unpack(packed, like):
    flat = packed.reshape(-1)
    out, off = [], 0
    for a in like:
        out.append(flat[off:off + a.size].reshape(a.shape))
        off += a.size
    return out


class _SmallGather:
    n_sems = 7

    def __init__(self, g, parts):
        self.ro, self.rw = [g], [parts]

    def start(self, ro, rw, send, recv):
        x, y, c, _ = _place()
        for j in range(1, 8):
            _remote(ro[0], rw[0].at[4 * x + 2 * y + c], send(j - 1), recv(j - 1), _flip(x, y, c, j)).start()

    def finish(self, ro, rw, send, recv):
        x, y, c, _ = _place()
        for j in range(1, 8):
            px, py, pc = _flip(x, y, c, j)
            slot = rw[0].at[4 * px + 2 * py + pc]
            cp = _remote(slot, slot, send(j - 1), recv(j - 1), (x, y, c))
            cp.wait_recv()
            cp.wait_send()


def _small_update(own, parts, w, m, v, name):
    rows = w.shape[0]

    def body(own_ref, p_ref, w_ref, m_ref, v_ref, g_ref, d_ref, mo_ref, vo_ref):
        me = 4 * lax.axis_index("x") + 2 * lax.axis_index("y") + lax.axis_index("c")
        g = jnp.where(me == 0, own_ref[...], p_ref[0])
        for k in range(1, 8):
            g = g + jnp.where(me == k, own_ref[...], p_ref[k])
        g_ref[...] = g
        d_ref[...], mo_ref[...], vo_ref[...] = _adamw_math(w_ref[...], g, m_ref[...], v_ref[...])

    return pl.pallas_call(body, name=name, out_shape=[jax.ShapeDtypeStruct((rows, 128), F32)] * 4,
                          compiler_params=_params())(own, parts, w, m, v)


SMALL = ("norm_pre_mix", "norm_post_mix", "norm_pre_ffn", "norm_post_ffn", "sgu_ln_g", "sgu_ln_b", "sgu_w", "sgu_b")
BIG = (("ab_w_in", ("ab_w_in",)), ("ab_w_out", ("ab_w_out",)), ("sb_w_in", ("sb_w_in",)), ("sb_w_out", ("sb_w_out",)),
       ("ffn_w1", ("ffn_w1_0", "ffn_w1_1")), ("ffn_w2", ("ffn_w2_0", "ffn_w2_1")))
WEIGHTS = ("norm_pre_mix", "norm_post_mix", "norm_pre_ffn", "norm_post_ffn", "ab_w_in", "sgu_ln_g", "sgu_ln_b", "sgu_w", "sgu_b",
           "ab_w_out", "sb_w_in", "sb_w_out", "ffn_w1", "ffn_w2")


def kernel(x, norm_pre_mix, norm_post_mix, norm_pre_ffn, norm_post_ffn, ab_w_in, sgu_ln_g, sgu_ln_b, sgu_w, sgu_b, ab_w_out, sb_w_in, sb_w_out, ffn_w1, ffn_w2, loss_target, m_norm_pre_mix, m_norm_post_mix, m_norm_pre_ffn, m_norm_post_ffn, m_ab_w_in, m_sgu_ln_g, m_sgu_ln_b, m_sgu_w, m_sgu_b, m_ab_w_out, m_sb_w_in, m_sb_w_out, m_ffn_w1, m_ffn_w2, v_norm_pre_mix, v_norm_post_mix, v_norm_pre_ffn, v_norm_post_ffn, v_ab_w_in, v_sgu_ln_g, v_sgu_ln_b, v_sgu_w, v_sgu_b, v_ab_w_out, v_sb_w_in, v_sb_w_out, v_ffn_w1, v_ffn_w2):
    w = dict(norm_pre_mix=norm_pre_mix, norm_post_mix=norm_post_mix, norm_pre_ffn=norm_pre_ffn, norm_post_ffn=norm_post_ffn,
             ab_w_in=ab_w_in, sgu_ln_g=sgu_ln_g, sgu_ln_b=sgu_ln_b, sgu_w=sgu_w, sgu_b=sgu_b, ab_w_out=ab_w_out, sb_w_in=sb_w_in,
             sb_w_out=sb_w_out, ffn_w1=ffn_w1, ffn_w2=ffn_w2)
    m = dict(norm_pre_mix=m_norm_pre_mix, norm_post_mix=m_norm_post_mix, norm_pre_ffn=m_norm_pre_ffn, norm_post_ffn=m_norm_post_ffn,
             ab_w_in=m_ab_w_in, sgu_ln_g=m_sgu_ln_g, sgu_ln_b=m_sgu_ln_b, sgu_w=m_sgu_w, sgu_b=m_sgu_b, ab_w_out=m_ab_w_out,
             sb_w_in=m_sb_w_in, sb_w_out=m_sb_w_out, ffn_w1=m_ffn_w1, ffn_w2=m_ffn_w2)
    v = dict(norm_pre_mix=v_norm_pre_mix, norm_post_mix=v_norm_post_mix, norm_pre_ffn=v_norm_pre_ffn, norm_post_ffn=v_norm_post_ffn,
             ab_w_in=v_ab_w_in, sgu_ln_g=v_sgu_ln_g, sgu_ln_b=v_sgu_ln_b, sgu_w=v_sgu_w, sgu_b=v_sgu_b, ab_w_out=v_ab_w_out,
             sb_w_in=v_sb_w_in, sb_w_out=v_sb_w_out, ffn_w1=v_ffn_w1, ffn_w2=v_ffn_w2)
    big, pair, got = {}, {}, {}
    for name, keys in BIG:
        for layer, key in enumerate(keys):
            big[key] = _to_bf16_full(w[name], layer, KIND[key], f"bf16_{key}")
            half = _half_shape(big[key].shape, KIND[key])
            pair[key], got[key] = lax.empty((4,) + half, BF16), lax.empty((3,) + half, BF16)
    big["ab_w_in"] = _comm_call([_Gather(big["ab_w_in"], KIND["ab_w_in"])], "gather_first")[0][0]

    norms = {k: w["norm_" + k] for k in ("pre_mix", "post_mix", "pre_ffn", "post_ffn")}
    sgu = (sgu_ln_g, sgu_ln_b, sgu_w[0], sgu_b[0])
    loss_blk, grad_x, d_norms, d_sgu, (psum, got) = _local_step(x[0], loss_target[0], norms, sgu, big, (pair, got))
    loss = lax.psum(loss_blk[0, 0], ("x", "y", "c"))

    grads, deltas, new_m, new_v = {}, {}, {}, {}
    keys_of = dict(BIG)
    bufs, pending = {}, None
    for name in ("ffn_w2", "ffn_w1", "sb_w_in", "sb_w_out", "ab_w_out"):
        buf = lax.empty((len(keys_of[name]), 2) + psum[keys_of[name][0]].shape[1:], F32)
        for layer, key in enumerate(keys_of[name]):
            if pending is not None:
                buf, rws = _owner_sum(psum[key], got[key], buf, layer, f"sum_{key}", comms=[_Join([bufs[pending]])])
                bufs[pending], pending = rws[0][0], None
            else:
                buf = _owner_sum(psum[key], got[key], buf, layer, f"sum_{key}")
        bufs[name], pending = buf, name

    small_g = _pack([d_norms["pre_mix"], d_norms["post_mix"], d_norms["pre_ffn"], d_norms["post_ffn"],
                     d_sgu[0], d_sgu[1], d_sgu[2][None], d_sgu[3][None]])
    parts = lax.empty((8,) + small_g.shape, F32)

    def adamw(name, comms=()):
        grads[name] = bufs[name].reshape(w[name].shape)
        res = _adamw(w[name], grads[name], m[name], v[name], f"adamw_{name}", comms=comms)
        (deltas[name], new_m[name], new_v[name]), rws = res if comms else (res, None)
        return rws

    rws = adamw("ffn_w2", [_Join([bufs["ab_w_out"]]), _ChipScatter(psum["ab_w_in"], got["ab_w_in"], DIAG), _SmallGather(small_g, parts)])
    bufs["ab_w_out"], got["ab_w_in"], parts = rws[0][0], rws[1][0], rws[2][0]
    bufs["ab_w_in"] = _owner_sum(psum["ab_w_in"], got["ab_w_in"], lax.empty((1, 2) + psum["ab_w_in"].shape[1:], F32), 0, "sum_ab_w_in")
    bufs["ab_w_in"] = adamw("ffn_w1", [_Join([bufs["ab_w_in"]])])[0][0]
    for name in ("sb_w_in", "sb_w_out", "ab_w_out", "ab_w_in"):
        adamw(name)

    outs = _small_update(small_g, parts, _pack([w[k] for k in SMALL]), _pack([m[k] for k in SMALL]), _pack([v[k] for k in SMALL]), "small_update")
    like = [w[k] for k in SMALL]
    for dst, packed in zip((grads, deltas, new_m, new_v), outs):
        for k, a in zip(SMALL, _unpack(packed, like)):
            dst[k] = a

    return (loss, grad_x[None], *[grads[k] for k in WEIGHTS], *[deltas[k] for k in WEIGHTS],
            *[new_m[k] for k in WEIGHTS], *[new_v[k] for k in WEIGHTS])
```

```python
import functools

import jax
import jax.numpy as jnp
from jax import lax
from jax.experimental import pallas as pl
from jax.experimental.pallas import tpu as pltpu

F32 = jnp.float32
BF16 = jnp.bfloat16
MESH = pl.DeviceIdType.MESH

HEAD_DIM = 128
CHUNK = 128
DILATIONS = (1, 4, 16)
SB_BLOCK = 256
RMS_EPS = 1e-6
LN_EPS = 1e-5
ADAM_LR, ADAM_B1, ADAM_B2, ADAM_EPS, ADAM_WD, ADAM_STEP = 0.001, 0.9, 0.999, 1e-08, 0.01, 10
NEG = -1e30
V7X_VMEM_LIMIT = 48 * 1024 * 1024
ANY = pl.BlockSpec(memory_space=pl.ANY)


def _params(*sem):
    return pltpu.CompilerParams(dimension_semantics=sem if sem else None, vmem_limit_bytes=V7X_VMEM_LIMIT)


def _tile(n, pref):
    if n <= pref:
        return n
    t = pref
    while n % t:
        t -= 128
    return t


def _dot(a, b, dims):
    return lax.dot_general(a, b, (dims, ((), ())), preferred_element_type=F32)


NN = ((1,), (0,))
NT = ((1,), (1,))
TN = ((0,), (0,))


def _place():
    x, y, c = lax.axis_index("x"), lax.axis_index("y"), lax.axis_index("c")
    return x, y, c, 2 * x + y


def _flip(x, y, c, j):
    return (1 - x if j & 4 else x), (1 - y if j & 2 else y), (1 - c if j & 1 else c)


def _half_shape(full_shape, kind):
    rows, cols = full_shape
    return (rows // 2, cols // 4) if kind == "col" else (rows // 8, cols)


def _half(ref, kind, s, h):
    rh, cs = _half_shape(ref.shape, kind)
    if kind == "col":
        return ref.at[pl.ds(h * rh, rh), pl.ds(s * cs, cs)]
    return ref.at[pl.ds((2 * s + h) * rh, rh), :]


def _remote(src, dst, send, recv, to):
    return pltpu.make_async_remote_copy(src_ref=src, dst_ref=dst, send_sem=send, recv_sem=recv, device_id=to, device_id_type=MESH)


class _Gather:
    n_sems = 6

    def __init__(self, full, kind):
        self.ro, self.rw, self.kind = [], [full], kind

    def start(self, ro, rw, send, recv):
        x, y, c, mine = _place()
        own = _half(rw[0], self.kind, mine, c)
        for k, j in enumerate((2, 4, 6)):
            px, py, _ = _flip(x, y, c, j)
            _remote(own, own, send(k), recv(k), (px, py, c)).start()

    def finish(self, ro, rw, send, recv):
        x, y, c, mine = _place()
        own = _half(rw[0], self.kind, mine, c)
        for k, j in enumerate((2, 4, 6)):
            px, py, _ = _flip(x, y, c, j)
            got = _half(rw[0], self.kind, 2 * px + py, c)
            _remote(got, got, send(k), recv(k), (x, y, c)).wait_recv()
            _remote(got, got, send(3 + k), recv(3 + k), (x, y, 1 - c)).start()
        for k, j in enumerate((2, 4, 6)):
            px, py, _ = _flip(x, y, c, j)
            got = _half(rw[0], self.kind, 2 * px + py, 1 - c)
            _remote(got, got, send(3 + k), recv(3 + k), (x, y, c)).wait_recv()
        for k in range(6):
            _remote(own, own, send(k), recv(k), (x, y, c)).wait_send()


class _GatherSend:
    def __init__(self, full, kind, patterns, part=(0, 1)):
        self.ro, self.rw, self.kind, self.patterns, self.part, self.n_sems = [], [full], kind, patterns, part, len(patterns)

    def _rows(self, half):
        i, n = self.part
        rows = half.shape[0] // n
        return half.at[pl.ds(i * rows, rows), :]

    def start(self, ro, rw, send, recv):
        x, y, c, mine = _place()
        own = self._rows(_half(rw[0], self.kind, mine, c))
        for k, j in enumerate(self.patterns):
            px, py, _ = _flip(x, y, c, j)
            _remote(own, own, send(k), recv(k), (px, py, c)).start()

    def finish(self, ro, rw, send, recv):
        x, y, c, _ = _place()
        for k, j in enumerate(self.patterns):
            px, py, _ = _flip(x, y, c, j)
            got = self._rows(_half(rw[0], self.kind, 2 * px + py, c))
            cp = _remote(got, got, send(k), recv(k), (x, y, c))
            cp.wait_recv()
            cp.wait_send()


class _GatherFwd:
    def __init__(self, full, kind, patterns):
        self.ro, self.rw, self.kind, self.patterns, self.n_sems = [], [full], kind, patterns, len(patterns)

    def start(self, ro, rw, send, recv):
        x, y, c, _ = _place()
        for k, j in enumerate(self.patterns):
            px, py, _ = _flip(x, y, c, j)
            got = _half(rw[0], self.kind, 2 * px + py, c)
            _remote(got, got, send(k), recv(k), (x, y, 1 - c)).start()

    def finish(self, ro, rw, send, recv):
        x, y, c, _ = _place()
        for k, j in enumerate(self.patterns):
            px, py, _ = _flip(x, y, c, j)
            got = _half(rw[0], self.kind, 2 * px + py, 1 - c)
            cp = _remote(got, got, send(k), recv(k), (x, y, c))
            cp.wait_recv()
            cp.wait_send()


class _PairSwap:
    n_sems = 4

    def __init__(self, dw16, pair, kind):
        self.ro, self.rw, self.kind = [dw16], [pair], kind

    def start(self, ro, rw, send, recv):
        x, y, c, _ = _place()
        for s in range(4):
            _remote(_half(ro[0], self.kind, s, 1 - c), rw[0].at[s], send(s), recv(s), (x, y, 1 - c)).start()

    def finish(self, ro, rw, send, recv):
        x, y, c, _ = _place()
        for s in range(4):
            cp = _remote(rw[0].at[s], rw[0].at[s], send(s), recv(s), (x, y, c))
            cp.wait_recv()
            cp.wait_send()


class _ChipScatter:
    def __init__(self, psum, got, patterns):
        self.ro, self.rw, self.patterns, self.n_sems = [psum], [got], patterns, len(patterns)

    def start(self, ro, rw, send, recv):
        x, y, c, _ = _place()
        for k, j in enumerate(self.patterns):
            px, py, _ = _flip(x, y, c, j)
            _remote(ro[0].at[2 * px + py], rw[0].at[j // 2 - 1], send(k), recv(k), (px, py, c)).start()

    def finish(self, ro, rw, send, recv):
        x, y, c, _ = _place()
        for k, j in enumerate(self.patterns):
            slot = rw[0].at[j // 2 - 1]
            cp = _remote(slot, slot, send(k), recv(k), (x, y, c))
            cp.wait_recv()
            cp.wait_send()


class _Join:
    def __init__(self, bufs):
        self.ro, self.rw, self.n_sems = [], list(bufs), sum(b.shape[0] for b in bufs)

    def _copies(self, rw, send, recv, slot):
        x, y, c, _ = _place()
        k = 0
        for ref in rw:
            for l in range(ref.shape[0]):
                yield _remote(ref.at[l, c], ref.at[l, slot(c)], send(k), recv(k), (x, y, 1 - c))
                k += 1

    def start(self, ro, rw, send, recv):
        for cp in self._copies(rw, send, recv, lambda c: c):
            cp.start()

    def finish(self, ro, rw, send, recv):
        for cp in self._copies(rw, send, recv, lambda c: 1 - c):
            cp.wait_recv()
        for cp in self._copies(rw, send, recv, lambda c: c):
            cp.wait_send()


def _comm_layout(comms):
    ro = [a for c in comms for a in c.ro]
    rw = [a for c in comms for a in c.rw]
    return ro, rw, sum(c.n_sems for c in comms)


def _comm_each(comms, method, ro_refs, rw_refs, send, recv):
    i_ro = i_rw = i_sem = 0
    for c in comms:
        getattr(c, method)(ro_refs[i_ro:i_ro + len(c.ro)], rw_refs[i_rw:i_rw + len(c.rw)],
                           lambda k, b=i_sem: send.at[b + k], lambda k, b=i_sem: recv.at[b + k])
        i_ro, i_rw, i_sem = i_ro + len(c.ro), i_rw + len(c.rw), i_sem + c.n_sems


def _split_results(comms, rws):
    out, i = [], 0
    for c in comms:
        out.append(list(rws[i:i + len(c.rw)]))
        i += len(c.rw)
    return out


def _comm_call(comms, name):
    ro, rw, n_sems = _comm_layout(comms)

    def body(*refs):
        ro_refs = refs[:len(ro)]
        rw_refs = refs[len(ro) + len(rw):len(ro) + 2 * len(rw)]
        send, recv = refs[len(ro) + 2 * len(rw):]
        _comm_each(comms, "start", ro_refs, rw_refs, send, recv)
        _comm_each(comms, "finish", ro_refs, rw_refs, send, recv)

    rws = pl.pallas_call(
        body, name=name, in_specs=[ANY] * (len(ro) + len(rw)), out_specs=[ANY] * len(rw),
        out_shape=[jax.ShapeDtypeStruct(a.shape, a.dtype) for a in rw],
        input_output_aliases={len(ro) + k: k for k in range(len(rw))},
        scratch_shapes=[pltpu.SemaphoreType.DMA((n_sems,)), pltpu.SemaphoreType.DMA((n_sems,))],
    )(*ro, *rw)
    return _split_results(comms, rws)


def _pcall(body, args, *, name, grid, in_specs, out_specs, out_shape, scratch=(), sem=(), comms=(), aliases=None):
    n_in, n_out, n_scr = len(in_specs), len(out_specs), len(scratch)
    aliases = dict(aliases or {})
    if not comms:
        return pl.pallas_call(body, name=name, grid=grid, in_specs=list(in_specs), out_specs=list(out_specs),
                              out_shape=list(out_shape), scratch_shapes=list(scratch), input_output_aliases=aliases,
                              compiler_params=_params(*sem))(*args)
    ro, rw, n_sems = _comm_layout(comms)

    def carrier(*refs):
        ins = refs[:n_in]
        ro_refs = refs[n_in:n_in + len(ro)]
        o0 = n_in + len(ro) + len(rw)
        outs = refs[o0:o0 + n_out]
        rw_refs = refs[o0 + n_out:o0 + n_out + len(rw)]
        s0 = o0 + n_out + len(rw)
        send, recv = refs[s0 + n_scr], refs[s0 + n_scr + 1]
        ids = [pl.program_id(a) for a in range(len(grid))]
        first = functools.reduce(jnp.logical_and, [i == 0 for i in ids])
        last = functools.reduce(jnp.logical_and, [i == g - 1 for i, g in zip(ids, grid)])

        @pl.when(first)
        def _():
            _comm_each(comms, "start", ro_refs, rw_refs, send, recv)

        body(*ins, *outs, *refs[s0:s0 + n_scr])

        @pl.when(last)
        def _():
            _comm_each(comms, "finish", ro_refs, rw_refs, send, recv)

    res = pl.pallas_call(
        carrier, name=name, grid=grid, in_specs=list(in_specs) + [ANY] * (len(ro) + len(rw)),
        out_specs=list(out_specs) + [ANY] * len(rw),
        out_shape=list(out_shape) + [jax.ShapeDtypeStruct(a.shape, a.dtype) for a in rw],
        input_output_aliases={**aliases, **{n_in + len(ro) + k: n_out + k for k in range(len(rw))}},
        scratch_shapes=list(scratch) + [pltpu.SemaphoreType.DMA((n_sems,)), pltpu.SemaphoreType.DMA((n_sems,))],
        compiler_params=_params(*["arbitrary"] * len(grid)),
    )(*args, *ro, *rw)
    return list(res[:n_out]), _split_results(comms, res[n_out:])


def _matmul(a, b, mode, out_dtype, name, a_square=False, relu_out=False, mul2=None, comms=()):
    if mode == "nn":
        (m, k), n = a.shape, b.shape[1]
    elif mode == "nt":
        (m, k), n = a.shape, b.shape[0]
    else:
        (k, m), n = a.shape, b.shape[1]
    tm, tn, tk = _tile(m, 1024), _tile(n, 1024), _tile(k, 2048)
    nk = k // tk
    dims = {"nn": NN, "nt": NT, "tn": TN}[mode]
    a_spec = pl.BlockSpec((tk, tm), lambda i, j, kk: (kk, i)) if mode == "tn" else pl.BlockSpec((tm, tk), lambda i, j, kk: (i, kk))
    b_spec = pl.BlockSpec((tn, tk), lambda i, j, kk: (j, kk)) if mode == "nt" else pl.BlockSpec((tk, tn), lambda i, j, kk: (kk, j))
    o_spec = pl.BlockSpec((tm, tn), lambda i, j, kk: (i, j))

    def body(a_ref, b_ref, *rest):
        m_ref = None if mul2 is None else rest[0]
        o_ref = rest[0 if mul2 is None else 1]
        kk = pl.program_id(2)

        def partial():
            av = a_ref[...]
            if a_square:
                av = av * av
            return _dot(av, b_ref[...], dims)

        def finish(r):
            if relu_out:
                r = jnp.maximum(r, 0.0)
            if mul2 is not None:
                r = r * (2.0 * m_ref[...].astype(F32))
            o_ref[...] = r.astype(out_dtype)

        if nk == 1:
            finish(partial())
            return
        acc_ref = rest[-1]

        @pl.when(kk == 0)
        def _():
            acc_ref[...] = partial()

        @pl.when(kk > 0)
        def _():
            acc_ref[...] += partial()

        @pl.when(kk == nk - 1)
        def _():
            finish(acc_ref[...])

    args = (a, b) if mul2 is None else (a, b, mul2)
    specs = [a_spec, b_spec] + ([] if mul2 is None else [o_spec])
    res = _pcall(body, args, name=name, grid=(m // tm, n // tn, nk), in_specs=specs, out_specs=[o_spec],
                 out_shape=[jax.ShapeDtypeStruct((m, n), out_dtype)], scratch=[pltpu.VMEM((tm, tn), F32)] if nk > 1 else [],
                 sem=("parallel", "parallel", "arbitrary"), comms=comms)
    return (res[0][0], res[1]) if comms else res[0]


NORM_ROWS = 256


def _rms(x, g):
    rstd = lax.rsqrt(jnp.mean(x * x, axis=-1, keepdims=True) + RMS_EPS)
    n = x * rstd
    return n * g, n, rstd


def _rms_bwd(n, rstd, g, dout):
    dn = dout * g
    return rstd * (dn - n * jnp.mean(dn * n, axis=-1, keepdims=True))


def _row_spec(d):
    return pl.BlockSpec((NORM_ROWS, d), lambda i: (i, 0))


def _vec_spec(d):
    return pl.BlockSpec((1, d), lambda i: (0, 0))


def _accumulate(ref, val):
    @pl.when(pl.program_id(0) == 0)
    def _():
        ref[...] = jnp.zeros_like(ref)

    ref[...] += val


def _rms_fwd(x, g, name):
    t, d = x.shape

    def body(x_ref, g_ref, h_ref):
        h_ref[...] = _rms(x_ref[...], g_ref[...])[0].astype(BF16)

    return pl.pallas_call(
        body, name=name, grid=(t // NORM_ROWS,), in_specs=[_row_spec(d), _vec_spec(d)], out_specs=_row_spec(d),
        out_shape=jax.ShapeDtypeStruct((t, d), BF16), compiler_params=_params("parallel"),
    )(x, g)


def _post_pre_fwd(y, g_post, x, g_pre, name, comms=()):
    t, d = x.shape

    def body(y_ref, gp_ref, x_ref, gn_ref, xn_ref, h_ref):
        xn = x_ref[...] + _rms(y_ref[...], gp_ref[...])[0]
        xn_ref[...] = xn
        h_ref[...] = _rms(xn, gn_ref[...])[0].astype(BF16)

    return _pcall(
        body, (y, g_post, x, g_pre), name=name, grid=(t // NORM_ROWS,),
        in_specs=[_row_spec(d), _vec_spec(d), _row_spec(d), _vec_spec(d)], out_specs=[_row_spec(d), _row_spec(d)],
        out_shape=[jax.ShapeDtypeStruct((t, d), F32), jax.ShapeDtypeStruct((t, d), BF16)], sem=("parallel",), comms=comms)


def _final_fwd_bwd(y, g_post, x, target, name):
    t, d = x.shape

    def body(y_ref, g_ref, x_ref, t_ref, loss_ref, dx_ref, dy_ref, dg_ref):
        g = g_ref[...]
        out, n, rstd = _rms(y_ref[...], g)
        e = x_ref[...] + out - t_ref[...]
        _accumulate(loss_ref, jnp.full(loss_ref.shape, 0.5 / d, F32) * jnp.sum(e * e))
        dx = e * (1.0 / d)
        dx_ref[...] = dx
        dy_ref[...] = _rms_bwd(n, rstd, g, dx).astype(BF16)
        _accumulate(dg_ref, jnp.sum(dx * n, axis=0, keepdims=True))

    return pl.pallas_call(
        body, name=name, grid=(t // NORM_ROWS,),
        in_specs=[_row_spec(d), _vec_spec(d), _row_spec(d), _row_spec(d)],
        out_specs=[pl.BlockSpec((8, 128), lambda i: (0, 0)), _row_spec(d), _row_spec(d), _vec_spec(d)],
        out_shape=[jax.ShapeDtypeStruct((8, 128), F32), jax.ShapeDtypeStruct((t, d), F32),
                   jax.ShapeDtypeStruct((t, d), BF16), jax.ShapeDtypeStruct((1, d), F32)],
        compiler_params=_params("arbitrary"),
    )(y, g_post, x, target)


def _pre_post_bwd(x, g_pre, dh, dx_in, y, g_post, name, comms=()):
    t, d = x.shape
    both = y is not None

    def body(x_ref, gp_ref, dh_ref, dxi_ref, *rest):
        if both:
            y_ref, gq_ref, dx_ref, dy_ref, dgp_ref, dgq_ref = rest
        else:
            dx_ref, dgp_ref = rest
        gp = gp_ref[...]
        _, n, rstd = _rms(x_ref[...], gp)
        dh_v = dh_ref[...]
        dx = dxi_ref[...] + _rms_bwd(n, rstd, gp, dh_v)
        dx_ref[...] = dx
        _accumulate(dgp_ref, jnp.sum(dh_v * n, axis=0, keepdims=True))
        if both:
            gq = gq_ref[...]
            _, ny, rstdy = _rms(y_ref[...], gq)
            dy_ref[...] = _rms_bwd(ny, rstdy, gq, dx).astype(BF16)
            _accumulate(dgq_ref, jnp.sum(dx * ny, axis=0, keepdims=True))

    in_specs = [_row_spec(d), _vec_spec(d), _row_spec(d), _row_spec(d)]
    args = [x, g_pre, dh, dx_in]
    if both:
        in_specs += [_row_spec(d), _vec_spec(d)]
        args += [y, g_post]
        out_specs = [_row_spec(d), _row_spec(d), _vec_spec(d), _vec_spec(d)]
        out_shape = [jax.ShapeDtypeStruct((t, d), F32), jax.ShapeDtypeStruct((t, d), BF16),
                     jax.ShapeDtypeStruct((1, d), F32), jax.ShapeDtypeStruct((1, d), F32)]
    else:
        out_specs = [_row_spec(d), _vec_spec(d)]
        out_shape = [jax.ShapeDtypeStruct((t, d), F32), jax.ShapeDtypeStruct((1, d), F32)]
    return _pcall(body, args, name=name, grid=(t // NORM_ROWS,), in_specs=in_specs, out_specs=out_specs, out_shape=out_shape,
                  sem=("arbitrary",), comms=comms)


def _gelu(x):
    return 0.5 * x * (1.0 + lax.erf(x * 0.7071067811865476))


def _gelu_grad(x):
    return 0.5 * (1.0 + lax.erf(x * 0.7071067811865476)) + x * jnp.exp(-0.5 * x * x) * 0.3989422804014327


def _layernorm(v, g, b):
    mu = jnp.mean(v, axis=-1, keepdims=True)
    vc = v - mu
    rs = lax.rsqrt(jnp.mean(vc * vc, axis=-1, keepdims=True) + LN_EPS)
    vhat = vc * rs
    return vhat * g + b, vhat, rs


def _tril_mask():
    return lax.broadcasted_iota(jnp.int32, (CHUNK, CHUNK), 0) >= lax.broadcasted_iota(jnp.int32, (CHUNK, CHUNK), 1)


def _sgu_fwd(z, ln_g, ln_b, w16, bias_b, name, comms=()):
    t = z.shape[0]
    groups = w16.shape[0]
    a = groups * CHUNK

    def body(u_ref, v_ref, g_ref, b_ref, w_ref, bb_ref, o_ref):
        u = _gelu(u_ref[...].astype(F32))
        vn = _layernorm(_gelu(v_ref[...].astype(F32)), g_ref[...], b_ref[...])[0].astype(BF16)
        tril = _tril_mask()
        for g in range(groups):
            sl = slice(g * CHUNK, (g + 1) * CHUNK)
            w = jnp.where(tril, w_ref[g], jnp.zeros((), BF16))
            mixed = _dot(w, vn[:, sl], NN) + bb_ref[g]
            o_ref[:, sl] = (u[:, sl] * mixed).astype(BF16)

    full3 = pl.BlockSpec((groups, CHUNK, CHUNK), lambda c: (0, 0, 0))
    res = _pcall(
        body, (z, z, ln_g, ln_b, w16, bias_b), name=name, grid=(t // CHUNK,),
        in_specs=[pl.BlockSpec((CHUNK, a), lambda c: (c, 0)), pl.BlockSpec((CHUNK, a), lambda c: (c, 1)),
                  _vec_spec(a), _vec_spec(a), full3, full3],
        out_specs=[pl.BlockSpec((CHUNK, a), lambda c: (c, 0))], out_shape=[jax.ShapeDtypeStruct((t, a), BF16)],
        sem=("parallel",), comms=comms)
    return (res[0][0], res[1]) if comms else res[0]


def _sgu_bwd(z, dab, ln_g, ln_b, w16, bias_b, name, comms=()):
    t = z.shape[0]
    groups = w16.shape[0]
    a = groups * CHUNK

    def body(u_ref, v_ref, da_ref, g_ref, b_ref, w_ref, bb_ref, duv_ref, dg_ref, db_ref, dw_ref, dbs_ref, dvn_ref):
        up = u_ref[...].astype(F32)
        vp = v_ref[...].astype(F32)
        u = _gelu(up)
        ln_gain = g_ref[...]
        vn32, vhat, rs = _layernorm(_gelu(vp), ln_gain, b_ref[...])
        vn = vn32.astype(BF16)
        da = da_ref[...].astype(F32)
        tril = _tril_mask()
        ones = jnp.ones((8, CHUNK), F32)

        @pl.when(pl.program_id(0) == 0)
        def _():
            dw_ref[...] = jnp.zeros_like(dw_ref)
            dbs_ref[...] = jnp.zeros_like(dbs_ref)

        for g in range(groups):
            sl = slice(g * CHUNK, (g + 1) * CHUNK)
            w = jnp.where(tril, w_ref[g], jnp.zeros((), BF16))
            mixed = _dot(w, vn[:, sl], NN) + bb_ref[g]
            dmix = da[:, sl] * u[:, sl]
            dmix16 = dmix.astype(BF16)
            duv_ref[:, sl] = (da[:, sl] * mixed * _gelu_grad(up[:, sl])).astype(BF16)
            dvn_ref[:, sl] = _dot(w, dmix16, TN)
            dw_ref[g] += jnp.where(tril, _dot(dmix16, vn[:, sl], NT), 0.0)
            dbs_ref[g:g + 1, :] += lax.dot_general(ones, dmix, (NT, ((), ())), precision=lax.Precision.HIGHEST,
                                                   preferred_element_type=F32)[0:1]
        dvn = dvn_ref[...]
        dvhat = dvn * ln_gain
        dva = rs * (dvhat - jnp.mean(dvhat, axis=-1, keepdims=True) - vhat * jnp.mean(dvhat * vhat, axis=-1, keepdims=True))
        duv_ref[:, a:] = (dva * _gelu_grad(vp)).astype(BF16)
        _accumulate(dg_ref, jnp.sum(dvn * vhat, axis=0, keepdims=True))
        _accumulate(db_ref, jnp.sum(dvn, axis=0, keepdims=True))

    full3 = pl.BlockSpec((groups, CHUNK, CHUNK), lambda c: (0, 0, 0))
    return _pcall(
        body, (z, z, dab, ln_g, ln_b, w16, bias_b), name=name, grid=(t // CHUNK,),
        in_specs=[pl.BlockSpec((CHUNK, a), lambda c: (c, 0)), pl.BlockSpec((CHUNK, a), lambda c: (c, 1)),
                  pl.BlockSpec((CHUNK, a), lambda c: (c, 0)), _vec_spec(a), _vec_spec(a), full3, full3],
        out_specs=[pl.BlockSpec((CHUNK, 2 * a), lambda c: (c, 0)), _vec_spec(a), _vec_spec(a), full3,
                   pl.BlockSpec((groups, CHUNK), lambda c: (0, 0))],
        out_shape=[jax.ShapeDtypeStruct((t, 2 * a), BF16), jax.ShapeDtypeStruct((1, a), F32), jax.ShapeDtypeStruct((1, a), F32),
                   jax.ShapeDtypeStruct((groups, CHUNK, CHUNK), F32), jax.ShapeDtypeStruct((groups, CHUNK), F32)],
        scratch=[pltpu.VMEM((CHUNK, a), F32)], sem=("arbitrary",), comms=comms)


def _dil_masks(d):
    qi = lax.broadcasted_iota(jnp.int32, (CHUNK, CHUNK), 0)
    kj = lax.broadcasted_iota(jnp.int32, (CHUNK, CHUNK), 1)
    dist_c = qi - kj
    return dist_c >= 0, dist_c <= 0, (dist_c * d).astype(F32), ((dist_c + CHUNK) * d).astype(F32)


def _alibi_slope(h, heads):
    return 2.0 ** (-8.0 * (h + 1) / heads)


def _dil_view(z, d):
    t, w = z.shape[0], z.shape[1] // 5
    if d == 1:
        return z, 5, 2
    return z[:, 2 * w:].reshape(t // d, d * 3 * w), 3, 0


def _dil_fwd(z, d, name, comms=()):
    t = z.shape[0]
    w = z.shape[1] // 5
    heads = w // HEAD_DIM
    nb = t // d // CHUNK
    scale = HEAD_DIM ** -0.5
    zv, mult, col_q = _dil_view(z, d)

    def body(q_ref, kp_ref, kc_ref, vp_ref, vc_ref, o_ref, l_ref):
        ok_c, ok_p0, bias_c, bias_p = _dil_masks(d)
        ok_p = ok_p0 & (pl.program_id(1) > 0)
        hs = range(heads)
        sl = [slice(h * HEAD_DIM, (h + 1) * HEAD_DIM) for h in hs]
        slope = [_alibi_slope(h, heads) for h in hs]
        ones = jnp.ones((CHUNK, HEAD_DIM), BF16)
        s_c = [_dot(q_ref[:, sl[h]], kc_ref[:, sl[h]], NT) for h in hs]
        s_p = [_dot(q_ref[:, sl[h]], kp_ref[:, sl[h]], NT) for h in hs]
        s_c = [jnp.where(ok_c, s_c[h] * scale - slope[h] * bias_c, NEG) for h in hs]
        s_p = [jnp.where(ok_p, s_p[h] * scale - slope[h] * bias_p, NEG) for h in hs]
        m = [jnp.max(jnp.maximum(s_c[h], s_p[h]), axis=1, keepdims=True) for h in hs]
        p_c = [jnp.exp(s_c[h] - m[h]).astype(BF16) for h in hs]
        p_p = [jnp.exp(s_p[h] - m[h]).astype(BF16) for h in hs]
        den = [_dot(p_c[h], ones, NN) + _dot(p_p[h], ones, NN) for h in hs]
        o = [_dot(p_c[h], vc_ref[:, sl[h]], NN) + _dot(p_p[h], vp_ref[:, sl[h]], NN) for h in hs]
        l_ref[...] = jnp.zeros_like(l_ref)
        for h in hs:
            o_ref[:, sl[h]] = (o[h] / den[h]).astype(BF16)
            l_ref[:, h:h + 1] = m[h] + jnp.log(den[h][:, 0:1])

    def zspec(col, prev):
        if prev:
            return pl.BlockSpec((CHUNK, w), lambda r, n: (jnp.maximum(n - 1, 0), r * mult + col_q + col))
        return pl.BlockSpec((CHUNK, w), lambda r, n: (n, r * mult + col_q + col))

    res = _pcall(
        body, (zv, zv, zv, zv, zv), name=name, grid=(d, nb),
        in_specs=[zspec(0, False), zspec(1, True), zspec(1, False), zspec(2, True), zspec(2, False)],
        out_specs=[pl.BlockSpec((CHUNK, w), lambda r, n: (n, r)), pl.BlockSpec((CHUNK, HEAD_DIM), lambda r, n: (n, r))],
        out_shape=[jax.ShapeDtypeStruct((t // d, d * w), BF16), jax.ShapeDtypeStruct((t // d, d * HEAD_DIM), F32)],
        sem=("parallel", "parallel"), comms=comms)
    (o, lse), rws = res if comms else (res, None)
    outs = (o.reshape(t, w), lse.reshape(t, HEAD_DIM))
    return (outs, rws) if comms else outs


def _dil_merge(a_out, outs, lses, name, comms=()):
    t, a = a_out.shape
    w = outs[0].shape[1]
    heads = w // HEAD_DIM
    nbr = len(outs)

    def body(a_ref, *rest):
        o_refs, l_refs, (ab_ref, lt_ref) = rest[:nbr], rest[nbr:2 * nbr], rest[2 * nbr:]
        ls = [r[...] for r in l_refs]
        m = functools.reduce(jnp.maximum, ls)
        ws = [jnp.exp(l - m) for l in ls]
        tot = functools.reduce(jnp.add, ws)
        ws = [wt / tot for wt in ws]
        ab_ref[:, :a] = a_ref[...]
        for h in range(heads):
            sl = slice(h * HEAD_DIM, (h + 1) * HEAD_DIM)
            mix = functools.reduce(jnp.add, [wt[:, h:h + 1] * r[:, sl].astype(F32) for wt, r in zip(ws, o_refs)])
            ab_ref[:, a + h * HEAD_DIM:a + (h + 1) * HEAD_DIM] = mix.astype(BF16)
        lt_ref[...] = m + jnp.log(tot)

    return _pcall(
        body, (a_out, *outs, *lses), name=name, grid=(t // NORM_ROWS,),
        in_specs=[_row_spec(a)] + [_row_spec(w)] * nbr + [_row_spec(HEAD_DIM)] * nbr,
        out_specs=[_row_spec(a + w), _row_spec(HEAD_DIM)],
        out_shape=[jax.ShapeDtypeStruct((t, a + w), BF16), jax.ShapeDtypeStruct((t, HEAD_DIM), F32)],
        sem=("parallel",), comms=comms)


def _dil_delta(ab, dab, name):
    t, aw = ab.shape
    w = aw // 2
    heads = w // HEAD_DIM

    def body(o_ref, do_ref, dl_ref):
        dl_ref[...] = jnp.zeros_like(dl_ref)
        for h in range(heads):
            sl = slice(h * HEAD_DIM, (h + 1) * HEAD_DIM)
            dl_ref[:, h:h + 1] = jnp.sum(do_ref[:, sl].astype(F32) * o_ref[:, sl].astype(F32), axis=1, keepdims=True)

    half = pl.BlockSpec((NORM_ROWS, w), lambda i: (i, 1))
    return pl.pallas_call(body, name=name, grid=(t // NORM_ROWS,), in_specs=[half, half], out_specs=_row_spec(HEAD_DIM),
                          out_shape=jax.ShapeDtypeStruct((t, HEAD_DIM), F32), compiler_params=_params("parallel"))(ab, dab)


def _dil_bwd(z, dab, ltot, delta, d, name, comms=()):
    t = z.shape[0]
    w = z.shape[1] // 5
    heads = w // HEAD_DIM
    nb = t // d // CHUNK
    scale = HEAD_DIM ** -0.5

    def body(q_ref, qn_ref, kp_ref, kc_ref, vp_ref, vc_ref, do_ref, don_ref, l_ref, ln_ref, dl_ref, dln_ref,
             dq_ref, dk_ref, dv_ref):
        n = pl.program_id(1)
        ok_c, ok_p0, bias_c, bias_p = _dil_masks(d)
        ok_p = ok_p0 & (n > 0)
        ok_n = ok_p0 & (n < nb - 1)
        hs = range(heads)
        sl = [slice(h * HEAD_DIM, (h + 1) * HEAD_DIM) for h in hs]
        slope = [_alibi_slope(h, heads) for h in hs]
        q, qn = [q_ref[:, s] for s in sl], [qn_ref[:, s] for s in sl]
        kp, kc = [kp_ref[:, s] for s in sl], [kc_ref[:, s] for s in sl]
        vp, vc = [vp_ref[:, s] for s in sl], [vc_ref[:, s] for s in sl]
        do, don = [do_ref[:, s] for s in sl], [don_ref[:, s] for s in sl]
        s_c = [_dot(q[h], kc[h], NT) for h in hs]
        s_p = [_dot(q[h], kp[h], NT) for h in hs]
        s_n = [_dot(qn[h], kc[h], NT) for h in hs]
        dp_c = [_dot(do[h], vc[h], NT) for h in hs]
        dp_p = [_dot(do[h], vp[h], NT) for h in hs]
        dp_n = [_dot(don[h], vc[h], NT) for h in hs]
        delta = [dl_ref[:, h:h + 1] for h in hs]
        delta_n = [dln_ref[:, h:h + 1] for h in hs]
        p_c = [jnp.exp(jnp.where(ok_c, s_c[h] * scale - slope[h] * bias_c, NEG) - l_ref[:, h:h + 1]) for h in hs]
        p_p = [jnp.exp(jnp.where(ok_p, s_p[h] * scale - slope[h] * bias_p, NEG) - l_ref[:, h:h + 1]) for h in hs]
        p_n = [jnp.exp(jnp.where(ok_n, s_n[h] * scale - slope[h] * bias_p, NEG) - ln_ref[:, h:h + 1]) for h in hs]
        ds_c = [(p_c[h] * (dp_c[h] - delta[h])).astype(BF16) for h in hs]
        ds_p = [(p_p[h] * (dp_p[h] - delta[h])).astype(BF16) for h in hs]
        ds_n = [(p_n[h] * (dp_n[h] - delta_n[h])).astype(BF16) for h in hs]
        dq = [_dot(ds_c[h], kc[h], NN) + _dot(ds_p[h], kp[h], NN) for h in hs]
        dk = [_dot(ds_c[h], q[h], TN) + _dot(ds_n[h], qn[h], TN) for h in hs]
        dv = [_dot(p_c[h].astype(BF16), do[h], TN) + _dot(p_n[h].astype(BF16), don[h], TN) for h in hs]
        for h in hs:
            dq_ref[:, sl[h]] = (dq[h] * scale).astype(BF16)
            dk_ref[:, sl[h]] = (dk[h] * scale).astype(BF16)
            dv_ref[:, sl[h]] = dv[h].astype(BF16)

    def spec(mult, col, shift, width=w):
        if shift < 0:
            return pl.BlockSpec((CHUNK, width), lambda r, n: (jnp.maximum(n - 1, 0), r * mult + col))
        if shift > 0:
            return pl.BlockSpec((CHUNK, width), lambda r, n: (jnp.minimum(n + 1, nb - 1), r * mult + col))
        return pl.BlockSpec((CHUNK, width), lambda r, n: (n, r * mult + col))

    zv, mult, cq = _dil_view(z, d)
    dov = dab[:, w:].reshape(t // d, d * w)
    lv = ltot.reshape(t // d, d * HEAD_DIM)
    dlv = delta.reshape(t // d, d * HEAD_DIM)
    ospec = spec(1, 0, 0)
    res = _pcall(
        body, (zv, zv, zv, zv, zv, zv, dov, dov, lv, lv, dlv, dlv), name=name, grid=(d, nb),
        in_specs=[spec(mult, cq, 0), spec(mult, cq, 1), spec(mult, cq + 1, -1), spec(mult, cq + 1, 0),
                  spec(mult, cq + 2, -1), spec(mult, cq + 2, 0), spec(1, 0, 0), spec(1, 0, 1),
                  spec(1, 0, 0, HEAD_DIM), spec(1, 0, 1, HEAD_DIM), spec(1, 0, 0, HEAD_DIM), spec(1, 0, 1, HEAD_DIM)],
        out_specs=[ospec, ospec, ospec], out_shape=[jax.ShapeDtypeStruct((t // d, d * w), BF16)] * 3,
        sem=("parallel", "parallel"), comms=comms)
    outs, rws = res if comms else (res, None)
    outs = [o.reshape(t, w) for o in outs]
    return (outs, rws) if comms else outs


def _dz_assemble(duv, parts, name):
    t, a2 = duv.shape
    w = parts[0][0].shape[1]
    nbr = len(parts)

    def body(duv_ref, *rest):
        refs, dz_ref = rest[:-1], rest[-1]
        dz_ref[:, :a2] = duv_ref[...]
        for i in range(3):
            tot = functools.reduce(jnp.add, [refs[b * 3 + i][...].astype(F32) for b in range(nbr)])
            dz_ref[:, a2 + i * w:a2 + (i + 1) * w] = tot.astype(BF16)

    flat = [p for branch in parts for p in branch]
    return pl.pallas_call(
        body, name=name, grid=(t // NORM_ROWS,), in_specs=[_row_spec(a2)] + [_row_spec(w)] * len(flat),
        out_specs=_row_spec(a2 + 3 * w), out_shape=jax.ShapeDtypeStruct((t, a2 + 3 * w), BF16),
        compiler_params=_params("parallel"),
    )(duv, *flat)


def _split_dot(x, m16):
    hi = x.astype(BF16)
    lo = (x - hi.astype(F32)).astype(BF16)
    return _dot(hi, m16, NN) + _dot(lo, m16, NN)


SB_DEAD = -110.0


def _sb_scaled(q):
    return (q.astype(F32) * (HEAD_DIM ** -0.5)).astype(BF16)


def _sb_log(qs, kj, below):
    zt = _dot(qs, kj, NT)
    sp = jnp.maximum(zt, 0.0) + jnp.log(1.0 + jnp.exp(-jnp.abs(zt)))
    return zt - sp, (-sp if below is None else jnp.where(below, -sp, 0.0))


def _sb_alive(s, i, c_run):
    return (s <= i) & (jnp.max(c_run) > SB_DEAD)


def _sb_fwd(zc, name, comms=()):
    t = zc.shape[0]
    c = zc.shape[1] // 3
    heads = c // HEAD_DIM
    blk = min(SB_BLOCK, t)

    def body(q_ref, k_ref, v_ref, o_ref, ct_ref, nb_ref):
        i = pl.program_id(1)
        qs = _sb_scaled(q_ref[...])
        rows = lax.broadcasted_iota(jnp.int32, (blk, blk), 0)
        cols = lax.broadcasted_iota(jnp.int32, (blk, blk), 1)
        below = rows > cols
        m_right = below.astype(BF16)

        def tile(carry, diagonal):
            s, acc, c_run = carry
            off = pl.multiple_of((i - s) * blk, blk)
            log_beta, l = _sb_log(qs, k_ref[pl.ds(off, blk), :], below if diagonal else None)
            a = jnp.exp(log_beta + (c_run + _split_dot(l, m_right)))
            if diagonal:
                a = jnp.where(below, a, 0.0)
            acc = acc + _dot(a.astype(BF16), v_ref[pl.ds(off, blk), :], NN)
            return s + 1, acc, c_run + jnp.sum(l, axis=1, keepdims=True)

        first = tile((jnp.int32(0), jnp.zeros((blk, HEAD_DIM), F32), jnp.zeros((blk, 1), F32)), True)
        swept, acc, c_tot = lax.while_loop(lambda carry: _sb_alive(carry[0], i, carry[2]), lambda carry: tile(carry, False), first)
        o_ref[...] = acc.astype(BF16)
        ct_ref[...] = jnp.broadcast_to(c_tot, (blk, HEAD_DIM))
        nb_ref[...] = jnp.zeros((blk, HEAD_DIM), F32) + swept.astype(F32)

    qspec = pl.BlockSpec((blk, HEAD_DIM), lambda h, i: (i, h))
    return _pcall(body, (zc, zc, zc), name=name, grid=(heads, t // blk),
                  in_specs=[qspec, pl.BlockSpec((t, HEAD_DIM), lambda h, i: (0, heads + h)),
                            pl.BlockSpec((t, HEAD_DIM), lambda h, i: (0, 2 * heads + h))],
                  out_specs=[qspec, qspec, qspec],
                  out_shape=[jax.ShapeDtypeStruct((t, c), BF16), jax.ShapeDtypeStruct((t, c), F32), jax.ShapeDtypeStruct((t, c), F32)],
                  sem=("parallel", "parallel"), comms=comms)


def _sb_bwd(zc, ctot, swept, do, name, comms=()):
    t = zc.shape[0]
    c = zc.shape[1] // 3
    heads = c // HEAD_DIM
    blk = min(SB_BLOCK, t)
    scale = HEAD_DIM ** -0.5

    def body(q_ref, k_ref, v_ref, ct_ref, nb_ref, do_ref, dq_ref, dk_ref, dv_ref):
        i = pl.program_id(1)

        @pl.when(i == 0)
        def _():
            dk_ref[...] = jnp.zeros_like(dk_ref)
            dv_ref[...] = jnp.zeros_like(dv_ref)

        qs = _sb_scaled(q_ref[...])
        dov = do_ref[...]
        c_tot = ct_ref[:, 0:1]
        n_blocks = jnp.clip(jnp.max(nb_ref[0:8, :]).astype(jnp.int32), 1, i + 1)
        rows = lax.broadcasted_iota(jnp.int32, (blk, blk), 0)
        cols = lax.broadcasted_iota(jnp.int32, (blk, blk), 1)
        below = rows > cols
        m_upto = (rows <= cols).astype(BF16)
        m_left = (rows < cols).astype(BF16)

        def tile(j, carry, diagonal):
            dq, l_run, w_run = carry
            off = pl.multiple_of(j * blk, blk)
            kj = k_ref[pl.ds(off, blk), :]
            vj = v_ref[pl.ds(off, blk), :]
            log_beta, l = _sb_log(qs, kj, below if diagonal else None)
            a = jnp.exp(log_beta + (c_tot - l_run - _split_dot(l, m_upto)))
            if diagonal:
                a = jnp.where(below, a, 0.0)
            wgt = a * _dot(dov, vj, NT)
            before = w_run + _split_dot(wgt, m_left)
            dz = wgt * jnp.exp(l) - jnp.exp(log_beta) * before
            if diagonal:
                dz = jnp.where(below, dz, 0.0)
            dz16 = dz.astype(BF16)
            dk_ref[pl.ds(off, blk), :] += _dot(dz16, qs, TN)
            dv_ref[pl.ds(off, blk), :] += _dot(a.astype(BF16), dov, TN)
            return (dq + _dot(dz16, kj, NN), l_run + jnp.sum(l, axis=1, keepdims=True),
                    w_run + jnp.sum(wgt, axis=1, keepdims=True))

        zero = jnp.zeros((blk, 1), F32)
        carry = lax.fori_loop(i + 1 - n_blocks, i, lambda j, carry: tile(j, carry, False),
                              (jnp.zeros((blk, HEAD_DIM), F32), zero, zero))
        dq_ref[...] = tile(i, carry, True)[0] * scale

    qspec = pl.BlockSpec((blk, HEAD_DIM), lambda h, i: (i, h))
    full = pl.BlockSpec((t, HEAD_DIM), lambda h, i: (0, h))
    return _pcall(body, (zc, zc, zc, ctot, swept, do), name=name, grid=(heads, t // blk),
                  in_specs=[qspec, pl.BlockSpec((t, HEAD_DIM), lambda h, i: (0, heads + h)),
                            pl.BlockSpec((t, HEAD_DIM), lambda h, i: (0, 2 * heads + h)), qspec, qspec, qspec],
                  out_specs=[qspec, full, full], out_shape=[jax.ShapeDtypeStruct((t, c), F32)] * 3,
                  sem=("arbitrary", "arbitrary"), comms=comms)


def _concat_bf16(parts, name, comms=()):
    t, c = parts[0].shape

    def body(*refs):
        for k, r in enumerate(refs[:-1]):
            refs[-1][:, k * c:(k + 1) * c] = r[...].astype(BF16)

    res = _pcall(body, tuple(parts), name=name, grid=(t // NORM_ROWS,), in_specs=[_row_spec(c)] * len(parts),
                 out_specs=[_row_spec(c * len(parts))], out_shape=[jax.ShapeDtypeStruct((t, c * len(parts)), BF16)],
                 sem=("parallel",), comms=comms)
    return (res[0][0], res[1]) if comms else res[0]


KIND = {"ab_w_in": "col", "ab_w_out": "row", "sb_w_in": "col", "sb_w_out": "row",
        "ffn_w1_0": "col", "ffn_w1_1": "col", "ffn_w2_0": "row", "ffn_w2_1": "row"}
X_Y, DIAG, CHIPS = (2, 4), (6,), (2, 4, 6)


def _local_step(x, target, norms, sgu, big, bufs=None):
    g = {k: [v[l:l + 1] for l in range(2)] for k, v in norms.items()}
    ln_g, ln_b, sgu_w, sgu_b = sgu
    groups = sgu_w.shape[0]
    w16 = sgu_w.astype(BF16)
    bias_b = jnp.broadcast_to(sgu_b[:, :, None], (groups, CHUNK, CHUNK))
    big, dws, psum, dist = dict(big), {}, {}, bufs is not None
    pair, got = (dict(bufs[0]), dict(bufs[1])) if dist else ({}, {})

    def run(fn, *args, ops=(), **kw):
        if not dist or not ops:
            return fn(*args, **kw)
        make = {"gs": lambda k, p, *part: _GatherSend(big[k], KIND[k], p, *part), "gf": lambda k, p: _GatherFwd(big[k], KIND[k], p),
                "swap": lambda k, p: _PairSwap(dws[k], pair[k], KIND[k]), "chips": lambda k, p: _ChipScatter(psum[k], got[k], p)}
        out, rws = fn(*args, comms=[make[op[0]](*op[1:]) for op in ops], **kw)
        for (op, k, *_), r in zip(ops, rws):
            if op in ("gs", "gf"):
                big[k] = r[0]
            elif op == "swap":
                psum[k] = _pair_sum(dws[k], r[0], KIND[k], f"pair_sum_{k}")
            else:
                got[k] = r[0]
        return out

    h1_0 = _rms_fwd(x, g["pre_mix"][0], "rms_in")
    z0 = run(_matmul, h1_0, big["ab_w_in"], "nn", BF16, "ab_in", ops=[("gs", "ffn_w1_0", X_Y)])
    a_out = run(_sgu_fwd, z0, ln_g, ln_b, w16, bias_b, "sgu_fwd", ops=[("gf", "ffn_w1_0", X_Y), ("gs", "ab_w_out", CHIPS)])
    branches = [run(_dil_fwd, z0, 1, "dil_fwd_1", ops=[("gs", "ffn_w1_0", DIAG, (0, 2)), ("gf", "ab_w_out", CHIPS)]),
                run(_dil_fwd, z0, 4, "dil_fwd_4", ops=[("gs", "ffn_w1_0", DIAG, (1, 2))]),
                run(_dil_fwd, z0, 16, "dil_fwd_16", ops=[("gf", "ffn_w1_0", DIAG), ("gs", "ffn_w2_0", X_Y, (0, 2))])]
    ab, ltot = run(_dil_merge, a_out, [b[0] for b in branches], [b[1] for b in branches], "dil_merge",
                   ops=[("gs", "ffn_w2_0", X_Y, (1, 2))])
    y_0 = run(_matmul, ab, big["ab_w_out"], "nn", F32, "ab_out", ops=[("gs", "ffn_w2_0", DIAG, (0, 2))])
    x1, h2_0 = run(_post_pre_fwd, y_0, g["post_mix"][0], x, g["pre_ffn"][0], "norm_mix0", ops=[("gs", "ffn_w2_0", DIAG, (1, 2))])
    r_0 = run(_matmul, h2_0, big["ffn_w1_0"], "nn", BF16, "ffn_up_0", relu_out=True,
              ops=[("gf", "ffn_w2_0", CHIPS), ("gs", "sb_w_in", CHIPS)])
    y2_0 = run(_matmul, r_0, big["ffn_w2_0"], "nn", F32, "ffn_down_0", a_square=True,
               ops=[("gf", "sb_w_in", CHIPS), ("gs", "sb_w_out", CHIPS), ("gs", "ffn_w1_1", X_Y)])
    x2, h1_1 = run(_post_pre_fwd, y2_0, g["post_ffn"][0], x1, g["pre_mix"][1], "norm_ffn0",
                   ops=[("gf", "ffn_w1_1", X_Y), ("gf", "sb_w_out", CHIPS)])
    zc = run(_matmul, h1_1, big["sb_w_in"], "nn", BF16, "sb_in", ops=[("gs", "ffn_w1_1", DIAG)])
    o_sb, ct_sb, nb_sb = run(_sb_fwd, zc, "sb_fwd", ops=[("gf", "ffn_w1_1", DIAG), ("gs", "ffn_w2_1", CHIPS)])
    y_1 = run(_matmul, o_sb, big["sb_w_out"], "nn", F32, "sb_out", ops=[("gf", "ffn_w2_1", CHIPS)])
    x3, h2_1 = _post_pre_fwd(y_1, g["post_mix"][1], x2, g["pre_ffn"][1], "norm_mix1")
    r_1 = _matmul(h2_1, big["ffn_w1_1"], "nn", BF16, "ffn_up_1", relu_out=True)
    y2_1 = _matmul(r_1, big["ffn_w2_1"], "nn", F32, "ffn_down_1", a_square=True)
    loss, dx4, dy2_1, dg_post_ffn1 = _final_fwd_bwd(y2_1, g["post_ffn"][1], x3, target, "loss")

    da = _matmul(dy2_1, big["ffn_w2_1"], "nt", BF16, "ffn_da_1", mul2=r_1)
    dws["ffn_w2_1"] = _matmul(r_1, dy2_1, "tn", BF16, "ffn_dw2_1", a_square=True)
    dh2 = run(_matmul, da, big["ffn_w1_1"], "nt", F32, "ffn_dh_1", ops=[("swap", "ffn_w2_1", None)])
    dws["ffn_w1_1"] = run(_matmul, h2_1, da, "tn", BF16, "ffn_dw1_1", ops=[("chips", "ffn_w2_1", X_Y)])
    dx3, dy_1, dg_pre_ffn1, dg_post_mix1 = run(_pre_post_bwd, x3, g["pre_ffn"][1], dh2, dx4, y_1, g["post_mix"][1], "norm_bwd_mix1",
                                               ops=[("swap", "ffn_w1_1", None)])
    do_sb = _matmul(dy_1, big["sb_w_out"], "nt", BF16, "sb_out_dx")
    dws["sb_w_out"] = _matmul(o_sb, dy_1, "tn", BF16, "sb_out_dw")
    dqkv = run(_sb_bwd, zc, ct_sb, nb_sb, do_sb, "sb_bwd",
               ops=[("chips", "ffn_w2_1", DIAG), ("chips", "ffn_w1_1", CHIPS), ("swap", "sb_w_out", None)])
    dzc = run(_concat_bf16, dqkv, "sb_dz", ops=[("chips", "sb_w_out", X_Y)])
    dh1 = run(_matmul, dzc, big["sb_w_in"], "nt", F32, "sb_in_dx", ops=[("chips", "sb_w_out", DIAG)])
    dws["sb_w_in"] = _matmul(h1_1, dzc, "tn", BF16, "sb_in_dw")
    dx2, dy2_0, dg_pre_mix1, dg_post_ffn0 = run(_pre_post_bwd, x2, g["pre_mix"][1], dh1, dx3, y2_0, g["post_ffn"][0], "norm_bwd_ffn0",
                                                ops=[("swap", "sb_w_in", None)])
    da = run(_matmul, dy2_0, big["ffn_w2_0"], "nt", BF16, "ffn_da_0", mul2=r_0, ops=[("chips", "sb_w_in", X_Y)])
    dws["ffn_w2_0"] = run(_matmul, r_0, dy2_0, "tn", BF16, "ffn_dw2_0", a_square=True, ops=[("chips", "sb_w_in", DIAG)])
    dws["ffn_w1_0"] = run(_matmul, h2_0, da, "tn", BF16, "ffn_dw1_0", ops=[("swap", "ffn_w2_0", None)])
    dh2 = run(_matmul, da, big["ffn_w1_0"], "nt", F32, "ffn_dh_0", ops=[("chips", "ffn_w2_0", X_Y), ("swap", "ffn_w1_0", None)])
    dx1, dy_0, dg_pre_ffn0, dg_post_mix0 = _pre_post_bwd(x1, g["pre_ffn"][0], dh2, dx2, y_0, g["post_mix"][0], "norm_bwd_mix0")
    dab = _matmul(dy_0, big["ab_w_out"], "nt", BF16, "ab_out_dx")
    dws["ab_w_out"] = _matmul(ab, dy_0, "tn", BF16, "ab_out_dw")
    duv, d_ln_g, d_ln_b, d_sgu_w, d_sgu_b = run(_sgu_bwd, z0, dab, ln_g, ln_b, w16, bias_b, "sgu_bwd", ops=[("chips", "ffn_w2_0", DIAG)])
    delta = _dil_delta(ab, dab, "dil_delta")
    parts = [run(_dil_bwd, z0, dab, ltot, delta, 1, "dil_bwd_1", ops=[("chips", "ffn_w1_0", X_Y), ("swap", "ab_w_out", None)]),
             run(_dil_bwd, z0, dab, ltot, delta, 4, "dil_bwd_4", ops=[("chips", "ffn_w1_0", DIAG)]),
             run(_dil_bwd, z0, dab, ltot, delta, 16, "dil_bwd_16", ops=[("chips", "ab_w_out", CHIPS)])]
    dz0 = _dz_assemble(duv, parts, "dz_assemble")
    dws["ab_w_in"] = _matmul(h1_0, dz0, "tn", BF16, "ab_in_dw")
    dh1 = run(_matmul, dz0, big["ab_w_in"], "nt", F32, "ab_in_dx", ops=[("swap", "ab_w_in", None)])
    grad_x, dg_pre_mix0 = run(_pre_post_bwd, x, g["pre_mix"][0], dh1, dx1, None, None, "norm_bwd_in", ops=[("chips", "ab_w_in", X_Y)])

    d_norms = {
        "pre_mix": jnp.concatenate([dg_pre_mix0, dg_pre_mix1]), "post_mix": jnp.concatenate([dg_post_mix0, dg_post_mix1]),
        "pre_ffn": jnp.concatenate([dg_pre_ffn0, dg_pre_ffn1]), "post_ffn": jnp.concatenate([dg_post_ffn0, dg_post_ffn1]),
    }
    return loss, grad_x, d_norms, (d_ln_g, d_ln_b, d_sgu_w, d_sgu_b), (psum, got) if dist else dws


def _to_bf16_full(w, layer, kind, name):
    _, rows, cols = w.shape
    tr = _tile(rows, 512)
    nblk = rows // tr
    full = (rows, 4 * cols) if kind == "col" else (4 * rows, cols)

    def body(w_ref, o_ref):
        o_ref[...] = w_ref[...].astype(BF16)

    def place(i):
        mine = 2 * lax.axis_index("x") + lax.axis_index("y")
        return (i, mine) if kind == "col" else (mine * nblk + i, 0)

    return pl.pallas_call(
        body, name=name, grid=(nblk,), in_specs=[pl.BlockSpec((None, tr, cols), lambda i: (layer, i, 0))],
        out_specs=pl.BlockSpec((tr, cols), place), out_shape=jax.ShapeDtypeStruct(full, BF16), compiler_params=_params("parallel"),
    )(w)


def _pair_sum(dw16, pair, kind, name):
    rh, cs = _half_shape(dw16.shape, kind)
    tr = _tile(rh, 256)
    nblk = rh // tr

    def body(dw_ref, pair_ref, o_ref):
        o_ref[...] = (dw_ref[...].astype(F32) + pair_ref[...].astype(F32)).astype(BF16)

    def own(s, i):
        c = lax.axis_index("c")
        return (c * nblk + i, s) if kind == "col" else ((2 * s + c) * nblk + i, 0)

    spec3 = pl.BlockSpec((None, tr, cs), lambda s, i: (s, i, 0))
    return pl.pallas_call(
        body, name=name, grid=(4, nblk), in_specs=[pl.BlockSpec((tr, cs), own), spec3], out_specs=spec3,
        out_shape=jax.ShapeDtypeStruct((4, rh, cs), BF16), compiler_params=_params("parallel", "parallel"),
    )(dw16, pair)


def _owner_sum(psum, got, buf, layer, name, comms=()):
    _, rh, cs = psum.shape
    tr = _tile(rh, 256)

    def body(p_ref, got_ref, buf_ref, o_ref):
        tot = p_ref[...].astype(F32)
        for j in range(3):
            tot = tot + got_ref[j].astype(F32)
        o_ref[...] = tot

    res = _pcall(
        body, (psum, got, buf), name=name, grid=(rh // tr,),
        in_specs=[pl.BlockSpec((None, tr, cs), lambda i: (2 * lax.axis_index("x") + lax.axis_index("y"), i, 0)),
                  pl.BlockSpec((3, tr, cs), lambda i: (0, i, 0)), ANY],
        out_specs=[pl.BlockSpec((None, None, tr, cs), lambda i: (layer, lax.axis_index("c"), i, 0))],
        out_shape=[jax.ShapeDtypeStruct(buf.shape, F32)], sem=("parallel",), comms=comms, aliases={2: 0})
    return (res[0][0], res[1]) if comms else res[0]


def _adamw_math(w, g, m, v):
    m = ADAM_B1 * m + (1.0 - ADAM_B1) * g
    v = ADAM_B2 * v + (1.0 - ADAM_B2) * (g * g)
    m_hat = m / (1.0 - ADAM_B1 ** ADAM_STEP)
    v_hat = v / (1.0 - ADAM_B2 ** ADAM_STEP)
    return -ADAM_LR * (m_hat / (jnp.sqrt(v_hat) + ADAM_EPS) + ADAM_WD * w), m, v


def _adamw(w, g, m, v, name):
    layers, rows, cols = w.shape
    tr = _tile(rows, 256)

    def body(w_ref, g_ref, m_ref, v_ref, go_ref, d_ref, mo_ref, vo_ref):
        g = g_ref[...]
        go_ref[...] = g
        d_ref[...], mo_ref[...], vo_ref[...] = _adamw_math(w_ref[...], g, m_ref[...], v_ref[...])

    spec = pl.BlockSpec((None, tr, cols), lambda l, i: (l, i, 0))
    return _pcall(body, (w, g, m, v), name=name, grid=(layers, rows // tr), in_specs=[spec] * 4, out_specs=[spec] * 4,
                  out_shape=[jax.ShapeDtypeStruct(w.shape, F32)] * 4, sem=("parallel", "parallel"))


def _pack(arrays):
    flat = jnp.concatenate([a.reshape(-1) for a in arrays])
    pad = (-flat.shape[0]) % 1024
    return jnp.pad(flat, (0, pad)).reshape(-1, 128)


def _unpack(packed, like):
    flat = packed.reshape(-1)
    out, off = [], 0
    for a in like:
        out.append(flat[off:off + a.size].reshape(a.shape))
        off += a.size
    return out


class _SmallGather:
    n_sems = 7

    def __init__(self, g, parts):
        self.ro, self.rw = [g], [parts]

    def start(self, ro, rw, send, recv):
        x, y, c, _ = _place()
        for j in range(1, 8):
            _remote(ro[0], rw[0].at[4 * x + 2 * y + c], send(j - 1), recv(j - 1), _flip(x, y, c, j)).start()

    def finish(self, ro, rw, send, recv):
        x, y, c, _ = _place()
        for j in range(1, 8):
            px, py, pc = _flip(x, y, c, j)
            slot = rw[0].at[4 * px + 2 * py + pc]
            cp = _remote(slot, slot, send(j - 1), recv(j - 1), (x, y, c))
            cp.wait_recv()
            cp.wait_send()


def _small_update(own, parts, w, m, v, name):
    rows = w.shape[0]

    def body(own_ref, p_ref, w_ref, m_ref, v_ref, g_ref, d_ref, mo_ref, vo_ref):
        me = 4 * lax.axis_index("x") + 2 * lax.axis_index("y") + lax.axis_index("c")
        g = jnp.where(me == 0, own_ref[...], p_ref[0])
        for k in range(1, 8):
            g = g + jnp.where(me == k, own_ref[...], p_ref[k])
        g_ref[...] = g
        d_ref[...], mo_ref[...], vo_ref[...] = _adamw_math(w_ref[...], g, m_ref[...], v_ref[...])

    return pl.pallas_call(body, name=name, out_shape=[jax.ShapeDtypeStruct((rows, 128), F32)] * 4,
                          compiler_params=_params())(own, parts, w, m, v)


SMALL = ("norm_pre_mix", "norm_post_mix", "norm_pre_ffn", "norm_post_ffn", "sgu_ln_g", "sgu_ln_b", "sgu_w", "sgu_b")
BIG = (("ab_w_in", ("ab_w_in",)), ("ab_w_out", ("ab_w_out",)), ("sb_w_in", ("sb_w_in",)), ("sb_w_out", ("sb_w_out",)),
       ("ffn_w1", ("ffn_w1_0", "ffn_w1_1")), ("ffn_w2", ("ffn_w2_0", "ffn_w2_1")))
WEIGHTS = ("norm_pre_mix", "norm_post_mix", "norm_pre_ffn", "norm_post_ffn", "ab_w_in", "sgu_ln_g", "sgu_ln_b", "sgu_w", "sgu_b",
           "ab_w_out", "sb_w_in", "sb_w_out", "ffn_w1", "ffn_w2")


def kernel(x, norm_pre_mix, norm_post_mix, norm_pre_ffn, norm_post_ffn, ab_w_in, sgu_ln_g, sgu_ln_b, sgu_w, sgu_b, ab_w_out, sb_w_in, sb_w_out, ffn_w1, ffn_w2, loss_target, m_norm_pre_mix, m_norm_post_mix, m_norm_pre_ffn, m_norm_post_ffn, m_ab_w_in, m_sgu_ln_g, m_sgu_ln_b, m_sgu_w, m_sgu_b, m_ab_w_out, m_sb_w_in, m_sb_w_out, m_ffn_w1, m_ffn_w2, v_norm_pre_mix, v_norm_post_mix, v_norm_pre_ffn, v_norm_post_ffn, v_ab_w_in, v_sgu_ln_g, v_sgu_ln_b, v_sgu_w, v_sgu_b, v_ab_w_out, v_sb_w_in, v_sb_w_out, v_ffn_w1, v_ffn_w2):
    w = dict(norm_pre_mix=norm_pre_mix, norm_post_mix=norm_post_mix, norm_pre_ffn=norm_pre_ffn, norm_post_ffn=norm_post_ffn,
             ab_w_in=ab_w_in, sgu_ln_g=sgu_ln_g, sgu_ln_b=sgu_ln_b, sgu_w=sgu_w, sgu_b=sgu_b, ab_w_out=ab_w_out, sb_w_in=sb_w_in,
             sb_w_out=sb_w_out, ffn_w1=ffn_w1, ffn_w2=ffn_w2)
    m = dict(norm_pre_mix=m_norm_pre_mix, norm_post_mix=m_norm_post_mix, norm_pre_ffn=m_norm_pre_ffn, norm_post_ffn=m_norm_post_ffn,
             ab_w_in=m_ab_w_in, sgu_ln_g=m_sgu_ln_g, sgu_ln_b=m_sgu_ln_b, sgu_w=m_sgu_w, sgu_b=m_sgu_b, ab_w_out=m_ab_w_out,
             sb_w_in=m_sb_w_in, sb_w_out=m_sb_w_out, ffn_w1=m_ffn_w1, ffn_w2=m_ffn_w2)
    v = dict(norm_pre_mix=v_norm_pre_mix, norm_post_mix=v_norm_post_mix, norm_pre_ffn=v_norm_pre_ffn, norm_post_ffn=v_norm_post_ffn,
             ab_w_in=v_ab_w_in, sgu_ln_g=v_sgu_ln_g, sgu_ln_b=v_sgu_ln_b, sgu_w=v_sgu_w, sgu_b=v_sgu_b, ab_w_out=v_ab_w_out,
             sb_w_in=v_sb_w_in, sb_w_out=v_sb_w_out, ffn_w1=v_ffn_w1, ffn_w2=v_ffn_w2)
    big, pair, got = {}, {}, {}
    for name, keys in BIG:
        for layer, key in enumerate(keys):
            big[key] = _to_bf16_full(w[name], layer, KIND[key], f"bf16_{key}")
            half = _half_shape(big[key].shape, KIND[key])
            pair[key], got[key] = lax.empty((4,) + half, BF16), lax.empty((3,) + half, BF16)
    big["ab_w_in"] = _comm_call([_Gather(big["ab_w_in"], KIND["ab_w_in"])], "gather_first")[0][0]

    norms = {k: w["norm_" + k] for k in ("pre_mix", "post_mix", "pre_ffn", "post_ffn")}
    sgu = (sgu_ln_g, sgu_ln_b, sgu_w[0], sgu_b[0])
    loss_blk, grad_x, d_norms, d_sgu, (psum, got) = _local_step(x[0], loss_target[0], norms, sgu, big, (pair, got))
    loss = lax.psum(loss_blk[0, 0], ("x", "y", "c"))

    grads, deltas, new_m, new_v = {}, {}, {}, {}
    keys_of = dict(BIG)
    bufs, pending = {}, None
    for name in ("ffn_w2", "ffn_w1", "sb_w_in", "sb_w_out", "ab_w_out"):
        buf = lax.empty((len(keys_of[name]), 2) + psum[keys_of[name][0]].shape[1:], F32)
        for layer, key in enumerate(keys_of[name]):
            if pending is not None:
                buf, rws = _owner_sum(psum[key], got[key], buf, layer, f"sum_{key}", comms=[_Join([bufs[pending]])])
                bufs[pending], pending = rws[0][0], None
            else:
                buf = _owner_sum(psum[key], got[key], buf, layer, f"sum_{key}")
        bufs[name], pending = buf, name

    small_g = _pack([d_norms["pre_mix"], d_norms["post_mix"], d_norms["pre_ffn"], d_norms["post_ffn"],
                     d_sgu[0], d_sgu[1], d_sgu[2][None], d_sgu[3][None]])
    parts = lax.empty((8,) + small_g.shape, F32)

    rws = _comm_call([_Join([bufs["ab_w_out"]]), _ChipScatter(psum["ab_w_in"], got["ab_w_in"], DIAG), _SmallGather(small_g, parts)], "tail_comm")
    bufs["ab_w_out"], got["ab_w_in"], parts = rws[0][0], rws[1][0], rws[2][0]
    bufs["ab_w_in"] = _owner_sum(psum["ab_w_in"], got["ab_w_in"], lax.empty((1, 2) + psum["ab_w_in"].shape[1:], F32), 0, "sum_ab_w_in")
    bufs["ab_w_in"] = _comm_call([_Join([bufs["ab_w_in"]])], "join_last")[0][0]
    for name, _ in BIG:
        grads[name], deltas[name], new_m[name], new_v[name] = _adamw(w[name], bufs[name].reshape(w[name].shape), m[name], v[name], f"adamw_{name}")

    outs = _small_update(small_g, parts, _pack([w[k] for k in SMALL]), _pack([m[k] for k in SMALL]), _pack([v[k] for k in SMALL]), "small_update")
    like = [w[k] for k in SMALL]
    for dst, packed in zip((grads, deltas, new_m, new_v), outs):
        for k, a in zip(SMALL, _unpack(packed, like)):
            dst[k] = a

    return (loss, grad_x[None], *[grads[k] for k in WEIGHTS], *[deltas[k] for k in WEIGHTS],
            *[new_m[k] for k in WEIGHTS], *[new_v[k] for k in WEIGHTS])
```

```python
import functools

import jax
import jax.numpy as jnp
from jax import lax
from jax.experimental import pallas as pl
from jax.experimental.pallas import tpu as pltpu

F32 = jnp.float32
BF16 = jnp.bfloat16
MESH = pl.DeviceIdType.MESH

HEAD_DIM = 128
CHUNK = 128
DILATIONS = (1, 4, 16)
SB_BLOCK = 256
RMS_EPS = 1e-6
LN_EPS = 1e-5
ADAM_LR, ADAM_B1, ADAM_B2, ADAM_EPS, ADAM_WD, ADAM_STEP = 0.001, 0.9, 0.999, 1e-08, 0.01, 10
NEG = -1e30
V7X_VMEM_LIMIT = 48 * 1024 * 1024
ANY = pl.BlockSpec(memory_space=pl.ANY)


def _params(*sem):
    return pltpu.CompilerParams(dimension_semantics=sem if sem else None, vmem_limit_bytes=V7X_VMEM_LIMIT)


def _tile(n, pref):
    if n <= pref:
        return n
    t = pref
    while n % t:
        t -= 128
    return t


def _dot(a, b, dims):
    return lax.dot_general(a, b, (dims, ((), ())), preferred_element_type=F32)


NN = ((1,), (0,))
NT = ((1,), (1,))
TN = ((0,), (0,))


def _place():
    x, y, c = lax.axis_index("x"), lax.axis_index("y"), lax.axis_index("c")
    return x, y, c, 2 * x + y


def _flip(x, y, c, j):
    return (1 - x if j & 4 else x), (1 - y if j & 2 else y), (1 - c if j & 1 else c)


def _half_shape(full_shape, kind):
    rows, cols = full_shape
    return (rows // 2, cols // 4) if kind == "col" else (rows // 8, cols)


def _half(ref, kind, s, h):
    rh, cs = _half_shape(ref.shape, kind)
    if kind == "col":
        return ref.at[pl.ds(h * rh, rh), pl.ds(s * cs, cs)]
    return ref.at[pl.ds((2 * s + h) * rh, rh), :]


def _remote(src, dst, send, recv, to):
    return pltpu.make_async_remote_copy(src_ref=src, dst_ref=dst, send_sem=send, recv_sem=recv, device_id=to, device_id_type=MESH)


class _Gather:
    n_sems = 6

    def __init__(self, full, kind):
        self.ro, self.rw, self.kind = [], [full], kind

    def start(self, ro, rw, send, recv):
        x, y, c, mine = _place()
        own = _half(rw[0], self.kind, mine, c)
        for k, j in enumerate((2, 4, 6)):
            px, py, _ = _flip(x, y, c, j)
            _remote(own, own, send(k), recv(k), (px, py, c)).start()

    def finish(self, ro, rw, send, recv):
        x, y, c, mine = _place()
        own = _half(rw[0], self.kind, mine, c)
        for k, j in enumerate((2, 4, 6)):
            px, py, _ = _flip(x, y, c, j)
            got = _half(rw[0], self.kind, 2 * px + py, c)
            _remote(got, got, send(k), recv(k), (x, y, c)).wait_recv()
            _remote(got, got, send(3 + k), recv(3 + k), (x, y, 1 - c)).start()
        for k, j in enumerate((2, 4, 6)):
            px, py, _ = _flip(x, y, c, j)
            got = _half(rw[0], self.kind, 2 * px + py, 1 - c)
            _remote(got, got, send(3 + k), recv(3 + k), (x, y, c)).wait_recv()
        for k in range(6):
            _remote(own, own, send(k), recv(k), (x, y, c)).wait_send()


class _GatherSend:
    def __init__(self, full, kind, patterns, part=(0, 1)):
        self.ro, self.rw, self.kind, self.patterns, self.part, self.n_sems = [], [full], kind, patterns, part, len(patterns)

    def _rows(self, half):
        i, n = self.part
        rows = half.shape[0] // n
        return half.at[pl.ds(i * rows, rows), :]

    def start(self, ro, rw, send, recv):
        x, y, c, mine = _place()
        own = self._rows(_half(rw[0], self.kind, mine, c))
        for k, j in enumerate(self.patterns):
            px, py, _ = _flip(x, y, c, j)
            _remote(own, own, send(k), recv(k), (px, py, c)).start()

    def finish(self, ro, rw, send, recv):
        x, y, c, _ = _place()
        for k, j in enumerate(self.patterns):
            px, py, _ = _flip(x, y, c, j)
            got = self._rows(_half(rw[0], self.kind, 2 * px + py, c))
            cp = _remote(got, got, send(k), recv(k), (x, y, c))
            cp.wait_recv()
            cp.wait_send()


class _GatherFwd:
    def __init__(self, full, kind, patterns):
        self.ro, self.rw, self.kind, self.patterns, self.n_sems = [], [full], kind, patterns, len(patterns)

    def start(self, ro, rw, send, recv):
        x, y, c, _ = _place()
        for k, j in enumerate(self.patterns):
            px, py, _ = _flip(x, y, c, j)
            got = _half(rw[0], self.kind, 2 * px + py, c)
            _remote(got, got, send(k), recv(k), (x, y, 1 - c)).start()

    def finish(self, ro, rw, send, recv):
        x, y, c, _ = _place()
        for k, j in enumerate(self.patterns):
            px, py, _ = _flip(x, y, c, j)
            got = _half(rw[0], self.kind, 2 * px + py, 1 - c)
            cp = _remote(got, got, send(k), recv(k), (x, y, c))
            cp.wait_recv()
            cp.wait_send()


class _PairSwap:
    n_sems = 4

    def __init__(self, dw16, pair, kind):
        self.ro, self.rw, self.kind = [dw16], [pair], kind

    def start(self, ro, rw, send, recv):
        x, y, c, _ = _place()
        for s in range(4):
            _remote(_half(ro[0], self.kind, s, 1 - c), rw[0].at[s], send(s), recv(s), (x, y, 1 - c)).start()

    def finish(self, ro, rw, send, recv):
        x, y, c, _ = _place()
        for s in range(4):
            cp = _remote(rw[0].at[s], rw[0].at[s], send(s), recv(s), (x, y, c))
            cp.wait_recv()
            cp.wait_send()


class _ChipScatter:
    def __init__(self, psum, got, patterns):
        self.ro, self.rw, self.patterns, self.n_sems = [psum], [got], patterns, len(patterns)

    def start(self, ro, rw, send, recv):
        x, y, c, _ = _place()
        for k, j in enumerate(self.patterns):
            px, py, _ = _flip(x, y, c, j)
            _remote(ro[0].at[2 * px + py], rw[0].at[j // 2 - 1], send(k), recv(k), (px, py, c)).start()

    def finish(self, ro, rw, send, recv):
        x, y, c, _ = _place()
        for k, j in enumerate(self.patterns):
            slot = rw[0].at[j // 2 - 1]
            cp = _remote(slot, slot, send(k), recv(k), (x, y, c))
            cp.wait_recv()
            cp.wait_send()


class _Join:
    def __init__(self, bufs):
        self.ro, self.rw, self.n_sems = [], list(bufs), sum(b.shape[0] for b in bufs)

    def _copies(self, rw, send, recv, slot):
        x, y, c, _ = _place()
        k = 0
        for ref in rw:
            for l in range(ref.shape[0]):
                yield _remote(ref.at[l, c], ref.at[l, slot(c)], send(k), recv(k), (x, y, 1 - c))
                k += 1

    def start(self, ro, rw, send, recv):
        for cp in self._copies(rw, send, recv, lambda c: c):
            cp.start()

    def finish(self, ro, rw, send, recv):
        for cp in self._copies(rw, send, recv, lambda c: 1 - c):
            cp.wait_recv()
        for cp in self._copies(rw, send, recv, lambda c: c):
            cp.wait_send()


def _comm_layout(comms):
    ro = [a for c in comms for a in c.ro]
    rw = [a for c in comms for a in c.rw]
    return ro, rw, sum(c.n_sems for c in comms)


def _comm_each(comms, method, ro_refs, rw_refs, send, recv):
    i_ro = i_rw = i_sem = 0
    for c in comms:
        getattr(c, method)(ro_refs[i_ro:i_ro + len(c.ro)], rw_refs[i_rw:i_rw + len(c.rw)],
                           lambda k, b=i_sem: send.at[b + k], lambda k, b=i_sem: recv.at[b + k])
        i_ro, i_rw, i_sem = i_ro + len(c.ro), i_rw + len(c.rw), i_sem + c.n_sems


def _split_results(comms, rws):
    out, i = [], 0
    for c in comms:
        out.append(list(rws[i:i + len(c.rw)]))
        i += len(c.rw)
    return out


def _comm_call(comms, name):
    ro, rw, n_sems = _comm_layout(comms)

    def body(*refs):
        ro_refs = refs[:len(ro)]
        rw_refs = refs[len(ro) + len(rw):len(ro) + 2 * len(rw)]
        send, recv = refs[len(ro) + 2 * len(rw):]
        _comm_each(comms, "start", ro_refs, rw_refs, send, recv)
        _comm_each(comms, "finish", ro_refs, rw_refs, send, recv)

    rws = pl.pallas_call(
        body, name=name, in_specs=[ANY] * (len(ro) + len(rw)), out_specs=[ANY] * len(rw),
        out_shape=[jax.ShapeDtypeStruct(a.shape, a.dtype) for a in rw],
        input_output_aliases={len(ro) + k: k for k in range(len(rw))},
        scratch_shapes=[pltpu.SemaphoreType.DMA((n_sems,)), pltpu.SemaphoreType.DMA((n_sems,))],
    )(*ro, *rw)
    return _split_results(comms, rws)


def _pcall(body, args, *, name, grid, in_specs, out_specs, out_shape, scratch=(), sem=(), comms=(), aliases=None):
    n_in, n_out, n_scr = len(in_specs), len(out_specs), len(scratch)
    aliases = dict(aliases or {})
    if not comms:
        return pl.pallas_call(body, name=name, grid=grid, in_specs=list(in_specs), out_specs=list(out_specs),
                              out_shape=list(out_shape), scratch_shapes=list(scratch), input_output_aliases=aliases,
                              compiler_params=_params(*sem))(*args)
    ro, rw, n_sems = _comm_layout(comms)

    def carrier(*refs):
        ins = refs[:n_in]
        ro_refs = refs[n_in:n_in + len(ro)]
        o0 = n_in + len(ro) + len(rw)
        outs = refs[o0:o0 + n_out]
        rw_refs = refs[o0 + n_out:o0 + n_out + len(rw)]
        s0 = o0 + n_out + len(rw)
        send, recv = refs[s0 + n_scr], refs[s0 + n_scr + 1]
        ids = [pl.program_id(a) for a in range(len(grid))]
        first = functools.reduce(jnp.logical_and, [i == 0 for i in ids])
        last = functools.reduce(jnp.logical_and, [i == g - 1 for i, g in zip(ids, grid)])

        @pl.when(first)
        def _():
            _comm_each(comms, "start", ro_refs, rw_refs, send, recv)

        body(*ins, *outs, *refs[s0:s0 + n_scr])

        @pl.when(last)
        def _():
            _comm_each(comms, "finish", ro_refs, rw_refs, send, recv)

    res = pl.pallas_call(
        carrier, name=name, grid=grid, in_specs=list(in_specs) + [ANY] * (len(ro) + len(rw)),
        out_specs=list(out_specs) + [ANY] * len(rw),
        out_shape=list(out_shape) + [jax.ShapeDtypeStruct(a.shape, a.dtype) for a in rw],
        input_output_aliases={**aliases, **{n_in + len(ro) + k: n_out + k for k in range(len(rw))}},
        scratch_shapes=list(scratch) + [pltpu.SemaphoreType.DMA((n_sems,)), pltpu.SemaphoreType.DMA((n_sems,))],
        compiler_params=_params(*["arbitrary"] * len(grid)),
    )(*args, *ro, *rw)
    return list(res[:n_out]), _split_results(comms, res[n_out:])


def _matmul(a, b, mode, out_dtype, name, a_square=False, relu_out=False, mul2=None, comms=()):
    if mode == "nn":
        (m, k), n = a.shape, b.shape[1]
    elif mode == "nt":
        (m, k), n = a.shape, b.shape[0]
    else:
        (k, m), n = a.shape, b.shape[1]
    tm, tn, tk = _tile(m, 1024), _tile(n, 1024), _tile(k, 2048)
    nk = k // tk
    dims = {"nn": NN, "nt": NT, "tn": TN}[mode]
    a_spec = pl.BlockSpec((tk, tm), lambda i, j, kk: (kk, i)) if mode == "tn" else pl.BlockSpec((tm, tk), lambda i, j, kk: (i, kk))
    b_spec = pl.BlockSpec((tn, tk), lambda i, j, kk: (j, kk)) if mode == "nt" else pl.BlockSpec((tk, tn), lambda i, j, kk: (kk, j))
    o_spec = pl.BlockSpec((tm, tn), lambda i, j, kk: (i, j))

    def body(a_ref, b_ref, *rest):
        m_ref = None if mul2 is None else rest[0]
        o_ref = rest[0 if mul2 is None else 1]
        kk = pl.program_id(2)

        def partial():
            av = a_ref[...]
            if a_square:
                av = av * av
            return _dot(av, b_ref[...], dims)

        def finish(r):
            if relu_out:
                r = jnp.maximum(r, 0.0)
            if mul2 is not None:
                r = r * (2.0 * m_ref[...].astype(F32))
            o_ref[...] = r.astype(out_dtype)

        if nk == 1:
            finish(partial())
            return
        acc_ref = rest[-1]

        @pl.when(kk == 0)
        def _():
            acc_ref[...] = partial()

        @pl.when(kk > 0)
        def _():
            acc_ref[...] += partial()

        @pl.when(kk == nk - 1)
        def _():
            finish(acc_ref[...])

    args = (a, b) if mul2 is None else (a, b, mul2)
    specs = [a_spec, b_spec] + ([] if mul2 is None else [o_spec])
    res = _pcall(body, args, name=name, grid=(m // tm, n // tn, nk), in_specs=specs, out_specs=[o_spec],
                 out_shape=[jax.ShapeDtypeStruct((m, n), out_dtype)], scratch=[pltpu.VMEM((tm, tn), F32)] if nk > 1 else [],
                 sem=("parallel", "parallel", "arbitrary"), comms=comms)
    return (res[0][0], res[1]) if comms else res[0]


NORM_ROWS = 256


def _rms(x, g):
    rstd = lax.rsqrt(jnp.mean(x * x, axis=-1, keepdims=True) + RMS_EPS)
    n = x * rstd
    return n * g, n, rstd


def _rms_bwd(n, rstd, g, dout):
    dn = dout * g
    return rstd * (dn - n * jnp.mean(dn * n, axis=-1, keepdims=True))


def _row_spec(d):
    return pl.BlockSpec((NORM_ROWS, d), lambda i: (i, 0))


def _vec_spec(d):
    return pl.BlockSpec((1, d), lambda i: (0, 0))


def _accumulate(ref, val):
    @pl.when(pl.program_id(0) == 0)
    def _():
        ref[...] = jnp.zeros_like(ref)

    ref[...] += val


def _rms_fwd(x, g, name):
    t, d = x.shape

    def body(x_ref, g_ref, h_ref):
        h_ref[...] = _rms(x_ref[...], g_ref[...])[0].astype(BF16)

    return pl.pallas_call(
        body, name=name, grid=(t // NORM_ROWS,), in_specs=[_row_spec(d), _vec_spec(d)], out_specs=_row_spec(d),
        out_shape=jax.ShapeDtypeStruct((t, d), BF16), compiler_params=_params("parallel"),
    )(x, g)


def _post_pre_fwd(y, g_post, x, g_pre, name, comms=()):
    t, d = x.shape

    def body(y_ref, gp_ref, x_ref, gn_ref, xn_ref, h_ref):
        xn = x_ref[...] + _rms(y_ref[...], gp_ref[...])[0]
        xn_ref[...] = xn
        h_ref[...] = _rms(xn, gn_ref[...])[0].astype(BF16)

    return _pcall(
        body, (y, g_post, x, g_pre), name=name, grid=(t // NORM_ROWS,),
        in_specs=[_row_spec(d), _vec_spec(d), _row_spec(d), _vec_spec(d)], out_specs=[_row_spec(d), _row_spec(d)],
        out_shape=[jax.ShapeDtypeStruct((t, d), F32), jax.ShapeDtypeStruct((t, d), BF16)], sem=("parallel",), comms=comms)


def _final_fwd_bwd(y, g_post, x, target, name):
    t, d = x.shape

    def body(y_ref, g_ref, x_ref, t_ref, loss_ref, dx_ref, dy_ref, dg_ref):
        g = g_ref[...]
        out, n, rstd = _rms(y_ref[...], g)
        e = x_ref[...] + out - t_ref[...]
        _accumulate(loss_ref, jnp.full(loss_ref.shape, 0.5 / d, F32) * jnp.sum(e * e))
        dx = e * (1.0 / d)
        dx_ref[...] = dx
        dy_ref[...] = _rms_bwd(n, rstd, g, dx).astype(BF16)
        _accumulate(dg_ref, jnp.sum(dx * n, axis=0, keepdims=True))

    return pl.pallas_call(
        body, name=name, grid=(t // NORM_ROWS,),
        in_specs=[_row_spec(d), _vec_spec(d), _row_spec(d), _row_spec(d)],
        out_specs=[pl.BlockSpec((8, 128), lambda i: (0, 0)), _row_spec(d), _row_spec(d), _vec_spec(d)],
        out_shape=[jax.ShapeDtypeStruct((8, 128), F32), jax.ShapeDtypeStruct((t, d), F32),
                   jax.ShapeDtypeStruct((t, d), BF16), jax.ShapeDtypeStruct((1, d), F32)],
        compiler_params=_params("arbitrary"),
    )(y, g_post, x, target)


def _pre_post_bwd(x, g_pre, dh, dx_in, y, g_post, name, comms=()):
    t, d = x.shape
    both = y is not None

    def body(x_ref, gp_ref, dh_ref, dxi_ref, *rest):
        if both:
            y_ref, gq_ref, dx_ref, dy_ref, dgp_ref, dgq_ref = rest
        else:
            dx_ref, dgp_ref = rest
        gp = gp_ref[...]
        _, n, rstd = _rms(x_ref[...], gp)
        dh_v = dh_ref[...]
        dx = dxi_ref[...] + _rms_bwd(n, rstd, gp, dh_v)
        dx_ref[...] = dx
        _accumulate(dgp_ref, jnp.sum(dh_v * n, axis=0, keepdims=True))
        if both:
            gq = gq_ref[...]
            _, ny, rstdy = _rms(y_ref[...], gq)
            dy_ref[...] = _rms_bwd(ny, rstdy, gq, dx).astype(BF16)
            _accumulate(dgq_ref, jnp.sum(dx * ny, axis=0, keepdims=True))

    in_specs = [_row_spec(d), _vec_spec(d), _row_spec(d), _row_spec(d)]
    args = [x, g_pre, dh, dx_in]
    if both:
        in_specs += [_row_spec(d), _vec_spec(d)]
        args += [y, g_post]
        out_specs = [_row_spec(d), _row_spec(d), _vec_spec(d), _vec_spec(d)]
        out_shape = [jax.ShapeDtypeStruct((t, d), F32), jax.ShapeDtypeStruct((t, d), BF16),
                     jax.ShapeDtypeStruct((1, d), F32), jax.ShapeDtypeStruct((1, d), F32)]
    else:
        out_specs = [_row_spec(d), _vec_spec(d)]
        out_shape = [jax.ShapeDtypeStruct((t, d), F32), jax.ShapeDtypeStruct((1, d), F32)]
    return _pcall(body, args, name=name, grid=(t // NORM_ROWS,), in_specs=in_specs, out_specs=out_specs, out_shape=out_shape,
                  sem=("arbitrary",), comms=comms)


def _gelu(x):
    return 0.5 * x * (1.0 + lax.erf(x * 0.7071067811865476))


def _gelu_grad(x):
    return 0.5 * (1.0 + lax.erf(x * 0.7071067811865476)) + x * jnp.exp(-0.5 * x * x) * 0.3989422804014327


def _layernorm(v, g, b):
    mu = jnp.mean(v, axis=-1, keepdims=True)
    vc = v - mu
    rs = lax.rsqrt(jnp.mean(vc * vc, axis=-1, keepdims=True) + LN_EPS)
    vhat = vc * rs
    return vhat * g + b, vhat, rs


def _tril_mask():
    return lax.broadcasted_iota(jnp.int32, (CHUNK, CHUNK), 0) >= lax.broadcasted_iota(jnp.int32, (CHUNK, CHUNK), 1)


def _sgu_fwd(z, ln_g, ln_b, w16, bias_b, name, comms=()):
    t = z.shape[0]
    groups = w16.shape[0]
    a = groups * CHUNK

    def body(u_ref, v_ref, g_ref, b_ref, w_ref, bb_ref, o_ref):
        u = _gelu(u_ref[...].astype(F32))
        vn = _layernorm(_gelu(v_ref[...].astype(F32)), g_ref[...], b_ref[...])[0].astype(BF16)
        tril = _tril_mask()
        for g in range(groups):
            sl = slice(g * CHUNK, (g + 1) * CHUNK)
            w = jnp.where(tril, w_ref[g], jnp.zeros((), BF16))
            mixed = _dot(w, vn[:, sl], NN) + bb_ref[g]
            o_ref[:, sl] = (u[:, sl] * mixed).astype(BF16)

    full3 = pl.BlockSpec((groups, CHUNK, CHUNK), lambda c: (0, 0, 0))
    res = _pcall(
        body, (z, z, ln_g, ln_b, w16, bias_b), name=name, grid=(t // CHUNK,),
        in_specs=[pl.BlockSpec((CHUNK, a), lambda c: (c, 0)), pl.BlockSpec((CHUNK, a), lambda c: (c, 1)),
                  _vec_spec(a), _vec_spec(a), full3, full3],
        out_specs=[pl.BlockSpec((CHUNK, a), lambda c: (c, 0))], out_shape=[jax.ShapeDtypeStruct((t, a), BF16)],
        sem=("parallel",), comms=comms)
    return (res[0][0], res[1]) if comms else res[0]


def _sgu_bwd(z, dab, ln_g, ln_b, w16, bias_b, name, comms=()):
    t = z.shape[0]
    groups = w16.shape[0]
    a = groups * CHUNK

    def body(u_ref, v_ref, da_ref, g_ref, b_ref, w_ref, bb_ref, duv_ref, dg_ref, db_ref, dw_ref, dbs_ref, dvn_ref):
        up = u_ref[...].astype(F32)
        vp = v_ref[...].astype(F32)
        u = _gelu(up)
        ln_gain = g_ref[...]
        vn32, vhat, rs = _layernorm(_gelu(vp), ln_gain, b_ref[...])
        vn = vn32.astype(BF16)
        da = da_ref[...].astype(F32)
        tril = _tril_mask()
        ones = jnp.ones((8, CHUNK), F32)

        @pl.when(pl.program_id(0) == 0)
        def _():
            dw_ref[...] = jnp.zeros_like(dw_ref)
            dbs_ref[...] = jnp.zeros_like(dbs_ref)

        for g in range(groups):
            sl = slice(g * CHUNK, (g + 1) * CHUNK)
            w = jnp.where(tril, w_ref[g], jnp.zeros((), BF16))
            mixed = _dot(w, vn[:, sl], NN) + bb_ref[g]
            dmix = da[:, sl] * u[:, sl]
            dmix16 = dmix.astype(BF16)
            duv_ref[:, sl] = (da[:, sl] * mixed * _gelu_grad(up[:, sl])).astype(BF16)
            dvn_ref[:, sl] = _dot(w, dmix16, TN)
            dw_ref[g] += jnp.where(tril, _dot(dmix16, vn[:, sl], NT), 0.0)
            dbs_ref[g:g + 1, :] += lax.dot_general(ones, dmix, (NT, ((), ())), precision=lax.Precision.HIGHEST,
                                                   preferred_element_type=F32)[0:1]
        dvn = dvn_ref[...]
        dvhat = dvn * ln_gain
        dva = rs * (dvhat - jnp.mean(dvhat, axis=-1, keepdims=True) - vhat * jnp.mean(dvhat * vhat, axis=-1, keepdims=True))
        duv_ref[:, a:] = (dva * _gelu_grad(vp)).astype(BF16)
        _accumulate(dg_ref, jnp.sum(dvn * vhat, axis=0, keepdims=True))
        _accumulate(db_ref, jnp.sum(dvn, axis=0, keepdims=True))

    full3 = pl.BlockSpec((groups, CHUNK, CHUNK), lambda c: (0, 0, 0))
    return _pcall(
        body, (z, z, dab, ln_g, ln_b, w16, bias_b), name=name, grid=(t // CHUNK,),
        in_specs=[pl.BlockSpec((CHUNK, a), lambda c: (c, 0)), pl.BlockSpec((CHUNK, a), lambda c: (c, 1)),
                  pl.BlockSpec((CHUNK, a), lambda c: (c, 0)), _vec_spec(a), _vec_spec(a), full3, full3],
        out_specs=[pl.BlockSpec((CHUNK, 2 * a), lambda c: (c, 0)), _vec_spec(a), _vec_spec(a), full3,
                   pl.BlockSpec((groups, CHUNK), lambda c: (0, 0))],
        out_shape=[jax.ShapeDtypeStruct((t, 2 * a), BF16), jax.ShapeDtypeStruct((1, a), F32), jax.ShapeDtypeStruct((1, a), F32),
                   jax.ShapeDtypeStruct((groups, CHUNK, CHUNK), F32), jax.ShapeDtypeStruct((groups, CHUNK), F32)],
        scratch=[pltpu.VMEM((CHUNK, a), F32)], sem=("arbitrary",), comms=comms)


def _dil_masks(d):
    qi = lax.broadcasted_iota(jnp.int32, (CHUNK, CHUNK), 0)
    kj = lax.broadcasted_iota(jnp.int32, (CHUNK, CHUNK), 1)
    dist_c = qi - kj
    return dist_c >= 0, dist_c <= 0, (dist_c * d).astype(F32), ((dist_c + CHUNK) * d).astype(F32)


def _alibi_slope(h, heads):
    return 2.0 ** (-8.0 * (h + 1) / heads)


def _dil_view(z, d):
    t, w = z.shape[0], z.shape[1] // 5
    if d == 1:
        return z, 5, 2
    return z[:, 2 * w:].reshape(t // d, d * 3 * w), 3, 0


def _dil_fwd(z, d, name, comms=()):
    t = z.shape[0]
    w = z.shape[1] // 5
    heads = w // HEAD_DIM
    nb = t // d // CHUNK
    scale = HEAD_DIM ** -0.5
    zv, mult, col_q = _dil_view(z, d)

    def body(q_ref, kp_ref, kc_ref, vp_ref, vc_ref, o_ref, l_ref):
        ok_c, ok_p0, bias_c, bias_p = _dil_masks(d)
        ok_p = ok_p0 & (pl.program_id(1) > 0)
        hs = range(heads)
        sl = [slice(h * HEAD_DIM, (h + 1) * HEAD_DIM) for h in hs]
        slope = [_alibi_slope(h, heads) for h in hs]
        ones = jnp.ones((CHUNK, HEAD_DIM), BF16)
        s_c = [_dot(q_ref[:, sl[h]], kc_ref[:, sl[h]], NT) for h in hs]
        s_p = [_dot(q_ref[:, sl[h]], kp_ref[:, sl[h]], NT) for h in hs]
        s_c = [jnp.where(ok_c, s_c[h] * scale - slope[h] * bias_c, NEG) for h in hs]
        s_p = [jnp.where(ok_p, s_p[h] * scale - slope[h] * bias_p, NEG) for h in hs]
        m = [jnp.max(jnp.maximum(s_c[h], s_p[h]), axis=1, keepdims=True) for h in hs]
        p_c = [jnp.exp(s_c[h] - m[h]).astype(BF16) for h in hs]
        p_p = [jnp.exp(s_p[h] - m[h]).astype(BF16) for h in hs]
        den = [_dot(p_c[h], ones, NN) + _dot(p_p[h], ones, NN) for h in hs]
        o = [_dot(p_c[h], vc_ref[:, sl[h]], NN) + _dot(p_p[h], vp_ref[:, sl[h]], NN) for h in hs]
        l_ref[...] = jnp.zeros_like(l_ref)
        for h in hs:
            o_ref[:, sl[h]] = (o[h] / den[h]).astype(BF16)
            l_ref[:, h:h + 1] = m[h] + jnp.log(den[h][:, 0:1])

    def zspec(col, prev):
        if prev:
            return pl.BlockSpec((CHUNK, w), lambda r, n: (jnp.maximum(n - 1, 0), r * mult + col_q + col))
        return pl.BlockSpec((CHUNK, w), lambda r, n: (n, r * mult + col_q + col))

    res = _pcall(
        body, (zv, zv, zv, zv, zv), name=name, grid=(d, nb),
        in_specs=[zspec(0, False), zspec(1, True), zspec(1, False), zspec(2, True), zspec(2, False)],
        out_specs=[pl.BlockSpec((CHUNK, w), lambda r, n: (n, r)), pl.BlockSpec((CHUNK, HEAD_DIM), lambda r, n: (n, r))],
        out_shape=[jax.ShapeDtypeStruct((t // d, d * w), BF16), jax.ShapeDtypeStruct((t // d, d * HEAD_DIM), F32)],
        sem=("parallel", "parallel"), comms=comms)
    (o, lse), rws = res if comms else (res, None)
    outs = (o.reshape(t, w), lse.reshape(t, HEAD_DIM))
    return (outs, rws) if comms else outs


def _dil_merge(a_out, outs, lses, name, comms=()):
    t, a = a_out.shape
    w = outs[0].shape[1]
    heads = w // HEAD_DIM
    nbr = len(outs)

    def body(a_ref, *rest):
        o_refs, l_refs, (ab_ref, lt_ref) = rest[:nbr], rest[nbr:2 * nbr], rest[2 * nbr:]
        ls = [r[...] for r in l_refs]
        m = functools.reduce(jnp.maximum, ls)
        ws = [jnp.exp(l - m) for l in ls]
        tot = functools.reduce(jnp.add, ws)
        ws = [wt / tot for wt in ws]
        ab_ref[:, :a] = a_ref[...]
        for h in range(heads):
            sl = slice(h * HEAD_DIM, (h + 1) * HEAD_DIM)
            mix = functools.reduce(jnp.add, [wt[:, h:h + 1] * r[:, sl].astype(F32) for wt, r in zip(ws, o_refs)])
            ab_ref[:, a + h * HEAD_DIM:a + (h + 1) * HEAD_DIM] = mix.astype(BF16)
        lt_ref[...] = m + jnp.log(tot)

    return _pcall(
        body, (a_out, *outs, *lses), name=name, grid=(t // NORM_ROWS,),
        in_specs=[_row_spec(a)] + [_row_spec(w)] * nbr + [_row_spec(HEAD_DIM)] * nbr,
        out_specs=[_row_spec(a + w), _row_spec(HEAD_DIM)],
        out_shape=[jax.ShapeDtypeStruct((t, a + w), BF16), jax.ShapeDtypeStruct((t, HEAD_DIM), F32)],
        sem=("parallel",), comms=comms)


def _dil_delta(ab, dab, name):
    t, aw = ab.shape
    w = aw // 2
    heads = w // HEAD_DIM

    def body(o_ref, do_ref, dl_ref):
        dl_ref[...] = jnp.zeros_like(dl_ref)
        for h in range(heads):
            sl = slice(h * HEAD_DIM, (h + 1) * HEAD_DIM)
            dl_ref[:, h:h + 1] = jnp.sum(do_ref[:, sl].astype(F32) * o_ref[:, sl].astype(F32), axis=1, keepdims=True)

    half = pl.BlockSpec((NORM_ROWS, w), lambda i: (i, 1))
    return pl.pallas_call(body, name=name, grid=(t // NORM_ROWS,), in_specs=[half, half], out_specs=_row_spec(HEAD_DIM),
                          out_shape=jax.ShapeDtypeStruct((t, HEAD_DIM), F32), compiler_params=_params("parallel"))(ab, dab)


def _dil_bwd(z, dab, ltot, delta, d, name, comms=()):
    t = z.shape[0]
    w = z.shape[1] // 5
    heads = w // HEAD_DIM
    nb = t // d // CHUNK
    scale = HEAD_DIM ** -0.5

    def body(q_ref, qn_ref, kp_ref, kc_ref, vp_ref, vc_ref, do_ref, don_ref, l_ref, ln_ref, dl_ref, dln_ref,
             dq_ref, dk_ref, dv_ref):
        n = pl.program_id(1)
        ok_c, ok_p0, bias_c, bias_p = _dil_masks(d)
        ok_p = ok_p0 & (n > 0)
        ok_n = ok_p0 & (n < nb - 1)
        hs = range(heads)
        sl = [slice(h * HEAD_DIM, (h + 1) * HEAD_DIM) for h in hs]
        slope = [_alibi_slope(h, heads) for h in hs]
        q, qn = [q_ref[:, s] for s in sl], [qn_ref[:, s] for s in sl]
        kp, kc = [kp_ref[:, s] for s in sl], [kc_ref[:, s] for s in sl]
        vp, vc = [vp_ref[:, s] for s in sl], [vc_ref[:, s] for s in sl]
        do, don = [do_ref[:, s] for s in sl], [don_ref[:, s] for s in sl]
        s_c = [_dot(q[h], kc[h], NT) for h in hs]
        s_p = [_dot(q[h], kp[h], NT) for h in hs]
        s_n = [_dot(qn[h], kc[h], NT) for h in hs]
        dp_c = [_dot(do[h], vc[h], NT) for h in hs]
        dp_p = [_dot(do[h], vp[h], NT) for h in hs]
        dp_n = [_dot(don[h], vc[h], NT) for h in hs]
        delta = [dl_ref[:, h:h + 1] for h in hs]
        delta_n = [dln_ref[:, h:h + 1] for h in hs]
        p_c = [jnp.exp(jnp.where(ok_c, s_c[h] * scale - slope[h] * bias_c, NEG) - l_ref[:, h:h + 1]) for h in hs]
        p_p = [jnp.exp(jnp.where(ok_p, s_p[h] * scale - slope[h] * bias_p, NEG) - l_ref[:, h:h + 1]) for h in hs]
        p_n = [jnp.exp(jnp.where(ok_n, s_n[h] * scale - slope[h] * bias_p, NEG) - ln_ref[:, h:h + 1]) for h in hs]
        ds_c = [(p_c[h] * (dp_c[h] - delta[h])).astype(BF16) for h in hs]
        ds_p = [(p_p[h] * (dp_p[h] - delta[h])).astype(BF16) for h in hs]
        ds_n = [(p_n[h] * (dp_n[h] - delta_n[h])).astype(BF16) for h in hs]
        dq = [_dot(ds_c[h], kc[h], NN) + _dot(ds_p[h], kp[h], NN) for h in hs]
        dk = [_dot(ds_c[h], q[h], TN) + _dot(ds_n[h], qn[h], TN) for h in hs]
        dv = [_dot(p_c[h].astype(BF16), do[h], TN) + _dot(p_n[h].astype(BF16), don[h], TN) for h in hs]
        for h in hs:
            dq_ref[:, sl[h]] = (dq[h] * scale).astype(BF16)
            dk_ref[:, sl[h]] = (dk[h] * scale).astype(BF16)
            dv_ref[:, sl[h]] = dv[h].astype(BF16)

    def spec(mult, col, shift, width=w):
        if shift < 0:
            return pl.BlockSpec((CHUNK, width), lambda r, n: (jnp.maximum(n - 1, 0), r * mult + col))
        if shift > 0:
            return pl.BlockSpec((CHUNK, width), lambda r, n: (jnp.minimum(n + 1, nb - 1), r * mult + col))
        return pl.BlockSpec((CHUNK, width), lambda r, n: (n, r * mult + col))

    zv, mult, cq = _dil_view(z, d)
    dov = dab[:, w:].reshape(t // d, d * w)
    lv = ltot.reshape(t // d, d * HEAD_DIM)
    dlv = delta.reshape(t // d, d * HEAD_DIM)
    ospec = spec(1, 0, 0)
    res = _pcall(
        body, (zv, zv, zv, zv, zv, zv, dov, dov, lv, lv, dlv, dlv), name=name, grid=(d, nb),
        in_specs=[spec(mult, cq, 0), spec(mult, cq, 1), spec(mult, cq + 1, -1), spec(mult, cq + 1, 0),
                  spec(mult, cq + 2, -1), spec(mult, cq + 2, 0), spec(1, 0, 0), spec(1, 0, 1),
                  spec(1, 0, 0, HEAD_DIM), spec(1, 0, 1, HEAD_DIM), spec(1, 0, 0, HEAD_DIM), spec(1, 0, 1, HEAD_DIM)],
        out_specs=[ospec, ospec, ospec], out_shape=[jax.ShapeDtypeStruct((t // d, d * w), BF16)] * 3,
        sem=("parallel", "parallel"), comms=comms)
    outs, rws = res if comms else (res, None)
    outs = [o.reshape(t, w) for o in outs]
    return (outs, rws) if comms else outs


def _dz_assemble(duv, parts, name):
    t, a2 = duv.shape
    w = parts[0][0].shape[1]
    nbr = len(parts)

    def body(duv_ref, *rest):
        refs, dz_ref = rest[:-1], rest[-1]
        dz_ref[:, :a2] = duv_ref[...]
        for i in range(3):
            tot = functools.reduce(jnp.add, [refs[b * 3 + i][...].astype(F32) for b in range(nbr)])
            dz_ref[:, a2 + i * w:a2 + (i + 1) * w] = tot.astype(BF16)

    flat = [p for branch in parts for p in branch]
    return pl.pallas_call(
        body, name=name, grid=(t // NORM_ROWS,), in_specs=[_row_spec(a2)] + [_row_spec(w)] * len(flat),
        out_specs=_row_spec(a2 + 3 * w), out_shape=jax.ShapeDtypeStruct((t, a2 + 3 * w), BF16),
        compiler_params=_params("parallel"),
    )(duv, *flat)


def _split_dot(x, m16):
    hi = x.astype(BF16)
    lo = (x - hi.astype(F32)).astype(BF16)
    return _dot(hi, m16, NN) + _dot(lo, m16, NN)


SB_DEAD = -110.0


def _sb_scaled(q):
    return (q.astype(F32) * (HEAD_DIM ** -0.5)).astype(BF16)


SB_PAIR = 2


def _sb_logs(qs, kj, below):
    zt = [_dot(q, k, NT) for q, k in zip(qs, kj)]
    sp = [jnp.maximum(z, 0.0) + jnp.log(1.0 + jnp.exp(-jnp.abs(z))) for z in zt]
    return [z - s for z, s in zip(zt, sp)], [(-s if below is None else jnp.where(below, -s, 0.0)) for s in sp]


def _sb_alive(s, i, c_run):
    return (s <= i) & (jnp.max(c_run) > SB_DEAD)


def _sb_fwd(zc, name, comms=()):
    t = zc.shape[0]
    c = zc.shape[1] // 3
    heads = c // HEAD_DIM
    blk = min(SB_BLOCK, t)

    def body(q_ref, k_ref, v_ref, o_ref, ct_ref, nb_ref):
        i = pl.program_id(1)
        sl = [slice(p * HEAD_DIM, (p + 1) * HEAD_DIM) for p in range(SB_PAIR)]
        qs = [_sb_scaled(q_ref[:, s]) for s in sl]
        rows = lax.broadcasted_iota(jnp.int32, (blk, blk), 0)
        cols = lax.broadcasted_iota(jnp.int32, (blk, blk), 1)
        below = rows > cols
        m_right = below.astype(BF16)

        def tile(carry, diagonal):
            s, acc, c_run = carry[0], carry[1:1 + SB_PAIR], carry[1 + SB_PAIR:]
            off = pl.multiple_of((i - s) * blk, blk)
            log_beta, l = _sb_logs(qs, [k_ref[pl.ds(off, blk), p] for p in sl], below if diagonal else None)
            right = [_split_dot(x, m_right) for x in l]
            a = [jnp.exp(lb + (c + r)) for lb, c, r in zip(log_beta, c_run, right)]
            if diagonal:
                a = [jnp.where(below, x, 0.0) for x in a]
            acc = [o + _dot(x.astype(BF16), v_ref[pl.ds(off, blk), p], NN) for o, x, p in zip(acc, a, sl)]
            return (s + 1, *acc, *[c + jnp.sum(x, axis=1, keepdims=True) for c, x in zip(c_run, l)])

        zeros = [jnp.zeros((blk, HEAD_DIM), F32)] * SB_PAIR + [jnp.zeros((blk, 1), F32)] * SB_PAIR
        out = lax.while_loop(lambda carry: _sb_alive(carry[0], i, functools.reduce(jnp.maximum, carry[1 + SB_PAIR:])),
                             lambda carry: tile(carry, False), tile((jnp.int32(0), *zeros), True))
        for p, s in enumerate(sl):
            o_ref[:, s] = out[1 + p].astype(BF16)
            ct_ref[:, s] = jnp.broadcast_to(out[1 + SB_PAIR + p], (blk, HEAD_DIM))
        nb_ref[...] = jnp.zeros(nb_ref.shape, F32) + out[0].astype(F32)

    pairs = heads // SB_PAIR
    qspec = pl.BlockSpec((blk, SB_PAIR * HEAD_DIM), lambda h, i: (i, h))
    return _pcall(body, (zc, zc, zc), name=name, grid=(pairs, t // blk),
                  in_specs=[qspec, pl.BlockSpec((t, SB_PAIR * HEAD_DIM), lambda h, i: (0, pairs + h)),
                            pl.BlockSpec((t, SB_PAIR * HEAD_DIM), lambda h, i: (0, 2 * pairs + h))],
                  out_specs=[qspec, qspec, qspec],
                  out_shape=[jax.ShapeDtypeStruct((t, c), BF16), jax.ShapeDtypeStruct((t, c), F32), jax.ShapeDtypeStruct((t, c), F32)],
                  sem=("parallel", "parallel"), comms=comms)


def _sb_bwd(zc, ctot, swept, do, name, comms=()):
    t = zc.shape[0]
    c = zc.shape[1] // 3
    heads = c // HEAD_DIM
    blk = min(SB_BLOCK, t)
    scale = HEAD_DIM ** -0.5

    def body(q_ref, k_ref, v_ref, ct_ref, nb_ref, do_ref, dq_ref, dk_ref, dv_ref):
        i = pl.program_id(1)

        @pl.when(i == 0)
        def _():
            dk_ref[...] = jnp.zeros_like(dk_ref)
            dv_ref[...] = jnp.zeros_like(dv_ref)

        ps = range(SB_PAIR)
        sl = [slice(p * HEAD_DIM, (p + 1) * HEAD_DIM) for p in ps]
        qs = [_sb_scaled(q_ref[:, s]) for s in sl]
        dov = [do_ref[:, s] for s in sl]
        c_tot = [ct_ref[:, p * HEAD_DIM:p * HEAD_DIM + 1] for p in ps]
        n_blocks = jnp.clip(jnp.max(nb_ref[0:8, :]).astype(jnp.int32), 1, i + 1)
        rows = lax.broadcasted_iota(jnp.int32, (blk, blk), 0)
        cols = lax.broadcasted_iota(jnp.int32, (blk, blk), 1)
        below = rows > cols
        m_upto = (rows <= cols).astype(BF16)
        m_left = (rows < cols).astype(BF16)

        def tile(j, carry, diagonal):
            dq, l_run, w_run = carry[:SB_PAIR], carry[SB_PAIR:2 * SB_PAIR], carry[2 * SB_PAIR:]
            off = pl.multiple_of(j * blk, blk)
            kj = [k_ref[pl.ds(off, blk), s] for s in sl]
            vj = [v_ref[pl.ds(off, blk), s] for s in sl]
            log_beta, l = _sb_logs(qs, kj, below if diagonal else None)
            d_a = [_dot(dov[p], vj[p], NT) for p in ps]
            upto = [_split_dot(x, m_upto) for x in l]
            a = [jnp.exp(log_beta[p] + (c_tot[p] - l_run[p] - upto[p])) for p in ps]
            if diagonal:
                a = [jnp.where(below, x, 0.0) for x in a]
            wgt = [a[p] * d_a[p] for p in ps]
            before = [w_run[p] + _split_dot(wgt[p], m_left) for p in ps]
            dz = [wgt[p] * jnp.exp(l[p]) - jnp.exp(log_beta[p]) * before[p] for p in ps]
            if diagonal:
                dz = [jnp.where(below, x, 0.0) for x in dz]
            dz16 = [x.astype(BF16) for x in dz]
            dk = [_dot(dz16[p], qs[p], TN) for p in ps]
            dv = [_dot(a[p].astype(BF16), dov[p], TN) for p in ps]
            dq = [dq[p] + _dot(dz16[p], kj[p], NN) for p in ps]
            for p in ps:
                dk_ref[pl.ds(off, blk), sl[p]] += dk[p]
                dv_ref[pl.ds(off, blk), sl[p]] += dv[p]
            return (*dq, *[l_run[p] + jnp.sum(l[p], axis=1, keepdims=True) for p in ps],
                    *[w_run[p] + jnp.sum(wgt[p], axis=1, keepdims=True) for p in ps])

        zeros = [jnp.zeros((blk, HEAD_DIM), F32)] * SB_PAIR + [jnp.zeros((blk, 1), F32)] * (2 * SB_PAIR)
        carry = lax.fori_loop(i + 1 - n_blocks, i, lambda j, carry: tile(j, carry, False), tuple(zeros))
        out = tile(i, carry, True)
        for p in ps:
            dq_ref[:, sl[p]] = out[p] * scale

    pairs = heads // SB_PAIR
    qspec = pl.BlockSpec((blk, SB_PAIR * HEAD_DIM), lambda h, i: (i, h))
    full = pl.BlockSpec((t, SB_PAIR * HEAD_DIM), lambda h, i: (0, h))
    return _pcall(body, (zc, zc, zc, ctot, swept, do), name=name, grid=(pairs, t // blk),
                  in_specs=[qspec, pl.BlockSpec((t, SB_PAIR * HEAD_DIM), lambda h, i: (0, pairs + h)),
                            pl.BlockSpec((t, SB_PAIR * HEAD_DIM), lambda h, i: (0, 2 * pairs + h)), qspec, qspec, qspec],
                  out_specs=[qspec, full, full], out_shape=[jax.ShapeDtypeStruct((t, c), F32)] * 3,
                  sem=("arbitrary", "arbitrary"), comms=comms)


def _concat_bf16(parts, name, comms=()):
    t, c = parts[0].shape

    def body(*refs):
        for k, r in enumerate(refs[:-1]):
            refs[-1][:, k * c:(k + 1) * c] = r[...].astype(BF16)

    res = _pcall(body, tuple(parts), name=name, grid=(t // NORM_ROWS,), in_specs=[_row_spec(c)] * len(parts),
                 out_specs=[_row_spec(c * len(parts))], out_shape=[jax.ShapeDtypeStruct((t, c * len(parts)), BF16)],
                 sem=("parallel",), comms=comms)
    return (res[0][0], res[1]) if comms else res[0]


KIND = {"ab_w_in": "col", "ab_w_out": "row", "sb_w_in": "col", "sb_w_out": "row",
        "ffn_w1_0": "col", "ffn_w1_1": "col", "ffn_w2_0": "row", "ffn_w2_1": "row"}
X_Y, DIAG, CHIPS = (2, 4), (6,), (2, 4, 6)


def _local_step(x, target, norms, sgu, big, bufs=None):
    g = {k: [v[l:l + 1] for l in range(2)] for k, v in norms.items()}
    ln_g, ln_b, sgu_w, sgu_b = sgu
    groups = sgu_w.shape[0]
    w16 = sgu_w.astype(BF16)
    bias_b = jnp.broadcast_to(sgu_b[:, :, None], (groups, CHUNK, CHUNK))
    big, dws, psum, dist = dict(big), {}, {}, bufs is not None
    pair, got = (dict(bufs[0]), dict(bufs[1])) if dist else ({}, {})

    def run(fn, *args, ops=(), **kw):
        if not dist or not ops:
            return fn(*args, **kw)
        make = {"gs": lambda k, p, *part: _GatherSend(big[k], KIND[k], p, *part), "gf": lambda k, p: _GatherFwd(big[k], KIND[k], p),
                "swap": lambda k, p: _PairSwap(dws[k], pair[k], KIND[k]), "chips": lambda k, p: _ChipScatter(psum[k], got[k], p)}
        out, rws = fn(*args, comms=[make[op[0]](*op[1:]) for op in ops], **kw)
        for (op, k, *_), r in zip(ops, rws):
            if op in ("gs", "gf"):
                big[k] = r[0]
            elif op == "swap":
                psum[k] = _pair_sum(dws[k], r[0], KIND[k], f"pair_sum_{k}")
            else:
                got[k] = r[0]
        return out

    h1_0 = _rms_fwd(x, g["pre_mix"][0], "rms_in")
    z0 = run(_matmul, h1_0, big["ab_w_in"], "nn", BF16, "ab_in", ops=[("gs", "ffn_w1_0", X_Y)])
    a_out = run(_sgu_fwd, z0, ln_g, ln_b, w16, bias_b, "sgu_fwd", ops=[("gf", "ffn_w1_0", X_Y), ("gs", "ab_w_out", CHIPS)])
    branches = [run(_dil_fwd, z0, 1, "dil_fwd_1", ops=[("gs", "ffn_w1_0", DIAG, (0, 2)), ("gf", "ab_w_out", CHIPS)]),
                run(_dil_fwd, z0, 4, "dil_fwd_4", ops=[("gs", "ffn_w1_0", DIAG, (1, 2))]),
                run(_dil_fwd, z0, 16, "dil_fwd_16", ops=[("gf", "ffn_w1_0", DIAG), ("gs", "ffn_w2_0", X_Y, (0, 2))])]
    ab, ltot = run(_dil_merge, a_out, [b[0] for b in branches], [b[1] for b in branches], "dil_merge",
                   ops=[("gs", "ffn_w2_0", X_Y, (1, 2))])
    y_0 = run(_matmul, ab, big["ab_w_out"], "nn", F32, "ab_out", ops=[("gs", "ffn_w2_0", DIAG, (0, 2))])
    x1, h2_0 = run(_post_pre_fwd, y_0, g["post_mix"][0], x, g["pre_ffn"][0], "norm_mix0", ops=[("gs", "ffn_w2_0", DIAG, (1, 2))])
    r_0 = run(_matmul, h2_0, big["ffn_w1_0"], "nn", BF16, "ffn_up_0", relu_out=True,
              ops=[("gf", "ffn_w2_0", CHIPS), ("gs", "sb_w_in", CHIPS)])
    y2_0 = run(_matmul, r_0, big["ffn_w2_0"], "nn", F32, "ffn_down_0", a_square=True,
               ops=[("gf", "sb_w_in", CHIPS), ("gs", "sb_w_out", CHIPS), ("gs", "ffn_w1_1", X_Y)])
    x2, h1_1 = run(_post_pre_fwd, y2_0, g["post_ffn"][0], x1, g["pre_mix"][1], "norm_ffn0",
                   ops=[("gf", "ffn_w1_1", X_Y), ("gf", "sb_w_out", CHIPS)])
    zc = run(_matmul, h1_1, big["sb_w_in"], "nn", BF16, "sb_in", ops=[("gs", "ffn_w1_1", DIAG)])
    o_sb, ct_sb, nb_sb = run(_sb_fwd, zc, "sb_fwd", ops=[("gf", "ffn_w1_1", DIAG), ("gs", "ffn_w2_1", CHIPS)])
    y_1 = run(_matmul, o_sb, big["sb_w_out"], "nn", F32, "sb_out", ops=[("gf", "ffn_w2_1", CHIPS)])
    x3, h2_1 = _post_pre_fwd(y_1, g["post_mix"][1], x2, g["pre_ffn"][1], "norm_mix1")
    r_1 = _matmul(h2_1, big["ffn_w1_1"], "nn", BF16, "ffn_up_1", relu_out=True)
    y2_1 = _matmul(r_1, big["ffn_w2_1"], "nn", F32, "ffn_down_1", a_square=True)
    loss, dx4, dy2_1, dg_post_ffn1 = _final_fwd_bwd(y2_1, g["post_ffn"][1], x3, target, "loss")

    da = _matmul(dy2_1, big["ffn_w2_1"], "nt", BF16, "ffn_da_1", mul2=r_1)
    dws["ffn_w2_1"] = _matmul(r_1, dy2_1, "tn", BF16, "ffn_dw2_1", a_square=True)
    dh2 = run(_matmul, da, big["ffn_w1_1"], "nt", F32, "ffn_dh_1", ops=[("swap", "ffn_w2_1", None)])
    dws["ffn_w1_1"] = run(_matmul, h2_1, da, "tn", BF16, "ffn_dw1_1", ops=[("chips", "ffn_w2_1", X_Y)])
    dx3, dy_1, dg_pre_ffn1, dg_post_mix1 = run(_pre_post_bwd, x3, g["pre_ffn"][1], dh2, dx4, y_1, g["post_mix"][1], "norm_bwd_mix1",
                                               ops=[("swap", "ffn_w1_1", None)])
    do_sb = _matmul(dy_1, big["sb_w_out"], "nt", BF16, "sb_out_dx")
    dws["sb_w_out"] = _matmul(o_sb, dy_1, "tn", BF16, "sb_out_dw")
    dqkv = run(_sb_bwd, zc, ct_sb, nb_sb, do_sb, "sb_bwd",
               ops=[("chips", "ffn_w2_1", DIAG), ("chips", "ffn_w1_1", CHIPS), ("swap", "sb_w_out", None)])
    dzc = run(_concat_bf16, dqkv, "sb_dz", ops=[("chips", "sb_w_out", X_Y)])
    dh1 = run(_matmul, dzc, big["sb_w_in"], "nt", F32, "sb_in_dx", ops=[("chips", "sb_w_out", DIAG)])
    dws["sb_w_in"] = _matmul(h1_1, dzc, "tn", BF16, "sb_in_dw")
    dx2, dy2_0, dg_pre_mix1, dg_post_ffn0 = run(_pre_post_bwd, x2, g["pre_mix"][1], dh1, dx3, y2_0, g["post_ffn"][0], "norm_bwd_ffn0",
                                                ops=[("swap", "sb_w_in", None)])
    da = run(_matmul, dy2_0, big["ffn_w2_0"], "nt", BF16, "ffn_da_0", mul2=r_0, ops=[("chips", "sb_w_in", X_Y)])
    dws["ffn_w2_0"] = run(_matmul, r_0, dy2_0, "tn", BF16, "ffn_dw2_0", a_square=True, ops=[("chips", "sb_w_in", DIAG)])
    dws["ffn_w1_0"] = run(_matmul, h2_0, da, "tn", BF16, "ffn_dw1_0", ops=[("swap", "ffn_w2_0", None)])
    dh2 = run(_matmul, da, big["ffn_w1_0"], "nt", F32, "ffn_dh_0", ops=[("chips", "ffn_w2_0", X_Y), ("swap", "ffn_w1_0", None)])
    dx1, dy_0, dg_pre_ffn0, dg_post_mix0 = _pre_post_bwd(x1, g["pre_ffn"][0], dh2, dx2, y_0, g["post_mix"][0], "norm_bwd_mix0")
    dab = _matmul(dy_0, big["ab_w_out"], "nt", BF16, "ab_out_dx")
    dws["ab_w_out"] = _matmul(ab, dy_0, "tn", BF16, "ab_out_dw")
    duv, d_ln_g, d_ln_b, d_sgu_w, d_sgu_b = run(_sgu_bwd, z0, dab, ln_g, ln_b, w16, bias_b, "sgu_bwd", ops=[("chips", "ffn_w2_0", DIAG)])
    delta = _dil_delta(ab, dab, "dil_delta")
    parts = [run(_dil_bwd, z0, dab, ltot, delta, 1, "dil_bwd_1", ops=[("chips", "ffn_w1_0", X_Y), ("swap", "ab_w_out", None)]),
             run(_dil_bwd, z0, dab, ltot, delta, 4, "dil_bwd_4", ops=[("chips", "ffn_w1_0", DIAG)]),
             run(_dil_bwd, z0, dab, ltot, delta, 16, "dil_bwd_16", ops=[("chips", "ab_w_out", CHIPS)])]
    dz0 = _dz_assemble(duv, parts, "dz_assemble")
    dws["ab_w_in"] = _matmul(h1_0, dz0, "tn", BF16, "ab_in_dw")
    dh1 = run(_matmul, dz0, big["ab_w_in"], "nt", F32, "ab_in_dx", ops=[("swap", "ab_w_in", None)])
    grad_x, dg_pre_mix0 = run(_pre_post_bwd, x, g["pre_mix"][0], dh1, dx1, None, None, "norm_bwd_in", ops=[("chips", "ab_w_in", X_Y)])

    d_norms = {
        "pre_mix": jnp.concatenate([dg_pre_mix0, dg_pre_mix1]), "post_mix": jnp.concatenate([dg_post_mix0, dg_post_mix1]),
        "pre_ffn": jnp.concatenate([dg_pre_ffn0, dg_pre_ffn1]), "post_ffn": jnp.concatenate([dg_post_ffn0, dg_post_ffn1]),
    }
    return loss, grad_x, d_norms, (d_ln_g, d_ln_b, d_sgu_w, d_sgu_b), (psum, got) if dist else dws


def _to_bf16_full(w, layer, kind, name):
    _, rows, cols = w.shape
    tr = _tile(rows, 512)
    nblk = rows // tr
    full = (rows, 4 * cols) if kind == "col" else (4 * rows, cols)

    def body(w_ref, o_ref):
        o_ref[...] = w_ref[...].astype(BF16)

    def place(i):
        mine = 2 * lax.axis_index("x") + lax.axis_index("y")
        return (i, mine) if kind == "col" else (mine * nblk + i, 0)

    return pl.pallas_call(
        body, name=name, grid=(nblk,), in_specs=[pl.BlockSpec((None, tr, cols), lambda i: (layer, i, 0))],
        out_specs=pl.BlockSpec((tr, cols), place), out_shape=jax.ShapeDtypeStruct(full, BF16), compiler_params=_params("parallel"),
    )(w)


def _pair_sum(dw16, pair, kind, name):
    rh, cs = _half_shape(dw16.shape, kind)
    tr = _tile(rh, 256)
    nblk = rh // tr

    def body(dw_ref, pair_ref, o_ref):
        o_ref[...] = (dw_ref[...].astype(F32) + pair_ref[...].astype(F32)).astype(BF16)

    def own(s, i):
        c = lax.axis_index("c")
        return (c * nblk + i, s) if kind == "col" else ((2 * s + c) * nblk + i, 0)

    spec3 = pl.BlockSpec((None, tr, cs), lambda s, i: (s, i, 0))
    return pl.pallas_call(
        body, name=name, grid=(4, nblk), in_specs=[pl.BlockSpec((tr, cs), own), spec3], out_specs=spec3,
        out_shape=jax.ShapeDtypeStruct((4, rh, cs), BF16), compiler_params=_params("parallel", "parallel"),
    )(dw16, pair)


def _owner_sum(psum, got, buf, layer, name, comms=()):
    _, rh, cs = psum.shape
    tr = _tile(rh, 256)

    def body(p_ref, got_ref, buf_ref, o_ref):
        tot = p_ref[...].astype(F32)
        for j in range(3):
            tot = tot + got_ref[j].astype(F32)
        o_ref[...] = tot

    res = _pcall(
        body, (psum, got, buf), name=name, grid=(rh // tr,),
        in_specs=[pl.BlockSpec((None, tr, cs), lambda i: (2 * lax.axis_index("x") + lax.axis_index("y"), i, 0)),
                  pl.BlockSpec((3, tr, cs), lambda i: (0, i, 0)), ANY],
        out_specs=[pl.BlockSpec((None, None, tr, cs), lambda i: (layer, lax.axis_index("c"), i, 0))],
        out_shape=[jax.ShapeDtypeStruct(buf.shape, F32)], sem=("parallel",), comms=comms, aliases={2: 0})
    return (res[0][0], res[1]) if comms else res[0]


def _adamw_math(w, g, m, v):
    m = ADAM_B1 * m + (1.0 - ADAM_B1) * g
    v = ADAM_B2 * v + (1.0 - ADAM_B2) * (g * g)
    m_hat = m / (1.0 - ADAM_B1 ** ADAM_STEP)
    v_hat = v / (1.0 - ADAM_B2 ** ADAM_STEP)
    return -ADAM_LR * (m_hat / (jnp.sqrt(v_hat) + ADAM_EPS) + ADAM_WD * w), m, v


def _adamw(w, g, m, v, name):
    layers, rows, cols = w.shape
    tr = _tile(rows, 256)

    def body(w_ref, g_ref, m_ref, v_ref, go_ref, d_ref, mo_ref, vo_ref):
        g = g_ref[...]
        go_ref[...] = g
        d_ref[...], mo_ref[...], vo_ref[...] = _adamw_math(w_ref[...], g, m_ref[...], v_ref[...])

    spec = pl.BlockSpec((None, tr, cols), lambda l, i: (l, i, 0))
    return _pcall(body, (w, g, m, v), name=name, grid=(layers, rows // tr), in_specs=[spec] * 4, out_specs=[spec] * 4,
                  out_shape=[jax.ShapeDtypeStruct(w.shape, F32)] * 4, sem=("parallel", "parallel"))


def _pack(arrays):
    flat = jnp.concatenate([a.reshape(-1) for a in arrays])
    pad = (-flat.shape[0]) % 1024
    return jnp.pad(flat, (0, pad)).reshape(-1, 128)


def _unpack(packed, like):
    flat = packed.reshape(-1)
    out, off = [], 0
    for a in like:
        out.append(flat[off:off + a.size].reshape(a.shape))
        off += a.size
    return out


class _SmallGather:
    n_sems = 7

    def __init__(self, g, parts):
        self.ro, self.rw = [g], [parts]

    def start(self, ro, rw, send, recv):
        x, y, c, _ = _place()
        for j in range(1, 8):
            _remote(ro[0], rw[0].at[4 * x + 2 * y + c], send(j - 1), recv(j - 1), _flip(x, y, c, j)).start()

    def finish(self, ro, rw, send, recv):
        x, y, c, _ = _place()
        for j in range(1, 8):
            px, py, pc = _flip(x, y, c, j)
            slot = rw[0].at[4 * px + 2 * py + pc]
            cp = _remote(slot, slot, send(j - 1), recv(j - 1), (x, y, c))
            cp.wait_recv()
            cp.wait_send()


def _small_update(own, parts, w, m, v, name):
    rows = w.shape[0]

    def body(own_ref, p_ref, w_ref, m_ref, v_ref, g_ref, d_ref, mo_ref, vo_ref):
        me = 4 * lax.axis_index("x") + 2 * lax.axis_index("y") + lax.axis_index("c")
        g = jnp.where(me == 0, own_ref[...], p_ref[0])
        for k in range(1, 8):
            g = g + jnp.where(me == k, own_ref[...], p_ref[k])
        g_ref[...] = g
        d_ref[...], mo_ref[...], vo_ref[...] = _adamw_math(w_ref[...], g, m_ref[...], v_ref[...])

    return pl.pallas_call(body, name=name, out_shape=[jax.ShapeDtypeStruct((rows, 128), F32)] * 4,
                          compiler_params=_params())(own, parts, w, m, v)


SMALL = ("norm_pre_mix", "norm_post_mix", "norm_pre_ffn", "norm_post_ffn", "sgu_ln_g", "sgu_ln_b", "sgu_w", "sgu_b")
BIG = (("ab_w_in", ("ab_w_in",)), ("ab_w_out", ("ab_w_out",)), ("sb_w_in", ("sb_w_in",)), ("sb_w_out", ("sb_w_out",)),
       ("ffn_w1", ("ffn_w1_0", "ffn_w1_1")), ("ffn_w2", ("ffn_w2_0", "ffn_w2_1")))
WEIGHTS = ("norm_pre_mix", "norm_post_mix", "norm_pre_ffn", "norm_post_ffn", "ab_w_in", "sgu_ln_g", "sgu_ln_b", "sgu_w", "sgu_b",
           "ab_w_out", "sb_w_in", "sb_w_out", "ffn_w1", "ffn_w2")


def kernel(x, norm_pre_mix, norm_post_mix, norm_pre_ffn, norm_post_ffn, ab_w_in, sgu_ln_g, sgu_ln_b, sgu_w, sgu_b, ab_w_out, sb_w_in, sb_w_out, ffn_w1, ffn_w2, loss_target, m_norm_pre_mix, m_norm_post_mix, m_norm_pre_ffn, m_norm_post_ffn, m_ab_w_in, m_sgu_ln_g, m_sgu_ln_b, m_sgu_w, m_sgu_b, m_ab_w_out, m_sb_w_in, m_sb_w_out, m_ffn_w1, m_ffn_w2, v_norm_pre_mix, v_norm_post_mix, v_norm_pre_ffn, v_norm_post_ffn, v_ab_w_in, v_sgu_ln_g, v_sgu_ln_b, v_sgu_w, v_sgu_b, v_ab_w_out, v_sb_w_in, v_sb_w_out, v_ffn_w1, v_ffn_w2):
    w = dict(norm_pre_mix=norm_pre_mix, norm_post_mix=norm_post_mix, norm_pre_ffn=norm_pre_ffn, norm_post_ffn=norm_post_ffn,
             ab_w_in=ab_w_in, sgu_ln_g=sgu_ln_g, sgu_ln_b=sgu_ln_b, sgu_w=sgu_w, sgu_b=sgu_b, ab_w_out=ab_w_out, sb_w_in=sb_w_in,
             sb_w_out=sb_w_out, ffn_w1=ffn_w1, ffn_w2=ffn_w2)
    m = dict(norm_pre_mix=m_norm_pre_mix, norm_post_mix=m_norm_post_mix, norm_pre_ffn=m_norm_pre_ffn, norm_post_ffn=m_norm_post_ffn,
             ab_w_in=m_ab_w_in, sgu_ln_g=m_sgu_ln_g, sgu_ln_b=m_sgu_ln_b, sgu_w=m_sgu_w, sgu_b=m_sgu_b, ab_w_out=m_ab_w_out,
             sb_w_in=m_sb_w_in, sb_w_out=m_sb_w_out, ffn_w1=m_ffn_w1, ffn_w2=m_ffn_w2)
    v = dict(norm_pre_mix=v_norm_pre_mix, norm_post_mix=v_norm_post_mix, norm_pre_ffn=v_norm_pre_ffn, norm_post_ffn=v_norm_post_ffn,
             ab_w_in=v_ab_w_in, sgu_ln_g=v_sgu_ln_g, sgu_ln_b=v_sgu_ln_b, sgu_w=v_sgu_w, sgu_b=v_sgu_b, ab_w_out=v_ab_w_out,
             sb_w_in=v_sb_w_in, sb_w_out=v_sb_w_out, ffn_w1=v_ffn_w1, ffn_w2=v_ffn_w2)
    big, pair, got = {}, {}, {}
    for name, keys in BIG:
        for layer, key in enumerate(keys):
            big[key] = _to_bf16_full(w[name], layer, KIND[key], f"bf16_{key}")
            half = _half_shape(big[key].shape, KIND[key])
            pair[key], got[key] = lax.empty((4,) + half, BF16), lax.empty((3,) + half, BF16)
    big["ab_w_in"] = _comm_call([_Gather(big["ab_w_in"], KIND["ab_w_in"])], "gather_first")[0][0]

    norms = {k: w["norm_" + k] for k in ("pre_mix", "post_mix", "pre_ffn", "post_ffn")}
    sgu = (sgu_ln_g, sgu_ln_b, sgu_w[0], sgu_b[0])
    loss_blk, grad_x, d_norms, d_sgu, (psum, got) = _local_step(x[0], loss_target[0], norms, sgu, big, (pair, got))
    loss = lax.psum(loss_blk[0, 0], ("x", "y", "c"))

    grads, deltas, new_m, new_v = {}, {}, {}, {}
    keys_of = dict(BIG)
    small_g = _pack([d_norms["pre_mix"], d_norms["post_mix"], d_norms["pre_ffn"], d_norms["post_ffn"],
                     d_sgu[0], d_sgu[1], d_sgu[2][None], d_sgu[3][None]])
    parts = None
    bufs, pending = {}, None
    for name in ("ffn_w2", "ffn_w1", "sb_w_in", "sb_w_out", "ab_w_out"):
        buf = lax.empty((len(keys_of[name]), 2) + psum[keys_of[name][0]].shape[1:], F32)
        for layer, key in enumerate(keys_of[name]):
            if pending is not None:
                buf, rws = _owner_sum(psum[key], got[key], buf, layer, f"sum_{key}", comms=[_Join([bufs[pending]])])
                bufs[pending], pending = rws[0][0], None
            elif parts is None:
                buf, rws = _owner_sum(psum[key], got[key], buf, layer, f"sum_{key}",
                                      comms=[_SmallGather(small_g, lax.empty((8,) + small_g.shape, F32))])
                parts = rws[0][0]
            else:
                buf = _owner_sum(psum[key], got[key], buf, layer, f"sum_{key}")
        bufs[name], pending = buf, name

    rws = _comm_call([_Join([bufs["ab_w_out"]]), _ChipScatter(psum["ab_w_in"], got["ab_w_in"], DIAG)], "tail_comm")
    bufs["ab_w_out"], got["ab_w_in"] = rws[0][0], rws[1][0]
    bufs["ab_w_in"] = _owner_sum(psum["ab_w_in"], got["ab_w_in"], lax.empty((1, 2) + psum["ab_w_in"].shape[1:], F32), 0, "sum_ab_w_in")
    bufs["ab_w_in"] = _comm_call([_Join([bufs["ab_w_in"]])], "join_last")[0][0]
    for name, _ in BIG:
        grads[name], deltas[name], new_m[name], new_v[name] = _adamw(w[name], bufs[name].reshape(w[name].shape), m[name], v[name], f"adamw_{name}")

    outs = _small_update(small_g, parts, _pack([w[k] for k in SMALL]), _pack([m[k] for k in SMALL]), _pack([v[k] for k in SMALL]), "small_update")
    like = [w[k] for k in SMALL]
    for dst, packed in zip((grads, deltas, new_m, new_v), outs):
        for k, a in zip(SMALL, _unpack(packed, like)):
            dst[k] = a

    return (loss, grad_x[None], *[grads[k] for k in WEIGHTS], *[deltas[k] for k in WEIGHTS],
            *[new_m[k] for k in WEIGHTS], *[new_v[k] for k in WEIGHTS])
```

```python
import functools

import jax
import jax.numpy as jnp
from jax import lax
from jax.experimental import pallas as pl
from jax.experimental.pallas import tpu as pltpu

F32 = jnp.float32
BF16 = jnp.bfloat16
MESH = pl.DeviceIdType.MESH

HEAD_DIM = 128
CHUNK = 128
DILATIONS = (1, 4, 16)
SB_BLOCK = 256
RMS_EPS = 1e-6
LN_EPS = 1e-5
ADAM_LR, ADAM_B1, ADAM_B2, ADAM_EPS, ADAM_WD, ADAM_STEP = 0.001, 0.9, 0.999, 1e-08, 0.01, 10
NEG = -1e30
V7X_VMEM_LIMIT = 48 * 1024 * 1024
ANY = pl.BlockSpec(memory_space=pl.ANY)


def _params(*sem):
    return pltpu.CompilerParams(dimension_semantics=sem if sem else None, vmem_limit_bytes=V7X_VMEM_LIMIT)


def _tile(n, pref):
    if n <= pref:
        return n
    t = pref
    while n % t:
        t -= 128
    return t


def _dot(a, b, dims):
    return lax.dot_general(a, b, (dims, ((), ())), preferred_element_type=F32)


NN = ((1,), (0,))
NT = ((1,), (1,))
TN = ((0,), (0,))


def _place():
    x, y, c = lax.axis_index("x"), lax.axis_index("y"), lax.axis_index("c")
    return x, y, c, 2 * x + y


def _flip(x, y, c, j):
    return (1 - x if j & 4 else x), (1 - y if j & 2 else y), (1 - c if j & 1 else c)


def _half_shape(full_shape, kind):
    rows, cols = full_shape
    return (rows // 2, cols // 4) if kind == "col" else (rows // 8, cols)


def _half(ref, kind, s, h):
    rh, cs = _half_shape(ref.shape, kind)
    if kind == "col":
        return ref.at[pl.ds(h * rh, rh), pl.ds(s * cs, cs)]
    return ref.at[pl.ds((2 * s + h) * rh, rh), :]


def _remote(src, dst, send, recv, to):
    return pltpu.make_async_remote_copy(src_ref=src, dst_ref=dst, send_sem=send, recv_sem=recv, device_id=to, device_id_type=MESH)


class _Gather:
    n_sems = 6

    def __init__(self, full, kind):
        self.ro, self.rw, self.kind = [], [full], kind

    def start(self, ro, rw, send, recv):
        x, y, c, mine = _place()
        own = _half(rw[0], self.kind, mine, c)
        for k, j in enumerate((2, 4, 6)):
            px, py, _ = _flip(x, y, c, j)
            _remote(own, own, send(k), recv(k), (px, py, c)).start()

    def finish(self, ro, rw, send, recv):
        x, y, c, mine = _place()
        own = _half(rw[0], self.kind, mine, c)
        for k, j in enumerate((2, 4, 6)):
            px, py, _ = _flip(x, y, c, j)
            got = _half(rw[0], self.kind, 2 * px + py, c)
            _remote(got, got, send(k), recv(k), (x, y, c)).wait_recv()
            _remote(got, got, send(3 + k), recv(3 + k), (x, y, 1 - c)).start()
        for k, j in enumerate((2, 4, 6)):
            px, py, _ = _flip(x, y, c, j)
            got = _half(rw[0], self.kind, 2 * px + py, 1 - c)
            _remote(got, got, send(3 + k), recv(3 + k), (x, y, c)).wait_recv()
        for k in range(6):
            _remote(own, own, send(k), recv(k), (x, y, c)).wait_send()


class _GatherSend:
    def __init__(self, full, kind, patterns, part=(0, 1)):
        self.ro, self.rw, self.kind, self.patterns, self.part, self.n_sems = [], [full], kind, patterns, part, len(patterns)

    def _rows(self, half):
        i, n = self.part
        rows = half.shape[0] // n
        return half.at[pl.ds(i * rows, rows), :]

    def start(self, ro, rw, send, recv):
        x, y, c, mine = _place()
        own = self._rows(_half(rw[0], self.kind, mine, c))
        for k, j in enumerate(self.patterns):
            px, py, _ = _flip(x, y, c, j)
            _remote(own, own, send(k), recv(k), (px, py, c)).start()

    def finish(self, ro, rw, send, recv):
        x, y, c, _ = _place()
        for k, j in enumerate(self.patterns):
            px, py, _ = _flip(x, y, c, j)
            got = self._rows(_half(rw[0], self.kind, 2 * px + py, c))
            cp = _remote(got, got, send(k), recv(k), (x, y, c))
            cp.wait_recv()
            cp.wait_send()


class _GatherFwd:
    def __init__(self, full, kind, patterns):
        self.ro, self.rw, self.kind, self.patterns, self.n_sems = [], [full], kind, patterns, len(patterns)

    def start(self, ro, rw, send, recv):
        x, y, c, _ = _place()
        for k, j in enumerate(self.patterns):
            px, py, _ = _flip(x, y, c, j)
            got = _half(rw[0], self.kind, 2 * px + py, c)
            _remote(got, got, send(k), recv(k), (x, y, 1 - c)).start()

    def finish(self, ro, rw, send, recv):
        x, y, c, _ = _place()
        for k, j in enumerate(self.patterns):
            px, py, _ = _flip(x, y, c, j)
            got = _half(rw[0], self.kind, 2 * px + py, 1 - c)
            cp = _remote(got, got, send(k), recv(k), (x, y, c))
            cp.wait_recv()
            cp.wait_send()


class _PairSwap:
    n_sems = 4

    def __init__(self, dw16, pair, kind):
        self.ro, self.rw, self.kind = [dw16], [pair], kind

    def start(self, ro, rw, send, recv):
        x, y, c, _ = _place()
        for s in range(4):
            _remote(_half(ro[0], self.kind, s, 1 - c), rw[0].at[s], send(s), recv(s), (x, y, 1 - c)).start()

    def finish(self, ro, rw, send, recv):
        x, y, c, _ = _place()
        for s in range(4):
            cp = _remote(rw[0].at[s], rw[0].at[s], send(s), recv(s), (x, y, c))
            cp.wait_recv()
            cp.wait_send()


class _ChipScatter:
    def __init__(self, psum, got, patterns, part=(0, 1)):
        self.ro, self.rw, self.patterns, self.part, self.n_sems = [psum], [got], patterns, part, len(patterns)

    def _rows(self, ref, slot):
        i, n = self.part
        rows = ref.shape[1] // n
        return ref.at[slot, pl.ds(i * rows, rows), :]

    def start(self, ro, rw, send, recv):
        x, y, c, _ = _place()
        for k, j in enumerate(self.patterns):
            px, py, _ = _flip(x, y, c, j)
            _remote(self._rows(ro[0], 2 * px + py), self._rows(rw[0], j // 2 - 1), send(k), recv(k), (px, py, c)).start()

    def finish(self, ro, rw, send, recv):
        x, y, c, _ = _place()
        for k, j in enumerate(self.patterns):
            slot = self._rows(rw[0], j // 2 - 1)
            cp = _remote(slot, slot, send(k), recv(k), (x, y, c))
            cp.wait_recv()
            cp.wait_send()


class _Join:
    def __init__(self, bufs):
        self.ro, self.rw, self.n_sems = [], list(bufs), sum(b.shape[0] for b in bufs)

    def _copies(self, rw, send, recv, slot):
        x, y, c, _ = _place()
        k = 0
        for ref in rw:
            for l in range(ref.shape[0]):
                yield _remote(ref.at[l, c], ref.at[l, slot(c)], send(k), recv(k), (x, y, 1 - c))
                k += 1

    def start(self, ro, rw, send, recv):
        for cp in self._copies(rw, send, recv, lambda c: c):
            cp.start()

    def finish(self, ro, rw, send, recv):
        for cp in self._copies(rw, send, recv, lambda c: 1 - c):
            cp.wait_recv()
        for cp in self._copies(rw, send, recv, lambda c: c):
            cp.wait_send()


def _comm_layout(comms):
    ro = [a for c in comms for a in c.ro]
    rw = [a for c in comms for a in c.rw]
    return ro, rw, sum(c.n_sems for c in comms)


def _comm_each(comms, method, ro_refs, rw_refs, send, recv):
    i_ro = i_rw = i_sem = 0
    for c in comms:
        getattr(c, method)(ro_refs[i_ro:i_ro + len(c.ro)], rw_refs[i_rw:i_rw + len(c.rw)],
                           lambda k, b=i_sem: send.at[b + k], lambda k, b=i_sem: recv.at[b + k])
        i_ro, i_rw, i_sem = i_ro + len(c.ro), i_rw + len(c.rw), i_sem + c.n_sems


def _split_results(comms, rws):
    out, i = [], 0
    for c in comms:
        out.append(list(rws[i:i + len(c.rw)]))
        i += len(c.rw)
    return out


def _comm_call(comms, name):
    ro, rw, n_sems = _comm_layout(comms)

    def body(*refs):
        ro_refs = refs[:len(ro)]
        rw_refs = refs[len(ro) + len(rw):len(ro) + 2 * len(rw)]
        send, recv = refs[len(ro) + 2 * len(rw):]
        _comm_each(comms, "start", ro_refs, rw_refs, send, recv)
        _comm_each(comms, "finish", ro_refs, rw_refs, send, recv)

    rws = pl.pallas_call(
        body, name=name, in_specs=[ANY] * (len(ro) + len(rw)), out_specs=[ANY] * len(rw),
        out_shape=[jax.ShapeDtypeStruct(a.shape, a.dtype) for a in rw],
        input_output_aliases={len(ro) + k: k for k in range(len(rw))},
        scratch_shapes=[pltpu.SemaphoreType.DMA((n_sems,)), pltpu.SemaphoreType.DMA((n_sems,))],
    )(*ro, *rw)
    return _split_results(comms, rws)


def _pcall(body, args, *, name, grid, in_specs, out_specs, out_shape, scratch=(), sem=(), comms=(), aliases=None):
    n_in, n_out, n_scr = len(in_specs), len(out_specs), len(scratch)
    aliases = dict(aliases or {})
    if not comms:
        return pl.pallas_call(body, name=name, grid=grid, in_specs=list(in_specs), out_specs=list(out_specs),
                              out_shape=list(out_shape), scratch_shapes=list(scratch), input_output_aliases=aliases,
                              compiler_params=_params(*sem))(*args)
    ro, rw, n_sems = _comm_layout(comms)

    def carrier(*refs):
        ins = refs[:n_in]
        ro_refs = refs[n_in:n_in + len(ro)]
        o0 = n_in + len(ro) + len(rw)
        outs = refs[o0:o0 + n_out]
        rw_refs = refs[o0 + n_out:o0 + n_out + len(rw)]
        s0 = o0 + n_out + len(rw)
        send, recv = refs[s0 + n_scr], refs[s0 + n_scr + 1]
        ids = [pl.program_id(a) for a in range(len(grid))]
        first = functools.reduce(jnp.logical_and, [i == 0 for i in ids])
        last = functools.reduce(jnp.logical_and, [i == g - 1 for i, g in zip(ids, grid)])

        @pl.when(first)
        def _():
            _comm_each(comms, "start", ro_refs, rw_refs, send, recv)

        body(*ins, *outs, *refs[s0:s0 + n_scr])

        @pl.when(last)
        def _():
            _comm_each(comms, "finish", ro_refs, rw_refs, send, recv)

    res = pl.pallas_call(
        carrier, name=name, grid=grid, in_specs=list(in_specs) + [ANY] * (len(ro) + len(rw)),
        out_specs=list(out_specs) + [ANY] * len(rw),
        out_shape=list(out_shape) + [jax.ShapeDtypeStruct(a.shape, a.dtype) for a in rw],
        input_output_aliases={**aliases, **{n_in + len(ro) + k: n_out + k for k in range(len(rw))}},
        scratch_shapes=list(scratch) + [pltpu.SemaphoreType.DMA((n_sems,)), pltpu.SemaphoreType.DMA((n_sems,))],
        compiler_params=_params(*["arbitrary"] * len(grid)),
    )(*args, *ro, *rw)
    return list(res[:n_out]), _split_results(comms, res[n_out:])


def _matmul(a, b, mode, out_dtype, name, a_square=False, relu_out=False, mul2=None, comms=()):
    if mode == "nn":
        (m, k), n = a.shape, b.shape[1]
    elif mode == "nt":
        (m, k), n = a.shape, b.shape[0]
    else:
        (k, m), n = a.shape, b.shape[1]
    tm, tn, tk = _tile(m, 1024), _tile(n, 1024), _tile(k, 2048)
    nk = k // tk
    dims = {"nn": NN, "nt": NT, "tn": TN}[mode]
    a_spec = pl.BlockSpec((tk, tm), lambda i, j, kk: (kk, i)) if mode == "tn" else pl.BlockSpec((tm, tk), lambda i, j, kk: (i, kk))
    b_spec = pl.BlockSpec((tn, tk), lambda i, j, kk: (j, kk)) if mode == "nt" else pl.BlockSpec((tk, tn), lambda i, j, kk: (kk, j))
    o_spec = pl.BlockSpec((tm, tn), lambda i, j, kk: (i, j))

    def body(a_ref, b_ref, *rest):
        m_ref = None if mul2 is None else rest[0]
        o_ref = rest[0 if mul2 is None else 1]
        kk = pl.program_id(2)

        def partial():
            av = a_ref[...]
            if a_square:
                av = av * av
            return _dot(av, b_ref[...], dims)

        def finish(r):
            if relu_out:
                r = jnp.maximum(r, 0.0)
            if mul2 is not None:
                r = r * (2.0 * m_ref[...].astype(F32))
            o_ref[...] = r.astype(out_dtype)

        if nk == 1:
            finish(partial())
            return
        acc_ref = rest[-1]

        @pl.when(kk == 0)
        def _():
            acc_ref[...] = partial()

        @pl.when(kk > 0)
        def _():
            acc_ref[...] += partial()

        @pl.when(kk == nk - 1)
        def _():
            finish(acc_ref[...])

    args = (a, b) if mul2 is None else (a, b, mul2)
    specs = [a_spec, b_spec] + ([] if mul2 is None else [o_spec])
    res = _pcall(body, args, name=name, grid=(m // tm, n // tn, nk), in_specs=specs, out_specs=[o_spec],
                 out_shape=[jax.ShapeDtypeStruct((m, n), out_dtype)], scratch=[pltpu.VMEM((tm, tn), F32)] if nk > 1 else [],
                 sem=("parallel", "parallel", "arbitrary"), comms=comms)
    return (res[0][0], res[1]) if comms else res[0]


NORM_ROWS = 256


def _rms(x, g):
    rstd = lax.rsqrt(jnp.mean(x * x, axis=-1, keepdims=True) + RMS_EPS)
    n = x * rstd
    return n * g, n, rstd


def _rms_bwd(n, rstd, g, dout):
    dn = dout * g
    return rstd * (dn - n * jnp.mean(dn * n, axis=-1, keepdims=True))


def _row_spec(d):
    return pl.BlockSpec((NORM_ROWS, d), lambda i: (i, 0))


def _vec_spec(d):
    return pl.BlockSpec((1, d), lambda i: (0, 0))


def _accumulate(ref, val):
    @pl.when(pl.program_id(0) == 0)
    def _():
        ref[...] = jnp.zeros_like(ref)

    ref[...] += val


def _rms_fwd(x, g, name):
    t, d = x.shape

    def body(x_ref, g_ref, h_ref):
        h_ref[...] = _rms(x_ref[...], g_ref[...])[0].astype(BF16)

    return pl.pallas_call(
        body, name=name, grid=(t // NORM_ROWS,), in_specs=[_row_spec(d), _vec_spec(d)], out_specs=_row_spec(d),
        out_shape=jax.ShapeDtypeStruct((t, d), BF16), compiler_params=_params("parallel"),
    )(x, g)


def _post_pre_fwd(y, g_post, x, g_pre, name, comms=()):
    t, d = x.shape

    def body(y_ref, gp_ref, x_ref, gn_ref, xn_ref, h_ref):
        xn = x_ref[...] + _rms(y_ref[...], gp_ref[...])[0]
        xn_ref[...] = xn
        h_ref[...] = _rms(xn, gn_ref[...])[0].astype(BF16)

    return _pcall(
        body, (y, g_post, x, g_pre), name=name, grid=(t // NORM_ROWS,),
        in_specs=[_row_spec(d), _vec_spec(d), _row_spec(d), _vec_spec(d)], out_specs=[_row_spec(d), _row_spec(d)],
        out_shape=[jax.ShapeDtypeStruct((t, d), F32), jax.ShapeDtypeStruct((t, d), BF16)], sem=("parallel",), comms=comms)


def _final_fwd_bwd(y, g_post, x, target, name):
    t, d = x.shape

    def body(y_ref, g_ref, x_ref, t_ref, loss_ref, dx_ref, dy_ref, dg_ref):
        g = g_ref[...]
        out, n, rstd = _rms(y_ref[...], g)
        e = x_ref[...] + out - t_ref[...]
        _accumulate(loss_ref, jnp.full(loss_ref.shape, 0.5 / d, F32) * jnp.sum(e * e))
        dx = e * (1.0 / d)
        dx_ref[...] = dx
        dy_ref[...] = _rms_bwd(n, rstd, g, dx).astype(BF16)
        _accumulate(dg_ref, jnp.sum(dx * n, axis=0, keepdims=True))

    return pl.pallas_call(
        body, name=name, grid=(t // NORM_ROWS,),
        in_specs=[_row_spec(d), _vec_spec(d), _row_spec(d), _row_spec(d)],
        out_specs=[pl.BlockSpec((8, 128), lambda i: (0, 0)), _row_spec(d), _row_spec(d), _vec_spec(d)],
        out_shape=[jax.ShapeDtypeStruct((8, 128), F32), jax.ShapeDtypeStruct((t, d), F32),
                   jax.ShapeDtypeStruct((t, d), BF16), jax.ShapeDtypeStruct((1, d), F32)],
        compiler_params=_params("arbitrary"),
    )(y, g_post, x, target)


def _pre_post_bwd(x, g_pre, dh, dx_in, y, g_post, name, comms=()):
    t, d = x.shape
    both = y is not None

    def body(x_ref, gp_ref, dh_ref, dxi_ref, *rest):
        if both:
            y_ref, gq_ref, dx_ref, dy_ref, dgp_ref, dgq_ref = rest
        else:
            dx_ref, dgp_ref = rest
        gp = gp_ref[...]
        _, n, rstd = _rms(x_ref[...], gp)
        dh_v = dh_ref[...]
        dx = dxi_ref[...] + _rms_bwd(n, rstd, gp, dh_v)
        dx_ref[...] = dx
        _accumulate(dgp_ref, jnp.sum(dh_v * n, axis=0, keepdims=True))
        if both:
            gq = gq_ref[...]
            _, ny, rstdy = _rms(y_ref[...], gq)
            dy_ref[...] = _rms_bwd(ny, rstdy, gq, dx).astype(BF16)
            _accumulate(dgq_ref, jnp.sum(dx * ny, axis=0, keepdims=True))

    in_specs = [_row_spec(d), _vec_spec(d), _row_spec(d), _row_spec(d)]
    args = [x, g_pre, dh, dx_in]
    if both:
        in_specs += [_row_spec(d), _vec_spec(d)]
        args += [y, g_post]
        out_specs = [_row_spec(d), _row_spec(d), _vec_spec(d), _vec_spec(d)]
        out_shape = [jax.ShapeDtypeStruct((t, d), F32), jax.ShapeDtypeStruct((t, d), BF16),
                     jax.ShapeDtypeStruct((1, d), F32), jax.ShapeDtypeStruct((1, d), F32)]
    else:
        out_specs = [_row_spec(d), _vec_spec(d)]
        out_shape = [jax.ShapeDtypeStruct((t, d), F32), jax.ShapeDtypeStruct((1, d), F32)]
    return _pcall(body, args, name=name, grid=(t // NORM_ROWS,), in_specs=in_specs, out_specs=out_specs, out_shape=out_shape,
                  sem=("arbitrary",), comms=comms)


def _gelu(x):
    return 0.5 * x * (1.0 + lax.erf(x * 0.7071067811865476))


def _gelu_grad(x):
    return 0.5 * (1.0 + lax.erf(x * 0.7071067811865476)) + x * jnp.exp(-0.5 * x * x) * 0.3989422804014327


def _layernorm(v, g, b):
    mu = jnp.mean(v, axis=-1, keepdims=True)
    vc = v - mu
    rs = lax.rsqrt(jnp.mean(vc * vc, axis=-1, keepdims=True) + LN_EPS)
    vhat = vc * rs
    return vhat * g + b, vhat, rs


def _tril_mask():
    return lax.broadcasted_iota(jnp.int32, (CHUNK, CHUNK), 0) >= lax.broadcasted_iota(jnp.int32, (CHUNK, CHUNK), 1)


def _sgu_fwd(z, ln_g, ln_b, w16, bias_b, name, comms=()):
    t = z.shape[0]
    groups = w16.shape[0]
    a = groups * CHUNK

    def body(u_ref, v_ref, g_ref, b_ref, w_ref, bb_ref, o_ref):
        u = _gelu(u_ref[...].astype(F32))
        vn = _layernorm(_gelu(v_ref[...].astype(F32)), g_ref[...], b_ref[...])[0].astype(BF16)
        tril = _tril_mask()
        for g in range(groups):
            sl = slice(g * CHUNK, (g + 1) * CHUNK)
            w = jnp.where(tril, w_ref[g], jnp.zeros((), BF16))
            mixed = _dot(w, vn[:, sl], NN) + bb_ref[g]
            o_ref[:, sl] = (u[:, sl] * mixed).astype(BF16)

    full3 = pl.BlockSpec((groups, CHUNK, CHUNK), lambda c: (0, 0, 0))
    res = _pcall(
        body, (z, z, ln_g, ln_b, w16, bias_b), name=name, grid=(t // CHUNK,),
        in_specs=[pl.BlockSpec((CHUNK, a), lambda c: (c, 0)), pl.BlockSpec((CHUNK, a), lambda c: (c, 1)),
                  _vec_spec(a), _vec_spec(a), full3, full3],
        out_specs=[pl.BlockSpec((CHUNK, a), lambda c: (c, 0))], out_shape=[jax.ShapeDtypeStruct((t, a), BF16)],
        sem=("parallel",), comms=comms)
    return (res[0][0], res[1]) if comms else res[0]


def _sgu_bwd(z, dab, ln_g, ln_b, w16, bias_b, name, comms=()):
    t = z.shape[0]
    groups = w16.shape[0]
    a = groups * CHUNK

    def body(u_ref, v_ref, da_ref, g_ref, b_ref, w_ref, bb_ref, duv_ref, dg_ref, db_ref, dw_ref, dbs_ref, dvn_ref):
        up = u_ref[...].astype(F32)
        vp = v_ref[...].astype(F32)
        u = _gelu(up)
        ln_gain = g_ref[...]
        vn32, vhat, rs = _layernorm(_gelu(vp), ln_gain, b_ref[...])
        vn = vn32.astype(BF16)
        da = da_ref[...].astype(F32)
        tril = _tril_mask()
        ones = jnp.ones((8, CHUNK), F32)

        @pl.when(pl.program_id(0) == 0)
        def _():
            dw_ref[...] = jnp.zeros_like(dw_ref)
            dbs_ref[...] = jnp.zeros_like(dbs_ref)

        for g in range(groups):
            sl = slice(g * CHUNK, (g + 1) * CHUNK)
            w = jnp.where(tril, w_ref[g], jnp.zeros((), BF16))
            mixed = _dot(w, vn[:, sl], NN) + bb_ref[g]
            dmix = da[:, sl] * u[:, sl]
            dmix16 = dmix.astype(BF16)
            duv_ref[:, sl] = (da[:, sl] * mixed * _gelu_grad(up[:, sl])).astype(BF16)
            dvn_ref[:, sl] = _dot(w, dmix16, TN)
            dw_ref[g] += jnp.where(tril, _dot(dmix16, vn[:, sl], NT), 0.0)
            dbs_ref[g:g + 1, :] += lax.dot_general(ones, dmix, (NT, ((), ())), precision=lax.Precision.HIGHEST,
                                                   preferred_element_type=F32)[0:1]
        dvn = dvn_ref[...]
        dvhat = dvn * ln_gain
        dva = rs * (dvhat - jnp.mean(dvhat, axis=-1, keepdims=True) - vhat * jnp.mean(dvhat * vhat, axis=-1, keepdims=True))
        duv_ref[:, a:] = (dva * _gelu_grad(vp)).astype(BF16)
        _accumulate(dg_ref, jnp.sum(dvn * vhat, axis=0, keepdims=True))
        _accumulate(db_ref, jnp.sum(dvn, axis=0, keepdims=True))

    full3 = pl.BlockSpec((groups, CHUNK, CHUNK), lambda c: (0, 0, 0))
    return _pcall(
        body, (z, z, dab, ln_g, ln_b, w16, bias_b), name=name, grid=(t // CHUNK,),
        in_specs=[pl.BlockSpec((CHUNK, a), lambda c: (c, 0)), pl.BlockSpec((CHUNK, a), lambda c: (c, 1)),
                  pl.BlockSpec((CHUNK, a), lambda c: (c, 0)), _vec_spec(a), _vec_spec(a), full3, full3],
        out_specs=[pl.BlockSpec((CHUNK, 2 * a), lambda c: (c, 0)), _vec_spec(a), _vec_spec(a), full3,
                   pl.BlockSpec((groups, CHUNK), lambda c: (0, 0))],
        out_shape=[jax.ShapeDtypeStruct((t, 2 * a), BF16), jax.ShapeDtypeStruct((1, a), F32), jax.ShapeDtypeStruct((1, a), F32),
                   jax.ShapeDtypeStruct((groups, CHUNK, CHUNK), F32), jax.ShapeDtypeStruct((groups, CHUNK), F32)],
        scratch=[pltpu.VMEM((CHUNK, a), F32)], sem=("arbitrary",), comms=comms)


def _dil_masks(d):
    qi = lax.broadcasted_iota(jnp.int32, (CHUNK, CHUNK), 0)
    kj = lax.broadcasted_iota(jnp.int32, (CHUNK, CHUNK), 1)
    dist_c = qi - kj
    return dist_c >= 0, dist_c <= 0, (dist_c * d).astype(F32), ((dist_c + CHUNK) * d).astype(F32)


def _alibi_slope(h, heads):
    return 2.0 ** (-8.0 * (h + 1) / heads)


def _dil_view(z, d):
    t, w = z.shape[0], z.shape[1] // 5
    if d == 1:
        return z, 5, 2
    return z[:, 2 * w:].reshape(t // d, d * 3 * w), 3, 0


def _dil_fwd(z, d, name, comms=()):
    t = z.shape[0]
    w = z.shape[1] // 5
    heads = w // HEAD_DIM
    nb = t // d // CHUNK
    scale = HEAD_DIM ** -0.5
    zv, mult, col_q = _dil_view(z, d)

    def body(q_ref, kp_ref, kc_ref, vp_ref, vc_ref, o_ref, l_ref):
        ok_c, ok_p0, bias_c, bias_p = _dil_masks(d)
        ok_p = ok_p0 & (pl.program_id(1) > 0)
        hs = range(heads)
        sl = [slice(h * HEAD_DIM, (h + 1) * HEAD_DIM) for h in hs]
        slope = [_alibi_slope(h, heads) for h in hs]
        ones = jnp.ones((CHUNK, HEAD_DIM), BF16)
        s_c = [_dot(q_ref[:, sl[h]], kc_ref[:, sl[h]], NT) for h in hs]
        s_p = [_dot(q_ref[:, sl[h]], kp_ref[:, sl[h]], NT) for h in hs]
        s_c = [jnp.where(ok_c, s_c[h] * scale - slope[h] * bias_c, NEG) for h in hs]
        s_p = [jnp.where(ok_p, s_p[h] * scale - slope[h] * bias_p, NEG) for h in hs]
        m = [jnp.max(jnp.maximum(s_c[h], s_p[h]), axis=1, keepdims=True) for h in hs]
        p_c = [jnp.exp(s_c[h] - m[h]).astype(BF16) for h in hs]
        p_p = [jnp.exp(s_p[h] - m[h]).astype(BF16) for h in hs]
        den = [_dot(p_c[h], ones, NN) + _dot(p_p[h], ones, NN) for h in hs]
        o = [_dot(p_c[h], vc_ref[:, sl[h]], NN) + _dot(p_p[h], vp_ref[:, sl[h]], NN) for h in hs]
        l_ref[...] = jnp.zeros_like(l_ref)
        for h in hs:
            o_ref[:, sl[h]] = (o[h] / den[h]).astype(BF16)
            l_ref[:, h:h + 1] = m[h] + jnp.log(den[h][:, 0:1])

    def zspec(col, prev):
        if prev:
            return pl.BlockSpec((CHUNK, w), lambda r, n: (jnp.maximum(n - 1, 0), r * mult + col_q + col))
        return pl.BlockSpec((CHUNK, w), lambda r, n: (n, r * mult + col_q + col))

    res = _pcall(
        body, (zv, zv, zv, zv, zv), name=name, grid=(d, nb),
        in_specs=[zspec(0, False), zspec(1, True), zspec(1, False), zspec(2, True), zspec(2, False)],
        out_specs=[pl.BlockSpec((CHUNK, w), lambda r, n: (n, r)), pl.BlockSpec((CHUNK, HEAD_DIM), lambda r, n: (n, r))],
        out_shape=[jax.ShapeDtypeStruct((t // d, d * w), BF16), jax.ShapeDtypeStruct((t // d, d * HEAD_DIM), F32)],
        sem=("parallel", "parallel"), comms=comms)
    (o, lse), rws = res if comms else (res, None)
    outs = (o.reshape(t, w), lse.reshape(t, HEAD_DIM))
    return (outs, rws) if comms else outs


def _dil_merge(a_out, outs, lses, name, comms=()):
    t, a = a_out.shape
    w = outs[0].shape[1]
    heads = w // HEAD_DIM
    nbr = len(outs)

    def body(a_ref, *rest):
        o_refs, l_refs, (ab_ref, lt_ref) = rest[:nbr], rest[nbr:2 * nbr], rest[2 * nbr:]
        ls = [r[...] for r in l_refs]
        m = functools.reduce(jnp.maximum, ls)
        ws = [jnp.exp(l - m) for l in ls]
        tot = functools.reduce(jnp.add, ws)
        ws = [wt / tot for wt in ws]
        ab_ref[:, :a] = a_ref[...]
        for h in range(heads):
            sl = slice(h * HEAD_DIM, (h + 1) * HEAD_DIM)
            mix = functools.reduce(jnp.add, [wt[:, h:h + 1] * r[:, sl].astype(F32) for wt, r in zip(ws, o_refs)])
            ab_ref[:, a + h * HEAD_DIM:a + (h + 1) * HEAD_DIM] = mix.astype(BF16)
        lt_ref[...] = m + jnp.log(tot)

    return _pcall(
        body, (a_out, *outs, *lses), name=name, grid=(t // NORM_ROWS,),
        in_specs=[_row_spec(a)] + [_row_spec(w)] * nbr + [_row_spec(HEAD_DIM)] * nbr,
        out_specs=[_row_spec(a + w), _row_spec(HEAD_DIM)],
        out_shape=[jax.ShapeDtypeStruct((t, a + w), BF16), jax.ShapeDtypeStruct((t, HEAD_DIM), F32)],
        sem=("parallel",), comms=comms)


def _dil_delta(ab, dab, name):
    t, aw = ab.shape
    w = aw // 2
    heads = w // HEAD_DIM

    def body(o_ref, do_ref, dl_ref):
        dl_ref[...] = jnp.zeros_like(dl_ref)
        for h in range(heads):
            sl = slice(h * HEAD_DIM, (h + 1) * HEAD_DIM)
            dl_ref[:, h:h + 1] = jnp.sum(do_ref[:, sl].astype(F32) * o_ref[:, sl].astype(F32), axis=1, keepdims=True)

    half = pl.BlockSpec((NORM_ROWS, w), lambda i: (i, 1))
    return pl.pallas_call(body, name=name, grid=(t // NORM_ROWS,), in_specs=[half, half], out_specs=_row_spec(HEAD_DIM),
                          out_shape=jax.ShapeDtypeStruct((t, HEAD_DIM), F32), compiler_params=_params("parallel"))(ab, dab)


def _dil_bwd(z, dab, ltot, delta, d, name, comms=()):
    t = z.shape[0]
    w = z.shape[1] // 5
    heads = w // HEAD_DIM
    nb = t // d // CHUNK
    scale = HEAD_DIM ** -0.5

    def body(q_ref, qn_ref, kp_ref, kc_ref, vp_ref, vc_ref, do_ref, don_ref, l_ref, ln_ref, dl_ref, dln_ref,
             dq_ref, dk_ref, dv_ref):
        n = pl.program_id(1)
        ok_c, ok_p0, bias_c, bias_p = _dil_masks(d)
        ok_p = ok_p0 & (n > 0)
        ok_n = ok_p0 & (n < nb - 1)
        hs = range(heads)
        sl = [slice(h * HEAD_DIM, (h + 1) * HEAD_DIM) for h in hs]
        slope = [_alibi_slope(h, heads) for h in hs]
        q, qn = [q_ref[:, s] for s in sl], [qn_ref[:, s] for s in sl]
        kp, kc = [kp_ref[:, s] for s in sl], [kc_ref[:, s] for s in sl]
        vp, vc = [vp_ref[:, s] for s in sl], [vc_ref[:, s] for s in sl]
        do, don = [do_ref[:, s] for s in sl], [don_ref[:, s] for s in sl]
        s_c = [_dot(q[h], kc[h], NT) for h in hs]
        s_p = [_dot(q[h], kp[h], NT) for h in hs]
        s_n = [_dot(qn[h], kc[h], NT) for h in hs]
        dp_c = [_dot(do[h], vc[h], NT) for h in hs]
        dp_p = [_dot(do[h], vp[h], NT) for h in hs]
        dp_n = [_dot(don[h], vc[h], NT) for h in hs]
        delta = [dl_ref[:, h:h + 1] for h in hs]
        delta_n = [dln_ref[:, h:h + 1] for h in hs]
        p_c = [jnp.exp(jnp.where(ok_c, s_c[h] * scale - slope[h] * bias_c, NEG) - l_ref[:, h:h + 1]) for h in hs]
        p_p = [jnp.exp(jnp.where(ok_p, s_p[h] * scale - slope[h] * bias_p, NEG) - l_ref[:, h:h + 1]) for h in hs]
        p_n = [jnp.exp(jnp.where(ok_n, s_n[h] * scale - slope[h] * bias_p, NEG) - ln_ref[:, h:h + 1]) for h in hs]
        ds_c = [(p_c[h] * (dp_c[h] - delta[h])).astype(BF16) for h in hs]
        ds_p = [(p_p[h] * (dp_p[h] - delta[h])).astype(BF16) for h in hs]
        ds_n = [(p_n[h] * (dp_n[h] - delta_n[h])).astype(BF16) for h in hs]
        dq = [_dot(ds_c[h], kc[h], NN) + _dot(ds_p[h], kp[h], NN) for h in hs]
        dk = [_dot(ds_c[h], q[h], TN) + _dot(ds_n[h], qn[h], TN) for h in hs]
        dv = [_dot(p_c[h].astype(BF16), do[h], TN) + _dot(p_n[h].astype(BF16), don[h], TN) for h in hs]
        for h in hs:
            dq_ref[:, sl[h]] = (dq[h] * scale).astype(BF16)
            dk_ref[:, sl[h]] = (dk[h] * scale).astype(BF16)
            dv_ref[:, sl[h]] = dv[h].astype(BF16)

    def spec(mult, col, shift, width=w):
        if shift < 0:
            return pl.BlockSpec((CHUNK, width), lambda r, n: (jnp.maximum(n - 1, 0), r * mult + col))
        if shift > 0:
            return pl.BlockSpec((CHUNK, width), lambda r, n: (jnp.minimum(n + 1, nb - 1), r * mult + col))
        return pl.BlockSpec((CHUNK, width), lambda r, n: (n, r * mult + col))

    zv, mult, cq = _dil_view(z, d)
    dov = dab[:, w:].reshape(t // d, d * w)
    lv = ltot.reshape(t // d, d * HEAD_DIM)
    dlv = delta.reshape(t // d, d * HEAD_DIM)
    ospec = spec(1, 0, 0)
    res = _pcall(
        body, (zv, zv, zv, zv, zv, zv, dov, dov, lv, lv, dlv, dlv), name=name, grid=(d, nb),
        in_specs=[spec(mult, cq, 0), spec(mult, cq, 1), spec(mult, cq + 1, -1), spec(mult, cq + 1, 0),
                  spec(mult, cq + 2, -1), spec(mult, cq + 2, 0), spec(1, 0, 0), spec(1, 0, 1),
                  spec(1, 0, 0, HEAD_DIM), spec(1, 0, 1, HEAD_DIM), spec(1, 0, 0, HEAD_DIM), spec(1, 0, 1, HEAD_DIM)],
        out_specs=[ospec, ospec, ospec], out_shape=[jax.ShapeDtypeStruct((t // d, d * w), BF16)] * 3,
        sem=("parallel", "parallel"), comms=comms)
    outs, rws = res if comms else (res, None)
    outs = [o.reshape(t, w) for o in outs]
    return (outs, rws) if comms else outs


def _dz_assemble(duv, parts, name):
    t, a2 = duv.shape
    w = parts[0][0].shape[1]
    nbr = len(parts)

    def body(duv_ref, *rest):
        refs, dz_ref = rest[:-1], rest[-1]
        dz_ref[:, :a2] = duv_ref[...]
        for i in range(3):
            tot = functools.reduce(jnp.add, [refs[b * 3 + i][...].astype(F32) for b in range(nbr)])
            dz_ref[:, a2 + i * w:a2 + (i + 1) * w] = tot.astype(BF16)

    flat = [p for branch in parts for p in branch]
    return pl.pallas_call(
        body, name=name, grid=(t // NORM_ROWS,), in_specs=[_row_spec(a2)] + [_row_spec(w)] * len(flat),
        out_specs=_row_spec(a2 + 3 * w), out_shape=jax.ShapeDtypeStruct((t, a2 + 3 * w), BF16),
        compiler_params=_params("parallel"),
    )(duv, *flat)


def _split_dot(x, m16):
    hi = x.astype(BF16)
    lo = (x - hi.astype(F32)).astype(BF16)
    return _dot(hi, m16, NN) + _dot(lo, m16, NN)


SB_DEAD = -110.0


def _sb_scaled(q):
    return (q.astype(F32) * (HEAD_DIM ** -0.5)).astype(BF16)


SB_PAIR = 2


def _sb_logs(qs, kj, below):
    zt = [_dot(q, k, NT) for q, k in zip(qs, kj)]
    sp = [jnp.maximum(z, 0.0) + jnp.log(1.0 + jnp.exp(-jnp.abs(z))) for z in zt]
    return [z - s for z, s in zip(zt, sp)], [(-s if below is None else jnp.where(below, -s, 0.0)) for s in sp]


def _sb_alive(s, i, c_run):
    return (s <= i) & (jnp.max(c_run) > SB_DEAD)


def _sb_fwd(zc, name, comms=()):
    t = zc.shape[0]
    c = zc.shape[1] // 3
    heads = c // HEAD_DIM
    blk = min(SB_BLOCK, t)

    def body(q_ref, k_ref, v_ref, o_ref, ct_ref, nb_ref):
        i = pl.program_id(1)
        sl = [slice(p * HEAD_DIM, (p + 1) * HEAD_DIM) for p in range(SB_PAIR)]
        qs = [_sb_scaled(q_ref[:, s]) for s in sl]
        rows = lax.broadcasted_iota(jnp.int32, (blk, blk), 0)
        cols = lax.broadcasted_iota(jnp.int32, (blk, blk), 1)
        below = rows > cols
        m_right = below.astype(BF16)

        def tile(carry, diagonal):
            s, acc, c_run = carry[0], carry[1:1 + SB_PAIR], carry[1 + SB_PAIR:]
            off = pl.multiple_of((i - s) * blk, blk)
            log_beta, l = _sb_logs(qs, [k_ref[pl.ds(off, blk), p] for p in sl], below if diagonal else None)
            right = [_split_dot(x, m_right) for x in l]
            a = [jnp.exp(lb + (c + r)) for lb, c, r in zip(log_beta, c_run, right)]
            if diagonal:
                a = [jnp.where(below, x, 0.0) for x in a]
            acc = [o + _dot(x.astype(BF16), v_ref[pl.ds(off, blk), p], NN) for o, x, p in zip(acc, a, sl)]
            return (s + 1, *acc, *[c + jnp.sum(x, axis=1, keepdims=True) for c, x in zip(c_run, l)])

        zeros = [jnp.zeros((blk, HEAD_DIM), F32)] * SB_PAIR + [jnp.zeros((blk, 1), F32)] * SB_PAIR
        out = lax.while_loop(lambda carry: _sb_alive(carry[0], i, functools.reduce(jnp.maximum, carry[1 + SB_PAIR:])),
                             lambda carry: tile(carry, False), tile((jnp.int32(0), *zeros), True))
        for p, s in enumerate(sl):
            o_ref[:, s] = out[1 + p].astype(BF16)
            ct_ref[:, s] = jnp.broadcast_to(out[1 + SB_PAIR + p], (blk, HEAD_DIM))
        nb_ref[...] = jnp.zeros(nb_ref.shape, F32) + out[0].astype(F32)

    pairs = heads // SB_PAIR
    qspec = pl.BlockSpec((blk, SB_PAIR * HEAD_DIM), lambda h, i: (i, h))
    return _pcall(body, (zc, zc, zc), name=name, grid=(pairs, t // blk),
                  in_specs=[qspec, pl.BlockSpec((t, SB_PAIR * HEAD_DIM), lambda h, i: (0, pairs + h)),
                            pl.BlockSpec((t, SB_PAIR * HEAD_DIM), lambda h, i: (0, 2 * pairs + h))],
                  out_specs=[qspec, qspec, qspec],
                  out_shape=[jax.ShapeDtypeStruct((t, c), BF16), jax.ShapeDtypeStruct((t, c), F32), jax.ShapeDtypeStruct((t, c), F32)],
                  sem=("parallel", "parallel"), comms=comms)


def _sb_bwd(zc, ctot, swept, do, name, comms=()):
    t = zc.shape[0]
    c = zc.shape[1] // 3
    heads = c // HEAD_DIM
    blk = min(SB_BLOCK, t)
    scale = HEAD_DIM ** -0.5

    def body(q_ref, k_ref, v_ref, ct_ref, nb_ref, do_ref, dq_ref, dk_ref, dv_ref):
        i = pl.program_id(1)

        @pl.when(i == 0)
        def _():
            dk_ref[...] = jnp.zeros_like(dk_ref)
            dv_ref[...] = jnp.zeros_like(dv_ref)

        ps = range(SB_PAIR)
        sl = [slice(p * HEAD_DIM, (p + 1) * HEAD_DIM) for p in ps]
        qs = [_sb_scaled(q_ref[:, s]) for s in sl]
        dov = [do_ref[:, s] for s in sl]
        c_tot = [ct_ref[:, p * HEAD_DIM:p * HEAD_DIM + 1] for p in ps]
        n_blocks = jnp.clip(jnp.max(nb_ref[0:8, :]).astype(jnp.int32), 1, i + 1)
        rows = lax.broadcasted_iota(jnp.int32, (blk, blk), 0)
        cols = lax.broadcasted_iota(jnp.int32, (blk, blk), 1)
        below = rows > cols
        m_upto = (rows <= cols).astype(BF16)
        m_left = (rows < cols).astype(BF16)

        def tile(j, carry, diagonal):
            dq, l_run, w_run = carry[:SB_PAIR], carry[SB_PAIR:2 * SB_PAIR], carry[2 * SB_PAIR:]
            off = pl.multiple_of(j * blk, blk)
            kj = [k_ref[pl.ds(off, blk), s] for s in sl]
            vj = [v_ref[pl.ds(off, blk), s] for s in sl]
            log_beta, l = _sb_logs(qs, kj, below if diagonal else None)
            d_a = [_dot(dov[p], vj[p], NT) for p in ps]
            upto = [_split_dot(x, m_upto) for x in l]
            a = [jnp.exp(log_beta[p] + (c_tot[p] - l_run[p] - upto[p])) for p in ps]
            if diagonal:
                a = [jnp.where(below, x, 0.0) for x in a]
            wgt = [a[p] * d_a[p] for p in ps]
            before = [w_run[p] + _split_dot(wgt[p], m_left) for p in ps]
            dz = [wgt[p] * jnp.exp(l[p]) - jnp.exp(log_beta[p]) * before[p] for p in ps]
            if diagonal:
                dz = [jnp.where(below, x, 0.0) for x in dz]
            dz16 = [x.astype(BF16) for x in dz]
            dk = [_dot(dz16[p], qs[p], TN) for p in ps]
            dv = [_dot(a[p].astype(BF16), dov[p], TN) for p in ps]
            dq = [dq[p] + _dot(dz16[p], kj[p], NN) for p in ps]
            for p in ps:
                dk_ref[pl.ds(off, blk), sl[p]] += dk[p]
                dv_ref[pl.ds(off, blk), sl[p]] += dv[p]
            return (*dq, *[l_run[p] + jnp.sum(l[p], axis=1, keepdims=True) for p in ps],
                    *[w_run[p] + jnp.sum(wgt[p], axis=1, keepdims=True) for p in ps])

        zeros = [jnp.zeros((blk, HEAD_DIM), F32)] * SB_PAIR + [jnp.zeros((blk, 1), F32)] * (2 * SB_PAIR)
        carry = lax.fori_loop(i + 1 - n_blocks, i, lambda j, carry: tile(j, carry, False), tuple(zeros))
        out = tile(i, carry, True)
        for p in ps:
            dq_ref[:, sl[p]] = out[p] * scale

    pairs = heads // SB_PAIR
    qspec = pl.BlockSpec((blk, SB_PAIR * HEAD_DIM), lambda h, i: (i, h))
    full = pl.BlockSpec((t, SB_PAIR * HEAD_DIM), lambda h, i: (0, h))
    return _pcall(body, (zc, zc, zc, ctot, swept, do), name=name, grid=(pairs, t // blk),
                  in_specs=[qspec, pl.BlockSpec((t, SB_PAIR * HEAD_DIM), lambda h, i: (0, pairs + h)),
                            pl.BlockSpec((t, SB_PAIR * HEAD_DIM), lambda h, i: (0, 2 * pairs + h)), qspec, qspec, qspec],
                  out_specs=[qspec, full, full], out_shape=[jax.ShapeDtypeStruct((t, c), F32)] * 3,
                  sem=("arbitrary", "arbitrary"), comms=comms)


def _concat_bf16(parts, name, comms=()):
    t, c = parts[0].shape

    def body(*refs):
        for k, r in enumerate(refs[:-1]):
            refs[-1][:, k * c:(k + 1) * c] = r[...].astype(BF16)

    res = _pcall(body, tuple(parts), name=name, grid=(t // NORM_ROWS,), in_specs=[_row_spec(c)] * len(parts),
                 out_specs=[_row_spec(c * len(parts))], out_shape=[jax.ShapeDtypeStruct((t, c * len(parts)), BF16)],
                 sem=("parallel",), comms=comms)
    return (res[0][0], res[1]) if comms else res[0]


KIND = {"ab_w_in": "col", "ab_w_out": "row", "sb_w_in": "col", "sb_w_out": "row",
        "ffn_w1_0": "col", "ffn_w1_1": "col", "ffn_w2_0": "row", "ffn_w2_1": "row"}
X_Y, DIAG, CHIPS = (2, 4), (6,), (2, 4, 6)


def _local_step(x, target, norms, sgu, big, bufs=None):
    g = {k: [v[l:l + 1] for l in range(2)] for k, v in norms.items()}
    ln_g, ln_b, sgu_w, sgu_b = sgu
    groups = sgu_w.shape[0]
    w16 = sgu_w.astype(BF16)
    bias_b = jnp.broadcast_to(sgu_b[:, :, None], (groups, CHUNK, CHUNK))
    big, dws, psum, dist = dict(big), {}, {}, bufs is not None
    pair, got = (dict(bufs[0]), dict(bufs[1])) if dist else ({}, {})

    def run(fn, *args, ops=(), **kw):
        if not dist or not ops:
            return fn(*args, **kw)
        make = {"gs": lambda k, p, *part: _GatherSend(big[k], KIND[k], p, *part), "gf": lambda k, p: _GatherFwd(big[k], KIND[k], p),
                "swap": lambda k, p: _PairSwap(dws[k], pair[k], KIND[k]),
                "chips": lambda k, p, *part: _ChipScatter(psum[k], got[k], p, *part)}
        out, rws = fn(*args, comms=[make[op[0]](*op[1:]) for op in ops], **kw)
        for (op, k, *_), r in zip(ops, rws):
            if op in ("gs", "gf"):
                big[k] = r[0]
            elif op == "swap":
                psum[k] = _pair_sum(dws[k], r[0], KIND[k], f"pair_sum_{k}")
            else:
                got[k] = r[0]
        return out

    h1_0 = _rms_fwd(x, g["pre_mix"][0], "rms_in")
    z0 = run(_matmul, h1_0, big["ab_w_in"], "nn", BF16, "ab_in", ops=[("gs", "ffn_w1_0", X_Y)])
    a_out = run(_sgu_fwd, z0, ln_g, ln_b, w16, bias_b, "sgu_fwd", ops=[("gf", "ffn_w1_0", X_Y), ("gs", "ab_w_out", CHIPS)])
    branches = [run(_dil_fwd, z0, 1, "dil_fwd_1", ops=[("gs", "ffn_w1_0", DIAG, (0, 2)), ("gf", "ab_w_out", CHIPS)]),
                run(_dil_fwd, z0, 4, "dil_fwd_4", ops=[("gs", "ffn_w1_0", DIAG, (1, 2))]),
                run(_dil_fwd, z0, 16, "dil_fwd_16", ops=[("gf", "ffn_w1_0", DIAG), ("gs", "ffn_w2_0", X_Y, (0, 2))])]
    ab, ltot = run(_dil_merge, a_out, [b[0] for b in branches], [b[1] for b in branches], "dil_merge",
                   ops=[("gs", "ffn_w2_0", X_Y, (1, 2))])
    y_0 = run(_matmul, ab, big["ab_w_out"], "nn", F32, "ab_out", ops=[("gs", "ffn_w2_0", DIAG, (0, 2))])
    x1, h2_0 = run(_post_pre_fwd, y_0, g["post_mix"][0], x, g["pre_ffn"][0], "norm_mix0", ops=[("gs", "ffn_w2_0", DIAG, (1, 2))])
    r_0 = run(_matmul, h2_0, big["ffn_w1_0"], "nn", BF16, "ffn_up_0", relu_out=True,
              ops=[("gf", "ffn_w2_0", CHIPS), ("gs", "sb_w_in", CHIPS)])
    y2_0 = run(_matmul, r_0, big["ffn_w2_0"], "nn", F32, "ffn_down_0", a_square=True,
               ops=[("gf", "sb_w_in", CHIPS), ("gs", "sb_w_out", CHIPS), ("gs", "ffn_w1_1", X_Y)])
    x2, h1_1 = run(_post_pre_fwd, y2_0, g["post_ffn"][0], x1, g["pre_mix"][1], "norm_ffn0",
                   ops=[("gf", "ffn_w1_1", X_Y), ("gf", "sb_w_out", CHIPS)])
    zc = run(_matmul, h1_1, big["sb_w_in"], "nn", BF16, "sb_in", ops=[("gs", "ffn_w1_1", DIAG)])
    o_sb, ct_sb, nb_sb = run(_sb_fwd, zc, "sb_fwd", ops=[("gf", "ffn_w1_1", DIAG), ("gs", "ffn_w2_1", CHIPS)])
    y_1 = run(_matmul, o_sb, big["sb_w_out"], "nn", F32, "sb_out", ops=[("gf", "ffn_w2_1", CHIPS)])
    x3, h2_1 = _post_pre_fwd(y_1, g["post_mix"][1], x2, g["pre_ffn"][1], "norm_mix1")
    r_1 = _matmul(h2_1, big["ffn_w1_1"], "nn", BF16, "ffn_up_1", relu_out=True)
    y2_1 = _matmul(r_1, big["ffn_w2_1"], "nn", F32, "ffn_down_1", a_square=True)
    loss, dx4, dy2_1, dg_post_ffn1 = _final_fwd_bwd(y2_1, g["post_ffn"][1], x3, target, "loss")

    da = _matmul(dy2_1, big["ffn_w2_1"], "nt", BF16, "ffn_da_1", mul2=r_1)
    dws["ffn_w2_1"] = _matmul(r_1, dy2_1, "tn", BF16, "ffn_dw2_1", a_square=True)
    dh2 = run(_matmul, da, big["ffn_w1_1"], "nt", F32, "ffn_dh_1", ops=[("swap", "ffn_w2_1", None)])
    dws["ffn_w1_1"] = run(_matmul, h2_1, da, "tn", BF16, "ffn_dw1_1", ops=[("chips", "ffn_w2_1", X_Y)])
    dx3, dy_1, dg_pre_ffn1, dg_post_mix1 = run(_pre_post_bwd, x3, g["pre_ffn"][1], dh2, dx4, y_1, g["post_mix"][1], "norm_bwd_mix1",
                                               ops=[("swap", "ffn_w1_1", None)])
    do_sb = _matmul(dy_1, big["sb_w_out"], "nt", BF16, "sb_out_dx")
    dws["sb_w_out"] = _matmul(o_sb, dy_1, "tn", BF16, "sb_out_dw")
    dqkv = run(_sb_bwd, zc, ct_sb, nb_sb, do_sb, "sb_bwd",
               ops=[("chips", "ffn_w2_1", DIAG), ("chips", "ffn_w1_1", CHIPS), ("swap", "sb_w_out", None)])
    dzc = run(_concat_bf16, dqkv, "sb_dz", ops=[("chips", "sb_w_out", X_Y)])
    dh1 = run(_matmul, dzc, big["sb_w_in"], "nt", F32, "sb_in_dx", ops=[("chips", "sb_w_out", DIAG)])
    dws["sb_w_in"] = _matmul(h1_1, dzc, "tn", BF16, "sb_in_dw")
    dx2, dy2_0, dg_pre_mix1, dg_post_ffn0 = run(_pre_post_bwd, x2, g["pre_mix"][1], dh1, dx3, y2_0, g["post_ffn"][0], "norm_bwd_ffn0",
                                                ops=[("swap", "sb_w_in", None)])
    da = run(_matmul, dy2_0, big["ffn_w2_0"], "nt", BF16, "ffn_da_0", mul2=r_0, ops=[("chips", "sb_w_in", X_Y)])
    dws["ffn_w2_0"] = run(_matmul, r_0, dy2_0, "tn", BF16, "ffn_dw2_0", a_square=True, ops=[("chips", "sb_w_in", DIAG)])
    dws["ffn_w1_0"] = run(_matmul, h2_0, da, "tn", BF16, "ffn_dw1_0", ops=[("swap", "ffn_w2_0", None)])
    dh2 = run(_matmul, da, big["ffn_w1_0"], "nt", F32, "ffn_dh_0", ops=[("chips", "ffn_w2_0", X_Y), ("swap", "ffn_w1_0", None)])
    dx1, dy_0, dg_pre_ffn0, dg_post_mix0 = run(_pre_post_bwd, x1, g["pre_ffn"][0], dh2, dx2, y_0, g["post_mix"][0], "norm_bwd_mix0",
                                               ops=[("chips", "ffn_w2_0", DIAG, (0, 2))])
    dab = run(_matmul, dy_0, big["ab_w_out"], "nt", BF16, "ab_out_dx", ops=[("chips", "ffn_w2_0", DIAG, (1, 2))])
    dws["ab_w_out"] = run(_matmul, ab, dy_0, "tn", BF16, "ab_out_dw", ops=[("chips", "ffn_w1_0", X_Y, (0, 2))])
    duv, d_ln_g, d_ln_b, d_sgu_w, d_sgu_b = run(_sgu_bwd, z0, dab, ln_g, ln_b, w16, bias_b, "sgu_bwd",
                                                ops=[("chips", "ffn_w1_0", X_Y, (1, 2))])
    delta = _dil_delta(ab, dab, "dil_delta")
    parts = [run(_dil_bwd, z0, dab, ltot, delta, 1, "dil_bwd_1", ops=[("chips", "ffn_w1_0", DIAG, (0, 2)), ("swap", "ab_w_out", None)]),
             run(_dil_bwd, z0, dab, ltot, delta, 4, "dil_bwd_4", ops=[("chips", "ffn_w1_0", DIAG, (1, 2))]),
             run(_dil_bwd, z0, dab, ltot, delta, 16, "dil_bwd_16", ops=[("chips", "ab_w_out", CHIPS)])]
    dz0 = _dz_assemble(duv, parts, "dz_assemble")
    dws["ab_w_in"] = _matmul(h1_0, dz0, "tn", BF16, "ab_in_dw")
    dh1 = run(_matmul, dz0, big["ab_w_in"], "nt", F32, "ab_in_dx", ops=[("swap", "ab_w_in", None)])
    grad_x, dg_pre_mix0 = run(_pre_post_bwd, x, g["pre_mix"][0], dh1, dx1, None, None, "norm_bwd_in", ops=[("chips", "ab_w_in", X_Y)])

    d_norms = {
        "pre_mix": jnp.concatenate([dg_pre_mix0, dg_pre_mix1]), "post_mix": jnp.concatenate([dg_post_mix0, dg_post_mix1]),
        "pre_ffn": jnp.concatenate([dg_pre_ffn0, dg_pre_ffn1]), "post_ffn": jnp.concatenate([dg_post_ffn0, dg_post_ffn1]),
    }
    return loss, grad_x, d_norms, (d_ln_g, d_ln_b, d_sgu_w, d_sgu_b), (psum, got) if dist else dws


def _to_bf16_full(w, layer, kind, name):
    _, rows, cols = w.shape
    tr = _tile(rows, 512)
    nblk = rows // tr
    full = (rows, 4 * cols) if kind == "col" else (4 * rows, cols)

    def body(w_ref, o_ref):
        o_ref[...] = w_ref[...].astype(BF16)

    def place(i):
        mine = 2 * lax.axis_index("x") + lax.axis_index("y")
        return (i, mine) if kind == "col" else (mine * nblk + i, 0)

    return pl.pallas_call(
        body, name=name, grid=(nblk,), in_specs=[pl.BlockSpec((None, tr, cols), lambda i: (layer, i, 0))],
        out_specs=pl.BlockSpec((tr, cols), place), out_shape=jax.ShapeDtypeStruct(full, BF16), compiler_params=_params("parallel"),
    )(w)


def _pair_sum(dw16, pair, kind, name):
    rh, cs = _half_shape(dw16.shape, kind)
    tr = _tile(rh, 256)
    nblk = rh // tr

    def body(dw_ref, pair_ref, o_ref):
        o_ref[...] = (dw_ref[...].astype(F32) + pair_ref[...].astype(F32)).astype(BF16)

    def own(s, i):
        c = lax.axis_index("c")
        return (c * nblk + i, s) if kind == "col" else ((2 * s + c) * nblk + i, 0)

    spec3 = pl.BlockSpec((None, tr, cs), lambda s, i: (s, i, 0))
    return pl.pallas_call(
        body, name=name, grid=(4, nblk), in_specs=[pl.BlockSpec((tr, cs), own), spec3], out_specs=spec3,
        out_shape=jax.ShapeDtypeStruct((4, rh, cs), BF16), compiler_params=_params("parallel", "parallel"),
    )(dw16, pair)


def _owner_sum(psum, got, buf, layer, name, comms=()):
    _, rh, cs = psum.shape
    tr = _tile(rh, 256)

    def body(p_ref, got_ref, buf_ref, o_ref):
        tot = p_ref[...].astype(F32)
        for j in range(3):
            tot = tot + got_ref[j].astype(F32)
        o_ref[...] = tot

    res = _pcall(
        body, (psum, got, buf), name=name, grid=(rh // tr,),
        in_specs=[pl.BlockSpec((None, tr, cs), lambda i: (2 * lax.axis_index("x") + lax.axis_index("y"), i, 0)),
                  pl.BlockSpec((3, tr, cs), lambda i: (0, i, 0)), ANY],
        out_specs=[pl.BlockSpec((None, None, tr, cs), lambda i: (layer, lax.axis_index("c"), i, 0))],
        out_shape=[jax.ShapeDtypeStruct(buf.shape, F32)], sem=("parallel",), comms=comms, aliases={2: 0})
    return (res[0][0], res[1]) if comms else res[0]


def _adamw_math(w, g, m, v):
    m = ADAM_B1 * m + (1.0 - ADAM_B1) * g
    v = ADAM_B2 * v + (1.0 - ADAM_B2) * (g * g)
    m_hat = m / (1.0 - ADAM_B1 ** ADAM_STEP)
    v_hat = v / (1.0 - ADAM_B2 ** ADAM_STEP)
    return -ADAM_LR * (m_hat / (jnp.sqrt(v_hat) + ADAM_EPS) + ADAM_WD * w), m, v


def _adamw(w, g, m, v, name):
    layers, rows, cols = w.shape
    tr = _tile(rows, 256)

    def body(w_ref, g_ref, m_ref, v_ref, go_ref, d_ref, mo_ref, vo_ref):
        g = g_ref[...]
        go_ref[...] = g
        d_ref[...], mo_ref[...], vo_ref[...] = _adamw_math(w_ref[...], g, m_ref[...], v_ref[...])

    spec = pl.BlockSpec((None, tr, cols), lambda l, i: (l, i, 0))
    return _pcall(body, (w, g, m, v), name=name, grid=(layers, rows // tr), in_specs=[spec] * 4, out_specs=[spec] * 4,
                  out_shape=[jax.ShapeDtypeStruct(w.shape, F32)] * 4, sem=("parallel", "parallel"))


def _pack(arrays):
    flat = jnp.concatenate([a.reshape(-1) for a in arrays])
    pad = (-flat.shape[0]) % 1024
    return jnp.pad(flat, (0, pad)).reshape(-1, 128)


def _unpack(packed, like):
    flat = packed.reshape(-1)
    out, off = [], 0
    for a in like:
        out.append(flat[off:off + a.size].reshape(a.shape))
        off += a.size
    return out


class _SmallGather:
    n_sems = 7

    def __init__(self, g, parts):
        self.ro, self.rw = [g], [parts]

    def start(self, ro, rw, send, recv):
        x, y, c, _ = _place()
        for j in range(1, 8):
            _remote(ro[0], rw[0].at[4 * x + 2 * y + c], send(j - 1), recv(j - 1), _flip(x, y, c, j)).start()

    def finish(self, ro, rw, send, recv):
        x, y, c, _ = _place()
        for j in range(1, 8):
            px, py, pc = _flip(x, y, c, j)
            slot = rw[0].at[4 * px + 2 * py + pc]
            cp = _remote(slot, slot, send(j - 1), recv(j - 1), (x, y, c))
            cp.wait_recv()
            cp.wait_send()


def _small_update(own, parts, w, m, v, name):
    rows = w.shape[0]

    def body(own_ref, p_ref, w_ref, m_ref, v_ref, g_ref, d_ref, mo_ref, vo_ref):
        me = 4 * lax.axis_index("x") + 2 * lax.axis_index("y") + lax.axis_index("c")
        g = jnp.where(me == 0, own_ref[...], p_ref[0])
        for k in range(1, 8):
            g = g + jnp.where(me == k, own_ref[...], p_ref[k])
        g_ref[...] = g
        d_ref[...], mo_ref[...], vo_ref[...] = _adamw_math(w_ref[...], g, m_ref[...], v_ref[...])

    return pl.pallas_call(body, name=name, out_shape=[jax.ShapeDtypeStruct((rows, 128), F32)] * 4,
                          compiler_params=_params())(own, parts, w, m, v)


SMALL = ("norm_pre_mix", "norm_post_mix", "norm_pre_ffn", "norm_post_ffn", "sgu_ln_g", "sgu_ln_b", "sgu_w", "sgu_b")
BIG = (("ab_w_in", ("ab_w_in",)), ("ab_w_out", ("ab_w_out",)), ("sb_w_in", ("sb_w_in",)), ("sb_w_out", ("sb_w_out",)),
       ("ffn_w1", ("ffn_w1_0", "ffn_w1_1")), ("ffn_w2", ("ffn_w2_0", "ffn_w2_1")))
WEIGHTS = ("norm_pre_mix", "norm_post_mix", "norm_pre_ffn", "norm_post_ffn", "ab_w_in", "sgu_ln_g", "sgu_ln_b", "sgu_w", "sgu_b",
           "ab_w_out", "sb_w_in", "sb_w_out", "ffn_w1", "ffn_w2")


def kernel(x, norm_pre_mix, norm_post_mix, norm_pre_ffn, norm_post_ffn, ab_w_in, sgu_ln_g, sgu_ln_b, sgu_w, sgu_b, ab_w_out, sb_w_in, sb_w_out, ffn_w1, ffn_w2, loss_target, m_norm_pre_mix, m_norm_post_mix, m_norm_pre_ffn, m_norm_post_ffn, m_ab_w_in, m_sgu_ln_g, m_sgu_ln_b, m_sgu_w, m_sgu_b, m_ab_w_out, m_sb_w_in, m_sb_w_out, m_ffn_w1, m_ffn_w2, v_norm_pre_mix, v_norm_post_mix, v_norm_pre_ffn, v_norm_post_ffn, v_ab_w_in, v_sgu_ln_g, v_sgu_ln_b, v_sgu_w, v_sgu_b, v_ab_w_out, v_sb_w_in, v_sb_w_out, v_ffn_w1, v_ffn_w2):
    w = dict(norm_pre_mix=norm_pre_mix, norm_post_mix=norm_post_mix, norm_pre_ffn=norm_pre_ffn, norm_post_ffn=norm_post_ffn,
             ab_w_in=ab_w_in, sgu_ln_g=sgu_ln_g, sgu_ln_b=sgu_ln_b, sgu_w=sgu_w, sgu_b=sgu_b, ab_w_out=ab_w_out, sb_w_in=sb_w_in,
             sb_w_out=sb_w_out, ffn_w1=ffn_w1, ffn_w2=ffn_w2)
    m = dict(norm_pre_mix=m_norm_pre_mix, norm_post_mix=m_norm_post_mix, norm_pre_ffn=m_norm_pre_ffn, norm_post_ffn=m_norm_post_ffn,
             ab_w_in=m_ab_w_in, sgu_ln_g=m_sgu_ln_g, sgu_ln_b=m_sgu_ln_b, sgu_w=m_sgu_w, sgu_b=m_sgu_b, ab_w_out=m_ab_w_out,
             sb_w_in=m_sb_w_in, sb_w_out=m_sb_w_out, ffn_w1=m_ffn_w1, ffn_w2=m_ffn_w2)
    v = dict(norm_pre_mix=v_norm_pre_mix, norm_post_mix=v_norm_post_mix, norm_pre_ffn=v_norm_pre_ffn, norm_post_ffn=v_norm_post_ffn,
             ab_w_in=v_ab_w_in, sgu_ln_g=v_sgu_ln_g, sgu_ln_b=v_sgu_ln_b, sgu_w=v_sgu_w, sgu_b=v_sgu_b, ab_w_out=v_ab_w_out,
             sb_w_in=v_sb_w_in, sb_w_out=v_sb_w_out, ffn_w1=v_ffn_w1, ffn_w2=v_ffn_w2)
    big, pair, got = {}, {}, {}
    for name, keys in BIG:
        for layer, key in enumerate(keys):
            big[key] = _to_bf16_full(w[name], layer, KIND[key], f"bf16_{key}")
            half = _half_shape(big[key].shape, KIND[key])
            pair[key], got[key] = lax.empty((4,) + half, BF16), lax.empty((3,) + half, BF16)
    big["ab_w_in"] = _comm_call([_Gather(big["ab_w_in"], KIND["ab_w_in"])], "gather_first")[0][0]

    norms = {k: w["norm_" + k] for k in ("pre_mix", "post_mix", "pre_ffn", "post_ffn")}
    sgu = (sgu_ln_g, sgu_ln_b, sgu_w[0], sgu_b[0])
    loss_blk, grad_x, d_norms, d_sgu, (psum, got) = _local_step(x[0], loss_target[0], norms, sgu, big, (pair, got))
    loss = lax.psum(loss_blk[0, 0], ("x", "y", "c"))

    grads, deltas, new_m, new_v = {}, {}, {}, {}
    keys_of = dict(BIG)
    small_g = _pack([d_norms["pre_mix"], d_norms["post_mix"], d_norms["pre_ffn"], d_norms["post_ffn"],
                     d_sgu[0], d_sgu[1], d_sgu[2][None], d_sgu[3][None]])
    parts = None
    bufs, pending = {}, None
    for name in ("ffn_w2", "ffn_w1", "sb_w_in", "sb_w_out", "ab_w_out"):
        buf = lax.empty((len(keys_of[name]), 2) + psum[keys_of[name][0]].shape[1:], F32)
        for layer, key in enumerate(keys_of[name]):
            if pending is not None:
                buf, rws = _owner_sum(psum[key], got[key], buf, layer, f"sum_{key}", comms=[_Join([bufs[pending]])])
                bufs[pending], pending = rws[0][0], None
            elif parts is None:
                buf, rws = _owner_sum(psum[key], got[key], buf, layer, f"sum_{key}",
                                      comms=[_SmallGather(small_g, lax.empty((8,) + small_g.shape, F32))])
                parts = rws[0][0]
            else:
                buf = _owner_sum(psum[key], got[key], buf, layer, f"sum_{key}")
        bufs[name], pending = buf, name

    rws = _comm_call([_Join([bufs["ab_w_out"]]), _ChipScatter(psum["ab_w_in"], got["ab_w_in"], DIAG)], "tail_comm")
    bufs["ab_w_out"], got["ab_w_in"] = rws[0][0], rws[1][0]
    bufs["ab_w_in"] = _owner_sum(psum["ab_w_in"], got["ab_w_in"], lax.empty((1, 2) + psum["ab_w_in"].shape[1:], F32), 0, "sum_ab_w_in")
    bufs["ab_w_in"] = _comm_call([_Join([bufs["ab_w_in"]])], "join_last")[0][0]
    for name, _ in BIG:
        grads[name], deltas[name], new_m[name], new_v[name] = _adamw(w[name], bufs[name].reshape(w[name].shape), m[name], v[name], f"adamw_{name}")

    outs = _small_update(small_g, parts, _pack([w[k] for k in SMALL]), _pack([m[k] for k in SMALL]), _pack([v[k] for k in SMALL]), "small_update")
    like = [w[k] for k in SMALL]
    for dst, packed in zip((grads, deltas, new_m, new_v), outs):
        for k, a in zip(SMALL, _unpack(packed, like)):
            dst[k] = a

    return (loss, grad_x[None], *[grads[k] for k in WEIGHTS], *[deltas[k] for k in WEIGHTS],
            *[new_m[k] for k in WEIGHTS], *[new_v[k] for k in WEIGHTS])
```

```python
import functools

import jax
import jax.numpy as jnp
from jax import lax
from jax.experimental import pallas as pl
from jax.experimental.pallas import tpu as pltpu

F32 = jnp.float32
BF16 = jnp.bfloat16
MESH = pl.DeviceIdType.MESH

HEAD_DIM = 128
CHUNK = 128
DILATIONS = (1, 4, 16)
SB_BLOCK = 256
RMS_EPS = 1e-6
LN_EPS = 1e-5
ADAM_LR, ADAM_B1, ADAM_B2, ADAM_EPS, ADAM_WD, ADAM_STEP = 0.001, 0.9, 0.999, 1e-08, 0.01, 10
NEG = -1e30
V7X_VMEM_LIMIT = 48 * 1024 * 1024
ANY = pl.BlockSpec(memory_space=pl.ANY)


def _params(*sem):
    return pltpu.CompilerParams(dimension_semantics=sem if sem else None, vmem_limit_bytes=V7X_VMEM_LIMIT)


def _tile(n, pref):
    if n <= pref:
        return n
    t = pref
    while n % t:
        t -= 128
    return t


def _dot(a, b, dims):
    return lax.dot_general(a, b, (dims, ((), ())), preferred_element_type=F32)


NN = ((1,), (0,))
NT = ((1,), (1,))
TN = ((0,), (0,))


def _place():
    x, y, c = lax.axis_index("x"), lax.axis_index("y"), lax.axis_index("c")
    return x, y, c, 2 * x + y


def _flip(x, y, c, j):
    return (1 - x if j & 4 else x), (1 - y if j & 2 else y), (1 - c if j & 1 else c)


def _half_shape(full_shape, kind):
    rows, cols = full_shape
    return (rows // 2, cols // 4) if kind == "col" else (rows // 8, cols)


def _half(ref, kind, s, h):
    rh, cs = _half_shape(ref.shape, kind)
    if kind == "col":
        return ref.at[pl.ds(h * rh, rh), pl.ds(s * cs, cs)]
    return ref.at[pl.ds((2 * s + h) * rh, rh), :]


def _remote(src, dst, send, recv, to):
    return pltpu.make_async_remote_copy(src_ref=src, dst_ref=dst, send_sem=send, recv_sem=recv, device_id=to, device_id_type=MESH)


class _Gather:
    n_sems = 6

    def __init__(self, full, kind):
        self.ro, self.rw, self.kind = [], [full], kind

    def start(self, ro, rw, send, recv):
        x, y, c, mine = _place()
        own = _half(rw[0], self.kind, mine, c)
        for k, j in enumerate((2, 4, 6)):
            px, py, _ = _flip(x, y, c, j)
            _remote(own, own, send(k), recv(k), (px, py, c)).start()

    def finish(self, ro, rw, send, recv):
        x, y, c, mine = _place()
        own = _half(rw[0], self.kind, mine, c)
        for k, j in enumerate((2, 4, 6)):
            px, py, _ = _flip(x, y, c, j)
            got = _half(rw[0], self.kind, 2 * px + py, c)
            _remote(got, got, send(k), recv(k), (x, y, c)).wait_recv()
            _remote(got, got, send(3 + k), recv(3 + k), (x, y, 1 - c)).start()
        for k, j in enumerate((2, 4, 6)):
            px, py, _ = _flip(x, y, c, j)
            got = _half(rw[0], self.kind, 2 * px + py, 1 - c)
            _remote(got, got, send(3 + k), recv(3 + k), (x, y, c)).wait_recv()
        for k in range(6):
            _remote(own, own, send(k), recv(k), (x, y, c)).wait_send()


class _GatherSend:
    def __init__(self, full, kind, patterns, part=(0, 1)):
        self.ro, self.rw, self.kind, self.patterns, self.part, self.n_sems = [], [full], kind, patterns, part, len(patterns)

    def _rows(self, half):
        i, n = self.part
        rows = half.shape[0] // n
        return half.at[pl.ds(i * rows, rows), :]

    def start(self, ro, rw, send, recv):
        x, y, c, mine = _place()
        own = self._rows(_half(rw[0], self.kind, mine, c))
        for k, j in enumerate(self.patterns):
            px, py, _ = _flip(x, y, c, j)
            _remote(own, own, send(k), recv(k), (px, py, c)).start()

    def finish(self, ro, rw, send, recv):
        x, y, c, _ = _place()
        for k, j in enumerate(self.patterns):
            px, py, _ = _flip(x, y, c, j)
            got = self._rows(_half(rw[0], self.kind, 2 * px + py, c))
            cp = _remote(got, got, send(k), recv(k), (x, y, c))
            cp.wait_recv()
            cp.wait_send()


class _GatherFwd:
    def __init__(self, full, kind, patterns):
        self.ro, self.rw, self.kind, self.patterns, self.n_sems = [], [full], kind, patterns, len(patterns)

    def start(self, ro, rw, send, recv):
        x, y, c, _ = _place()
        for k, j in enumerate(self.patterns):
            px, py, _ = _flip(x, y, c, j)
            got = _half(rw[0], self.kind, 2 * px + py, c)
            _remote(got, got, send(k), recv(k), (x, y, 1 - c)).start()

    def finish(self, ro, rw, send, recv):
        x, y, c, _ = _place()
        for k, j in enumerate(self.patterns):
            px, py, _ = _flip(x, y, c, j)
            got = _half(rw[0], self.kind, 2 * px + py, 1 - c)
            cp = _remote(got, got, send(k), recv(k), (x, y, c))
            cp.wait_recv()
            cp.wait_send()


class _PairSwap:
    n_sems = 4

    def __init__(self, dw16, pair, kind):
        self.ro, self.rw, self.kind = [dw16], [pair], kind

    def start(self, ro, rw, send, recv):
        x, y, c, _ = _place()
        for s in range(4):
            _remote(_half(ro[0], self.kind, s, 1 - c), rw[0].at[s], send(s), recv(s), (x, y, 1 - c)).start()

    def finish(self, ro, rw, send, recv):
        x, y, c, _ = _place()
        for s in range(4):
            cp = _remote(rw[0].at[s], rw[0].at[s], send(s), recv(s), (x, y, c))
            cp.wait_recv()
            cp.wait_send()


class _ChipScatter:
    def __init__(self, psum, got, patterns, part=(0, 1)):
        self.ro, self.rw, self.patterns, self.part, self.n_sems = [psum], [got], patterns, part, len(patterns)

    def _rows(self, ref, slot):
        i, n = self.part
        rows = ref.shape[1] // n
        return ref.at[slot, pl.ds(i * rows, rows), :]

    def start(self, ro, rw, send, recv):
        x, y, c, _ = _place()
        for k, j in enumerate(self.patterns):
            px, py, _ = _flip(x, y, c, j)
            _remote(self._rows(ro[0], 2 * px + py), self._rows(rw[0], j // 2 - 1), send(k), recv(k), (px, py, c)).start()

    def finish(self, ro, rw, send, recv):
        x, y, c, _ = _place()
        for k, j in enumerate(self.patterns):
            slot = self._rows(rw[0], j // 2 - 1)
            cp = _remote(slot, slot, send(k), recv(k), (x, y, c))
            cp.wait_recv()
            cp.wait_send()


class _Join:
    def __init__(self, bufs):
        self.ro, self.rw, self.n_sems = [], list(bufs), sum(b.shape[0] for b in bufs)

    def _copies(self, rw, send, recv, slot):
        x, y, c, _ = _place()
        k = 0
        for ref in rw:
            for l in range(ref.shape[0]):
                yield _remote(ref.at[l, c], ref.at[l, slot(c)], send(k), recv(k), (x, y, 1 - c))
                k += 1

    def start(self, ro, rw, send, recv):
        for cp in self._copies(rw, send, recv, lambda c: c):
            cp.start()

    def finish(self, ro, rw, send, recv):
        for cp in self._copies(rw, send, recv, lambda c: 1 - c):
            cp.wait_recv()
        for cp in self._copies(rw, send, recv, lambda c: c):
            cp.wait_send()


def _comm_layout(comms):
    ro = [a for c in comms for a in c.ro]
    rw = [a for c in comms for a in c.rw]
    return ro, rw, sum(c.n_sems for c in comms)


def _comm_each(comms, method, ro_refs, rw_refs, send, recv):
    i_ro = i_rw = i_sem = 0
    for c in comms:
        getattr(c, method)(ro_refs[i_ro:i_ro + len(c.ro)], rw_refs[i_rw:i_rw + len(c.rw)],
                           lambda k, b=i_sem: send.at[b + k], lambda k, b=i_sem: recv.at[b + k])
        i_ro, i_rw, i_sem = i_ro + len(c.ro), i_rw + len(c.rw), i_sem + c.n_sems


def _split_results(comms, rws):
    out, i = [], 0
    for c in comms:
        out.append(list(rws[i:i + len(c.rw)]))
        i += len(c.rw)
    return out


def _comm_call(comms, name):
    ro, rw, n_sems = _comm_layout(comms)

    def body(*refs):
        ro_refs = refs[:len(ro)]
        rw_refs = refs[len(ro) + len(rw):len(ro) + 2 * len(rw)]
        send, recv = refs[len(ro) + 2 * len(rw):]
        _comm_each(comms, "start", ro_refs, rw_refs, send, recv)
        _comm_each(comms, "finish", ro_refs, rw_refs, send, recv)

    rws = pl.pallas_call(
        body, name=name, in_specs=[ANY] * (len(ro) + len(rw)), out_specs=[ANY] * len(rw),
        out_shape=[jax.ShapeDtypeStruct(a.shape, a.dtype) for a in rw],
        input_output_aliases={len(ro) + k: k for k in range(len(rw))},
        scratch_shapes=[pltpu.SemaphoreType.DMA((n_sems,)), pltpu.SemaphoreType.DMA((n_sems,))],
    )(*ro, *rw)
    return _split_results(comms, rws)


def _pcall(body, args, *, name, grid, in_specs, out_specs, out_shape, scratch=(), sem=(), comms=(), aliases=None):
    n_in, n_out, n_scr = len(in_specs), len(out_specs), len(scratch)
    aliases = dict(aliases or {})
    if not comms:
        return pl.pallas_call(body, name=name, grid=grid, in_specs=list(in_specs), out_specs=list(out_specs),
                              out_shape=list(out_shape), scratch_shapes=list(scratch), input_output_aliases=aliases,
                              compiler_params=_params(*sem))(*args)
    ro, rw, n_sems = _comm_layout(comms)

    def carrier(*refs):
        ins = refs[:n_in]
        ro_refs = refs[n_in:n_in + len(ro)]
        o0 = n_in + len(ro) + len(rw)
        outs = refs[o0:o0 + n_out]
        rw_refs = refs[o0 + n_out:o0 + n_out + len(rw)]
        s0 = o0 + n_out + len(rw)
        send, recv = refs[s0 + n_scr], refs[s0 + n_scr + 1]
        ids = [pl.program_id(a) for a in range(len(grid))]
        first = functools.reduce(jnp.logical_and, [i == 0 for i in ids])
        last = functools.reduce(jnp.logical_and, [i == g - 1 for i, g in zip(ids, grid)])

        @pl.when(first)
        def _():
            _comm_each(comms, "start", ro_refs, rw_refs, send, recv)

        body(*ins, *outs, *refs[s0:s0 + n_scr])

        @pl.when(last)
        def _():
            _comm_each(comms, "finish", ro_refs, rw_refs, send, recv)

    res = pl.pallas_call(
        carrier, name=name, grid=grid, in_specs=list(in_specs) + [ANY] * (len(ro) + len(rw)),
        out_specs=list(out_specs) + [ANY] * len(rw),
        out_shape=list(out_shape) + [jax.ShapeDtypeStruct(a.shape, a.dtype) for a in rw],
        input_output_aliases={**aliases, **{n_in + len(ro) + k: n_out + k for k in range(len(rw))}},
        scratch_shapes=list(scratch) + [pltpu.SemaphoreType.DMA((n_sems,)), pltpu.SemaphoreType.DMA((n_sems,))],
        compiler_params=_params(*["arbitrary"] * len(grid)),
    )(*args, *ro, *rw)
    return list(res[:n_out]), _split_results(comms, res[n_out:])


def _matmul(a, b, mode, out_dtype, name, a_square=False, relu_out=False, mul2=None, comms=()):
    if mode == "nn":
        (m, k), n = a.shape, b.shape[1]
    elif mode == "nt":
        (m, k), n = a.shape, b.shape[0]
    else:
        (k, m), n = a.shape, b.shape[1]
    tm, tn, tk = _tile(m, 1024), _tile(n, 2048 if out_dtype == BF16 else 1024), _tile(k, 2048)
    nk = k // tk
    dims = {"nn": NN, "nt": NT, "tn": TN}[mode]
    a_spec = pl.BlockSpec((tk, tm), lambda i, j, kk: (kk, i)) if mode == "tn" else pl.BlockSpec((tm, tk), lambda i, j, kk: (i, kk))
    b_spec = pl.BlockSpec((tn, tk), lambda i, j, kk: (j, kk)) if mode == "nt" else pl.BlockSpec((tk, tn), lambda i, j, kk: (kk, j))
    o_spec = pl.BlockSpec((tm, tn), lambda i, j, kk: (i, j))

    def body(a_ref, b_ref, *rest):
        m_ref = None if mul2 is None else rest[0]
        o_ref = rest[0 if mul2 is None else 1]
        kk = pl.program_id(2)

        def partial():
            av = a_ref[...]
            if a_square:
                av = av * av
            return _dot(av, b_ref[...], dims)

        def finish(r):
            if relu_out:
                r = jnp.maximum(r, 0.0)
            if mul2 is not None:
                r = r * (2.0 * m_ref[...].astype(F32))
            o_ref[...] = r.astype(out_dtype)

        if nk == 1:
            finish(partial())
            return
        acc_ref = rest[-1]

        @pl.when(kk == 0)
        def _():
            acc_ref[...] = partial()

        @pl.when(kk > 0)
        def _():
            acc_ref[...] += partial()

        @pl.when(kk == nk - 1)
        def _():
            finish(acc_ref[...])

    args = (a, b) if mul2 is None else (a, b, mul2)
    specs = [a_spec, b_spec] + ([] if mul2 is None else [o_spec])
    res = _pcall(body, args, name=name, grid=(m // tm, n // tn, nk), in_specs=specs, out_specs=[o_spec],
                 out_shape=[jax.ShapeDtypeStruct((m, n), out_dtype)], scratch=[pltpu.VMEM((tm, tn), F32)] if nk > 1 else [],
                 sem=("parallel", "parallel", "arbitrary"), comms=comms)
    return (res[0][0], res[1]) if comms else res[0]


NORM_ROWS = 256


def _rms(x, g):
    rstd = lax.rsqrt(jnp.mean(x * x, axis=-1, keepdims=True) + RMS_EPS)
    n = x * rstd
    return n * g, n, rstd


def _rms_bwd(n, rstd, g, dout):
    dn = dout * g
    return rstd * (dn - n * jnp.mean(dn * n, axis=-1, keepdims=True))


def _row_spec(d):
    return pl.BlockSpec((NORM_ROWS, d), lambda i: (i, 0))


def _vec_spec(d):
    return pl.BlockSpec((1, d), lambda i: (0, 0))


def _accumulate(ref, val):
    @pl.when(pl.program_id(0) == 0)
    def _():
        ref[...] = jnp.zeros_like(ref)

    ref[...] += val


def _rms_fwd(x, g, name):
    t, d = x.shape

    def body(x_ref, g_ref, h_ref):
        h_ref[...] = _rms(x_ref[...], g_ref[...])[0].astype(BF16)

    return pl.pallas_call(
        body, name=name, grid=(t // NORM_ROWS,), in_specs=[_row_spec(d), _vec_spec(d)], out_specs=_row_spec(d),
        out_shape=jax.ShapeDtypeStruct((t, d), BF16), compiler_params=_params("parallel"),
    )(x, g)


def _post_pre_fwd(y, g_post, x, g_pre, name, comms=()):
    t, d = x.shape

    def body(y_ref, gp_ref, x_ref, gn_ref, xn_ref, h_ref):
        xn = x_ref[...] + _rms(y_ref[...], gp_ref[...])[0]
        xn_ref[...] = xn
        h_ref[...] = _rms(xn, gn_ref[...])[0].astype(BF16)

    return _pcall(
        body, (y, g_post, x, g_pre), name=name, grid=(t // NORM_ROWS,),
        in_specs=[_row_spec(d), _vec_spec(d), _row_spec(d), _vec_spec(d)], out_specs=[_row_spec(d), _row_spec(d)],
        out_shape=[jax.ShapeDtypeStruct((t, d), F32), jax.ShapeDtypeStruct((t, d), BF16)], sem=("parallel",), comms=comms)


def _final_fwd_bwd(y, g_post, x, target, name):
    t, d = x.shape

    def body(y_ref, g_ref, x_ref, t_ref, loss_ref, dx_ref, dy_ref, dg_ref):
        g = g_ref[...]
        out, n, rstd = _rms(y_ref[...], g)
        e = x_ref[...] + out - t_ref[...]
        _accumulate(loss_ref, jnp.full(loss_ref.shape, 0.5 / d, F32) * jnp.sum(e * e))
        dx = e * (1.0 / d)
        dx_ref[...] = dx
        dy_ref[...] = _rms_bwd(n, rstd, g, dx).astype(BF16)
        _accumulate(dg_ref, jnp.sum(dx * n, axis=0, keepdims=True))

    return pl.pallas_call(
        body, name=name, grid=(t // NORM_ROWS,),
        in_specs=[_row_spec(d), _vec_spec(d), _row_spec(d), _row_spec(d)],
        out_specs=[pl.BlockSpec((8, 128), lambda i: (0, 0)), _row_spec(d), _row_spec(d), _vec_spec(d)],
        out_shape=[jax.ShapeDtypeStruct((8, 128), F32), jax.ShapeDtypeStruct((t, d), F32),
                   jax.ShapeDtypeStruct((t, d), BF16), jax.ShapeDtypeStruct((1, d), F32)],
        compiler_params=_params("arbitrary"),
    )(y, g_post, x, target)


def _pre_post_bwd(x, g_pre, dh, dx_in, y, g_post, name, comms=()):
    t, d = x.shape
    both = y is not None

    def body(x_ref, gp_ref, dh_ref, dxi_ref, *rest):
        if both:
            y_ref, gq_ref, dx_ref, dy_ref, dgp_ref, dgq_ref = rest
        else:
            dx_ref, dgp_ref = rest
        gp = gp_ref[...]
        _, n, rstd = _rms(x_ref[...], gp)
        dh_v = dh_ref[...]
        dx = dxi_ref[...] + _rms_bwd(n, rstd, gp, dh_v)
        dx_ref[...] = dx
        _accumulate(dgp_ref, jnp.sum(dh_v * n, axis=0, keepdims=True))
        if both:
            gq = gq_ref[...]
            _, ny, rstdy = _rms(y_ref[...], gq)
            dy_ref[...] = _rms_bwd(ny, rstdy, gq, dx).astype(BF16)
            _accumulate(dgq_ref, jnp.sum(dx * ny, axis=0, keepdims=True))

    in_specs = [_row_spec(d), _vec_spec(d), _row_spec(d), _row_spec(d)]
    args = [x, g_pre, dh, dx_in]
    if both:
        in_specs += [_row_spec(d), _vec_spec(d)]
        args += [y, g_post]
        out_specs = [_row_spec(d), _row_spec(d), _vec_spec(d), _vec_spec(d)]
        out_shape = [jax.ShapeDtypeStruct((t, d), F32), jax.ShapeDtypeStruct((t, d), BF16),
                     jax.ShapeDtypeStruct((1, d), F32), jax.ShapeDtypeStruct((1, d), F32)]
    else:
        out_specs = [_row_spec(d), _vec_spec(d)]
        out_shape = [jax.ShapeDtypeStruct((t, d), F32), jax.ShapeDtypeStruct((1, d), F32)]
    return _pcall(body, args, name=name, grid=(t // NORM_ROWS,), in_specs=in_specs, out_specs=out_specs, out_shape=out_shape,
                  sem=("arbitrary",), comms=comms)


def _gelu(x):
    return 0.5 * x * (1.0 + lax.erf(x * 0.7071067811865476))


def _gelu_grad(x):
    return 0.5 * (1.0 + lax.erf(x * 0.7071067811865476)) + x * jnp.exp(-0.5 * x * x) * 0.3989422804014327


def _layernorm(v, g, b):
    mu = jnp.mean(v, axis=-1, keepdims=True)
    vc = v - mu
    rs = lax.rsqrt(jnp.mean(vc * vc, axis=-1, keepdims=True) + LN_EPS)
    vhat = vc * rs
    return vhat * g + b, vhat, rs


def _tril_mask():
    return lax.broadcasted_iota(jnp.int32, (CHUNK, CHUNK), 0) >= lax.broadcasted_iota(jnp.int32, (CHUNK, CHUNK), 1)


def _sgu_fwd(z, ln_g, ln_b, w16, bias_b, name, comms=()):
    t = z.shape[0]
    groups = w16.shape[0]
    a = groups * CHUNK

    def body(u_ref, v_ref, g_ref, b_ref, w_ref, bb_ref, o_ref):
        u = _gelu(u_ref[...].astype(F32))
        vn = _layernorm(_gelu(v_ref[...].astype(F32)), g_ref[...], b_ref[...])[0].astype(BF16)
        tril = _tril_mask()
        for g in range(groups):
            sl = slice(g * CHUNK, (g + 1) * CHUNK)
            w = jnp.where(tril, w_ref[g], jnp.zeros((), BF16))
            mixed = _dot(w, vn[:, sl], NN) + bb_ref[g]
            o_ref[:, sl] = (u[:, sl] * mixed).astype(BF16)

    full3 = pl.BlockSpec((groups, CHUNK, CHUNK), lambda c: (0, 0, 0))
    res = _pcall(
        body, (z, z, ln_g, ln_b, w16, bias_b), name=name, grid=(t // CHUNK,),
        in_specs=[pl.BlockSpec((CHUNK, a), lambda c: (c, 0)), pl.BlockSpec((CHUNK, a), lambda c: (c, 1)),
                  _vec_spec(a), _vec_spec(a), full3, full3],
        out_specs=[pl.BlockSpec((CHUNK, a), lambda c: (c, 0))], out_shape=[jax.ShapeDtypeStruct((t, a), BF16)],
        sem=("parallel",), comms=comms)
    return (res[0][0], res[1]) if comms else res[0]


def _sgu_bwd(z, dab, ln_g, ln_b, w16, bias_b, name, comms=()):
    t = z.shape[0]
    groups = w16.shape[0]
    a = groups * CHUNK

    def body(u_ref, v_ref, da_ref, g_ref, b_ref, w_ref, bb_ref, duv_ref, dg_ref, db_ref, dw_ref, dbs_ref, dvn_ref):
        up = u_ref[...].astype(F32)
        vp = v_ref[...].astype(F32)
        u = _gelu(up)
        ln_gain = g_ref[...]
        vn32, vhat, rs = _layernorm(_gelu(vp), ln_gain, b_ref[...])
        vn = vn32.astype(BF16)
        da = da_ref[...].astype(F32)
        tril = _tril_mask()
        ones = jnp.ones((8, CHUNK), F32)

        @pl.when(pl.program_id(0) == 0)
        def _():
            dw_ref[...] = jnp.zeros_like(dw_ref)
            dbs_ref[...] = jnp.zeros_like(dbs_ref)

        for g in range(groups):
            sl = slice(g * CHUNK, (g + 1) * CHUNK)
            w = jnp.where(tril, w_ref[g], jnp.zeros((), BF16))
            mixed = _dot(w, vn[:, sl], NN) + bb_ref[g]
            dmix = da[:, sl] * u[:, sl]
            dmix16 = dmix.astype(BF16)
            duv_ref[:, sl] = (da[:, sl] * mixed * _gelu_grad(up[:, sl])).astype(BF16)
            dvn_ref[:, sl] = _dot(w, dmix16, TN)
            dw_ref[g] += jnp.where(tril, _dot(dmix16, vn[:, sl], NT), 0.0)
            dbs_ref[g:g + 1, :] += lax.dot_general(ones, dmix, (NT, ((), ())), precision=lax.Precision.HIGHEST,
                                                   preferred_element_type=F32)[0:1]
        dvn = dvn_ref[...]
        dvhat = dvn * ln_gain
        dva = rs * (dvhat - jnp.mean(dvhat, axis=-1, keepdims=True) - vhat * jnp.mean(dvhat * vhat, axis=-1, keepdims=True))
        duv_ref[:, a:] = (dva * _gelu_grad(vp)).astype(BF16)
        _accumulate(dg_ref, jnp.sum(dvn * vhat, axis=0, keepdims=True))
        _accumulate(db_ref, jnp.sum(dvn, axis=0, keepdims=True))

    full3 = pl.BlockSpec((groups, CHUNK, CHUNK), lambda c: (0, 0, 0))
    return _pcall(
        body, (z, z, dab, ln_g, ln_b, w16, bias_b), name=name, grid=(t // CHUNK,),
        in_specs=[pl.BlockSpec((CHUNK, a), lambda c: (c, 0)), pl.BlockSpec((CHUNK, a), lambda c: (c, 1)),
                  pl.BlockSpec((CHUNK, a), lambda c: (c, 0)), _vec_spec(a), _vec_spec(a), full3, full3],
        out_specs=[pl.BlockSpec((CHUNK, 2 * a), lambda c: (c, 0)), _vec_spec(a), _vec_spec(a), full3,
                   pl.BlockSpec((groups, CHUNK), lambda c: (0, 0))],
        out_shape=[jax.ShapeDtypeStruct((t, 2 * a), BF16), jax.ShapeDtypeStruct((1, a), F32), jax.ShapeDtypeStruct((1, a), F32),
                   jax.ShapeDtypeStruct((groups, CHUNK, CHUNK), F32), jax.ShapeDtypeStruct((groups, CHUNK), F32)],
        scratch=[pltpu.VMEM((CHUNK, a), F32)], sem=("arbitrary",), comms=comms)


def _dil_masks(d):
    qi = lax.broadcasted_iota(jnp.int32, (CHUNK, CHUNK), 0)
    kj = lax.broadcasted_iota(jnp.int32, (CHUNK, CHUNK), 1)
    dist_c = qi - kj
    return dist_c >= 0, dist_c <= 0, (dist_c * d).astype(F32), ((dist_c + CHUNK) * d).astype(F32)


def _alibi_slope(h, heads):
    return 2.0 ** (-8.0 * (h + 1) / heads)


def _dil_view(z, d):
    t, w = z.shape[0], z.shape[1] // 5
    if d == 1:
        return z, 5, 2
    return z[:, 2 * w:].reshape(t // d, d * 3 * w), 3, 0


def _dil_fwd(z, d, name, comms=()):
    t = z.shape[0]
    w = z.shape[1] // 5
    heads = w // HEAD_DIM
    nb = t // d // CHUNK
    scale = HEAD_DIM ** -0.5
    zv, mult, col_q = _dil_view(z, d)

    def body(q_ref, kp_ref, kc_ref, vp_ref, vc_ref, o_ref, l_ref):
        ok_c, ok_p0, bias_c, bias_p = _dil_masks(d)
        ok_p = ok_p0 & (pl.program_id(1) > 0)
        hs = range(heads)
        sl = [slice(h * HEAD_DIM, (h + 1) * HEAD_DIM) for h in hs]
        slope = [_alibi_slope(h, heads) for h in hs]
        ones = jnp.ones((CHUNK, HEAD_DIM), BF16)
        s_c = [_dot(q_ref[:, sl[h]], kc_ref[:, sl[h]], NT) for h in hs]
        s_p = [_dot(q_ref[:, sl[h]], kp_ref[:, sl[h]], NT) for h in hs]
        s_c = [jnp.where(ok_c, s_c[h] * scale - slope[h] * bias_c, NEG) for h in hs]
        s_p = [jnp.where(ok_p, s_p[h] * scale - slope[h] * bias_p, NEG) for h in hs]
        m = [jnp.max(jnp.maximum(s_c[h], s_p[h]), axis=1, keepdims=True) for h in hs]
        p_c = [jnp.exp(s_c[h] - m[h]).astype(BF16) for h in hs]
        p_p = [jnp.exp(s_p[h] - m[h]).astype(BF16) for h in hs]
        den = [_dot(p_c[h], ones, NN) + _dot(p_p[h], ones, NN) for h in hs]
        o = [_dot(p_c[h], vc_ref[:, sl[h]], NN) + _dot(p_p[h], vp_ref[:, sl[h]], NN) for h in hs]
        l_ref[...] = jnp.zeros_like(l_ref)
        for h in hs:
            o_ref[:, sl[h]] = (o[h] / den[h]).astype(BF16)
            l_ref[:, h:h + 1] = m[h] + jnp.log(den[h][:, 0:1])

    def zspec(col, prev):
        if prev:
            return pl.BlockSpec((CHUNK, w), lambda r, n: (jnp.maximum(n - 1, 0), r * mult + col_q + col))
        return pl.BlockSpec((CHUNK, w), lambda r, n: (n, r * mult + col_q + col))

    res = _pcall(
        body, (zv, zv, zv, zv, zv), name=name, grid=(d, nb),
        in_specs=[zspec(0, False), zspec(1, True), zspec(1, False), zspec(2, True), zspec(2, False)],
        out_specs=[pl.BlockSpec((CHUNK, w), lambda r, n: (n, r)), pl.BlockSpec((CHUNK, HEAD_DIM), lambda r, n: (n, r))],
        out_shape=[jax.ShapeDtypeStruct((t // d, d * w), BF16), jax.ShapeDtypeStruct((t // d, d * HEAD_DIM), F32)],
        sem=("parallel", "parallel"), comms=comms)
    (o, lse), rws = res if comms else (res, None)
    outs = (o.reshape(t, w), lse.reshape(t, HEAD_DIM))
    return (outs, rws) if comms else outs


def _dil_merge(a_out, outs, lses, name, comms=()):
    t, a = a_out.shape
    w = outs[0].shape[1]
    heads = w // HEAD_DIM
    nbr = len(outs)

    def body(a_ref, *rest):
        o_refs, l_refs, (ab_ref, lt_ref) = rest[:nbr], rest[nbr:2 * nbr], rest[2 * nbr:]
        ls = [r[...] for r in l_refs]
        m = functools.reduce(jnp.maximum, ls)
        ws = [jnp.exp(l - m) for l in ls]
        tot = functools.reduce(jnp.add, ws)
        ws = [wt / tot for wt in ws]
        ab_ref[:, :a] = a_ref[...]
        for h in range(heads):
            sl = slice(h * HEAD_DIM, (h + 1) * HEAD_DIM)
            mix = functools.reduce(jnp.add, [wt[:, h:h + 1] * r[:, sl].astype(F32) for wt, r in zip(ws, o_refs)])
            ab_ref[:, a + h * HEAD_DIM:a + (h + 1) * HEAD_DIM] = mix.astype(BF16)
        lt_ref[...] = m + jnp.log(tot)

    return _pcall(
        body, (a_out, *outs, *lses), name=name, grid=(t // NORM_ROWS,),
        in_specs=[_row_spec(a)] + [_row_spec(w)] * nbr + [_row_spec(HEAD_DIM)] * nbr,
        out_specs=[_row_spec(a + w), _row_spec(HEAD_DIM)],
        out_shape=[jax.ShapeDtypeStruct((t, a + w), BF16), jax.ShapeDtypeStruct((t, HEAD_DIM), F32)],
        sem=("parallel",), comms=comms)


def _dil_delta(ab, dab, name):
    t, aw = ab.shape
    w = aw // 2
    heads = w // HEAD_DIM

    def body(o_ref, do_ref, dl_ref):
        dl_ref[...] = jnp.zeros_like(dl_ref)
        for h in range(heads):
            sl = slice(h * HEAD_DIM, (h + 1) * HEAD_DIM)
            dl_ref[:, h:h + 1] = jnp.sum(do_ref[:, sl].astype(F32) * o_ref[:, sl].astype(F32), axis=1, keepdims=True)

    half = pl.BlockSpec((NORM_ROWS, w), lambda i: (i, 1))
    return pl.pallas_call(body, name=name, grid=(t // NORM_ROWS,), in_specs=[half, half], out_specs=_row_spec(HEAD_DIM),
                          out_shape=jax.ShapeDtypeStruct((t, HEAD_DIM), F32), compiler_params=_params("parallel"))(ab, dab)


def _dil_bwd(z, dab, ltot, delta, d, name, comms=()):
    t = z.shape[0]
    w = z.shape[1] // 5
    heads = w // HEAD_DIM
    nb = t // d // CHUNK
    scale = HEAD_DIM ** -0.5

    def body(q_ref, qn_ref, kp_ref, kc_ref, vp_ref, vc_ref, do_ref, don_ref, l_ref, ln_ref, dl_ref, dln_ref,
             dq_ref, dk_ref, dv_ref):
        n = pl.program_id(1)
        ok_c, ok_p0, bias_c, bias_p = _dil_masks(d)
        ok_p = ok_p0 & (n > 0)
        ok_n = ok_p0 & (n < nb - 1)
        hs = range(heads)
        sl = [slice(h * HEAD_DIM, (h + 1) * HEAD_DIM) for h in hs]
        slope = [_alibi_slope(h, heads) for h in hs]
        q, qn = [q_ref[:, s] for s in sl], [qn_ref[:, s] for s in sl]
        kp, kc = [kp_ref[:, s] for s in sl], [kc_ref[:, s] for s in sl]
        vp, vc = [vp_ref[:, s] for s in sl], [vc_ref[:, s] for s in sl]
        do, don = [do_ref[:, s] for s in sl], [don_ref[:, s] for s in sl]
        s_c = [_dot(q[h], kc[h], NT) for h in hs]
        s_p = [_dot(q[h], kp[h], NT) for h in hs]
        s_n = [_dot(qn[h], kc[h], NT) for h in hs]
        dp_c = [_dot(do[h], vc[h], NT) for h in hs]
        dp_p = [_dot(do[h], vp[h], NT) for h in hs]
        dp_n = [_dot(don[h], vc[h], NT) for h in hs]
        delta = [dl_ref[:, h:h + 1] for h in hs]
        delta_n = [dln_ref[:, h:h + 1] for h in hs]
        p_c = [jnp.exp(jnp.where(ok_c, s_c[h] * scale - slope[h] * bias_c, NEG) - l_ref[:, h:h + 1]) for h in hs]
        p_p = [jnp.exp(jnp.where(ok_p, s_p[h] * scale - slope[h] * bias_p, NEG) - l_ref[:, h:h + 1]) for h in hs]
        p_n = [jnp.exp(jnp.where(ok_n, s_n[h] * scale - slope[h] * bias_p, NEG) - ln_ref[:, h:h + 1]) for h in hs]
        ds_c = [(p_c[h] * (dp_c[h] - delta[h])).astype(BF16) for h in hs]
        ds_p = [(p_p[h] * (dp_p[h] - delta[h])).astype(BF16) for h in hs]
        ds_n = [(p_n[h] * (dp_n[h] - delta_n[h])).astype(BF16) for h in hs]
        dq = [_dot(ds_c[h], kc[h], NN) + _dot(ds_p[h], kp[h], NN) for h in hs]
        dk = [_dot(ds_c[h], q[h], TN) + _dot(ds_n[h], qn[h], TN) for h in hs]
        dv = [_dot(p_c[h].astype(BF16), do[h], TN) + _dot(p_n[h].astype(BF16), don[h], TN) for h in hs]
        for h in hs:
            dq_ref[:, sl[h]] = (dq[h] * scale).astype(BF16)
            dk_ref[:, sl[h]] = (dk[h] * scale).astype(BF16)
            dv_ref[:, sl[h]] = dv[h].astype(BF16)

    def spec(mult, col, shift, width=w):
        if shift < 0:
            return pl.BlockSpec((CHUNK, width), lambda r, n: (jnp.maximum(n - 1, 0), r * mult + col))
        if shift > 0:
            return pl.BlockSpec((CHUNK, width), lambda r, n: (jnp.minimum(n + 1, nb - 1), r * mult + col))
        return pl.BlockSpec((CHUNK, width), lambda r, n: (n, r * mult + col))

    zv, mult, cq = _dil_view(z, d)
    dov = dab[:, w:].reshape(t // d, d * w)
    lv = ltot.reshape(t // d, d * HEAD_DIM)
    dlv = delta.reshape(t // d, d * HEAD_DIM)
    ospec = spec(1, 0, 0)
    res = _pcall(
        body, (zv, zv, zv, zv, zv, zv, dov, dov, lv, lv, dlv, dlv), name=name, grid=(d, nb),
        in_specs=[spec(mult, cq, 0), spec(mult, cq, 1), spec(mult, cq + 1, -1), spec(mult, cq + 1, 0),
                  spec(mult, cq + 2, -1), spec(mult, cq + 2, 0), spec(1, 0, 0), spec(1, 0, 1),
                  spec(1, 0, 0, HEAD_DIM), spec(1, 0, 1, HEAD_DIM), spec(1, 0, 0, HEAD_DIM), spec(1, 0, 1, HEAD_DIM)],
        out_specs=[ospec, ospec, ospec], out_shape=[jax.ShapeDtypeStruct((t // d, d * w), BF16)] * 3,
        sem=("parallel", "parallel"), comms=comms)
    outs, rws = res if comms else (res, None)
    outs = [o.reshape(t, w) for o in outs]
    return (outs, rws) if comms else outs


def _dz_assemble(duv, parts, name):
    t, a2 = duv.shape
    w = parts[0][0].shape[1]
    nbr = len(parts)

    def body(duv_ref, *rest):
        refs, dz_ref = rest[:-1], rest[-1]
        dz_ref[:, :a2] = duv_ref[...]
        for i in range(3):
            tot = functools.reduce(jnp.add, [refs[b * 3 + i][...].astype(F32) for b in range(nbr)])
            dz_ref[:, a2 + i * w:a2 + (i + 1) * w] = tot.astype(BF16)

    flat = [p for branch in parts for p in branch]
    return pl.pallas_call(
        body, name=name, grid=(t // NORM_ROWS,), in_specs=[_row_spec(a2)] + [_row_spec(w)] * len(flat),
        out_specs=_row_spec(a2 + 3 * w), out_shape=jax.ShapeDtypeStruct((t, a2 + 3 * w), BF16),
        compiler_params=_params("parallel"),
    )(duv, *flat)


def _split_dot(x, m16):
    hi = x.astype(BF16)
    lo = (x - hi.astype(F32)).astype(BF16)
    return _dot(hi, m16, NN) + _dot(lo, m16, NN)


SB_DEAD = -110.0


def _sb_scaled(q):
    return (q.astype(F32) * (HEAD_DIM ** -0.5)).astype(BF16)


SB_PAIR = 2


def _sb_logs(qs, kj, below):
    zt = [_dot(q, k, NT) for q, k in zip(qs, kj)]
    sp = [jnp.maximum(z, 0.0) + jnp.log(1.0 + jnp.exp(-jnp.abs(z))) for z in zt]
    return [z - s for z, s in zip(zt, sp)], [(-s if below is None else jnp.where(below, -s, 0.0)) for s in sp]


def _sb_alive(s, i, c_run):
    return (s <= i) & (jnp.max(c_run) > SB_DEAD)


def _sb_fwd(zc, name, comms=()):
    t = zc.shape[0]
    c = zc.shape[1] // 3
    heads = c // HEAD_DIM
    blk = min(SB_BLOCK, t)

    def body(q_ref, k_ref, v_ref, o_ref, ct_ref, nb_ref):
        i = pl.program_id(1)
        sl = [slice(p * HEAD_DIM, (p + 1) * HEAD_DIM) for p in range(SB_PAIR)]
        qs = [_sb_scaled(q_ref[:, s]) for s in sl]
        rows = lax.broadcasted_iota(jnp.int32, (blk, blk), 0)
        cols = lax.broadcasted_iota(jnp.int32, (blk, blk), 1)
        below = rows > cols
        m_right = below.astype(BF16)

        def tile(carry, diagonal):
            s, acc, c_run = carry[0], carry[1:1 + SB_PAIR], carry[1 + SB_PAIR:]
            off = pl.multiple_of((i - s) * blk, blk)
            log_beta, l = _sb_logs(qs, [k_ref[pl.ds(off, blk), p] for p in sl], below if diagonal else None)
            right = [_split_dot(x, m_right) for x in l]
            a = [jnp.exp(lb + (c + r)) for lb, c, r in zip(log_beta, c_run, right)]
            if diagonal:
                a = [jnp.where(below, x, 0.0) for x in a]
            acc = [o + _dot(x.astype(BF16), v_ref[pl.ds(off, blk), p], NN) for o, x, p in zip(acc, a, sl)]
            return (s + 1, *acc, *[c + jnp.sum(x, axis=1, keepdims=True) for c, x in zip(c_run, l)])

        zeros = [jnp.zeros((blk, HEAD_DIM), F32)] * SB_PAIR + [jnp.zeros((blk, 1), F32)] * SB_PAIR
        out = lax.while_loop(lambda carry: _sb_alive(carry[0], i, functools.reduce(jnp.maximum, carry[1 + SB_PAIR:])),
                             lambda carry: tile(carry, False), tile((jnp.int32(0), *zeros), True))
        for p, s in enumerate(sl):
            o_ref[:, s] = out[1 + p].astype(BF16)
            ct_ref[:, s] = jnp.broadcast_to(out[1 + SB_PAIR + p], (blk, HEAD_DIM))
        nb_ref[...] = jnp.zeros(nb_ref.shape, F32) + out[0].astype(F32)

    pairs = heads // SB_PAIR
    qspec = pl.BlockSpec((blk, SB_PAIR * HEAD_DIM), lambda h, i: (i, h))
    return _pcall(body, (zc, zc, zc), name=name, grid=(pairs, t // blk),
                  in_specs=[qspec, pl.BlockSpec((t, SB_PAIR * HEAD_DIM), lambda h, i: (0, pairs + h)),
                            pl.BlockSpec((t, SB_PAIR * HEAD_DIM), lambda h, i: (0, 2 * pairs + h))],
                  out_specs=[qspec, qspec, qspec],
                  out_shape=[jax.ShapeDtypeStruct((t, c), BF16), jax.ShapeDtypeStruct((t, c), F32), jax.ShapeDtypeStruct((t, c), F32)],
                  sem=("parallel", "parallel"), comms=comms)


def _sb_bwd(zc, ctot, swept, do, name, comms=()):
    t = zc.shape[0]
    c = zc.shape[1] // 3
    heads = c // HEAD_DIM
    blk = min(SB_BLOCK, t)
    scale = HEAD_DIM ** -0.5

    def body(q_ref, k_ref, v_ref, ct_ref, nb_ref, do_ref, dq_ref, dk_ref, dv_ref):
        i = pl.program_id(1)

        @pl.when(i == 0)
        def _():
            dk_ref[...] = jnp.zeros_like(dk_ref)
            dv_ref[...] = jnp.zeros_like(dv_ref)

        ps = range(SB_PAIR)
        sl = [slice(p * HEAD_DIM, (p + 1) * HEAD_DIM) for p in ps]
        qs = [_sb_scaled(q_ref[:, s]) for s in sl]
        dov = [do_ref[:, s] for s in sl]
        c_tot = [ct_ref[:, p * HEAD_DIM:p * HEAD_DIM + 1] for p in ps]
        n_blocks = jnp.clip(jnp.max(nb_ref[0:8, :]).astype(jnp.int32), 1, i + 1)
        rows = lax.broadcasted_iota(jnp.int32, (blk, blk), 0)
        cols = lax.broadcasted_iota(jnp.int32, (blk, blk), 1)
        below = rows > cols
        m_upto = (rows <= cols).astype(BF16)
        m_left = (rows < cols).astype(BF16)

        def tile(j, carry, diagonal):
            dq, l_run, w_run = carry[:SB_PAIR], carry[SB_PAIR:2 * SB_PAIR], carry[2 * SB_PAIR:]
            off = pl.multiple_of(j * blk, blk)
            kj = [k_ref[pl.ds(off, blk), s] for s in sl]
            vj = [v_ref[pl.ds(off, blk), s] for s in sl]
            log_beta, l = _sb_logs(qs, kj, below if diagonal else None)
            d_a = [_dot(dov[p], vj[p], NT) for p in ps]
            upto = [_split_dot(x, m_upto) for x in l]
            a = [jnp.exp(log_beta[p] + (c_tot[p] - l_run[p] - upto[p])) for p in ps]
            if diagonal:
                a = [jnp.where(below, x, 0.0) for x in a]
            wgt = [a[p] * d_a[p] for p in ps]
            before = [w_run[p] + _split_dot(wgt[p], m_left) for p in ps]
            dz = [wgt[p] * jnp.exp(l[p]) - jnp.exp(log_beta[p]) * before[p] for p in ps]
            if diagonal:
                dz = [jnp.where(below, x, 0.0) for x in dz]
            dz16 = [x.astype(BF16) for x in dz]
            dk = [_dot(dz16[p], qs[p], TN) for p in ps]
            dv = [_dot(a[p].astype(BF16), dov[p], TN) for p in ps]
            dq = [dq[p] + _dot(dz16[p], kj[p], NN) for p in ps]
            for p in ps:
                dk_ref[pl.ds(off, blk), sl[p]] += dk[p]
                dv_ref[pl.ds(off, blk), sl[p]] += dv[p]
            return (*dq, *[l_run[p] + jnp.sum(l[p], axis=1, keepdims=True) for p in ps],
                    *[w_run[p] + jnp.sum(wgt[p], axis=1, keepdims=True) for p in ps])

        zeros = [jnp.zeros((blk, HEAD_DIM), F32)] * SB_PAIR + [jnp.zeros((blk, 1), F32)] * (2 * SB_PAIR)
        carry = lax.fori_loop(i + 1 - n_blocks, i, lambda j, carry: tile(j, carry, False), tuple(zeros))
        out = tile(i, carry, True)
        for p in ps:
            dq_ref[:, sl[p]] = out[p] * scale

    pairs = heads // SB_PAIR
    qspec = pl.BlockSpec((blk, SB_PAIR * HEAD_DIM), lambda h, i: (i, h))
    full = pl.BlockSpec((t, SB_PAIR * HEAD_DIM), lambda h, i: (0, h))
    return _pcall(body, (zc, zc, zc, ctot, swept, do), name=name, grid=(pairs, t // blk),
                  in_specs=[qspec, pl.BlockSpec((t, SB_PAIR * HEAD_DIM), lambda h, i: (0, pairs + h)),
                            pl.BlockSpec((t, SB_PAIR * HEAD_DIM), lambda h, i: (0, 2 * pairs + h)), qspec, qspec, qspec],
                  out_specs=[qspec, full, full], out_shape=[jax.ShapeDtypeStruct((t, c), F32)] * 3,
                  sem=("arbitrary", "arbitrary"), comms=comms)


def _concat_bf16(parts, name, comms=()):
    t, c = parts[0].shape

    def body(*refs):
        for k, r in enumerate(refs[:-1]):
            refs[-1][:, k * c:(k + 1) * c] = r[...].astype(BF16)

    res = _pcall(body, tuple(parts), name=name, grid=(t // NORM_ROWS,), in_specs=[_row_spec(c)] * len(parts),
                 out_specs=[_row_spec(c * len(parts))], out_shape=[jax.ShapeDtypeStruct((t, c * len(parts)), BF16)],
                 sem=("parallel",), comms=comms)
    return (res[0][0], res[1]) if comms else res[0]


KIND = {"ab_w_in": "col", "ab_w_out": "row", "sb_w_in": "col", "sb_w_out": "row",
        "ffn_w1_0": "col", "ffn_w1_1": "col", "ffn_w2_0": "row", "ffn_w2_1": "row"}
X_Y, DIAG, CHIPS = (2, 4), (6,), (2, 4, 6)


def _local_step(x, target, norms, sgu, big, bufs=None):
    g = {k: [v[l:l + 1] for l in range(2)] for k, v in norms.items()}
    ln_g, ln_b, sgu_w, sgu_b = sgu
    groups = sgu_w.shape[0]
    w16 = sgu_w.astype(BF16)
    bias_b = jnp.broadcast_to(sgu_b[:, :, None], (groups, CHUNK, CHUNK))
    big, dws, psum, dist = dict(big), {}, {}, bufs is not None
    pair, got = (dict(bufs[0]), dict(bufs[1])) if dist else ({}, {})

    def run(fn, *args, ops=(), **kw):
        if not dist or not ops:
            return fn(*args, **kw)
        make = {"gs": lambda k, p, *part: _GatherSend(big[k], KIND[k], p, *part), "gf": lambda k, p: _GatherFwd(big[k], KIND[k], p),
                "swap": lambda k, p: _PairSwap(dws[k], pair[k], KIND[k]),
                "chips": lambda k, p, *part: _ChipScatter(psum[k], got[k], p, *part)}
        out, rws = fn(*args, comms=[make[op[0]](*op[1:]) for op in ops], **kw)
        for (op, k, *_), r in zip(ops, rws):
            if op in ("gs", "gf"):
                big[k] = r[0]
            elif op == "swap":
                psum[k] = _pair_sum(dws[k], r[0], KIND[k], f"pair_sum_{k}")
            else:
                got[k] = r[0]
        return out

    h1_0 = _rms_fwd(x, g["pre_mix"][0], "rms_in")
    z0 = run(_matmul, h1_0, big["ab_w_in"], "nn", BF16, "ab_in", ops=[("gs", "ffn_w1_0", X_Y)])
    a_out = run(_sgu_fwd, z0, ln_g, ln_b, w16, bias_b, "sgu_fwd", ops=[("gf", "ffn_w1_0", X_Y), ("gs", "ab_w_out", CHIPS)])
    branches = [run(_dil_fwd, z0, 1, "dil_fwd_1", ops=[("gs", "ffn_w1_0", DIAG, (0, 2)), ("gf", "ab_w_out", CHIPS)]),
                run(_dil_fwd, z0, 4, "dil_fwd_4", ops=[("gs", "ffn_w1_0", DIAG, (1, 2))]),
                run(_dil_fwd, z0, 16, "dil_fwd_16", ops=[("gf", "ffn_w1_0", DIAG), ("gs", "ffn_w2_0", X_Y, (0, 2))])]
    ab, ltot = run(_dil_merge, a_out, [b[0] for b in branches], [b[1] for b in branches], "dil_merge",
                   ops=[("gs", "ffn_w2_0", X_Y, (1, 2))])
    y_0 = run(_matmul, ab, big["ab_w_out"], "nn", F32, "ab_out", ops=[("gs", "ffn_w2_0", DIAG, (0, 2))])
    x1, h2_0 = run(_post_pre_fwd, y_0, g["post_mix"][0], x, g["pre_ffn"][0], "norm_mix0", ops=[("gs", "ffn_w2_0", DIAG, (1, 2))])
    r_0 = run(_matmul, h2_0, big["ffn_w1_0"], "nn", BF16, "ffn_up_0", relu_out=True,
              ops=[("gf", "ffn_w2_0", CHIPS), ("gs", "sb_w_in", CHIPS)])
    y2_0 = run(_matmul, r_0, big["ffn_w2_0"], "nn", F32, "ffn_down_0", a_square=True,
               ops=[("gf", "sb_w_in", CHIPS), ("gs", "sb_w_out", CHIPS), ("gs", "ffn_w1_1", X_Y)])
    x2, h1_1 = run(_post_pre_fwd, y2_0, g["post_ffn"][0], x1, g["pre_mix"][1], "norm_ffn0",
                   ops=[("gf", "ffn_w1_1", X_Y), ("gf", "sb_w_out", CHIPS)])
    zc = run(_matmul, h1_1, big["sb_w_in"], "nn", BF16, "sb_in", ops=[("gs", "ffn_w1_1", DIAG)])
    o_sb, ct_sb, nb_sb = run(_sb_fwd, zc, "sb_fwd", ops=[("gf", "ffn_w1_1", DIAG), ("gs", "ffn_w2_1", CHIPS)])
    y_1 = run(_matmul, o_sb, big["sb_w_out"], "nn", F32, "sb_out", ops=[("gf", "ffn_w2_1", CHIPS)])
    x3, h2_1 = _post_pre_fwd(y_1, g["post_mix"][1], x2, g["pre_ffn"][1], "norm_mix1")
    r_1 = _matmul(h2_1, big["ffn_w1_1"], "nn", BF16, "ffn_up_1", relu_out=True)
    y2_1 = _matmul(r_1, big["ffn_w2_1"], "nn", F32, "ffn_down_1", a_square=True)
    loss, dx4, dy2_1, dg_post_ffn1 = _final_fwd_bwd(y2_1, g["post_ffn"][1], x3, target, "loss")

    da = _matmul(dy2_1, big["ffn_w2_1"], "nt", BF16, "ffn_da_1", mul2=r_1)
    dws["ffn_w2_1"] = _matmul(r_1, dy2_1, "tn", BF16, "ffn_dw2_1", a_square=True)
    dh2 = run(_matmul, da, big["ffn_w1_1"], "nt", F32, "ffn_dh_1", ops=[("swap", "ffn_w2_1", None)])
    dws["ffn_w1_1"] = run(_matmul, h2_1, da, "tn", BF16, "ffn_dw1_1", ops=[("chips", "ffn_w2_1", X_Y)])
    dx3, dy_1, dg_pre_ffn1, dg_post_mix1 = run(_pre_post_bwd, x3, g["pre_ffn"][1], dh2, dx4, y_1, g["post_mix"][1], "norm_bwd_mix1",
                                               ops=[("swap", "ffn_w1_1", None)])
    do_sb = _matmul(dy_1, big["sb_w_out"], "nt", BF16, "sb_out_dx")
    dws["sb_w_out"] = _matmul(o_sb, dy_1, "tn", BF16, "sb_out_dw")
    dqkv = run(_sb_bwd, zc, ct_sb, nb_sb, do_sb, "sb_bwd",
               ops=[("chips", "ffn_w2_1", DIAG), ("chips", "ffn_w1_1", CHIPS), ("swap", "sb_w_out", None)])
    dzc = run(_concat_bf16, dqkv, "sb_dz", ops=[("chips", "sb_w_out", X_Y)])
    dh1 = run(_matmul, dzc, big["sb_w_in"], "nt", F32, "sb_in_dx", ops=[("chips", "sb_w_out", DIAG)])
    dws["sb_w_in"] = _matmul(h1_1, dzc, "tn", BF16, "sb_in_dw")
    dx2, dy2_0, dg_pre_mix1, dg_post_ffn0 = run(_pre_post_bwd, x2, g["pre_mix"][1], dh1, dx3, y2_0, g["post_ffn"][0], "norm_bwd_ffn0",
                                                ops=[("swap", "sb_w_in", None)])
    da = run(_matmul, dy2_0, big["ffn_w2_0"], "nt", BF16, "ffn_da_0", mul2=r_0, ops=[("chips", "sb_w_in", X_Y)])
    dws["ffn_w2_0"] = run(_matmul, r_0, dy2_0, "tn", BF16, "ffn_dw2_0", a_square=True, ops=[("chips", "sb_w_in", DIAG)])
    dws["ffn_w1_0"] = run(_matmul, h2_0, da, "tn", BF16, "ffn_dw1_0", ops=[("swap", "ffn_w2_0", None)])
    dh2 = run(_matmul, da, big["ffn_w1_0"], "nt", F32, "ffn_dh_0", ops=[("chips", "ffn_w2_0", X_Y), ("swap", "ffn_w1_0", None)])
    dx1, dy_0, dg_pre_ffn0, dg_post_mix0 = run(_pre_post_bwd, x1, g["pre_ffn"][0], dh2, dx2, y_0, g["post_mix"][0], "norm_bwd_mix0",
                                               ops=[("chips", "ffn_w2_0", DIAG, (0, 2))])
    dab = run(_matmul, dy_0, big["ab_w_out"], "nt", BF16, "ab_out_dx", ops=[("chips", "ffn_w2_0", DIAG, (1, 2))])
    dws["ab_w_out"] = run(_matmul, ab, dy_0, "tn", BF16, "ab_out_dw", ops=[("chips", "ffn_w1_0", X_Y, (0, 2))])
    duv, d_ln_g, d_ln_b, d_sgu_w, d_sgu_b = run(_sgu_bwd, z0, dab, ln_g, ln_b, w16, bias_b, "sgu_bwd",
                                                ops=[("chips", "ffn_w1_0", X_Y, (1, 2))])
    delta = _dil_delta(ab, dab, "dil_delta")
    parts = [run(_dil_bwd, z0, dab, ltot, delta, 1, "dil_bwd_1", ops=[("chips", "ffn_w1_0", DIAG, (0, 2)), ("swap", "ab_w_out", None)]),
             run(_dil_bwd, z0, dab, ltot, delta, 4, "dil_bwd_4", ops=[("chips", "ffn_w1_0", DIAG, (1, 2))]),
             run(_dil_bwd, z0, dab, ltot, delta, 16, "dil_bwd_16", ops=[("chips", "ab_w_out", CHIPS)])]
    dz0 = _dz_assemble(duv, parts, "dz_assemble")
    dws["ab_w_in"] = _matmul(h1_0, dz0, "tn", BF16, "ab_in_dw")
    dh1 = run(_matmul, dz0, big["ab_w_in"], "nt", F32, "ab_in_dx", ops=[("swap", "ab_w_in", None)])
    grad_x, dg_pre_mix0 = run(_pre_post_bwd, x, g["pre_mix"][0], dh1, dx1, None, None, "norm_bwd_in", ops=[("chips", "ab_w_in", X_Y)])

    d_norms = {
        "pre_mix": jnp.concatenate([dg_pre_mix0, dg_pre_mix1]), "post_mix": jnp.concatenate([dg_post_mix0, dg_post_mix1]),
        "pre_ffn": jnp.concatenate([dg_pre_ffn0, dg_pre_ffn1]), "post_ffn": jnp.concatenate([dg_post_ffn0, dg_post_ffn1]),
    }
    return loss, grad_x, d_norms, (d_ln_g, d_ln_b, d_sgu_w, d_sgu_b), (psum, got) if dist else dws


def _to_bf16_full(w, layer, kind, name):
    _, rows, cols = w.shape
    tr = _tile(rows, 512)
    nblk = rows // tr
    full = (rows, 4 * cols) if kind == "col" else (4 * rows, cols)

    def body(w_ref, o_ref):
        o_ref[...] = w_ref[...].astype(BF16)

    def place(i):
        mine = 2 * lax.axis_index("x") + lax.axis_index("y")
        return (i, mine) if kind == "col" else (mine * nblk + i, 0)

    return pl.pallas_call(
        body, name=name, grid=(nblk,), in_specs=[pl.BlockSpec((None, tr, cols), lambda i: (layer, i, 0))],
        out_specs=pl.BlockSpec((tr, cols), place), out_shape=jax.ShapeDtypeStruct(full, BF16), compiler_params=_params("parallel"),
    )(w)


def _pair_sum(dw16, pair, kind, name):
    rh, cs = _half_shape(dw16.shape, kind)
    tr = _tile(rh, 256)
    nblk = rh // tr

    def body(dw_ref, pair_ref, o_ref):
        o_ref[...] = (dw_ref[...].astype(F32) + pair_ref[...].astype(F32)).astype(BF16)

    def own(s, i):
        c = lax.axis_index("c")
        return (c * nblk + i, s) if kind == "col" else ((2 * s + c) * nblk + i, 0)

    spec3 = pl.BlockSpec((None, tr, cs), lambda s, i: (s, i, 0))
    return pl.pallas_call(
        body, name=name, grid=(4, nblk), in_specs=[pl.BlockSpec((tr, cs), own), spec3], out_specs=spec3,
        out_shape=jax.ShapeDtypeStruct((4, rh, cs), BF16), compiler_params=_params("parallel", "parallel"),
    )(dw16, pair)


def _owner_sum(psum, got, buf, layer, name, comms=()):
    _, rh, cs = psum.shape
    tr = _tile(rh, 256)

    def body(p_ref, got_ref, buf_ref, o_ref):
        tot = p_ref[...].astype(F32)
        for j in range(3):
            tot = tot + got_ref[j].astype(F32)
        o_ref[...] = tot

    res = _pcall(
        body, (psum, got, buf), name=name, grid=(rh // tr,),
        in_specs=[pl.BlockSpec((None, tr, cs), lambda i: (2 * lax.axis_index("x") + lax.axis_index("y"), i, 0)),
                  pl.BlockSpec((3, tr, cs), lambda i: (0, i, 0)), ANY],
        out_specs=[pl.BlockSpec((None, None, tr, cs), lambda i: (layer, lax.axis_index("c"), i, 0))],
        out_shape=[jax.ShapeDtypeStruct(buf.shape, F32)], sem=("parallel",), comms=comms, aliases={2: 0})
    return (res[0][0], res[1]) if comms else res[0]


def _adamw_math(w, g, m, v):
    m = ADAM_B1 * m + (1.0 - ADAM_B1) * g
    v = ADAM_B2 * v + (1.0 - ADAM_B2) * (g * g)
    m_hat = m / (1.0 - ADAM_B1 ** ADAM_STEP)
    v_hat = v / (1.0 - ADAM_B2 ** ADAM_STEP)
    return -ADAM_LR * (m_hat / (jnp.sqrt(v_hat) + ADAM_EPS) + ADAM_WD * w), m, v


def _adamw(w, g, m, v, name):
    layers, rows, cols = w.shape
    tr = _tile(rows, 256)

    def body(w_ref, g_ref, m_ref, v_ref, go_ref, d_ref, mo_ref, vo_ref):
        g = g_ref[...]
        go_ref[...] = g
        d_ref[...], mo_ref[...], vo_ref[...] = _adamw_math(w_ref[...], g, m_ref[...], v_ref[...])

    spec = pl.BlockSpec((None, tr, cols), lambda l, i: (l, i, 0))
    return _pcall(body, (w, g, m, v), name=name, grid=(layers, rows // tr), in_specs=[spec] * 4, out_specs=[spec] * 4,
                  out_shape=[jax.ShapeDtypeStruct(w.shape, F32)] * 4, sem=("parallel", "parallel"))


def _pack(arrays):
    flat = jnp.concatenate([a.reshape(-1) for a in arrays])
    pad = (-flat.shape[0]) % 1024
    return jnp.pad(flat, (0, pad)).reshape(-1, 128)


def _unpack(packed, like):
    flat = packed.reshape(-1)
    out, off = [], 0
    for a in like:
        out.append(flat[off:off + a.size].reshape(a.shape))
        off += a.size
    return out


class _SmallGather:
    n_sems = 7

    def __init__(self, g, parts):
        self.ro, self.rw = [g], [parts]

    def start(self, ro, rw, send, recv):
        x, y, c, _ = _place()
        for j in range(1, 8):
            _remote(ro[0], rw[0].at[4 * x + 2 * y + c], send(j - 1), recv(j - 1), _flip(x, y, c, j)).start()

    def finish(self, ro, rw, send, recv):
        x, y, c, _ = _place()
        for j in range(1, 8):
            px, py, pc = _flip(x, y, c, j)
            slot = rw[0].at[4 * px + 2 * py + pc]
            cp = _remote(slot, slot, send(j - 1), recv(j - 1), (x, y, c))
            cp.wait_recv()
            cp.wait_send()


def _small_update(own, parts, w, m, v, name):
    rows = w.shape[0]

    def body(own_ref, p_ref, w_ref, m_ref, v_ref, g_ref, d_ref, mo_ref, vo_ref):
        me = 4 * lax.axis_index("x") + 2 * lax.axis_index("y") + lax.axis_index("c")
        g = jnp.where(me == 0, own_ref[...], p_ref[0])
        for k in range(1, 8):
            g = g + jnp.where(me == k, own_ref[...], p_ref[k])
        g_ref[...] = g
        d_ref[...], mo_ref[...], vo_ref[...] = _adamw_math(w_ref[...], g, m_ref[...], v_ref[...])

    return pl.pallas_call(body, name=name, out_shape=[jax.ShapeDtypeStruct((rows, 128), F32)] * 4,
                          compiler_params=_params())(own, parts, w, m, v)


SMALL = ("norm_pre_mix", "norm_post_mix", "norm_pre_ffn", "norm_post_ffn", "sgu_ln_g", "sgu_ln_b", "sgu_w", "sgu_b")
BIG = (("ab_w_in", ("ab_w_in",)), ("ab_w_out", ("ab_w_out",)), ("sb_w_in", ("sb_w_in",)), ("sb_w_out", ("sb_w_out",)),
       ("ffn_w1", ("ffn_w1_0", "ffn_w1_1")), ("ffn_w2", ("ffn_w2_0", "ffn_w2_1")))
WEIGHTS = ("norm_pre_mix", "norm_post_mix", "norm_pre_ffn", "norm_post_ffn", "ab_w_in", "sgu_ln_g", "sgu_ln_b", "sgu_w", "sgu_b",
           "ab_w_out", "sb_w_in", "sb_w_out", "ffn_w1", "ffn_w2")


def kernel(x, norm_pre_mix, norm_post_mix, norm_pre_ffn, norm_post_ffn, ab_w_in, sgu_ln_g, sgu_ln_b, sgu_w, sgu_b, ab_w_out, sb_w_in, sb_w_out, ffn_w1, ffn_w2, loss_target, m_norm_pre_mix, m_norm_post_mix, m_norm_pre_ffn, m_norm_post_ffn, m_ab_w_in, m_sgu_ln_g, m_sgu_ln_b, m_sgu_w, m_sgu_b, m_ab_w_out, m_sb_w_in, m_sb_w_out, m_ffn_w1, m_ffn_w2, v_norm_pre_mix, v_norm_post_mix, v_norm_pre_ffn, v_norm_post_ffn, v_ab_w_in, v_sgu_ln_g, v_sgu_ln_b, v_sgu_w, v_sgu_b, v_ab_w_out, v_sb_w_in, v_sb_w_out, v_ffn_w1, v_ffn_w2):
    w = dict(norm_pre_mix=norm_pre_mix, norm_post_mix=norm_post_mix, norm_pre_ffn=norm_pre_ffn, norm_post_ffn=norm_post_ffn,
             ab_w_in=ab_w_in, sgu_ln_g=sgu_ln_g, sgu_ln_b=sgu_ln_b, sgu_w=sgu_w, sgu_b=sgu_b, ab_w_out=ab_w_out, sb_w_in=sb_w_in,
             sb_w_out=sb_w_out, ffn_w1=ffn_w1, ffn_w2=ffn_w2)
    m = dict(norm_pre_mix=m_norm_pre_mix, norm_post_mix=m_norm_post_mix, norm_pre_ffn=m_norm_pre_ffn, norm_post_ffn=m_norm_post_ffn,
             ab_w_in=m_ab_w_in, sgu_ln_g=m_sgu_ln_g, sgu_ln_b=m_sgu_ln_b, sgu_w=m_sgu_w, sgu_b=m_sgu_b, ab_w_out=m_ab_w_out,
             sb_w_in=m_sb_w_in, sb_w_out=m_sb_w_out, ffn_w1=m_ffn_w1, ffn_w2=m_ffn_w2)
    v = dict(norm_pre_mix=v_norm_pre_mix, norm_post_mix=v_norm_post_mix, norm_pre_ffn=v_norm_pre_ffn, norm_post_ffn=v_norm_post_ffn,
             ab_w_in=v_ab_w_in, sgu_ln_g=v_sgu_ln_g, sgu_ln_b=v_sgu_ln_b, sgu_w=v_sgu_w, sgu_b=v_sgu_b, ab_w_out=v_ab_w_out,
             sb_w_in=v_sb_w_in, sb_w_out=v_sb_w_out, ffn_w1=v_ffn_w1, ffn_w2=v_ffn_w2)
    big, pair, got = {}, {}, {}
    for name, keys in BIG:
        for layer, key in enumerate(keys):
            big[key] = _to_bf16_full(w[name], layer, KIND[key], f"bf16_{key}")
            half = _half_shape(big[key].shape, KIND[key])
            pair[key], got[key] = lax.empty((4,) + half, BF16), lax.empty((3,) + half, BF16)
    big["ab_w_in"] = _comm_call([_Gather(big["ab_w_in"], KIND["ab_w_in"])], "gather_first")[0][0]

    norms = {k: w["norm_" + k] for k in ("pre_mix", "post_mix", "pre_ffn", "post_ffn")}
    sgu = (sgu_ln_g, sgu_ln_b, sgu_w[0], sgu_b[0])
    loss_blk, grad_x, d_norms, d_sgu, (psum, got) = _local_step(x[0], loss_target[0], norms, sgu, big, (pair, got))
    loss = lax.psum(loss_blk[0, 0], ("x", "y", "c"))

    grads, deltas, new_m, new_v = {}, {}, {}, {}
    keys_of = dict(BIG)
    small_g = _pack([d_norms["pre_mix"], d_norms["post_mix"], d_norms["pre_ffn"], d_norms["post_ffn"],
                     d_sgu[0], d_sgu[1], d_sgu[2][None], d_sgu[3][None]])
    parts = None
    bufs, pending = {}, None
    for name in ("ffn_w2", "ffn_w1", "sb_w_in", "sb_w_out", "ab_w_out"):
        buf = lax.empty((len(keys_of[name]), 2) + psum[keys_of[name][0]].shape[1:], F32)
        for layer, key in enumerate(keys_of[name]):
            if pending is not None:
                buf, rws = _owner_sum(psum[key], got[key], buf, layer, f"sum_{key}", comms=[_Join([bufs[pending]])])
                bufs[pending], pending = rws[0][0], None
            elif parts is None:
                buf, rws = _owner_sum(psum[key], got[key], buf, layer, f"sum_{key}",
                                      comms=[_SmallGather(small_g, lax.empty((8,) + small_g.shape, F32))])
                parts = rws[0][0]
            else:
                buf = _owner_sum(psum[key], got[key], buf, layer, f"sum_{key}")
        bufs[name], pending = buf, name

    rws = _comm_call([_Join([bufs["ab_w_out"]]), _ChipScatter(psum["ab_w_in"], got["ab_w_in"], DIAG)], "tail_comm")
    bufs["ab_w_out"], got["ab_w_in"] = rws[0][0], rws[1][0]
    bufs["ab_w_in"] = _owner_sum(psum["ab_w_in"], got["ab_w_in"], lax.empty((1, 2) + psum["ab_w_in"].shape[1:], F32), 0, "sum_ab_w_in")
    bufs["ab_w_in"] = _comm_call([_Join([bufs["ab_w_in"]])], "join_last")[0][0]
    for name, _ in BIG:
        grads[name], deltas[name], new_m[name], new_v[name] = _adamw(w[name], bufs[name].reshape(w[name].shape), m[name], v[name], f"adamw_{name}")

    outs = _small_update(small_g, parts, _pack([w[k] for k in SMALL]), _pack([m[k] for k in SMALL]), _pack([v[k] for k in SMALL]), "small_update")
    like = [w[k] for k in SMALL]
    for dst, packed in zip((grads, deltas, new_m, new_v), outs):
        for k, a in zip(SMALL, _unpack(packed, like)):
            dst[k] = a

    return (loss, grad_x[None], *[grads[k] for k in WEIGHTS], *[deltas[k] for k in WEIGHTS],
            *[new_m[k] for k in WEIGHTS], *[new_v[k] for k in WEIGHTS])
```

```python
import functools

import jax
import jax.numpy as jnp
from jax import lax
from jax.experimental import pallas as pl
from jax.experimental.pallas import tpu as pltpu

F32 = jnp.float32
BF16 = jnp.bfloat16
MESH = pl.DeviceIdType.MESH

HEAD_DIM = 128
CHUNK = 128
DILATIONS = (1, 4, 16)
SB_BLOCK = 256
RMS_EPS = 1e-6
LN_EPS = 1e-5
ADAM_LR, ADAM_B1, ADAM_B2, ADAM_EPS, ADAM_WD, ADAM_STEP = 0.001, 0.9, 0.999, 1e-08, 0.01, 10
NEG = -1e30
V7X_VMEM_LIMIT = 48 * 1024 * 1024
ANY = pl.BlockSpec(memory_space=pl.ANY)


def _params(*sem):
    return pltpu.CompilerParams(dimension_semantics=sem if sem else None, vmem_limit_bytes=V7X_VMEM_LIMIT)


def _tile(n, pref):
    if n <= pref:
        return n
    t = pref
    while n % t:
        t -= 128
    return t


def _dot(a, b, dims):
    return lax.dot_general(a, b, (dims, ((), ())), preferred_element_type=F32)


NN = ((1,), (0,))
NT = ((1,), (1,))
TN = ((0,), (0,))


def _place():
    x, y, c = lax.axis_index("x"), lax.axis_index("y"), lax.axis_index("c")
    return x, y, c, 2 * x + y


def _flip(x, y, c, j):
    return (1 - x if j & 4 else x), (1 - y if j & 2 else y), (1 - c if j & 1 else c)


def _half_shape(full_shape, kind):
    rows, cols = full_shape
    return (rows // 2, cols // 4) if kind == "col" else (rows // 8, cols)


def _half(ref, kind, s, h):
    rh, cs = _half_shape(ref.shape, kind)
    if kind == "col":
        return ref.at[pl.ds(h * rh, rh), pl.ds(s * cs, cs)]
    return ref.at[pl.ds((2 * s + h) * rh, rh), :]


def _remote(src, dst, send, recv, to):
    return pltpu.make_async_remote_copy(src_ref=src, dst_ref=dst, send_sem=send, recv_sem=recv, device_id=to, device_id_type=MESH)


class _Gather:
    n_sems = 6

    def __init__(self, full, kind):
        self.ro, self.rw, self.kind = [], [full], kind

    def start(self, ro, rw, send, recv):
        x, y, c, mine = _place()
        own = _half(rw[0], self.kind, mine, c)
        for k, j in enumerate((2, 4, 6)):
            px, py, _ = _flip(x, y, c, j)
            _remote(own, own, send(k), recv(k), (px, py, c)).start()

    def finish(self, ro, rw, send, recv):
        x, y, c, mine = _place()
        own = _half(rw[0], self.kind, mine, c)
        for k, j in enumerate((2, 4, 6)):
            px, py, _ = _flip(x, y, c, j)
            got = _half(rw[0], self.kind, 2 * px + py, c)
            _remote(got, got, send(k), recv(k), (x, y, c)).wait_recv()
            _remote(got, got, send(3 + k), recv(3 + k), (x, y, 1 - c)).start()
        for k, j in enumerate((2, 4, 6)):
            px, py, _ = _flip(x, y, c, j)
            got = _half(rw[0], self.kind, 2 * px + py, 1 - c)
            _remote(got, got, send(3 + k), recv(3 + k), (x, y, c)).wait_recv()
        for k in range(6):
            _remote(own, own, send(k), recv(k), (x, y, c)).wait_send()


class _GatherSend:
    def __init__(self, full, kind, patterns, part=(0, 1)):
        self.ro, self.rw, self.kind, self.patterns, self.part, self.n_sems = [], [full], kind, patterns, part, len(patterns)

    def _rows(self, half):
        i, n = self.part
        rows = half.shape[0] // n
        return half.at[pl.ds(i * rows, rows), :]

    def start(self, ro, rw, send, recv):
        x, y, c, mine = _place()
        own = self._rows(_half(rw[0], self.kind, mine, c))
        for k, j in enumerate(self.patterns):
            px, py, _ = _flip(x, y, c, j)
            _remote(own, own, send(k), recv(k), (px, py, c)).start()

    def finish(self, ro, rw, send, recv):
        x, y, c, _ = _place()
        for k, j in enumerate(self.patterns):
            px, py, _ = _flip(x, y, c, j)
            got = self._rows(_half(rw[0], self.kind, 2 * px + py, c))
            cp = _remote(got, got, send(k), recv(k), (x, y, c))
            cp.wait_recv()
            cp.wait_send()


class _GatherFwd:
    def __init__(self, full, kind, patterns):
        self.ro, self.rw, self.kind, self.patterns, self.n_sems = [], [full], kind, patterns, len(patterns)

    def start(self, ro, rw, send, recv):
        x, y, c, _ = _place()
        for k, j in enumerate(self.patterns):
            px, py, _ = _flip(x, y, c, j)
            got = _half(rw[0], self.kind, 2 * px + py, c)
            _remote(got, got, send(k), recv(k), (x, y, 1 - c)).start()

    def finish(self, ro, rw, send, recv):
        x, y, c, _ = _place()
        for k, j in enumerate(self.patterns):
            px, py, _ = _flip(x, y, c, j)
            got = _half(rw[0], self.kind, 2 * px + py, 1 - c)
            cp = _remote(got, got, send(k), recv(k), (x, y, c))
            cp.wait_recv()
            cp.wait_send()


class _PairSwap:
    n_sems = 4

    def __init__(self, dw16, pair, kind):
        self.ro, self.rw, self.kind = [dw16], [pair], kind

    def start(self, ro, rw, send, recv):
        x, y, c, _ = _place()
        for s in range(4):
            _remote(_half(ro[0], self.kind, s, 1 - c), rw[0].at[s], send(s), recv(s), (x, y, 1 - c)).start()

    def finish(self, ro, rw, send, recv):
        x, y, c, _ = _place()
        for s in range(4):
            cp = _remote(rw[0].at[s], rw[0].at[s], send(s), recv(s), (x, y, c))
            cp.wait_recv()
            cp.wait_send()


class _ChipScatter:
    def __init__(self, psum, got, patterns, part=(0, 1)):
        self.ro, self.rw, self.patterns, self.part, self.n_sems = [psum], [got], patterns, part, len(patterns)

    def _rows(self, ref, slot):
        i, n = self.part
        rows = ref.shape[1] // n
        return ref.at[slot, pl.ds(i * rows, rows), :]

    def start(self, ro, rw, send, recv):
        x, y, c, _ = _place()
        for k, j in enumerate(self.patterns):
            px, py, _ = _flip(x, y, c, j)
            _remote(self._rows(ro[0], 2 * px + py), self._rows(rw[0], j // 2 - 1), send(k), recv(k), (px, py, c)).start()

    def finish(self, ro, rw, send, recv):
        x, y, c, _ = _place()
        for k, j in enumerate(self.patterns):
            slot = self._rows(rw[0], j // 2 - 1)
            cp = _remote(slot, slot, send(k), recv(k), (x, y, c))
            cp.wait_recv()
            cp.wait_send()


class _Join:
    def __init__(self, bufs):
        self.ro, self.rw, self.n_sems = [], list(bufs), sum(b.shape[0] for b in bufs)

    def _copies(self, rw, send, recv, slot):
        x, y, c, _ = _place()
        k = 0
        for ref in rw:
            for l in range(ref.shape[0]):
                yield _remote(ref.at[l, c], ref.at[l, slot(c)], send(k), recv(k), (x, y, 1 - c))
                k += 1

    def start(self, ro, rw, send, recv):
        for cp in self._copies(rw, send, recv, lambda c: c):
            cp.start()

    def finish(self, ro, rw, send, recv):
        for cp in self._copies(rw, send, recv, lambda c: 1 - c):
            cp.wait_recv()
        for cp in self._copies(rw, send, recv, lambda c: c):
            cp.wait_send()


def _comm_layout(comms):
    ro = [a for c in comms for a in c.ro]
    rw = [a for c in comms for a in c.rw]
    return ro, rw, sum(c.n_sems for c in comms)


def _comm_each(comms, method, ro_refs, rw_refs, send, recv):
    i_ro = i_rw = i_sem = 0
    for c in comms:
        getattr(c, method)(ro_refs[i_ro:i_ro + len(c.ro)], rw_refs[i_rw:i_rw + len(c.rw)],
                           lambda k, b=i_sem: send.at[b + k], lambda k, b=i_sem: recv.at[b + k])
        i_ro, i_rw, i_sem = i_ro + len(c.ro), i_rw + len(c.rw), i_sem + c.n_sems


def _split_results(comms, rws):
    out, i = [], 0
    for c in comms:
        out.append(list(rws[i:i + len(c.rw)]))
        i += len(c.rw)
    return out


def _comm_call(comms, name):
    ro, rw, n_sems = _comm_layout(comms)

    def body(*refs):
        ro_refs = refs[:len(ro)]
        rw_refs = refs[len(ro) + len(rw):len(ro) + 2 * len(rw)]
        send, recv = refs[len(ro) + 2 * len(rw):]
        _comm_each(comms, "start", ro_refs, rw_refs, send, recv)
        _comm_each(comms, "finish", ro_refs, rw_refs, send, recv)

    rws = pl.pallas_call(
        body, name=name, in_specs=[ANY] * (len(ro) + len(rw)), out_specs=[ANY] * len(rw),
        out_shape=[jax.ShapeDtypeStruct(a.shape, a.dtype) for a in rw],
        input_output_aliases={len(ro) + k: k for k in range(len(rw))},
        scratch_shapes=[pltpu.SemaphoreType.DMA((n_sems,)), pltpu.SemaphoreType.DMA((n_sems,))],
    )(*ro, *rw)
    return _split_results(comms, rws)


def _pcall(body, args, *, name, grid, in_specs, out_specs, out_shape, scratch=(), sem=(), comms=(), aliases=None):
    n_in, n_out, n_scr = len(in_specs), len(out_specs), len(scratch)
    aliases = dict(aliases or {})
    if not comms:
        return pl.pallas_call(body, name=name, grid=grid, in_specs=list(in_specs), out_specs=list(out_specs),
                              out_shape=list(out_shape), scratch_shapes=list(scratch), input_output_aliases=aliases,
                              compiler_params=_params(*sem))(*args)
    ro, rw, n_sems = _comm_layout(comms)

    def carrier(*refs):
        ins = refs[:n_in]
        ro_refs = refs[n_in:n_in + len(ro)]
        o0 = n_in + len(ro) + len(rw)
        outs = refs[o0:o0 + n_out]
        rw_refs = refs[o0 + n_out:o0 + n_out + len(rw)]
        s0 = o0 + n_out + len(rw)
        send, recv = refs[s0 + n_scr], refs[s0 + n_scr + 1]
        ids = [pl.program_id(a) for a in range(len(grid))]
        first = functools.reduce(jnp.logical_and, [i == 0 for i in ids])
        last = functools.reduce(jnp.logical_and, [i == g - 1 for i, g in zip(ids, grid)])

        @pl.when(first)
        def _():
            _comm_each(comms, "start", ro_refs, rw_refs, send, recv)

        body(*ins, *outs, *refs[s0:s0 + n_scr])

        @pl.when(last)
        def _():
            _comm_each(comms, "finish", ro_refs, rw_refs, send, recv)

    res = pl.pallas_call(
        carrier, name=name, grid=grid, in_specs=list(in_specs) + [ANY] * (len(ro) + len(rw)),
        out_specs=list(out_specs) + [ANY] * len(rw),
        out_shape=list(out_shape) + [jax.ShapeDtypeStruct(a.shape, a.dtype) for a in rw],
        input_output_aliases={**aliases, **{n_in + len(ro) + k: n_out + k for k in range(len(rw))}},
        scratch_shapes=list(scratch) + [pltpu.SemaphoreType.DMA((n_sems,)), pltpu.SemaphoreType.DMA((n_sems,))],
        compiler_params=_params(*["arbitrary"] * len(grid)),
    )(*args, *ro, *rw)
    return list(res[:n_out]), _split_results(comms, res[n_out:])


def _matmul(a, b, mode, out_dtype, name, a_square=False, relu_out=False, mul2=None, comms=()):
    if mode == "nn":
        (m, k), n = a.shape, b.shape[1]
    elif mode == "nt":
        (m, k), n = a.shape, b.shape[0]
    else:
        (k, m), n = a.shape, b.shape[1]
    tm, tn, tk = _tile(m, 1024), _tile(n, 2048 if out_dtype == BF16 else 1024), _tile(k, 2048)
    nk = k // tk
    dims = {"nn": NN, "nt": NT, "tn": TN}[mode]
    a_spec = pl.BlockSpec((tk, tm), lambda i, j, kk: (kk, i)) if mode == "tn" else pl.BlockSpec((tm, tk), lambda i, j, kk: (i, kk))
    b_spec = pl.BlockSpec((tn, tk), lambda i, j, kk: (j, kk)) if mode == "nt" else pl.BlockSpec((tk, tn), lambda i, j, kk: (kk, j))
    o_spec = pl.BlockSpec((tm, tn), lambda i, j, kk: (i, j))

    def body(a_ref, b_ref, *rest):
        m_ref = None if mul2 is None else rest[0]
        o_ref = rest[0 if mul2 is None else 1]
        kk = pl.program_id(2)

        def partial():
            av = a_ref[...]
            if a_square:
                av = av * av
            return _dot(av, b_ref[...], dims)

        def finish(r):
            if relu_out:
                r = jnp.maximum(r, 0.0)
            if mul2 is not None:
                r = r * (2.0 * m_ref[...].astype(F32))
            o_ref[...] = r.astype(out_dtype)

        if nk == 1:
            finish(partial())
            return
        acc_ref = rest[-1]

        @pl.when(kk == 0)
        def _():
            acc_ref[...] = partial()

        @pl.when(kk > 0)
        def _():
            acc_ref[...] += partial()

        @pl.when(kk == nk - 1)
        def _():
            finish(acc_ref[...])

    args = (a, b) if mul2 is None else (a, b, mul2)
    specs = [a_spec, b_spec] + ([] if mul2 is None else [o_spec])
    res = _pcall(body, args, name=name, grid=(m // tm, n // tn, nk), in_specs=specs, out_specs=[o_spec],
                 out_shape=[jax.ShapeDtypeStruct((m, n), out_dtype)], scratch=[pltpu.VMEM((tm, tn), F32)] if nk > 1 else [],
                 sem=("parallel", "parallel", "arbitrary"), comms=comms)
    return (res[0][0], res[1]) if comms else res[0]


NORM_ROWS = 256


def _rms(x, g):
    rstd = lax.rsqrt(jnp.mean(x * x, axis=-1, keepdims=True) + RMS_EPS)
    n = x * rstd
    return n * g, n, rstd


def _rms_bwd(n, rstd, g, dout):
    dn = dout * g
    return rstd * (dn - n * jnp.mean(dn * n, axis=-1, keepdims=True))


def _row_spec(d):
    return pl.BlockSpec((NORM_ROWS, d), lambda i: (i, 0))


def _vec_spec(d):
    return pl.BlockSpec((1, d), lambda i: (0, 0))


def _accumulate(ref, val):
    @pl.when(pl.program_id(0) == 0)
    def _():
        ref[...] = jnp.zeros_like(ref)

    ref[...] += val


def _rms_fwd(x, g, name):
    t, d = x.shape

    def body(x_ref, g_ref, h_ref):
        h_ref[...] = _rms(x_ref[...], g_ref[...])[0].astype(BF16)

    return pl.pallas_call(
        body, name=name, grid=(t // NORM_ROWS,), in_specs=[_row_spec(d), _vec_spec(d)], out_specs=_row_spec(d),
        out_shape=jax.ShapeDtypeStruct((t, d), BF16), compiler_params=_params("parallel"),
    )(x, g)


def _post_pre_fwd(y, g_post, x, g_pre, name, comms=()):
    t, d = x.shape

    def body(y_ref, gp_ref, x_ref, gn_ref, xn_ref, h_ref):
        xn = x_ref[...] + _rms(y_ref[...], gp_ref[...])[0]
        xn_ref[...] = xn
        h_ref[...] = _rms(xn, gn_ref[...])[0].astype(BF16)

    return _pcall(
        body, (y, g_post, x, g_pre), name=name, grid=(t // NORM_ROWS,),
        in_specs=[_row_spec(d), _vec_spec(d), _row_spec(d), _vec_spec(d)], out_specs=[_row_spec(d), _row_spec(d)],
        out_shape=[jax.ShapeDtypeStruct((t, d), F32), jax.ShapeDtypeStruct((t, d), BF16)], sem=("parallel",), comms=comms)


def _final_fwd_bwd(y, g_post, x, target, name):
    t, d = x.shape

    def body(y_ref, g_ref, x_ref, t_ref, loss_ref, dx_ref, dy_ref, dg_ref):
        g = g_ref[...]
        out, n, rstd = _rms(y_ref[...], g)
        e = x_ref[...] + out - t_ref[...]
        _accumulate(loss_ref, jnp.full(loss_ref.shape, 0.5 / d, F32) * jnp.sum(e * e))
        dx = e * (1.0 / d)
        dx_ref[...] = dx
        dy_ref[...] = _rms_bwd(n, rstd, g, dx).astype(BF16)
        _accumulate(dg_ref, jnp.sum(dx * n, axis=0, keepdims=True))

    return pl.pallas_call(
        body, name=name, grid=(t // NORM_ROWS,),
        in_specs=[_row_spec(d), _vec_spec(d), _row_spec(d), _row_spec(d)],
        out_specs=[pl.BlockSpec((8, 128), lambda i: (0, 0)), _row_spec(d), _row_spec(d), _vec_spec(d)],
        out_shape=[jax.ShapeDtypeStruct((8, 128), F32), jax.ShapeDtypeStruct((t, d), F32),
                   jax.ShapeDtypeStruct((t, d), BF16), jax.ShapeDtypeStruct((1, d), F32)],
        compiler_params=_params("arbitrary"),
    )(y, g_post, x, target)


def _pre_post_bwd(x, g_pre, dh, dx_in, y, g_post, name, comms=()):
    t, d = x.shape
    both = y is not None

    def body(x_ref, gp_ref, dh_ref, dxi_ref, *rest):
        if both:
            y_ref, gq_ref, dx_ref, dy_ref, dgp_ref, dgq_ref = rest
        else:
            dx_ref, dgp_ref = rest
        gp = gp_ref[...]
        _, n, rstd = _rms(x_ref[...], gp)
        dh_v = dh_ref[...]
        dx = dxi_ref[...] + _rms_bwd(n, rstd, gp, dh_v)
        dx_ref[...] = dx
        _accumulate(dgp_ref, jnp.sum(dh_v * n, axis=0, keepdims=True))
        if both:
            gq = gq_ref[...]
            _, ny, rstdy = _rms(y_ref[...], gq)
            dy_ref[...] = _rms_bwd(ny, rstdy, gq, dx).astype(BF16)
            _accumulate(dgq_ref, jnp.sum(dx * ny, axis=0, keepdims=True))

    in_specs = [_row_spec(d), _vec_spec(d), _row_spec(d), _row_spec(d)]
    args = [x, g_pre, dh, dx_in]
    if both:
        in_specs += [_row_spec(d), _vec_spec(d)]
        args += [y, g_post]
        out_specs = [_row_spec(d), _row_spec(d), _vec_spec(d), _vec_spec(d)]
        out_shape = [jax.ShapeDtypeStruct((t, d), F32), jax.ShapeDtypeStruct((t, d), BF16),
                     jax.ShapeDtypeStruct((1, d), F32), jax.ShapeDtypeStruct((1, d), F32)]
    else:
        out_specs = [_row_spec(d), _vec_spec(d)]
        out_shape = [jax.ShapeDtypeStruct((t, d), F32), jax.ShapeDtypeStruct((1, d), F32)]
    return _pcall(body, args, name=name, grid=(t // NORM_ROWS,), in_specs=in_specs, out_specs=out_specs, out_shape=out_shape,
                  sem=("arbitrary",), comms=comms)


def _gelu(x):
    return 0.5 * x * (1.0 + lax.erf(x * 0.7071067811865476))


def _gelu_grad(x):
    return 0.5 * (1.0 + lax.erf(x * 0.7071067811865476)) + x * jnp.exp(-0.5 * x * x) * 0.3989422804014327


def _layernorm(v, g, b):
    mu = jnp.mean(v, axis=-1, keepdims=True)
    vc = v - mu
    rs = lax.rsqrt(jnp.mean(vc * vc, axis=-1, keepdims=True) + LN_EPS)
    vhat = vc * rs
    return vhat * g + b, vhat, rs


def _tril_mask():
    return lax.broadcasted_iota(jnp.int32, (CHUNK, CHUNK), 0) >= lax.broadcasted_iota(jnp.int32, (CHUNK, CHUNK), 1)


def _sgu_fwd(z, ln_g, ln_b, w16, bias_b, name, comms=()):
    t = z.shape[0]
    groups = w16.shape[0]
    a = groups * CHUNK

    def body(u_ref, v_ref, g_ref, b_ref, w_ref, bb_ref, o_ref):
        u = _gelu(u_ref[...].astype(F32))
        vn = _layernorm(_gelu(v_ref[...].astype(F32)), g_ref[...], b_ref[...])[0].astype(BF16)
        tril = _tril_mask()
        for g in range(groups):
            sl = slice(g * CHUNK, (g + 1) * CHUNK)
            w = jnp.where(tril, w_ref[g], jnp.zeros((), BF16))
            mixed = _dot(w, vn[:, sl], NN) + bb_ref[g]
            o_ref[:, sl] = (u[:, sl] * mixed).astype(BF16)

    full3 = pl.BlockSpec((groups, CHUNK, CHUNK), lambda c: (0, 0, 0))
    res = _pcall(
        body, (z, z, ln_g, ln_b, w16, bias_b), name=name, grid=(t // CHUNK,),
        in_specs=[pl.BlockSpec((CHUNK, a), lambda c: (c, 0)), pl.BlockSpec((CHUNK, a), lambda c: (c, 1)),
                  _vec_spec(a), _vec_spec(a), full3, full3],
        out_specs=[pl.BlockSpec((CHUNK, a), lambda c: (c, 0))], out_shape=[jax.ShapeDtypeStruct((t, a), BF16)],
        sem=("parallel",), comms=comms)
    return (res[0][0], res[1]) if comms else res[0]


def _sgu_bwd(z, dab, ln_g, ln_b, w16, bias_b, name, comms=()):
    t = z.shape[0]
    groups = w16.shape[0]
    a = groups * CHUNK

    def body(u_ref, v_ref, da_ref, g_ref, b_ref, w_ref, bb_ref, duv_ref, dg_ref, db_ref, dw_ref, dbs_ref, dvn_ref):
        up = u_ref[...].astype(F32)
        vp = v_ref[...].astype(F32)
        u = _gelu(up)
        ln_gain = g_ref[...]
        vn32, vhat, rs = _layernorm(_gelu(vp), ln_gain, b_ref[...])
        vn = vn32.astype(BF16)
        da = da_ref[...].astype(F32)
        tril = _tril_mask()
        ones = jnp.ones((8, CHUNK), F32)

        @pl.when(pl.program_id(0) == 0)
        def _():
            dw_ref[...] = jnp.zeros_like(dw_ref)
            dbs_ref[...] = jnp.zeros_like(dbs_ref)

        for g in range(groups):
            sl = slice(g * CHUNK, (g + 1) * CHUNK)
            w = jnp.where(tril, w_ref[g], jnp.zeros((), BF16))
            mixed = _dot(w, vn[:, sl], NN) + bb_ref[g]
            dmix = da[:, sl] * u[:, sl]
            dmix16 = dmix.astype(BF16)
            duv_ref[:, sl] = (da[:, sl] * mixed * _gelu_grad(up[:, sl])).astype(BF16)
            dvn_ref[:, sl] = _dot(w, dmix16, TN)
            dw_ref[g] += jnp.where(tril, _dot(dmix16, vn[:, sl], NT), 0.0)
            dbs_ref[g:g + 1, :] += lax.dot_general(ones, dmix, (NT, ((), ())), precision=lax.Precision.HIGHEST,
                                                   preferred_element_type=F32)[0:1]
        dvn = dvn_ref[...]
        dvhat = dvn * ln_gain
        dva = rs * (dvhat - jnp.mean(dvhat, axis=-1, keepdims=True) - vhat * jnp.mean(dvhat * vhat, axis=-1, keepdims=True))
        duv_ref[:, a:] = (dva * _gelu_grad(vp)).astype(BF16)
        _accumulate(dg_ref, jnp.sum(dvn * vhat, axis=0, keepdims=True))
        _accumulate(db_ref, jnp.sum(dvn, axis=0, keepdims=True))

    full3 = pl.BlockSpec((groups, CHUNK, CHUNK), lambda c: (0, 0, 0))
    return _pcall(
        body, (z, z, dab, ln_g, ln_b, w16, bias_b), name=name, grid=(t // CHUNK,),
        in_specs=[pl.BlockSpec((CHUNK, a), lambda c: (c, 0)), pl.BlockSpec((CHUNK, a), lambda c: (c, 1)),
                  pl.BlockSpec((CHUNK, a), lambda c: (c, 0)), _vec_spec(a), _vec_spec(a), full3, full3],
        out_specs=[pl.BlockSpec((CHUNK, 2 * a), lambda c: (c, 0)), _vec_spec(a), _vec_spec(a), full3,
                   pl.BlockSpec((groups, CHUNK), lambda c: (0, 0))],
        out_shape=[jax.ShapeDtypeStruct((t, 2 * a), BF16), jax.ShapeDtypeStruct((1, a), F32), jax.ShapeDtypeStruct((1, a), F32),
                   jax.ShapeDtypeStruct((groups, CHUNK, CHUNK), F32), jax.ShapeDtypeStruct((groups, CHUNK), F32)],
        scratch=[pltpu.VMEM((CHUNK, a), F32)], sem=("arbitrary",), comms=comms)


def _dil_masks(d):
    qi = lax.broadcasted_iota(jnp.int32, (CHUNK, CHUNK), 0)
    kj = lax.broadcasted_iota(jnp.int32, (CHUNK, CHUNK), 1)
    dist_c = qi - kj
    return dist_c >= 0, dist_c <= 0, (dist_c * d).astype(F32), ((dist_c + CHUNK) * d).astype(F32)


def _alibi_slope(h, heads):
    return 2.0 ** (-8.0 * (h + 1) / heads)


def _dil_view(z, d):
    t, w = z.shape[0], z.shape[1] // 5
    if d == 1:
        return z, 5, 2
    return z[:, 2 * w:].reshape(t // d, d * 3 * w), 3, 0


def _dil_fwd(z, d, name, comms=()):
    t = z.shape[0]
    w = z.shape[1] // 5
    heads = w // HEAD_DIM
    nb = t // d // CHUNK
    scale = HEAD_DIM ** -0.5
    zv, mult, col_q = _dil_view(z, d)

    def body(q_ref, kp_ref, kc_ref, vp_ref, vc_ref, o_ref, l_ref):
        ok_c, ok_p0, bias_c, bias_p = _dil_masks(d)
        ok_p = ok_p0 & (pl.program_id(1) > 0)
        hs = range(heads)
        sl = [slice(h * HEAD_DIM, (h + 1) * HEAD_DIM) for h in hs]
        slope = [_alibi_slope(h, heads) for h in hs]
        ones = jnp.ones((CHUNK, HEAD_DIM), BF16)
        s_c = [_dot(q_ref[:, sl[h]], kc_ref[:, sl[h]], NT) for h in hs]
        s_p = [_dot(q_ref[:, sl[h]], kp_ref[:, sl[h]], NT) for h in hs]
        s_c = [jnp.where(ok_c, s_c[h] * scale - slope[h] * bias_c, NEG) for h in hs]
        s_p = [jnp.where(ok_p, s_p[h] * scale - slope[h] * bias_p, NEG) for h in hs]
        m = [jnp.max(jnp.maximum(s_c[h], s_p[h]), axis=1, keepdims=True) for h in hs]
        p_c = [jnp.exp(s_c[h] - m[h]).astype(BF16) for h in hs]
        p_p = [jnp.exp(s_p[h] - m[h]).astype(BF16) for h in hs]
        den = [_dot(p_c[h], ones, NN) + _dot(p_p[h], ones, NN) for h in hs]
        o = [_dot(p_c[h], vc_ref[:, sl[h]], NN) + _dot(p_p[h], vp_ref[:, sl[h]], NN) for h in hs]
        l_ref[...] = jnp.zeros_like(l_ref)
        for h in hs:
            o_ref[:, sl[h]] = (o[h] / den[h]).astype(BF16)
            l_ref[:, h:h + 1] = m[h] + jnp.log(den[h][:, 0:1])

    def zspec(col, prev):
        if prev:
            return pl.BlockSpec((CHUNK, w), lambda r, n: (jnp.maximum(n - 1, 0), r * mult + col_q + col))
        return pl.BlockSpec((CHUNK, w), lambda r, n: (n, r * mult + col_q + col))

    res = _pcall(
        body, (zv, zv, zv, zv, zv), name=name, grid=(d, nb),
        in_specs=[zspec(0, False), zspec(1, True), zspec(1, False), zspec(2, True), zspec(2, False)],
        out_specs=[pl.BlockSpec((CHUNK, w), lambda r, n: (n, r)), pl.BlockSpec((CHUNK, HEAD_DIM), lambda r, n: (n, r))],
        out_shape=[jax.ShapeDtypeStruct((t // d, d * w), BF16), jax.ShapeDtypeStruct((t // d, d * HEAD_DIM), F32)],
        sem=("parallel", "parallel"), comms=comms)
    (o, lse), rws = res if comms else (res, None)
    outs = (o.reshape(t, w), lse.reshape(t, HEAD_DIM))
    return (outs, rws) if comms else outs


def _dil_merge(a_out, outs, lses, name, comms=()):
    t, a = a_out.shape
    w = outs[0].shape[1]
    heads = w // HEAD_DIM
    nbr = len(outs)

    def body(a_ref, *rest):
        o_refs, l_refs, (ab_ref, lt_ref) = rest[:nbr], rest[nbr:2 * nbr], rest[2 * nbr:]
        ls = [r[...] for r in l_refs]
        m = functools.reduce(jnp.maximum, ls)
        ws = [jnp.exp(l - m) for l in ls]
        tot = functools.reduce(jnp.add, ws)
        ws = [wt / tot for wt in ws]
        ab_ref[:, :a] = a_ref[...]
        for h in range(heads):
            sl = slice(h * HEAD_DIM, (h + 1) * HEAD_DIM)
            mix = functools.reduce(jnp.add, [wt[:, h:h + 1] * r[:, sl].astype(F32) for wt, r in zip(ws, o_refs)])
            ab_ref[:, a + h * HEAD_DIM:a + (h + 1) * HEAD_DIM] = mix.astype(BF16)
        lt_ref[...] = m + jnp.log(tot)

    return _pcall(
        body, (a_out, *outs, *lses), name=name, grid=(t // NORM_ROWS,),
        in_specs=[_row_spec(a)] + [_row_spec(w)] * nbr + [_row_spec(HEAD_DIM)] * nbr,
        out_specs=[_row_spec(a + w), _row_spec(HEAD_DIM)],
        out_shape=[jax.ShapeDtypeStruct((t, a + w), BF16), jax.ShapeDtypeStruct((t, HEAD_DIM), F32)],
        sem=("parallel",), comms=comms)


def _dil_delta(ab, dab, name):
    t, aw = ab.shape
    w = aw // 2
    heads = w // HEAD_DIM

    def body(o_ref, do_ref, dl_ref):
        dl_ref[...] = jnp.zeros_like(dl_ref)
        for h in range(heads):
            sl = slice(h * HEAD_DIM, (h + 1) * HEAD_DIM)
            dl_ref[:, h:h + 1] = jnp.sum(do_ref[:, sl].astype(F32) * o_ref[:, sl].astype(F32), axis=1, keepdims=True)

    half = pl.BlockSpec((NORM_ROWS, w), lambda i: (i, 1))
    return pl.pallas_call(body, name=name, grid=(t // NORM_ROWS,), in_specs=[half, half], out_specs=_row_spec(HEAD_DIM),
                          out_shape=jax.ShapeDtypeStruct((t, HEAD_DIM), F32), compiler_params=_params("parallel"))(ab, dab)


def _dil_bwd(z, dab, ltot, delta, d, name, comms=()):
    t = z.shape[0]
    w = z.shape[1] // 5
    heads = w // HEAD_DIM
    nb = t // d // CHUNK
    scale = HEAD_DIM ** -0.5

    def body(q_ref, qn_ref, kp_ref, kc_ref, vp_ref, vc_ref, do_ref, don_ref, l_ref, ln_ref, dl_ref, dln_ref,
             dq_ref, dk_ref, dv_ref):
        n = pl.program_id(1)
        ok_c, ok_p0, bias_c, bias_p = _dil_masks(d)
        ok_p = ok_p0 & (n > 0)
        ok_n = ok_p0 & (n < nb - 1)
        hs = range(heads)
        sl = [slice(h * HEAD_DIM, (h + 1) * HEAD_DIM) for h in hs]
        slope = [_alibi_slope(h, heads) for h in hs]
        q, qn = [q_ref[:, s] for s in sl], [qn_ref[:, s] for s in sl]
        kp, kc = [kp_ref[:, s] for s in sl], [kc_ref[:, s] for s in sl]
        vp, vc = [vp_ref[:, s] for s in sl], [vc_ref[:, s] for s in sl]
        do, don = [do_ref[:, s] for s in sl], [don_ref[:, s] for s in sl]
        s_c = [_dot(q[h], kc[h], NT) for h in hs]
        s_p = [_dot(q[h], kp[h], NT) for h in hs]
        s_n = [_dot(qn[h], kc[h], NT) for h in hs]
        dp_c = [_dot(do[h], vc[h], NT) for h in hs]
        dp_p = [_dot(do[h], vp[h], NT) for h in hs]
        dp_n = [_dot(don[h], vc[h], NT) for h in hs]
        delta = [dl_ref[:, h:h + 1] for h in hs]
        delta_n = [dln_ref[:, h:h + 1] for h in hs]
        p_c = [jnp.exp(jnp.where(ok_c, s_c[h] * scale - slope[h] * bias_c, NEG) - l_ref[:, h:h + 1]) for h in hs]
        p_p = [jnp.exp(jnp.where(ok_p, s_p[h] * scale - slope[h] * bias_p, NEG) - l_ref[:, h:h + 1]) for h in hs]
        p_n = [jnp.exp(jnp.where(ok_n, s_n[h] * scale - slope[h] * bias_p, NEG) - ln_ref[:, h:h + 1]) for h in hs]
        ds_c = [(p_c[h] * (dp_c[h] - delta[h])).astype(BF16) for h in hs]
        ds_p = [(p_p[h] * (dp_p[h] - delta[h])).astype(BF16) for h in hs]
        ds_n = [(p_n[h] * (dp_n[h] - delta_n[h])).astype(BF16) for h in hs]
        dq = [_dot(ds_c[h], kc[h], NN) + _dot(ds_p[h], kp[h], NN) for h in hs]
        dk = [_dot(ds_c[h], q[h], TN) + _dot(ds_n[h], qn[h], TN) for h in hs]
        dv = [_dot(p_c[h].astype(BF16), do[h], TN) + _dot(p_n[h].astype(BF16), don[h], TN) for h in hs]
        for h in hs:
            dq_ref[:, sl[h]] = (dq[h] * scale).astype(BF16)
            dk_ref[:, sl[h]] = (dk[h] * scale).astype(BF16)
            dv_ref[:, sl[h]] = dv[h].astype(BF16)

    def spec(mult, col, shift, width=w):
        if shift < 0:
            return pl.BlockSpec((CHUNK, width), lambda r, n: (jnp.maximum(n - 1, 0), r * mult + col))
        if shift > 0:
            return pl.BlockSpec((CHUNK, width), lambda r, n: (jnp.minimum(n + 1, nb - 1), r * mult + col))
        return pl.BlockSpec((CHUNK, width), lambda r, n: (n, r * mult + col))

    zv, mult, cq = _dil_view(z, d)
    dov = dab[:, w:].reshape(t // d, d * w)
    lv = ltot.reshape(t // d, d * HEAD_DIM)
    dlv = delta.reshape(t // d, d * HEAD_DIM)
    ospec = spec(1, 0, 0)
    res = _pcall(
        body, (zv, zv, zv, zv, zv, zv, dov, dov, lv, lv, dlv, dlv), name=name, grid=(d, nb),
        in_specs=[spec(mult, cq, 0), spec(mult, cq, 1), spec(mult, cq + 1, -1), spec(mult, cq + 1, 0),
                  spec(mult, cq + 2, -1), spec(mult, cq + 2, 0), spec(1, 0, 0), spec(1, 0, 1),
                  spec(1, 0, 0, HEAD_DIM), spec(1, 0, 1, HEAD_DIM), spec(1, 0, 0, HEAD_DIM), spec(1, 0, 1, HEAD_DIM)],
        out_specs=[ospec, ospec, ospec], out_shape=[jax.ShapeDtypeStruct((t // d, d * w), BF16)] * 3,
        sem=("parallel", "parallel"), comms=comms)
    outs, rws = res if comms else (res, None)
    outs = [o.reshape(t, w) for o in outs]
    return (outs, rws) if comms else outs


def _dz_assemble(duv, parts, name):
    t, a2 = duv.shape
    w = parts[0][0].shape[1]
    nbr = len(parts)

    def body(duv_ref, *rest):
        refs, dz_ref = rest[:-1], rest[-1]
        dz_ref[:, :a2] = duv_ref[...]
        for i in range(3):
            tot = functools.reduce(jnp.add, [refs[b * 3 + i][...].astype(F32) for b in range(nbr)])
            dz_ref[:, a2 + i * w:a2 + (i + 1) * w] = tot.astype(BF16)

    flat = [p for branch in parts for p in branch]
    return pl.pallas_call(
        body, name=name, grid=(t // NORM_ROWS,), in_specs=[_row_spec(a2)] + [_row_spec(w)] * len(flat),
        out_specs=_row_spec(a2 + 3 * w), out_shape=jax.ShapeDtypeStruct((t, a2 + 3 * w), BF16),
        compiler_params=_params("parallel"),
    )(duv, *flat)


def _split_dot(x, m16):
    hi = x.astype(BF16)
    lo = (x - hi.astype(F32)).astype(BF16)
    return _dot(hi, m16, NN) + _dot(lo, m16, NN)


SB_DEAD = -110.0


def _sb_scaled(q):
    return (q.astype(F32) * (HEAD_DIM ** -0.5)).astype(BF16)


SB_PAIR = 2
SB_GROUP_FWD = 4


def _sb_logs(qs, kj, below):
    zt = [_dot(q, k, NT) for q, k in zip(qs, kj)]
    sp = [jnp.maximum(z, 0.0) + jnp.log(1.0 + jnp.exp(-jnp.abs(z))) for z in zt]
    return [z - s for z, s in zip(zt, sp)], [(-s if below is None else jnp.where(below, -s, 0.0)) for s in sp]


def _sb_alive(s, i, c_run):
    return (s <= i) & (jnp.max(c_run) > SB_DEAD)


def _sb_fwd(zc, name, comms=()):
    t = zc.shape[0]
    c = zc.shape[1] // 3
    heads = c // HEAD_DIM
    blk = min(SB_BLOCK, t)
    grp = SB_GROUP_FWD if heads % SB_GROUP_FWD == 0 else SB_PAIR

    def body(q_ref, k_ref, v_ref, o_ref, ct_ref, nb_ref):
        i = pl.program_id(1)
        sl = [slice(p * HEAD_DIM, (p + 1) * HEAD_DIM) for p in range(grp)]
        qs = [_sb_scaled(q_ref[:, s]) for s in sl]
        rows = lax.broadcasted_iota(jnp.int32, (blk, blk), 0)
        cols = lax.broadcasted_iota(jnp.int32, (blk, blk), 1)
        below = rows > cols
        m_right = below.astype(BF16)

        def tile(carry, diagonal):
            s, acc, c_run = carry[0], carry[1:1 + grp], carry[1 + grp:]
            off = pl.multiple_of((i - s) * blk, blk)
            log_beta, l = _sb_logs(qs, [k_ref[pl.ds(off, blk), p] for p in sl], below if diagonal else None)
            right = [_split_dot(x, m_right) for x in l]
            a = [jnp.exp(lb + (c + r)) for lb, c, r in zip(log_beta, c_run, right)]
            if diagonal:
                a = [jnp.where(below, x, 0.0) for x in a]
            acc = [o + _dot(x.astype(BF16), v_ref[pl.ds(off, blk), p], NN) for o, x, p in zip(acc, a, sl)]
            return (s + 1, *acc, *[c + jnp.sum(x, axis=1, keepdims=True) for c, x in zip(c_run, l)])

        zeros = [jnp.zeros((blk, HEAD_DIM), F32)] * grp + [jnp.zeros((blk, 1), F32)] * grp
        out = lax.while_loop(lambda carry: _sb_alive(carry[0], i, functools.reduce(jnp.maximum, carry[1 + grp:])),
                             lambda carry: tile(carry, False), tile((jnp.int32(0), *zeros), True))
        for p, s in enumerate(sl):
            o_ref[:, s] = out[1 + p].astype(BF16)
            ct_ref[:, s] = jnp.broadcast_to(out[1 + grp + p], (blk, HEAD_DIM))
        nb_ref[...] = jnp.zeros(nb_ref.shape, F32) + out[0].astype(F32)

    groups = heads // grp
    qspec = pl.BlockSpec((blk, grp * HEAD_DIM), lambda h, i: (i, h))
    return _pcall(body, (zc, zc, zc), name=name, grid=(groups, t // blk),
                  in_specs=[qspec, pl.BlockSpec((t, grp * HEAD_DIM), lambda h, i: (0, groups + h)),
                            pl.BlockSpec((t, grp * HEAD_DIM), lambda h, i: (0, 2 * groups + h))],
                  out_specs=[qspec, qspec, qspec],
                  out_shape=[jax.ShapeDtypeStruct((t, c), BF16), jax.ShapeDtypeStruct((t, c), F32), jax.ShapeDtypeStruct((t, c), F32)],
                  sem=("parallel", "parallel"), comms=comms)


def _sb_bwd(zc, ctot, swept, do, name, comms=()):
    t = zc.shape[0]
    c = zc.shape[1] // 3
    heads = c // HEAD_DIM
    blk = min(SB_BLOCK, t)
    scale = HEAD_DIM ** -0.5

    def body(q_ref, k_ref, v_ref, ct_ref, nb_ref, do_ref, dq_ref, dk_ref, dv_ref):
        i = pl.program_id(1)

        @pl.when(i == 0)
        def _():
            dk_ref[...] = jnp.zeros_like(dk_ref)
            dv_ref[...] = jnp.zeros_like(dv_ref)

        ps = range(SB_PAIR)
        sl = [slice(p * HEAD_DIM, (p + 1) * HEAD_DIM) for p in ps]
        qs = [_sb_scaled(q_ref[:, s]) for s in sl]
        dov = [do_ref[:, s] for s in sl]
        c_tot = [ct_ref[:, p * HEAD_DIM:p * HEAD_DIM + 1] for p in ps]
        n_blocks = jnp.clip(jnp.max(nb_ref[0:8, :]).astype(jnp.int32), 1, i + 1)
        rows = lax.broadcasted_iota(jnp.int32, (blk, blk), 0)
        cols = lax.broadcasted_iota(jnp.int32, (blk, blk), 1)
        below = rows > cols
        m_upto = (rows <= cols).astype(BF16)
        m_left = (rows < cols).astype(BF16)

        def tile(j, carry, diagonal):
            dq, l_run, w_run = carry[:SB_PAIR], carry[SB_PAIR:2 * SB_PAIR], carry[2 * SB_PAIR:]
            off = pl.multiple_of(j * blk, blk)
            kj = [k_ref[pl.ds(off, blk), s] for s in sl]
            vj = [v_ref[pl.ds(off, blk), s] for s in sl]
            log_beta, l = _sb_logs(qs, kj, below if diagonal else None)
            d_a = [_dot(dov[p], vj[p], NT) for p in ps]
            upto = [_split_dot(x, m_upto) for x in l]
            a = [jnp.exp(log_beta[p] + (c_tot[p] - l_run[p] - upto[p])) for p in ps]
            if diagonal:
                a = [jnp.where(below, x, 0.0) for x in a]
            wgt = [a[p] * d_a[p] for p in ps]
            before = [w_run[p] + _split_dot(wgt[p], m_left) for p in ps]
            dz = [wgt[p] * jnp.exp(l[p]) - jnp.exp(log_beta[p]) * before[p] for p in ps]
            if diagonal:
                dz = [jnp.where(below, x, 0.0) for x in dz]
            dz16 = [x.astype(BF16) for x in dz]
            dk = [_dot(dz16[p], qs[p], TN) for p in ps]
            dv = [_dot(a[p].astype(BF16), dov[p], TN) for p in ps]
            dq = [dq[p] + _dot(dz16[p], kj[p], NN) for p in ps]
            for p in ps:
                dk_ref[pl.ds(off, blk), sl[p]] += dk[p]
                dv_ref[pl.ds(off, blk), sl[p]] += dv[p]
            return (*dq, *[l_run[p] + jnp.sum(l[p], axis=1, keepdims=True) for p in ps],
                    *[w_run[p] + jnp.sum(wgt[p], axis=1, keepdims=True) for p in ps])

        zeros = [jnp.zeros((blk, HEAD_DIM), F32)] * SB_PAIR + [jnp.zeros((blk, 1), F32)] * (2 * SB_PAIR)
        carry = lax.fori_loop(i + 1 - n_blocks, i, lambda j, carry: tile(j, carry, False), tuple(zeros))
        out = tile(i, carry, True)
        for p in ps:
            dq_ref[:, sl[p]] = out[p] * scale

    pairs = heads // SB_PAIR
    qspec = pl.BlockSpec((blk, SB_PAIR * HEAD_DIM), lambda h, i: (i, h))
    full = pl.BlockSpec((t, SB_PAIR * HEAD_DIM), lambda h, i: (0, h))
    return _pcall(body, (zc, zc, zc, ctot, swept, do), name=name, grid=(pairs, t // blk),
                  in_specs=[qspec, pl.BlockSpec((t, SB_PAIR * HEAD_DIM), lambda h, i: (0, pairs + h)),
                            pl.BlockSpec((t, SB_PAIR * HEAD_DIM), lambda h, i: (0, 2 * pairs + h)), qspec, qspec, qspec],
                  out_specs=[qspec, full, full], out_shape=[jax.ShapeDtypeStruct((t, c), F32)] * 3,
                  sem=("arbitrary", "arbitrary"), comms=comms)


def _concat_bf16(parts, name, comms=()):
    t, c = parts[0].shape

    def body(*refs):
        for k, r in enumerate(refs[:-1]):
            refs[-1][:, k * c:(k + 1) * c] = r[...].astype(BF16)

    res = _pcall(body, tuple(parts), name=name, grid=(t // NORM_ROWS,), in_specs=[_row_spec(c)] * len(parts),
                 out_specs=[_row_spec(c * len(parts))], out_shape=[jax.ShapeDtypeStruct((t, c * len(parts)), BF16)],
                 sem=("parallel",), comms=comms)
    return (res[0][0], res[1]) if comms else res[0]


KIND = {"ab_w_in": "col", "ab_w_out": "row", "sb_w_in": "col", "sb_w_out": "row",
        "ffn_w1_0": "col", "ffn_w1_1": "col", "ffn_w2_0": "row", "ffn_w2_1": "row"}
X_Y, DIAG, CHIPS = (2, 4), (6,), (2, 4, 6)


def _local_step(x, target, norms, sgu, big, bufs=None):
    g = {k: [v[l:l + 1] for l in range(2)] for k, v in norms.items()}
    ln_g, ln_b, sgu_w, sgu_b = sgu
    groups = sgu_w.shape[0]
    w16 = sgu_w.astype(BF16)
    bias_b = jnp.broadcast_to(sgu_b[:, :, None], (groups, CHUNK, CHUNK))
    big, dws, psum, dist = dict(big), {}, {}, bufs is not None
    pair, got = (dict(bufs[0]), dict(bufs[1])) if dist else ({}, {})

    def run(fn, *args, ops=(), **kw):
        if not dist or not ops:
            return fn(*args, **kw)
        make = {"gs": lambda k, p, *part: _GatherSend(big[k], KIND[k], p, *part), "gf": lambda k, p: _GatherFwd(big[k], KIND[k], p),
                "swap": lambda k, p: _PairSwap(dws[k], pair[k], KIND[k]),
                "chips": lambda k, p, *part: _ChipScatter(psum[k], got[k], p, *part)}
        out, rws = fn(*args, comms=[make[op[0]](*op[1:]) for op in ops], **kw)
        for (op, k, *_), r in zip(ops, rws):
            if op in ("gs", "gf"):
                big[k] = r[0]
            elif op == "swap":
                psum[k] = _pair_sum(dws[k], r[0], KIND[k], f"pair_sum_{k}")
            else:
                got[k] = r[0]
        return out

    h1_0 = _rms_fwd(x, g["pre_mix"][0], "rms_in")
    z0 = run(_matmul, h1_0, big["ab_w_in"], "nn", BF16, "ab_in", ops=[("gs", "ffn_w1_0", X_Y)])
    a_out = run(_sgu_fwd, z0, ln_g, ln_b, w16, bias_b, "sgu_fwd", ops=[("gf", "ffn_w1_0", X_Y), ("gs", "ab_w_out", CHIPS)])
    branches = [run(_dil_fwd, z0, 1, "dil_fwd_1", ops=[("gs", "ffn_w1_0", DIAG, (0, 2)), ("gf", "ab_w_out", CHIPS)]),
                run(_dil_fwd, z0, 4, "dil_fwd_4", ops=[("gs", "ffn_w1_0", DIAG, (1, 2))]),
                run(_dil_fwd, z0, 16, "dil_fwd_16", ops=[("gf", "ffn_w1_0", DIAG), ("gs", "ffn_w2_0", X_Y, (0, 2))])]
    ab, ltot = run(_dil_merge, a_out, [b[0] for b in branches], [b[1] for b in branches], "dil_merge",
                   ops=[("gs", "ffn_w2_0", X_Y, (1, 2))])
    y_0 = run(_matmul, ab, big["ab_w_out"], "nn", F32, "ab_out", ops=[("gs", "ffn_w2_0", DIAG, (0, 2))])
    x1, h2_0 = run(_post_pre_fwd, y_0, g["post_mix"][0], x, g["pre_ffn"][0], "norm_mix0", ops=[("gs", "ffn_w2_0", DIAG, (1, 2))])
    r_0 = run(_matmul, h2_0, big["ffn_w1_0"], "nn", BF16, "ffn_up_0", relu_out=True,
              ops=[("gf", "ffn_w2_0", CHIPS), ("gs", "sb_w_in", CHIPS)])
    y2_0 = run(_matmul, r_0, big["ffn_w2_0"], "nn", F32, "ffn_down_0", a_square=True,
               ops=[("gf", "sb_w_in", CHIPS), ("gs", "sb_w_out", CHIPS), ("gs", "ffn_w1_1", X_Y)])
    x2, h1_1 = run(_post_pre_fwd, y2_0, g["post_ffn"][0], x1, g["pre_mix"][1], "norm_ffn0",
                   ops=[("gf", "ffn_w1_1", X_Y), ("gf", "sb_w_out", CHIPS)])
    zc = run(_matmul, h1_1, big["sb_w_in"], "nn", BF16, "sb_in", ops=[("gs", "ffn_w1_1", DIAG)])
    o_sb, ct_sb, nb_sb = run(_sb_fwd, zc, "sb_fwd", ops=[("gf", "ffn_w1_1", DIAG), ("gs", "ffn_w2_1", CHIPS)])
    y_1 = run(_matmul, o_sb, big["sb_w_out"], "nn", F32, "sb_out", ops=[("gf", "ffn_w2_1", CHIPS)])
    x3, h2_1 = _post_pre_fwd(y_1, g["post_mix"][1], x2, g["pre_ffn"][1], "norm_mix1")
    r_1 = _matmul(h2_1, big["ffn_w1_1"], "nn", BF16, "ffn_up_1", relu_out=True)
    y2_1 = _matmul(r_1, big["ffn_w2_1"], "nn", F32, "ffn_down_1", a_square=True)
    loss, dx4, dy2_1, dg_post_ffn1 = _final_fwd_bwd(y2_1, g["post_ffn"][1], x3, target, "loss")

    da = _matmul(dy2_1, big["ffn_w2_1"], "nt", BF16, "ffn_da_1", mul2=r_1)
    dws["ffn_w2_1"] = _matmul(r_1, dy2_1, "tn", BF16, "ffn_dw2_1", a_square=True)
    dh2 = run(_matmul, da, big["ffn_w1_1"], "nt", F32, "ffn_dh_1", ops=[("swap", "ffn_w2_1", None)])
    dws["ffn_w1_1"] = run(_matmul, h2_1, da, "tn", BF16, "ffn_dw1_1", ops=[("chips", "ffn_w2_1", X_Y)])
    dx3, dy_1, dg_pre_ffn1, dg_post_mix1 = run(_pre_post_bwd, x3, g["pre_ffn"][1], dh2, dx4, y_1, g["post_mix"][1], "norm_bwd_mix1",
                                               ops=[("swap", "ffn_w1_1", None)])
    do_sb = _matmul(dy_1, big["sb_w_out"], "nt", BF16, "sb_out_dx")
    dws["sb_w_out"] = _matmul(o_sb, dy_1, "tn", BF16, "sb_out_dw")
    dqkv = run(_sb_bwd, zc, ct_sb, nb_sb, do_sb, "sb_bwd",
               ops=[("chips", "ffn_w2_1", DIAG), ("chips", "ffn_w1_1", CHIPS), ("swap", "sb_w_out", None)])
    dzc = run(_concat_bf16, dqkv, "sb_dz", ops=[("chips", "sb_w_out", X_Y)])
    dh1 = run(_matmul, dzc, big["sb_w_in"], "nt", F32, "sb_in_dx", ops=[("chips", "sb_w_out", DIAG)])
    dws["sb_w_in"] = _matmul(h1_1, dzc, "tn", BF16, "sb_in_dw")
    dx2, dy2_0, dg_pre_mix1, dg_post_ffn0 = run(_pre_post_bwd, x2, g["pre_mix"][1], dh1, dx3, y2_0, g["post_ffn"][0], "norm_bwd_ffn0",
                                                ops=[("swap", "sb_w_in", None)])
    da = run(_matmul, dy2_0, big["ffn_w2_0"], "nt", BF16, "ffn_da_0", mul2=r_0, ops=[("chips", "sb_w_in", X_Y)])
    dws["ffn_w2_0"] = run(_matmul, r_0, dy2_0, "tn", BF16, "ffn_dw2_0", a_square=True, ops=[("chips", "sb_w_in", DIAG)])
    dws["ffn_w1_0"] = run(_matmul, h2_0, da, "tn", BF16, "ffn_dw1_0", ops=[("swap", "ffn_w2_0", None)])
    dh2 = run(_matmul, da, big["ffn_w1_0"], "nt", F32, "ffn_dh_0", ops=[("chips", "ffn_w2_0", X_Y), ("swap", "ffn_w1_0", None)])
    dx1, dy_0, dg_pre_ffn0, dg_post_mix0 = run(_pre_post_bwd, x1, g["pre_ffn"][0], dh2, dx2, y_0, g["post_mix"][0], "norm_bwd_mix0",
                                               ops=[("chips", "ffn_w2_0", DIAG, (0, 2))])
    dab = run(_matmul, dy_0, big["ab_w_out"], "nt", BF16, "ab_out_dx", ops=[("chips", "ffn_w2_0", DIAG, (1, 2))])
    dws["ab_w_out"] = run(_matmul, ab, dy_0, "tn", BF16, "ab_out_dw", ops=[("chips", "ffn_w1_0", X_Y, (0, 2))])
    duv, d_ln_g, d_ln_b, d_sgu_w, d_sgu_b = run(_sgu_bwd, z0, dab, ln_g, ln_b, w16, bias_b, "sgu_bwd",
                                                ops=[("chips", "ffn_w1_0", X_Y, (1, 2))])
    delta = _dil_delta(ab, dab, "dil_delta")
    parts = [run(_dil_bwd, z0, dab, ltot, delta, 1, "dil_bwd_1", ops=[("chips", "ffn_w1_0", DIAG, (0, 2)), ("swap", "ab_w_out", None)]),
             run(_dil_bwd, z0, dab, ltot, delta, 4, "dil_bwd_4", ops=[("chips", "ffn_w1_0", DIAG, (1, 2))]),
             run(_dil_bwd, z0, dab, ltot, delta, 16, "dil_bwd_16", ops=[("chips", "ab_w_out", CHIPS)])]
    dz0 = _dz_assemble(duv, parts, "dz_assemble")
    dws["ab_w_in"] = _matmul(h1_0, dz0, "tn", BF16, "ab_in_dw")
    dh1 = run(_matmul, dz0, big["ab_w_in"], "nt", F32, "ab_in_dx", ops=[("swap", "ab_w_in", None)])
    grad_x, dg_pre_mix0 = run(_pre_post_bwd, x, g["pre_mix"][0], dh1, dx1, None, None, "norm_bwd_in", ops=[("chips", "ab_w_in", X_Y)])

    d_norms = {
        "pre_mix": jnp.concatenate([dg_pre_mix0, dg_pre_mix1]), "post_mix": jnp.concatenate([dg_post_mix0, dg_post_mix1]),
        "pre_ffn": jnp.concatenate([dg_pre_ffn0, dg_pre_ffn1]), "post_ffn": jnp.concatenate([dg_post_ffn0, dg_post_ffn1]),
    }
    return loss, grad_x, d_norms, (d_ln_g, d_ln_b, d_sgu_w, d_sgu_b), (psum, got) if dist else dws


def _to_bf16_full(w, layer, kind, name):
    _, rows, cols = w.shape
    tr = _tile(rows, 512)
    nblk = rows // tr
    full = (rows, 4 * cols) if kind == "col" else (4 * rows, cols)

    def body(w_ref, o_ref):
        o_ref[...] = w_ref[...].astype(BF16)

    def place(i):
        mine = 2 * lax.axis_index("x") + lax.axis_index("y")
        return (i, mine) if kind == "col" else (mine * nblk + i, 0)

    return pl.pallas_call(
        body, name=name, grid=(nblk,), in_specs=[pl.BlockSpec((None, tr, cols), lambda i: (layer, i, 0))],
        out_specs=pl.BlockSpec((tr, cols), place), out_shape=jax.ShapeDtypeStruct(full, BF16), compiler_params=_params("parallel"),
    )(w)


def _pair_sum(dw16, pair, kind, name):
    rh, cs = _half_shape(dw16.shape, kind)
    tr = _tile(rh, 256)
    nblk = rh // tr

    def body(dw_ref, pair_ref, o_ref):
        o_ref[...] = (dw_ref[...].astype(F32) + pair_ref[...].astype(F32)).astype(BF16)

    def own(s, i):
        c = lax.axis_index("c")
        return (c * nblk + i, s) if kind == "col" else ((2 * s + c) * nblk + i, 0)

    spec3 = pl.BlockSpec((None, tr, cs), lambda s, i: (s, i, 0))
    return pl.pallas_call(
        body, name=name, grid=(4, nblk), in_specs=[pl.BlockSpec((tr, cs), own), spec3], out_specs=spec3,
        out_shape=jax.ShapeDtypeStruct((4, rh, cs), BF16), compiler_params=_params("parallel", "parallel"),
    )(dw16, pair)


def _owner_sum(psum, got, buf, layer, name, comms=()):
    _, rh, cs = psum.shape
    tr = _tile(rh, 256)

    def body(p_ref, got_ref, buf_ref, o_ref):
        tot = p_ref[...].astype(F32)
        for j in range(3):
            tot = tot + got_ref[j].astype(F32)
        o_ref[...] = tot

    res = _pcall(
        body, (psum, got, buf), name=name, grid=(rh // tr,),
        in_specs=[pl.BlockSpec((None, tr, cs), lambda i: (2 * lax.axis_index("x") + lax.axis_index("y"), i, 0)),
                  pl.BlockSpec((3, tr, cs), lambda i: (0, i, 0)), ANY],
        out_specs=[pl.BlockSpec((None, None, tr, cs), lambda i: (layer, lax.axis_index("c"), i, 0))],
        out_shape=[jax.ShapeDtypeStruct(buf.shape, F32)], sem=("parallel",), comms=comms, aliases={2: 0})
    return (res[0][0], res[1]) if comms else res[0]


def _adamw_math(w, g, m, v):
    m = ADAM_B1 * m + (1.0 - ADAM_B1) * g
    v = ADAM_B2 * v + (1.0 - ADAM_B2) * (g * g)
    m_hat = m / (1.0 - ADAM_B1 ** ADAM_STEP)
    v_hat = v / (1.0 - ADAM_B2 ** ADAM_STEP)
    return -ADAM_LR * (m_hat / (jnp.sqrt(v_hat) + ADAM_EPS) + ADAM_WD * w), m, v


def _adamw(w, g, m, v, name):
    layers, rows, cols = w.shape
    tr = _tile(rows, 256)

    def body(w_ref, g_ref, m_ref, v_ref, go_ref, d_ref, mo_ref, vo_ref):
        g = g_ref[...]
        go_ref[...] = g
        d_ref[...], mo_ref[...], vo_ref[...] = _adamw_math(w_ref[...], g, m_ref[...], v_ref[...])

    spec = pl.BlockSpec((None, tr, cols), lambda l, i: (l, i, 0))
    return _pcall(body, (w, g, m, v), name=name, grid=(layers, rows // tr), in_specs=[spec] * 4, out_specs=[spec] * 4,
                  out_shape=[jax.ShapeDtypeStruct(w.shape, F32)] * 4, sem=("parallel", "parallel"))


def _pack(arrays):
    flat = jnp.concatenate([a.reshape(-1) for a in arrays])
    pad = (-flat.shape[0]) % 1024
    return jnp.pad(flat, (0, pad)).reshape(-1, 128)


def _unpack(packed, like):
    flat = packed.reshape(-1)
    out, off = [], 0
    for a in like:
        out.append(flat[off:off + a.size].reshape(a.shape))
        off += a.size
    return out


class _SmallGather:
    def __init__(self, g, parts, patterns):
        self.ro, self.rw, self.patterns, self.n_sems = [g], [parts], patterns, len(patterns)

    def start(self, ro, rw, send, recv):
        x, y, c, _ = _place()
        for k, j in enumerate(self.patterns):
            _remote(ro[0], rw[0].at[4 * x + 2 * y + c], send(k), recv(k), _flip(x, y, c, j)).start()

    def finish(self, ro, rw, send, recv):
        x, y, c, _ = _place()
        for k, j in enumerate(self.patterns):
            px, py, pc = _flip(x, y, c, j)
            slot = rw[0].at[4 * px + 2 * py + pc]
            cp = _remote(slot, slot, send(k), recv(k), (x, y, c))
            cp.wait_recv()
            cp.wait_send()


def _small_update(own, parts, w, m, v, name):
    rows = w.shape[0]

    def body(own_ref, p_ref, w_ref, m_ref, v_ref, g_ref, d_ref, mo_ref, vo_ref):
        me = 4 * lax.axis_index("x") + 2 * lax.axis_index("y") + lax.axis_index("c")
        g = jnp.where(me == 0, own_ref[...], p_ref[0])
        for k in range(1, 8):
            g = g + jnp.where(me == k, own_ref[...], p_ref[k])
        g_ref[...] = g
        d_ref[...], mo_ref[...], vo_ref[...] = _adamw_math(w_ref[...], g, m_ref[...], v_ref[...])

    return pl.pallas_call(body, name=name, out_shape=[jax.ShapeDtypeStruct((rows, 128), F32)] * 4,
                          compiler_params=_params())(own, parts, w, m, v)


SMALL = ("norm_pre_mix", "norm_post_mix", "norm_pre_ffn", "norm_post_ffn", "sgu_ln_g", "sgu_ln_b", "sgu_w", "sgu_b")
BIG = (("ab_w_in", ("ab_w_in",)), ("ab_w_out", ("ab_w_out",)), ("sb_w_in", ("sb_w_in",)), ("sb_w_out", ("sb_w_out",)),
       ("ffn_w1", ("ffn_w1_0", "ffn_w1_1")), ("ffn_w2", ("ffn_w2_0", "ffn_w2_1")))
WEIGHTS = ("norm_pre_mix", "norm_post_mix", "norm_pre_ffn", "norm_post_ffn", "ab_w_in", "sgu_ln_g", "sgu_ln_b", "sgu_w", "sgu_b",
           "ab_w_out", "sb_w_in", "sb_w_out", "ffn_w1", "ffn_w2")


def kernel(x, norm_pre_mix, norm_post_mix, norm_pre_ffn, norm_post_ffn, ab_w_in, sgu_ln_g, sgu_ln_b, sgu_w, sgu_b, ab_w_out, sb_w_in, sb_w_out, ffn_w1, ffn_w2, loss_target, m_norm_pre_mix, m_norm_post_mix, m_norm_pre_ffn, m_norm_post_ffn, m_ab_w_in, m_sgu_ln_g, m_sgu_ln_b, m_sgu_w, m_sgu_b, m_ab_w_out, m_sb_w_in, m_sb_w_out, m_ffn_w1, m_ffn_w2, v_norm_pre_mix, v_norm_post_mix, v_norm_pre_ffn, v_norm_post_ffn, v_ab_w_in, v_sgu_ln_g, v_sgu_ln_b, v_sgu_w, v_sgu_b, v_ab_w_out, v_sb_w_in, v_sb_w_out, v_ffn_w1, v_ffn_w2):
    w = dict(norm_pre_mix=norm_pre_mix, norm_post_mix=norm_post_mix, norm_pre_ffn=norm_pre_ffn, norm_post_ffn=norm_post_ffn,
             ab_w_in=ab_w_in, sgu_ln_g=sgu_ln_g, sgu_ln_b=sgu_ln_b, sgu_w=sgu_w, sgu_b=sgu_b, ab_w_out=ab_w_out, sb_w_in=sb_w_in,
             sb_w_out=sb_w_out, ffn_w1=ffn_w1, ffn_w2=ffn_w2)
    m = dict(norm_pre_mix=m_norm_pre_mix, norm_post_mix=m_norm_post_mix, norm_pre_ffn=m_norm_pre_ffn, norm_post_ffn=m_norm_post_ffn,
             ab_w_in=m_ab_w_in, sgu_ln_g=m_sgu_ln_g, sgu_ln_b=m_sgu_ln_b, sgu_w=m_sgu_w, sgu_b=m_sgu_b, ab_w_out=m_ab_w_out,
             sb_w_in=m_sb_w_in, sb_w_out=m_sb_w_out, ffn_w1=m_ffn_w1, ffn_w2=m_ffn_w2)
    v = dict(norm_pre_mix=v_norm_pre_mix, norm_post_mix=v_norm_post_mix, norm_pre_ffn=v_norm_pre_ffn, norm_post_ffn=v_norm_post_ffn,
             ab_w_in=v_ab_w_in, sgu_ln_g=v_sgu_ln_g, sgu_ln_b=v_sgu_ln_b, sgu_w=v_sgu_w, sgu_b=v_sgu_b, ab_w_out=v_ab_w_out,
             sb_w_in=v_sb_w_in, sb_w_out=v_sb_w_out, ffn_w1=v_ffn_w1, ffn_w2=v_ffn_w2)
    big, pair, got = {}, {}, {}
    for name, keys in BIG:
        for layer, key in enumerate(keys):
            big[key] = _to_bf16_full(w[name], layer, KIND[key], f"bf16_{key}")
            half = _half_shape(big[key].shape, KIND[key])
            pair[key], got[key] = lax.empty((4,) + half, BF16), lax.empty((3,) + half, BF16)
    big["ab_w_in"] = _comm_call([_Gather(big["ab_w_in"], KIND["ab_w_in"])], "gather_first")[0][0]

    norms = {k: w["norm_" + k] for k in ("pre_mix", "post_mix", "pre_ffn", "post_ffn")}
    sgu = (sgu_ln_g, sgu_ln_b, sgu_w[0], sgu_b[0])
    loss_blk, grad_x, d_norms, d_sgu, (psum, got) = _local_step(x[0], loss_target[0], norms, sgu, big, (pair, got))
    loss = lax.psum(loss_blk[0, 0], ("x", "y", "c"))

    grads, deltas, new_m, new_v = {}, {}, {}, {}
    keys_of = dict(BIG)
    small_g = _pack([d_norms["pre_mix"], d_norms["post_mix"], d_norms["pre_ffn"], d_norms["post_ffn"],
                     d_sgu[0], d_sgu[1], d_sgu[2][None], d_sgu[3][None]])
    parts = lax.empty((8,) + small_g.shape, F32)
    small_todo = [(1, 2, 4, 6), (3, 5, 7)]
    bufs, pending = {}, None
    for name in ("ffn_w2", "ffn_w1", "sb_w_in", "sb_w_out", "ab_w_out"):
        buf = lax.empty((len(keys_of[name]), 2) + psum[keys_of[name][0]].shape[1:], F32)
        for layer, key in enumerate(keys_of[name]):
            if pending is not None:
                buf, rws = _owner_sum(psum[key], got[key], buf, layer, f"sum_{key}", comms=[_Join([bufs[pending]])])
                bufs[pending], pending = rws[0][0], None
            elif small_todo:
                buf, rws = _owner_sum(psum[key], got[key], buf, layer, f"sum_{key}",
                                      comms=[_SmallGather(small_g, parts, small_todo.pop(0))])
                parts = rws[0][0]
            else:
                buf = _owner_sum(psum[key], got[key], buf, layer, f"sum_{key}")
        bufs[name], pending = buf, name
    assert not small_todo

    rws = _comm_call([_Join([bufs["ab_w_out"]]), _ChipScatter(psum["ab_w_in"], got["ab_w_in"], DIAG)], "tail_comm")
    bufs["ab_w_out"], got["ab_w_in"] = rws[0][0], rws[1][0]
    bufs["ab_w_in"] = _owner_sum(psum["ab_w_in"], got["ab_w_in"], lax.empty((1, 2) + psum["ab_w_in"].shape[1:], F32), 0, "sum_ab_w_in")
    bufs["ab_w_in"] = _comm_call([_Join([bufs["ab_w_in"]])], "join_last")[0][0]
    for name, _ in BIG:
        grads[name], deltas[name], new_m[name], new_v[name] = _adamw(w[name], bufs[name].reshape(w[name].shape), m[name], v[name], f"adamw_{name}")

    outs = _small_update(small_g, parts, _pack([w[k] for k in SMALL]), _pack([m[k] for k in SMALL]), _pack([v[k] for k in SMALL]), "small_update")
    like = [w[k] for k in SMALL]
    for dst, packed in zip((grads, deltas, new_m, new_v), outs):
        for k, a in zip(SMALL, _unpack(packed, like)):
            dst[k] = a

    return (loss, grad_x[None], *[grads[k] for k in WEIGHTS], *[deltas[k] for k in WEIGHTS],
            *[new_m[k] for k in WEIGHTS], *[new_v[k] for k in WEIGHTS])
```

```python
import functools

import jax
import jax.numpy as jnp
from jax import lax
from jax.experimental import pallas as pl
from jax.experimental.pallas import tpu as pltpu

F32 = jnp.float32
BF16 = jnp.bfloat16
MESH = pl.DeviceIdType.MESH

HEAD_DIM = 128
CHUNK = 128
DILATIONS = (1, 4, 16)
SB_BLOCK = 256
RMS_EPS = 1e-6
LN_EPS = 1e-5
ADAM_LR, ADAM_B1, ADAM_B2, ADAM_EPS, ADAM_WD, ADAM_STEP = 0.001, 0.9, 0.999, 1e-08, 0.01, 10
NEG = -1e30
V7X_VMEM_LIMIT = 48 * 1024 * 1024
ANY = pl.BlockSpec(memory_space=pl.ANY)


def _params(*sem):
    return pltpu.CompilerParams(dimension_semantics=sem if sem else None, vmem_limit_bytes=V7X_VMEM_LIMIT)


def _tile(n, pref):
    if n <= pref:
        return n
    t = pref
    while n % t:
        t -= 128
    return t


def _dot(a, b, dims):
    return lax.dot_general(a, b, (dims, ((), ())), preferred_element_type=F32)


NN = ((1,), (0,))
NT = ((1,), (1,))
TN = ((0,), (0,))


def _place():
    x, y, c = lax.axis_index("x"), lax.axis_index("y"), lax.axis_index("c")
    return x, y, c, 2 * x + y


def _flip(x, y, c, j):
    return (1 - x if j & 4 else x), (1 - y if j & 2 else y), (1 - c if j & 1 else c)


def _half_shape(full_shape, kind):
    rows, cols = full_shape
    return (rows // 2, cols // 4) if kind == "col" else (rows // 8, cols)


def _half(ref, kind, s, h):
    rh, cs = _half_shape(ref.shape, kind)
    if kind == "col":
        return ref.at[pl.ds(h * rh, rh), pl.ds(s * cs, cs)]
    return ref.at[pl.ds((2 * s + h) * rh, rh), :]


def _remote(src, dst, send, recv, to):
    return pltpu.make_async_remote_copy(src_ref=src, dst_ref=dst, send_sem=send, recv_sem=recv, device_id=to, device_id_type=MESH)


class _GatherSend:
    def __init__(self, full, kind, patterns, part=(0, 1)):
        self.ro, self.rw, self.kind, self.patterns, self.part, self.n_sems = [], [full], kind, patterns, part, len(patterns)

    def _rows(self, half):
        i, n = self.part
        rows = half.shape[0] // n
        return half.at[pl.ds(i * rows, rows), :]

    def start(self, ro, rw, send, recv):
        x, y, c, mine = _place()
        own = self._rows(_half(rw[0], self.kind, mine, c))
        for k, j in enumerate(self.patterns):
            px, py, _ = _flip(x, y, c, j)
            _remote(own, own, send(k), recv(k), (px, py, c)).start()

    def finish(self, ro, rw, send, recv):
        x, y, c, _ = _place()
        for k, j in enumerate(self.patterns):
            px, py, _ = _flip(x, y, c, j)
            got = self._rows(_half(rw[0], self.kind, 2 * px + py, c))
            cp = _remote(got, got, send(k), recv(k), (x, y, c))
            cp.wait_recv()
            cp.wait_send()


class _GatherFwd:
    def __init__(self, full, kind, patterns):
        self.ro, self.rw, self.kind, self.patterns, self.n_sems = [], [full], kind, patterns, len(patterns)

    def start(self, ro, rw, send, recv):
        x, y, c, _ = _place()
        for k, j in enumerate(self.patterns):
            px, py, _ = _flip(x, y, c, j)
            got = _half(rw[0], self.kind, 2 * px + py, c)
            _remote(got, got, send(k), recv(k), (x, y, 1 - c)).start()

    def finish(self, ro, rw, send, recv):
        x, y, c, _ = _place()
        for k, j in enumerate(self.patterns):
            px, py, _ = _flip(x, y, c, j)
            got = _half(rw[0], self.kind, 2 * px + py, 1 - c)
            cp = _remote(got, got, send(k), recv(k), (x, y, c))
            cp.wait_recv()
            cp.wait_send()


class _PairSwap:
    n_sems = 4

    def __init__(self, dw16, pair, kind):
        self.ro, self.rw, self.kind = [dw16], [pair], kind

    def start(self, ro, rw, send, recv):
        x, y, c, _ = _place()
        for s in range(4):
            _remote(_half(ro[0], self.kind, s, 1 - c), rw[0].at[s], send(s), recv(s), (x, y, 1 - c)).start()

    def finish(self, ro, rw, send, recv):
        x, y, c, _ = _place()
        for s in range(4):
            cp = _remote(rw[0].at[s], rw[0].at[s], send(s), recv(s), (x, y, c))
            cp.wait_recv()
            cp.wait_send()


class _ChipScatter:
    def __init__(self, psum, got, patterns, part=(0, 1)):
        self.ro, self.rw, self.patterns, self.part, self.n_sems = [psum], [got], patterns, part, len(patterns)

    def _rows(self, ref, slot):
        i, n = self.part
        rows = ref.shape[1] // n
        return ref.at[slot, pl.ds(i * rows, rows), :]

    def start(self, ro, rw, send, recv):
        x, y, c, _ = _place()
        for k, j in enumerate(self.patterns):
            px, py, _ = _flip(x, y, c, j)
            _remote(self._rows(ro[0], 2 * px + py), self._rows(rw[0], j // 2 - 1), send(k), recv(k), (px, py, c)).start()

    def finish(self, ro, rw, send, recv):
        x, y, c, _ = _place()
        for k, j in enumerate(self.patterns):
            slot = self._rows(rw[0], j // 2 - 1)
            cp = _remote(slot, slot, send(k), recv(k), (x, y, c))
            cp.wait_recv()
            cp.wait_send()


class _Join:
    def __init__(self, bufs):
        self.ro, self.rw, self.n_sems = [], list(bufs), sum(b.shape[0] for b in bufs)

    def _copies(self, rw, send, recv, slot):
        x, y, c, _ = _place()
        k = 0
        for ref in rw:
            for l in range(ref.shape[0]):
                yield _remote(ref.at[l, c], ref.at[l, slot(c)], send(k), recv(k), (x, y, 1 - c))
                k += 1

    def start(self, ro, rw, send, recv):
        for cp in self._copies(rw, send, recv, lambda c: c):
            cp.start()

    def finish(self, ro, rw, send, recv):
        for cp in self._copies(rw, send, recv, lambda c: 1 - c):
            cp.wait_recv()
        for cp in self._copies(rw, send, recv, lambda c: c):
            cp.wait_send()


def _comm_layout(comms):
    ro = [a for c in comms for a in c.ro]
    rw = [a for c in comms for a in c.rw]
    return ro, rw, sum(c.n_sems for c in comms)


def _comm_each(comms, method, ro_refs, rw_refs, send, recv):
    i_ro = i_rw = i_sem = 0
    for c in comms:
        getattr(c, method)(ro_refs[i_ro:i_ro + len(c.ro)], rw_refs[i_rw:i_rw + len(c.rw)],
                           lambda k, b=i_sem: send.at[b + k], lambda k, b=i_sem: recv.at[b + k])
        i_ro, i_rw, i_sem = i_ro + len(c.ro), i_rw + len(c.rw), i_sem + c.n_sems


def _split_results(comms, rws):
    out, i = [], 0
    for c in comms:
        out.append(list(rws[i:i + len(c.rw)]))
        i += len(c.rw)
    return out


def _comm_call(comms, name):
    ro, rw, n_sems = _comm_layout(comms)

    def body(*refs):
        ro_refs = refs[:len(ro)]
        rw_refs = refs[len(ro) + len(rw):len(ro) + 2 * len(rw)]
        send, recv = refs[len(ro) + 2 * len(rw):]
        _comm_each(comms, "start", ro_refs, rw_refs, send, recv)
        _comm_each(comms, "finish", ro_refs, rw_refs, send, recv)

    rws = pl.pallas_call(
        body, name=name, in_specs=[ANY] * (len(ro) + len(rw)), out_specs=[ANY] * len(rw),
        out_shape=[jax.ShapeDtypeStruct(a.shape, a.dtype) for a in rw],
        input_output_aliases={len(ro) + k: k for k in range(len(rw))},
        scratch_shapes=[pltpu.SemaphoreType.DMA((n_sems,)), pltpu.SemaphoreType.DMA((n_sems,))],
    )(*ro, *rw)
    return _split_results(comms, rws)


def _pcall(body, args, *, name, grid, in_specs, out_specs, out_shape, scratch=(), sem=(), comms=(), aliases=None):
    n_in, n_out, n_scr = len(in_specs), len(out_specs), len(scratch)
    aliases = dict(aliases or {})
    if not comms:
        return pl.pallas_call(body, name=name, grid=grid, in_specs=list(in_specs), out_specs=list(out_specs),
                              out_shape=list(out_shape), scratch_shapes=list(scratch), input_output_aliases=aliases,
                              compiler_params=_params(*sem))(*args)
    ro, rw, n_sems = _comm_layout(comms)

    def carrier(*refs):
        ins = refs[:n_in]
        ro_refs = refs[n_in:n_in + len(ro)]
        o0 = n_in + len(ro) + len(rw)
        outs = refs[o0:o0 + n_out]
        rw_refs = refs[o0 + n_out:o0 + n_out + len(rw)]
        s0 = o0 + n_out + len(rw)
        send, recv = refs[s0 + n_scr], refs[s0 + n_scr + 1]
        ids = [pl.program_id(a) for a in range(len(grid))]
        first = functools.reduce(jnp.logical_and, [i == 0 for i in ids])
        last = functools.reduce(jnp.logical_and, [i == g - 1 for i, g in zip(ids, grid)])

        @pl.when(first)
        def _():
            _comm_each(comms, "start", ro_refs, rw_refs, send, recv)

        body(*ins, *outs, *refs[s0:s0 + n_scr])

        @pl.when(last)
        def _():
            _comm_each(comms, "finish", ro_refs, rw_refs, send, recv)

    res = pl.pallas_call(
        carrier, name=name, grid=grid, in_specs=list(in_specs) + [ANY] * (len(ro) + len(rw)),
        out_specs=list(out_specs) + [ANY] * len(rw),
        out_shape=list(out_shape) + [jax.ShapeDtypeStruct(a.shape, a.dtype) for a in rw],
        input_output_aliases={**aliases, **{n_in + len(ro) + k: n_out + k for k in range(len(rw))}},
        scratch_shapes=list(scratch) + [pltpu.SemaphoreType.DMA((n_sems,)), pltpu.SemaphoreType.DMA((n_sems,))],
        compiler_params=_params(*["arbitrary"] * len(grid)),
    )(*args, *ro, *rw)
    return list(res[:n_out]), _split_results(comms, res[n_out:])


def _matmul(a, b, mode, out_dtype, name, a_square=False, relu_out=False, mul2=None, comms=()):
    if mode == "nn":
        (m, k), n = a.shape, b.shape[1]
    elif mode == "nt":
        (m, k), n = a.shape, b.shape[0]
    else:
        (k, m), n = a.shape, b.shape[1]
    tm, tn, tk = _tile(m, 1024), _tile(n, 2048 if out_dtype == BF16 else 1024), _tile(k, 2048)
    nk = k // tk
    dims = {"nn": NN, "nt": NT, "tn": TN}[mode]
    a_spec = pl.BlockSpec((tk, tm), lambda i, j, kk: (kk, i)) if mode == "tn" else pl.BlockSpec((tm, tk), lambda i, j, kk: (i, kk))
    b_spec = pl.BlockSpec((tn, tk), lambda i, j, kk: (j, kk)) if mode == "nt" else pl.BlockSpec((tk, tn), lambda i, j, kk: (kk, j))
    o_spec = pl.BlockSpec((tm, tn), lambda i, j, kk: (i, j))

    def body(a_ref, b_ref, *rest):
        m_ref = None if mul2 is None else rest[0]
        o_ref = rest[0 if mul2 is None else 1]
        kk = pl.program_id(2)

        def partial():
            av = a_ref[...]
            if a_square:
                av = av * av
            return _dot(av, b_ref[...], dims)

        def finish(r):
            if relu_out:
                r = jnp.maximum(r, 0.0)
            if mul2 is not None:
                r = r * (2.0 * m_ref[...].astype(F32))
            o_ref[...] = r.astype(out_dtype)

        if nk == 1:
            finish(partial())
            return
        acc_ref = rest[-1]

        @pl.when(kk == 0)
        def _():
            acc_ref[...] = partial()

        @pl.when(kk > 0)
        def _():
            acc_ref[...] += partial()

        @pl.when(kk == nk - 1)
        def _():
            finish(acc_ref[...])

    args = (a, b) if mul2 is None else (a, b, mul2)
    specs = [a_spec, b_spec] + ([] if mul2 is None else [o_spec])
    res = _pcall(body, args, name=name, grid=(m // tm, n // tn, nk), in_specs=specs, out_specs=[o_spec],
                 out_shape=[jax.ShapeDtypeStruct((m, n), out_dtype)], scratch=[pltpu.VMEM((tm, tn), F32)] if nk > 1 else [],
                 sem=("parallel", "parallel", "arbitrary"), comms=comms)
    return (res[0][0], res[1]) if comms else res[0]


NORM_ROWS = 256


def _rms(x, g):
    rstd = lax.rsqrt(jnp.mean(x * x, axis=-1, keepdims=True) + RMS_EPS)
    n = x * rstd
    return n * g, n, rstd


def _rms_bwd(n, rstd, g, dout):
    dn = dout * g
    return rstd * (dn - n * jnp.mean(dn * n, axis=-1, keepdims=True))


def _row_spec(d):
    return pl.BlockSpec((NORM_ROWS, d), lambda i: (i, 0))


def _vec_spec(d):
    return pl.BlockSpec((1, d), lambda i: (0, 0))


def _accumulate(ref, val):
    @pl.when(pl.program_id(0) == 0)
    def _():
        ref[...] = jnp.zeros_like(ref)

    ref[...] += val


def _rms_fwd(x, g, name):
    t, d = x.shape

    def body(x_ref, g_ref, h_ref):
        h_ref[...] = _rms(x_ref[...], g_ref[...])[0].astype(BF16)

    return pl.pallas_call(
        body, name=name, grid=(t // NORM_ROWS,), in_specs=[_row_spec(d), _vec_spec(d)], out_specs=_row_spec(d),
        out_shape=jax.ShapeDtypeStruct((t, d), BF16), compiler_params=_params("parallel"),
    )(x, g)


def _post_pre_fwd(y, g_post, x, g_pre, name, comms=()):
    t, d = x.shape

    def body(y_ref, gp_ref, x_ref, gn_ref, xn_ref, h_ref):
        xn = x_ref[...] + _rms(y_ref[...], gp_ref[...])[0]
        xn_ref[...] = xn
        h_ref[...] = _rms(xn, gn_ref[...])[0].astype(BF16)

    return _pcall(
        body, (y, g_post, x, g_pre), name=name, grid=(t // NORM_ROWS,),
        in_specs=[_row_spec(d), _vec_spec(d), _row_spec(d), _vec_spec(d)], out_specs=[_row_spec(d), _row_spec(d)],
        out_shape=[jax.ShapeDtypeStruct((t, d), F32), jax.ShapeDtypeStruct((t, d), BF16)], sem=("parallel",), comms=comms)


def _final_fwd_bwd(y, g_post, x, target, name):
    t, d = x.shape

    def body(y_ref, g_ref, x_ref, t_ref, loss_ref, dx_ref, dy_ref, dg_ref):
        g = g_ref[...]
        out, n, rstd = _rms(y_ref[...], g)
        e = x_ref[...] + out - t_ref[...]
        _accumulate(loss_ref, jnp.full(loss_ref.shape, 0.5 / d, F32) * jnp.sum(e * e))
        dx = e * (1.0 / d)
        dx_ref[...] = dx
        dy_ref[...] = _rms_bwd(n, rstd, g, dx).astype(BF16)
        _accumulate(dg_ref, jnp.sum(dx * n, axis=0, keepdims=True))

    return pl.pallas_call(
        body, name=name, grid=(t // NORM_ROWS,),
        in_specs=[_row_spec(d), _vec_spec(d), _row_spec(d), _row_spec(d)],
        out_specs=[pl.BlockSpec((8, 128), lambda i: (0, 0)), _row_spec(d), _row_spec(d), _vec_spec(d)],
        out_shape=[jax.ShapeDtypeStruct((8, 128), F32), jax.ShapeDtypeStruct((t, d), F32),
                   jax.ShapeDtypeStruct((t, d), BF16), jax.ShapeDtypeStruct((1, d), F32)],
        compiler_params=_params("arbitrary"),
    )(y, g_post, x, target)


def _pre_post_bwd(x, g_pre, dh, dx_in, y, g_post, name, comms=()):
    t, d = x.shape
    both = y is not None

    def body(x_ref, gp_ref, dh_ref, dxi_ref, *rest):
        if both:
            y_ref, gq_ref, dx_ref, dy_ref, dgp_ref, dgq_ref = rest
        else:
            dx_ref, dgp_ref = rest
        gp = gp_ref[...]
        _, n, rstd = _rms(x_ref[...], gp)
        dh_v = dh_ref[...]
        dx = dxi_ref[...] + _rms_bwd(n, rstd, gp, dh_v)
        dx_ref[...] = dx
        _accumulate(dgp_ref, jnp.sum(dh_v * n, axis=0, keepdims=True))
        if both:
            gq = gq_ref[...]
            _, ny, rstdy = _rms(y_ref[...], gq)
            dy_ref[...] = _rms_bwd(ny, rstdy, gq, dx).astype(BF16)
            _accumulate(dgq_ref, jnp.sum(dx * ny, axis=0, keepdims=True))

    in_specs = [_row_spec(d), _vec_spec(d), _row_spec(d), _row_spec(d)]
    args = [x, g_pre, dh, dx_in]
    if both:
        in_specs += [_row_spec(d), _vec_spec(d)]
        args += [y, g_post]
        out_specs = [_row_spec(d), _row_spec(d), _vec_spec(d), _vec_spec(d)]
        out_shape = [jax.ShapeDtypeStruct((t, d), F32), jax.ShapeDtypeStruct((t, d), BF16),
                     jax.ShapeDtypeStruct((1, d), F32), jax.ShapeDtypeStruct((1, d), F32)]
    else:
        out_specs = [_row_spec(d), _vec_spec(d)]
        out_shape = [jax.ShapeDtypeStruct((t, d), F32), jax.ShapeDtypeStruct((1, d), F32)]
    return _pcall(body, args, name=name, grid=(t // NORM_ROWS,), in_specs=in_specs, out_specs=out_specs, out_shape=out_shape,
                  sem=("arbitrary",), comms=comms)


def _gelu(x):
    return 0.5 * x * (1.0 + lax.erf(x * 0.7071067811865476))


def _gelu_grad(x):
    return 0.5 * (1.0 + lax.erf(x * 0.7071067811865476)) + x * jnp.exp(-0.5 * x * x) * 0.3989422804014327


def _layernorm(v, g, b):
    mu = jnp.mean(v, axis=-1, keepdims=True)
    vc = v - mu
    rs = lax.rsqrt(jnp.mean(vc * vc, axis=-1, keepdims=True) + LN_EPS)
    vhat = vc * rs
    return vhat * g + b, vhat, rs


def _tril_mask():
    return lax.broadcasted_iota(jnp.int32, (CHUNK, CHUNK), 0) >= lax.broadcasted_iota(jnp.int32, (CHUNK, CHUNK), 1)


def _sgu_fwd(z, ln_g, ln_b, w16, bias_b, name, comms=()):
    t = z.shape[0]
    groups = w16.shape[0]
    a = groups * CHUNK

    def body(u_ref, v_ref, g_ref, b_ref, w_ref, bb_ref, o_ref):
        u = _gelu(u_ref[...].astype(F32))
        vn = _layernorm(_gelu(v_ref[...].astype(F32)), g_ref[...], b_ref[...])[0].astype(BF16)
        tril = _tril_mask()
        for g in range(groups):
            sl = slice(g * CHUNK, (g + 1) * CHUNK)
            w = jnp.where(tril, w_ref[g], jnp.zeros((), BF16))
            mixed = _dot(w, vn[:, sl], NN) + bb_ref[g]
            o_ref[:, sl] = (u[:, sl] * mixed).astype(BF16)

    full3 = pl.BlockSpec((groups, CHUNK, CHUNK), lambda c: (0, 0, 0))
    res = _pcall(
        body, (z, z, ln_g, ln_b, w16, bias_b), name=name, grid=(t // CHUNK,),
        in_specs=[pl.BlockSpec((CHUNK, a), lambda c: (c, 0)), pl.BlockSpec((CHUNK, a), lambda c: (c, 1)),
                  _vec_spec(a), _vec_spec(a), full3, full3],
        out_specs=[pl.BlockSpec((CHUNK, a), lambda c: (c, 0))], out_shape=[jax.ShapeDtypeStruct((t, a), BF16)],
        sem=("parallel",), comms=comms)
    return (res[0][0], res[1]) if comms else res[0]


def _sgu_bwd(z, dab, ln_g, ln_b, w16, bias_b, name, comms=()):
    t = z.shape[0]
    groups = w16.shape[0]
    a = groups * CHUNK

    def body(u_ref, v_ref, da_ref, g_ref, b_ref, w_ref, bb_ref, duv_ref, dg_ref, db_ref, dw_ref, dbs_ref, dvn_ref):
        up = u_ref[...].astype(F32)
        vp = v_ref[...].astype(F32)
        u = _gelu(up)
        ln_gain = g_ref[...]
        vn32, vhat, rs = _layernorm(_gelu(vp), ln_gain, b_ref[...])
        vn = vn32.astype(BF16)
        da = da_ref[...].astype(F32)
        tril = _tril_mask()
        ones = jnp.ones((8, CHUNK), F32)

        @pl.when(pl.program_id(0) == 0)
        def _():
            dw_ref[...] = jnp.zeros_like(dw_ref)
            dbs_ref[...] = jnp.zeros_like(dbs_ref)

        for g in range(groups):
            sl = slice(g * CHUNK, (g + 1) * CHUNK)
            w = jnp.where(tril, w_ref[g], jnp.zeros((), BF16))
            mixed = _dot(w, vn[:, sl], NN) + bb_ref[g]
            dmix = da[:, sl] * u[:, sl]
            dmix16 = dmix.astype(BF16)
            duv_ref[:, sl] = (da[:, sl] * mixed * _gelu_grad(up[:, sl])).astype(BF16)
            dvn_ref[:, sl] = _dot(w, dmix16, TN)
            dw_ref[g] += jnp.where(tril, _dot(dmix16, vn[:, sl], NT), 0.0)
            dbs_ref[g:g + 1, :] += lax.dot_general(ones, dmix, (NT, ((), ())), precision=lax.Precision.HIGHEST,
                                                   preferred_element_type=F32)[0:1]
        dvn = dvn_ref[...]
        dvhat = dvn * ln_gain
        dva = rs * (dvhat - jnp.mean(dvhat, axis=-1, keepdims=True) - vhat * jnp.mean(dvhat * vhat, axis=-1, keepdims=True))
        duv_ref[:, a:] = (dva * _gelu_grad(vp)).astype(BF16)
        _accumulate(dg_ref, jnp.sum(dvn * vhat, axis=0, keepdims=True))
        _accumulate(db_ref, jnp.sum(dvn, axis=0, keepdims=True))

    full3 = pl.BlockSpec((groups, CHUNK, CHUNK), lambda c: (0, 0, 0))
    return _pcall(
        body, (z, z, dab, ln_g, ln_b, w16, bias_b), name=name, grid=(t // CHUNK,),
        in_specs=[pl.BlockSpec((CHUNK, a), lambda c: (c, 0)), pl.BlockSpec((CHUNK, a), lambda c: (c, 1)),
                  pl.BlockSpec((CHUNK, a), lambda c: (c, 0)), _vec_spec(a), _vec_spec(a), full3, full3],
        out_specs=[pl.BlockSpec((CHUNK, 2 * a), lambda c: (c, 0)), _vec_spec(a), _vec_spec(a), full3,
                   pl.BlockSpec((groups, CHUNK), lambda c: (0, 0))],
        out_shape=[jax.ShapeDtypeStruct((t, 2 * a), BF16), jax.ShapeDtypeStruct((1, a), F32), jax.ShapeDtypeStruct((1, a), F32),
                   jax.ShapeDtypeStruct((groups, CHUNK, CHUNK), F32), jax.ShapeDtypeStruct((groups, CHUNK), F32)],
        scratch=[pltpu.VMEM((CHUNK, a), F32)], sem=("arbitrary",), comms=comms)


def _dil_masks(d):
    qi = lax.broadcasted_iota(jnp.int32, (CHUNK, CHUNK), 0)
    kj = lax.broadcasted_iota(jnp.int32, (CHUNK, CHUNK), 1)
    dist_c = qi - kj
    return dist_c >= 0, dist_c <= 0, (dist_c * d).astype(F32), ((dist_c + CHUNK) * d).astype(F32)


def _alibi_slope(h, heads):
    return 2.0 ** (-8.0 * (h + 1) / heads)


def _dil_view(z, d):
    t, w = z.shape[0], z.shape[1] // 5
    if d == 1:
        return z, 5, 2
    return z[:, 2 * w:].reshape(t // d, d * 3 * w), 3, 0


def _dil_fwd(z, d, name, comms=()):
    t = z.shape[0]
    w = z.shape[1] // 5
    heads = w // HEAD_DIM
    nb = t // d // CHUNK
    scale = HEAD_DIM ** -0.5
    zv, mult, col_q = _dil_view(z, d)

    def body(q_ref, kp_ref, kc_ref, vp_ref, vc_ref, o_ref, l_ref):
        ok_c, ok_p0, bias_c, bias_p = _dil_masks(d)
        ok_p = ok_p0 & (pl.program_id(1) > 0)
        hs = range(heads)
        sl = [slice(h * HEAD_DIM, (h + 1) * HEAD_DIM) for h in hs]
        slope = [_alibi_slope(h, heads) for h in hs]
        ones = jnp.ones((CHUNK, HEAD_DIM), BF16)
        s_c = [_dot(q_ref[:, sl[h]], kc_ref[:, sl[h]], NT) for h in hs]
        s_p = [_dot(q_ref[:, sl[h]], kp_ref[:, sl[h]], NT) for h in hs]
        s_c = [jnp.where(ok_c, s_c[h] * scale - slope[h] * bias_c, NEG) for h in hs]
        s_p = [jnp.where(ok_p, s_p[h] * scale - slope[h] * bias_p, NEG) for h in hs]
        m = [jnp.max(jnp.maximum(s_c[h], s_p[h]), axis=1, keepdims=True) for h in hs]
        p_c = [jnp.exp(s_c[h] - m[h]).astype(BF16) for h in hs]
        p_p = [jnp.exp(s_p[h] - m[h]).astype(BF16) for h in hs]
        den = [_dot(p_c[h], ones, NN) + _dot(p_p[h], ones, NN) for h in hs]
        o = [_dot(p_c[h], vc_ref[:, sl[h]], NN) + _dot(p_p[h], vp_ref[:, sl[h]], NN) for h in hs]
        l_ref[...] = jnp.zeros_like(l_ref)
        for h in hs:
            o_ref[:, sl[h]] = (o[h] / den[h]).astype(BF16)
            l_ref[:, h:h + 1] = m[h] + jnp.log(den[h][:, 0:1])

    def zspec(col, prev):
        if prev:
            return pl.BlockSpec((CHUNK, w), lambda r, n: (jnp.maximum(n - 1, 0), r * mult + col_q + col))
        return pl.BlockSpec((CHUNK, w), lambda r, n: (n, r * mult + col_q + col))

    res = _pcall(
        body, (zv, zv, zv, zv, zv), name=name, grid=(d, nb),
        in_specs=[zspec(0, False), zspec(1, True), zspec(1, False), zspec(2, True), zspec(2, False)],
        out_specs=[pl.BlockSpec((CHUNK, w), lambda r, n: (n, r)), pl.BlockSpec((CHUNK, HEAD_DIM), lambda r, n: (n, r))],
        out_shape=[jax.ShapeDtypeStruct((t // d, d * w), BF16), jax.ShapeDtypeStruct((t // d, d * HEAD_DIM), F32)],
        sem=("parallel", "parallel"), comms=comms)
    (o, lse), rws = res if comms else (res, None)
    outs = (o.reshape(t, w), lse.reshape(t, HEAD_DIM))
    return (outs, rws) if comms else outs


def _dil_merge(a_out, outs, lses, name, comms=()):
    t, a = a_out.shape
    w = outs[0].shape[1]
    heads = w // HEAD_DIM
    nbr = len(outs)

    def body(a_ref, *rest):
        o_refs, l_refs, (ab_ref, lt_ref) = rest[:nbr], rest[nbr:2 * nbr], rest[2 * nbr:]
        ls = [r[...] for r in l_refs]
        m = functools.reduce(jnp.maximum, ls)
        ws = [jnp.exp(l - m) for l in ls]
        tot = functools.reduce(jnp.add, ws)
        ws = [wt / tot for wt in ws]
        ab_ref[:, :a] = a_ref[...]
        for h in range(heads):
            sl = slice(h * HEAD_DIM, (h + 1) * HEAD_DIM)
            mix = functools.reduce(jnp.add, [wt[:, h:h + 1] * r[:, sl].astype(F32) for wt, r in zip(ws, o_refs)])
            ab_ref[:, a + h * HEAD_DIM:a + (h + 1) * HEAD_DIM] = mix.astype(BF16)
        lt_ref[...] = m + jnp.log(tot)

    return _pcall(
        body, (a_out, *outs, *lses), name=name, grid=(t // NORM_ROWS,),
        in_specs=[_row_spec(a)] + [_row_spec(w)] * nbr + [_row_spec(HEAD_DIM)] * nbr,
        out_specs=[_row_spec(a + w), _row_spec(HEAD_DIM)],
        out_shape=[jax.ShapeDtypeStruct((t, a + w), BF16), jax.ShapeDtypeStruct((t, HEAD_DIM), F32)],
        sem=("parallel",), comms=comms)


def _dil_delta(ab, dab, name):
    t, aw = ab.shape
    w = aw // 2
    heads = w // HEAD_DIM

    def body(o_ref, do_ref, dl_ref):
        dl_ref[...] = jnp.zeros_like(dl_ref)
        for h in range(heads):
            sl = slice(h * HEAD_DIM, (h + 1) * HEAD_DIM)
            dl_ref[:, h:h + 1] = jnp.sum(do_ref[:, sl].astype(F32) * o_ref[:, sl].astype(F32), axis=1, keepdims=True)

    half = pl.BlockSpec((NORM_ROWS, w), lambda i: (i, 1))
    return pl.pallas_call(body, name=name, grid=(t // NORM_ROWS,), in_specs=[half, half], out_specs=_row_spec(HEAD_DIM),
                          out_shape=jax.ShapeDtypeStruct((t, HEAD_DIM), F32), compiler_params=_params("parallel"))(ab, dab)


def _dil_bwd(z, dab, ltot, delta, d, name, comms=()):
    t = z.shape[0]
    w = z.shape[1] // 5
    heads = w // HEAD_DIM
    nb = t // d // CHUNK
    scale = HEAD_DIM ** -0.5

    def body(q_ref, qn_ref, kp_ref, kc_ref, vp_ref, vc_ref, do_ref, don_ref, l_ref, ln_ref, dl_ref, dln_ref,
             dq_ref, dk_ref, dv_ref):
        n = pl.program_id(1)
        ok_c, ok_p0, bias_c, bias_p = _dil_masks(d)
        ok_p = ok_p0 & (n > 0)
        ok_n = ok_p0 & (n < nb - 1)
        hs = range(heads)
        sl = [slice(h * HEAD_DIM, (h + 1) * HEAD_DIM) for h in hs]
        slope = [_alibi_slope(h, heads) for h in hs]
        q, qn = [q_ref[:, s] for s in sl], [qn_ref[:, s] for s in sl]
        kp, kc = [kp_ref[:, s] for s in sl], [kc_ref[:, s] for s in sl]
        vp, vc = [vp_ref[:, s] for s in sl], [vc_ref[:, s] for s in sl]
        do, don = [do_ref[:, s] for s in sl], [don_ref[:, s] for s in sl]
        s_c = [_dot(q[h], kc[h], NT) for h in hs]
        s_p = [_dot(q[h], kp[h], NT) for h in hs]
        s_n = [_dot(qn[h], kc[h], NT) for h in hs]
        dp_c = [_dot(do[h], vc[h], NT) for h in hs]
        dp_p = [_dot(do[h], vp[h], NT) for h in hs]
        dp_n = [_dot(don[h], vc[h], NT) for h in hs]
        delta = [dl_ref[:, h:h + 1] for h in hs]
        delta_n = [dln_ref[:, h:h + 1] for h in hs]
        p_c = [jnp.exp(jnp.where(ok_c, s_c[h] * scale - slope[h] * bias_c, NEG) - l_ref[:, h:h + 1]) for h in hs]
        p_p = [jnp.exp(jnp.where(ok_p, s_p[h] * scale - slope[h] * bias_p, NEG) - l_ref[:, h:h + 1]) for h in hs]
        p_n = [jnp.exp(jnp.where(ok_n, s_n[h] * scale - slope[h] * bias_p, NEG) - ln_ref[:, h:h + 1]) for h in hs]
        ds_c = [(p_c[h] * (dp_c[h] - delta[h])).astype(BF16) for h in hs]
        ds_p = [(p_p[h] * (dp_p[h] - delta[h])).astype(BF16) for h in hs]
        ds_n = [(p_n[h] * (dp_n[h] - delta_n[h])).astype(BF16) for h in hs]
        dq = [_dot(ds_c[h], kc[h], NN) + _dot(ds_p[h], kp[h], NN) for h in hs]
        dk = [_dot(ds_c[h], q[h], TN) + _dot(ds_n[h], qn[h], TN) for h in hs]
        dv = [_dot(p_c[h].astype(BF16), do[h], TN) + _dot(p_n[h].astype(BF16), don[h], TN) for h in hs]
        for h in hs:
            dq_ref[:, sl[h]] = (dq[h] * scale).astype(BF16)
            dk_ref[:, sl[h]] = (dk[h] * scale).astype(BF16)
            dv_ref[:, sl[h]] = dv[h].astype(BF16)

    def spec(mult, col, shift, width=w):
        if shift < 0:
            return pl.BlockSpec((CHUNK, width), lambda r, n: (jnp.maximum(n - 1, 0), r * mult + col))
        if shift > 0:
            return pl.BlockSpec((CHUNK, width), lambda r, n: (jnp.minimum(n + 1, nb - 1), r * mult + col))
        return pl.BlockSpec((CHUNK, width), lambda r, n: (n, r * mult + col))

    zv, mult, cq = _dil_view(z, d)
    dov = dab[:, w:].reshape(t // d, d * w)
    lv = ltot.reshape(t // d, d * HEAD_DIM)
    dlv = delta.reshape(t // d, d * HEAD_DIM)
    ospec = spec(1, 0, 0)
    res = _pcall(
        body, (zv, zv, zv, zv, zv, zv, dov, dov, lv, lv, dlv, dlv), name=name, grid=(d, nb),
        in_specs=[spec(mult, cq, 0), spec(mult, cq, 1), spec(mult, cq + 1, -1), spec(mult, cq + 1, 0),
                  spec(mult, cq + 2, -1), spec(mult, cq + 2, 0), spec(1, 0, 0), spec(1, 0, 1),
                  spec(1, 0, 0, HEAD_DIM), spec(1, 0, 1, HEAD_DIM), spec(1, 0, 0, HEAD_DIM), spec(1, 0, 1, HEAD_DIM)],
        out_specs=[ospec, ospec, ospec], out_shape=[jax.ShapeDtypeStruct((t // d, d * w), BF16)] * 3,
        sem=("parallel", "parallel"), comms=comms)
    outs, rws = res if comms else (res, None)
    outs = [o.reshape(t, w) for o in outs]
    return (outs, rws) if comms else outs


def _dz_assemble(duv, parts, name):
    t, a2 = duv.shape
    w = parts[0][0].shape[1]
    nbr = len(parts)

    def body(duv_ref, *rest):
        refs, dz_ref = rest[:-1], rest[-1]
        dz_ref[:, :a2] = duv_ref[...]
        for i in range(3):
            tot = functools.reduce(jnp.add, [refs[b * 3 + i][...].astype(F32) for b in range(nbr)])
            dz_ref[:, a2 + i * w:a2 + (i + 1) * w] = tot.astype(BF16)

    flat = [p for branch in parts for p in branch]
    return pl.pallas_call(
        body, name=name, grid=(t // NORM_ROWS,), in_specs=[_row_spec(a2)] + [_row_spec(w)] * len(flat),
        out_specs=_row_spec(a2 + 3 * w), out_shape=jax.ShapeDtypeStruct((t, a2 + 3 * w), BF16),
        compiler_params=_params("parallel"),
    )(duv, *flat)


def _split_dot(x, m16):
    hi = x.astype(BF16)
    lo = (x - hi.astype(F32)).astype(BF16)
    return _dot(hi, m16, NN) + _dot(lo, m16, NN)


SB_DEAD = -110.0


def _sb_scaled(q):
    return (q.astype(F32) * (HEAD_DIM ** -0.5)).astype(BF16)


SB_PAIR = 2
SB_GROUP_FWD = 4


def _sb_logs(qs, kj, below):
    zt = [_dot(q, k, NT) for q, k in zip(qs, kj)]
    sp = [jnp.maximum(z, 0.0) + jnp.log(1.0 + jnp.exp(-jnp.abs(z))) for z in zt]
    return [z - s for z, s in zip(zt, sp)], [(-s if below is None else jnp.where(below, -s, 0.0)) for s in sp]


def _sb_alive(s, i, c_run):
    return (s <= i) & (jnp.max(c_run) > SB_DEAD)


def _sb_fwd(zc, name, comms=()):
    t = zc.shape[0]
    c = zc.shape[1] // 3
    heads = c // HEAD_DIM
    blk = min(SB_BLOCK, t)
    grp = SB_GROUP_FWD if heads % SB_GROUP_FWD == 0 else SB_PAIR

    def body(q_ref, k_ref, v_ref, o_ref, ct_ref, nb_ref):
        i = pl.program_id(1)
        sl = [slice(p * HEAD_DIM, (p + 1) * HEAD_DIM) for p in range(grp)]
        qs = [_sb_scaled(q_ref[:, s]) for s in sl]
        rows = lax.broadcasted_iota(jnp.int32, (blk, blk), 0)
        cols = lax.broadcasted_iota(jnp.int32, (blk, blk), 1)
        below = rows > cols
        m_right = below.astype(BF16)

        def tile(carry, diagonal):
            s, acc, c_run = carry[0], carry[1:1 + grp], carry[1 + grp:]
            off = pl.multiple_of((i - s) * blk, blk)
            log_beta, l = _sb_logs(qs, [k_ref[pl.ds(off, blk), p] for p in sl], below if diagonal else None)
            right = [_split_dot(x, m_right) for x in l]
            a = [jnp.exp(lb + (c + r)) for lb, c, r in zip(log_beta, c_run, right)]
            if diagonal:
                a = [jnp.where(below, x, 0.0) for x in a]
            acc = [o + _dot(x.astype(BF16), v_ref[pl.ds(off, blk), p], NN) for o, x, p in zip(acc, a, sl)]
            return (s + 1, *acc, *[c + jnp.sum(x, axis=1, keepdims=True) for c, x in zip(c_run, l)])

        zeros = [jnp.zeros((blk, HEAD_DIM), F32)] * grp + [jnp.zeros((blk, 1), F32)] * grp
        out = lax.while_loop(lambda carry: _sb_alive(carry[0], i, functools.reduce(jnp.maximum, carry[1 + grp:])),
                             lambda carry: tile(carry, False), tile((jnp.int32(0), *zeros), True))
        for p, s in enumerate(sl):
            o_ref[:, s] = out[1 + p].astype(BF16)
            ct_ref[:, s] = jnp.broadcast_to(out[1 + grp + p], (blk, HEAD_DIM))
        nb_ref[...] = jnp.zeros(nb_ref.shape, F32) + out[0].astype(F32)

    groups = heads // grp
    qspec = pl.BlockSpec((blk, grp * HEAD_DIM), lambda h, i: (i, h))
    return _pcall(body, (zc, zc, zc), name=name, grid=(groups, t // blk),
                  in_specs=[qspec, pl.BlockSpec((t, grp * HEAD_DIM), lambda h, i: (0, groups + h)),
                            pl.BlockSpec((t, grp * HEAD_DIM), lambda h, i: (0, 2 * groups + h))],
                  out_specs=[qspec, qspec, qspec],
                  out_shape=[jax.ShapeDtypeStruct((t, c), BF16), jax.ShapeDtypeStruct((t, c), F32), jax.ShapeDtypeStruct((t, c), F32)],
                  sem=("parallel", "parallel"), comms=comms)


def _sb_bwd(zc, ctot, swept, do, name, comms=()):
    t = zc.shape[0]
    c = zc.shape[1] // 3
    heads = c // HEAD_DIM
    blk = min(SB_BLOCK, t)
    scale = HEAD_DIM ** -0.5

    def body(q_ref, k_ref, v_ref, ct_ref, nb_ref, do_ref, dq_ref, dk_ref, dv_ref):
        i = pl.program_id(1)

        @pl.when(i == 0)
        def _():
            dk_ref[...] = jnp.zeros_like(dk_ref)
            dv_ref[...] = jnp.zeros_like(dv_ref)

        ps = range(SB_PAIR)
        sl = [slice(p * HEAD_DIM, (p + 1) * HEAD_DIM) for p in ps]
        qs = [_sb_scaled(q_ref[:, s]) for s in sl]
        dov = [do_ref[:, s] for s in sl]
        c_tot = [ct_ref[:, p * HEAD_DIM:p * HEAD_DIM + 1] for p in ps]
        n_blocks = jnp.clip(jnp.max(nb_ref[0:8, :]).astype(jnp.int32), 1, i + 1)
        rows = lax.broadcasted_iota(jnp.int32, (blk, blk), 0)
        cols = lax.broadcasted_iota(jnp.int32, (blk, blk), 1)
        below = rows > cols
        m_upto = (rows <= cols).astype(BF16)
        m_left = (rows < cols).astype(BF16)

        def tile(j, carry, diagonal):
            dq, l_run, w_run = carry[:SB_PAIR], carry[SB_PAIR:2 * SB_PAIR], carry[2 * SB_PAIR:]
            off = pl.multiple_of(j * blk, blk)
            kj = [k_ref[pl.ds(off, blk), s] for s in sl]
            vj = [v_ref[pl.ds(off, blk), s] for s in sl]
            log_beta, l = _sb_logs(qs, kj, below if diagonal else None)
            d_a = [_dot(dov[p], vj[p], NT) for p in ps]
            upto = [_split_dot(x, m_upto) for x in l]
            a = [jnp.exp(log_beta[p] + (c_tot[p] - l_run[p] - upto[p])) for p in ps]
            if diagonal:
                a = [jnp.where(below, x, 0.0) for x in a]
            wgt = [a[p] * d_a[p] for p in ps]
            before = [w_run[p] + _split_dot(wgt[p], m_left) for p in ps]
            dz = [wgt[p] * jnp.exp(l[p]) - jnp.exp(log_beta[p]) * before[p] for p in ps]
            if diagonal:
                dz = [jnp.where(below, x, 0.0) for x in dz]
            dz16 = [x.astype(BF16) for x in dz]
            dk = [_dot(dz16[p], qs[p], TN) for p in ps]
            dv = [_dot(a[p].astype(BF16), dov[p], TN) for p in ps]
            dq = [dq[p] + _dot(dz16[p], kj[p], NN) for p in ps]
            for p in ps:
                dk_ref[pl.ds(off, blk), sl[p]] += dk[p]
                dv_ref[pl.ds(off, blk), sl[p]] += dv[p]
            return (*dq, *[l_run[p] + jnp.sum(l[p], axis=1, keepdims=True) for p in ps],
                    *[w_run[p] + jnp.sum(wgt[p], axis=1, keepdims=True) for p in ps])

        zeros = [jnp.zeros((blk, HEAD_DIM), F32)] * SB_PAIR + [jnp.zeros((blk, 1), F32)] * (2 * SB_PAIR)
        carry = lax.fori_loop(i + 1 - n_blocks, i, lambda j, carry: tile(j, carry, False), tuple(zeros))
        out = tile(i, carry, True)
        for p in ps:
            dq_ref[:, sl[p]] = out[p] * scale

    pairs = heads // SB_PAIR
    qspec = pl.BlockSpec((blk, SB_PAIR * HEAD_DIM), lambda h, i: (i, h))
    full = pl.BlockSpec((t, SB_PAIR * HEAD_DIM), lambda h, i: (0, h))
    return _pcall(body, (zc, zc, zc, ctot, swept, do), name=name, grid=(pairs, t // blk),
                  in_specs=[qspec, pl.BlockSpec((t, SB_PAIR * HEAD_DIM), lambda h, i: (0, pairs + h)),
                            pl.BlockSpec((t, SB_PAIR * HEAD_DIM), lambda h, i: (0, 2 * pairs + h)), qspec, qspec, qspec],
                  out_specs=[qspec, full, full], out_shape=[jax.ShapeDtypeStruct((t, c), F32)] * 3,
                  sem=("arbitrary", "arbitrary"), comms=comms)


def _concat_bf16(parts, name, comms=()):
    t, c = parts[0].shape

    def body(*refs):
        for k, r in enumerate(refs[:-1]):
            refs[-1][:, k * c:(k + 1) * c] = r[...].astype(BF16)

    res = _pcall(body, tuple(parts), name=name, grid=(t // NORM_ROWS,), in_specs=[_row_spec(c)] * len(parts),
                 out_specs=[_row_spec(c * len(parts))], out_shape=[jax.ShapeDtypeStruct((t, c * len(parts)), BF16)],
                 sem=("parallel",), comms=comms)
    return (res[0][0], res[1]) if comms else res[0]


KIND = {"ab_w_in": "col", "ab_w_out": "row", "sb_w_in": "col", "sb_w_out": "row",
        "ffn_w1_0": "col", "ffn_w1_1": "col", "ffn_w2_0": "row", "ffn_w2_1": "row"}
X_Y, DIAG, CHIPS = (2, 4), (6,), (2, 4, 6)


def _local_step(x, target, norms, sgu, big, bufs=None):
    g = {k: [v[l:l + 1] for l in range(2)] for k, v in norms.items()}
    ln_g, ln_b, sgu_w, sgu_b = sgu
    groups = sgu_w.shape[0]
    w16 = sgu_w.astype(BF16)
    bias_b = jnp.broadcast_to(sgu_b[:, :, None], (groups, CHUNK, CHUNK))
    big, dws, psum, dist = dict(big), {}, {}, bufs is not None
    pair, got = (dict(bufs[0]), dict(bufs[1])) if dist else ({}, {})

    def run(fn, *args, ops=(), **kw):
        if not dist or not ops:
            return fn(*args, **kw)
        make = {"gs": lambda k, p, *part: _GatherSend(big[k], KIND[k], p, *part), "gf": lambda k, p: _GatherFwd(big[k], KIND[k], p),
                "swap": lambda k, p: _PairSwap(dws[k], pair[k], KIND[k]),
                "chips": lambda k, p, *part: _ChipScatter(psum[k], got[k], p, *part)}
        out, rws = fn(*args, comms=[make[op[0]](*op[1:]) for op in ops], **kw)
        for (op, k, *_), r in zip(ops, rws):
            if op in ("gs", "gf"):
                big[k] = r[0]
            elif op == "swap":
                psum[k] = _pair_sum(dws[k], r[0], KIND[k], f"pair_sum_{k}")
            else:
                got[k] = r[0]
        return out

    h1_0 = _rms_fwd(x, g["pre_mix"][0], "rms_in")
    z0 = run(_matmul, h1_0, big["ab_w_in"], "nn", BF16, "ab_in", ops=[("gs", "ffn_w1_0", X_Y)])
    a_out = run(_sgu_fwd, z0, ln_g, ln_b, w16, bias_b, "sgu_fwd", ops=[("gf", "ffn_w1_0", X_Y), ("gs", "ab_w_out", CHIPS)])
    branches = [run(_dil_fwd, z0, 1, "dil_fwd_1", ops=[("gs", "ffn_w1_0", DIAG, (0, 2)), ("gf", "ab_w_out", CHIPS)]),
                run(_dil_fwd, z0, 4, "dil_fwd_4", ops=[("gs", "ffn_w1_0", DIAG, (1, 2))]),
                run(_dil_fwd, z0, 16, "dil_fwd_16", ops=[("gf", "ffn_w1_0", DIAG), ("gs", "ffn_w2_0", X_Y, (0, 2))])]
    ab, ltot = run(_dil_merge, a_out, [b[0] for b in branches], [b[1] for b in branches], "dil_merge",
                   ops=[("gs", "ffn_w2_0", X_Y, (1, 2))])
    y_0 = run(_matmul, ab, big["ab_w_out"], "nn", F32, "ab_out", ops=[("gs", "ffn_w2_0", DIAG, (0, 2))])
    x1, h2_0 = run(_post_pre_fwd, y_0, g["post_mix"][0], x, g["pre_ffn"][0], "norm_mix0", ops=[("gs", "ffn_w2_0", DIAG, (1, 2))])
    r_0 = run(_matmul, h2_0, big["ffn_w1_0"], "nn", BF16, "ffn_up_0", relu_out=True,
              ops=[("gf", "ffn_w2_0", CHIPS), ("gs", "sb_w_in", CHIPS)])
    y2_0 = run(_matmul, r_0, big["ffn_w2_0"], "nn", F32, "ffn_down_0", a_square=True,
               ops=[("gf", "sb_w_in", CHIPS), ("gs", "sb_w_out", CHIPS), ("gs", "ffn_w1_1", X_Y)])
    x2, h1_1 = run(_post_pre_fwd, y2_0, g["post_ffn"][0], x1, g["pre_mix"][1], "norm_ffn0",
                   ops=[("gf", "ffn_w1_1", X_Y), ("gf", "sb_w_out", CHIPS)])
    zc = run(_matmul, h1_1, big["sb_w_in"], "nn", BF16, "sb_in", ops=[("gs", "ffn_w1_1", DIAG)])
    o_sb, ct_sb, nb_sb = run(_sb_fwd, zc, "sb_fwd", ops=[("gf", "ffn_w1_1", DIAG), ("gs", "ffn_w2_1", CHIPS)])
    y_1 = run(_matmul, o_sb, big["sb_w_out"], "nn", F32, "sb_out", ops=[("gf", "ffn_w2_1", CHIPS)])
    x3, h2_1 = _post_pre_fwd(y_1, g["post_mix"][1], x2, g["pre_ffn"][1], "norm_mix1")
    r_1 = _matmul(h2_1, big["ffn_w1_1"], "nn", BF16, "ffn_up_1", relu_out=True)
    y2_1 = _matmul(r_1, big["ffn_w2_1"], "nn", F32, "ffn_down_1", a_square=True)
    loss, dx4, dy2_1, dg_post_ffn1 = _final_fwd_bwd(y2_1, g["post_ffn"][1], x3, target, "loss")

    da = _matmul(dy2_1, big["ffn_w2_1"], "nt", BF16, "ffn_da_1", mul2=r_1)
    dws["ffn_w2_1"] = _matmul(r_1, dy2_1, "tn", BF16, "ffn_dw2_1", a_square=True)
    dh2 = run(_matmul, da, big["ffn_w1_1"], "nt", F32, "ffn_dh_1", ops=[("swap", "ffn_w2_1", None)])
    dws["ffn_w1_1"] = run(_matmul, h2_1, da, "tn", BF16, "ffn_dw1_1", ops=[("chips", "ffn_w2_1", X_Y)])
    dx3, dy_1, dg_pre_ffn1, dg_post_mix1 = run(_pre_post_bwd, x3, g["pre_ffn"][1], dh2, dx4, y_1, g["post_mix"][1], "norm_bwd_mix1",
                                               ops=[("swap", "ffn_w1_1", None)])
    do_sb = _matmul(dy_1, big["sb_w_out"], "nt", BF16, "sb_out_dx")
    dws["sb_w_out"] = _matmul(o_sb, dy_1, "tn", BF16, "sb_out_dw")
    dqkv = run(_sb_bwd, zc, ct_sb, nb_sb, do_sb, "sb_bwd",
               ops=[("chips", "ffn_w2_1", DIAG), ("chips", "ffn_w1_1", CHIPS), ("swap", "sb_w_out", None)])
    dzc = run(_concat_bf16, dqkv, "sb_dz", ops=[("chips", "sb_w_out", X_Y)])
    dh1 = run(_matmul, dzc, big["sb_w_in"], "nt", F32, "sb_in_dx", ops=[("chips", "sb_w_out", DIAG)])
    dws["sb_w_in"] = _matmul(h1_1, dzc, "tn", BF16, "sb_in_dw")
    dx2, dy2_0, dg_pre_mix1, dg_post_ffn0 = run(_pre_post_bwd, x2, g["pre_mix"][1], dh1, dx3, y2_0, g["post_ffn"][0], "norm_bwd_ffn0",
                                                ops=[("swap", "sb_w_in", None)])
    da = run(_matmul, dy2_0, big["ffn_w2_0"], "nt", BF16, "ffn_da_0", mul2=r_0, ops=[("chips", "sb_w_in", X_Y)])
    dws["ffn_w2_0"] = run(_matmul, r_0, dy2_0, "tn", BF16, "ffn_dw2_0", a_square=True, ops=[("chips", "sb_w_in", DIAG)])
    dws["ffn_w1_0"] = run(_matmul, h2_0, da, "tn", BF16, "ffn_dw1_0", ops=[("swap", "ffn_w2_0", None)])
    dh2 = run(_matmul, da, big["ffn_w1_0"], "nt", F32, "ffn_dh_0", ops=[("chips", "ffn_w2_0", X_Y), ("swap", "ffn_w1_0", None)])
    dx1, dy_0, dg_pre_ffn0, dg_post_mix0 = run(_pre_post_bwd, x1, g["pre_ffn"][0], dh2, dx2, y_0, g["post_mix"][0], "norm_bwd_mix0",
                                               ops=[("chips", "ffn_w2_0", DIAG, (0, 2))])
    dab = run(_matmul, dy_0, big["ab_w_out"], "nt", BF16, "ab_out_dx", ops=[("chips", "ffn_w2_0", DIAG, (1, 2))])
    dws["ab_w_out"] = run(_matmul, ab, dy_0, "tn", BF16, "ab_out_dw", ops=[("chips", "ffn_w1_0", X_Y, (0, 2))])
    duv, d_ln_g, d_ln_b, d_sgu_w, d_sgu_b = run(_sgu_bwd, z0, dab, ln_g, ln_b, w16, bias_b, "sgu_bwd",
                                                ops=[("chips", "ffn_w1_0", X_Y, (1, 2))])
    delta = _dil_delta(ab, dab, "dil_delta")
    parts = [run(_dil_bwd, z0, dab, ltot, delta, 1, "dil_bwd_1", ops=[("chips", "ffn_w1_0", DIAG, (0, 2)), ("swap", "ab_w_out", None)]),
             run(_dil_bwd, z0, dab, ltot, delta, 4, "dil_bwd_4", ops=[("chips", "ffn_w1_0", DIAG, (1, 2))]),
             run(_dil_bwd, z0, dab, ltot, delta, 16, "dil_bwd_16", ops=[("chips", "ab_w_out", CHIPS)])]
    dz0 = _dz_assemble(duv, parts, "dz_assemble")
    dws["ab_w_in"] = _matmul(h1_0, dz0, "tn", BF16, "ab_in_dw")
    dh1 = run(_matmul, dz0, big["ab_w_in"], "nt", F32, "ab_in_dx", ops=[("swap", "ab_w_in", None)])
    grad_x, dg_pre_mix0 = run(_pre_post_bwd, x, g["pre_mix"][0], dh1, dx1, None, None, "norm_bwd_in", ops=[("chips", "ab_w_in", X_Y)])

    d_norms = {
        "pre_mix": jnp.concatenate([dg_pre_mix0, dg_pre_mix1]), "post_mix": jnp.concatenate([dg_post_mix0, dg_post_mix1]),
        "pre_ffn": jnp.concatenate([dg_pre_ffn0, dg_pre_ffn1]), "post_ffn": jnp.concatenate([dg_post_ffn0, dg_post_ffn1]),
    }
    return loss, grad_x, d_norms, (d_ln_g, d_ln_b, d_sgu_w, d_sgu_b), (psum, got) if dist else dws


def _to_bf16_full(w, layer, kind, name, comms=()):
    _, rows, cols = w.shape
    tr = _tile(rows, 512)
    nblk = rows // tr
    full = (rows, 4 * cols) if kind == "col" else (4 * rows, cols)

    def body(w_ref, o_ref):
        o_ref[...] = w_ref[...].astype(BF16)

    def place(i):
        mine = 2 * lax.axis_index("x") + lax.axis_index("y")
        return (i, mine) if kind == "col" else (mine * nblk + i, 0)

    res = _pcall(body, (w,), name=name, grid=(nblk,), in_specs=[pl.BlockSpec((None, tr, cols), lambda i: (layer, i, 0))],
                 out_specs=[pl.BlockSpec((tr, cols), place)], out_shape=[jax.ShapeDtypeStruct(full, BF16)], sem=("parallel",), comms=comms)
    return (res[0][0], res[1]) if comms else res[0]


def _pair_sum(dw16, pair, kind, name):
    rh, cs = _half_shape(dw16.shape, kind)
    tr = _tile(rh, 256)
    nblk = rh // tr

    def body(dw_ref, pair_ref, o_ref):
        o_ref[...] = (dw_ref[...].astype(F32) + pair_ref[...].astype(F32)).astype(BF16)

    def own(s, i):
        c = lax.axis_index("c")
        return (c * nblk + i, s) if kind == "col" else ((2 * s + c) * nblk + i, 0)

    spec3 = pl.BlockSpec((None, tr, cs), lambda s, i: (s, i, 0))
    return pl.pallas_call(
        body, name=name, grid=(4, nblk), in_specs=[pl.BlockSpec((tr, cs), own), spec3], out_specs=spec3,
        out_shape=jax.ShapeDtypeStruct((4, rh, cs), BF16), compiler_params=_params("parallel", "parallel"),
    )(dw16, pair)


def _owner_sum(psum, got, buf, layer, name, comms=()):
    _, rh, cs = psum.shape
    tr = _tile(rh, 256)

    def body(p_ref, got_ref, buf_ref, o_ref):
        tot = p_ref[...].astype(F32)
        for j in range(3):
            tot = tot + got_ref[j].astype(F32)
        o_ref[...] = tot

    res = _pcall(
        body, (psum, got, buf), name=name, grid=(rh // tr,),
        in_specs=[pl.BlockSpec((None, tr, cs), lambda i: (2 * lax.axis_index("x") + lax.axis_index("y"), i, 0)),
                  pl.BlockSpec((3, tr, cs), lambda i: (0, i, 0)), ANY],
        out_specs=[pl.BlockSpec((None, None, tr, cs), lambda i: (layer, lax.axis_index("c"), i, 0))],
        out_shape=[jax.ShapeDtypeStruct(buf.shape, F32)], sem=("parallel",), comms=comms, aliases={2: 0})
    return (res[0][0], res[1]) if comms else res[0]


def _adamw_math(w, g, m, v):
    m = ADAM_B1 * m + (1.0 - ADAM_B1) * g
    v = ADAM_B2 * v + (1.0 - ADAM_B2) * (g * g)
    m_hat = m / (1.0 - ADAM_B1 ** ADAM_STEP)
    v_hat = v / (1.0 - ADAM_B2 ** ADAM_STEP)
    return -ADAM_LR * (m_hat / (jnp.sqrt(v_hat) + ADAM_EPS) + ADAM_WD * w), m, v


def _adamw(w, g, m, v, name):
    layers, rows, cols = w.shape
    tr = _tile(rows, 256)

    def body(w_ref, g_ref, m_ref, v_ref, go_ref, d_ref, mo_ref, vo_ref):
        g = g_ref[...]
        go_ref[...] = g
        d_ref[...], mo_ref[...], vo_ref[...] = _adamw_math(w_ref[...], g, m_ref[...], v_ref[...])

    spec = pl.BlockSpec((None, tr, cols), lambda l, i: (l, i, 0))
    return _pcall(body, (w, g, m, v), name=name, grid=(layers, rows // tr), in_specs=[spec] * 4, out_specs=[spec] * 4,
                  out_shape=[jax.ShapeDtypeStruct(w.shape, F32)] * 4, sem=("parallel", "parallel"))


def _pack(arrays):
    flat = jnp.concatenate([a.reshape(-1) for a in arrays])
    pad = (-flat.shape[0]) % 1024
    return jnp.pad(flat, (0, pad)).reshape(-1, 128)


def _unpack(packed, like):
    flat = packed.reshape(-1)
    out, off = [], 0
    for a in like:
        out.append(flat[off:off + a.size].reshape(a.shape))
        off += a.size
    return out


class _SmallGather:
    def __init__(self, g, parts, patterns):
        self.ro, self.rw, self.patterns, self.n_sems = [g], [parts], patterns, len(patterns)

    def start(self, ro, rw, send, recv):
        x, y, c, _ = _place()
        for k, j in enumerate(self.patterns):
            _remote(ro[0], rw[0].at[4 * x + 2 * y + c], send(k), recv(k), _flip(x, y, c, j)).start()

    def finish(self, ro, rw, send, recv):
        x, y, c, _ = _place()
        for k, j in enumerate(self.patterns):
            px, py, pc = _flip(x, y, c, j)
            slot = rw[0].at[4 * px + 2 * py + pc]
            cp = _remote(slot, slot, send(k), recv(k), (x, y, c))
            cp.wait_recv()
            cp.wait_send()


def _small_update(own, parts, w, m, v, name):
    rows = w.shape[0]

    def body(own_ref, p_ref, w_ref, m_ref, v_ref, g_ref, d_ref, mo_ref, vo_ref):
        me = 4 * lax.axis_index("x") + 2 * lax.axis_index("y") + lax.axis_index("c")
        g = jnp.where(me == 0, own_ref[...], p_ref[0])
        for k in range(1, 8):
            g = g + jnp.where(me == k, own_ref[...], p_ref[k])
        g_ref[...] = g
        d_ref[...], mo_ref[...], vo_ref[...] = _adamw_math(w_ref[...], g, m_ref[...], v_ref[...])

    return pl.pallas_call(body, name=name, out_shape=[jax.ShapeDtypeStruct((rows, 128), F32)] * 4,
                          compiler_params=_params())(own, parts, w, m, v)


SMALL = ("norm_pre_mix", "norm_post_mix", "norm_pre_ffn", "norm_post_ffn", "sgu_ln_g", "sgu_ln_b", "sgu_w", "sgu_b")
BIG = (("ab_w_in", ("ab_w_in",)), ("ab_w_out", ("ab_w_out",)), ("sb_w_in", ("sb_w_in",)), ("sb_w_out", ("sb_w_out",)),
       ("ffn_w1", ("ffn_w1_0", "ffn_w1_1")), ("ffn_w2", ("ffn_w2_0", "ffn_w2_1")))
WEIGHTS = ("norm_pre_mix", "norm_post_mix", "norm_pre_ffn", "norm_post_ffn", "ab_w_in", "sgu_ln_g", "sgu_ln_b", "sgu_w", "sgu_b",
           "ab_w_out", "sb_w_in", "sb_w_out", "ffn_w1", "ffn_w2")


def kernel(x, norm_pre_mix, norm_post_mix, norm_pre_ffn, norm_post_ffn, ab_w_in, sgu_ln_g, sgu_ln_b, sgu_w, sgu_b, ab_w_out, sb_w_in, sb_w_out, ffn_w1, ffn_w2, loss_target, m_norm_pre_mix, m_norm_post_mix, m_norm_pre_ffn, m_norm_post_ffn, m_ab_w_in, m_sgu_ln_g, m_sgu_ln_b, m_sgu_w, m_sgu_b, m_ab_w_out, m_sb_w_in, m_sb_w_out, m_ffn_w1, m_ffn_w2, v_norm_pre_mix, v_norm_post_mix, v_norm_pre_ffn, v_norm_post_ffn, v_ab_w_in, v_sgu_ln_g, v_sgu_ln_b, v_sgu_w, v_sgu_b, v_ab_w_out, v_sb_w_in, v_sb_w_out, v_ffn_w1, v_ffn_w2):
    w = dict(norm_pre_mix=norm_pre_mix, norm_post_mix=norm_post_mix, norm_pre_ffn=norm_pre_ffn, norm_post_ffn=norm_post_ffn,
             ab_w_in=ab_w_in, sgu_ln_g=sgu_ln_g, sgu_ln_b=sgu_ln_b, sgu_w=sgu_w, sgu_b=sgu_b, ab_w_out=ab_w_out, sb_w_in=sb_w_in,
             sb_w_out=sb_w_out, ffn_w1=ffn_w1, ffn_w2=ffn_w2)
    m = dict(norm_pre_mix=m_norm_pre_mix, norm_post_mix=m_norm_post_mix, norm_pre_ffn=m_norm_pre_ffn, norm_post_ffn=m_norm_post_ffn,
             ab_w_in=m_ab_w_in, sgu_ln_g=m_sgu_ln_g, sgu_ln_b=m_sgu_ln_b, sgu_w=m_sgu_w, sgu_b=m_sgu_b, ab_w_out=m_ab_w_out,
             sb_w_in=m_sb_w_in, sb_w_out=m_sb_w_out, ffn_w1=m_ffn_w1, ffn_w2=m_ffn_w2)
    v = dict(norm_pre_mix=v_norm_pre_mix, norm_post_mix=v_norm_post_mix, norm_pre_ffn=v_norm_pre_ffn, norm_post_ffn=v_norm_post_ffn,
             ab_w_in=v_ab_w_in, sgu_ln_g=v_sgu_ln_g, sgu_ln_b=v_sgu_ln_b, sgu_w=v_sgu_w, sgu_b=v_sgu_b, ab_w_out=v_ab_w_out,
             sb_w_in=v_sb_w_in, sb_w_out=v_sb_w_out, ffn_w1=v_ffn_w1, ffn_w2=v_ffn_w2)
    first = {"ab_w_out": (_GatherSend, X_Y, (0, 2)), "sb_w_in": (_GatherSend, X_Y, (1, 2)), "sb_w_out": (_GatherSend, DIAG, (0, 2)),
             "ffn_w1_0": (_GatherSend, DIAG, (1, 2)), "ffn_w1_1": (_GatherFwd, X_Y), "ffn_w2_0": (_GatherFwd, DIAG)}
    big, pair, got = {}, {}, {}
    for name, keys in BIG:
        for layer, key in enumerate(keys):
            if key in first:
                op = first[key][0](big["ab_w_in"], KIND["ab_w_in"], *first[key][1:])
                big[key], rws = _to_bf16_full(w[name], layer, KIND[key], f"bf16_{key}", comms=[op])
                big["ab_w_in"] = rws[0][0]
            else:
                big[key] = _to_bf16_full(w[name], layer, KIND[key], f"bf16_{key}")
            half = _half_shape(big[key].shape, KIND[key])
            pair[key], got[key] = lax.empty((4,) + half, BF16), lax.empty((3,) + half, BF16)

    norms = {k: w["norm_" + k] for k in ("pre_mix", "post_mix", "pre_ffn", "post_ffn")}
    sgu = (sgu_ln_g, sgu_ln_b, sgu_w[0], sgu_b[0])
    loss_blk, grad_x, d_norms, d_sgu, (psum, got) = _local_step(x[0], loss_target[0], norms, sgu, big, (pair, got))
    loss = lax.psum(loss_blk[0, 0], ("x", "y", "c"))

    grads, deltas, new_m, new_v = {}, {}, {}, {}
    keys_of = dict(BIG)
    small_g = _pack([d_norms["pre_mix"], d_norms["post_mix"], d_norms["pre_ffn"], d_norms["post_ffn"],
                     d_sgu[0], d_sgu[1], d_sgu[2][None], d_sgu[3][None]])
    parts = lax.empty((8,) + small_g.shape, F32)
    small_todo = [(1, 2, 4, 6), (3, 5, 7)]
    bufs, pending = {}, None
    for name in ("ffn_w2", "ffn_w1", "sb_w_in", "sb_w_out", "ab_w_out"):
        buf = lax.empty((len(keys_of[name]), 2) + psum[keys_of[name][0]].shape[1:], F32)
        for layer, key in enumerate(keys_of[name]):
            if pending is not None:
                buf, rws = _owner_sum(psum[key], got[key], buf, layer, f"sum_{key}", comms=[_Join([bufs[pending]])])
                bufs[pending], pending = rws[0][0], None
            elif small_todo:
                buf, rws = _owner_sum(psum[key], got[key], buf, layer, f"sum_{key}",
                                      comms=[_SmallGather(small_g, parts, small_todo.pop(0))])
                parts = rws[0][0]
            else:
                buf = _owner_sum(psum[key], got[key], buf, layer, f"sum_{key}")
        bufs[name], pending = buf, name
    assert not small_todo

    rws = _comm_call([_Join([bufs["ab_w_out"]]), _ChipScatter(psum["ab_w_in"], got["ab_w_in"], DIAG)], "tail_comm")
    bufs["ab_w_out"], got["ab_w_in"] = rws[0][0], rws[1][0]
    bufs["ab_w_in"] = _owner_sum(psum["ab_w_in"], got["ab_w_in"], lax.empty((1, 2) + psum["ab_w_in"].shape[1:], F32), 0, "sum_ab_w_in")
    bufs["ab_w_in"] = _comm_call([_Join([bufs["ab_w_in"]])], "join_last")[0][0]
    for name, _ in BIG:
        grads[name], deltas[name], new_m[name], new_v[name] = _adamw(w[name], bufs[name].reshape(w[name].shape), m[name], v[name], f"adamw_{name}")

    outs = _small_update(small_g, parts, _pack([w[k] for k in SMALL]), _pack([m[k] for k in SMALL]), _pack([v[k] for k in SMALL]), "small_update")
    like = [w[k] for k in SMALL]
    for dst, packed in zip((grads, deltas, new_m, new_v), outs):
        for k, a in zip(SMALL, _unpack(packed, like)):
            dst[k] = a

    return (loss, grad_x[None], *[grads[k] for k in WEIGHTS], *[deltas[k] for k in WEIGHTS],
            *[new_m[k] for k in WEIGHTS], *[new_v[k] for k in WEIGHTS])
```

```python
import functools

import jax
import jax.numpy as jnp
from jax import lax
from jax.experimental import pallas as pl
from jax.experimental.pallas import tpu as pltpu

F32 = jnp.float32
BF16 = jnp.bfloat16
MESH = pl.DeviceIdType.MESH

HEAD_DIM = 128
CHUNK = 128
DILATIONS = (1, 4, 16)
SB_BLOCK = 256
RMS_EPS = 1e-6
LN_EPS = 1e-5
ADAM_LR, ADAM_B1, ADAM_B2, ADAM_EPS, ADAM_WD, ADAM_STEP = 0.001, 0.9, 0.999, 1e-08, 0.01, 10
NEG = -1e30
V7X_VMEM_LIMIT = 48 * 1024 * 1024
ANY = pl.BlockSpec(memory_space=pl.ANY)


def _params(*sem):
    return pltpu.CompilerParams(dimension_semantics=sem if sem else None, vmem_limit_bytes=V7X_VMEM_LIMIT)


def _tile(n, pref):
    if n <= pref:
        return n
    t = pref
    while n % t:
        t -= 128
    return t


def _dot(a, b, dims):
    return lax.dot_general(a, b, (dims, ((), ())), preferred_element_type=F32)


NN = ((1,), (0,))
NT = ((1,), (1,))
TN = ((0,), (0,))


def _place():
    x, y, c = lax.axis_index("x"), lax.axis_index("y"), lax.axis_index("c")
    return x, y, c, 2 * x + y


def _flip(x, y, c, j):
    return (1 - x if j & 4 else x), (1 - y if j & 2 else y), (1 - c if j & 1 else c)


def _half_shape(full_shape, kind):
    rows, cols = full_shape
    return (rows // 2, cols // 4) if kind == "col" else (rows // 8, cols)


def _half(ref, kind, s, h):
    rh, cs = _half_shape(ref.shape, kind)
    if kind == "col":
        return ref.at[pl.ds(h * rh, rh), pl.ds(s * cs, cs)]
    return ref.at[pl.ds((2 * s + h) * rh, rh), :]


def _remote(src, dst, send, recv, to):
    return pltpu.make_async_remote_copy(src_ref=src, dst_ref=dst, send_sem=send, recv_sem=recv, device_id=to, device_id_type=MESH)


class _Gather:
    n_sems = 6

    def __init__(self, full, kind):
        self.ro, self.rw, self.kind = [], [full], kind

    def start(self, ro, rw, send, recv):
        x, y, c, mine = _place()
        own = _half(rw[0], self.kind, mine, c)
        for k, j in enumerate((2, 4, 6)):
            px, py, _ = _flip(x, y, c, j)
            _remote(own, own, send(k), recv(k), (px, py, c)).start()

    def finish(self, ro, rw, send, recv):
        x, y, c, mine = _place()
        own = _half(rw[0], self.kind, mine, c)
        for k, j in enumerate((2, 4, 6)):
            px, py, _ = _flip(x, y, c, j)
            got = _half(rw[0], self.kind, 2 * px + py, c)
            _remote(got, got, send(k), recv(k), (x, y, c)).wait_recv()
            _remote(got, got, send(3 + k), recv(3 + k), (x, y, 1 - c)).start()
        for k, j in enumerate((2, 4, 6)):
            px, py, _ = _flip(x, y, c, j)
            got = _half(rw[0], self.kind, 2 * px + py, 1 - c)
            _remote(got, got, send(3 + k), recv(3 + k), (x, y, c)).wait_recv()
        for k in range(6):
            _remote(own, own, send(k), recv(k), (x, y, c)).wait_send()


class _GatherSend:
    def __init__(self, full, kind, patterns, part=(0, 1)):
        self.ro, self.rw, self.kind, self.patterns, self.part, self.n_sems = [], [full], kind, patterns, part, len(patterns)

    def _rows(self, half):
        i, n = self.part
        rows = half.shape[0] // n
        return half.at[pl.ds(i * rows, rows), :]

    def start(self, ro, rw, send, recv):
        x, y, c, mine = _place()
        own = self._rows(_half(rw[0], self.kind, mine, c))
        for k, j in enumerate(self.patterns):
            px, py, _ = _flip(x, y, c, j)
            _remote(own, own, send(k), recv(k), (px, py, c)).start()

    def finish(self, ro, rw, send, recv):
        x, y, c, _ = _place()
        for k, j in enumerate(self.patterns):
            px, py, _ = _flip(x, y, c, j)
            got = self._rows(_half(rw[0], self.kind, 2 * px + py, c))
            cp = _remote(got, got, send(k), recv(k), (x, y, c))
            cp.wait_recv()
            cp.wait_send()


class _GatherFwd:
    def __init__(self, full, kind, patterns):
        self.ro, self.rw, self.kind, self.patterns, self.n_sems = [], [full], kind, patterns, len(patterns)

    def start(self, ro, rw, send, recv):
        x, y, c, _ = _place()
        for k, j in enumerate(self.patterns):
            px, py, _ = _flip(x, y, c, j)
            got = _half(rw[0], self.kind, 2 * px + py, c)
            _remote(got, got, send(k), recv(k), (x, y, 1 - c)).start()

    def finish(self, ro, rw, send, recv):
        x, y, c, _ = _place()
        for k, j in enumerate(self.patterns):
            px, py, _ = _flip(x, y, c, j)
            got = _half(rw[0], self.kind, 2 * px + py, 1 - c)
            cp = _remote(got, got, send(k), recv(k), (x, y, c))
            cp.wait_recv()
            cp.wait_send()


class _PairSwap:
    n_sems = 4

    def __init__(self, dw16, pair, kind):
        self.ro, self.rw, self.kind = [dw16], [pair], kind

    def start(self, ro, rw, send, recv):
        x, y, c, _ = _place()
        for s in range(4):
            _remote(_half(ro[0], self.kind, s, 1 - c), rw[0].at[s], send(s), recv(s), (x, y, 1 - c)).start()

    def finish(self, ro, rw, send, recv):
        x, y, c, _ = _place()
        for s in range(4):
            cp = _remote(rw[0].at[s], rw[0].at[s], send(s), recv(s), (x, y, c))
            cp.wait_recv()
            cp.wait_send()


class _ChipScatter:
    def __init__(self, psum, got, patterns, part=(0, 1)):
        self.ro, self.rw, self.patterns, self.part, self.n_sems = [psum], [got], patterns, part, len(patterns)

    def _rows(self, ref, slot):
        i, n = self.part
        rows = ref.shape[1] // n
        return ref.at[slot, pl.ds(i * rows, rows), :]

    def start(self, ro, rw, send, recv):
        x, y, c, _ = _place()
        for k, j in enumerate(self.patterns):
            px, py, _ = _flip(x, y, c, j)
            _remote(self._rows(ro[0], 2 * px + py), self._rows(rw[0], j // 2 - 1), send(k), recv(k), (px, py, c)).start()

    def finish(self, ro, rw, send, recv):
        x, y, c, _ = _place()
        for k, j in enumerate(self.patterns):
            slot = self._rows(rw[0], j // 2 - 1)
            cp = _remote(slot, slot, send(k), recv(k), (x, y, c))
            cp.wait_recv()
            cp.wait_send()


class _Join:
    def __init__(self, bufs):
        self.ro, self.rw, self.n_sems = [], list(bufs), sum(b.shape[0] for b in bufs)

    def _copies(self, rw, send, recv, slot):
        x, y, c, _ = _place()
        k = 0
        for ref in rw:
            for l in range(ref.shape[0]):
                yield _remote(ref.at[l, c], ref.at[l, slot(c)], send(k), recv(k), (x, y, 1 - c))
                k += 1

    def start(self, ro, rw, send, recv):
        for cp in self._copies(rw, send, recv, lambda c: c):
            cp.start()

    def finish(self, ro, rw, send, recv):
        for cp in self._copies(rw, send, recv, lambda c: 1 - c):
            cp.wait_recv()
        for cp in self._copies(rw, send, recv, lambda c: c):
            cp.wait_send()


def _comm_layout(comms):
    ro = [a for c in comms for a in c.ro]
    rw = [a for c in comms for a in c.rw]
    return ro, rw, sum(c.n_sems for c in comms)


def _comm_each(comms, method, ro_refs, rw_refs, send, recv):
    i_ro = i_rw = i_sem = 0
    for c in comms:
        getattr(c, method)(ro_refs[i_ro:i_ro + len(c.ro)], rw_refs[i_rw:i_rw + len(c.rw)],
                           lambda k, b=i_sem: send.at[b + k], lambda k, b=i_sem: recv.at[b + k])
        i_ro, i_rw, i_sem = i_ro + len(c.ro), i_rw + len(c.rw), i_sem + c.n_sems


def _split_results(comms, rws):
    out, i = [], 0
    for c in comms:
        out.append(list(rws[i:i + len(c.rw)]))
        i += len(c.rw)
    return out


def _comm_call(comms, name):
    ro, rw, n_sems = _comm_layout(comms)

    def body(*refs):
        ro_refs = refs[:len(ro)]
        rw_refs = refs[len(ro) + len(rw):len(ro) + 2 * len(rw)]
        send, recv = refs[len(ro) + 2 * len(rw):]
        _comm_each(comms, "start", ro_refs, rw_refs, send, recv)
        _comm_each(comms, "finish", ro_refs, rw_refs, send, recv)

    rws = pl.pallas_call(
        body, name=name, in_specs=[ANY] * (len(ro) + len(rw)), out_specs=[ANY] * len(rw),
        out_shape=[jax.ShapeDtypeStruct(a.shape, a.dtype) for a in rw],
        input_output_aliases={len(ro) + k: k for k in range(len(rw))},
        scratch_shapes=[pltpu.SemaphoreType.DMA((n_sems,)), pltpu.SemaphoreType.DMA((n_sems,))],
    )(*ro, *rw)
    return _split_results(comms, rws)


def _pcall(body, args, *, name, grid, in_specs, out_specs, out_shape, scratch=(), sem=(), comms=(), aliases=None):
    n_in, n_out, n_scr = len(in_specs), len(out_specs), len(scratch)
    aliases = dict(aliases or {})
    if not comms:
        return pl.pallas_call(body, name=name, grid=grid, in_specs=list(in_specs), out_specs=list(out_specs),
                              out_shape=list(out_shape), scratch_shapes=list(scratch), input_output_aliases=aliases,
                              compiler_params=_params(*sem))(*args)
    ro, rw, n_sems = _comm_layout(comms)

    def carrier(*refs):
        ins = refs[:n_in]
        ro_refs = refs[n_in:n_in + len(ro)]
        o0 = n_in + len(ro) + len(rw)
        outs = refs[o0:o0 + n_out]
        rw_refs = refs[o0 + n_out:o0 + n_out + len(rw)]
        s0 = o0 + n_out + len(rw)
        send, recv = refs[s0 + n_scr], refs[s0 + n_scr + 1]
        ids = [pl.program_id(a) for a in range(len(grid))]
        first = functools.reduce(jnp.logical_and, [i == 0 for i in ids])
        last = functools.reduce(jnp.logical_and, [i == g - 1 for i, g in zip(ids, grid)])

        @pl.when(first)
        def _():
            _comm_each(comms, "start", ro_refs, rw_refs, send, recv)

        body(*ins, *outs, *refs[s0:s0 + n_scr])

        @pl.when(last)
        def _():
            _comm_each(comms, "finish", ro_refs, rw_refs, send, recv)

    res = pl.pallas_call(
        carrier, name=name, grid=grid, in_specs=list(in_specs) + [ANY] * (len(ro) + len(rw)),
        out_specs=list(out_specs) + [ANY] * len(rw),
        out_shape=list(out_shape) + [jax.ShapeDtypeStruct(a.shape, a.dtype) for a in rw],
        input_output_aliases={**aliases, **{n_in + len(ro) + k: n_out + k for k in range(len(rw))}},
        scratch_shapes=list(scratch) + [pltpu.SemaphoreType.DMA((n_sems,)), pltpu.SemaphoreType.DMA((n_sems,))],
        compiler_params=_params(*["arbitrary"] * len(grid)),
    )(*args, *ro, *rw)
    return list(res[:n_out]), _split_results(comms, res[n_out:])


def _matmul(a, b, mode, out_dtype, name, a_square=False, relu_out=False, mul2=None, comms=()):
    if mode == "nn":
        (m, k), n = a.shape, b.shape[1]
    elif mode == "nt":
        (m, k), n = a.shape, b.shape[0]
    else:
        (k, m), n = a.shape, b.shape[1]
    tm, tn, tk = _tile(m, 1024), _tile(n, 2048 if out_dtype == BF16 else 1024), _tile(k, 2048)
    nk = k // tk
    dims = {"nn": NN, "nt": NT, "tn": TN}[mode]
    a_spec = pl.BlockSpec((tk, tm), lambda i, j, kk: (kk, i)) if mode == "tn" else pl.BlockSpec((tm, tk), lambda i, j, kk: (i, kk))
    b_spec = pl.BlockSpec((tn, tk), lambda i, j, kk: (j, kk)) if mode == "nt" else pl.BlockSpec((tk, tn), lambda i, j, kk: (kk, j))
    o_spec = pl.BlockSpec((tm, tn), lambda i, j, kk: (i, j))

    def body(a_ref, b_ref, *rest):
        m_ref = None if mul2 is None else rest[0]
        o_ref = rest[0 if mul2 is None else 1]
        kk = pl.program_id(2)

        def partial():
            av = a_ref[...]
            if a_square:
                av = av * av
            return _dot(av, b_ref[...], dims)

        def finish(r):
            if relu_out:
                r = jnp.maximum(r, 0.0)
            if mul2 is not None:
                r = r * (2.0 * m_ref[...].astype(F32))
            o_ref[...] = r.astype(out_dtype)

        if nk == 1:
            finish(partial())
            return
        acc_ref = rest[-1]

        @pl.when(kk == 0)
        def _():
            acc_ref[...] = partial()

        @pl.when(kk > 0)
        def _():
            acc_ref[...] += partial()

        @pl.when(kk == nk - 1)
        def _():
            finish(acc_ref[...])

    args = (a, b) if mul2 is None else (a, b, mul2)
    specs = [a_spec, b_spec] + ([] if mul2 is None else [o_spec])
    res = _pcall(body, args, name=name, grid=(m // tm, n // tn, nk), in_specs=specs, out_specs=[o_spec],
                 out_shape=[jax.ShapeDtypeStruct((m, n), out_dtype)], scratch=[pltpu.VMEM((tm, tn), F32)] if nk > 1 else [],
                 sem=("parallel", "parallel", "arbitrary"), comms=comms)
    return (res[0][0], res[1]) if comms else res[0]


NORM_ROWS = 256


def _rms(x, g):
    rstd = lax.rsqrt(jnp.mean(x * x, axis=-1, keepdims=True) + RMS_EPS)
    n = x * rstd
    return n * g, n, rstd


def _rms_bwd(n, rstd, g, dout):
    dn = dout * g
    return rstd * (dn - n * jnp.mean(dn * n, axis=-1, keepdims=True))


def _row_spec(d):
    return pl.BlockSpec((NORM_ROWS, d), lambda i: (i, 0))


def _vec_spec(d):
    return pl.BlockSpec((1, d), lambda i: (0, 0))


def _accumulate(ref, val):
    @pl.when(pl.program_id(0) == 0)
    def _():
        ref[...] = jnp.zeros_like(ref)

    ref[...] += val


def _rms_fwd(x, g, name):
    t, d = x.shape

    def body(x_ref, g_ref, h_ref):
        h_ref[...] = _rms(x_ref[...], g_ref[...])[0].astype(BF16)

    return pl.pallas_call(
        body, name=name, grid=(t // NORM_ROWS,), in_specs=[_row_spec(d), _vec_spec(d)], out_specs=_row_spec(d),
        out_shape=jax.ShapeDtypeStruct((t, d), BF16), compiler_params=_params("parallel"),
    )(x, g)


def _post_pre_fwd(y, g_post, x, g_pre, name, comms=()):
    t, d = x.shape

    def body(y_ref, gp_ref, x_ref, gn_ref, xn_ref, h_ref):
        xn = x_ref[...] + _rms(y_ref[...].astype(F32), gp_ref[...])[0]
        xn_ref[...] = xn
        h_ref[...] = _rms(xn, gn_ref[...])[0].astype(BF16)

    return _pcall(
        body, (y, g_post, x, g_pre), name=name, grid=(t // NORM_ROWS,),
        in_specs=[_row_spec(d), _vec_spec(d), _row_spec(d), _vec_spec(d)], out_specs=[_row_spec(d), _row_spec(d)],
        out_shape=[jax.ShapeDtypeStruct((t, d), F32), jax.ShapeDtypeStruct((t, d), BF16)], sem=("parallel",), comms=comms)


def _final_fwd_bwd(y, g_post, x, target, name):
    t, d = x.shape

    def body(y_ref, g_ref, x_ref, t_ref, loss_ref, dx_ref, dy_ref, dg_ref):
        g = g_ref[...]
        out, n, rstd = _rms(y_ref[...].astype(F32), g)
        e = x_ref[...] + out - t_ref[...]
        _accumulate(loss_ref, jnp.full(loss_ref.shape, 0.5 / d, F32) * jnp.sum(e * e))
        dx = e * (1.0 / d)
        dx_ref[...] = dx
        dy_ref[...] = _rms_bwd(n, rstd, g, dx).astype(BF16)
        _accumulate(dg_ref, jnp.sum(dx * n, axis=0, keepdims=True))

    return pl.pallas_call(
        body, name=name, grid=(t // NORM_ROWS,),
        in_specs=[_row_spec(d), _vec_spec(d), _row_spec(d), _row_spec(d)],
        out_specs=[pl.BlockSpec((8, 128), lambda i: (0, 0)), _row_spec(d), _row_spec(d), _vec_spec(d)],
        out_shape=[jax.ShapeDtypeStruct((8, 128), F32), jax.ShapeDtypeStruct((t, d), F32),
                   jax.ShapeDtypeStruct((t, d), BF16), jax.ShapeDtypeStruct((1, d), F32)],
        compiler_params=_params("arbitrary"),
    )(y, g_post, x, target)


def _pre_post_bwd(x, g_pre, dh, dx_in, y, g_post, name, comms=()):
    t, d = x.shape
    both = y is not None

    def body(x_ref, gp_ref, dh_ref, dxi_ref, *rest):
        if both:
            y_ref, gq_ref, dx_ref, dy_ref, dgp_ref, dgq_ref = rest
        else:
            dx_ref, dgp_ref = rest
        gp = gp_ref[...]
        _, n, rstd = _rms(x_ref[...], gp)
        dh_v = dh_ref[...].astype(F32)
        dx = dxi_ref[...] + _rms_bwd(n, rstd, gp, dh_v)
        dx_ref[...] = dx
        _accumulate(dgp_ref, jnp.sum(dh_v * n, axis=0, keepdims=True))
        if both:
            gq = gq_ref[...]
            _, ny, rstdy = _rms(y_ref[...].astype(F32), gq)
            dy_ref[...] = _rms_bwd(ny, rstdy, gq, dx).astype(BF16)
            _accumulate(dgq_ref, jnp.sum(dx * ny, axis=0, keepdims=True))

    in_specs = [_row_spec(d), _vec_spec(d), _row_spec(d), _row_spec(d)]
    args = [x, g_pre, dh, dx_in]
    if both:
        in_specs += [_row_spec(d), _vec_spec(d)]
        args += [y, g_post]
        out_specs = [_row_spec(d), _row_spec(d), _vec_spec(d), _vec_spec(d)]
        out_shape = [jax.ShapeDtypeStruct((t, d), F32), jax.ShapeDtypeStruct((t, d), BF16),
                     jax.ShapeDtypeStruct((1, d), F32), jax.ShapeDtypeStruct((1, d), F32)]
    else:
        out_specs = [_row_spec(d), _vec_spec(d)]
        out_shape = [jax.ShapeDtypeStruct((t, d), F32), jax.ShapeDtypeStruct((1, d), F32)]
    return _pcall(body, args, name=name, grid=(t // NORM_ROWS,), in_specs=in_specs, out_specs=out_specs, out_shape=out_shape,
                  sem=("arbitrary",), comms=comms)


def _gelu(x):
    return 0.5 * x * (1.0 + lax.erf(x * 0.7071067811865476))


def _gelu_grad(x):
    return 0.5 * (1.0 + lax.erf(x * 0.7071067811865476)) + x * jnp.exp(-0.5 * x * x) * 0.3989422804014327


def _layernorm(v, g, b):
    mu = jnp.mean(v, axis=-1, keepdims=True)
    vc = v - mu
    rs = lax.rsqrt(jnp.mean(vc * vc, axis=-1, keepdims=True) + LN_EPS)
    vhat = vc * rs
    return vhat * g + b, vhat, rs


def _tril_mask():
    return lax.broadcasted_iota(jnp.int32, (CHUNK, CHUNK), 0) >= lax.broadcasted_iota(jnp.int32, (CHUNK, CHUNK), 1)


def _sgu_fwd(z, ln_g, ln_b, w16, bias_b, name, comms=()):
    t = z.shape[0]
    groups = w16.shape[0]
    a = groups * CHUNK

    def body(u_ref, v_ref, g_ref, b_ref, w_ref, bb_ref, o_ref):
        u = _gelu(u_ref[...].astype(F32))
        vn = _layernorm(_gelu(v_ref[...].astype(F32)), g_ref[...], b_ref[...])[0].astype(BF16)
        tril = _tril_mask()
        for g in range(groups):
            sl = slice(g * CHUNK, (g + 1) * CHUNK)
            w = jnp.where(tril, w_ref[g], jnp.zeros((), BF16))
            mixed = _dot(w, vn[:, sl], NN) + bb_ref[g]
            o_ref[:, sl] = (u[:, sl] * mixed).astype(BF16)

    full3 = pl.BlockSpec((groups, CHUNK, CHUNK), lambda c: (0, 0, 0))
    res = _pcall(
        body, (z, z, ln_g, ln_b, w16, bias_b), name=name, grid=(t // CHUNK,),
        in_specs=[pl.BlockSpec((CHUNK, a), lambda c: (c, 0)), pl.BlockSpec((CHUNK, a), lambda c: (c, 1)),
                  _vec_spec(a), _vec_spec(a), full3, full3],
        out_specs=[pl.BlockSpec((CHUNK, a), lambda c: (c, 0))], out_shape=[jax.ShapeDtypeStruct((t, a), BF16)],
        sem=("parallel",), comms=comms)
    return (res[0][0], res[1]) if comms else res[0]


def _sgu_bwd(z, dab, ln_g, ln_b, w16, bias_b, name, comms=()):
    t = z.shape[0]
    groups = w16.shape[0]
    a = groups * CHUNK

    def body(u_ref, v_ref, da_ref, g_ref, b_ref, w_ref, bb_ref, duv_ref, dg_ref, db_ref, dw_ref, dbs_ref, dvn_ref):
        up = u_ref[...].astype(F32)
        vp = v_ref[...].astype(F32)
        u = _gelu(up)
        ln_gain = g_ref[...]
        vn32, vhat, rs = _layernorm(_gelu(vp), ln_gain, b_ref[...])
        vn = vn32.astype(BF16)
        da = da_ref[...].astype(F32)
        tril = _tril_mask()
        ones = jnp.ones((8, CHUNK), F32)

        @pl.when(pl.program_id(0) == 0)
        def _():
            dw_ref[...] = jnp.zeros_like(dw_ref)
            dbs_ref[...] = jnp.zeros_like(dbs_ref)

        for g in range(groups):
            sl = slice(g * CHUNK, (g + 1) * CHUNK)
            w = jnp.where(tril, w_ref[g], jnp.zeros((), BF16))
            mixed = _dot(w, vn[:, sl], NN) + bb_ref[g]
            dmix = da[:, sl] * u[:, sl]
            dmix16 = dmix.astype(BF16)
            duv_ref[:, sl] = (da[:, sl] * mixed * _gelu_grad(up[:, sl])).astype(BF16)
            dvn_ref[:, sl] = _dot(w, dmix16, TN)
            dw_ref[g] += jnp.where(tril, _dot(dmix16, vn[:, sl], NT), 0.0)
            dbs_ref[g:g + 1, :] += lax.dot_general(ones, dmix, (NT, ((), ())), precision=lax.Precision.HIGHEST,
                                                   preferred_element_type=F32)[0:1]
        dvn = dvn_ref[...]
        dvhat = dvn * ln_gain
        dva = rs * (dvhat - jnp.mean(dvhat, axis=-1, keepdims=True) - vhat * jnp.mean(dvhat * vhat, axis=-1, keepdims=True))
        duv_ref[:, a:] = (dva * _gelu_grad(vp)).astype(BF16)
        _accumulate(dg_ref, jnp.sum(dvn * vhat, axis=0, keepdims=True))
        _accumulate(db_ref, jnp.sum(dvn, axis=0, keepdims=True))

    full3 = pl.BlockSpec((groups, CHUNK, CHUNK), lambda c: (0, 0, 0))
    return _pcall(
        body, (z, z, dab, ln_g, ln_b, w16, bias_b), name=name, grid=(t // CHUNK,),
        in_specs=[pl.BlockSpec((CHUNK, a), lambda c: (c, 0)), pl.BlockSpec((CHUNK, a), lambda c: (c, 1)),
                  pl.BlockSpec((CHUNK, a), lambda c: (c, 0)), _vec_spec(a), _vec_spec(a), full3, full3],
        out_specs=[pl.BlockSpec((CHUNK, 2 * a), lambda c: (c, 0)), _vec_spec(a), _vec_spec(a), full3,
                   pl.BlockSpec((groups, CHUNK), lambda c: (0, 0))],
        out_shape=[jax.ShapeDtypeStruct((t, 2 * a), BF16), jax.ShapeDtypeStruct((1, a), F32), jax.ShapeDtypeStruct((1, a), F32),
                   jax.ShapeDtypeStruct((groups, CHUNK, CHUNK), F32), jax.ShapeDtypeStruct((groups, CHUNK), F32)],
        scratch=[pltpu.VMEM((CHUNK, a), F32)], sem=("arbitrary",), comms=comms)


def _dil_masks(d):
    qi = lax.broadcasted_iota(jnp.int32, (CHUNK, CHUNK), 0)
    kj = lax.broadcasted_iota(jnp.int32, (CHUNK, CHUNK), 1)
    dist_c = qi - kj
    return dist_c >= 0, dist_c <= 0, (dist_c * d).astype(F32), ((dist_c + CHUNK) * d).astype(F32)


def _alibi_slope(h, heads):
    return 2.0 ** (-8.0 * (h + 1) / heads)


def _dil_view(z, d):
    t, w = z.shape[0], z.shape[1] // 5
    if d == 1:
        return z, 5, 2
    return z[:, 2 * w:].reshape(t // d, d * 3 * w), 3, 0


def _dil_fwd(z, d, name, comms=()):
    t = z.shape[0]
    w = z.shape[1] // 5
    heads = w // HEAD_DIM
    nb = t // d // CHUNK
    scale = HEAD_DIM ** -0.5
    zv, mult, col_q = _dil_view(z, d)

    def body(q_ref, kp_ref, kc_ref, vp_ref, vc_ref, o_ref, l_ref):
        ok_c, ok_p0, bias_c, bias_p = _dil_masks(d)
        ok_p = ok_p0 & (pl.program_id(1) > 0)
        hs = range(heads)
        sl = [slice(h * HEAD_DIM, (h + 1) * HEAD_DIM) for h in hs]
        slope = [_alibi_slope(h, heads) for h in hs]
        ones = jnp.ones((CHUNK, HEAD_DIM), BF16)
        s_c = [_dot(q_ref[:, sl[h]], kc_ref[:, sl[h]], NT) for h in hs]
        s_p = [_dot(q_ref[:, sl[h]], kp_ref[:, sl[h]], NT) for h in hs]
        s_c = [jnp.where(ok_c, s_c[h] * scale - slope[h] * bias_c, NEG) for h in hs]
        s_p = [jnp.where(ok_p, s_p[h] * scale - slope[h] * bias_p, NEG) for h in hs]
        m = [jnp.max(jnp.maximum(s_c[h], s_p[h]), axis=1, keepdims=True) for h in hs]
        p_c = [jnp.exp(s_c[h] - m[h]).astype(BF16) for h in hs]
        p_p = [jnp.exp(s_p[h] - m[h]).astype(BF16) for h in hs]
        den = [_dot(p_c[h], ones, NN) + _dot(p_p[h], ones, NN) for h in hs]
        o = [_dot(p_c[h], vc_ref[:, sl[h]], NN) + _dot(p_p[h], vp_ref[:, sl[h]], NN) for h in hs]
        l_ref[...] = jnp.zeros_like(l_ref)
        for h in hs:
            o_ref[:, sl[h]] = (o[h] / den[h]).astype(BF16)
            l_ref[:, h:h + 1] = m[h] + jnp.log(den[h][:, 0:1])

    def zspec(col, prev):
        if prev:
            return pl.BlockSpec((CHUNK, w), lambda r, n: (jnp.maximum(n - 1, 0), r * mult + col_q + col))
        return pl.BlockSpec((CHUNK, w), lambda r, n: (n, r * mult + col_q + col))

    res = _pcall(
        body, (zv, zv, zv, zv, zv), name=name, grid=(d, nb),
        in_specs=[zspec(0, False), zspec(1, True), zspec(1, False), zspec(2, True), zspec(2, False)],
        out_specs=[pl.BlockSpec((CHUNK, w), lambda r, n: (n, r)), pl.BlockSpec((CHUNK, HEAD_DIM), lambda r, n: (n, r))],
        out_shape=[jax.ShapeDtypeStruct((t // d, d * w), BF16), jax.ShapeDtypeStruct((t // d, d * HEAD_DIM), F32)],
        sem=("parallel", "parallel"), comms=comms)
    (o, lse), rws = res if comms else (res, None)
    outs = (o.reshape(t, w), lse.reshape(t, HEAD_DIM))
    return (outs, rws) if comms else outs


def _dil_merge(a_out, outs, lses, name, comms=()):
    t, a = a_out.shape
    w = outs[0].shape[1]
    heads = w // HEAD_DIM
    nbr = len(outs)

    def body(a_ref, *rest):
        o_refs, l_refs, (ab_ref, lt_ref) = rest[:nbr], rest[nbr:2 * nbr], rest[2 * nbr:]
        ls = [r[...] for r in l_refs]
        m = functools.reduce(jnp.maximum, ls)
        ws = [jnp.exp(l - m) for l in ls]
        tot = functools.reduce(jnp.add, ws)
        ws = [wt / tot for wt in ws]
        ab_ref[:, :a] = a_ref[...]
        for h in range(heads):
            sl = slice(h * HEAD_DIM, (h + 1) * HEAD_DIM)
            mix = functools.reduce(jnp.add, [wt[:, h:h + 1] * r[:, sl].astype(F32) for wt, r in zip(ws, o_refs)])
            ab_ref[:, a + h * HEAD_DIM:a + (h + 1) * HEAD_DIM] = mix.astype(BF16)
        lt_ref[...] = m + jnp.log(tot)

    return _pcall(
        body, (a_out, *outs, *lses), name=name, grid=(t // NORM_ROWS,),
        in_specs=[_row_spec(a)] + [_row_spec(w)] * nbr + [_row_spec(HEAD_DIM)] * nbr,
        out_specs=[_row_spec(a + w), _row_spec(HEAD_DIM)],
        out_shape=[jax.ShapeDtypeStruct((t, a + w), BF16), jax.ShapeDtypeStruct((t, HEAD_DIM), F32)],
        sem=("parallel",), comms=comms)


def _dil_delta(ab, dab, name):
    t, aw = ab.shape
    w = aw // 2
    heads = w // HEAD_DIM

    def body(o_ref, do_ref, dl_ref):
        dl_ref[...] = jnp.zeros_like(dl_ref)
        for h in range(heads):
            sl = slice(h * HEAD_DIM, (h + 1) * HEAD_DIM)
            dl_ref[:, h:h + 1] = jnp.sum(do_ref[:, sl].astype(F32) * o_ref[:, sl].astype(F32), axis=1, keepdims=True)

    half = pl.BlockSpec((NORM_ROWS, w), lambda i: (i, 1))
    return pl.pallas_call(body, name=name, grid=(t // NORM_ROWS,), in_specs=[half, half], out_specs=_row_spec(HEAD_DIM),
                          out_shape=jax.ShapeDtypeStruct((t, HEAD_DIM), F32), compiler_params=_params("parallel"))(ab, dab)


def _dil_bwd(z, dab, ltot, delta, d, name, comms=()):
    t = z.shape[0]
    w = z.shape[1] // 5
    heads = w // HEAD_DIM
    nb = t // d // CHUNK
    scale = HEAD_DIM ** -0.5

    def body(q_ref, qn_ref, kp_ref, kc_ref, vp_ref, vc_ref, do_ref, don_ref, l_ref, ln_ref, dl_ref, dln_ref,
             dq_ref, dk_ref, dv_ref):
        n = pl.program_id(1)
        ok_c, ok_p0, bias_c, bias_p = _dil_masks(d)
        ok_p = ok_p0 & (n > 0)
        ok_n = ok_p0 & (n < nb - 1)
        hs = range(heads)
        sl = [slice(h * HEAD_DIM, (h + 1) * HEAD_DIM) for h in hs]
        slope = [_alibi_slope(h, heads) for h in hs]
        q, qn = [q_ref[:, s] for s in sl], [qn_ref[:, s] for s in sl]
        kp, kc = [kp_ref[:, s] for s in sl], [kc_ref[:, s] for s in sl]
        vp, vc = [vp_ref[:, s] for s in sl], [vc_ref[:, s] for s in sl]
        do, don = [do_ref[:, s] for s in sl], [don_ref[:, s] for s in sl]
        s_c = [_dot(q[h], kc[h], NT) for h in hs]
        s_p = [_dot(q[h], kp[h], NT) for h in hs]
        s_n = [_dot(qn[h], kc[h], NT) for h in hs]
        dp_c = [_dot(do[h], vc[h], NT) for h in hs]
        dp_p = [_dot(do[h], vp[h], NT) for h in hs]
        dp_n = [_dot(don[h], vc[h], NT) for h in hs]
        delta = [dl_ref[:, h:h + 1] for h in hs]
        delta_n = [dln_ref[:, h:h + 1] for h in hs]
        p_c = [jnp.exp(jnp.where(ok_c, s_c[h] * scale - slope[h] * bias_c, NEG) - l_ref[:, h:h + 1]) for h in hs]
        p_p = [jnp.exp(jnp.where(ok_p, s_p[h] * scale - slope[h] * bias_p, NEG) - l_ref[:, h:h + 1]) for h in hs]
        p_n = [jnp.exp(jnp.where(ok_n, s_n[h] * scale - slope[h] * bias_p, NEG) - ln_ref[:, h:h + 1]) for h in hs]
        ds_c = [(p_c[h] * (dp_c[h] - delta[h])).astype(BF16) for h in hs]
        ds_p = [(p_p[h] * (dp_p[h] - delta[h])).astype(BF16) for h in hs]
        ds_n = [(p_n[h] * (dp_n[h] - delta_n[h])).astype(BF16) for h in hs]
        dq = [_dot(ds_c[h], kc[h], NN) + _dot(ds_p[h], kp[h], NN) for h in hs]
        dk = [_dot(ds_c[h], q[h], TN) + _dot(ds_n[h], qn[h], TN) for h in hs]
        dv = [_dot(p_c[h].astype(BF16), do[h], TN) + _dot(p_n[h].astype(BF16), don[h], TN) for h in hs]
        for h in hs:
            dq_ref[:, sl[h]] = (dq[h] * scale).astype(BF16)
            dk_ref[:, sl[h]] = (dk[h] * scale).astype(BF16)
            dv_ref[:, sl[h]] = dv[h].astype(BF16)

    def spec(mult, col, shift, width=w):
        if shift < 0:
            return pl.BlockSpec((CHUNK, width), lambda r, n: (jnp.maximum(n - 1, 0), r * mult + col))
        if shift > 0:
            return pl.BlockSpec((CHUNK, width), lambda r, n: (jnp.minimum(n + 1, nb - 1), r * mult + col))
        return pl.BlockSpec((CHUNK, width), lambda r, n: (n, r * mult + col))

    zv, mult, cq = _dil_view(z, d)
    dov = dab[:, w:].reshape(t // d, d * w)
    lv = ltot.reshape(t // d, d * HEAD_DIM)
    dlv = delta.reshape(t // d, d * HEAD_DIM)
    ospec = spec(1, 0, 0)
    res = _pcall(
        body, (zv, zv, zv, zv, zv, zv, dov, dov, lv, lv, dlv, dlv), name=name, grid=(d, nb),
        in_specs=[spec(mult, cq, 0), spec(mult, cq, 1), spec(mult, cq + 1, -1), spec(mult, cq + 1, 0),
                  spec(mult, cq + 2, -1), spec(mult, cq + 2, 0), spec(1, 0, 0), spec(1, 0, 1),
                  spec(1, 0, 0, HEAD_DIM), spec(1, 0, 1, HEAD_DIM), spec(1, 0, 0, HEAD_DIM), spec(1, 0, 1, HEAD_DIM)],
        out_specs=[ospec, ospec, ospec], out_shape=[jax.ShapeDtypeStruct((t // d, d * w), BF16)] * 3,
        sem=("parallel", "parallel"), comms=comms)
    outs, rws = res if comms else (res, None)
    outs = [o.reshape(t, w) for o in outs]
    return (outs, rws) if comms else outs


def _dz_assemble(duv, parts, name):
    t, a2 = duv.shape
    w = parts[0][0].shape[1]
    nbr = len(parts)

    def body(duv_ref, *rest):
        refs, dz_ref = rest[:-1], rest[-1]
        dz_ref[:, :a2] = duv_ref[...]
        for i in range(3):
            tot = functools.reduce(jnp.add, [refs[b * 3 + i][...].astype(F32) for b in range(nbr)])
            dz_ref[:, a2 + i * w:a2 + (i + 1) * w] = tot.astype(BF16)

    flat = [p for branch in parts for p in branch]
    return pl.pallas_call(
        body, name=name, grid=(t // NORM_ROWS,), in_specs=[_row_spec(a2)] + [_row_spec(w)] * len(flat),
        out_specs=_row_spec(a2 + 3 * w), out_shape=jax.ShapeDtypeStruct((t, a2 + 3 * w), BF16),
        compiler_params=_params("parallel"),
    )(duv, *flat)


def _split_dot(x, m16):
    hi = x.astype(BF16)
    lo = (x - hi.astype(F32)).astype(BF16)
    return _dot(hi, m16, NN) + _dot(lo, m16, NN)


SB_DEAD = -110.0


def _sb_scaled(q):
    return (q.astype(F32) * (HEAD_DIM ** -0.5)).astype(BF16)


SB_PAIR = 2
SB_GROUP_FWD = 4


def _sb_logs(qs, kj, below):
    zt = [_dot(q, k, NT) for q, k in zip(qs, kj)]
    sp = [jnp.maximum(z, 0.0) + jnp.log(1.0 + jnp.exp(-jnp.abs(z))) for z in zt]
    return [z - s for z, s in zip(zt, sp)], [(-s if below is None else jnp.where(below, -s, 0.0)) for s in sp]


def _sb_alive(s, i, c_run):
    return (s <= i) & (jnp.max(c_run) > SB_DEAD)


def _sb_fwd(zc, name, comms=()):
    t = zc.shape[0]
    c = zc.shape[1] // 3
    heads = c // HEAD_DIM
    blk = min(SB_BLOCK, t)
    grp = SB_GROUP_FWD if heads % SB_GROUP_FWD == 0 else SB_PAIR

    def body(q_ref, k_ref, v_ref, o_ref, ct_ref, nb_ref):
        i = pl.program_id(1)
        sl = [slice(p * HEAD_DIM, (p + 1) * HEAD_DIM) for p in range(grp)]
        qs = [_sb_scaled(q_ref[:, s]) for s in sl]
        rows = lax.broadcasted_iota(jnp.int32, (blk, blk), 0)
        cols = lax.broadcasted_iota(jnp.int32, (blk, blk), 1)
        below = rows > cols
        m_right = below.astype(BF16)

        def tile(carry, diagonal):
            s, acc, c_run = carry[0], carry[1:1 + grp], carry[1 + grp:]
            off = pl.multiple_of((i - s) * blk, blk)
            log_beta, l = _sb_logs(qs, [k_ref[pl.ds(off, blk), p] for p in sl], below if diagonal else None)
            right = [_split_dot(x, m_right) for x in l]
            a = [jnp.exp(lb + (c + r)) for lb, c, r in zip(log_beta, c_run, right)]
            if diagonal:
                a = [jnp.where(below, x, 0.0) for x in a]
            acc = [o + _dot(x.astype(BF16), v_ref[pl.ds(off, blk), p], NN) for o, x, p in zip(acc, a, sl)]
            return (s + 1, *acc, *[c + jnp.sum(x, axis=1, keepdims=True) for c, x in zip(c_run, l)])

        zeros = [jnp.zeros((blk, HEAD_DIM), F32)] * grp + [jnp.zeros((blk, 1), F32)] * grp
        out = lax.while_loop(lambda carry: _sb_alive(carry[0], i, functools.reduce(jnp.maximum, carry[1 + grp:])),
                             lambda carry: tile(carry, False), tile((jnp.int32(0), *zeros), True))
        for p, s in enumerate(sl):
            o_ref[:, s] = out[1 + p].astype(BF16)
            ct_ref[:, s] = jnp.broadcast_to(out[1 + grp + p], (blk, HEAD_DIM))
        nb_ref[...] = jnp.zeros(nb_ref.shape, F32) + out[0].astype(F32)

    groups = heads // grp
    qspec = pl.BlockSpec((blk, grp * HEAD_DIM), lambda h, i: (i, h))
    return _pcall(body, (zc, zc, zc), name=name, grid=(groups, t // blk),
                  in_specs=[qspec, pl.BlockSpec((t, grp * HEAD_DIM), lambda h, i: (0, groups + h)),
                            pl.BlockSpec((t, grp * HEAD_DIM), lambda h, i: (0, 2 * groups + h))],
                  out_specs=[qspec, qspec, qspec],
                  out_shape=[jax.ShapeDtypeStruct((t, c), BF16), jax.ShapeDtypeStruct((t, c), F32), jax.ShapeDtypeStruct((t, c), F32)],
                  sem=("parallel", "parallel"), comms=comms)


def _sb_bwd(zc, ctot, swept, do, name, comms=()):
    t = zc.shape[0]
    c = zc.shape[1] // 3
    heads = c // HEAD_DIM
    blk = min(SB_BLOCK, t)
    scale = HEAD_DIM ** -0.5

    def body(q_ref, k_ref, v_ref, ct_ref, nb_ref, do_ref, dq_ref, dk_ref, dv_ref):
        i = pl.program_id(1)

        @pl.when(i == 0)
        def _():
            dk_ref[...] = jnp.zeros_like(dk_ref)
            dv_ref[...] = jnp.zeros_like(dv_ref)

        ps = range(SB_PAIR)
        sl = [slice(p * HEAD_DIM, (p + 1) * HEAD_DIM) for p in ps]
        qs = [_sb_scaled(q_ref[:, s]) for s in sl]
        dov = [do_ref[:, s] for s in sl]
        c_tot = [ct_ref[:, p * HEAD_DIM:p * HEAD_DIM + 1] for p in ps]
        n_blocks = jnp.clip(jnp.max(nb_ref[0:8, :]).astype(jnp.int32), 1, i + 1)
        rows = lax.broadcasted_iota(jnp.int32, (blk, blk), 0)
        cols = lax.broadcasted_iota(jnp.int32, (blk, blk), 1)
        below = rows > cols
        m_upto = (rows <= cols).astype(BF16)
        m_left = (rows < cols).astype(BF16)

        def tile(j, carry, diagonal):
            dq, l_run, w_run = carry[:SB_PAIR], carry[SB_PAIR:2 * SB_PAIR], carry[2 * SB_PAIR:]
            off = pl.multiple_of(j * blk, blk)
            kj = [k_ref[pl.ds(off, blk), s] for s in sl]
            vj = [v_ref[pl.ds(off, blk), s] for s in sl]
            log_beta, l = _sb_logs(qs, kj, below if diagonal else None)
            d_a = [_dot(dov[p], vj[p], NT) for p in ps]
            upto = [_split_dot(x, m_upto) for x in l]
            a = [jnp.exp(log_beta[p] + (c_tot[p] - l_run[p] - upto[p])) for p in ps]
            if diagonal:
                a = [jnp.where(below, x, 0.0) for x in a]
            wgt = [a[p] * d_a[p] for p in ps]
            before = [w_run[p] + _split_dot(wgt[p], m_left) for p in ps]
            dz = [wgt[p] * jnp.exp(l[p]) - jnp.exp(log_beta[p]) * before[p] for p in ps]
            if diagonal:
                dz = [jnp.where(below, x, 0.0) for x in dz]
            dz16 = [x.astype(BF16) for x in dz]
            dk = [_dot(dz16[p], qs[p], TN) for p in ps]
            dv = [_dot(a[p].astype(BF16), dov[p], TN) for p in ps]
            dq = [dq[p] + _dot(dz16[p], kj[p], NN) for p in ps]
            for p in ps:
                dk_ref[pl.ds(off, blk), sl[p]] += dk[p]
                dv_ref[pl.ds(off, blk), sl[p]] += dv[p]
            return (*dq, *[l_run[p] + jnp.sum(l[p], axis=1, keepdims=True) for p in ps],
                    *[w_run[p] + jnp.sum(wgt[p], axis=1, keepdims=True) for p in ps])

        zeros = [jnp.zeros((blk, HEAD_DIM), F32)] * SB_PAIR + [jnp.zeros((blk, 1), F32)] * (2 * SB_PAIR)
        carry = lax.fori_loop(i + 1 - n_blocks, i, lambda j, carry: tile(j, carry, False), tuple(zeros))
        out = tile(i, carry, True)
        for p in ps:
            dq_ref[:, sl[p]] = out[p] * scale

    pairs = heads // SB_PAIR
    qspec = pl.BlockSpec((blk, SB_PAIR * HEAD_DIM), lambda h, i: (i, h))
    full = pl.BlockSpec((t, SB_PAIR * HEAD_DIM), lambda h, i: (0, h))
    return _pcall(body, (zc, zc, zc, ctot, swept, do), name=name, grid=(pairs, t // blk),
                  in_specs=[qspec, pl.BlockSpec((t, SB_PAIR * HEAD_DIM), lambda h, i: (0, pairs + h)),
                            pl.BlockSpec((t, SB_PAIR * HEAD_DIM), lambda h, i: (0, 2 * pairs + h)), qspec, qspec, qspec],
                  out_specs=[qspec, full, full], out_shape=[jax.ShapeDtypeStruct((t, c), F32)] * 3,
                  sem=("arbitrary", "arbitrary"), comms=comms)


def _concat_bf16(parts, name, comms=()):
    t, c = parts[0].shape

    def body(*refs):
        for k, r in enumerate(refs[:-1]):
            refs[-1][:, k * c:(k + 1) * c] = r[...].astype(BF16)

    res = _pcall(body, tuple(parts), name=name, grid=(t // NORM_ROWS,), in_specs=[_row_spec(c)] * len(parts),
                 out_specs=[_row_spec(c * len(parts))], out_shape=[jax.ShapeDtypeStruct((t, c * len(parts)), BF16)],
                 sem=("parallel",), comms=comms)
    return (res[0][0], res[1]) if comms else res[0]


KIND = {"ab_w_in": "col", "ab_w_out": "row", "sb_w_in": "col", "sb_w_out": "row",
        "ffn_w1_0": "col", "ffn_w1_1": "col", "ffn_w2_0": "row", "ffn_w2_1": "row"}
X_Y, DIAG, CHIPS = (2, 4), (6,), (2, 4, 6)


def _local_step(x, target, norms, sgu, big, bufs=None):
    g = {k: [v[l:l + 1] for l in range(2)] for k, v in norms.items()}
    ln_g, ln_b, sgu_w, sgu_b = sgu
    groups = sgu_w.shape[0]
    w16 = sgu_w.astype(BF16)
    bias_b = jnp.broadcast_to(sgu_b[:, :, None], (groups, CHUNK, CHUNK))
    big, dws, psum, dist = dict(big), {}, {}, bufs is not None
    pair, got = (dict(bufs[0]), dict(bufs[1])) if dist else ({}, {})

    def run(fn, *args, ops=(), **kw):
        if not dist or not ops:
            return fn(*args, **kw)
        make = {"gs": lambda k, p, *part: _GatherSend(big[k], KIND[k], p, *part), "gf": lambda k, p: _GatherFwd(big[k], KIND[k], p),
                "swap": lambda k, p: _PairSwap(dws[k], pair[k], KIND[k]),
                "chips": lambda k, p, *part: _ChipScatter(psum[k], got[k], p, *part)}
        out, rws = fn(*args, comms=[make[op[0]](*op[1:]) for op in ops], **kw)
        for (op, k, *_), r in zip(ops, rws):
            if op in ("gs", "gf"):
                big[k] = r[0]
            elif op == "swap":
                psum[k] = _pair_sum(dws[k], r[0], KIND[k], f"pair_sum_{k}")
            else:
                got[k] = r[0]
        return out

    h1_0 = _rms_fwd(x, g["pre_mix"][0], "rms_in")
    z0 = run(_matmul, h1_0, big["ab_w_in"], "nn", BF16, "ab_in", ops=[("gs", "ffn_w1_0", X_Y)])
    a_out = run(_sgu_fwd, z0, ln_g, ln_b, w16, bias_b, "sgu_fwd", ops=[("gf", "ffn_w1_0", X_Y), ("gs", "ab_w_out", CHIPS)])
    branches = [run(_dil_fwd, z0, 1, "dil_fwd_1", ops=[("gs", "ffn_w1_0", DIAG, (0, 2)), ("gf", "ab_w_out", CHIPS)]),
                run(_dil_fwd, z0, 4, "dil_fwd_4", ops=[("gs", "ffn_w1_0", DIAG, (1, 2))]),
                run(_dil_fwd, z0, 16, "dil_fwd_16", ops=[("gf", "ffn_w1_0", DIAG), ("gs", "ffn_w2_0", X_Y, (0, 2))])]
    ab, ltot = run(_dil_merge, a_out, [b[0] for b in branches], [b[1] for b in branches], "dil_merge",
                   ops=[("gs", "ffn_w2_0", X_Y, (1, 2))])
    y_0 = run(_matmul, ab, big["ab_w_out"], "nn", BF16, "ab_out", ops=[("gs", "ffn_w2_0", DIAG, (0, 2))])
    x1, h2_0 = run(_post_pre_fwd, y_0, g["post_mix"][0], x, g["pre_ffn"][0], "norm_mix0", ops=[("gs", "ffn_w2_0", DIAG, (1, 2))])
    r_0 = run(_matmul, h2_0, big["ffn_w1_0"], "nn", BF16, "ffn_up_0", relu_out=True,
              ops=[("gf", "ffn_w2_0", CHIPS), ("gs", "sb_w_in", CHIPS)])
    y2_0 = run(_matmul, r_0, big["ffn_w2_0"], "nn", BF16, "ffn_down_0", a_square=True,
               ops=[("gf", "sb_w_in", CHIPS), ("gs", "sb_w_out", CHIPS), ("gs", "ffn_w1_1", X_Y)])
    x2, h1_1 = run(_post_pre_fwd, y2_0, g["post_ffn"][0], x1, g["pre_mix"][1], "norm_ffn0",
                   ops=[("gf", "ffn_w1_1", X_Y), ("gf", "sb_w_out", CHIPS)])
    zc = run(_matmul, h1_1, big["sb_w_in"], "nn", BF16, "sb_in", ops=[("gs", "ffn_w1_1", DIAG)])
    o_sb, ct_sb, nb_sb = run(_sb_fwd, zc, "sb_fwd", ops=[("gf", "ffn_w1_1", DIAG), ("gs", "ffn_w2_1", CHIPS)])
    y_1 = run(_matmul, o_sb, big["sb_w_out"], "nn", BF16, "sb_out", ops=[("gf", "ffn_w2_1", CHIPS)])
    x3, h2_1 = _post_pre_fwd(y_1, g["post_mix"][1], x2, g["pre_ffn"][1], "norm_mix1")
    r_1 = _matmul(h2_1, big["ffn_w1_1"], "nn", BF16, "ffn_up_1", relu_out=True)
    y2_1 = _matmul(r_1, big["ffn_w2_1"], "nn", BF16, "ffn_down_1", a_square=True)
    loss, dx4, dy2_1, dg_post_ffn1 = _final_fwd_bwd(y2_1, g["post_ffn"][1], x3, target, "loss")

    da = _matmul(dy2_1, big["ffn_w2_1"], "nt", BF16, "ffn_da_1", mul2=r_1)
    dws["ffn_w2_1"] = _matmul(r_1, dy2_1, "tn", BF16, "ffn_dw2_1", a_square=True)
    dh2 = run(_matmul, da, big["ffn_w1_1"], "nt", BF16, "ffn_dh_1", ops=[("swap", "ffn_w2_1", None)])
    dws["ffn_w1_1"] = run(_matmul, h2_1, da, "tn", BF16, "ffn_dw1_1", ops=[("chips", "ffn_w2_1", X_Y)])
    dx3, dy_1, dg_pre_ffn1, dg_post_mix1 = run(_pre_post_bwd, x3, g["pre_ffn"][1], dh2, dx4, y_1, g["post_mix"][1], "norm_bwd_mix1",
                                               ops=[("swap", "ffn_w1_1", None)])
    do_sb = _matmul(dy_1, big["sb_w_out"], "nt", BF16, "sb_out_dx")
    dws["sb_w_out"] = _matmul(o_sb, dy_1, "tn", BF16, "sb_out_dw")
    dqkv = run(_sb_bwd, zc, ct_sb, nb_sb, do_sb, "sb_bwd",
               ops=[("chips", "ffn_w2_1", DIAG), ("chips", "ffn_w1_1", CHIPS), ("swap", "sb_w_out", None)])
    dzc = run(_concat_bf16, dqkv, "sb_dz", ops=[("chips", "sb_w_out", X_Y)])
    dh1 = run(_matmul, dzc, big["sb_w_in"], "nt", BF16, "sb_in_dx", ops=[("chips", "sb_w_out", DIAG)])
    dws["sb_w_in"] = _matmul(h1_1, dzc, "tn", BF16, "sb_in_dw")
    dx2, dy2_0, dg_pre_mix1, dg_post_ffn0 = run(_pre_post_bwd, x2, g["pre_mix"][1], dh1, dx3, y2_0, g["post_ffn"][0], "norm_bwd_ffn0",
                                                ops=[("swap", "sb_w_in", None)])
    da = run(_matmul, dy2_0, big["ffn_w2_0"], "nt", BF16, "ffn_da_0", mul2=r_0, ops=[("chips", "sb_w_in", X_Y)])
    dws["ffn_w2_0"] = run(_matmul, r_0, dy2_0, "tn", BF16, "ffn_dw2_0", a_square=True, ops=[("chips", "sb_w_in", DIAG)])
    dws["ffn_w1_0"] = run(_matmul, h2_0, da, "tn", BF16, "ffn_dw1_0", ops=[("swap", "ffn_w2_0", None)])
    dh2 = run(_matmul, da, big["ffn_w1_0"], "nt", BF16, "ffn_dh_0", ops=[("chips", "ffn_w2_0", X_Y), ("swap", "ffn_w1_0", None)])
    dx1, dy_0, dg_pre_ffn0, dg_post_mix0 = run(_pre_post_bwd, x1, g["pre_ffn"][0], dh2, dx2, y_0, g["post_mix"][0], "norm_bwd_mix0",
                                               ops=[("chips", "ffn_w2_0", DIAG, (0, 2))])
    dab = run(_matmul, dy_0, big["ab_w_out"], "nt", BF16, "ab_out_dx", ops=[("chips", "ffn_w2_0", DIAG, (1, 2))])
    dws["ab_w_out"] = run(_matmul, ab, dy_0, "tn", BF16, "ab_out_dw", ops=[("chips", "ffn_w1_0", X_Y, (0, 2))])
    duv, d_ln_g, d_ln_b, d_sgu_w, d_sgu_b = run(_sgu_bwd, z0, dab, ln_g, ln_b, w16, bias_b, "sgu_bwd",
                                                ops=[("chips", "ffn_w1_0", X_Y, (1, 2))])
    delta = _dil_delta(ab, dab, "dil_delta")
    parts = [run(_dil_bwd, z0, dab, ltot, delta, 1, "dil_bwd_1", ops=[("chips", "ffn_w1_0", DIAG, (0, 2)), ("swap", "ab_w_out", None)]),
             run(_dil_bwd, z0, dab, ltot, delta, 4, "dil_bwd_4", ops=[("chips", "ffn_w1_0", DIAG, (1, 2))]),
             run(_dil_bwd, z0, dab, ltot, delta, 16, "dil_bwd_16", ops=[("chips", "ab_w_out", CHIPS)])]
    dz0 = _dz_assemble(duv, parts, "dz_assemble")
    dws["ab_w_in"] = _matmul(h1_0, dz0, "tn", BF16, "ab_in_dw")
    dh1 = run(_matmul, dz0, big["ab_w_in"], "nt", BF16, "ab_in_dx", ops=[("swap", "ab_w_in", None)])
    grad_x, dg_pre_mix0 = run(_pre_post_bwd, x, g["pre_mix"][0], dh1, dx1, None, None, "norm_bwd_in", ops=[("chips", "ab_w_in", X_Y)])

    d_norms = {
        "pre_mix": jnp.concatenate([dg_pre_mix0, dg_pre_mix1]), "post_mix": jnp.concatenate([dg_post_mix0, dg_post_mix1]),
        "pre_ffn": jnp.concatenate([dg_pre_ffn0, dg_pre_ffn1]), "post_ffn": jnp.concatenate([dg_post_ffn0, dg_post_ffn1]),
    }
    return loss, grad_x, d_norms, (d_ln_g, d_ln_b, d_sgu_w, d_sgu_b), (psum, got) if dist else dws


def _to_bf16_full(w, layer, kind, name):
    _, rows, cols = w.shape
    tr = _tile(rows, 512)
    nblk = rows // tr
    full = (rows, 4 * cols) if kind == "col" else (4 * rows, cols)

    def body(w_ref, o_ref):
        o_ref[...] = w_ref[...].astype(BF16)

    def place(i):
        mine = 2 * lax.axis_index("x") + lax.axis_index("y")
        return (i, mine) if kind == "col" else (mine * nblk + i, 0)

    return pl.pallas_call(
        body, name=name, grid=(nblk,), in_specs=[pl.BlockSpec((None, tr, cols), lambda i: (layer, i, 0))],
        out_specs=pl.BlockSpec((tr, cols), place), out_shape=jax.ShapeDtypeStruct(full, BF16), compiler_params=_params("parallel"),
    )(w)


def _pair_sum(dw16, pair, kind, name):
    rh, cs = _half_shape(dw16.shape, kind)
    tr = _tile(rh, 256)
    nblk = rh // tr

    def body(dw_ref, pair_ref, o_ref):
        o_ref[...] = (dw_ref[...].astype(F32) + pair_ref[...].astype(F32)).astype(BF16)

    def own(s, i):
        c = lax.axis_index("c")
        return (c * nblk + i, s) if kind == "col" else ((2 * s + c) * nblk + i, 0)

    spec3 = pl.BlockSpec((None, tr, cs), lambda s, i: (s, i, 0))
    return pl.pallas_call(
        body, name=name, grid=(4, nblk), in_specs=[pl.BlockSpec((tr, cs), own), spec3], out_specs=spec3,
        out_shape=jax.ShapeDtypeStruct((4, rh, cs), BF16), compiler_params=_params("parallel", "parallel"),
    )(dw16, pair)


def _owner_sum(psum, got, buf, layer, name, comms=()):
    _, rh, cs = psum.shape
    tr = _tile(rh, 256)

    def body(p_ref, got_ref, buf_ref, o_ref):
        tot = p_ref[...].astype(F32)
        for j in range(3):
            tot = tot + got_ref[j].astype(F32)
        o_ref[...] = tot

    res = _pcall(
        body, (psum, got, buf), name=name, grid=(rh // tr,),
        in_specs=[pl.BlockSpec((None, tr, cs), lambda i: (2 * lax.axis_index("x") + lax.axis_index("y"), i, 0)),
                  pl.BlockSpec((3, tr, cs), lambda i: (0, i, 0)), ANY],
        out_specs=[pl.BlockSpec((None, None, tr, cs), lambda i: (layer, lax.axis_index("c"), i, 0))],
        out_shape=[jax.ShapeDtypeStruct(buf.shape, F32)], sem=("parallel",), comms=comms, aliases={2: 0})
    return (res[0][0], res[1]) if comms else res[0]


def _adamw_math(w, g, m, v):
    m = ADAM_B1 * m + (1.0 - ADAM_B1) * g
    v = ADAM_B2 * v + (1.0 - ADAM_B2) * (g * g)
    m_hat = m / (1.0 - ADAM_B1 ** ADAM_STEP)
    v_hat = v / (1.0 - ADAM_B2 ** ADAM_STEP)
    return -ADAM_LR * (m_hat / (jnp.sqrt(v_hat) + ADAM_EPS) + ADAM_WD * w), m, v


def _adamw(w, g, m, v, name):
    layers, rows, cols = w.shape
    tr = _tile(rows, 256)

    def body(w_ref, g_ref, m_ref, v_ref, go_ref, d_ref, mo_ref, vo_ref):
        g = g_ref[...]
        go_ref[...] = g
        d_ref[...], mo_ref[...], vo_ref[...] = _adamw_math(w_ref[...], g, m_ref[...], v_ref[...])

    spec = pl.BlockSpec((None, tr, cols), lambda l, i: (l, i, 0))
    return _pcall(body, (w, g, m, v), name=name, grid=(layers, rows // tr), in_specs=[spec] * 4, out_specs=[spec] * 4,
                  out_shape=[jax.ShapeDtypeStruct(w.shape, F32)] * 4, sem=("parallel", "parallel"))


def _pack(arrays):
    flat = jnp.concatenate([a.reshape(-1) for a in arrays])
    pad = (-flat.shape[0]) % 1024
    return jnp.pad(flat, (0, pad)).reshape(-1, 128)


def _unpack(packed, like):
    flat = packed.reshape(-1)
    out, off = [], 0
    for a in like:
        out.append(flat[off:off + a.size].reshape(a.shape))
        off += a.size
    return out


class _SmallGather:
    def __init__(self, g, parts, patterns):
        self.ro, self.rw, self.patterns, self.n_sems = [g], [parts], patterns, len(patterns)

    def start(self, ro, rw, send, recv):
        x, y, c, _ = _place()
        for k, j in enumerate(self.patterns):
            _remote(ro[0], rw[0].at[4 * x + 2 * y + c], send(k), recv(k), _flip(x, y, c, j)).start()

    def finish(self, ro, rw, send, recv):
        x, y, c, _ = _place()
        for k, j in enumerate(self.patterns):
            px, py, pc = _flip(x, y, c, j)
            slot = rw[0].at[4 * px + 2 * py + pc]
            cp = _remote(slot, slot, send(k), recv(k), (x, y, c))
            cp.wait_recv()
            cp.wait_send()


def _small_update(own, parts, w, m, v, name):
    rows = w.shape[0]

    def body(own_ref, p_ref, w_ref, m_ref, v_ref, g_ref, d_ref, mo_ref, vo_ref):
        me = 4 * lax.axis_index("x") + 2 * lax.axis_index("y") + lax.axis_index("c")
        g = jnp.where(me == 0, own_ref[...], p_ref[0])
        for k in range(1, 8):
            g = g + jnp.where(me == k, own_ref[...], p_ref[k])
        g_ref[...] = g
        d_ref[...], mo_ref[...], vo_ref[...] = _adamw_math(w_ref[...], g, m_ref[...], v_ref[...])

    return pl.pallas_call(body, name=name, out_shape=[jax.ShapeDtypeStruct((rows, 128), F32)] * 4,
                          compiler_params=_params())(own, parts, w, m, v)


SMALL = ("norm_pre_mix", "norm_post_mix", "norm_pre_ffn", "norm_post_ffn", "sgu_ln_g", "sgu_ln_b", "sgu_w", "sgu_b")
BIG = (("ab_w_in", ("ab_w_in",)), ("ab_w_out", ("ab_w_out",)), ("sb_w_in", ("sb_w_in",)), ("sb_w_out", ("sb_w_out",)),
       ("ffn_w1", ("ffn_w1_0", "ffn_w1_1")), ("ffn_w2", ("ffn_w2_0", "ffn_w2_1")))
WEIGHTS = ("norm_pre_mix", "norm_post_mix", "norm_pre_ffn", "norm_post_ffn", "ab_w_in", "sgu_ln_g", "sgu_ln_b", "sgu_w", "sgu_b",
           "ab_w_out", "sb_w_in", "sb_w_out", "ffn_w1", "ffn_w2")


def kernel(x, norm_pre_mix, norm_post_mix, norm_pre_ffn, norm_post_ffn, ab_w_in, sgu_ln_g, sgu_ln_b, sgu_w, sgu_b, ab_w_out, sb_w_in, sb_w_out, ffn_w1, ffn_w2, loss_target, m_norm_pre_mix, m_norm_post_mix, m_norm_pre_ffn, m_norm_post_ffn, m_ab_w_in, m_sgu_ln_g, m_sgu_ln_b, m_sgu_w, m_sgu_b, m_ab_w_out, m_sb_w_in, m_sb_w_out, m_ffn_w1, m_ffn_w2, v_norm_pre_mix, v_norm_post_mix, v_norm_pre_ffn, v_norm_post_ffn, v_ab_w_in, v_sgu_ln_g, v_sgu_ln_b, v_sgu_w, v_sgu_b, v_ab_w_out, v_sb_w_in, v_sb_w_out, v_ffn_w1, v_ffn_w2):
    w = dict(norm_pre_mix=norm_pre_mix, norm_post_mix=norm_post_mix, norm_pre_ffn=norm_pre_ffn, norm_post_ffn=norm_post_ffn,
             ab_w_in=ab_w_in, sgu_ln_g=sgu_ln_g, sgu_ln_b=sgu_ln_b, sgu_w=sgu_w, sgu_b=sgu_b, ab_w_out=ab_w_out, sb_w_in=sb_w_in,
             sb_w_out=sb_w_out, ffn_w1=ffn_w1, ffn_w2=ffn_w2)
    m = dict(norm_pre_mix=m_norm_pre_mix, norm_post_mix=m_norm_post_mix, norm_pre_ffn=m_norm_pre_ffn, norm_post_ffn=m_norm_post_ffn,
             ab_w_in=m_ab_w_in, sgu_ln_g=m_sgu_ln_g, sgu_ln_b=m_sgu_ln_b, sgu_w=m_sgu_w, sgu_b=m_sgu_b, ab_w_out=m_ab_w_out,
             sb_w_in=m_sb_w_in, sb_w_out=m_sb_w_out, ffn_w1=m_ffn_w1, ffn_w2=m_ffn_w2)
    v = dict(norm_pre_mix=v_norm_pre_mix, norm_post_mix=v_norm_post_mix, norm_pre_ffn=v_norm_pre_ffn, norm_post_ffn=v_norm_post_ffn,
             ab_w_in=v_ab_w_in, sgu_ln_g=v_sgu_ln_g, sgu_ln_b=v_sgu_ln_b, sgu_w=v_sgu_w, sgu_b=v_sgu_b, ab_w_out=v_ab_w_out,
             sb_w_in=v_sb_w_in, sb_w_out=v_sb_w_out, ffn_w1=v_ffn_w1, ffn_w2=v_ffn_w2)
    big, pair, got = {}, {}, {}
    for name, keys in BIG:
        for layer, key in enumerate(keys):
            big[key] = _to_bf16_full(w[name], layer, KIND[key], f"bf16_{key}")
            half = _half_shape(big[key].shape, KIND[key])
            pair[key], got[key] = lax.empty((4,) + half, BF16), lax.empty((3,) + half, BF16)
    big["ab_w_in"] = _comm_call([_Gather(big["ab_w_in"], KIND["ab_w_in"])], "gather_first")[0][0]

    norms = {k: w["norm_" + k] for k in ("pre_mix", "post_mix", "pre_ffn", "post_ffn")}
    sgu = (sgu_ln_g, sgu_ln_b, sgu_w[0], sgu_b[0])
    loss_blk, grad_x, d_norms, d_sgu, (psum, got) = _local_step(x[0], loss_target[0], norms, sgu, big, (pair, got))
    loss = lax.psum(loss_blk[0, 0], ("x", "y", "c"))

    grads, deltas, new_m, new_v = {}, {}, {}, {}
    keys_of = dict(BIG)
    small_g = _pack([d_norms["pre_mix"], d_norms["post_mix"], d_norms["pre_ffn"], d_norms["post_ffn"],
                     d_sgu[0], d_sgu[1], d_sgu[2][None], d_sgu[3][None]])
    parts = lax.empty((8,) + small_g.shape, F32)
    small_todo = [(1, 2, 4, 6), (3, 5, 7)]
    bufs, pending = {}, None
    for name in ("ffn_w2", "ffn_w1", "sb_w_in", "sb_w_out", "ab_w_out"):
        buf = lax.empty((len(keys_of[name]), 2) + psum[keys_of[name][0]].shape[1:], F32)
        for layer, key in enumerate(keys_of[name]):
            if pending is not None:
                buf, rws = _owner_sum(psum[key], got[key], buf, layer, f"sum_{key}", comms=[_Join([bufs[pending]])])
                bufs[pending], pending = rws[0][0], None
            elif small_todo:
                buf, rws = _owner_sum(psum[key], got[key], buf, layer, f"sum_{key}",
                                      comms=[_SmallGather(small_g, parts, small_todo.pop(0))])
                parts = rws[0][0]
            else:
                buf = _owner_sum(psum[key], got[key], buf, layer, f"sum_{key}")
        bufs[name], pending = buf, name
    assert not small_todo

    rws = _comm_call([_Join([bufs["ab_w_out"]]), _ChipScatter(psum["ab_w_in"], got["ab_w_in"], DIAG)], "tail_comm")
    bufs["ab_w_out"], got["ab_w_in"] = rws[0][0], rws[1][0]
    bufs["ab_w_in"] = _owner_sum(psum["ab_w_in"], got["ab_w_in"], lax.empty((1, 2) + psum["ab_w_in"].shape[1:], F32), 0, "sum_ab_w_in")
    bufs["ab_w_in"] = _comm_call([_Join([bufs["ab_w_in"]])], "join_last")[0][0]
    for name, _ in BIG:
        grads[name], deltas[name], new_m[name], new_v[name] = _adamw(w[name], bufs[name].reshape(w[name].shape), m[name], v[name], f"adamw_{name}")

    outs = _small_update(small_g, parts, _pack([w[k] for k in SMALL]), _pack([m[k] for k in SMALL]), _pack([v[k] for k in SMALL]), "small_update")
    like = [w[k] for k in SMALL]
    for dst, packed in zip((grads, deltas, new_m, new_v), outs):
        for k, a in zip(SMALL, _unpack(packed, like)):
            dst[k] = a

    return (loss, grad_x[None], *[grads[k] for k in WEIGHTS], *[deltas[k] for k in WEIGHTS],
            *[new_m[k] for k in WEIGHTS], *[new_v[k] for k in WEIGHTS])
```

```python
import functools

import jax
import jax.numpy as jnp
from jax import lax
from jax.experimental import pallas as pl
from jax.experimental.pallas import tpu as pltpu

F32 = jnp.float32
BF16 = jnp.bfloat16
MESH = pl.DeviceIdType.MESH

HEAD_DIM = 128
CHUNK = 128
DILATIONS = (1, 4, 16)
SB_BLOCK = 256
RMS_EPS = 1e-6
LN_EPS = 1e-5
ADAM_LR, ADAM_B1, ADAM_B2, ADAM_EPS, ADAM_WD, ADAM_STEP = 0.001, 0.9, 0.999, 1e-08, 0.01, 10
NEG = -1e30
V7X_VMEM_LIMIT = 48 * 1024 * 1024
ANY = pl.BlockSpec(memory_space=pl.ANY)


def _params(*sem):
    return pltpu.CompilerParams(dimension_semantics=sem if sem else None, vmem_limit_bytes=V7X_VMEM_LIMIT)


def _tile(n, pref):
    if n <= pref:
        return n
    t = pref
    while n % t:
        t -= 128
    return t


def _dot(a, b, dims):
    return lax.dot_general(a, b, (dims, ((), ())), preferred_element_type=F32)


NN = ((1,), (0,))
NT = ((1,), (1,))
TN = ((0,), (0,))


def _place():
    x, y, c = lax.axis_index("x"), lax.axis_index("y"), lax.axis_index("c")
    return x, y, c, 2 * x + y


def _flip(x, y, c, j):
    return (1 - x if j & 4 else x), (1 - y if j & 2 else y), (1 - c if j & 1 else c)


def _half_shape(full_shape, kind):
    rows, cols = full_shape
    return (rows // 2, cols // 4) if kind == "col" else (rows // 8, cols)


def _half(ref, kind, s, h):
    rh, cs = _half_shape(ref.shape, kind)
    if kind == "col":
        return ref.at[pl.ds(h * rh, rh), pl.ds(s * cs, cs)]
    return ref.at[pl.ds((2 * s + h) * rh, rh), :]


def _remote(src, dst, send, recv, to):
    return pltpu.make_async_remote_copy(src_ref=src, dst_ref=dst, send_sem=send, recv_sem=recv, device_id=to, device_id_type=MESH)


class _Gather:
    n_sems = 6

    def __init__(self, full, kind):
        self.ro, self.rw, self.kind = [], [full], kind

    def start(self, ro, rw, send, recv):
        x, y, c, mine = _place()
        own = _half(rw[0], self.kind, mine, c)
        for k, j in enumerate((2, 4, 6)):
            px, py, _ = _flip(x, y, c, j)
            _remote(own, own, send(k), recv(k), (px, py, c)).start()

    def finish(self, ro, rw, send, recv):
        x, y, c, mine = _place()
        own = _half(rw[0], self.kind, mine, c)
        for k, j in enumerate((2, 4, 6)):
            px, py, _ = _flip(x, y, c, j)
            got = _half(rw[0], self.kind, 2 * px + py, c)
            _remote(got, got, send(k), recv(k), (x, y, c)).wait_recv()
            _remote(got, got, send(3 + k), recv(3 + k), (x, y, 1 - c)).start()
        for k, j in enumerate((2, 4, 6)):
            px, py, _ = _flip(x, y, c, j)
            got = _half(rw[0], self.kind, 2 * px + py, 1 - c)
            _remote(got, got, send(3 + k), recv(3 + k), (x, y, c)).wait_recv()
        for k in range(6):
            _remote(own, own, send(k), recv(k), (x, y, c)).wait_send()


class _GatherSend:
    def __init__(self, full, kind, patterns, part=(0, 1)):
        self.ro, self.rw, self.kind, self.patterns, self.part, self.n_sems = [], [full], kind, patterns, part, len(patterns)

    def _rows(self, half):
        i, n = self.part
        rows = half.shape[0] // n
        return half.at[pl.ds(i * rows, rows), :]

    def start(self, ro, rw, send, recv):
        x, y, c, mine = _place()
        own = self._rows(_half(rw[0], self.kind, mine, c))
        for k, j in enumerate(self.patterns):
            px, py, _ = _flip(x, y, c, j)
            _remote(own, own, send(k), recv(k), (px, py, c)).start()

    def finish(self, ro, rw, send, recv):
        x, y, c, _ = _place()
        for k, j in enumerate(self.patterns):
            px, py, _ = _flip(x, y, c, j)
            got = self._rows(_half(rw[0], self.kind, 2 * px + py, c))
            cp = _remote(got, got, send(k), recv(k), (x, y, c))
            cp.wait_recv()
            cp.wait_send()


class _GatherFwd:
    def __init__(self, full, kind, patterns):
        self.ro, self.rw, self.kind, self.patterns, self.n_sems = [], [full], kind, patterns, len(patterns)

    def start(self, ro, rw, send, recv):
        x, y, c, _ = _place()
        for k, j in enumerate(self.patterns):
            px, py, _ = _flip(x, y, c, j)
            got = _half(rw[0], self.kind, 2 * px + py, c)
            _remote(got, got, send(k), recv(k), (x, y, 1 - c)).start()

    def finish(self, ro, rw, send, recv):
        x, y, c, _ = _place()
        for k, j in enumerate(self.patterns):
            px, py, _ = _flip(x, y, c, j)
            got = _half(rw[0], self.kind, 2 * px + py, 1 - c)
            cp = _remote(got, got, send(k), recv(k), (x, y, c))
            cp.wait_recv()
            cp.wait_send()


class _PairSwap:
    n_sems = 4

    def __init__(self, dw16, pair, kind):
        self.ro, self.rw, self.kind = [dw16], [pair], kind

    def start(self, ro, rw, send, recv):
        x, y, c, _ = _place()
        for s in range(4):
            _remote(_half(ro[0], self.kind, s, 1 - c), rw[0].at[s], send(s), recv(s), (x, y, 1 - c)).start()

    def finish(self, ro, rw, send, recv):
        x, y, c, _ = _place()
        for s in range(4):
            cp = _remote(rw[0].at[s], rw[0].at[s], send(s), recv(s), (x, y, c))
            cp.wait_recv()
            cp.wait_send()


class _ChipScatter:
    def __init__(self, psum, got, patterns, part=(0, 1)):
        self.ro, self.rw, self.patterns, self.part, self.n_sems = [psum], [got], patterns, part, len(patterns)

    def _rows(self, ref, slot):
        i, n = self.part
        rows = ref.shape[1] // n
        return ref.at[slot, pl.ds(i * rows, rows), :]

    def start(self, ro, rw, send, recv):
        x, y, c, _ = _place()
        for k, j in enumerate(self.patterns):
            px, py, _ = _flip(x, y, c, j)
            _remote(self._rows(ro[0], 2 * px + py), self._rows(rw[0], j // 2 - 1), send(k), recv(k), (px, py, c)).start()

    def finish(self, ro, rw, send, recv):
        x, y, c, _ = _place()
        for k, j in enumerate(self.patterns):
            slot = self._rows(rw[0], j // 2 - 1)
            cp = _remote(slot, slot, send(k), recv(k), (x, y, c))
            cp.wait_recv()
            cp.wait_send()


class _Join:
    def __init__(self, bufs):
        self.ro, self.rw, self.n_sems = [], list(bufs), sum(b.shape[0] for b in bufs)

    def _copies(self, rw, send, recv, slot):
        x, y, c, _ = _place()
        k = 0
        for ref in rw:
            for l in range(ref.shape[0]):
                yield _remote(ref.at[l, c], ref.at[l, slot(c)], send(k), recv(k), (x, y, 1 - c))
                k += 1

    def start(self, ro, rw, send, recv):
        for cp in self._copies(rw, send, recv, lambda c: c):
            cp.start()

    def finish(self, ro, rw, send, recv):
        for cp in self._copies(rw, send, recv, lambda c: 1 - c):
            cp.wait_recv()
        for cp in self._copies(rw, send, recv, lambda c: c):
            cp.wait_send()


def _comm_layout(comms):
    ro = [a for c in comms for a in c.ro]
    rw = [a for c in comms for a in c.rw]
    return ro, rw, sum(c.n_sems for c in comms)


def _comm_each(comms, method, ro_refs, rw_refs, send, recv):
    i_ro = i_rw = i_sem = 0
    for c in comms:
        getattr(c, method)(ro_refs[i_ro:i_ro + len(c.ro)], rw_refs[i_rw:i_rw + len(c.rw)],
                           lambda k, b=i_sem: send.at[b + k], lambda k, b=i_sem: recv.at[b + k])
        i_ro, i_rw, i_sem = i_ro + len(c.ro), i_rw + len(c.rw), i_sem + c.n_sems


def _split_results(comms, rws):
    out, i = [], 0
    for c in comms:
        out.append(list(rws[i:i + len(c.rw)]))
        i += len(c.rw)
    return out


def _comm_call(comms, name):
    ro, rw, n_sems = _comm_layout(comms)

    def body(*refs):
        ro_refs = refs[:len(ro)]
        rw_refs = refs[len(ro) + len(rw):len(ro) + 2 * len(rw)]
        send, recv = refs[len(ro) + 2 * len(rw):]
        _comm_each(comms, "start", ro_refs, rw_refs, send, recv)
        _comm_each(comms, "finish", ro_refs, rw_refs, send, recv)

    rws = pl.pallas_call(
        body, name=name, in_specs=[ANY] * (len(ro) + len(rw)), out_specs=[ANY] * len(rw),
        out_shape=[jax.ShapeDtypeStruct(a.shape, a.dtype) for a in rw],
        input_output_aliases={len(ro) + k: k for k in range(len(rw))},
        scratch_shapes=[pltpu.SemaphoreType.DMA((n_sems,)), pltpu.SemaphoreType.DMA((n_sems,))],
    )(*ro, *rw)
    return _split_results(comms, rws)


def _pcall(body, args, *, name, grid, in_specs, out_specs, out_shape, scratch=(), sem=(), comms=(), aliases=None):
    n_in, n_out, n_scr = len(in_specs), len(out_specs), len(scratch)
    aliases = dict(aliases or {})
    if not comms:
        return pl.pallas_call(body, name=name, grid=grid, in_specs=list(in_specs), out_specs=list(out_specs),
                              out_shape=list(out_shape), scratch_shapes=list(scratch), input_output_aliases=aliases,
                              compiler_params=_params(*sem))(*args)
    ro, rw, n_sems = _comm_layout(comms)

    def carrier(*refs):
        ins = refs[:n_in]
        ro_refs = refs[n_in:n_in + len(ro)]
        o0 = n_in + len(ro) + len(rw)
        outs = refs[o0:o0 + n_out]
        rw_refs = refs[o0 + n_out:o0 + n_out + len(rw)]
        s0 = o0 + n_out + len(rw)
        send, recv = refs[s0 + n_scr], refs[s0 + n_scr + 1]
        ids = [pl.program_id(a) for a in range(len(grid))]
        first = functools.reduce(jnp.logical_and, [i == 0 for i in ids])
        last = functools.reduce(jnp.logical_and, [i == g - 1 for i, g in zip(ids, grid)])

        @pl.when(first)
        def _():
            _comm_each(comms, "start", ro_refs, rw_refs, send, recv)

        body(*ins, *outs, *refs[s0:s0 + n_scr])

        @pl.when(last)
        def _():
            _comm_each(comms, "finish", ro_refs, rw_refs, send, recv)

    res = pl.pallas_call(
        carrier, name=name, grid=grid, in_specs=list(in_specs) + [ANY] * (len(ro) + len(rw)),
        out_specs=list(out_specs) + [ANY] * len(rw),
        out_shape=list(out_shape) + [jax.ShapeDtypeStruct(a.shape, a.dtype) for a in rw],
        input_output_aliases={**aliases, **{n_in + len(ro) + k: n_out + k for k in range(len(rw))}},
        scratch_shapes=list(scratch) + [pltpu.SemaphoreType.DMA((n_sems,)), pltpu.SemaphoreType.DMA((n_sems,))],
        compiler_params=_params(*["arbitrary"] * len(grid)),
    )(*args, *ro, *rw)
    return list(res[:n_out]), _split_results(comms, res[n_out:])


def _matmul(a, b, mode, out_dtype, name, a_square=False, relu_out=False, mul2=None, comms=()):
    if mode == "nn":
        (m, k), n = a.shape, b.shape[1]
    elif mode == "nt":
        (m, k), n = a.shape, b.shape[0]
    else:
        (k, m), n = a.shape, b.shape[1]
    tm, tn, tk = _tile(m, 1024), _tile(n, 2048 if out_dtype == BF16 else 1024), _tile(k, 2048)
    nk = k // tk
    dims = {"nn": NN, "nt": NT, "tn": TN}[mode]
    a_spec = pl.BlockSpec((tk, tm), lambda i, j, kk: (kk, i)) if mode == "tn" else pl.BlockSpec((tm, tk), lambda i, j, kk: (i, kk))
    b_spec = pl.BlockSpec((tn, tk), lambda i, j, kk: (j, kk)) if mode == "nt" else pl.BlockSpec((tk, tn), lambda i, j, kk: (kk, j))
    o_spec = pl.BlockSpec((tm, tn), lambda i, j, kk: (i, j))

    def body(a_ref, b_ref, *rest):
        m_ref = None if mul2 is None else rest[0]
        o_ref = rest[0 if mul2 is None else 1]
        kk = pl.program_id(2)

        def partial():
            av = a_ref[...]
            if a_square:
                av = av * av
            return _dot(av, b_ref[...], dims)

        def finish(r):
            if relu_out:
                r = jnp.maximum(r, 0.0)
            if mul2 is not None:
                r = r * (2.0 * m_ref[...].astype(F32))
            o_ref[...] = r.astype(out_dtype)

        if nk == 1:
            finish(partial())
            return
        acc_ref = rest[-1]

        @pl.when(kk == 0)
        def _():
            acc_ref[...] = partial()

        @pl.when(kk > 0)
        def _():
            acc_ref[...] += partial()

        @pl.when(kk == nk - 1)
        def _():
            finish(acc_ref[...])

    args = (a, b) if mul2 is None else (a, b, mul2)
    specs = [a_spec, b_spec] + ([] if mul2 is None else [o_spec])
    res = _pcall(body, args, name=name, grid=(m // tm, n // tn, nk), in_specs=specs, out_specs=[o_spec],
                 out_shape=[jax.ShapeDtypeStruct((m, n), out_dtype)], scratch=[pltpu.VMEM((tm, tn), F32)] if nk > 1 else [],
                 sem=("parallel", "parallel", "arbitrary"), comms=comms)
    return (res[0][0], res[1]) if comms else res[0]


NORM_ROWS = 256


def _rms(x, g):
    rstd = lax.rsqrt(jnp.mean(x * x, axis=-1, keepdims=True) + RMS_EPS)
    n = x * rstd
    return n * g, n, rstd


def _rms_bwd(n, rstd, g, dout):
    dn = dout * g
    return rstd * (dn - n * jnp.mean(dn * n, axis=-1, keepdims=True))


def _row_spec(d):
    return pl.BlockSpec((NORM_ROWS, d), lambda i: (i, 0))


def _vec_spec(d):
    return pl.BlockSpec((1, d), lambda i: (0, 0))


def _accumulate(ref, val):
    @pl.when(pl.program_id(0) == 0)
    def _():
        ref[...] = jnp.zeros_like(ref)

    ref[...] += val


def _rms_fwd(x, g, name):
    t, d = x.shape

    def body(x_ref, g_ref, h_ref):
        h_ref[...] = _rms(x_ref[...], g_ref[...])[0].astype(BF16)

    return pl.pallas_call(
        body, name=name, grid=(t // NORM_ROWS,), in_specs=[_row_spec(d), _vec_spec(d)], out_specs=_row_spec(d),
        out_shape=jax.ShapeDtypeStruct((t, d), BF16), compiler_params=_params("parallel"),
    )(x, g)


def _post_pre_fwd(y, g_post, x, g_pre, name, comms=()):
    t, d = x.shape

    def body(y_ref, gp_ref, x_ref, gn_ref, xn_ref, h_ref):
        xn = x_ref[...] + _rms(y_ref[...].astype(F32), gp_ref[...])[0]
        xn_ref[...] = xn
        h_ref[...] = _rms(xn, gn_ref[...])[0].astype(BF16)

    return _pcall(
        body, (y, g_post, x, g_pre), name=name, grid=(t // NORM_ROWS,),
        in_specs=[_row_spec(d), _vec_spec(d), _row_spec(d), _vec_spec(d)], out_specs=[_row_spec(d), _row_spec(d)],
        out_shape=[jax.ShapeDtypeStruct((t, d), F32), jax.ShapeDtypeStruct((t, d), BF16)], sem=("parallel",), comms=comms)


def _final_fwd_bwd(y, g_post, x, target, name):
    t, d = x.shape

    def body(y_ref, g_ref, x_ref, t_ref, loss_ref, dx_ref, dy_ref, dg_ref):
        g = g_ref[...]
        out, n, rstd = _rms(y_ref[...].astype(F32), g)
        e = x_ref[...] + out - t_ref[...]
        _accumulate(loss_ref, jnp.full(loss_ref.shape, 0.5 / d, F32) * jnp.sum(e * e))
        dx = e * (1.0 / d)
        dx_ref[...] = dx
        dy_ref[...] = _rms_bwd(n, rstd, g, dx).astype(BF16)
        _accumulate(dg_ref, jnp.sum(dx * n, axis=0, keepdims=True))

    return pl.pallas_call(
        body, name=name, grid=(t // NORM_ROWS,),
        in_specs=[_row_spec(d), _vec_spec(d), _row_spec(d), _row_spec(d)],
        out_specs=[pl.BlockSpec((8, 128), lambda i: (0, 0)), _row_spec(d), _row_spec(d), _vec_spec(d)],
        out_shape=[jax.ShapeDtypeStruct((8, 128), F32), jax.ShapeDtypeStruct((t, d), F32),
                   jax.ShapeDtypeStruct((t, d), BF16), jax.ShapeDtypeStruct((1, d), F32)],
        compiler_params=_params("arbitrary"),
    )(y, g_post, x, target)


def _pre_post_bwd(x, g_pre, dh, dx_in, y, g_post, name, comms=()):
    t, d = x.shape
    both = y is not None

    def body(x_ref, gp_ref, dh_ref, dxi_ref, *rest):
        if both:
            y_ref, gq_ref, dx_ref, dy_ref, dgp_ref, dgq_ref = rest
        else:
            dx_ref, dgp_ref = rest
        gp = gp_ref[...]
        _, n, rstd = _rms(x_ref[...], gp)
        dh_v = dh_ref[...].astype(F32)
        dx = dxi_ref[...] + _rms_bwd(n, rstd, gp, dh_v)
        dx_ref[...] = dx
        _accumulate(dgp_ref, jnp.sum(dh_v * n, axis=0, keepdims=True))
        if both:
            gq = gq_ref[...]
            _, ny, rstdy = _rms(y_ref[...].astype(F32), gq)
            dy_ref[...] = _rms_bwd(ny, rstdy, gq, dx).astype(BF16)
            _accumulate(dgq_ref, jnp.sum(dx * ny, axis=0, keepdims=True))

    in_specs = [_row_spec(d), _vec_spec(d), _row_spec(d), _row_spec(d)]
    args = [x, g_pre, dh, dx_in]
    if both:
        in_specs += [_row_spec(d), _vec_spec(d)]
        args += [y, g_post]
        out_specs = [_row_spec(d), _row_spec(d), _vec_spec(d), _vec_spec(d)]
        out_shape = [jax.ShapeDtypeStruct((t, d), F32), jax.ShapeDtypeStruct((t, d), BF16),
                     jax.ShapeDtypeStruct((1, d), F32), jax.ShapeDtypeStruct((1, d), F32)]
    else:
        out_specs = [_row_spec(d), _vec_spec(d)]
        out_shape = [jax.ShapeDtypeStruct((t, d), F32), jax.ShapeDtypeStruct((1, d), F32)]
    return _pcall(body, args, name=name, grid=(t // NORM_ROWS,), in_specs=in_specs, out_specs=out_specs, out_shape=out_shape,
                  sem=("arbitrary",), comms=comms)


def _gelu(x):
    return 0.5 * x * (1.0 + lax.erf(x * 0.7071067811865476))


def _gelu_grad(x):
    return 0.5 * (1.0 + lax.erf(x * 0.7071067811865476)) + x * jnp.exp(-0.5 * x * x) * 0.3989422804014327


def _layernorm(v, g, b):
    mu = jnp.mean(v, axis=-1, keepdims=True)
    vc = v - mu
    rs = lax.rsqrt(jnp.mean(vc * vc, axis=-1, keepdims=True) + LN_EPS)
    vhat = vc * rs
    return vhat * g + b, vhat, rs


def _tril_mask():
    return lax.broadcasted_iota(jnp.int32, (CHUNK, CHUNK), 0) >= lax.broadcasted_iota(jnp.int32, (CHUNK, CHUNK), 1)


def _sgu_fwd(z, ln_g, ln_b, w16, bias_b, name, comms=()):
    t = z.shape[0]
    groups = w16.shape[0]
    a = groups * CHUNK

    def body(u_ref, v_ref, g_ref, b_ref, w_ref, bb_ref, o_ref):
        u = _gelu(u_ref[...].astype(F32))
        vn = _layernorm(_gelu(v_ref[...].astype(F32)), g_ref[...], b_ref[...])[0].astype(BF16)
        tril = _tril_mask()
        for g in range(groups):
            sl = slice(g * CHUNK, (g + 1) * CHUNK)
            w = jnp.where(tril, w_ref[g], jnp.zeros((), BF16))
            mixed = _dot(w, vn[:, sl], NN) + bb_ref[g]
            o_ref[:, sl] = (u[:, sl] * mixed).astype(BF16)

    full3 = pl.BlockSpec((groups, CHUNK, CHUNK), lambda c: (0, 0, 0))
    res = _pcall(
        body, (z, z, ln_g, ln_b, w16, bias_b), name=name, grid=(t // CHUNK,),
        in_specs=[pl.BlockSpec((CHUNK, a), lambda c: (c, 0)), pl.BlockSpec((CHUNK, a), lambda c: (c, 1)),
                  _vec_spec(a), _vec_spec(a), full3, full3],
        out_specs=[pl.BlockSpec((CHUNK, a), lambda c: (c, 0))], out_shape=[jax.ShapeDtypeStruct((t, a), BF16)],
        sem=("parallel",), comms=comms)
    return (res[0][0], res[1]) if comms else res[0]


def _sgu_bwd(z, dab, ln_g, ln_b, w16, bias_b, name, comms=()):
    t = z.shape[0]
    groups = w16.shape[0]
    a = groups * CHUNK

    def body(u_ref, v_ref, da_ref, g_ref, b_ref, w_ref, bb_ref, duv_ref, dg_ref, db_ref, dw_ref, dbs_ref, dvn_ref):
        up = u_ref[...].astype(F32)
        vp = v_ref[...].astype(F32)
        u = _gelu(up)
        ln_gain = g_ref[...]
        vn32, vhat, rs = _layernorm(_gelu(vp), ln_gain, b_ref[...])
        vn = vn32.astype(BF16)
        da = da_ref[...].astype(F32)
        tril = _tril_mask()
        ones = jnp.ones((8, CHUNK), F32)

        @pl.when(pl.program_id(0) == 0)
        def _():
            dw_ref[...] = jnp.zeros_like(dw_ref)
            dbs_ref[...] = jnp.zeros_like(dbs_ref)

        for g in range(groups):
            sl = slice(g * CHUNK, (g + 1) * CHUNK)
            w = jnp.where(tril, w_ref[g], jnp.zeros((), BF16))
            mixed = _dot(w, vn[:, sl], NN) + bb_ref[g]
            dmix = da[:, sl] * u[:, sl]
            dmix16 = dmix.astype(BF16)
            duv_ref[:, sl] = (da[:, sl] * mixed * _gelu_grad(up[:, sl])).astype(BF16)
            dvn_ref[:, sl] = _dot(w, dmix16, TN)
            dw_ref[g] += jnp.where(tril, _dot(dmix16, vn[:, sl], NT), 0.0)
            dbs_ref[g:g + 1, :] += lax.dot_general(ones, dmix, (NT, ((), ())), precision=lax.Precision.HIGHEST,
                                                   preferred_element_type=F32)[0:1]
        dvn = dvn_ref[...]
        dvhat = dvn * ln_gain
        dva = rs * (dvhat - jnp.mean(dvhat, axis=-1, keepdims=True) - vhat * jnp.mean(dvhat * vhat, axis=-1, keepdims=True))
        duv_ref[:, a:] = (dva * _gelu_grad(vp)).astype(BF16)
        _accumulate(dg_ref, jnp.sum(dvn * vhat, axis=0, keepdims=True))
        _accumulate(db_ref, jnp.sum(dvn, axis=0, keepdims=True))

    full3 = pl.BlockSpec((groups, CHUNK, CHUNK), lambda c: (0, 0, 0))
    return _pcall(
        body, (z, z, dab, ln_g, ln_b, w16, bias_b), name=name, grid=(t // CHUNK,),
        in_specs=[pl.BlockSpec((CHUNK, a), lambda c: (c, 0)), pl.BlockSpec((CHUNK, a), lambda c: (c, 1)),
                  pl.BlockSpec((CHUNK, a), lambda c: (c, 0)), _vec_spec(a), _vec_spec(a), full3, full3],
        out_specs=[pl.BlockSpec((CHUNK, 2 * a), lambda c: (c, 0)), _vec_spec(a), _vec_spec(a), full3,
                   pl.BlockSpec((groups, CHUNK), lambda c: (0, 0))],
        out_shape=[jax.ShapeDtypeStruct((t, 2 * a), BF16), jax.ShapeDtypeStruct((1, a), F32), jax.ShapeDtypeStruct((1, a), F32),
                   jax.ShapeDtypeStruct((groups, CHUNK, CHUNK), F32), jax.ShapeDtypeStruct((groups, CHUNK), F32)],
        scratch=[pltpu.VMEM((CHUNK, a), F32)], sem=("arbitrary",), comms=comms)


def _dil_masks(d):
    qi = lax.broadcasted_iota(jnp.int32, (CHUNK, CHUNK), 0)
    kj = lax.broadcasted_iota(jnp.int32, (CHUNK, CHUNK), 1)
    dist_c = qi - kj
    return dist_c >= 0, dist_c <= 0, (dist_c * d).astype(F32), ((dist_c + CHUNK) * d).astype(F32)


def _alibi_slope(h, heads):
    return 2.0 ** (-8.0 * (h + 1) / heads)


def _dil_view(z, d):
    t, w = z.shape[0], z.shape[1] // 5
    if d == 1:
        return z, 5, 2
    return z[:, 2 * w:].reshape(t // d, d * 3 * w), 3, 0


def _dil_fwd(z, d, name, comms=()):
    t = z.shape[0]
    w = z.shape[1] // 5
    heads = w // HEAD_DIM
    nb = t // d // CHUNK
    scale = HEAD_DIM ** -0.5
    zv, mult, col_q = _dil_view(z, d)

    def body(q_ref, kp_ref, kc_ref, vp_ref, vc_ref, o_ref, l_ref):
        ok_c, ok_p0, bias_c, bias_p = _dil_masks(d)
        ok_p = ok_p0 & (pl.program_id(1) > 0)
        hs = range(heads)
        sl = [slice(h * HEAD_DIM, (h + 1) * HEAD_DIM) for h in hs]
        slope = [_alibi_slope(h, heads) for h in hs]
        ones = jnp.ones((CHUNK, HEAD_DIM), BF16)
        s_c = [_dot(q_ref[:, sl[h]], kc_ref[:, sl[h]], NT) for h in hs]
        s_p = [_dot(q_ref[:, sl[h]], kp_ref[:, sl[h]], NT) for h in hs]
        s_c = [jnp.where(ok_c, s_c[h] * scale - slope[h] * bias_c, NEG) for h in hs]
        s_p = [jnp.where(ok_p, s_p[h] * scale - slope[h] * bias_p, NEG) for h in hs]
        m = [jnp.max(jnp.maximum(s_c[h], s_p[h]), axis=1, keepdims=True) for h in hs]
        p_c = [jnp.exp(s_c[h] - m[h]).astype(BF16) for h in hs]
        p_p = [jnp.exp(s_p[h] - m[h]).astype(BF16) for h in hs]
        den = [_dot(p_c[h], ones, NN) + _dot(p_p[h], ones, NN) for h in hs]
        o = [_dot(p_c[h], vc_ref[:, sl[h]], NN) + _dot(p_p[h], vp_ref[:, sl[h]], NN) for h in hs]
        l_ref[...] = jnp.zeros_like(l_ref)
        for h in hs:
            o_ref[:, sl[h]] = (o[h] / den[h]).astype(BF16)
            l_ref[:, h:h + 1] = m[h] + jnp.log(den[h][:, 0:1])

    def zspec(col, prev):
        if prev:
            return pl.BlockSpec((CHUNK, w), lambda r, n: (jnp.maximum(n - 1, 0), r * mult + col_q + col))
        return pl.BlockSpec((CHUNK, w), lambda r, n: (n, r * mult + col_q + col))

    res = _pcall(
        body, (zv, zv, zv, zv, zv), name=name, grid=(d, nb),
        in_specs=[zspec(0, False), zspec(1, True), zspec(1, False), zspec(2, True), zspec(2, False)],
        out_specs=[pl.BlockSpec((CHUNK, w), lambda r, n: (n, r)), pl.BlockSpec((CHUNK, HEAD_DIM), lambda r, n: (n, r))],
        out_shape=[jax.ShapeDtypeStruct((t // d, d * w), BF16), jax.ShapeDtypeStruct((t // d, d * HEAD_DIM), F32)],
        sem=("parallel", "parallel"), comms=comms)
    (o, lse), rws = res if comms else (res, None)
    outs = (o.reshape(t, w), lse.reshape(t, HEAD_DIM))
    return (outs, rws) if comms else outs


def _dil_merge(a_out, outs, lses, name, comms=()):
    t, a = a_out.shape
    w = outs[0].shape[1]
    heads = w // HEAD_DIM
    nbr = len(outs)

    def body(a_ref, *rest):
        o_refs, l_refs, (ab_ref, lt_ref) = rest[:nbr], rest[nbr:2 * nbr], rest[2 * nbr:]
        ls = [r[...] for r in l_refs]
        m = functools.reduce(jnp.maximum, ls)
        ws = [jnp.exp(l - m) for l in ls]
        tot = functools.reduce(jnp.add, ws)
        ws = [wt / tot for wt in ws]
        ab_ref[:, :a] = a_ref[...]
        for h in range(heads):
            sl = slice(h * HEAD_DIM, (h + 1) * HEAD_DIM)
            mix = functools.reduce(jnp.add, [wt[:, h:h + 1] * r[:, sl].astype(F32) for wt, r in zip(ws, o_refs)])
            ab_ref[:, a + h * HEAD_DIM:a + (h + 1) * HEAD_DIM] = mix.astype(BF16)
        lt_ref[...] = m + jnp.log(tot)

    return _pcall(
        body, (a_out, *outs, *lses), name=name, grid=(t // NORM_ROWS,),
        in_specs=[_row_spec(a)] + [_row_spec(w)] * nbr + [_row_spec(HEAD_DIM)] * nbr,
        out_specs=[_row_spec(a + w), _row_spec(HEAD_DIM)],
        out_shape=[jax.ShapeDtypeStruct((t, a + w), BF16), jax.ShapeDtypeStruct((t, HEAD_DIM), F32)],
        sem=("parallel",), comms=comms)


def _dil_delta(ab, dab, name):
    t, aw = ab.shape
    w = aw // 2
    heads = w // HEAD_DIM

    def body(o_ref, do_ref, dl_ref):
        dl_ref[...] = jnp.zeros_like(dl_ref)
        for h in range(heads):
            sl = slice(h * HEAD_DIM, (h + 1) * HEAD_DIM)
            dl_ref[:, h:h + 1] = jnp.sum(do_ref[:, sl].astype(F32) * o_ref[:, sl].astype(F32), axis=1, keepdims=True)

    half = pl.BlockSpec((NORM_ROWS, w), lambda i: (i, 1))
    return pl.pallas_call(body, name=name, grid=(t // NORM_ROWS,), in_specs=[half, half], out_specs=_row_spec(HEAD_DIM),
                          out_shape=jax.ShapeDtypeStruct((t, HEAD_DIM), F32), compiler_params=_params("parallel"))(ab, dab)


def _dil_bwd(z, dab, ltot, delta, d, name, comms=()):
    t = z.shape[0]
    w = z.shape[1] // 5
    heads = w // HEAD_DIM
    nb = t // d // CHUNK
    scale = HEAD_DIM ** -0.5

    def body(q_ref, qn_ref, kp_ref, kc_ref, vp_ref, vc_ref, do_ref, don_ref, l_ref, ln_ref, dl_ref, dln_ref,
             dq_ref, dk_ref, dv_ref):
        n = pl.program_id(1)
        ok_c, ok_p0, bias_c, bias_p = _dil_masks(d)
        ok_p = ok_p0 & (n > 0)
        ok_n = ok_p0 & (n < nb - 1)
        hs = range(heads)
        sl = [slice(h * HEAD_DIM, (h + 1) * HEAD_DIM) for h in hs]
        slope = [_alibi_slope(h, heads) for h in hs]
        q, qn = [q_ref[:, s] for s in sl], [qn_ref[:, s] for s in sl]
        kp, kc = [kp_ref[:, s] for s in sl], [kc_ref[:, s] for s in sl]
        vp, vc = [vp_ref[:, s] for s in sl], [vc_ref[:, s] for s in sl]
        do, don = [do_ref[:, s] for s in sl], [don_ref[:, s] for s in sl]
        s_c = [_dot(q[h], kc[h], NT) for h in hs]
        s_p = [_dot(q[h], kp[h], NT) for h in hs]
        s_n = [_dot(qn[h], kc[h], NT) for h in hs]
        dp_c = [_dot(do[h], vc[h], NT) for h in hs]
        dp_p = [_dot(do[h], vp[h], NT) for h in hs]
        dp_n = [_dot(don[h], vc[h], NT) for h in hs]
        delta = [dl_ref[:, h:h + 1] for h in hs]
        delta_n = [dln_ref[:, h:h + 1] for h in hs]
        p_c = [jnp.exp(jnp.where(ok_c, s_c[h] * scale - slope[h] * bias_c, NEG) - l_ref[:, h:h + 1]) for h in hs]
        p_p = [jnp.exp(jnp.where(ok_p, s_p[h] * scale - slope[h] * bias_p, NEG) - l_ref[:, h:h + 1]) for h in hs]
        p_n = [jnp.exp(jnp.where(ok_n, s_n[h] * scale - slope[h] * bias_p, NEG) - ln_ref[:, h:h + 1]) for h in hs]
        ds_c = [(p_c[h] * (dp_c[h] - delta[h])).astype(BF16) for h in hs]
        ds_p = [(p_p[h] * (dp_p[h] - delta[h])).astype(BF16) for h in hs]
        ds_n = [(p_n[h] * (dp_n[h] - delta_n[h])).astype(BF16) for h in hs]
        dq = [_dot(ds_c[h], kc[h], NN) + _dot(ds_p[h], kp[h], NN) for h in hs]
        dk = [_dot(ds_c[h], q[h], TN) + _dot(ds_n[h], qn[h], TN) for h in hs]
        dv = [_dot(p_c[h].astype(BF16), do[h], TN) + _dot(p_n[h].astype(BF16), don[h], TN) for h in hs]
        for h in hs:
            dq_ref[:, sl[h]] = (dq[h] * scale).astype(BF16)
            dk_ref[:, sl[h]] = (dk[h] * scale).astype(BF16)
            dv_ref[:, sl[h]] = dv[h].astype(BF16)

    def spec(mult, col, shift, width=w):
        if shift < 0:
            return pl.BlockSpec((CHUNK, width), lambda r, n: (jnp.maximum(n - 1, 0), r * mult + col))
        if shift > 0:
            return pl.BlockSpec((CHUNK, width), lambda r, n: (jnp.minimum(n + 1, nb - 1), r * mult + col))
        return pl.BlockSpec((CHUNK, width), lambda r, n: (n, r * mult + col))

    zv, mult, cq = _dil_view(z, d)
    dov = dab[:, w:].reshape(t // d, d * w)
    lv = ltot.reshape(t // d, d * HEAD_DIM)
    dlv = delta.reshape(t // d, d * HEAD_DIM)
    ospec = spec(1, 0, 0)
    res = _pcall(
        body, (zv, zv, zv, zv, zv, zv, dov, dov, lv, lv, dlv, dlv), name=name, grid=(d, nb),
        in_specs=[spec(mult, cq, 0), spec(mult, cq, 1), spec(mult, cq + 1, -1), spec(mult, cq + 1, 0),
                  spec(mult, cq + 2, -1), spec(mult, cq + 2, 0), spec(1, 0, 0), spec(1, 0, 1),
                  spec(1, 0, 0, HEAD_DIM), spec(1, 0, 1, HEAD_DIM), spec(1, 0, 0, HEAD_DIM), spec(1, 0, 1, HEAD_DIM)],
        out_specs=[ospec, ospec, ospec], out_shape=[jax.ShapeDtypeStruct((t // d, d * w), BF16)] * 3,
        sem=("parallel", "parallel"), comms=comms)
    outs, rws = res if comms else (res, None)
    outs = [o.reshape(t, w) for o in outs]
    return (outs, rws) if comms else outs


def _dz_assemble(duv, parts, name):
    t, a2 = duv.shape
    w = parts[0][0].shape[1]
    nbr = len(parts)

    def body(duv_ref, *rest):
        refs, dz_ref = rest[:-1], rest[-1]
        dz_ref[:, :a2] = duv_ref[...]
        for i in range(3):
            tot = functools.reduce(jnp.add, [refs[b * 3 + i][...].astype(F32) for b in range(nbr)])
            dz_ref[:, a2 + i * w:a2 + (i + 1) * w] = tot.astype(BF16)

    flat = [p for branch in parts for p in branch]
    return pl.pallas_call(
        body, name=name, grid=(t // NORM_ROWS,), in_specs=[_row_spec(a2)] + [_row_spec(w)] * len(flat),
        out_specs=_row_spec(a2 + 3 * w), out_shape=jax.ShapeDtypeStruct((t, a2 + 3 * w), BF16),
        compiler_params=_params("parallel"),
    )(duv, *flat)


def _split_dot(x, m16):
    hi = x.astype(BF16)
    lo = (x - hi.astype(F32)).astype(BF16)
    return _dot(hi, m16, NN) + _dot(lo, m16, NN)


SB_DEAD = -110.0


def _sb_scaled(q):
    return (q.astype(F32) * (HEAD_DIM ** -0.5)).astype(BF16)


SB_GROUP = 4
SB_PAIR = 2


def _sb_logs(qs, kj, below):
    zt = [_dot(q, k, NT) for q, k in zip(qs, kj)]
    sp = [jnp.maximum(z, 0.0) + jnp.log(1.0 + jnp.exp(-jnp.abs(z))) for z in zt]
    return [z - s for z, s in zip(zt, sp)], [(-s if below is None else jnp.where(below, -s, 0.0)) for s in sp]


def _sb_alive(s, i, c_run):
    return (s <= i) & (jnp.max(c_run) > SB_DEAD)


def _sb_fwd(zc, name, comms=()):
    t = zc.shape[0]
    c = zc.shape[1] // 3
    heads = c // HEAD_DIM
    blk = min(SB_BLOCK, t)
    grp = SB_GROUP if heads % SB_GROUP == 0 else SB_PAIR

    def body(q_ref, k_ref, v_ref, o_ref, ct_ref, nb_ref):
        i = pl.program_id(1)
        sl = [slice(p * HEAD_DIM, (p + 1) * HEAD_DIM) for p in range(grp)]
        qs = [_sb_scaled(q_ref[:, s]) for s in sl]
        rows = lax.broadcasted_iota(jnp.int32, (blk, blk), 0)
        cols = lax.broadcasted_iota(jnp.int32, (blk, blk), 1)
        below = rows > cols
        m_right = below.astype(BF16)

        def tile(carry, diagonal):
            s, acc, c_run = carry[0], carry[1:1 + grp], carry[1 + grp:]
            off = pl.multiple_of((i - s) * blk, blk)
            log_beta, l = _sb_logs(qs, [k_ref[pl.ds(off, blk), p] for p in sl], below if diagonal else None)
            right = [_split_dot(x, m_right) for x in l]
            a = [jnp.exp(lb + (c + r)) for lb, c, r in zip(log_beta, c_run, right)]
            if diagonal:
                a = [jnp.where(below, x, 0.0) for x in a]
            acc = [o + _dot(x.astype(BF16), v_ref[pl.ds(off, blk), p], NN) for o, x, p in zip(acc, a, sl)]
            return (s + 1, *acc, *[c + jnp.sum(x, axis=1, keepdims=True) for c, x in zip(c_run, l)])

        zeros = [jnp.zeros((blk, HEAD_DIM), F32)] * grp + [jnp.zeros((blk, 1), F32)] * grp
        out = lax.while_loop(lambda carry: _sb_alive(carry[0], i, functools.reduce(jnp.maximum, carry[1 + grp:])),
                             lambda carry: tile(carry, False), tile((jnp.int32(0), *zeros), True))
        for p, s in enumerate(sl):
            o_ref[:, s] = out[1 + p].astype(BF16)
            ct_ref[:, s] = jnp.broadcast_to(out[1 + grp + p], (blk, HEAD_DIM))
        nb_ref[...] = jnp.zeros(nb_ref.shape, F32) + out[0].astype(F32)

    groups = heads // grp
    qspec = pl.BlockSpec((blk, grp * HEAD_DIM), lambda h, i: (i, h))
    return _pcall(body, (zc, zc, zc), name=name, grid=(groups, t // blk),
                  in_specs=[qspec, pl.BlockSpec((t, grp * HEAD_DIM), lambda h, i: (0, groups + h)),
                            pl.BlockSpec((t, grp * HEAD_DIM), lambda h, i: (0, 2 * groups + h))],
                  out_specs=[qspec, qspec, qspec],
                  out_shape=[jax.ShapeDtypeStruct((t, c), BF16), jax.ShapeDtypeStruct((t, c), F32), jax.ShapeDtypeStruct((t, c), F32)],
                  sem=("parallel", "parallel"), comms=comms)


def _sb_bwd(zc, ctot, swept, do, name, comms=()):
    t = zc.shape[0]
    c = zc.shape[1] // 3
    heads = c // HEAD_DIM
    blk = min(SB_BLOCK, t)
    scale = HEAD_DIM ** -0.5
    grp = SB_GROUP if heads % SB_GROUP == 0 else SB_PAIR

    def body(q_ref, k_ref, v_ref, ct_ref, nb_ref, do_ref, dq_ref, dk_ref, dv_ref):
        i = pl.program_id(1)

        @pl.when(i == 0)
        def _():
            dk_ref[...] = jnp.zeros_like(dk_ref)
            dv_ref[...] = jnp.zeros_like(dv_ref)

        ps = range(grp)
        sl = [slice(p * HEAD_DIM, (p + 1) * HEAD_DIM) for p in ps]
        qs = [_sb_scaled(q_ref[:, s]) for s in sl]
        dov = [do_ref[:, s] for s in sl]
        c_tot = [ct_ref[:, p * HEAD_DIM:p * HEAD_DIM + 1] for p in ps]
        n_blocks = jnp.clip(jnp.max(nb_ref[0:8, :]).astype(jnp.int32), 1, i + 1)
        rows = lax.broadcasted_iota(jnp.int32, (blk, blk), 0)
        cols = lax.broadcasted_iota(jnp.int32, (blk, blk), 1)
        below = rows > cols
        m_upto = (rows <= cols).astype(BF16)
        m_left = (rows < cols).astype(BF16)

        def tile(j, carry, diagonal):
            dq, l_run, w_run = carry[:grp], carry[grp:2 * grp], carry[2 * grp:]
            off = pl.multiple_of(j * blk, blk)
            kj = [k_ref[pl.ds(off, blk), s] for s in sl]
            vj = [v_ref[pl.ds(off, blk), s] for s in sl]
            log_beta, l = _sb_logs(qs, kj, below if diagonal else None)
            d_a = [_dot(dov[p], vj[p], NT) for p in ps]
            upto = [_split_dot(x, m_upto) for x in l]
            a = [jnp.exp(log_beta[p] + (c_tot[p] - l_run[p] - upto[p])) for p in ps]
            if diagonal:
                a = [jnp.where(below, x, 0.0) for x in a]
            wgt = [a[p] * d_a[p] for p in ps]
            before = [w_run[p] + _split_dot(wgt[p], m_left) for p in ps]
            dz = [wgt[p] * jnp.exp(l[p]) - jnp.exp(log_beta[p]) * before[p] for p in ps]
            if diagonal:
                dz = [jnp.where(below, x, 0.0) for x in dz]
            dz16 = [x.astype(BF16) for x in dz]
            dk = [_dot(dz16[p], qs[p], TN) for p in ps]
            dv = [_dot(a[p].astype(BF16), dov[p], TN) for p in ps]
            dq = [dq[p] + _dot(dz16[p], kj[p], NN) for p in ps]
            for p in ps:
                dk_ref[pl.ds(off, blk), sl[p]] += dk[p]
                dv_ref[pl.ds(off, blk), sl[p]] += dv[p]
            return (*dq, *[l_run[p] + jnp.sum(l[p], axis=1, keepdims=True) for p in ps],
                    *[w_run[p] + jnp.sum(wgt[p], axis=1, keepdims=True) for p in ps])

        zeros = [jnp.zeros((blk, HEAD_DIM), F32)] * grp + [jnp.zeros((blk, 1), F32)] * (2 * grp)
        carry = lax.fori_loop(i + 1 - n_blocks, i, lambda j, carry: tile(j, carry, False), tuple(zeros))
        out = tile(i, carry, True)
        for p in ps:
            dq_ref[:, sl[p]] = out[p] * scale

    groups = heads // grp
    qspec = pl.BlockSpec((blk, grp * HEAD_DIM), lambda h, i: (i, h))
    once = dict(pipeline_mode=pl.Buffered(1))
    full = pl.BlockSpec((t, grp * HEAD_DIM), lambda h, i: (0, h), **once)
    return _pcall(body, (zc, zc, zc, ctot, swept, do), name=name, grid=(groups, t // blk),
                  in_specs=[qspec, pl.BlockSpec((t, grp * HEAD_DIM), lambda h, i: (0, groups + h), **once),
                            pl.BlockSpec((t, grp * HEAD_DIM), lambda h, i: (0, 2 * groups + h), **once), qspec, qspec, qspec],
                  out_specs=[qspec, full, full], out_shape=[jax.ShapeDtypeStruct((t, c), F32)] * 3,
                  sem=("arbitrary", "arbitrary"), comms=comms)


def _concat_bf16(parts, name, comms=()):
    t, c = parts[0].shape

    def body(*refs):
        for k, r in enumerate(refs[:-1]):
            refs[-1][:, k * c:(k + 1) * c] = r[...].astype(BF16)

    res = _pcall(body, tuple(parts), name=name, grid=(t // NORM_ROWS,), in_specs=[_row_spec(c)] * len(parts),
                 out_specs=[_row_spec(c * len(parts))], out_shape=[jax.ShapeDtypeStruct((t, c * len(parts)), BF16)],
                 sem=("parallel",), comms=comms)
    return (res[0][0], res[1]) if comms else res[0]


KIND = {"ab_w_in": "col", "ab_w_out": "row", "sb_w_in": "col", "sb_w_out": "row",
        "ffn_w1_0": "col", "ffn_w1_1": "col", "ffn_w2_0": "row", "ffn_w2_1": "row"}
X_Y, DIAG, CHIPS = (2, 4), (6,), (2, 4, 6)


def _local_step(x, target, norms, sgu, big, bufs=None):
    g = {k: [v[l:l + 1] for l in range(2)] for k, v in norms.items()}
    ln_g, ln_b, sgu_w, sgu_b = sgu
    groups = sgu_w.shape[0]
    w16 = sgu_w.astype(BF16)
    bias_b = jnp.broadcast_to(sgu_b[:, :, None], (groups, CHUNK, CHUNK))
    big, dws, psum, dist = dict(big), {}, {}, bufs is not None
    pair, got = (dict(bufs[0]), dict(bufs[1])) if dist else ({}, {})

    def run(fn, *args, ops=(), **kw):
        if not dist or not ops:
            return fn(*args, **kw)
        make = {"gs": lambda k, p, *part: _GatherSend(big[k], KIND[k], p, *part), "gf": lambda k, p: _GatherFwd(big[k], KIND[k], p),
                "swap": lambda k, p: _PairSwap(dws[k], pair[k], KIND[k]),
                "chips": lambda k, p, *part: _ChipScatter(psum[k], got[k], p, *part)}
        out, rws = fn(*args, comms=[make[op[0]](*op[1:]) for op in ops], **kw)
        for (op, k, *_), r in zip(ops, rws):
            if op in ("gs", "gf"):
                big[k] = r[0]
            elif op == "swap":
                psum[k] = _pair_sum(dws[k], r[0], KIND[k], f"pair_sum_{k}")
            else:
                got[k] = r[0]
        return out

    h1_0 = _rms_fwd(x, g["pre_mix"][0], "rms_in")
    z0 = run(_matmul, h1_0, big["ab_w_in"], "nn", BF16, "ab_in", ops=[("gs", "ffn_w1_0", X_Y)])
    a_out = run(_sgu_fwd, z0, ln_g, ln_b, w16, bias_b, "sgu_fwd", ops=[("gf", "ffn_w1_0", X_Y), ("gs", "ab_w_out", CHIPS)])
    branches = [run(_dil_fwd, z0, 1, "dil_fwd_1", ops=[("gs", "ffn_w1_0", DIAG, (0, 2)), ("gf", "ab_w_out", CHIPS)]),
                run(_dil_fwd, z0, 4, "dil_fwd_4", ops=[("gs", "ffn_w1_0", DIAG, (1, 2))]),
                run(_dil_fwd, z0, 16, "dil_fwd_16", ops=[("gf", "ffn_w1_0", DIAG), ("gs", "ffn_w2_0", X_Y, (0, 2))])]
    ab, ltot = run(_dil_merge, a_out, [b[0] for b in branches], [b[1] for b in branches], "dil_merge",
                   ops=[("gs", "ffn_w2_0", X_Y, (1, 2))])
    y_0 = run(_matmul, ab, big["ab_w_out"], "nn", BF16, "ab_out", ops=[("gs", "ffn_w2_0", DIAG, (0, 2))])
    x1, h2_0 = run(_post_pre_fwd, y_0, g["post_mix"][0], x, g["pre_ffn"][0], "norm_mix0", ops=[("gs", "ffn_w2_0", DIAG, (1, 2))])
    r_0 = run(_matmul, h2_0, big["ffn_w1_0"], "nn", BF16, "ffn_up_0", relu_out=True,
              ops=[("gf", "ffn_w2_0", CHIPS), ("gs", "sb_w_in", CHIPS)])
    y2_0 = run(_matmul, r_0, big["ffn_w2_0"], "nn", BF16, "ffn_down_0", a_square=True,
               ops=[("gf", "sb_w_in", CHIPS), ("gs", "sb_w_out", CHIPS), ("gs", "ffn_w1_1", X_Y)])
    x2, h1_1 = run(_post_pre_fwd, y2_0, g["post_ffn"][0], x1, g["pre_mix"][1], "norm_ffn0",
                   ops=[("gf", "ffn_w1_1", X_Y), ("gf", "sb_w_out", CHIPS)])
    zc = run(_matmul, h1_1, big["sb_w_in"], "nn", BF16, "sb_in", ops=[("gs", "ffn_w1_1", DIAG)])
    o_sb, ct_sb, nb_sb = run(_sb_fwd, zc, "sb_fwd", ops=[("gf", "ffn_w1_1", DIAG), ("gs", "ffn_w2_1", CHIPS)])
    y_1 = run(_matmul, o_sb, big["sb_w_out"], "nn", BF16, "sb_out", ops=[("gf", "ffn_w2_1", CHIPS)])
    x3, h2_1 = _post_pre_fwd(y_1, g["post_mix"][1], x2, g["pre_ffn"][1], "norm_mix1")
    r_1 = _matmul(h2_1, big["ffn_w1_1"], "nn", BF16, "ffn_up_1", relu_out=True)
    y2_1 = _matmul(r_1, big["ffn_w2_1"], "nn", BF16, "ffn_down_1", a_square=True)
    loss, dx4, dy2_1, dg_post_ffn1 = _final_fwd_bwd(y2_1, g["post_ffn"][1], x3, target, "loss")

    da = _matmul(dy2_1, big["ffn_w2_1"], "nt", BF16, "ffn_da_1", mul2=r_1)
    dws["ffn_w2_1"] = _matmul(r_1, dy2_1, "tn", BF16, "ffn_dw2_1", a_square=True)
    dh2 = run(_matmul, da, big["ffn_w1_1"], "nt", BF16, "ffn_dh_1", ops=[("swap", "ffn_w2_1", None)])
    dws["ffn_w1_1"] = run(_matmul, h2_1, da, "tn", BF16, "ffn_dw1_1", ops=[("chips", "ffn_w2_1", X_Y)])
    dx3, dy_1, dg_pre_ffn1, dg_post_mix1 = run(_pre_post_bwd, x3, g["pre_ffn"][1], dh2, dx4, y_1, g["post_mix"][1], "norm_bwd_mix1",
                                               ops=[("swap", "ffn_w1_1", None)])
    do_sb = _matmul(dy_1, big["sb_w_out"], "nt", BF16, "sb_out_dx")
    dws["sb_w_out"] = _matmul(o_sb, dy_1, "tn", BF16, "sb_out_dw")
    dqkv = run(_sb_bwd, zc, ct_sb, nb_sb, do_sb, "sb_bwd",
               ops=[("chips", "ffn_w2_1", DIAG), ("chips", "ffn_w1_1", CHIPS), ("swap", "sb_w_out", None)])
    dzc = run(_concat_bf16, dqkv, "sb_dz", ops=[("chips", "sb_w_out", X_Y)])
    dh1 = run(_matmul, dzc, big["sb_w_in"], "nt", BF16, "sb_in_dx", ops=[("chips", "sb_w_out", DIAG)])
    dws["sb_w_in"] = _matmul(h1_1, dzc, "tn", BF16, "sb_in_dw")
    dx2, dy2_0, dg_pre_mix1, dg_post_ffn0 = run(_pre_post_bwd, x2, g["pre_mix"][1], dh1, dx3, y2_0, g["post_ffn"][0], "norm_bwd_ffn0",
                                                ops=[("swap", "sb_w_in", None)])
    da = run(_matmul, dy2_0, big["ffn_w2_0"], "nt", BF16, "ffn_da_0", mul2=r_0, ops=[("chips", "sb_w_in", X_Y)])
    dws["ffn_w2_0"] = run(_matmul, r_0, dy2_0, "tn", BF16, "ffn_dw2_0", a_square=True, ops=[("chips", "sb_w_in", DIAG)])
    dws["ffn_w1_0"] = run(_matmul, h2_0, da, "tn", BF16, "ffn_dw1_0", ops=[("swap", "ffn_w2_0", None)])
    dh2 = run(_matmul, da, big["ffn_w1_0"], "nt", BF16, "ffn_dh_0", ops=[("chips", "ffn_w2_0", X_Y), ("swap", "ffn_w1_0", None)])
    dx1, dy_0, dg_pre_ffn0, dg_post_mix0 = run(_pre_post_bwd, x1, g["pre_ffn"][0], dh2, dx2, y_0, g["post_mix"][0], "norm_bwd_mix0",
                                               ops=[("chips", "ffn_w2_0", DIAG, (0, 2))])
    dab = run(_matmul, dy_0, big["ab_w_out"], "nt", BF16, "ab_out_dx", ops=[("chips", "ffn_w2_0", DIAG, (1, 2))])
    dws["ab_w_out"] = run(_matmul, ab, dy_0, "tn", BF16, "ab_out_dw", ops=[("chips", "ffn_w1_0", X_Y, (0, 2))])
    duv, d_ln_g, d_ln_b, d_sgu_w, d_sgu_b = run(_sgu_bwd, z0, dab, ln_g, ln_b, w16, bias_b, "sgu_bwd",
                                                ops=[("chips", "ffn_w1_0", X_Y, (1, 2))])
    delta = _dil_delta(ab, dab, "dil_delta")
    parts = [run(_dil_bwd, z0, dab, ltot, delta, 1, "dil_bwd_1", ops=[("chips", "ffn_w1_0", DIAG, (0, 2)), ("swap", "ab_w_out", None)]),
             run(_dil_bwd, z0, dab, ltot, delta, 4, "dil_bwd_4", ops=[("chips", "ffn_w1_0", DIAG, (1, 2))]),
             run(_dil_bwd, z0, dab, ltot, delta, 16, "dil_bwd_16", ops=[("chips", "ab_w_out", CHIPS)])]
    dz0 = _dz_assemble(duv, parts, "dz_assemble")
    dws["ab_w_in"] = _matmul(h1_0, dz0, "tn", BF16, "ab_in_dw")
    dh1 = run(_matmul, dz0, big["ab_w_in"], "nt", BF16, "ab_in_dx", ops=[("swap", "ab_w_in", None)])
    grad_x, dg_pre_mix0 = run(_pre_post_bwd, x, g["pre_mix"][0], dh1, dx1, None, None, "norm_bwd_in", ops=[("chips", "ab_w_in", X_Y)])

    d_norms = {
        "pre_mix": jnp.concatenate([dg_pre_mix0, dg_pre_mix1]), "post_mix": jnp.concatenate([dg_post_mix0, dg_post_mix1]),
        "pre_ffn": jnp.concatenate([dg_pre_ffn0, dg_pre_ffn1]), "post_ffn": jnp.concatenate([dg_post_ffn0, dg_post_ffn1]),
    }
    return loss, grad_x, d_norms, (d_ln_g, d_ln_b, d_sgu_w, d_sgu_b), (psum, got) if dist else dws


def _to_bf16_full(w, layer, kind, name):
    _, rows, cols = w.shape
    tr = _tile(rows, 512)
    nblk = rows // tr
    full = (rows, 4 * cols) if kind == "col" else (4 * rows, cols)

    def body(w_ref, o_ref):
        o_ref[...] = w_ref[...].astype(BF16)

    def place(i):
        mine = 2 * lax.axis_index("x") + lax.axis_index("y")
        return (i, mine) if kind == "col" else (mine * nblk + i, 0)

    return pl.pallas_call(
        body, name=name, grid=(nblk,), in_specs=[pl.BlockSpec((None, tr, cols), lambda i: (layer, i, 0))],
        out_specs=pl.BlockSpec((tr, cols), place), out_shape=jax.ShapeDtypeStruct(full, BF16), compiler_params=_params("parallel"),
    )(w)


def _pair_sum(dw16, pair, kind, name):
    rh, cs = _half_shape(dw16.shape, kind)
    tr = _tile(rh, 256)
    nblk = rh // tr

    def body(dw_ref, pair_ref, o_ref):
        o_ref[...] = (dw_ref[...].astype(F32) + pair_ref[...].astype(F32)).astype(BF16)

    def own(s, i):
        c = lax.axis_index("c")
        return (c * nblk + i, s) if kind == "col" else ((2 * s + c) * nblk + i, 0)

    spec3 = pl.BlockSpec((None, tr, cs), lambda s, i: (s, i, 0))
    return pl.pallas_call(
        body, name=name, grid=(4, nblk), in_specs=[pl.BlockSpec((tr, cs), own), spec3], out_specs=spec3,
        out_shape=jax.ShapeDtypeStruct((4, rh, cs), BF16), compiler_params=_params("parallel", "parallel"),
    )(dw16, pair)


def _owner_sum(psum, got, buf, layer, name, comms=()):
    _, rh, cs = psum.shape
    tr = _tile(rh, 256)

    def body(p_ref, got_ref, buf_ref, o_ref):
        tot = p_ref[...].astype(F32)
        for j in range(3):
            tot = tot + got_ref[j].astype(F32)
        o_ref[...] = tot

    res = _pcall(
        body, (psum, got, buf), name=name, grid=(rh // tr,),
        in_specs=[pl.BlockSpec((None, tr, cs), lambda i: (2 * lax.axis_index("x") + lax.axis_index("y"), i, 0)),
                  pl.BlockSpec((3, tr, cs), lambda i: (0, i, 0)), ANY],
        out_specs=[pl.BlockSpec((None, None, tr, cs), lambda i: (layer, lax.axis_index("c"), i, 0))],
        out_shape=[jax.ShapeDtypeStruct(buf.shape, F32)], sem=("parallel",), comms=comms, aliases={2: 0})
    return (res[0][0], res[1]) if comms else res[0]


def _adamw_math(w, g, m, v):
    m = ADAM_B1 * m + (1.0 - ADAM_B1) * g
    v = ADAM_B2 * v + (1.0 - ADAM_B2) * (g * g)
    m_hat = m / (1.0 - ADAM_B1 ** ADAM_STEP)
    v_hat = v / (1.0 - ADAM_B2 ** ADAM_STEP)
    return -ADAM_LR * (m_hat / (jnp.sqrt(v_hat) + ADAM_EPS) + ADAM_WD * w), m, v


def _adamw(w, g, m, v, name):
    layers, rows, cols = w.shape
    tr = _tile(rows, 256)

    def body(w_ref, g_ref, m_ref, v_ref, go_ref, d_ref, mo_ref, vo_ref):
        g = g_ref[...]
        go_ref[...] = g
        d_ref[...], mo_ref[...], vo_ref[...] = _adamw_math(w_ref[...], g, m_ref[...], v_ref[...])

    spec = pl.BlockSpec((None, tr, cols), lambda l, i: (l, i, 0))
    return _pcall(body, (w, g, m, v), name=name, grid=(layers, rows // tr), in_specs=[spec] * 4, out_specs=[spec] * 4,
                  out_shape=[jax.ShapeDtypeStruct(w.shape, F32)] * 4, sem=("parallel", "parallel"))


def _pack(arrays):
    flat = jnp.concatenate([a.reshape(-1) for a in arrays])
    pad = (-flat.shape[0]) % 1024
    return jnp.pad(flat, (0, pad)).reshape(-1, 128)


def _unpack(packed, like):
    flat = packed.reshape(-1)
    out, off = [], 0
    for a in like:
        out.append(flat[off:off + a.size].reshape(a.shape))
        off += a.size
    return out


class _SmallGather:
    def __init__(self, g, parts, patterns):
        self.ro, self.rw, self.patterns, self.n_sems = [g], [parts], patterns, len(patterns)

    def start(self, ro, rw, send, recv):
        x, y, c, _ = _place()
        for k, j in enumerate(self.patterns):
            _remote(ro[0], rw[0].at[4 * x + 2 * y + c], send(k), recv(k), _flip(x, y, c, j)).start()

    def finish(self, ro, rw, send, recv):
        x, y, c, _ = _place()
        for k, j in enumerate(self.patterns):
            px, py, pc = _flip(x, y, c, j)
            slot = rw[0].at[4 * px + 2 * py + pc]
            cp = _remote(slot, slot, send(k), recv(k), (x, y, c))
            cp.wait_recv()
            cp.wait_send()


def _small_update(own, parts, w, m, v, name):
    rows = w.shape[0]

    def body(own_ref, p_ref, w_ref, m_ref, v_ref, g_ref, d_ref, mo_ref, vo_ref):
        me = 4 * lax.axis_index("x") + 2 * lax.axis_index("y") + lax.axis_index("c")
        g = jnp.where(me == 0, own_ref[...], p_ref[0])
        for k in range(1, 8):
            g = g + jnp.where(me == k, own_ref[...], p_ref[k])
        g_ref[...] = g
        d_ref[...], mo_ref[...], vo_ref[...] = _adamw_math(w_ref[...], g, m_ref[...], v_ref[...])

    return pl.pallas_call(body, name=name, out_shape=[jax.ShapeDtypeStruct((rows, 128), F32)] * 4,
                          compiler_params=_params())(own, parts, w, m, v)


SMALL = ("norm_pre_mix", "norm_post_mix", "norm_pre_ffn", "norm_post_ffn", "sgu_ln_g", "sgu_ln_b", "sgu_w", "sgu_b")
BIG = (("ab_w_in", ("ab_w_in",)), ("ab_w_out", ("ab_w_out",)), ("sb_w_in", ("sb_w_in",)), ("sb_w_out", ("sb_w_out",)),
       ("ffn_w1", ("ffn_w1_0", "ffn_w1_1")), ("ffn_w2", ("ffn_w2_0", "ffn_w2_1")))
WEIGHTS = ("norm_pre_mix", "norm_post_mix", "norm_pre_ffn", "norm_post_ffn", "ab_w_in", "sgu_ln_g", "sgu_ln_b", "sgu_w", "sgu_b",
           "ab_w_out", "sb_w_in", "sb_w_out", "ffn_w1", "ffn_w2")


def kernel(x, norm_pre_mix, norm_post_mix, norm_pre_ffn, norm_post_ffn, ab_w_in, sgu_ln_g, sgu_ln_b, sgu_w, sgu_b, ab_w_out, sb_w_in, sb_w_out, ffn_w1, ffn_w2, loss_target, m_norm_pre_mix, m_norm_post_mix, m_norm_pre_ffn, m_norm_post_ffn, m_ab_w_in, m_sgu_ln_g, m_sgu_ln_b, m_sgu_w, m_sgu_b, m_ab_w_out, m_sb_w_in, m_sb_w_out, m_ffn_w1, m_ffn_w2, v_norm_pre_mix, v_norm_post_mix, v_norm_pre_ffn, v_norm_post_ffn, v_ab_w_in, v_sgu_ln_g, v_sgu_ln_b, v_sgu_w, v_sgu_b, v_ab_w_out, v_sb_w_in, v_sb_w_out, v_ffn_w1, v_ffn_w2):
    w = dict(norm_pre_mix=norm_pre_mix, norm_post_mix=norm_post_mix, norm_pre_ffn=norm_pre_ffn, norm_post_ffn=norm_post_ffn,
             ab_w_in=ab_w_in, sgu_ln_g=sgu_ln_g, sgu_ln_b=sgu_ln_b, sgu_w=sgu_w, sgu_b=sgu_b, ab_w_out=ab_w_out, sb_w_in=sb_w_in,
             sb_w_out=sb_w_out, ffn_w1=ffn_w1, ffn_w2=ffn_w2)
    m = dict(norm_pre_mix=m_norm_pre_mix, norm_post_mix=m_norm_post_mix, norm_pre_ffn=m_norm_pre_ffn, norm_post_ffn=m_norm_post_ffn,
             ab_w_in=m_ab_w_in, sgu_ln_g=m_sgu_ln_g, sgu_ln_b=m_sgu_ln_b, sgu_w=m_sgu_w, sgu_b=m_sgu_b, ab_w_out=m_ab_w_out,
             sb_w_in=m_sb_w_in, sb_w_out=m_sb_w_out, ffn_w1=m_ffn_w1, ffn_w2=m_ffn_w2)
    v = dict(norm_pre_mix=v_norm_pre_mix, norm_post_mix=v_norm_post_mix, norm_pre_ffn=v_norm_pre_ffn, norm_post_ffn=v_norm_post_ffn,
             ab_w_in=v_ab_w_in, sgu_ln_g=v_sgu_ln_g, sgu_ln_b=v_sgu_ln_b, sgu_w=v_sgu_w, sgu_b=v_sgu_b, ab_w_out=v_ab_w_out,
             sb_w_in=v_sb_w_in, sb_w_out=v_sb_w_out, ffn_w1=v_ffn_w1, ffn_w2=v_ffn_w2)
    big, pair, got = {}, {}, {}
    for name, keys in BIG:
        for layer, key in enumerate(keys):
            big[key] = _to_bf16_full(w[name], layer, KIND[key], f"bf16_{key}")
            half = _half_shape(big[key].shape, KIND[key])
            pair[key], got[key] = lax.empty((4,) + half, BF16), lax.empty((3,) + half, BF16)
    big["ab_w_in"] = _comm_call([_Gather(big["ab_w_in"], KIND["ab_w_in"])], "gather_first")[0][0]

    norms = {k: w["norm_" + k] for k in ("pre_mix", "post_mix", "pre_ffn", "post_ffn")}
    sgu = (sgu_ln_g, sgu_ln_b, sgu_w[0], sgu_b[0])
    loss_blk, grad_x, d_norms, d_sgu, (psum, got) = _local_step(x[0], loss_target[0], norms, sgu, big, (pair, got))
    loss = lax.psum(loss_blk[0, 0], ("x", "y", "c"))

    grads, deltas, new_m, new_v = {}, {}, {}, {}
    keys_of = dict(BIG)
    small_g = _pack([d_norms["pre_mix"], d_norms["post_mix"], d_norms["pre_ffn"], d_norms["post_ffn"],
                     d_sgu[0], d_sgu[1], d_sgu[2][None], d_sgu[3][None]])
    parts = lax.empty((8,) + small_g.shape, F32)
    small_todo = [(1, 2, 4, 6), (3, 5, 7)]
    bufs, pending = {}, None
    for name in ("ffn_w2", "ffn_w1", "sb_w_in", "sb_w_out", "ab_w_out"):
        buf = lax.empty((len(keys_of[name]), 2) + psum[keys_of[name][0]].shape[1:], F32)
        for layer, key in enumerate(keys_of[name]):
            if pending is not None:
                buf, rws = _owner_sum(psum[key], got[key], buf, layer, f"sum_{key}", comms=[_Join([bufs[pending]])])
                bufs[pending], pending = rws[0][0], None
            elif small_todo:
                buf, rws = _owner_sum(psum[key], got[key], buf, layer, f"sum_{key}",
                                      comms=[_SmallGather(small_g, parts, small_todo.pop(0))])
                parts = rws[0][0]
            else:
                buf = _owner_sum(psum[key], got[key], buf, layer, f"sum_{key}")
        bufs[name], pending = buf, name
    assert not small_todo

    rws = _comm_call([_Join([bufs["ab_w_out"]]), _ChipScatter(psum["ab_w_in"], got["ab_w_in"], DIAG)], "tail_comm")
    bufs["ab_w_out"], got["ab_w_in"] = rws[0][0], rws[1][0]
    bufs["ab_w_in"] = _owner_sum(psum["ab_w_in"], got["ab_w_in"], lax.empty((1, 2) + psum["ab_w_in"].shape[1:], F32), 0, "sum_ab_w_in")
    bufs["ab_w_in"] = _comm_call([_Join([bufs["ab_w_in"]])], "join_last")[0][0]
    for name, _ in BIG:
        grads[name], deltas[name], new_m[name], new_v[name] = _adamw(w[name], bufs[name].reshape(w[name].shape), m[name], v[name], f"adamw_{name}")

    outs = _small_update(small_g, parts, _pack([w[k] for k in SMALL]), _pack([m[k] for k in SMALL]), _pack([v[k] for k in SMALL]), "small_update")
    like = [w[k] for k in SMALL]
    for dst, packed in zip((grads, deltas, new_m, new_v), outs):
        for k, a in zip(SMALL, _unpack(packed, like)):
            dst[k] = a

    return (loss, grad_x[None], *[grads[k] for k in WEIGHTS], *[deltas[k] for k in WEIGHTS],
            *[new_m[k] for k in WEIGHTS], *[new_v[k] for k in WEIGHTS])
```

```python
import functools

import jax
import jax.numpy as jnp
from jax import lax
from jax.experimental import pallas as pl
from jax.experimental.pallas import tpu as pltpu

F32 = jnp.float32
BF16 = jnp.bfloat16
MESH = pl.DeviceIdType.MESH

HEAD_DIM = 128
CHUNK = 128
DILATIONS = (1, 4, 16)
SB_BLOCK = 256
RMS_EPS = 1e-6
LN_EPS = 1e-5
ADAM_LR, ADAM_B1, ADAM_B2, ADAM_EPS, ADAM_WD, ADAM_STEP = 0.001, 0.9, 0.999, 1e-08, 0.01, 10
NEG = -1e30
V7X_VMEM_LIMIT = 48 * 1024 * 1024
ANY = pl.BlockSpec(memory_space=pl.ANY)


def _params(*sem):
    return pltpu.CompilerParams(dimension_semantics=sem if sem else None, vmem_limit_bytes=V7X_VMEM_LIMIT)


def _tile(n, pref):
    if n <= pref:
        return n
    t = pref
    while n % t:
        t -= 128
    return t


def _dot(a, b, dims):
    return lax.dot_general(a, b, (dims, ((), ())), preferred_element_type=F32)


NN = ((1,), (0,))
NT = ((1,), (1,))
TN = ((0,), (0,))


def _place():
    x, y, c = lax.axis_index("x"), lax.axis_index("y"), lax.axis_index("c")
    return x, y, c, 2 * x + y


def _flip(x, y, c, j):
    return (1 - x if j & 4 else x), (1 - y if j & 2 else y), (1 - c if j & 1 else c)


def _half_shape(full_shape, kind):
    rows, cols = full_shape
    return (rows // 2, cols // 4) if kind == "col" else (rows // 8, cols)


def _half(ref, kind, s, h):
    rh, cs = _half_shape(ref.shape, kind)
    if kind == "col":
        return ref.at[pl.ds(h * rh, rh), pl.ds(s * cs, cs)]
    return ref.at[pl.ds((2 * s + h) * rh, rh), :]


def _remote(src, dst, send, recv, to):
    return pltpu.make_async_remote_copy(src_ref=src, dst_ref=dst, send_sem=send, recv_sem=recv, device_id=to, device_id_type=MESH)


class _Gather:
    n_sems = 6

    def __init__(self, full, kind):
        self.ro, self.rw, self.kind = [], [full], kind

    def start(self, ro, rw, send, recv):
        x, y, c, mine = _place()
        own = _half(rw[0], self.kind, mine, c)
        for k, j in enumerate((2, 4, 6)):
            px, py, _ = _flip(x, y, c, j)
            _remote(own, own, send(k), recv(k), (px, py, c)).start()

    def finish(self, ro, rw, send, recv):
        x, y, c, mine = _place()
        own = _half(rw[0], self.kind, mine, c)
        for k, j in enumerate((2, 4, 6)):
            px, py, _ = _flip(x, y, c, j)
            got = _half(rw[0], self.kind, 2 * px + py, c)
            _remote(got, got, send(k), recv(k), (x, y, c)).wait_recv()
            _remote(got, got, send(3 + k), recv(3 + k), (x, y, 1 - c)).start()
        for k, j in enumerate((2, 4, 6)):
            px, py, _ = _flip(x, y, c, j)
            got = _half(rw[0], self.kind, 2 * px + py, 1 - c)
            _remote(got, got, send(3 + k), recv(3 + k), (x, y, c)).wait_recv()
        for k in range(6):
            _remote(own, own, send(k), recv(k), (x, y, c)).wait_send()


class _GatherSend:
    def __init__(self, full, kind, patterns, part=(0, 1)):
        self.ro, self.rw, self.kind, self.patterns, self.part, self.n_sems = [], [full], kind, patterns, part, len(patterns)

    def _rows(self, half):
        i, n = self.part
        rows = half.shape[0] // n
        return half.at[pl.ds(i * rows, rows), :]

    def start(self, ro, rw, send, recv):
        x, y, c, mine = _place()
        own = self._rows(_half(rw[0], self.kind, mine, c))
        for k, j in enumerate(self.patterns):
            px, py, _ = _flip(x, y, c, j)
            _remote(own, own, send(k), recv(k), (px, py, c)).start()

    def finish(self, ro, rw, send, recv):
        x, y, c, _ = _place()
        for k, j in enumerate(self.patterns):
            px, py, _ = _flip(x, y, c, j)
            got = self._rows(_half(rw[0], self.kind, 2 * px + py, c))
            cp = _remote(got, got, send(k), recv(k), (x, y, c))
            cp.wait_recv()
            cp.wait_send()


class _GatherFwd:
    def __init__(self, full, kind, patterns):
        self.ro, self.rw, self.kind, self.patterns, self.n_sems = [], [full], kind, patterns, len(patterns)

    def start(self, ro, rw, send, recv):
        x, y, c, _ = _place()
        for k, j in enumerate(self.patterns):
            px, py, _ = _flip(x, y, c, j)
            got = _half(rw[0], self.kind, 2 * px + py, c)
            _remote(got, got, send(k), recv(k), (x, y, 1 - c)).start()

    def finish(self, ro, rw, send, recv):
        x, y, c, _ = _place()
        for k, j in enumerate(self.patterns):
            px, py, _ = _flip(x, y, c, j)
            got = _half(rw[0], self.kind, 2 * px + py, 1 - c)
            cp = _remote(got, got, send(k), recv(k), (x, y, c))
            cp.wait_recv()
            cp.wait_send()


class _PairSwap:
    n_sems = 4

    def __init__(self, dw16, pair, kind):
        self.ro, self.rw, self.kind = [dw16], [pair], kind

    def start(self, ro, rw, send, recv):
        x, y, c, _ = _place()
        for s in range(4):
            _remote(_half(ro[0], self.kind, s, 1 - c), rw[0].at[s], send(s), recv(s), (x, y, 1 - c)).start()

    def finish(self, ro, rw, send, recv):
        x, y, c, _ = _place()
        for s in range(4):
            cp = _remote(rw[0].at[s], rw[0].at[s], send(s), recv(s), (x, y, c))
            cp.wait_recv()
            cp.wait_send()


class _ChipScatter:
    def __init__(self, psum, got, patterns, part=(0, 1)):
        self.ro, self.rw, self.patterns, self.part, self.n_sems = [psum], [got], patterns, part, len(patterns)

    def _rows(self, ref, slot):
        i, n = self.part
        rows = ref.shape[1] // n
        return ref.at[slot, pl.ds(i * rows, rows), :]

    def start(self, ro, rw, send, recv):
        x, y, c, _ = _place()
        for k, j in enumerate(self.patterns):
            px, py, _ = _flip(x, y, c, j)
            _remote(self._rows(ro[0], 2 * px + py), self._rows(rw[0], j // 2 - 1), send(k), recv(k), (px, py, c)).start()

    def finish(self, ro, rw, send, recv):
        x, y, c, _ = _place()
        for k, j in enumerate(self.patterns):
            slot = self._rows(rw[0], j // 2 - 1)
            cp = _remote(slot, slot, send(k), recv(k), (x, y, c))
            cp.wait_recv()
            cp.wait_send()


class _Join:
    def __init__(self, bufs):
        self.ro, self.rw, self.n_sems = [], list(bufs), sum(b.shape[0] for b in bufs)

    def _copies(self, rw, send, recv, slot):
        x, y, c, _ = _place()
        k = 0
        for ref in rw:
            for l in range(ref.shape[0]):
                yield _remote(ref.at[l, c], ref.at[l, slot(c)], send(k), recv(k), (x, y, 1 - c))
                k += 1

    def start(self, ro, rw, send, recv):
        for cp in self._copies(rw, send, recv, lambda c: c):
            cp.start()

    def finish(self, ro, rw, send, recv):
        for cp in self._copies(rw, send, recv, lambda c: 1 - c):
            cp.wait_recv()
        for cp in self._copies(rw, send, recv, lambda c: c):
            cp.wait_send()


def _comm_layout(comms):
    ro = [a for c in comms for a in c.ro]
    rw = [a for c in comms for a in c.rw]
    return ro, rw, sum(c.n_sems for c in comms)


def _comm_each(comms, method, ro_refs, rw_refs, send, recv):
    i_ro = i_rw = i_sem = 0
    for c in comms:
        getattr(c, method)(ro_refs[i_ro:i_ro + len(c.ro)], rw_refs[i_rw:i_rw + len(c.rw)],
                           lambda k, b=i_sem: send.at[b + k], lambda k, b=i_sem: recv.at[b + k])
        i_ro, i_rw, i_sem = i_ro + len(c.ro), i_rw + len(c.rw), i_sem + c.n_sems


def _split_results(comms, rws):
    out, i = [], 0
    for c in comms:
        out.append(list(rws[i:i + len(c.rw)]))
        i += len(c.rw)
    return out


def _comm_call(comms, name):
    ro, rw, n_sems = _comm_layout(comms)

    def body(*refs):
        ro_refs = refs[:len(ro)]
        rw_refs = refs[len(ro) + len(rw):len(ro) + 2 * len(rw)]
        send, recv = refs[len(ro) + 2 * len(rw):]
        _comm_each(comms, "start", ro_refs, rw_refs, send, recv)
        _comm_each(comms, "finish", ro_refs, rw_refs, send, recv)

    rws = pl.pallas_call(
        body, name=name, in_specs=[ANY] * (len(ro) + len(rw)), out_specs=[ANY] * len(rw),
        out_shape=[jax.ShapeDtypeStruct(a.shape, a.dtype) for a in rw],
        input_output_aliases={len(ro) + k: k for k in range(len(rw))},
        scratch_shapes=[pltpu.SemaphoreType.DMA((n_sems,)), pltpu.SemaphoreType.DMA((n_sems,))],
    )(*ro, *rw)
    return _split_results(comms, rws)


def _pcall(body, args, *, name, grid, in_specs, out_specs, out_shape, scratch=(), sem=(), comms=(), aliases=None):
    n_in, n_out, n_scr = len(in_specs), len(out_specs), len(scratch)
    aliases = dict(aliases or {})
    if not comms:
        return pl.pallas_call(body, name=name, grid=grid, in_specs=list(in_specs), out_specs=list(out_specs),
                              out_shape=list(out_shape), scratch_shapes=list(scratch), input_output_aliases=aliases,
                              compiler_params=_params(*sem))(*args)
    ro, rw, n_sems = _comm_layout(comms)

    def carrier(*refs):
        ins = refs[:n_in]
        ro_refs = refs[n_in:n_in + len(ro)]
        o0 = n_in + len(ro) + len(rw)
        outs = refs[o0:o0 + n_out]
        rw_refs = refs[o0 + n_out:o0 + n_out + len(rw)]
        s0 = o0 + n_out + len(rw)
        send, recv = refs[s0 + n_scr], refs[s0 + n_scr + 1]
        ids = [pl.program_id(a) for a in range(len(grid))]
        first = functools.reduce(jnp.logical_and, [i == 0 for i in ids])
        last = functools.reduce(jnp.logical_and, [i == g - 1 for i, g in zip(ids, grid)])

        @pl.when(first)
        def _():
            _comm_each(comms, "start", ro_refs, rw_refs, send, recv)

        body(*ins, *outs, *refs[s0:s0 + n_scr])

        @pl.when(last)
        def _():
            _comm_each(comms, "finish", ro_refs, rw_refs, send, recv)

    res = pl.pallas_call(
        carrier, name=name, grid=grid, in_specs=list(in_specs) + [ANY] * (len(ro) + len(rw)),
        out_specs=list(out_specs) + [ANY] * len(rw),
        out_shape=list(out_shape) + [jax.ShapeDtypeStruct(a.shape, a.dtype) for a in rw],
        input_output_aliases={**aliases, **{n_in + len(ro) + k: n_out + k for k in range(len(rw))}},
        scratch_shapes=list(scratch) + [pltpu.SemaphoreType.DMA((n_sems,)), pltpu.SemaphoreType.DMA((n_sems,))],
        compiler_params=_params(*["arbitrary"] * len(grid)),
    )(*args, *ro, *rw)
    return list(res[:n_out]), _split_results(comms, res[n_out:])


def _matmul(a, b, mode, out_dtype, name, a_square=False, relu_out=False, mul2=None, comms=()):
    if mode == "nn":
        (m, k), n = a.shape, b.shape[1]
    elif mode == "nt":
        (m, k), n = a.shape, b.shape[0]
    else:
        (k, m), n = a.shape, b.shape[1]
    tm, tn, tk = _tile(m, 1024), _tile(n, 2048 if out_dtype == BF16 else 1024), _tile(k, 2048)
    nk = k // tk
    dims = {"nn": NN, "nt": NT, "tn": TN}[mode]
    a_spec = pl.BlockSpec((tk, tm), lambda i, j, kk: (kk, i)) if mode == "tn" else pl.BlockSpec((tm, tk), lambda i, j, kk: (i, kk))
    b_spec = pl.BlockSpec((tn, tk), lambda i, j, kk: (j, kk)) if mode == "nt" else pl.BlockSpec((tk, tn), lambda i, j, kk: (kk, j))
    o_spec = pl.BlockSpec((tm, tn), lambda i, j, kk: (i, j))

    def body(a_ref, b_ref, *rest):
        m_ref = None if mul2 is None else rest[0]
        o_ref = rest[0 if mul2 is None else 1]
        kk = pl.program_id(2)

        def partial():
            av = a_ref[...]
            if a_square:
                av = av * av
            return _dot(av, b_ref[...], dims)

        def finish(r):
            if relu_out:
                r = jnp.maximum(r, 0.0)
            if mul2 is not None:
                r = r * (2.0 * m_ref[...].astype(F32))
            o_ref[...] = r.astype(out_dtype)

        if nk == 1:
            finish(partial())
            return
        acc_ref = rest[-1]

        @pl.when(kk == 0)
        def _():
            acc_ref[...] = partial()

        @pl.when(kk > 0)
        def _():
            acc_ref[...] += partial()

        @pl.when(kk == nk - 1)
        def _():
            finish(acc_ref[...])

    args = (a, b) if mul2 is None else (a, b, mul2)
    specs = [a_spec, b_spec] + ([] if mul2 is None else [o_spec])
    res = _pcall(body, args, name=name, grid=(m // tm, n // tn, nk), in_specs=specs, out_specs=[o_spec],
                 out_shape=[jax.ShapeDtypeStruct((m, n), out_dtype)], scratch=[pltpu.VMEM((tm, tn), F32)] if nk > 1 else [],
                 sem=("parallel", "parallel", "arbitrary"), comms=comms)
    return (res[0][0], res[1]) if comms else res[0]


NORM_ROWS = 256


def _rms(x, g):
    rstd = lax.rsqrt(jnp.mean(x * x, axis=-1, keepdims=True) + RMS_EPS)
    n = x * rstd
    return n * g, n, rstd


def _rms_bwd(n, rstd, g, dout):
    dn = dout * g
    return rstd * (dn - n * jnp.mean(dn * n, axis=-1, keepdims=True))


def _row_spec(d):
    return pl.BlockSpec((NORM_ROWS, d), lambda i: (i, 0))


def _vec_spec(d):
    return pl.BlockSpec((1, d), lambda i: (0, 0))


def _accumulate(ref, val):
    @pl.when(pl.program_id(0) == 0)
    def _():
        ref[...] = jnp.zeros_like(ref)

    ref[...] += val


def _rms_fwd(x, g, name):
    t, d = x.shape

    def body(x_ref, g_ref, h_ref):
        h_ref[...] = _rms(x_ref[...], g_ref[...])[0].astype(BF16)

    return pl.pallas_call(
        body, name=name, grid=(t // NORM_ROWS,), in_specs=[_row_spec(d), _vec_spec(d)], out_specs=_row_spec(d),
        out_shape=jax.ShapeDtypeStruct((t, d), BF16), compiler_params=_params("parallel"),
    )(x, g)


def _post_pre_fwd(y, g_post, x, g_pre, name, comms=()):
    t, d = x.shape

    def body(y_ref, gp_ref, x_ref, gn_ref, xn_ref, h_ref):
        xn = x_ref[...] + _rms(y_ref[...].astype(F32), gp_ref[...])[0]
        xn_ref[...] = xn
        h_ref[...] = _rms(xn, gn_ref[...])[0].astype(BF16)

    return _pcall(
        body, (y, g_post, x, g_pre), name=name, grid=(t // NORM_ROWS,),
        in_specs=[_row_spec(d), _vec_spec(d), _row_spec(d), _vec_spec(d)], out_specs=[_row_spec(d), _row_spec(d)],
        out_shape=[jax.ShapeDtypeStruct((t, d), F32), jax.ShapeDtypeStruct((t, d), BF16)], sem=("parallel",), comms=comms)


def _final_fwd_bwd(y, g_post, x, target, name):
    t, d = x.shape

    def body(y_ref, g_ref, x_ref, t_ref, loss_ref, dx_ref, dy_ref, dg_ref):
        g = g_ref[...]
        out, n, rstd = _rms(y_ref[...].astype(F32), g)
        e = x_ref[...] + out - t_ref[...]
        _accumulate(loss_ref, jnp.full(loss_ref.shape, 0.5 / d, F32) * jnp.sum(e * e))
        dx = e * (1.0 / d)
        dx_ref[...] = dx
        dy_ref[...] = _rms_bwd(n, rstd, g, dx).astype(BF16)
        _accumulate(dg_ref, jnp.sum(dx * n, axis=0, keepdims=True))

    return pl.pallas_call(
        body, name=name, grid=(t // NORM_ROWS,),
        in_specs=[_row_spec(d), _vec_spec(d), _row_spec(d), _row_spec(d)],
        out_specs=[pl.BlockSpec((8, 128), lambda i: (0, 0)), _row_spec(d), _row_spec(d), _vec_spec(d)],
        out_shape=[jax.ShapeDtypeStruct((8, 128), F32), jax.ShapeDtypeStruct((t, d), F32),
                   jax.ShapeDtypeStruct((t, d), BF16), jax.ShapeDtypeStruct((1, d), F32)],
        compiler_params=_params("arbitrary"),
    )(y, g_post, x, target)


def _pre_post_bwd(x, g_pre, dh, dx_in, y, g_post, name, comms=()):
    t, d = x.shape
    both = y is not None

    def body(x_ref, gp_ref, dh_ref, dxi_ref, *rest):
        if both:
            y_ref, gq_ref, dx_ref, dy_ref, dgp_ref, dgq_ref = rest
        else:
            dx_ref, dgp_ref = rest
        gp = gp_ref[...]
        _, n, rstd = _rms(x_ref[...], gp)
        dh_v = dh_ref[...].astype(F32)
        dx = dxi_ref[...] + _rms_bwd(n, rstd, gp, dh_v)
        dx_ref[...] = dx
        _accumulate(dgp_ref, jnp.sum(dh_v * n, axis=0, keepdims=True))
        if both:
            gq = gq_ref[...]
            _, ny, rstdy = _rms(y_ref[...].astype(F32), gq)
            dy_ref[...] = _rms_bwd(ny, rstdy, gq, dx).astype(BF16)
            _accumulate(dgq_ref, jnp.sum(dx * ny, axis=0, keepdims=True))

    in_specs = [_row_spec(d), _vec_spec(d), _row_spec(d), _row_spec(d)]
    args = [x, g_pre, dh, dx_in]
    if both:
        in_specs += [_row_spec(d), _vec_spec(d)]
        args += [y, g_post]
        out_specs = [_row_spec(d), _row_spec(d), _vec_spec(d), _vec_spec(d)]
        out_shape = [jax.ShapeDtypeStruct((t, d), F32), jax.ShapeDtypeStruct((t, d), BF16),
                     jax.ShapeDtypeStruct((1, d), F32), jax.ShapeDtypeStruct((1, d), F32)]
    else:
        out_specs = [_row_spec(d), _vec_spec(d)]
        out_shape = [jax.ShapeDtypeStruct((t, d), F32), jax.ShapeDtypeStruct((1, d), F32)]
    return _pcall(body, args, name=name, grid=(t // NORM_ROWS,), in_specs=in_specs, out_specs=out_specs, out_shape=out_shape,
                  sem=("arbitrary",), comms=comms)


def _gelu(x):
    return 0.5 * x * (1.0 + lax.erf(x * 0.7071067811865476))


def _gelu_grad(x):
    return 0.5 * (1.0 + lax.erf(x * 0.7071067811865476)) + x * jnp.exp(-0.5 * x * x) * 0.3989422804014327


def _layernorm(v, g, b):
    mu = jnp.mean(v, axis=-1, keepdims=True)
    vc = v - mu
    rs = lax.rsqrt(jnp.mean(vc * vc, axis=-1, keepdims=True) + LN_EPS)
    vhat = vc * rs
    return vhat * g + b, vhat, rs


def _tril_mask():
    return lax.broadcasted_iota(jnp.int32, (CHUNK, CHUNK), 0) >= lax.broadcasted_iota(jnp.int32, (CHUNK, CHUNK), 1)


def _sgu_fwd(z, ln_g, ln_b, w16, bias_b, name, comms=()):
    t = z.shape[0]
    groups = w16.shape[0]
    a = groups * CHUNK

    def body(u_ref, v_ref, g_ref, b_ref, w_ref, bb_ref, o_ref):
        u = _gelu(u_ref[...].astype(F32))
        vn = _layernorm(_gelu(v_ref[...].astype(F32)), g_ref[...], b_ref[...])[0].astype(BF16)
        tril = _tril_mask()
        for g in range(groups):
            sl = slice(g * CHUNK, (g + 1) * CHUNK)
            w = jnp.where(tril, w_ref[g], jnp.zeros((), BF16))
            mixed = _dot(w, vn[:, sl], NN) + bb_ref[g]
            o_ref[:, sl] = (u[:, sl] * mixed).astype(BF16)

    full3 = pl.BlockSpec((groups, CHUNK, CHUNK), lambda c: (0, 0, 0))
    res = _pcall(
        body, (z, z, ln_g, ln_b, w16, bias_b), name=name, grid=(t // CHUNK,),
        in_specs=[pl.BlockSpec((CHUNK, a), lambda c: (c, 0)), pl.BlockSpec((CHUNK, a), lambda c: (c, 1)),
                  _vec_spec(a), _vec_spec(a), full3, full3],
        out_specs=[pl.BlockSpec((CHUNK, a), lambda c: (c, 0))], out_shape=[jax.ShapeDtypeStruct((t, a), BF16)],
        sem=("parallel",), comms=comms)
    return (res[0][0], res[1]) if comms else res[0]


def _sgu_bwd(z, dab, ln_g, ln_b, w16, bias_b, name, comms=()):
    t = z.shape[0]
    groups = w16.shape[0]
    a = groups * CHUNK

    def body(u_ref, v_ref, da_ref, g_ref, b_ref, w_ref, bb_ref, duv_ref, dg_ref, db_ref, dw_ref, dbs_ref, dvn_ref):
        up = u_ref[...].astype(F32)
        vp = v_ref[...].astype(F32)
        u = _gelu(up)
        ln_gain = g_ref[...]
        vn32, vhat, rs = _layernorm(_gelu(vp), ln_gain, b_ref[...])
        vn = vn32.astype(BF16)
        da = da_ref[...].astype(F32)
        tril = _tril_mask()
        ones = jnp.ones((8, CHUNK), F32)

        @pl.when(pl.program_id(0) == 0)
        def _():
            dw_ref[...] = jnp.zeros_like(dw_ref)
            dbs_ref[...] = jnp.zeros_like(dbs_ref)

        for g in range(groups):
            sl = slice(g * CHUNK, (g + 1) * CHUNK)
            w = jnp.where(tril, w_ref[g], jnp.zeros((), BF16))
            mixed = _dot(w, vn[:, sl], NN) + bb_ref[g]
            dmix = da[:, sl] * u[:, sl]
            dmix16 = dmix.astype(BF16)
            duv_ref[:, sl] = (da[:, sl] * mixed * _gelu_grad(up[:, sl])).astype(BF16)
            dvn_ref[:, sl] = _dot(w, dmix16, TN)
            dw_ref[g] += jnp.where(tril, _dot(dmix16, vn[:, sl], NT), 0.0)
            dbs_ref[g:g + 1, :] += lax.dot_general(ones, dmix, (NT, ((), ())), precision=lax.Precision.HIGHEST,
                                                   preferred_element_type=F32)[0:1]
        dvn = dvn_ref[...]
        dvhat = dvn * ln_gain
        dva = rs * (dvhat - jnp.mean(dvhat, axis=-1, keepdims=True) - vhat * jnp.mean(dvhat * vhat, axis=-1, keepdims=True))
        duv_ref[:, a:] = (dva * _gelu_grad(vp)).astype(BF16)
        _accumulate(dg_ref, jnp.sum(dvn * vhat, axis=0, keepdims=True))
        _accumulate(db_ref, jnp.sum(dvn, axis=0, keepdims=True))

    full3 = pl.BlockSpec((groups, CHUNK, CHUNK), lambda c: (0, 0, 0))
    return _pcall(
        body, (z, z, dab, ln_g, ln_b, w16, bias_b), name=name, grid=(t // CHUNK,),
        in_specs=[pl.BlockSpec((CHUNK, a), lambda c: (c, 0)), pl.BlockSpec((CHUNK, a), lambda c: (c, 1)),
                  pl.BlockSpec((CHUNK, a), lambda c: (c, 0)), _vec_spec(a), _vec_spec(a), full3, full3],
        out_specs=[pl.BlockSpec((CHUNK, 2 * a), lambda c: (c, 0)), _vec_spec(a), _vec_spec(a), full3,
                   pl.BlockSpec((groups, CHUNK), lambda c: (0, 0))],
        out_shape=[jax.ShapeDtypeStruct((t, 2 * a), BF16), jax.ShapeDtypeStruct((1, a), F32), jax.ShapeDtypeStruct((1, a), F32),
                   jax.ShapeDtypeStruct((groups, CHUNK, CHUNK), F32), jax.ShapeDtypeStruct((groups, CHUNK), F32)],
        scratch=[pltpu.VMEM((CHUNK, a), F32)], sem=("arbitrary",), comms=comms)


def _dil_masks(d):
    qi = lax.broadcasted_iota(jnp.int32, (CHUNK, CHUNK), 0)
    kj = lax.broadcasted_iota(jnp.int32, (CHUNK, CHUNK), 1)
    dist_c = qi - kj
    return dist_c >= 0, dist_c <= 0, (dist_c * d).astype(F32), ((dist_c + CHUNK) * d).astype(F32)


def _alibi_slope(h, heads):
    return 2.0 ** (-8.0 * (h + 1) / heads)


def _dil_view(z, d):
    t, w = z.shape[0], z.shape[1] // 5
    if d == 1:
        return z, 5, 2
    return z[:, 2 * w:].reshape(t // d, d * 3 * w), 3, 0


def _dil_fwd(z, d, name, comms=()):
    t = z.shape[0]
    w = z.shape[1] // 5
    heads = w // HEAD_DIM
    nb = t // d // CHUNK
    scale = HEAD_DIM ** -0.5
    zv, mult, col_q = _dil_view(z, d)

    def body(q_ref, kp_ref, kc_ref, vp_ref, vc_ref, o_ref, l_ref):
        ok_c, ok_p0, bias_c, bias_p = _dil_masks(d)
        ok_p = ok_p0 & (pl.program_id(1) > 0)
        hs = range(heads)
        sl = [slice(h * HEAD_DIM, (h + 1) * HEAD_DIM) for h in hs]
        slope = [_alibi_slope(h, heads) for h in hs]
        ones = jnp.ones((CHUNK, HEAD_DIM), BF16)
        s_c = [_dot(q_ref[:, sl[h]], kc_ref[:, sl[h]], NT) for h in hs]
        s_p = [_dot(q_ref[:, sl[h]], kp_ref[:, sl[h]], NT) for h in hs]
        s_c = [jnp.where(ok_c, s_c[h] * scale - slope[h] * bias_c, NEG) for h in hs]
        s_p = [jnp.where(ok_p, s_p[h] * scale - slope[h] * bias_p, NEG) for h in hs]
        m = [jnp.max(jnp.maximum(s_c[h], s_p[h]), axis=1, keepdims=True) for h in hs]
        p_c = [jnp.exp(s_c[h] - m[h]).astype(BF16) for h in hs]
        p_p = [jnp.exp(s_p[h] - m[h]).astype(BF16) for h in hs]
        den = [_dot(p_c[h], ones, NN) + _dot(p_p[h], ones, NN) for h in hs]
        o = [_dot(p_c[h], vc_ref[:, sl[h]], NN) + _dot(p_p[h], vp_ref[:, sl[h]], NN) for h in hs]
        l_ref[...] = jnp.zeros_like(l_ref)
        for h in hs:
            o_ref[:, sl[h]] = (o[h] / den[h]).astype(BF16)
            l_ref[:, h:h + 1] = m[h] + jnp.log(den[h][:, 0:1])

    def zspec(col, prev):
        if prev:
            return pl.BlockSpec((CHUNK, w), lambda r, n: (jnp.maximum(n - 1, 0), r * mult + col_q + col))
        return pl.BlockSpec((CHUNK, w), lambda r, n: (n, r * mult + col_q + col))

    res = _pcall(
        body, (zv, zv, zv, zv, zv), name=name, grid=(d, nb),
        in_specs=[zspec(0, False), zspec(1, True), zspec(1, False), zspec(2, True), zspec(2, False)],
        out_specs=[pl.BlockSpec((CHUNK, w), lambda r, n: (n, r)), pl.BlockSpec((CHUNK, HEAD_DIM), lambda r, n: (n, r))],
        out_shape=[jax.ShapeDtypeStruct((t // d, d * w), BF16), jax.ShapeDtypeStruct((t // d, d * HEAD_DIM), F32)],
        sem=("parallel", "parallel"), comms=comms)
    (o, lse), rws = res if comms else (res, None)
    outs = (o.reshape(t, w), lse.reshape(t, HEAD_DIM))
    return (outs, rws) if comms else outs


def _dil_merge(a_out, outs, lses, name, comms=()):
    t, a = a_out.shape
    w = outs[0].shape[1]
    heads = w // HEAD_DIM
    nbr = len(outs)

    def body(a_ref, *rest):
        o_refs, l_refs, (ab_ref, lt_ref) = rest[:nbr], rest[nbr:2 * nbr], rest[2 * nbr:]
        ls = [r[...] for r in l_refs]
        m = functools.reduce(jnp.maximum, ls)
        ws = [jnp.exp(l - m) for l in ls]
        tot = functools.reduce(jnp.add, ws)
        ws = [wt / tot for wt in ws]
        ab_ref[:, :a] = a_ref[...]
        for h in range(heads):
            sl = slice(h * HEAD_DIM, (h + 1) * HEAD_DIM)
            mix = functools.reduce(jnp.add, [wt[:, h:h + 1] * r[:, sl].astype(F32) for wt, r in zip(ws, o_refs)])
            ab_ref[:, a + h * HEAD_DIM:a + (h + 1) * HEAD_DIM] = mix.astype(BF16)
        lt_ref[...] = m + jnp.log(tot)

    return _pcall(
        body, (a_out, *outs, *lses), name=name, grid=(t // NORM_ROWS,),
        in_specs=[_row_spec(a)] + [_row_spec(w)] * nbr + [_row_spec(HEAD_DIM)] * nbr,
        out_specs=[_row_spec(a + w), _row_spec(HEAD_DIM)],
        out_shape=[jax.ShapeDtypeStruct((t, a + w), BF16), jax.ShapeDtypeStruct((t, HEAD_DIM), F32)],
        sem=("parallel",), comms=comms)


def _dil_delta(ab, dab, name):
    t, aw = ab.shape
    w = aw // 2
    heads = w // HEAD_DIM

    def body(o_ref, do_ref, dl_ref):
        dl_ref[...] = jnp.zeros_like(dl_ref)
        for h in range(heads):
            sl = slice(h * HEAD_DIM, (h + 1) * HEAD_DIM)
            dl_ref[:, h:h + 1] = jnp.sum(do_ref[:, sl].astype(F32) * o_ref[:, sl].astype(F32), axis=1, keepdims=True)

    half = pl.BlockSpec((NORM_ROWS, w), lambda i: (i, 1))
    return pl.pallas_call(body, name=name, grid=(t // NORM_ROWS,), in_specs=[half, half], out_specs=_row_spec(HEAD_DIM),
                          out_shape=jax.ShapeDtypeStruct((t, HEAD_DIM), F32), compiler_params=_params("parallel"))(ab, dab)


def _dil_bwd(z, dab, ltot, delta, d, name, comms=()):
    t = z.shape[0]
    w = z.shape[1] // 5
    heads = w // HEAD_DIM
    nb = t // d // CHUNK
    scale = HEAD_DIM ** -0.5

    def body(q_ref, qn_ref, kp_ref, kc_ref, vp_ref, vc_ref, do_ref, don_ref, l_ref, ln_ref, dl_ref, dln_ref,
             dq_ref, dk_ref, dv_ref):
        n = pl.program_id(1)
        ok_c, ok_p0, bias_c, bias_p = _dil_masks(d)
        ok_p = ok_p0 & (n > 0)
        ok_n = ok_p0 & (n < nb - 1)
        hs = range(heads)
        sl = [slice(h * HEAD_DIM, (h + 1) * HEAD_DIM) for h in hs]
        slope = [_alibi_slope(h, heads) for h in hs]
        q, qn = [q_ref[:, s] for s in sl], [qn_ref[:, s] for s in sl]
        kp, kc = [kp_ref[:, s] for s in sl], [kc_ref[:, s] for s in sl]
        vp, vc = [vp_ref[:, s] for s in sl], [vc_ref[:, s] for s in sl]
        do, don = [do_ref[:, s] for s in sl], [don_ref[:, s] for s in sl]
        s_c = [_dot(q[h], kc[h], NT) for h in hs]
        s_p = [_dot(q[h], kp[h], NT) for h in hs]
        s_n = [_dot(qn[h], kc[h], NT) for h in hs]
        dp_c = [_dot(do[h], vc[h], NT) for h in hs]
        dp_p = [_dot(do[h], vp[h], NT) for h in hs]
        dp_n = [_dot(don[h], vc[h], NT) for h in hs]
        delta = [dl_ref[:, h:h + 1] for h in hs]
        delta_n = [dln_ref[:, h:h + 1] for h in hs]
        p_c = [jnp.exp(jnp.where(ok_c, s_c[h] * scale - slope[h] * bias_c, NEG) - l_ref[:, h:h + 1]) for h in hs]
        p_p = [jnp.exp(jnp.where(ok_p, s_p[h] * scale - slope[h] * bias_p, NEG) - l_ref[:, h:h + 1]) for h in hs]
        p_n = [jnp.exp(jnp.where(ok_n, s_n[h] * scale - slope[h] * bias_p, NEG) - ln_ref[:, h:h + 1]) for h in hs]
        ds_c = [(p_c[h] * (dp_c[h] - delta[h])).astype(BF16) for h in hs]
        ds_p = [(p_p[h] * (dp_p[h] - delta[h])).astype(BF16) for h in hs]
        ds_n = [(p_n[h] * (dp_n[h] - delta_n[h])).astype(BF16) for h in hs]
        dq = [_dot(ds_c[h], kc[h], NN) + _dot(ds_p[h], kp[h], NN) for h in hs]
        dk = [_dot(ds_c[h], q[h], TN) + _dot(ds_n[h], qn[h], TN) for h in hs]
        dv = [_dot(p_c[h].astype(BF16), do[h], TN) + _dot(p_n[h].astype(BF16), don[h], TN) for h in hs]
        for h in hs:
            dq_ref[:, sl[h]] = (dq[h] * scale).astype(BF16)
            dk_ref[:, sl[h]] = (dk[h] * scale).astype(BF16)
            dv_ref[:, sl[h]] = dv[h].astype(BF16)

    def spec(mult, col, shift, width=w):
        if shift < 0:
            return pl.BlockSpec((CHUNK, width), lambda r, n: (jnp.maximum(n - 1, 0), r * mult + col))
        if shift > 0:
            return pl.BlockSpec((CHUNK, width), lambda r, n: (jnp.minimum(n + 1, nb - 1), r * mult + col))
        return pl.BlockSpec((CHUNK, width), lambda r, n: (n, r * mult + col))

    zv, mult, cq = _dil_view(z, d)
    dov = dab[:, w:].reshape(t // d, d * w)
    lv = ltot.reshape(t // d, d * HEAD_DIM)
    dlv = delta.reshape(t // d, d * HEAD_DIM)
    ospec = spec(1, 0, 0)
    res = _pcall(
        body, (zv, zv, zv, zv, zv, zv, dov, dov, lv, lv, dlv, dlv), name=name, grid=(d, nb),
        in_specs=[spec(mult, cq, 0), spec(mult, cq, 1), spec(mult, cq + 1, -1), spec(mult, cq + 1, 0),
                  spec(mult, cq + 2, -1), spec(mult, cq + 2, 0), spec(1, 0, 0), spec(1, 0, 1),
                  spec(1, 0, 0, HEAD_DIM), spec(1, 0, 1, HEAD_DIM), spec(1, 0, 0, HEAD_DIM), spec(1, 0, 1, HEAD_DIM)],
        out_specs=[ospec, ospec, ospec], out_shape=[jax.ShapeDtypeStruct((t // d, d * w), BF16)] * 3,
        sem=("parallel", "parallel"), comms=comms)
    outs, rws = res if comms else (res, None)
    outs = [o.reshape(t, w) for o in outs]
    return (outs, rws) if comms else outs


def _dz_assemble(duv, parts, name):
    t, a2 = duv.shape
    w = parts[0][0].shape[1]
    nbr = len(parts)

    def body(duv_ref, *rest):
        refs, dz_ref = rest[:-1], rest[-1]
        dz_ref[:, :a2] = duv_ref[...]
        for i in range(3):
            tot = functools.reduce(jnp.add, [refs[b * 3 + i][...].astype(F32) for b in range(nbr)])
            dz_ref[:, a2 + i * w:a2 + (i + 1) * w] = tot.astype(BF16)

    flat = [p for branch in parts for p in branch]
    return pl.pallas_call(
        body, name=name, grid=(t // NORM_ROWS,), in_specs=[_row_spec(a2)] + [_row_spec(w)] * len(flat),
        out_specs=_row_spec(a2 + 3 * w), out_shape=jax.ShapeDtypeStruct((t, a2 + 3 * w), BF16),
        compiler_params=_params("parallel"),
    )(duv, *flat)


def _split_dot(x, m16):
    hi = x.astype(BF16)
    lo = (x - hi.astype(F32)).astype(BF16)
    return _dot(hi, m16, NN) + _dot(lo, m16, NN)


SB_DEAD = -110.0


def _sb_scaled(q):
    return (q.astype(F32) * (HEAD_DIM ** -0.5)).astype(BF16)


SB_GROUP = 4
SB_PAIR = 2


def _sb_logs(qs, kj, below):
    zt = [_dot(q, k, NT) for q, k in zip(qs, kj)]
    sp = [jnp.maximum(z, 0.0) + jnp.log(1.0 + jnp.exp(-jnp.abs(z))) for z in zt]
    return [z - s for z, s in zip(zt, sp)], [(-s if below is None else jnp.where(below, -s, 0.0)) for s in sp]


def _sb_alive(s, i, c_run):
    return (s <= i) & (jnp.max(c_run) > SB_DEAD)


def _sb_fwd(zc, name, comms=()):
    t = zc.shape[0]
    c = zc.shape[1] // 3
    heads = c // HEAD_DIM
    blk = min(SB_BLOCK, t)
    grp = next(n for n in (2 * SB_GROUP, SB_GROUP, SB_PAIR) if heads % n == 0)

    def body(q_ref, k_ref, v_ref, o_ref, ct_ref, nb_ref):
        i = pl.program_id(1)
        sl = [slice(p * HEAD_DIM, (p + 1) * HEAD_DIM) for p in range(grp)]
        qs = [_sb_scaled(q_ref[:, s]) for s in sl]
        rows = lax.broadcasted_iota(jnp.int32, (blk, blk), 0)
        cols = lax.broadcasted_iota(jnp.int32, (blk, blk), 1)
        below = rows > cols
        m_right = below.astype(BF16)

        def tile(carry, diagonal):
            s, acc, c_run = carry[0], carry[1:1 + grp], carry[1 + grp:]
            off = pl.multiple_of((i - s) * blk, blk)
            log_beta, l = _sb_logs(qs, [k_ref[pl.ds(off, blk), p] for p in sl], below if diagonal else None)
            right = [_split_dot(x, m_right) for x in l]
            a = [jnp.exp(lb + (c + r)) for lb, c, r in zip(log_beta, c_run, right)]
            if diagonal:
                a = [jnp.where(below, x, 0.0) for x in a]
            acc = [o + _dot(x.astype(BF16), v_ref[pl.ds(off, blk), p], NN) for o, x, p in zip(acc, a, sl)]
            return (s + 1, *acc, *[c + jnp.sum(x, axis=1, keepdims=True) for c, x in zip(c_run, l)])

        zeros = [jnp.zeros((blk, HEAD_DIM), F32)] * grp + [jnp.zeros((blk, 1), F32)] * grp
        out = lax.while_loop(lambda carry: _sb_alive(carry[0], i, functools.reduce(jnp.maximum, carry[1 + grp:])),
                             lambda carry: tile(carry, False), tile((jnp.int32(0), *zeros), True))
        for p, s in enumerate(sl):
            o_ref[:, s] = out[1 + p].astype(BF16)
            ct_ref[:, s] = jnp.broadcast_to(out[1 + grp + p], (blk, HEAD_DIM))
        nb_ref[...] = jnp.zeros(nb_ref.shape, F32) + out[0].astype(F32)

    groups = heads // grp
    qspec = pl.BlockSpec((blk, grp * HEAD_DIM), lambda h, i: (i, h))
    once = dict(pipeline_mode=pl.Buffered(1))
    return _pcall(body, (zc, zc, zc), name=name, grid=(groups, t // blk),
                  in_specs=[qspec, pl.BlockSpec((t, grp * HEAD_DIM), lambda h, i: (0, groups + h), **once),
                            pl.BlockSpec((t, grp * HEAD_DIM), lambda h, i: (0, 2 * groups + h), **once)],
                  out_specs=[qspec, qspec, qspec],
                  out_shape=[jax.ShapeDtypeStruct((t, c), BF16), jax.ShapeDtypeStruct((t, c), F32), jax.ShapeDtypeStruct((t, c), F32)],
                  sem=("parallel", "parallel"), comms=comms)


def _sb_bwd(zc, ctot, swept, do, name, comms=()):
    t = zc.shape[0]
    c = zc.shape[1] // 3
    heads = c // HEAD_DIM
    blk = min(SB_BLOCK, t)
    scale = HEAD_DIM ** -0.5
    grp = SB_GROUP if heads % SB_GROUP == 0 else SB_PAIR

    def body(q_ref, k_ref, v_ref, ct_ref, nb_ref, do_ref, dq_ref, dk_ref, dv_ref):
        i = pl.program_id(1)

        @pl.when(i == 0)
        def _():
            dk_ref[...] = jnp.zeros_like(dk_ref)
            dv_ref[...] = jnp.zeros_like(dv_ref)

        ps = range(grp)
        sl = [slice(p * HEAD_DIM, (p + 1) * HEAD_DIM) for p in ps]
        qs = [_sb_scaled(q_ref[:, s]) for s in sl]
        dov = [do_ref[:, s] for s in sl]
        c_tot = [ct_ref[:, p * HEAD_DIM:p * HEAD_DIM + 1] for p in ps]
        n_blocks = jnp.clip(jnp.max(nb_ref[0:8, :]).astype(jnp.int32), 1, i + 1)
        rows = lax.broadcasted_iota(jnp.int32, (blk, blk), 0)
        cols = lax.broadcasted_iota(jnp.int32, (blk, blk), 1)
        below = rows > cols
        m_upto = (rows <= cols).astype(BF16)
        m_left = (rows < cols).astype(BF16)

        def tile(j, carry, diagonal):
            dq, l_run, w_run = carry[:grp], carry[grp:2 * grp], carry[2 * grp:]
            off = pl.multiple_of(j * blk, blk)
            kj = [k_ref[pl.ds(off, blk), s] for s in sl]
            vj = [v_ref[pl.ds(off, blk), s] for s in sl]
            log_beta, l = _sb_logs(qs, kj, below if diagonal else None)
            d_a = [_dot(dov[p], vj[p], NT) for p in ps]
            upto = [_split_dot(x, m_upto) for x in l]
            a = [jnp.exp(log_beta[p] + (c_tot[p] - l_run[p] - upto[p])) for p in ps]
            if diagonal:
                a = [jnp.where(below, x, 0.0) for x in a]
            wgt = [a[p] * d_a[p] for p in ps]
            before = [w_run[p] + _split_dot(wgt[p], m_left) for p in ps]
            dz = [wgt[p] * jnp.exp(l[p]) - jnp.exp(log_beta[p]) * before[p] for p in ps]
            if diagonal:
                dz = [jnp.where(below, x, 0.0) for x in dz]
            dz16 = [x.astype(BF16) for x in dz]
            dk = [_dot(dz16[p], qs[p], TN) for p in ps]
            dv = [_dot(a[p].astype(BF16), dov[p], TN) for p in ps]
            dq = [dq[p] + _dot(dz16[p], kj[p], NN) for p in ps]
            for p in ps:
                dk_ref[pl.ds(off, blk), sl[p]] += dk[p]
                dv_ref[pl.ds(off, blk), sl[p]] += dv[p]
            return (*dq, *[l_run[p] + jnp.sum(l[p], axis=1, keepdims=True) for p in ps],
                    *[w_run[p] + jnp.sum(wgt[p], axis=1, keepdims=True) for p in ps])

        zeros = [jnp.zeros((blk, HEAD_DIM), F32)] * grp + [jnp.zeros((blk, 1), F32)] * (2 * grp)
        carry = lax.fori_loop(i + 1 - n_blocks, i, lambda j, carry: tile(j, carry, False), tuple(zeros))
        out = tile(i, carry, True)
        for p in ps:
            dq_ref[:, sl[p]] = out[p] * scale

    groups = heads // grp
    qspec = pl.BlockSpec((blk, grp * HEAD_DIM), lambda h, i: (i, h))
    once = dict(pipeline_mode=pl.Buffered(1))
    full = pl.BlockSpec((t, grp * HEAD_DIM), lambda h, i: (0, h), **once)
    return _pcall(body, (zc, zc, zc, ctot, swept, do), name=name, grid=(groups, t // blk),
                  in_specs=[qspec, pl.BlockSpec((t, grp * HEAD_DIM), lambda h, i: (0, groups + h), **once),
                            pl.BlockSpec((t, grp * HEAD_DIM), lambda h, i: (0, 2 * groups + h), **once), qspec, qspec, qspec],
                  out_specs=[qspec, full, full], out_shape=[jax.ShapeDtypeStruct((t, c), F32)] * 3,
                  sem=("arbitrary", "arbitrary"), comms=comms)


def _concat_bf16(parts, name, comms=()):
    t, c = parts[0].shape

    def body(*refs):
        for k, r in enumerate(refs[:-1]):
            refs[-1][:, k * c:(k + 1) * c] = r[...].astype(BF16)

    res = _pcall(body, tuple(parts), name=name, grid=(t // NORM_ROWS,), in_specs=[_row_spec(c)] * len(parts),
                 out_specs=[_row_spec(c * len(parts))], out_shape=[jax.ShapeDtypeStruct((t, c * len(parts)), BF16)],
                 sem=("parallel",), comms=comms)
    return (res[0][0], res[1]) if comms else res[0]


KIND = {"ab_w_in": "col", "ab_w_out": "row", "sb_w_in": "col", "sb_w_out": "row",
        "ffn_w1_0": "col", "ffn_w1_1": "col", "ffn_w2_0": "row", "ffn_w2_1": "row"}
X_Y, DIAG, CHIPS = (2, 4), (6,), (2, 4, 6)


def _local_step(x, target, norms, sgu, big, bufs=None):
    g = {k: [v[l:l + 1] for l in range(2)] for k, v in norms.items()}
    ln_g, ln_b, sgu_w, sgu_b = sgu
    groups = sgu_w.shape[0]
    w16 = sgu_w.astype(BF16)
    bias_b = jnp.broadcast_to(sgu_b[:, :, None], (groups, CHUNK, CHUNK))
    big, dws, psum, dist = dict(big), {}, {}, bufs is not None
    pair, got = (dict(bufs[0]), dict(bufs[1])) if dist else ({}, {})

    def run(fn, *args, ops=(), **kw):
        if not dist or not ops:
            return fn(*args, **kw)
        make = {"gs": lambda k, p, *part: _GatherSend(big[k], KIND[k], p, *part), "gf": lambda k, p: _GatherFwd(big[k], KIND[k], p),
                "swap": lambda k, p: _PairSwap(dws[k], pair[k], KIND[k]),
                "chips": lambda k, p, *part: _ChipScatter(psum[k], got[k], p, *part)}
        out, rws = fn(*args, comms=[make[op[0]](*op[1:]) for op in ops], **kw)
        for (op, k, *_), r in zip(ops, rws):
            if op in ("gs", "gf"):
                big[k] = r[0]
            elif op == "swap":
                psum[k] = _pair_sum(dws[k], r[0], KIND[k], f"pair_sum_{k}")
            else:
                got[k] = r[0]
        return out

    h1_0 = _rms_fwd(x, g["pre_mix"][0], "rms_in")
    z0 = run(_matmul, h1_0, big["ab_w_in"], "nn", BF16, "ab_in", ops=[("gs", "ffn_w1_0", X_Y)])
    a_out = run(_sgu_fwd, z0, ln_g, ln_b, w16, bias_b, "sgu_fwd", ops=[("gf", "ffn_w1_0", X_Y), ("gs", "ab_w_out", CHIPS)])
    branches = [run(_dil_fwd, z0, 1, "dil_fwd_1", ops=[("gs", "ffn_w1_0", DIAG, (0, 2)), ("gf", "ab_w_out", CHIPS)]),
                run(_dil_fwd, z0, 4, "dil_fwd_4", ops=[("gs", "ffn_w1_0", DIAG, (1, 2))]),
                run(_dil_fwd, z0, 16, "dil_fwd_16", ops=[("gf", "ffn_w1_0", DIAG), ("gs", "ffn_w2_0", X_Y, (0, 2))])]
    ab, ltot = run(_dil_merge, a_out, [b[0] for b in branches], [b[1] for b in branches], "dil_merge",
                   ops=[("gs", "ffn_w2_0", X_Y, (1, 2))])
    y_0 = run(_matmul, ab, big["ab_w_out"], "nn", BF16, "ab_out", ops=[("gs", "ffn_w2_0", DIAG, (0, 2))])
    x1, h2_0 = run(_post_pre_fwd, y_0, g["post_mix"][0], x, g["pre_ffn"][0], "norm_mix0", ops=[("gs", "ffn_w2_0", DIAG, (1, 2))])
    r_0 = run(_matmul, h2_0, big["ffn_w1_0"], "nn", BF16, "ffn_up_0", relu_out=True,
              ops=[("gf", "ffn_w2_0", CHIPS), ("gs", "sb_w_in", CHIPS)])
    y2_0 = run(_matmul, r_0, big["ffn_w2_0"], "nn", BF16, "ffn_down_0", a_square=True,
               ops=[("gf", "sb_w_in", CHIPS), ("gs", "sb_w_out", CHIPS), ("gs", "ffn_w1_1", X_Y)])
    x2, h1_1 = run(_post_pre_fwd, y2_0, g["post_ffn"][0], x1, g["pre_mix"][1], "norm_ffn0",
                   ops=[("gf", "ffn_w1_1", X_Y), ("gf", "sb_w_out", CHIPS)])
    zc = run(_matmul, h1_1, big["sb_w_in"], "nn", BF16, "sb_in", ops=[("gs", "ffn_w1_1", DIAG)])
    o_sb, ct_sb, nb_sb = run(_sb_fwd, zc, "sb_fwd", ops=[("gf", "ffn_w1_1", DIAG), ("gs", "ffn_w2_1", CHIPS)])
    y_1 = run(_matmul, o_sb, big["sb_w_out"], "nn", BF16, "sb_out", ops=[("gf", "ffn_w2_1", CHIPS)])
    x3, h2_1 = _post_pre_fwd(y_1, g["post_mix"][1], x2, g["pre_ffn"][1], "norm_mix1")
    r_1 = _matmul(h2_1, big["ffn_w1_1"], "nn", BF16, "ffn_up_1", relu_out=True)
    y2_1 = _matmul(r_1, big["ffn_w2_1"], "nn", BF16, "ffn_down_1", a_square=True)
    loss, dx4, dy2_1, dg_post_ffn1 = _final_fwd_bwd(y2_1, g["post_ffn"][1], x3, target, "loss")

    da = _matmul(dy2_1, big["ffn_w2_1"], "nt", BF16, "ffn_da_1", mul2=r_1)
    dws["ffn_w2_1"] = _matmul(r_1, dy2_1, "tn", BF16, "ffn_dw2_1", a_square=True)
    dh2 = run(_matmul, da, big["ffn_w1_1"], "nt", BF16, "ffn_dh_1", ops=[("swap", "ffn_w2_1", None)])
    dws["ffn_w1_1"] = run(_matmul, h2_1, da, "tn", BF16, "ffn_dw1_1", ops=[("chips", "ffn_w2_1", X_Y)])
    dx3, dy_1, dg_pre_ffn1, dg_post_mix1 = run(_pre_post_bwd, x3, g["pre_ffn"][1], dh2, dx4, y_1, g["post_mix"][1], "norm_bwd_mix1",
                                               ops=[("swap", "ffn_w1_1", None)])
    do_sb = _matmul(dy_1, big["sb_w_out"], "nt", BF16, "sb_out_dx")
    dws["sb_w_out"] = _matmul(o_sb, dy_1, "tn", BF16, "sb_out_dw")
    dqkv = run(_sb_bwd, zc, ct_sb, nb_sb, do_sb, "sb_bwd",
               ops=[("chips", "ffn_w2_1", DIAG), ("chips", "ffn_w1_1", CHIPS), ("swap", "sb_w_out", None)])
    dzc = run(_concat_bf16, dqkv, "sb_dz", ops=[("chips", "sb_w_out", X_Y)])
    dh1 = run(_matmul, dzc, big["sb_w_in"], "nt", BF16, "sb_in_dx", ops=[("chips", "sb_w_out", DIAG)])
    dws["sb_w_in"] = _matmul(h1_1, dzc, "tn", BF16, "sb_in_dw")
    dx2, dy2_0, dg_pre_mix1, dg_post_ffn0 = run(_pre_post_bwd, x2, g["pre_mix"][1], dh1, dx3, y2_0, g["post_ffn"][0], "norm_bwd_ffn0",
                                                ops=[("swap", "sb_w_in", None)])
    da = run(_matmul, dy2_0, big["ffn_w2_0"], "nt", BF16, "ffn_da_0", mul2=r_0, ops=[("chips", "sb_w_in", X_Y)])
    dws["ffn_w2_0"] = run(_matmul, r_0, dy2_0, "tn", BF16, "ffn_dw2_0", a_square=True, ops=[("chips", "sb_w_in", DIAG)])
    dws["ffn_w1_0"] = run(_matmul, h2_0, da, "tn", BF16, "ffn_dw1_0", ops=[("swap", "ffn_w2_0", None)])
    dh2 = run(_matmul, da, big["ffn_w1_0"], "nt", BF16, "ffn_dh_0", ops=[("chips", "ffn_w2_0", X_Y), ("swap", "ffn_w1_0", None)])
    dx1, dy_0, dg_pre_ffn0, dg_post_mix0 = run(_pre_post_bwd, x1, g["pre_ffn"][0], dh2, dx2, y_0, g["post_mix"][0], "norm_bwd_mix0",
                                               ops=[("chips", "ffn_w2_0", DIAG, (0, 2))])
    dab = run(_matmul, dy_0, big["ab_w_out"], "nt", BF16, "ab_out_dx", ops=[("chips", "ffn_w2_0", DIAG, (1, 2))])
    dws["ab_w_out"] = run(_matmul, ab, dy_0, "tn", BF16, "ab_out_dw", ops=[("chips", "ffn_w1_0", X_Y, (0, 2))])
    duv, d_ln_g, d_ln_b, d_sgu_w, d_sgu_b = run(_sgu_bwd, z0, dab, ln_g, ln_b, w16, bias_b, "sgu_bwd",
                                                ops=[("chips", "ffn_w1_0", X_Y, (1, 2))])
    delta = _dil_delta(ab, dab, "dil_delta")
    parts = [run(_dil_bwd, z0, dab, ltot, delta, 1, "dil_bwd_1", ops=[("chips", "ffn_w1_0", DIAG, (0, 2)), ("swap", "ab_w_out", None)]),
             run(_dil_bwd, z0, dab, ltot, delta, 4, "dil_bwd_4", ops=[("chips", "ffn_w1_0", DIAG, (1, 2))]),
             run(_dil_bwd, z0, dab, ltot, delta, 16, "dil_bwd_16", ops=[("chips", "ab_w_out", CHIPS)])]
    dz0 = _dz_assemble(duv, parts, "dz_assemble")
    dws["ab_w_in"] = _matmul(h1_0, dz0, "tn", BF16, "ab_in_dw")
    dh1 = run(_matmul, dz0, big["ab_w_in"], "nt", BF16, "ab_in_dx", ops=[("swap", "ab_w_in", None)])
    grad_x, dg_pre_mix0 = run(_pre_post_bwd, x, g["pre_mix"][0], dh1, dx1, None, None, "norm_bwd_in", ops=[("chips", "ab_w_in", X_Y)])

    d_norms = {
        "pre_mix": jnp.concatenate([dg_pre_mix0, dg_pre_mix1]), "post_mix": jnp.concatenate([dg_post_mix0, dg_post_mix1]),
        "pre_ffn": jnp.concatenate([dg_pre_ffn0, dg_pre_ffn1]), "post_ffn": jnp.concatenate([dg_post_ffn0, dg_post_ffn1]),
    }
    return loss, grad_x, d_norms, (d_ln_g, d_ln_b, d_sgu_w, d_sgu_b), (psum, got) if dist else dws


def _to_bf16_full(w, layer, kind, name):
    _, rows, cols = w.shape
    tr = _tile(rows, 512)
    nblk = rows // tr
    full = (rows, 4 * cols) if kind == "col" else (4 * rows, cols)

    def body(w_ref, o_ref):
        o_ref[...] = w_ref[...].astype(BF16)

    def place(i):
        mine = 2 * lax.axis_index("x") + lax.axis_index("y")
        return (i, mine) if kind == "col" else (mine * nblk + i, 0)

    return pl.pallas_call(
        body, name=name, grid=(nblk,), in_specs=[pl.BlockSpec((None, tr, cols), lambda i: (layer, i, 0))],
        out_specs=pl.BlockSpec((tr, cols), place), out_shape=jax.ShapeDtypeStruct(full, BF16), compiler_params=_params("parallel"),
    )(w)


def _pair_sum(dw16, pair, kind, name):
    rh, cs = _half_shape(dw16.shape, kind)
    tr = _tile(rh, 256)
    nblk = rh // tr

    def body(dw_ref, pair_ref, o_ref):
        o_ref[...] = (dw_ref[...].astype(F32) + pair_ref[...].astype(F32)).astype(BF16)

    def own(s, i):
        c = lax.axis_index("c")
        return (c * nblk + i, s) if kind == "col" else ((2 * s + c) * nblk + i, 0)

    spec3 = pl.BlockSpec((None, tr, cs), lambda s, i: (s, i, 0))
    return pl.pallas_call(
        body, name=name, grid=(4, nblk), in_specs=[pl.BlockSpec((tr, cs), own), spec3], out_specs=spec3,
        out_shape=jax.ShapeDtypeStruct((4, rh, cs), BF16), compiler_params=_params("parallel", "parallel"),
    )(dw16, pair)


def _owner_sum(psum, got, buf, layer, name, comms=()):
    _, rh, cs = psum.shape
    tr = _tile(rh, 256)

    def body(p_ref, got_ref, buf_ref, o_ref):
        tot = p_ref[...].astype(F32)
        for j in range(3):
            tot = tot + got_ref[j].astype(F32)
        o_ref[...] = tot

    res = _pcall(
        body, (psum, got, buf), name=name, grid=(rh // tr,),
        in_specs=[pl.BlockSpec((None, tr, cs), lambda i: (2 * lax.axis_index("x") + lax.axis_index("y"), i, 0)),
                  pl.BlockSpec((3, tr, cs), lambda i: (0, i, 0)), ANY],
        out_specs=[pl.BlockSpec((None, None, tr, cs), lambda i: (layer, lax.axis_index("c"), i, 0))],
        out_shape=[jax.ShapeDtypeStruct(buf.shape, F32)], sem=("parallel",), comms=comms, aliases={2: 0})
    return (res[0][0], res[1]) if comms else res[0]


def _adamw_math(w, g, m, v):
    m = ADAM_B1 * m + (1.0 - ADAM_B1) * g
    v = ADAM_B2 * v + (1.0 - ADAM_B2) * (g * g)
    m_hat = m / (1.0 - ADAM_B1 ** ADAM_STEP)
    v_hat = v / (1.0 - ADAM_B2 ** ADAM_STEP)
    return -ADAM_LR * (m_hat / (jnp.sqrt(v_hat) + ADAM_EPS) + ADAM_WD * w), m, v


def _adamw(w, g, m, v, name):
    layers, rows, cols = w.shape
    tr = _tile(rows, 256)

    def body(w_ref, g_ref, m_ref, v_ref, go_ref, d_ref, mo_ref, vo_ref):
        g = g_ref[...]
        go_ref[...] = g
        d_ref[...], mo_ref[...], vo_ref[...] = _adamw_math(w_ref[...], g, m_ref[...], v_ref[...])

    spec = pl.BlockSpec((None, tr, cols), lambda l, i: (l, i, 0))
    return _pcall(body, (w, g, m, v), name=name, grid=(layers, rows // tr), in_specs=[spec] * 4, out_specs=[spec] * 4,
                  out_shape=[jax.ShapeDtypeStruct(w.shape, F32)] * 4, sem=("parallel", "parallel"))


def _pack(arrays):
    flat = jnp.concatenate([a.reshape(-1) for a in arrays])
    pad = (-flat.shape[0]) % 1024
    return jnp.pad(flat, (0, pad)).reshape(-1, 128)


def _unpack(packed, like):
    flat = packed.reshape(-1)
    out, off = [], 0
    for a in like:
        out.append(flat[off:off + a.size].reshape(a.shape))
        off += a.size
    return out


class _SmallGather:
    def __init__(self, g, parts, patterns):
        self.ro, self.rw, self.patterns, self.n_sems = [g], [parts], patterns, len(patterns)

    def start(self, ro, rw, send, recv):
        x, y, c, _ = _place()
        for k, j in enumerate(self.patterns):
            _remote(ro[0], rw[0].at[4 * x + 2 * y + c], send(k), recv(k), _flip(x, y, c, j)).start()

    def finish(self, ro, rw, send, recv):
        x, y, c, _ = _place()
        for k, j in enumerate(self.patterns):
            px, py, pc = _flip(x, y, c, j)
            slot = rw[0].at[4 * px + 2 * py + pc]
            cp = _remote(slot, slot, send(k), recv(k), (x, y, c))
            cp.wait_recv()
            cp.wait_send()


def _small_update(own, parts, w, m, v, name):
    rows = w.shape[0]

    def body(own_ref, p_ref, w_ref, m_ref, v_ref, g_ref, d_ref, mo_ref, vo_ref):
        me = 4 * lax.axis_index("x") + 2 * lax.axis_index("y") + lax.axis_index("c")
        g = jnp.where(me == 0, own_ref[...], p_ref[0])
        for k in range(1, 8):
            g = g + jnp.where(me == k, own_ref[...], p_ref[k])
        g_ref[...] = g
        d_ref[...], mo_ref[...], vo_ref[...] = _adamw_math(w_ref[...], g, m_ref[...], v_ref[...])

    return pl.pallas_call(body, name=name, out_shape=[jax.ShapeDtypeStruct((rows, 128), F32)] * 4,
                          compiler_params=_params())(own, parts, w, m, v)


SMALL = ("norm_pre_mix", "norm_post_mix", "norm_pre_ffn", "norm_post_ffn", "sgu_ln_g", "sgu_ln_b", "sgu_w", "sgu_b")
BIG = (("ab_w_in", ("ab_w_in",)), ("ab_w_out", ("ab_w_out",)), ("sb_w_in", ("sb_w_in",)), ("sb_w_out", ("sb_w_out",)),
       ("ffn_w1", ("ffn_w1_0", "ffn_w1_1")), ("ffn_w2", ("ffn_w2_0", "ffn_w2_1")))
WEIGHTS = ("norm_pre_mix", "norm_post_mix", "norm_pre_ffn", "norm_post_ffn", "ab_w_in", "sgu_ln_g", "sgu_ln_b", "sgu_w", "sgu_b",
           "ab_w_out", "sb_w_in", "sb_w_out", "ffn_w1", "ffn_w2")


def kernel(x, norm_pre_mix, norm_post_mix, norm_pre_ffn, norm_post_ffn, ab_w_in, sgu_ln_g, sgu_ln_b, sgu_w, sgu_b, ab_w_out, sb_w_in, sb_w_out, ffn_w1, ffn_w2, loss_target, m_norm_pre_mix, m_norm_post_mix, m_norm_pre_ffn, m_norm_post_ffn, m_ab_w_in, m_sgu_ln_g, m_sgu_ln_b, m_sgu_w, m_sgu_b, m_ab_w_out, m_sb_w_in, m_sb_w_out, m_ffn_w1, m_ffn_w2, v_norm_pre_mix, v_norm_post_mix, v_norm_pre_ffn, v_norm_post_ffn, v_ab_w_in, v_sgu_ln_g, v_sgu_ln_b, v_sgu_w, v_sgu_b, v_ab_w_out, v_sb_w_in, v_sb_w_out, v_ffn_w1, v_ffn_w2):
    w = dict(norm_pre_mix=norm_pre_mix, norm_post_mix=norm_post_mix, norm_pre_ffn=norm_pre_ffn, norm_post_ffn=norm_post_ffn,
             ab_w_in=ab_w_in, sgu_ln_g=sgu_ln_g, sgu_ln_b=sgu_ln_b, sgu_w=sgu_w, sgu_b=sgu_b, ab_w_out=ab_w_out, sb_w_in=sb_w_in,
             sb_w_out=sb_w_out, ffn_w1=ffn_w1, ffn_w2=ffn_w2)
    m = dict(norm_pre_mix=m_norm_pre_mix, norm_post_mix=m_norm_post_mix, norm_pre_ffn=m_norm_pre_ffn, norm_post_ffn=m_norm_post_ffn,
             ab_w_in=m_ab_w_in, sgu_ln_g=m_sgu_ln_g, sgu_ln_b=m_sgu_ln_b, sgu_w=m_sgu_w, sgu_b=m_sgu_b, ab_w_out=m_ab_w_out,
             sb_w_in=m_sb_w_in, sb_w_out=m_sb_w_out, ffn_w1=m_ffn_w1, ffn_w2=m_ffn_w2)
    v = dict(norm_pre_mix=v_norm_pre_mix, norm_post_mix=v_norm_post_mix, norm_pre_ffn=v_norm_pre_ffn, norm_post_ffn=v_norm_post_ffn,
             ab_w_in=v_ab_w_in, sgu_ln_g=v_sgu_ln_g, sgu_ln_b=v_sgu_ln_b, sgu_w=v_sgu_w, sgu_b=v_sgu_b, ab_w_out=v_ab_w_out,
             sb_w_in=v_sb_w_in, sb_w_out=v_sb_w_out, ffn_w1=v_ffn_w1, ffn_w2=v_ffn_w2)
    big, pair, got = {}, {}, {}
    for name, keys in BIG:
        for layer, key in enumerate(keys):
            big[key] = _to_bf16_full(w[name], layer, KIND[key], f"bf16_{key}")
            half = _half_shape(big[key].shape, KIND[key])
            pair[key], got[key] = lax.empty((4,) + half, BF16), lax.empty((3,) + half, BF16)
    big["ab_w_in"] = _comm_call([_Gather(big["ab_w_in"], KIND["ab_w_in"])], "gather_first")[0][0]

    norms = {k: w["norm_" + k] for k in ("pre_mix", "post_mix", "pre_ffn", "post_ffn")}
    sgu = (sgu_ln_g, sgu_ln_b, sgu_w[0], sgu_b[0])
    loss_blk, grad_x, d_norms, d_sgu, (psum, got) = _local_step(x[0], loss_target[0], norms, sgu, big, (pair, got))
    loss = lax.psum(loss_blk[0, 0], ("x", "y", "c"))

    grads, deltas, new_m, new_v = {}, {}, {}, {}
    keys_of = dict(BIG)
    small_g = _pack([d_norms["pre_mix"], d_norms["post_mix"], d_norms["pre_ffn"], d_norms["post_ffn"],
                     d_sgu[0], d_sgu[1], d_sgu[2][None], d_sgu[3][None]])
    parts = lax.empty((8,) + small_g.shape, F32)
    small_todo = [(1, 2, 4, 6), (3, 5, 7)]
    bufs, pending = {}, None
    for name in ("ffn_w2", "ffn_w1", "sb_w_in", "sb_w_out", "ab_w_out"):
        buf = lax.empty((len(keys_of[name]), 2) + psum[keys_of[name][0]].shape[1:], F32)
        for layer, key in enumerate(keys_of[name]):
            if pending is not None:
                buf, rws = _owner_sum(psum[key], got[key], buf, layer, f"sum_{key}", comms=[_Join([bufs[pending]])])
                bufs[pending], pending = rws[0][0], None
            elif small_todo:
                buf, rws = _owner_sum(psum[key], got[key], buf, layer, f"sum_{key}",
                                      comms=[_SmallGather(small_g, parts, small_todo.pop(0))])
                parts = rws[0][0]
            else:
                buf = _owner_sum(psum[key], got[key], buf, layer, f"sum_{key}")
        bufs[name], pending = buf, name
    assert not small_todo

    rws = _comm_call([_Join([bufs["ab_w_out"]]), _ChipScatter(psum["ab_w_in"], got["ab_w_in"], DIAG)], "tail_comm")
    bufs["ab_w_out"], got["ab_w_in"] = rws[0][0], rws[1][0]
    bufs["ab_w_in"] = _owner_sum(psum["ab_w_in"], got["ab_w_in"], lax.empty((1, 2) + psum["ab_w_in"].shape[1:], F32), 0, "sum_ab_w_in")
    bufs["ab_w_in"] = _comm_call([_Join([bufs["ab_w_in"]])], "join_last")[0][0]
    for name, _ in BIG:
        grads[name], deltas[name], new_m[name], new_v[name] = _adamw(w[name], bufs[name].reshape(w[name].shape), m[name], v[name], f"adamw_{name}")

    outs = _small_update(small_g, parts, _pack([w[k] for k in SMALL]), _pack([m[k] for k in SMALL]), _pack([v[k] for k in SMALL]), "small_update")
    like = [w[k] for k in SMALL]
    for dst, packed in zip((grads, deltas, new_m, new_v), outs):
        for k, a in zip(SMALL, _unpack(packed, like)):
            dst[k] = a

    return (loss, grad_x[None], *[grads[k] for k in WEIGHTS], *[deltas[k] for k in WEIGHTS],
            *[new_m[k] for k in WEIGHTS], *[new_v[k] for k in WEIGHTS])
```

```python
import functools

import jax
import jax.numpy as jnp
from jax import lax
from jax.experimental import pallas as pl
from jax.experimental.pallas import tpu as pltpu

F32 = jnp.float32
BF16 = jnp.bfloat16
MESH = pl.DeviceIdType.MESH

HEAD_DIM = 128
CHUNK = 128
DILATIONS = (1, 4, 16)
SB_BLOCK = 128
RMS_EPS = 1e-6
LN_EPS = 1e-5
ADAM_LR, ADAM_B1, ADAM_B2, ADAM_EPS, ADAM_WD, ADAM_STEP = 0.001, 0.9, 0.999, 1e-08, 0.01, 10
NEG = -1e30
V7X_VMEM_LIMIT = 48 * 1024 * 1024
ANY = pl.BlockSpec(memory_space=pl.ANY)


def _params(*sem):
    return pltpu.CompilerParams(dimension_semantics=sem if sem else None, vmem_limit_bytes=V7X_VMEM_LIMIT)


def _tile(n, pref):
    if n <= pref:
        return n
    t = pref
    while n % t:
        t -= 128
    return t


def _dot(a, b, dims):
    return lax.dot_general(a, b, (dims, ((), ())), preferred_element_type=F32)


NN = ((1,), (0,))
NT = ((1,), (1,))
TN = ((0,), (0,))


def _place():
    x, y, c = lax.axis_index("x"), lax.axis_index("y"), lax.axis_index("c")
    return x, y, c, 2 * x + y


def _flip(x, y, c, j):
    return (1 - x if j & 4 else x), (1 - y if j & 2 else y), (1 - c if j & 1 else c)


def _half_shape(full_shape, kind):
    rows, cols = full_shape
    return (rows // 2, cols // 4) if kind == "col" else (rows // 8, cols)


def _half(ref, kind, s, h):
    rh, cs = _half_shape(ref.shape, kind)
    if kind == "col":
        return ref.at[pl.ds(h * rh, rh), pl.ds(s * cs, cs)]
    return ref.at[pl.ds((2 * s + h) * rh, rh), :]


def _remote(src, dst, send, recv, to):
    return pltpu.make_async_remote_copy(src_ref=src, dst_ref=dst, send_sem=send, recv_sem=recv, device_id=to, device_id_type=MESH)


class _Gather:
    n_sems = 6

    def __init__(self, full, kind):
        self.ro, self.rw, self.kind = [], [full], kind

    def start(self, ro, rw, send, recv):
        x, y, c, mine = _place()
        own = _half(rw[0], self.kind, mine, c)
        for k, j in enumerate((2, 4, 6)):
            px, py, _ = _flip(x, y, c, j)
            _remote(own, own, send(k), recv(k), (px, py, c)).start()

    def finish(self, ro, rw, send, recv):
        x, y, c, mine = _place()
        own = _half(rw[0], self.kind, mine, c)
        for k, j in enumerate((2, 4, 6)):
            px, py, _ = _flip(x, y, c, j)
            got = _half(rw[0], self.kind, 2 * px + py, c)
            _remote(got, got, send(k), recv(k), (x, y, c)).wait_recv()
            _remote(got, got, send(3 + k), recv(3 + k), (x, y, 1 - c)).start()
        for k, j in enumerate((2, 4, 6)):
            px, py, _ = _flip(x, y, c, j)
            got = _half(rw[0], self.kind, 2 * px + py, 1 - c)
            _remote(got, got, send(3 + k), recv(3 + k), (x, y, c)).wait_recv()
        for k in range(6):
            _remote(own, own, send(k), recv(k), (x, y, c)).wait_send()


class _GatherSend:
    def __init__(self, full, kind, patterns, part=(0, 1)):
        self.ro, self.rw, self.kind, self.patterns, self.part, self.n_sems = [], [full], kind, patterns, part, len(patterns)

    def _rows(self, half):
        i, n = self.part
        rows = half.shape[0] // n
        return half.at[pl.ds(i * rows, rows), :]

    def start(self, ro, rw, send, recv):
        x, y, c, mine = _place()
        own = self._rows(_half(rw[0], self.kind, mine, c))
        for k, j in enumerate(self.patterns):
            px, py, _ = _flip(x, y, c, j)
            _remote(own, own, send(k), recv(k), (px, py, c)).start()

    def finish(self, ro, rw, send, recv):
        x, y, c, _ = _place()
        for k, j in enumerate(self.patterns):
            px, py, _ = _flip(x, y, c, j)
            got = self._rows(_half(rw[0], self.kind, 2 * px + py, c))
            cp = _remote(got, got, send(k), recv(k), (x, y, c))
            cp.wait_recv()
            cp.wait_send()


class _GatherFwd:
    def __init__(self, full, kind, patterns):
        self.ro, self.rw, self.kind, self.patterns, self.n_sems = [], [full], kind, patterns, len(patterns)

    def start(self, ro, rw, send, recv):
        x, y, c, _ = _place()
        for k, j in enumerate(self.patterns):
            px, py, _ = _flip(x, y, c, j)
            got = _half(rw[0], self.kind, 2 * px + py, c)
            _remote(got, got, send(k), recv(k), (x, y, 1 - c)).start()

    def finish(self, ro, rw, send, recv):
        x, y, c, _ = _place()
        for k, j in enumerate(self.patterns):
            px, py, _ = _flip(x, y, c, j)
            got = _half(rw[0], self.kind, 2 * px + py, 1 - c)
            cp = _remote(got, got, send(k), recv(k), (x, y, c))
            cp.wait_recv()
            cp.wait_send()


class _PairSwap:
    n_sems = 4

    def __init__(self, dw16, pair, kind):
        self.ro, self.rw, self.kind = [dw16], [pair], kind

    def start(self, ro, rw, send, recv):
        x, y, c, _ = _place()
        for s in range(4):
            _remote(_half(ro[0], self.kind, s, 1 - c), rw[0].at[s], send(s), recv(s), (x, y, 1 - c)).start()

    def finish(self, ro, rw, send, recv):
        x, y, c, _ = _place()
        for s in range(4):
            cp = _remote(rw[0].at[s], rw[0].at[s], send(s), recv(s), (x, y, c))
            cp.wait_recv()
            cp.wait_send()


class _ChipScatter:
    def __init__(self, psum, got, patterns, part=(0, 1)):
        self.ro, self.rw, self.patterns, self.part, self.n_sems = [psum], [got], patterns, part, len(patterns)

    def _rows(self, ref, slot):
        i, n = self.part
        rows = ref.shape[1] // n
        return ref.at[slot, pl.ds(i * rows, rows), :]

    def start(self, ro, rw, send, recv):
        x, y, c, _ = _place()
        for k, j in enumerate(self.patterns):
            px, py, _ = _flip(x, y, c, j)
            _remote(self._rows(ro[0], 2 * px + py), self._rows(rw[0], j // 2 - 1), send(k), recv(k), (px, py, c)).start()

    def finish(self, ro, rw, send, recv):
        x, y, c, _ = _place()
        for k, j in enumerate(self.patterns):
            slot = self._rows(rw[0], j // 2 - 1)
            cp = _remote(slot, slot, send(k), recv(k), (x, y, c))
            cp.wait_recv()
            cp.wait_send()


class _Join:
    def __init__(self, bufs):
        self.ro, self.rw, self.n_sems = [], list(bufs), sum(b.shape[0] for b in bufs)

    def _copies(self, rw, send, recv, slot):
        x, y, c, _ = _place()
        k = 0
        for ref in rw:
            for l in range(ref.shape[0]):
                yield _remote(ref.at[l, c], ref.at[l, slot(c)], send(k), recv(k), (x, y, 1 - c))
                k += 1

    def start(self, ro, rw, send, recv):
        for cp in self._copies(rw, send, recv, lambda c: c):
            cp.start()

    def finish(self, ro, rw, send, recv):
        for cp in self._copies(rw, send, recv, lambda c: 1 - c):
            cp.wait_recv()
        for cp in self._copies(rw, send, recv, lambda c: c):
            cp.wait_send()


def _comm_layout(comms):
    ro = [a for c in comms for a in c.ro]
    rw = [a for c in comms for a in c.rw]
    return ro, rw, sum(c.n_sems for c in comms)


def _comm_each(comms, method, ro_refs, rw_refs, send, recv):
    i_ro = i_rw = i_sem = 0
    for c in comms:
        getattr(c, method)(ro_refs[i_ro:i_ro + len(c.ro)], rw_refs[i_rw:i_rw + len(c.rw)],
                           lambda k, b=i_sem: send.at[b + k], lambda k, b=i_sem: recv.at[b + k])
        i_ro, i_rw, i_sem = i_ro + len(c.ro), i_rw + len(c.rw), i_sem + c.n_sems


def _split_results(comms, rws):
    out, i = [], 0
    for c in comms:
        out.append(list(rws[i:i + len(c.rw)]))
        i += len(c.rw)
    return out


def _comm_call(comms, name):
    ro, rw, n_sems = _comm_layout(comms)

    def body(*refs):
        ro_refs = refs[:len(ro)]
        rw_refs = refs[len(ro) + len(rw):len(ro) + 2 * len(rw)]
        send, recv = refs[len(ro) + 2 * len(rw):]
        _comm_each(comms, "start", ro_refs, rw_refs, send, recv)
        _comm_each(comms, "finish", ro_refs, rw_refs, send, recv)

    rws = pl.pallas_call(
        body, name=name, in_specs=[ANY] * (len(ro) + len(rw)), out_specs=[ANY] * len(rw),
        out_shape=[jax.ShapeDtypeStruct(a.shape, a.dtype) for a in rw],
        input_output_aliases={len(ro) + k: k for k in range(len(rw))},
        scratch_shapes=[pltpu.SemaphoreType.DMA((n_sems,)), pltpu.SemaphoreType.DMA((n_sems,))],
    )(*ro, *rw)
    return _split_results(comms, rws)


def _pcall(body, args, *, name, grid, in_specs, out_specs, out_shape, scratch=(), sem=(), comms=(), aliases=None):
    n_in, n_out, n_scr = len(in_specs), len(out_specs), len(scratch)
    aliases = dict(aliases or {})
    if not comms:
        return pl.pallas_call(body, name=name, grid=grid, in_specs=list(in_specs), out_specs=list(out_specs),
                              out_shape=list(out_shape), scratch_shapes=list(scratch), input_output_aliases=aliases,
                              compiler_params=_params(*sem))(*args)
    ro, rw, n_sems = _comm_layout(comms)

    def carrier(*refs):
        ins = refs[:n_in]
        ro_refs = refs[n_in:n_in + len(ro)]
        o0 = n_in + len(ro) + len(rw)
        outs = refs[o0:o0 + n_out]
        rw_refs = refs[o0 + n_out:o0 + n_out + len(rw)]
        s0 = o0 + n_out + len(rw)
        send, recv = refs[s0 + n_scr], refs[s0 + n_scr + 1]
        ids = [pl.program_id(a) for a in range(len(grid))]
        first = functools.reduce(jnp.logical_and, [i == 0 for i in ids])
        last = functools.reduce(jnp.logical_and, [i == g - 1 for i, g in zip(ids, grid)])

        @pl.when(first)
        def _():
            _comm_each(comms, "start", ro_refs, rw_refs, send, recv)

        body(*ins, *outs, *refs[s0:s0 + n_scr])

        @pl.when(last)
        def _():
            _comm_each(comms, "finish", ro_refs, rw_refs, send, recv)

    res = pl.pallas_call(
        carrier, name=name, grid=grid, in_specs=list(in_specs) + [ANY] * (len(ro) + len(rw)),
        out_specs=list(out_specs) + [ANY] * len(rw),
        out_shape=list(out_shape) + [jax.ShapeDtypeStruct(a.shape, a.dtype) for a in rw],
        input_output_aliases={**aliases, **{n_in + len(ro) + k: n_out + k for k in range(len(rw))}},
        scratch_shapes=list(scratch) + [pltpu.SemaphoreType.DMA((n_sems,)), pltpu.SemaphoreType.DMA((n_sems,))],
        compiler_params=_params(*["arbitrary"] * len(grid)),
    )(*args, *ro, *rw)
    return list(res[:n_out]), _split_results(comms, res[n_out:])


def _matmul(a, b, mode, out_dtype, name, a_square=False, relu_out=False, mul2=None, comms=()):
    if mode == "nn":
        (m, k), n = a.shape, b.shape[1]
    elif mode == "nt":
        (m, k), n = a.shape, b.shape[0]
    else:
        (k, m), n = a.shape, b.shape[1]
    tm, tn, tk = _tile(m, 1024), _tile(n, 2048 if out_dtype == BF16 else 1024), _tile(k, 2048)
    nk = k // tk
    dims = {"nn": NN, "nt": NT, "tn": TN}[mode]
    a_spec = pl.BlockSpec((tk, tm), lambda i, j, kk: (kk, i)) if mode == "tn" else pl.BlockSpec((tm, tk), lambda i, j, kk: (i, kk))
    b_spec = pl.BlockSpec((tn, tk), lambda i, j, kk: (j, kk)) if mode == "nt" else pl.BlockSpec((tk, tn), lambda i, j, kk: (kk, j))
    o_spec = pl.BlockSpec((tm, tn), lambda i, j, kk: (i, j))

    def body(a_ref, b_ref, *rest):
        m_ref = None if mul2 is None else rest[0]
        o_ref = rest[0 if mul2 is None else 1]
        kk = pl.program_id(2)

        def partial():
            av = a_ref[...]
            if a_square:
                av = av * av
            return _dot(av, b_ref[...], dims)

        def finish(r):
            if relu_out:
                r = jnp.maximum(r, 0.0)
            if mul2 is not None:
                r = r * (2.0 * m_ref[...].astype(F32))
            o_ref[...] = r.astype(out_dtype)

        if nk == 1:
            finish(partial())
            return
        acc_ref = rest[-1]

        @pl.when(kk == 0)
        def _():
            acc_ref[...] = partial()

        @pl.when(kk > 0)
        def _():
            acc_ref[...] += partial()

        @pl.when(kk == nk - 1)
        def _():
            finish(acc_ref[...])

    args = (a, b) if mul2 is None else (a, b, mul2)
    specs = [a_spec, b_spec] + ([] if mul2 is None else [o_spec])
    res = _pcall(body, args, name=name, grid=(m // tm, n // tn, nk), in_specs=specs, out_specs=[o_spec],
                 out_shape=[jax.ShapeDtypeStruct((m, n), out_dtype)], scratch=[pltpu.VMEM((tm, tn), F32)] if nk > 1 else [],
                 sem=("parallel", "parallel", "arbitrary"), comms=comms)
    return (res[0][0], res[1]) if comms else res[0]


NORM_ROWS = 256


def _rms(x, g):
    rstd = lax.rsqrt(jnp.mean(x * x, axis=-1, keepdims=True) + RMS_EPS)
    n = x * rstd
    return n * g, n, rstd


def _rms_bwd(n, rstd, g, dout):
    dn = dout * g
    return rstd * (dn - n * jnp.mean(dn * n, axis=-1, keepdims=True))


def _row_spec(d):
    return pl.BlockSpec((NORM_ROWS, d), lambda i: (i, 0))


def _vec_spec(d):
    return pl.BlockSpec((1, d), lambda i: (0, 0))


def _accumulate(ref, val):
    @pl.when(pl.program_id(0) == 0)
    def _():
        ref[...] = jnp.zeros_like(ref)

    ref[...] += val


def _rms_fwd(x, g, name):
    t, d = x.shape

    def body(x_ref, g_ref, h_ref):
        h_ref[...] = _rms(x_ref[...], g_ref[...])[0].astype(BF16)

    return pl.pallas_call(
        body, name=name, grid=(t // NORM_ROWS,), in_specs=[_row_spec(d), _vec_spec(d)], out_specs=_row_spec(d),
        out_shape=jax.ShapeDtypeStruct((t, d), BF16), compiler_params=_params("parallel"),
    )(x, g)


def _post_pre_fwd(y, g_post, x, g_pre, name, comms=()):
    t, d = x.shape

    def body(y_ref, gp_ref, x_ref, gn_ref, xn_ref, h_ref):
        xn = x_ref[...] + _rms(y_ref[...].astype(F32), gp_ref[...])[0]
        xn_ref[...] = xn
        h_ref[...] = _rms(xn, gn_ref[...])[0].astype(BF16)

    return _pcall(
        body, (y, g_post, x, g_pre), name=name, grid=(t // NORM_ROWS,),
        in_specs=[_row_spec(d), _vec_spec(d), _row_spec(d), _vec_spec(d)], out_specs=[_row_spec(d), _row_spec(d)],
        out_shape=[jax.ShapeDtypeStruct((t, d), F32), jax.ShapeDtypeStruct((t, d), BF16)], sem=("parallel",), comms=comms)


def _final_fwd_bwd(y, g_post, x, target, name):
    t, d = x.shape

    def body(y_ref, g_ref, x_ref, t_ref, loss_ref, dx_ref, dy_ref, dg_ref):
        g = g_ref[...]
        out, n, rstd = _rms(y_ref[...].astype(F32), g)
        e = x_ref[...] + out - t_ref[...]
        _accumulate(loss_ref, jnp.full(loss_ref.shape, 0.5 / d, F32) * jnp.sum(e * e))
        dx = e * (1.0 / d)
        dx_ref[...] = dx
        dy_ref[...] = _rms_bwd(n, rstd, g, dx).astype(BF16)
        _accumulate(dg_ref, jnp.sum(dx * n, axis=0, keepdims=True))

    return pl.pallas_call(
        body, name=name, grid=(t // NORM_ROWS,),
        in_specs=[_row_spec(d), _vec_spec(d), _row_spec(d), _row_spec(d)],
        out_specs=[pl.BlockSpec((8, 128), lambda i: (0, 0)), _row_spec(d), _row_spec(d), _vec_spec(d)],
        out_shape=[jax.ShapeDtypeStruct((8, 128), F32), jax.ShapeDtypeStruct((t, d), F32),
                   jax.ShapeDtypeStruct((t, d), BF16), jax.ShapeDtypeStruct((1, d), F32)],
        compiler_params=_params("arbitrary"),
    )(y, g_post, x, target)


def _pre_post_bwd(x, g_pre, dh, dx_in, y, g_post, name, comms=()):
    t, d = x.shape
    both = y is not None

    def body(x_ref, gp_ref, dh_ref, dxi_ref, *rest):
        if both:
            y_ref, gq_ref, dx_ref, dy_ref, dgp_ref, dgq_ref = rest
        else:
            dx_ref, dgp_ref = rest
        gp = gp_ref[...]
        _, n, rstd = _rms(x_ref[...], gp)
        dh_v = dh_ref[...].astype(F32)
        dx = dxi_ref[...] + _rms_bwd(n, rstd, gp, dh_v)
        dx_ref[...] = dx
        _accumulate(dgp_ref, jnp.sum(dh_v * n, axis=0, keepdims=True))
        if both:
            gq = gq_ref[...]
            _, ny, rstdy = _rms(y_ref[...].astype(F32), gq)
            dy_ref[...] = _rms_bwd(ny, rstdy, gq, dx).astype(BF16)
            _accumulate(dgq_ref, jnp.sum(dx * ny, axis=0, keepdims=True))

    in_specs = [_row_spec(d), _vec_spec(d), _row_spec(d), _row_spec(d)]
    args = [x, g_pre, dh, dx_in]
    if both:
        in_specs += [_row_spec(d), _vec_spec(d)]
        args += [y, g_post]
        out_specs = [_row_spec(d), _row_spec(d), _vec_spec(d), _vec_spec(d)]
        out_shape = [jax.ShapeDtypeStruct((t, d), F32), jax.ShapeDtypeStruct((t, d), BF16),
                     jax.ShapeDtypeStruct((1, d), F32), jax.ShapeDtypeStruct((1, d), F32)]
    else:
        out_specs = [_row_spec(d), _vec_spec(d)]
        out_shape = [jax.ShapeDtypeStruct((t, d), F32), jax.ShapeDtypeStruct((1, d), F32)]
    return _pcall(body, args, name=name, grid=(t // NORM_ROWS,), in_specs=in_specs, out_specs=out_specs, out_shape=out_shape,
                  sem=("arbitrary",), comms=comms)


def _gelu(x):
    return 0.5 * x * (1.0 + lax.erf(x * 0.7071067811865476))


def _gelu_grad(x):
    return 0.5 * (1.0 + lax.erf(x * 0.7071067811865476)) + x * jnp.exp(-0.5 * x * x) * 0.3989422804014327


def _layernorm(v, g, b):
    mu = jnp.mean(v, axis=-1, keepdims=True)
    vc = v - mu
    rs = lax.rsqrt(jnp.mean(vc * vc, axis=-1, keepdims=True) + LN_EPS)
    vhat = vc * rs
    return vhat * g + b, vhat, rs


def _tril_mask():
    return lax.broadcasted_iota(jnp.int32, (CHUNK, CHUNK), 0) >= lax.broadcasted_iota(jnp.int32, (CHUNK, CHUNK), 1)


def _sgu_fwd(z, ln_g, ln_b, w16, bias_b, name, comms=()):
    t = z.shape[0]
    groups = w16.shape[0]
    a = groups * CHUNK

    def body(u_ref, v_ref, g_ref, b_ref, w_ref, bb_ref, o_ref):
        u = _gelu(u_ref[...].astype(F32))
        vn = _layernorm(_gelu(v_ref[...].astype(F32)), g_ref[...], b_ref[...])[0].astype(BF16)
        tril = _tril_mask()
        for g in range(groups):
            sl = slice(g * CHUNK, (g + 1) * CHUNK)
            w = jnp.where(tril, w_ref[g], jnp.zeros((), BF16))
            mixed = _dot(w, vn[:, sl], NN) + bb_ref[g]
            o_ref[:, sl] = (u[:, sl] * mixed).astype(BF16)

    full3 = pl.BlockSpec((groups, CHUNK, CHUNK), lambda c: (0, 0, 0))
    res = _pcall(
        body, (z, z, ln_g, ln_b, w16, bias_b), name=name, grid=(t // CHUNK,),
        in_specs=[pl.BlockSpec((CHUNK, a), lambda c: (c, 0)), pl.BlockSpec((CHUNK, a), lambda c: (c, 1)),
                  _vec_spec(a), _vec_spec(a), full3, full3],
        out_specs=[pl.BlockSpec((CHUNK, a), lambda c: (c, 0))], out_shape=[jax.ShapeDtypeStruct((t, a), BF16)],
        sem=("parallel",), comms=comms)
    return (res[0][0], res[1]) if comms else res[0]


def _sgu_bwd(z, dab, ln_g, ln_b, w16, bias_b, name, comms=()):
    t = z.shape[0]
    groups = w16.shape[0]
    a = groups * CHUNK

    def body(u_ref, v_ref, da_ref, g_ref, b_ref, w_ref, bb_ref, duv_ref, dg_ref, db_ref, dw_ref, dbs_ref, dvn_ref):
        up = u_ref[...].astype(F32)
        vp = v_ref[...].astype(F32)
        u = _gelu(up)
        ln_gain = g_ref[...]
        vn32, vhat, rs = _layernorm(_gelu(vp), ln_gain, b_ref[...])
        vn = vn32.astype(BF16)
        da = da_ref[...].astype(F32)
        tril = _tril_mask()
        ones = jnp.ones((8, CHUNK), F32)

        @pl.when(pl.program_id(0) == 0)
        def _():
            dw_ref[...] = jnp.zeros_like(dw_ref)
            dbs_ref[...] = jnp.zeros_like(dbs_ref)

        for g in range(groups):
            sl = slice(g * CHUNK, (g + 1) * CHUNK)
            w = jnp.where(tril, w_ref[g], jnp.zeros((), BF16))
            mixed = _dot(w, vn[:, sl], NN) + bb_ref[g]
            dmix = da[:, sl] * u[:, sl]
            dmix16 = dmix.astype(BF16)
            duv_ref[:, sl] = (da[:, sl] * mixed * _gelu_grad(up[:, sl])).astype(BF16)
            dvn_ref[:, sl] = _dot(w, dmix16, TN)
            dw_ref[g] += jnp.where(tril, _dot(dmix16, vn[:, sl], NT), 0.0)
            dbs_ref[g:g + 1, :] += lax.dot_general(ones, dmix, (NT, ((), ())), precision=lax.Precision.HIGHEST,
                                                   preferred_element_type=F32)[0:1]
        dvn = dvn_ref[...]
        dvhat = dvn * ln_gain
        dva = rs * (dvhat - jnp.mean(dvhat, axis=-1, keepdims=True) - vhat * jnp.mean(dvhat * vhat, axis=-1, keepdims=True))
        duv_ref[:, a:] = (dva * _gelu_grad(vp)).astype(BF16)
        _accumulate(dg_ref, jnp.sum(dvn * vhat, axis=0, keepdims=True))
        _accumulate(db_ref, jnp.sum(dvn, axis=0, keepdims=True))

    full3 = pl.BlockSpec((groups, CHUNK, CHUNK), lambda c: (0, 0, 0))
    return _pcall(
        body, (z, z, dab, ln_g, ln_b, w16, bias_b), name=name, grid=(t // CHUNK,),
        in_specs=[pl.BlockSpec((CHUNK, a), lambda c: (c, 0)), pl.BlockSpec((CHUNK, a), lambda c: (c, 1)),
                  pl.BlockSpec((CHUNK, a), lambda c: (c, 0)), _vec_spec(a), _vec_spec(a), full3, full3],
        out_specs=[pl.BlockSpec((CHUNK, 2 * a), lambda c: (c, 0)), _vec_spec(a), _vec_spec(a), full3,
                   pl.BlockSpec((groups, CHUNK), lambda c: (0, 0))],
        out_shape=[jax.ShapeDtypeStruct((t, 2 * a), BF16), jax.ShapeDtypeStruct((1, a), F32), jax.ShapeDtypeStruct((1, a), F32),
                   jax.ShapeDtypeStruct((groups, CHUNK, CHUNK), F32), jax.ShapeDtypeStruct((groups, CHUNK), F32)],
        scratch=[pltpu.VMEM((CHUNK, a), F32)], sem=("arbitrary",), comms=comms)


def _dil_masks(d):
    qi = lax.broadcasted_iota(jnp.int32, (CHUNK, CHUNK), 0)
    kj = lax.broadcasted_iota(jnp.int32, (CHUNK, CHUNK), 1)
    dist_c = qi - kj
    return dist_c >= 0, dist_c <= 0, (dist_c * d).astype(F32), ((dist_c + CHUNK) * d).astype(F32)


def _alibi_slope(h, heads):
    return 2.0 ** (-8.0 * (h + 1) / heads)


def _dil_view(z, d):
    t, w = z.shape[0], z.shape[1] // 5
    if d == 1:
        return z, 5, 2
    return z[:, 2 * w:].reshape(t // d, d * 3 * w), 3, 0


def _dil_fwd(z, d, name, comms=()):
    t = z.shape[0]
    w = z.shape[1] // 5
    heads = w // HEAD_DIM
    nb = t // d // CHUNK
    scale = HEAD_DIM ** -0.5
    zv, mult, col_q = _dil_view(z, d)

    def body(q_ref, kp_ref, kc_ref, vp_ref, vc_ref, o_ref, l_ref):
        ok_c, ok_p0, bias_c, bias_p = _dil_masks(d)
        ok_p = ok_p0 & (pl.program_id(1) > 0)
        hs = range(heads)
        sl = [slice(h * HEAD_DIM, (h + 1) * HEAD_DIM) for h in hs]
        slope = [_alibi_slope(h, heads) for h in hs]
        ones = jnp.ones((CHUNK, HEAD_DIM), BF16)
        s_c = [_dot(q_ref[:, sl[h]], kc_ref[:, sl[h]], NT) for h in hs]
        s_p = [_dot(q_ref[:, sl[h]], kp_ref[:, sl[h]], NT) for h in hs]
        s_c = [jnp.where(ok_c, s_c[h] * scale - slope[h] * bias_c, NEG) for h in hs]
        s_p = [jnp.where(ok_p, s_p[h] * scale - slope[h] * bias_p, NEG) for h in hs]
        m = [jnp.max(jnp.maximum(s_c[h], s_p[h]), axis=1, keepdims=True) for h in hs]
        p_c = [jnp.exp(s_c[h] - m[h]).astype(BF16) for h in hs]
        p_p = [jnp.exp(s_p[h] - m[h]).astype(BF16) for h in hs]
        den = [_dot(p_c[h], ones, NN) + _dot(p_p[h], ones, NN) for h in hs]
        o = [_dot(p_c[h], vc_ref[:, sl[h]], NN) + _dot(p_p[h], vp_ref[:, sl[h]], NN) for h in hs]
        l_ref[...] = jnp.zeros_like(l_ref)
        for h in hs:
            o_ref[:, sl[h]] = (o[h] / den[h]).astype(BF16)
            l_ref[:, h:h + 1] = m[h] + jnp.log(den[h][:, 0:1])

    def zspec(col, prev):
        if prev:
            return pl.BlockSpec((CHUNK, w), lambda r, n: (jnp.maximum(n - 1, 0), r * mult + col_q + col))
        return pl.BlockSpec((CHUNK, w), lambda r, n: (n, r * mult + col_q + col))

    res = _pcall(
        body, (zv, zv, zv, zv, zv), name=name, grid=(d, nb),
        in_specs=[zspec(0, False), zspec(1, True), zspec(1, False), zspec(2, True), zspec(2, False)],
        out_specs=[pl.BlockSpec((CHUNK, w), lambda r, n: (n, r)), pl.BlockSpec((CHUNK, HEAD_DIM), lambda r, n: (n, r))],
        out_shape=[jax.ShapeDtypeStruct((t // d, d * w), BF16), jax.ShapeDtypeStruct((t // d, d * HEAD_DIM), F32)],
        sem=("parallel", "parallel"), comms=comms)
    (o, lse), rws = res if comms else (res, None)
    outs = (o.reshape(t, w), lse.reshape(t, HEAD_DIM))
    return (outs, rws) if comms else outs


def _dil_merge(a_out, outs, lses, name, comms=()):
    t, a = a_out.shape
    w = outs[0].shape[1]
    heads = w // HEAD_DIM
    nbr = len(outs)

    def body(a_ref, *rest):
        o_refs, l_refs, (ab_ref, lt_ref) = rest[:nbr], rest[nbr:2 * nbr], rest[2 * nbr:]
        ls = [r[...] for r in l_refs]
        m = functools.reduce(jnp.maximum, ls)
        ws = [jnp.exp(l - m) for l in ls]
        tot = functools.reduce(jnp.add, ws)
        ws = [wt / tot for wt in ws]
        ab_ref[:, :a] = a_ref[...]
        for h in range(heads):
            sl = slice(h * HEAD_DIM, (h + 1) * HEAD_DIM)
            mix = functools.reduce(jnp.add, [wt[:, h:h + 1] * r[:, sl].astype(F32) for wt, r in zip(ws, o_refs)])
            ab_ref[:, a + h * HEAD_DIM:a + (h + 1) * HEAD_DIM] = mix.astype(BF16)
        lt_ref[...] = m + jnp.log(tot)

    return _pcall(
        body, (a_out, *outs, *lses), name=name, grid=(t // NORM_ROWS,),
        in_specs=[_row_spec(a)] + [_row_spec(w)] * nbr + [_row_spec(HEAD_DIM)] * nbr,
        out_specs=[_row_spec(a + w), _row_spec(HEAD_DIM)],
        out_shape=[jax.ShapeDtypeStruct((t, a + w), BF16), jax.ShapeDtypeStruct((t, HEAD_DIM), F32)],
        sem=("parallel",), comms=comms)


def _dil_delta(ab, dab, name):
    t, aw = ab.shape
    w = aw // 2
    heads = w // HEAD_DIM

    def body(o_ref, do_ref, dl_ref):
        dl_ref[...] = jnp.zeros_like(dl_ref)
        for h in range(heads):
            sl = slice(h * HEAD_DIM, (h + 1) * HEAD_DIM)
            dl_ref[:, h:h + 1] = jnp.sum(do_ref[:, sl].astype(F32) * o_ref[:, sl].astype(F32), axis=1, keepdims=True)

    half = pl.BlockSpec((NORM_ROWS, w), lambda i: (i, 1))
    return pl.pallas_call(body, name=name, grid=(t // NORM_ROWS,), in_specs=[half, half], out_specs=_row_spec(HEAD_DIM),
                          out_shape=jax.ShapeDtypeStruct((t, HEAD_DIM), F32), compiler_params=_params("parallel"))(ab, dab)


def _dil_bwd(z, dab, ltot, delta, d, name, comms=()):
    t = z.shape[0]
    w = z.shape[1] // 5
    heads = w // HEAD_DIM
    nb = t // d // CHUNK
    scale = HEAD_DIM ** -0.5

    def body(q_ref, qn_ref, kp_ref, kc_ref, vp_ref, vc_ref, do_ref, don_ref, l_ref, ln_ref, dl_ref, dln_ref,
             dq_ref, dk_ref, dv_ref):
        n = pl.program_id(1)
        ok_c, ok_p0, bias_c, bias_p = _dil_masks(d)
        ok_p = ok_p0 & (n > 0)
        ok_n = ok_p0 & (n < nb - 1)
        hs = range(heads)
        sl = [slice(h * HEAD_DIM, (h + 1) * HEAD_DIM) for h in hs]
        slope = [_alibi_slope(h, heads) for h in hs]
        q, qn = [q_ref[:, s] for s in sl], [qn_ref[:, s] for s in sl]
        kp, kc = [kp_ref[:, s] for s in sl], [kc_ref[:, s] for s in sl]
        vp, vc = [vp_ref[:, s] for s in sl], [vc_ref[:, s] for s in sl]
        do, don = [do_ref[:, s] for s in sl], [don_ref[:, s] for s in sl]
        s_c = [_dot(q[h], kc[h], NT) for h in hs]
        s_p = [_dot(q[h], kp[h], NT) for h in hs]
        s_n = [_dot(qn[h], kc[h], NT) for h in hs]
        dp_c = [_dot(do[h], vc[h], NT) for h in hs]
        dp_p = [_dot(do[h], vp[h], NT) for h in hs]
        dp_n = [_dot(don[h], vc[h], NT) for h in hs]
        delta = [dl_ref[:, h:h + 1] for h in hs]
        delta_n = [dln_ref[:, h:h + 1] for h in hs]
        p_c = [jnp.exp(jnp.where(ok_c, s_c[h] * scale - slope[h] * bias_c, NEG) - l_ref[:, h:h + 1]) for h in hs]
        p_p = [jnp.exp(jnp.where(ok_p, s_p[h] * scale - slope[h] * bias_p, NEG) - l_ref[:, h:h + 1]) for h in hs]
        p_n = [jnp.exp(jnp.where(ok_n, s_n[h] * scale - slope[h] * bias_p, NEG) - ln_ref[:, h:h + 1]) for h in hs]
        ds_c = [(p_c[h] * (dp_c[h] - delta[h])).astype(BF16) for h in hs]
        ds_p = [(p_p[h] * (dp_p[h] - delta[h])).astype(BF16) for h in hs]
        ds_n = [(p_n[h] * (dp_n[h] - delta_n[h])).astype(BF16) for h in hs]
        dq = [_dot(ds_c[h], kc[h], NN) + _dot(ds_p[h], kp[h], NN) for h in hs]
        dk = [_dot(ds_c[h], q[h], TN) + _dot(ds_n[h], qn[h], TN) for h in hs]
        dv = [_dot(p_c[h].astype(BF16), do[h], TN) + _dot(p_n[h].astype(BF16), don[h], TN) for h in hs]
        for h in hs:
            dq_ref[:, sl[h]] = (dq[h] * scale).astype(BF16)
            dk_ref[:, sl[h]] = (dk[h] * scale).astype(BF16)
            dv_ref[:, sl[h]] = dv[h].astype(BF16)

    def spec(mult, col, shift, width=w):
        if shift < 0:
            return pl.BlockSpec((CHUNK, width), lambda r, n: (jnp.maximum(n - 1, 0), r * mult + col))
        if shift > 0:
            return pl.BlockSpec((CHUNK, width), lambda r, n: (jnp.minimum(n + 1, nb - 1), r * mult + col))
        return pl.BlockSpec((CHUNK, width), lambda r, n: (n, r * mult + col))

    zv, mult, cq = _dil_view(z, d)
    dov = dab[:, w:].reshape(t // d, d * w)
    lv = ltot.reshape(t // d, d * HEAD_DIM)
    dlv = delta.reshape(t // d, d * HEAD_DIM)
    ospec = spec(1, 0, 0)
    res = _pcall(
        body, (zv, zv, zv, zv, zv, zv, dov, dov, lv, lv, dlv, dlv), name=name, grid=(d, nb),
        in_specs=[spec(mult, cq, 0), spec(mult, cq, 1), spec(mult, cq + 1, -1), spec(mult, cq + 1, 0),
                  spec(mult, cq + 2, -1), spec(mult, cq + 2, 0), spec(1, 0, 0), spec(1, 0, 1),
                  spec(1, 0, 0, HEAD_DIM), spec(1, 0, 1, HEAD_DIM), spec(1, 0, 0, HEAD_DIM), spec(1, 0, 1, HEAD_DIM)],
        out_specs=[ospec, ospec, ospec], out_shape=[jax.ShapeDtypeStruct((t // d, d * w), BF16)] * 3,
        sem=("parallel", "parallel"), comms=comms)
    outs, rws = res if comms else (res, None)
    outs = [o.reshape(t, w) for o in outs]
    return (outs, rws) if comms else outs


def _dz_assemble(duv, parts, name):
    t, a2 = duv.shape
    w = parts[0][0].shape[1]
    nbr = len(parts)

    def body(duv_ref, *rest):
        refs, dz_ref = rest[:-1], rest[-1]
        dz_ref[:, :a2] = duv_ref[...]
        for i in range(3):
            tot = functools.reduce(jnp.add, [refs[b * 3 + i][...].astype(F32) for b in range(nbr)])
            dz_ref[:, a2 + i * w:a2 + (i + 1) * w] = tot.astype(BF16)

    flat = [p for branch in parts for p in branch]
    return pl.pallas_call(
        body, name=name, grid=(t // NORM_ROWS,), in_specs=[_row_spec(a2)] + [_row_spec(w)] * len(flat),
        out_specs=_row_spec(a2 + 3 * w), out_shape=jax.ShapeDtypeStruct((t, a2 + 3 * w), BF16),
        compiler_params=_params("parallel"),
    )(duv, *flat)


def _split_dot(x, m16):
    hi = x.astype(BF16)
    lo = (x - hi.astype(F32)).astype(BF16)
    return _dot(hi, m16, NN) + _dot(lo, m16, NN)


SB_DEAD = -110.0


def _sb_scaled(q):
    return (q.astype(F32) * (HEAD_DIM ** -0.5)).astype(BF16)


SB_GROUP = 4
SB_PAIR = 2


def _sb_logs(qs, kj, below):
    zt = [_dot(q, k, NT) for q, k in zip(qs, kj)]
    sp = [jnp.maximum(z, 0.0) + jnp.log(1.0 + jnp.exp(-jnp.abs(z))) for z in zt]
    return [z - s for z, s in zip(zt, sp)], [(-s if below is None else jnp.where(below, -s, 0.0)) for s in sp]


def _sb_alive(s, i, c_run):
    return (s <= i) & (jnp.max(c_run) > SB_DEAD)


def _sb_fwd(zc, name, comms=()):
    t = zc.shape[0]
    c = zc.shape[1] // 3
    heads = c // HEAD_DIM
    blk = min(SB_BLOCK, t)
    grp = SB_GROUP if heads % SB_GROUP == 0 else SB_PAIR

    def body(q_ref, k_ref, v_ref, o_ref, ct_ref, nb_ref):
        i = pl.program_id(1)
        sl = [slice(p * HEAD_DIM, (p + 1) * HEAD_DIM) for p in range(grp)]
        qs = [_sb_scaled(q_ref[:, s]) for s in sl]
        rows = lax.broadcasted_iota(jnp.int32, (blk, blk), 0)
        cols = lax.broadcasted_iota(jnp.int32, (blk, blk), 1)
        below = rows > cols
        m_right = below.astype(BF16)

        def tile(carry, diagonal):
            s, acc, c_run = carry[0], carry[1:1 + grp], carry[1 + grp:]
            off = pl.multiple_of((i - s) * blk, blk)
            log_beta, l = _sb_logs(qs, [k_ref[pl.ds(off, blk), p] for p in sl], below if diagonal else None)
            right = [_split_dot(x, m_right) for x in l]
            a = [jnp.exp(lb + (c + r)) for lb, c, r in zip(log_beta, c_run, right)]
            if diagonal:
                a = [jnp.where(below, x, 0.0) for x in a]
            acc = [o + _dot(x.astype(BF16), v_ref[pl.ds(off, blk), p], NN) for o, x, p in zip(acc, a, sl)]
            return (s + 1, *acc, *[c + jnp.sum(x, axis=1, keepdims=True) for c, x in zip(c_run, l)])

        zeros = [jnp.zeros((blk, HEAD_DIM), F32)] * grp + [jnp.zeros((blk, 1), F32)] * grp
        out = lax.while_loop(lambda carry: _sb_alive(carry[0], i, functools.reduce(jnp.maximum, carry[1 + grp:])),
                             lambda carry: tile(carry, False), tile((jnp.int32(0), *zeros), True))
        for p, s in enumerate(sl):
            o_ref[:, s] = out[1 + p].astype(BF16)
            ct_ref[:, s] = jnp.broadcast_to(out[1 + grp + p], (blk, HEAD_DIM))
        nb_ref[...] = jnp.zeros(nb_ref.shape, F32) + out[0].astype(F32)

    groups = heads // grp
    qspec = pl.BlockSpec((blk, grp * HEAD_DIM), lambda h, i: (i, h))
    return _pcall(body, (zc, zc, zc), name=name, grid=(groups, t // blk),
                  in_specs=[qspec, pl.BlockSpec((t, grp * HEAD_DIM), lambda h, i: (0, groups + h)),
                            pl.BlockSpec((t, grp * HEAD_DIM), lambda h, i: (0, 2 * groups + h))],
                  out_specs=[qspec, qspec, qspec],
                  out_shape=[jax.ShapeDtypeStruct((t, c), BF16), jax.ShapeDtypeStruct((t, c), F32), jax.ShapeDtypeStruct((t, c), F32)],
                  sem=("parallel", "parallel"), comms=comms)


def _sb_bwd(zc, ctot, swept, do, name, comms=()):
    t = zc.shape[0]
    c = zc.shape[1] // 3
    heads = c // HEAD_DIM
    blk = min(SB_BLOCK, t)
    scale = HEAD_DIM ** -0.5
    grp = SB_GROUP if heads % SB_GROUP == 0 else SB_PAIR

    def body(q_ref, k_ref, v_ref, ct_ref, nb_ref, do_ref, dq_ref, dk_ref, dv_ref):
        i = pl.program_id(1)

        @pl.when(i == 0)
        def _():
            dk_ref[...] = jnp.zeros_like(dk_ref)
            dv_ref[...] = jnp.zeros_like(dv_ref)

        ps = range(grp)
        sl = [slice(p * HEAD_DIM, (p + 1) * HEAD_DIM) for p in ps]
        qs = [_sb_scaled(q_ref[:, s]) for s in sl]
        dov = [do_ref[:, s] for s in sl]
        c_tot = [ct_ref[:, p * HEAD_DIM:p * HEAD_DIM + 1] for p in ps]
        n_blocks = jnp.clip(jnp.max(nb_ref[0:8, :]).astype(jnp.int32), 1, i + 1)
        rows = lax.broadcasted_iota(jnp.int32, (blk, blk), 0)
        cols = lax.broadcasted_iota(jnp.int32, (blk, blk), 1)
        below = rows > cols
        m_upto = (rows <= cols).astype(BF16)
        m_left = (rows < cols).astype(BF16)

        def tile(j, carry, diagonal):
            dq, l_run, w_run = carry[:grp], carry[grp:2 * grp], carry[2 * grp:]
            off = pl.multiple_of(j * blk, blk)
            kj = [k_ref[pl.ds(off, blk), s] for s in sl]
            vj = [v_ref[pl.ds(off, blk), s] for s in sl]
            log_beta, l = _sb_logs(qs, kj, below if diagonal else None)
            d_a = [_dot(dov[p], vj[p], NT) for p in ps]
            upto = [_split_dot(x, m_upto) for x in l]
            a = [jnp.exp(log_beta[p] + (c_tot[p] - l_run[p] - upto[p])) for p in ps]
            if diagonal:
                a = [jnp.where(below, x, 0.0) for x in a]
            wgt = [a[p] * d_a[p] for p in ps]
            before = [w_run[p] + _split_dot(wgt[p], m_left) for p in ps]
            dz = [wgt[p] * jnp.exp(l[p]) - jnp.exp(log_beta[p]) * before[p] for p in ps]
            if diagonal:
                dz = [jnp.where(below, x, 0.0) for x in dz]
            dz16 = [x.astype(BF16) for x in dz]
            dk = [_dot(dz16[p], qs[p], TN) for p in ps]
            dv = [_dot(a[p].astype(BF16), dov[p], TN) for p in ps]
            dq = [dq[p] + _dot(dz16[p], kj[p], NN) for p in ps]
            for p in ps:
                dk_ref[pl.ds(off, blk), sl[p]] += dk[p]
                dv_ref[pl.ds(off, blk), sl[p]] += dv[p]
            return (*dq, *[l_run[p] + jnp.sum(l[p], axis=1, keepdims=True) for p in ps],
                    *[w_run[p] + jnp.sum(wgt[p], axis=1, keepdims=True) for p in ps])

        zeros = [jnp.zeros((blk, HEAD_DIM), F32)] * grp + [jnp.zeros((blk, 1), F32)] * (2 * grp)
        carry = lax.fori_loop(i + 1 - n_blocks, i, lambda j, carry: tile(j, carry, False), tuple(zeros))
        out = tile(i, carry, True)
        for p in ps:
            dq_ref[:, sl[p]] = out[p] * scale

    groups = heads // grp
    qspec = pl.BlockSpec((blk, grp * HEAD_DIM), lambda h, i: (i, h))
    once = dict(pipeline_mode=pl.Buffered(1))
    full = pl.BlockSpec((t, grp * HEAD_DIM), lambda h, i: (0, h), **once)
    return _pcall(body, (zc, zc, zc, ctot, swept, do), name=name, grid=(groups, t // blk),
                  in_specs=[qspec, pl.BlockSpec((t, grp * HEAD_DIM), lambda h, i: (0, groups + h), **once),
                            pl.BlockSpec((t, grp * HEAD_DIM), lambda h, i: (0, 2 * groups + h), **once), qspec, qspec, qspec],
                  out_specs=[qspec, full, full], out_shape=[jax.ShapeDtypeStruct((t, c), F32)] * 3,
                  sem=("arbitrary", "arbitrary"), comms=comms)


def _concat_bf16(parts, name, comms=()):
    t, c = parts[0].shape

    def body(*refs):
        for k, r in enumerate(refs[:-1]):
            refs[-1][:, k * c:(k + 1) * c] = r[...].astype(BF16)

    res = _pcall(body, tuple(parts), name=name, grid=(t // NORM_ROWS,), in_specs=[_row_spec(c)] * len(parts),
                 out_specs=[_row_spec(c * len(parts))], out_shape=[jax.ShapeDtypeStruct((t, c * len(parts)), BF16)],
                 sem=("parallel",), comms=comms)
    return (res[0][0], res[1]) if comms else res[0]


KIND = {"ab_w_in": "col", "ab_w_out": "row", "sb_w_in": "col", "sb_w_out": "row",
        "ffn_w1_0": "col", "ffn_w1_1": "col", "ffn_w2_0": "row", "ffn_w2_1": "row"}
X_Y, DIAG, CHIPS = (2, 4), (6,), (2, 4, 6)


def _local_step(x, target, norms, sgu, big, bufs=None):
    g = {k: [v[l:l + 1] for l in range(2)] for k, v in norms.items()}
    ln_g, ln_b, sgu_w, sgu_b = sgu
    groups = sgu_w.shape[0]
    w16 = sgu_w.astype(BF16)
    bias_b = jnp.broadcast_to(sgu_b[:, :, None], (groups, CHUNK, CHUNK))
    big, dws, psum, dist = dict(big), {}, {}, bufs is not None
    pair, got = (dict(bufs[0]), dict(bufs[1])) if dist else ({}, {})

    def run(fn, *args, ops=(), **kw):
        if not dist or not ops:
            return fn(*args, **kw)
        make = {"gs": lambda k, p, *part: _GatherSend(big[k], KIND[k], p, *part), "gf": lambda k, p: _GatherFwd(big[k], KIND[k], p),
                "swap": lambda k, p: _PairSwap(dws[k], pair[k], KIND[k]),
                "chips": lambda k, p, *part: _ChipScatter(psum[k], got[k], p, *part)}
        out, rws = fn(*args, comms=[make[op[0]](*op[1:]) for op in ops], **kw)
        for (op, k, *_), r in zip(ops, rws):
            if op in ("gs", "gf"):
                big[k] = r[0]
            elif op == "swap":
                psum[k] = _pair_sum(dws[k], r[0], KIND[k], f"pair_sum_{k}")
            else:
                got[k] = r[0]
        return out

    h1_0 = _rms_fwd(x, g["pre_mix"][0], "rms_in")
    z0 = run(_matmul, h1_0, big["ab_w_in"], "nn", BF16, "ab_in", ops=[("gs", "ffn_w1_0", X_Y)])
    a_out = run(_sgu_fwd, z0, ln_g, ln_b, w16, bias_b, "sgu_fwd", ops=[("gf", "ffn_w1_0", X_Y), ("gs", "ab_w_out", CHIPS)])
    branches = [run(_dil_fwd, z0, 1, "dil_fwd_1", ops=[("gs", "ffn_w1_0", DIAG, (0, 2)), ("gf", "ab_w_out", CHIPS)]),
                run(_dil_fwd, z0, 4, "dil_fwd_4", ops=[("gs", "ffn_w1_0", DIAG, (1, 2))]),
                run(_dil_fwd, z0, 16, "dil_fwd_16", ops=[("gf", "ffn_w1_0", DIAG), ("gs", "ffn_w2_0", X_Y, (0, 2))])]
    ab, ltot = run(_dil_merge, a_out, [b[0] for b in branches], [b[1] for b in branches], "dil_merge",
                   ops=[("gs", "ffn_w2_0", X_Y, (1, 2))])
    y_0 = run(_matmul, ab, big["ab_w_out"], "nn", BF16, "ab_out", ops=[("gs", "ffn_w2_0", DIAG, (0, 2))])
    x1, h2_0 = run(_post_pre_fwd, y_0, g["post_mix"][0], x, g["pre_ffn"][0], "norm_mix0", ops=[("gs", "ffn_w2_0", DIAG, (1, 2))])
    r_0 = run(_matmul, h2_0, big["ffn_w1_0"], "nn", BF16, "ffn_up_0", relu_out=True,
              ops=[("gf", "ffn_w2_0", CHIPS), ("gs", "sb_w_in", CHIPS)])
    y2_0 = run(_matmul, r_0, big["ffn_w2_0"], "nn", BF16, "ffn_down_0", a_square=True,
               ops=[("gf", "sb_w_in", CHIPS), ("gs", "sb_w_out", CHIPS), ("gs", "ffn_w1_1", X_Y)])
    x2, h1_1 = run(_post_pre_fwd, y2_0, g["post_ffn"][0], x1, g["pre_mix"][1], "norm_ffn0",
                   ops=[("gf", "ffn_w1_1", X_Y), ("gf", "sb_w_out", CHIPS)])
    zc = run(_matmul, h1_1, big["sb_w_in"], "nn", BF16, "sb_in", ops=[("gs", "ffn_w1_1", DIAG)])
    o_sb, ct_sb, nb_sb = run(_sb_fwd, zc, "sb_fwd", ops=[("gf", "ffn_w1_1", DIAG), ("gs", "ffn_w2_1", CHIPS)])
    y_1 = run(_matmul, o_sb, big["sb_w_out"], "nn", BF16, "sb_out", ops=[("gf", "ffn_w2_1", CHIPS)])
    x3, h2_1 = _post_pre_fwd(y_1, g["post_mix"][1], x2, g["pre_ffn"][1], "norm_mix1")
    r_1 = _matmul(h2_1, big["ffn_w1_1"], "nn", BF16, "ffn_up_1", relu_out=True)
    y2_1 = _matmul(r_1, big["ffn_w2_1"], "nn", BF16, "ffn_down_1", a_square=True)
    loss, dx4, dy2_1, dg_post_ffn1 = _final_fwd_bwd(y2_1, g["post_ffn"][1], x3, target, "loss")

    da = _matmul(dy2_1, big["ffn_w2_1"], "nt", BF16, "ffn_da_1", mul2=r_1)
    dws["ffn_w2_1"] = _matmul(r_1, dy2_1, "tn", BF16, "ffn_dw2_1", a_square=True)
    dh2 = run(_matmul, da, big["ffn_w1_1"], "nt", BF16, "ffn_dh_1", ops=[("swap", "ffn_w2_1", None)])
    dws["ffn_w1_1"] = run(_matmul, h2_1, da, "tn", BF16, "ffn_dw1_1", ops=[("chips", "ffn_w2_1", X_Y)])
    dx3, dy_1, dg_pre_ffn1, dg_post_mix1 = run(_pre_post_bwd, x3, g["pre_ffn"][1], dh2, dx4, y_1, g["post_mix"][1], "norm_bwd_mix1",
                                               ops=[("swap", "ffn_w1_1", None)])
    do_sb = _matmul(dy_1, big["sb_w_out"], "nt", BF16, "sb_out_dx")
    dws["sb_w_out"] = _matmul(o_sb, dy_1, "tn", BF16, "sb_out_dw")
    dqkv = run(_sb_bwd, zc, ct_sb, nb_sb, do_sb, "sb_bwd",
               ops=[("chips", "ffn_w2_1", DIAG), ("chips", "ffn_w1_1", CHIPS), ("swap", "sb_w_out", None)])
    dzc = run(_concat_bf16, dqkv, "sb_dz", ops=[("chips", "sb_w_out", X_Y)])
    dh1 = run(_matmul, dzc, big["sb_w_in"], "nt", BF16, "sb_in_dx", ops=[("chips", "sb_w_out", DIAG)])
    dws["sb_w_in"] = _matmul(h1_1, dzc, "tn", BF16, "sb_in_dw")
    dx2, dy2_0, dg_pre_mix1, dg_post_ffn0 = run(_pre_post_bwd, x2, g["pre_mix"][1], dh1, dx3, y2_0, g["post_ffn"][0], "norm_bwd_ffn0",
                                                ops=[("swap", "sb_w_in", None)])
    da = run(_matmul, dy2_0, big["ffn_w2_0"], "nt", BF16, "ffn_da_0", mul2=r_0, ops=[("chips", "sb_w_in", X_Y)])
    dws["ffn_w2_0"] = run(_matmul, r_0, dy2_0, "tn", BF16, "ffn_dw2_0", a_square=True, ops=[("chips", "sb_w_in", DIAG)])
    dws["ffn_w1_0"] = run(_matmul, h2_0, da, "tn", BF16, "ffn_dw1_0", ops=[("swap", "ffn_w2_0", None)])
    dh2 = run(_matmul, da, big["ffn_w1_0"], "nt", BF16, "ffn_dh_0", ops=[("chips", "ffn_w2_0", X_Y), ("swap", "ffn_w1_0", None)])
    dx1, dy_0, dg_pre_ffn0, dg_post_mix0 = run(_pre_post_bwd, x1, g["pre_ffn"][0], dh2, dx2, y_0, g["post_mix"][0], "norm_bwd_mix0",
                                               ops=[("chips", "ffn_w2_0", DIAG, (0, 2))])
    dab = run(_matmul, dy_0, big["ab_w_out"], "nt", BF16, "ab_out_dx", ops=[("chips", "ffn_w2_0", DIAG, (1, 2))])
    dws["ab_w_out"] = run(_matmul, ab, dy_0, "tn", BF16, "ab_out_dw", ops=[("chips", "ffn_w1_0", X_Y, (0, 2))])
    duv, d_ln_g, d_ln_b, d_sgu_w, d_sgu_b = run(_sgu_bwd, z0, dab, ln_g, ln_b, w16, bias_b, "sgu_bwd",
                                                ops=[("chips", "ffn_w1_0", X_Y, (1, 2))])
    delta = _dil_delta(ab, dab, "dil_delta")
    parts = [run(_dil_bwd, z0, dab, ltot, delta, 1, "dil_bwd_1", ops=[("chips", "ffn_w1_0", DIAG, (0, 2)), ("swap", "ab_w_out", None)]),
             run(_dil_bwd, z0, dab, ltot, delta, 4, "dil_bwd_4", ops=[("chips", "ffn_w1_0", DIAG, (1, 2))]),
             run(_dil_bwd, z0, dab, ltot, delta, 16, "dil_bwd_16", ops=[("chips", "ab_w_out", CHIPS)])]
    dz0 = _dz_assemble(duv, parts, "dz_assemble")
    dws["ab_w_in"] = _matmul(h1_0, dz0, "tn", BF16, "ab_in_dw")
    dh1 = run(_matmul, dz0, big["ab_w_in"], "nt", BF16, "ab_in_dx", ops=[("swap", "ab_w_in", None)])
    grad_x, dg_pre_mix0 = run(_pre_post_bwd, x, g["pre_mix"][0], dh1, dx1, None, None, "norm_bwd_in", ops=[("chips", "ab_w_in", X_Y)])

    d_norms = {
        "pre_mix": jnp.concatenate([dg_pre_mix0, dg_pre_mix1]), "post_mix": jnp.concatenate([dg_post_mix0, dg_post_mix1]),
        "pre_ffn": jnp.concatenate([dg_pre_ffn0, dg_pre_ffn1]), "post_ffn": jnp.concatenate([dg_post_ffn0, dg_post_ffn1]),
    }
    return loss, grad_x, d_norms, (d_ln_g, d_ln_b, d_sgu_w, d_sgu_b), (psum, got) if dist else dws


def _to_bf16_full(w, layer, kind, name):
    _, rows, cols = w.shape
    tr = _tile(rows, 512)
    nblk = rows // tr
    full = (rows, 4 * cols) if kind == "col" else (4 * rows, cols)

    def body(w_ref, o_ref):
        o_ref[...] = w_ref[...].astype(BF16)

    def place(i):
        mine = 2 * lax.axis_index("x") + lax.axis_index("y")
        return (i, mine) if kind == "col" else (mine * nblk + i, 0)

    return pl.pallas_call(
        body, name=name, grid=(nblk,), in_specs=[pl.BlockSpec((None, tr, cols), lambda i: (layer, i, 0))],
        out_specs=pl.BlockSpec((tr, cols), place), out_shape=jax.ShapeDtypeStruct(full, BF16), compiler_params=_params("parallel"),
    )(w)


def _pair_sum(dw16, pair, kind, name):
    rh, cs = _half_shape(dw16.shape, kind)
    tr = _tile(rh, 256)
    nblk = rh // tr

    def body(dw_ref, pair_ref, o_ref):
        o_ref[...] = (dw_ref[...].astype(F32) + pair_ref[...].astype(F32)).astype(BF16)

    def own(s, i):
        c = lax.axis_index("c")
        return (c * nblk + i, s) if kind == "col" else ((2 * s + c) * nblk + i, 0)

    spec3 = pl.BlockSpec((None, tr, cs), lambda s, i: (s, i, 0))
    return pl.pallas_call(
        body, name=name, grid=(4, nblk), in_specs=[pl.BlockSpec((tr, cs), own), spec3], out_specs=spec3,
        out_shape=jax.ShapeDtypeStruct((4, rh, cs), BF16), compiler_params=_params("parallel", "parallel"),
    )(dw16, pair)


def _owner_sum(psum, got, buf, layer, name, comms=()):
    _, rh, cs = psum.shape
    tr = _tile(rh, 256)

    def body(p_ref, got_ref, buf_ref, o_ref):
        tot = p_ref[...].astype(F32)
        for j in range(3):
            tot = tot + got_ref[j].astype(F32)
        o_ref[...] = tot

    res = _pcall(
        body, (psum, got, buf), name=name, grid=(rh // tr,),
        in_specs=[pl.BlockSpec((None, tr, cs), lambda i: (2 * lax.axis_index("x") + lax.axis_index("y"), i, 0)),
                  pl.BlockSpec((3, tr, cs), lambda i: (0, i, 0)), ANY],
        out_specs=[pl.BlockSpec((None, None, tr, cs), lambda i: (layer, lax.axis_index("c"), i, 0))],
        out_shape=[jax.ShapeDtypeStruct(buf.shape, F32)], sem=("parallel",), comms=comms, aliases={2: 0})
    return (res[0][0], res[1]) if comms else res[0]


def _adamw_math(w, g, m, v):
    m = ADAM_B1 * m + (1.0 - ADAM_B1) * g
    v = ADAM_B2 * v + (1.0 - ADAM_B2) * (g * g)
    m_hat = m / (1.0 - ADAM_B1 ** ADAM_STEP)
    v_hat = v / (1.0 - ADAM_B2 ** ADAM_STEP)
    return -ADAM_LR * (m_hat / (jnp.sqrt(v_hat) + ADAM_EPS) + ADAM_WD * w), m, v


def _adamw(w, g, m, v, name):
    layers, rows, cols = w.shape
    tr = _tile(rows, 256)

    def body(w_ref, g_ref, m_ref, v_ref, go_ref, d_ref, mo_ref, vo_ref):
        g = g_ref[...]
        go_ref[...] = g
        d_ref[...], mo_ref[...], vo_ref[...] = _adamw_math(w_ref[...], g, m_ref[...], v_ref[...])

    spec = pl.BlockSpec((None, tr, cols), lambda l, i: (l, i, 0))
    return _pcall(body, (w, g, m, v), name=name, grid=(layers, rows // tr), in_specs=[spec] * 4, out_specs=[spec] * 4,
                  out_shape=[jax.ShapeDtypeStruct(w.shape, F32)] * 4, sem=("parallel", "parallel"))


def _pack(arrays):
    flat = jnp.concatenate([a.reshape(-1) for a in arrays])
    pad = (-flat.shape[0]) % 1024
    return jnp.pad(flat, (0, pad)).reshape(-1, 128)


def _unpack(packed, like):
    flat = packed.reshape(-1)
    out, off = [], 0
    for a in like:
        out.append(flat[off:off + a.size].reshape(a.shape))
        off += a.size
    return out


class _SmallGather:
    def __init__(self, g, parts, patterns):
        self.ro, self.rw, self.patterns, self.n_sems = [g], [parts], patterns, len(patterns)

    def start(self, ro, rw, send, recv):
        x, y, c, _ = _place()
        for k, j in enumerate(self.patterns):
            _remote(ro[0], rw[0].at[4 * x + 2 * y + c], send(k), recv(k), _flip(x, y, c, j)).start()

    def finish(self, ro, rw, send, recv):
        x, y, c, _ = _place()
        for k, j in enumerate(self.patterns):
            px, py, pc = _flip(x, y, c, j)
            slot = rw[0].at[4 * px + 2 * py + pc]
            cp = _remote(slot, slot, send(k), recv(k), (x, y, c))
            cp.wait_recv()
            cp.wait_send()


def _small_update(own, parts, w, m, v, name):
    rows = w.shape[0]

    def body(own_ref, p_ref, w_ref, m_ref, v_ref, g_ref, d_ref, mo_ref, vo_ref):
        me = 4 * lax.axis_index("x") + 2 * lax.axis_index("y") + lax.axis_index("c")
        g = jnp.where(me == 0, own_ref[...], p_ref[0])
        for k in range(1, 8):
            g = g + jnp.where(me == k, own_ref[...], p_ref[k])
        g_ref[...] = g
        d_ref[...], mo_ref[...], vo_ref[...] = _adamw_math(w_ref[...], g, m_ref[...], v_ref[...])

    return pl.pallas_call(body, name=name, out_shape=[jax.ShapeDtypeStruct((rows, 128), F32)] * 4,
                          compiler_params=_params())(own, parts, w, m, v)


SMALL = ("norm_pre_mix", "norm_post_mix", "norm_pre_ffn", "norm_post_ffn", "sgu_ln_g", "sgu_ln_b", "sgu_w", "sgu_b")
BIG = (("ab_w_in", ("ab_w_in",)), ("ab_w_out", ("ab_w_out",)), ("sb_w_in", ("sb_w_in",)), ("sb_w_out", ("sb_w_out",)),
       ("ffn_w1", ("ffn_w1_0", "ffn_w1_1")), ("ffn_w2", ("ffn_w2_0", "ffn_w2_1")))
WEIGHTS = ("norm_pre_mix", "norm_post_mix", "norm_pre_ffn", "norm_post_ffn", "ab_w_in", "sgu_ln_g", "sgu_ln_b", "sgu_w", "sgu_b",
           "ab_w_out", "sb_w_in", "sb_w_out", "ffn_w1", "ffn_w2")


def kernel(x, norm_pre_mix, norm_post_mix, norm_pre_ffn, norm_post_ffn, ab_w_in, sgu_ln_g, sgu_ln_b, sgu_w, sgu_b, ab_w_out, sb_w_in, sb_w_out, ffn_w1, ffn_w2, loss_target, m_norm_pre_mix, m_norm_post_mix, m_norm_pre_ffn, m_norm_post_ffn, m_ab_w_in, m_sgu_ln_g, m_sgu_ln_b, m_sgu_w, m_sgu_b, m_ab_w_out, m_sb_w_in, m_sb_w_out, m_ffn_w1, m_ffn_w2, v_norm_pre_mix, v_norm_post_mix, v_norm_pre_ffn, v_norm_post_ffn, v_ab_w_in, v_sgu_ln_g, v_sgu_ln_b, v_sgu_w, v_sgu_b, v_ab_w_out, v_sb_w_in, v_sb_w_out, v_ffn_w1, v_ffn_w2):
    w = dict(norm_pre_mix=norm_pre_mix, norm_post_mix=norm_post_mix, norm_pre_ffn=norm_pre_ffn, norm_post_ffn=norm_post_ffn,
             ab_w_in=ab_w_in, sgu_ln_g=sgu_ln_g, sgu_ln_b=sgu_ln_b, sgu_w=sgu_w, sgu_b=sgu_b, ab_w_out=ab_w_out, sb_w_in=sb_w_in,
             sb_w_out=sb_w_out, ffn_w1=ffn_w1, ffn_w2=ffn_w2)
    m = dict(norm_pre_mix=m_norm_pre_mix, norm_post_mix=m_norm_post_mix, norm_pre_ffn=m_norm_pre_ffn, norm_post_ffn=m_norm_post_ffn,
             ab_w_in=m_ab_w_in, sgu_ln_g=m_sgu_ln_g, sgu_ln_b=m_sgu_ln_b, sgu_w=m_sgu_w, sgu_b=m_sgu_b, ab_w_out=m_ab_w_out,
             sb_w_in=m_sb_w_in, sb_w_out=m_sb_w_out, ffn_w1=m_ffn_w1, ffn_w2=m_ffn_w2)
    v = dict(norm_pre_mix=v_norm_pre_mix, norm_post_mix=v_norm_post_mix, norm_pre_ffn=v_norm_pre_ffn, norm_post_ffn=v_norm_post_ffn,
             ab_w_in=v_ab_w_in, sgu_ln_g=v_sgu_ln_g, sgu_ln_b=v_sgu_ln_b, sgu_w=v_sgu_w, sgu_b=v_sgu_b, ab_w_out=v_ab_w_out,
             sb_w_in=v_sb_w_in, sb_w_out=v_sb_w_out, ffn_w1=v_ffn_w1, ffn_w2=v_ffn_w2)
    big, pair, got = {}, {}, {}
    for name, keys in BIG:
        for layer, key in enumerate(keys):
            big[key] = _to_bf16_full(w[name], layer, KIND[key], f"bf16_{key}")
            half = _half_shape(big[key].shape, KIND[key])
            pair[key], got[key] = lax.empty((4,) + half, BF16), lax.empty((3,) + half, BF16)
    big["ab_w_in"] = _comm_call([_Gather(big["ab_w_in"], KIND["ab_w_in"])], "gather_first")[0][0]

    norms = {k: w["norm_" + k] for k in ("pre_mix", "post_mix", "pre_ffn", "post_ffn")}
    sgu = (sgu_ln_g, sgu_ln_b, sgu_w[0], sgu_b[0])
    loss_blk, grad_x, d_norms, d_sgu, (psum, got) = _local_step(x[0], loss_target[0], norms, sgu, big, (pair, got))
    loss = lax.psum(loss_blk[0, 0], ("x", "y", "c"))

    grads, deltas, new_m, new_v = {}, {}, {}, {}
    keys_of = dict(BIG)
    small_g = _pack([d_norms["pre_mix"], d_norms["post_mix"], d_norms["pre_ffn"], d_norms["post_ffn"],
                     d_sgu[0], d_sgu[1], d_sgu[2][None], d_sgu[3][None]])
    parts = lax.empty((8,) + small_g.shape, F32)
    small_todo = [(1, 2, 4, 6), (3, 5, 7)]
    bufs, pending = {}, None
    for name in ("ffn_w2", "ffn_w1", "sb_w_in", "sb_w_out", "ab_w_out"):
        buf = lax.empty((len(keys_of[name]), 2) + psum[keys_of[name][0]].shape[1:], F32)
        for layer, key in enumerate(keys_of[name]):
            if pending is not None:
                buf, rws = _owner_sum(psum[key], got[key], buf, layer, f"sum_{key}", comms=[_Join([bufs[pending]])])
                bufs[pending], pending = rws[0][0], None
            elif small_todo:
                buf, rws = _owner_sum(psum[key], got[key], buf, layer, f"sum_{key}",
                                      comms=[_SmallGather(small_g, parts, small_todo.pop(0))])
                parts = rws[0][0]
            else:
                buf = _owner_sum(psum[key], got[key], buf, layer, f"sum_{key}")
        bufs[name], pending = buf, name
    assert not small_todo

    rws = _comm_call([_Join([bufs["ab_w_out"]]), _ChipScatter(psum["ab_w_in"], got["ab_w_in"], DIAG)], "tail_comm")
    bufs["ab_w_out"], got["ab_w_in"] = rws[0][0], rws[1][0]
    bufs["ab_w_in"] = _owner_sum(psum["ab_w_in"], got["ab_w_in"], lax.empty((1, 2) + psum["ab_w_in"].shape[1:], F32), 0, "sum_ab_w_in")
    bufs["ab_w_in"] = _comm_call([_Join([bufs["ab_w_in"]])], "join_last")[0][0]
    for name, _ in BIG:
        grads[name], deltas[name], new_m[name], new_v[name] = _adamw(w[name], bufs[name].reshape(w[name].shape), m[name], v[name], f"adamw_{name}")

    outs = _small_update(small_g, parts, _pack([w[k] for k in SMALL]), _pack([m[k] for k in SMALL]), _pack([v[k] for k in SMALL]), "small_update")
    like = [w[k] for k in SMALL]
    for dst, packed in zip((grads, deltas, new_m, new_v), outs):
        for k, a in zip(SMALL, _unpack(packed, like)):
            dst[k] = a

    return (loss, grad_x[None], *[grads[k] for k in WEIGHTS], *[deltas[k] for k in WEIGHTS],
            *[new_m[k] for k in WEIGHTS], *[new_v[k] for k in WEIGHTS])
```

```python
import functools

import jax
import jax.numpy as jnp
from jax import lax
from jax.experimental import pallas as pl
from jax.experimental.pallas import tpu as pltpu

F32 = jnp.float32
BF16 = jnp.bfloat16
MESH = pl.DeviceIdType.MESH

HEAD_DIM = 128
CHUNK = 128
DILATIONS = (1, 4, 16)
SB_BLOCK = 256
RMS_EPS = 1e-6
LN_EPS = 1e-5
ADAM_LR, ADAM_B1, ADAM_B2, ADAM_EPS, ADAM_WD, ADAM_STEP = 0.001, 0.9, 0.999, 1e-08, 0.01, 10
NEG = -1e30
V7X_VMEM_LIMIT = 48 * 1024 * 1024
ANY = pl.BlockSpec(memory_space=pl.ANY)


def _params(*sem):
    return pltpu.CompilerParams(dimension_semantics=sem if sem else None, vmem_limit_bytes=V7X_VMEM_LIMIT)


def _tile(n, pref):
    if n <= pref:
        return n
    t = pref
    while n % t:
        t -= 128
    return t


def _dot(a, b, dims):
    return lax.dot_general(a, b, (dims, ((), ())), preferred_element_type=F32)


NN = ((1,), (0,))
NT = ((1,), (1,))
TN = ((0,), (0,))


def _place():
    x, y, c = lax.axis_index("x"), lax.axis_index("y"), lax.axis_index("c")
    return x, y, c, 2 * x + y


def _flip(x, y, c, j):
    return (1 - x if j & 4 else x), (1 - y if j & 2 else y), (1 - c if j & 1 else c)


def _half_shape(full_shape, kind):
    rows, cols = full_shape
    return (rows // 2, cols // 4) if kind == "col" else (rows // 8, cols)


def _half(ref, kind, s, h):
    rh, cs = _half_shape(ref.shape, kind)
    if kind == "col":
        return ref.at[pl.ds(h * rh, rh), pl.ds(s * cs, cs)]
    return ref.at[pl.ds((2 * s + h) * rh, rh), :]


def _remote(src, dst, send, recv, to):
    return pltpu.make_async_remote_copy(src_ref=src, dst_ref=dst, send_sem=send, recv_sem=recv, device_id=to, device_id_type=MESH)


class _Gather:
    n_sems = 6

    def __init__(self, full, kind):
        self.ro, self.rw, self.kind = [], [full], kind

    def start(self, ro, rw, send, recv):
        x, y, c, mine = _place()
        own = _half(rw[0], self.kind, mine, c)
        for k, j in enumerate((2, 4, 6)):
            px, py, _ = _flip(x, y, c, j)
            _remote(own, own, send(k), recv(k), (px, py, c)).start()

    def finish(self, ro, rw, send, recv):
        x, y, c, mine = _place()
        own = _half(rw[0], self.kind, mine, c)
        for k, j in enumerate((2, 4, 6)):
            px, py, _ = _flip(x, y, c, j)
            got = _half(rw[0], self.kind, 2 * px + py, c)
            _remote(got, got, send(k), recv(k), (x, y, c)).wait_recv()
            _remote(got, got, send(3 + k), recv(3 + k), (x, y, 1 - c)).start()
        for k, j in enumerate((2, 4, 6)):
            px, py, _ = _flip(x, y, c, j)
            got = _half(rw[0], self.kind, 2 * px + py, 1 - c)
            _remote(got, got, send(3 + k), recv(3 + k), (x, y, c)).wait_recv()
        for k in range(6):
            _remote(own, own, send(k), recv(k), (x, y, c)).wait_send()


class _GatherSend:
    def __init__(self, full, kind, patterns, part=(0, 1)):
        self.ro, self.rw, self.kind, self.patterns, self.part, self.n_sems = [], [full], kind, patterns, part, len(patterns)

    def _rows(self, half):
        i, n = self.part
        rows = half.shape[0] // n
        return half.at[pl.ds(i * rows, rows), :]

    def start(self, ro, rw, send, recv):
        x, y, c, mine = _place()
        own = self._rows(_half(rw[0], self.kind, mine, c))
        for k, j in enumerate(self.patterns):
            px, py, _ = _flip(x, y, c, j)
            _remote(own, own, send(k), recv(k), (px, py, c)).start()

    def finish(self, ro, rw, send, recv):
        x, y, c, _ = _place()
        for k, j in enumerate(self.patterns):
            px, py, _ = _flip(x, y, c, j)
            got = self._rows(_half(rw[0], self.kind, 2 * px + py, c))
            cp = _remote(got, got, send(k), recv(k), (x, y, c))
            cp.wait_recv()
            cp.wait_send()


class _GatherFwd:
    def __init__(self, full, kind, patterns):
        self.ro, self.rw, self.kind, self.patterns, self.n_sems = [], [full], kind, patterns, len(patterns)

    def start(self, ro, rw, send, recv):
        x, y, c, _ = _place()
        for k, j in enumerate(self.patterns):
            px, py, _ = _flip(x, y, c, j)
            got = _half(rw[0], self.kind, 2 * px + py, c)
            _remote(got, got, send(k), recv(k), (x, y, 1 - c)).start()

    def finish(self, ro, rw, send, recv):
        x, y, c, _ = _place()
        for k, j in enumerate(self.patterns):
            px, py, _ = _flip(x, y, c, j)
            got = _half(rw[0], self.kind, 2 * px + py, 1 - c)
            cp = _remote(got, got, send(k), recv(k), (x, y, c))
            cp.wait_recv()
            cp.wait_send()


class _PairSwap:
    n_sems = 4

    def __init__(self, dw16, pair, kind):
        self.ro, self.rw, self.kind = [dw16], [pair], kind

    def start(self, ro, rw, send, recv):
        x, y, c, _ = _place()
        for s in range(4):
            _remote(_half(ro[0], self.kind, s, 1 - c), rw[0].at[s], send(s), recv(s), (x, y, 1 - c)).start()

    def finish(self, ro, rw, send, recv):
        x, y, c, _ = _place()
        for s in range(4):
            cp = _remote(rw[0].at[s], rw[0].at[s], send(s), recv(s), (x, y, c))
            cp.wait_recv()
            cp.wait_send()


class _ChipScatter:
    def __init__(self, psum, got, patterns, part=(0, 1)):
        self.ro, self.rw, self.patterns, self.part, self.n_sems = [psum], [got], patterns, part, len(patterns)

    def _rows(self, ref, slot):
        i, n = self.part
        rows = ref.shape[1] // n
        return ref.at[slot, pl.ds(i * rows, rows), :]

    def start(self, ro, rw, send, recv):
        x, y, c, _ = _place()
        for k, j in enumerate(self.patterns):
            px, py, _ = _flip(x, y, c, j)
            _remote(self._rows(ro[0], 2 * px + py), self._rows(rw[0], j // 2 - 1), send(k), recv(k), (px, py, c)).start()

    def finish(self, ro, rw, send, recv):
        x, y, c, _ = _place()
        for k, j in enumerate(self.patterns):
            slot = self._rows(rw[0], j // 2 - 1)
            cp = _remote(slot, slot, send(k), recv(k), (x, y, c))
            cp.wait_recv()
            cp.wait_send()


class _Join:
    def __init__(self, bufs):
        self.ro, self.rw, self.n_sems = [], list(bufs), sum(b.shape[0] for b in bufs)

    def _copies(self, rw, send, recv, slot):
        x, y, c, _ = _place()
        k = 0
        for ref in rw:
            for l in range(ref.shape[0]):
                yield _remote(ref.at[l, c], ref.at[l, slot(c)], send(k), recv(k), (x, y, 1 - c))
                k += 1

    def start(self, ro, rw, send, recv):
        for cp in self._copies(rw, send, recv, lambda c: c):
            cp.start()

    def finish(self, ro, rw, send, recv):
        for cp in self._copies(rw, send, recv, lambda c: 1 - c):
            cp.wait_recv()
        for cp in self._copies(rw, send, recv, lambda c: c):
            cp.wait_send()


def _comm_layout(comms):
    ro = [a for c in comms for a in c.ro]
    rw = [a for c in comms for a in c.rw]
    return ro, rw, sum(c.n_sems for c in comms)


def _comm_each(comms, method, ro_refs, rw_refs, send, recv):
    i_ro = i_rw = i_sem = 0
    for c in comms:
        getattr(c, method)(ro_refs[i_ro:i_ro + len(c.ro)], rw_refs[i_rw:i_rw + len(c.rw)],
                           lambda k, b=i_sem: send.at[b + k], lambda k, b=i_sem: recv.at[b + k])
        i_ro, i_rw, i_sem = i_ro + len(c.ro), i_rw + len(c.rw), i_sem + c.n_sems


def _split_results(comms, rws):
    out, i = [], 0
    for c in comms:
        out.append(list(rws[i:i + len(c.rw)]))
        i += len(c.rw)
    return out


def _comm_call(comms, name):
    ro, rw, n_sems = _comm_layout(comms)

    def body(*refs):
        ro_refs = refs[:len(ro)]
        rw_refs = refs[len(ro) + len(rw):len(ro) + 2 * len(rw)]
        send, recv = refs[len(ro) + 2 * len(rw):]
        _comm_each(comms, "start", ro_refs, rw_refs, send, recv)
        _comm_each(comms, "finish", ro_refs, rw_refs, send, recv)

    rws = pl.pallas_call(
        body, name=name, in_specs=[ANY] * (len(ro) + len(rw)), out_specs=[ANY] * len(rw),
        out_shape=[jax.ShapeDtypeStruct(a.shape, a.dtype) for a in rw],
        input_output_aliases={len(ro) + k: k for k in range(len(rw))},
        scratch_shapes=[pltpu.SemaphoreType.DMA((n_sems,)), pltpu.SemaphoreType.DMA((n_sems,))],
    )(*ro, *rw)
    return _split_results(comms, rws)


def _pcall(body, args, *, name, grid, in_specs, out_specs, out_shape, scratch=(), sem=(), comms=(), aliases=None):
    n_in, n_out, n_scr = len(in_specs), len(out_specs), len(scratch)
    aliases = dict(aliases or {})
    if not comms:
        return pl.pallas_call(body, name=name, grid=grid, in_specs=list(in_specs), out_specs=list(out_specs),
                              out_shape=list(out_shape), scratch_shapes=list(scratch), input_output_aliases=aliases,
                              compiler_params=_params(*sem))(*args)
    ro, rw, n_sems = _comm_layout(comms)

    def carrier(*refs):
        ins = refs[:n_in]
        ro_refs = refs[n_in:n_in + len(ro)]
        o0 = n_in + len(ro) + len(rw)
        outs = refs[o0:o0 + n_out]
        rw_refs = refs[o0 + n_out:o0 + n_out + len(rw)]
        s0 = o0 + n_out + len(rw)
        send, recv = refs[s0 + n_scr], refs[s0 + n_scr + 1]
        ids = [pl.program_id(a) for a in range(len(grid))]
        first = functools.reduce(jnp.logical_and, [i == 0 for i in ids])
        last = functools.reduce(jnp.logical_and, [i == g - 1 for i, g in zip(ids, grid)])

        @pl.when(first)
        def _():
            _comm_each(comms, "start", ro_refs, rw_refs, send, recv)

        body(*ins, *outs, *refs[s0:s0 + n_scr])

        @pl.when(last)
        def _():
            _comm_each(comms, "finish", ro_refs, rw_refs, send, recv)

    res = pl.pallas_call(
        carrier, name=name, grid=grid, in_specs=list(in_specs) + [ANY] * (len(ro) + len(rw)),
        out_specs=list(out_specs) + [ANY] * len(rw),
        out_shape=list(out_shape) + [jax.ShapeDtypeStruct(a.shape, a.dtype) for a in rw],
        input_output_aliases={**aliases, **{n_in + len(ro) + k: n_out + k for k in range(len(rw))}},
        scratch_shapes=list(scratch) + [pltpu.SemaphoreType.DMA((n_sems,)), pltpu.SemaphoreType.DMA((n_sems,))],
        compiler_params=_params(*["arbitrary"] * len(grid)),
    )(*args, *ro, *rw)
    return list(res[:n_out]), _split_results(comms, res[n_out:])


def _matmul(a, b, mode, out_dtype, name, a_square=False, relu_out=False, mul2=None, comms=()):
    if mode == "nn":
        (m, k), n = a.shape, b.shape[1]
    elif mode == "nt":
        (m, k), n = a.shape, b.shape[0]
    else:
        (k, m), n = a.shape, b.shape[1]
    tm, tn, tk = _tile(m, 1024), _tile(n, 2048 if out_dtype == BF16 else 1024), _tile(k, 2048)
    nk = k // tk
    dims = {"nn": NN, "nt": NT, "tn": TN}[mode]
    a_spec = pl.BlockSpec((tk, tm), lambda i, j, kk: (kk, i)) if mode == "tn" else pl.BlockSpec((tm, tk), lambda i, j, kk: (i, kk))
    b_spec = pl.BlockSpec((tn, tk), lambda i, j, kk: (j, kk)) if mode == "nt" else pl.BlockSpec((tk, tn), lambda i, j, kk: (kk, j))
    o_spec = pl.BlockSpec((tm, tn), lambda i, j, kk: (i, j))

    def body(a_ref, b_ref, *rest):
        m_ref = None if mul2 is None else rest[0]
        o_ref = rest[0 if mul2 is None else 1]
        kk = pl.program_id(2)

        def partial():
            av = a_ref[...]
            if a_square:
                av = av * av
            return _dot(av, b_ref[...], dims)

        def finish(r):
            if relu_out:
                r = jnp.maximum(r, 0.0)
            if mul2 is not None:
                r = r * (2.0 * m_ref[...].astype(F32))
            o_ref[...] = r.astype(out_dtype)

        if nk == 1:
            finish(partial())
            return
        acc_ref = rest[-1]

        @pl.when(kk == 0)
        def _():
            acc_ref[...] = partial()

        @pl.when(kk > 0)
        def _():
            acc_ref[...] += partial()

        @pl.when(kk == nk - 1)
        def _():
            finish(acc_ref[...])

    args = (a, b) if mul2 is None else (a, b, mul2)
    specs = [a_spec, b_spec] + ([] if mul2 is None else [o_spec])
    res = _pcall(body, args, name=name, grid=(m // tm, n // tn, nk), in_specs=specs, out_specs=[o_spec],
                 out_shape=[jax.ShapeDtypeStruct((m, n), out_dtype)], scratch=[pltpu.VMEM((tm, tn), F32)] if nk > 1 else [],
                 sem=("parallel", "parallel", "arbitrary"), comms=comms)
    return (res[0][0], res[1]) if comms else res[0]


NORM_ROWS = 256


def _rms(x, g):
    rstd = lax.rsqrt(jnp.mean(x * x, axis=-1, keepdims=True) + RMS_EPS)
    n = x * rstd
    return n * g, n, rstd


def _rms_bwd(n, rstd, g, dout):
    dn = dout * g
    return rstd * (dn - n * jnp.mean(dn * n, axis=-1, keepdims=True))


def _row_spec(d):
    return pl.BlockSpec((NORM_ROWS, d), lambda i: (i, 0))


def _vec_spec(d):
    return pl.BlockSpec((1, d), lambda i: (0, 0))


def _accumulate(ref, val):
    @pl.when(pl.program_id(0) == 0)
    def _():
        ref[...] = jnp.zeros_like(ref)

    ref[...] += val


def _rms_fwd(x, g, name):
    t, d = x.shape

    def body(x_ref, g_ref, h_ref):
        h_ref[...] = _rms(x_ref[...], g_ref[...])[0].astype(BF16)

    return pl.pallas_call(
        body, name=name, grid=(t // NORM_ROWS,), in_specs=[_row_spec(d), _vec_spec(d)], out_specs=_row_spec(d),
        out_shape=jax.ShapeDtypeStruct((t, d), BF16), compiler_params=_params("parallel"),
    )(x, g)


def _post_pre_fwd(y, g_post, x, g_pre, name, comms=()):
    t, d = x.shape

    def body(y_ref, gp_ref, x_ref, gn_ref, xn_ref, h_ref):
        xn = x_ref[...] + _rms(y_ref[...].astype(F32), gp_ref[...])[0]
        xn_ref[...] = xn
        h_ref[...] = _rms(xn, gn_ref[...])[0].astype(BF16)

    return _pcall(
        body, (y, g_post, x, g_pre), name=name, grid=(t // NORM_ROWS,),
        in_specs=[_row_spec(d), _vec_spec(d), _row_spec(d), _vec_spec(d)], out_specs=[_row_spec(d), _row_spec(d)],
        out_shape=[jax.ShapeDtypeStruct((t, d), F32), jax.ShapeDtypeStruct((t, d), BF16)], sem=("parallel",), comms=comms)


def _final_fwd_bwd(y, g_post, x, target, name):
    t, d = x.shape

    def body(y_ref, g_ref, x_ref, t_ref, loss_ref, dx_ref, dy_ref, dg_ref):
        g = g_ref[...]
        out, n, rstd = _rms(y_ref[...].astype(F32), g)
        e = x_ref[...] + out - t_ref[...]
        _accumulate(loss_ref, jnp.full(loss_ref.shape, 0.5 / d, F32) * jnp.sum(e * e))
        dx = e * (1.0 / d)
        dx_ref[...] = dx
        dy_ref[...] = _rms_bwd(n, rstd, g, dx).astype(BF16)
        _accumulate(dg_ref, jnp.sum(dx * n, axis=0, keepdims=True))

    return pl.pallas_call(
        body, name=name, grid=(t // NORM_ROWS,),
        in_specs=[_row_spec(d), _vec_spec(d), _row_spec(d), _row_spec(d)],
        out_specs=[pl.BlockSpec((8, 128), lambda i: (0, 0)), _row_spec(d), _row_spec(d), _vec_spec(d)],
        out_shape=[jax.ShapeDtypeStruct((8, 128), F32), jax.ShapeDtypeStruct((t, d), F32),
                   jax.ShapeDtypeStruct((t, d), BF16), jax.ShapeDtypeStruct((1, d), F32)],
        compiler_params=_params("arbitrary"),
    )(y, g_post, x, target)


def _pre_post_bwd(x, g_pre, dh, dx_in, y, g_post, name, comms=()):
    t, d = x.shape
    both = y is not None

    def body(x_ref, gp_ref, dh_ref, dxi_ref, *rest):
        if both:
            y_ref, gq_ref, dx_ref, dy_ref, dgp_ref, dgq_ref = rest
        else:
            dx_ref, dgp_ref = rest
        gp = gp_ref[...]
        _, n, rstd = _rms(x_ref[...], gp)
        dh_v = dh_ref[...].astype(F32)
        dx = dxi_ref[...] + _rms_bwd(n, rstd, gp, dh_v)
        dx_ref[...] = dx
        _accumulate(dgp_ref, jnp.sum(dh_v * n, axis=0, keepdims=True))
        if both:
            gq = gq_ref[...]
            _, ny, rstdy = _rms(y_ref[...].astype(F32), gq)
            dy_ref[...] = _rms_bwd(ny, rstdy, gq, dx).astype(BF16)
            _accumulate(dgq_ref, jnp.sum(dx * ny, axis=0, keepdims=True))

    in_specs = [_row_spec(d), _vec_spec(d), _row_spec(d), _row_spec(d)]
    args = [x, g_pre, dh, dx_in]
    if both:
        in_specs += [_row_spec(d), _vec_spec(d)]
        args += [y, g_post]
        out_specs = [_row_spec(d), _row_spec(d), _vec_spec(d), _vec_spec(d)]
        out_shape = [jax.ShapeDtypeStruct((t, d), F32), jax.ShapeDtypeStruct((t, d), BF16),
                     jax.ShapeDtypeStruct((1, d), F32), jax.ShapeDtypeStruct((1, d), F32)]
    else:
        out_specs = [_row_spec(d), _vec_spec(d)]
        out_shape = [jax.ShapeDtypeStruct((t, d), F32), jax.ShapeDtypeStruct((1, d), F32)]
    return _pcall(body, args, name=name, grid=(t // NORM_ROWS,), in_specs=in_specs, out_specs=out_specs, out_shape=out_shape,
                  sem=("arbitrary",), comms=comms)


def _gelu(x):
    return 0.5 * x * (1.0 + lax.erf(x * 0.7071067811865476))


def _gelu_grad(x):
    return 0.5 * (1.0 + lax.erf(x * 0.7071067811865476)) + x * jnp.exp(-0.5 * x * x) * 0.3989422804014327


def _layernorm(v, g, b):
    mu = jnp.mean(v, axis=-1, keepdims=True)
    vc = v - mu
    rs = lax.rsqrt(jnp.mean(vc * vc, axis=-1, keepdims=True) + LN_EPS)
    vhat = vc * rs
    return vhat * g + b, vhat, rs


def _tril_mask():
    return lax.broadcasted_iota(jnp.int32, (CHUNK, CHUNK), 0) >= lax.broadcasted_iota(jnp.int32, (CHUNK, CHUNK), 1)


def _sgu_fwd(z, ln_g, ln_b, w16, bias_b, name, comms=()):
    t = z.shape[0]
    groups = w16.shape[0]
    a = groups * CHUNK

    def body(u_ref, v_ref, g_ref, b_ref, w_ref, bb_ref, o_ref):
        u = _gelu(u_ref[...].astype(F32))
        vn = _layernorm(_gelu(v_ref[...].astype(F32)), g_ref[...], b_ref[...])[0].astype(BF16)
        tril = _tril_mask()
        for g in range(groups):
            sl = slice(g * CHUNK, (g + 1) * CHUNK)
            w = jnp.where(tril, w_ref[g], jnp.zeros((), BF16))
            mixed = _dot(w, vn[:, sl], NN) + bb_ref[g]
            o_ref[:, sl] = (u[:, sl] * mixed).astype(BF16)

    full3 = pl.BlockSpec((groups, CHUNK, CHUNK), lambda c: (0, 0, 0))
    res = _pcall(
        body, (z, z, ln_g, ln_b, w16, bias_b), name=name, grid=(t // CHUNK,),
        in_specs=[pl.BlockSpec((CHUNK, a), lambda c: (c, 0)), pl.BlockSpec((CHUNK, a), lambda c: (c, 1)),
                  _vec_spec(a), _vec_spec(a), full3, full3],
        out_specs=[pl.BlockSpec((CHUNK, a), lambda c: (c, 0))], out_shape=[jax.ShapeDtypeStruct((t, a), BF16)],
        sem=("parallel",), comms=comms)
    return (res[0][0], res[1]) if comms else res[0]


def _sgu_bwd(z, dab, ln_g, ln_b, w16, bias_b, name, comms=()):
    t = z.shape[0]
    groups = w16.shape[0]
    a = groups * CHUNK

    def body(u_ref, v_ref, da_ref, g_ref, b_ref, w_ref, bb_ref, duv_ref, dg_ref, db_ref, dw_ref, dbs_ref, dvn_ref):
        up = u_ref[...].astype(F32)
        vp = v_ref[...].astype(F32)
        u = _gelu(up)
        ln_gain = g_ref[...]
        vn32, vhat, rs = _layernorm(_gelu(vp), ln_gain, b_ref[...])
        vn = vn32.astype(BF16)
        da = da_ref[...].astype(F32)
        tril = _tril_mask()
        ones = jnp.ones((8, CHUNK), F32)

        @pl.when(pl.program_id(0) == 0)
        def _():
            dw_ref[...] = jnp.zeros_like(dw_ref)
            dbs_ref[...] = jnp.zeros_like(dbs_ref)

        for g in range(groups):
            sl = slice(g * CHUNK, (g + 1) * CHUNK)
            w = jnp.where(tril, w_ref[g], jnp.zeros((), BF16))
            mixed = _dot(w, vn[:, sl], NN) + bb_ref[g]
            dmix = da[:, sl] * u[:, sl]
            dmix16 = dmix.astype(BF16)
            duv_ref[:, sl] = (da[:, sl] * mixed * _gelu_grad(up[:, sl])).astype(BF16)
            dvn_ref[:, sl] = _dot(w, dmix16, TN)
            dw_ref[g] += jnp.where(tril, _dot(dmix16, vn[:, sl], NT), 0.0)
            dbs_ref[g:g + 1, :] += lax.dot_general(ones, dmix, (NT, ((), ())), precision=lax.Precision.HIGHEST,
                                                   preferred_element_type=F32)[0:1]
        dvn = dvn_ref[...]
        dvhat = dvn * ln_gain
        dva = rs * (dvhat - jnp.mean(dvhat, axis=-1, keepdims=True) - vhat * jnp.mean(dvhat * vhat, axis=-1, keepdims=True))
        duv_ref[:, a:] = (dva * _gelu_grad(vp)).astype(BF16)
        _accumulate(dg_ref, jnp.sum(dvn * vhat, axis=0, keepdims=True))
        _accumulate(db_ref, jnp.sum(dvn, axis=0, keepdims=True))

    full3 = pl.BlockSpec((groups, CHUNK, CHUNK), lambda c: (0, 0, 0))
    return _pcall(
        body, (z, z, dab, ln_g, ln_b, w16, bias_b), name=name, grid=(t // CHUNK,),
        in_specs=[pl.BlockSpec((CHUNK, a), lambda c: (c, 0)), pl.BlockSpec((CHUNK, a), lambda c: (c, 1)),
                  pl.BlockSpec((CHUNK, a), lambda c: (c, 0)), _vec_spec(a), _vec_spec(a), full3, full3],
        out_specs=[pl.BlockSpec((CHUNK, 2 * a), lambda c: (c, 0)), _vec_spec(a), _vec_spec(a), full3,
                   pl.BlockSpec((groups, CHUNK), lambda c: (0, 0))],
        out_shape=[jax.ShapeDtypeStruct((t, 2 * a), BF16), jax.ShapeDtypeStruct((1, a), F32), jax.ShapeDtypeStruct((1, a), F32),
                   jax.ShapeDtypeStruct((groups, CHUNK, CHUNK), F32), jax.ShapeDtypeStruct((groups, CHUNK), F32)],
        scratch=[pltpu.VMEM((CHUNK, a), F32)], sem=("arbitrary",), comms=comms)


def _dil_masks(d):
    qi = lax.broadcasted_iota(jnp.int32, (CHUNK, CHUNK), 0)
    kj = lax.broadcasted_iota(jnp.int32, (CHUNK, CHUNK), 1)
    dist_c = qi - kj
    return dist_c >= 0, dist_c <= 0, (dist_c * d).astype(F32), ((dist_c + CHUNK) * d).astype(F32)


def _alibi_slope(h, heads):
    return 2.0 ** (-8.0 * (h + 1) / heads)


def _dil_view(z, d):
    t, w = z.shape[0], z.shape[1] // 5
    if d == 1:
        return z, 5, 2
    return z[:, 2 * w:].reshape(t // d, d * 3 * w), 3, 0


def _dil_fwd(z, d, name, comms=()):
    t = z.shape[0]
    w = z.shape[1] // 5
    heads = w // HEAD_DIM
    nb = t // d // CHUNK
    scale = HEAD_DIM ** -0.5
    zv, mult, col_q = _dil_view(z, d)

    def body(q_ref, kp_ref, kc_ref, vp_ref, vc_ref, o_ref, l_ref):
        ok_c, ok_p0, bias_c, bias_p = _dil_masks(d)
        ok_p = ok_p0 & (pl.program_id(1) > 0)
        hs = range(heads)
        sl = [slice(h * HEAD_DIM, (h + 1) * HEAD_DIM) for h in hs]
        slope = [_alibi_slope(h, heads) for h in hs]
        ones = jnp.ones((CHUNK, HEAD_DIM), BF16)
        s_c = [_dot(q_ref[:, sl[h]], kc_ref[:, sl[h]], NT) for h in hs]
        s_p = [_dot(q_ref[:, sl[h]], kp_ref[:, sl[h]], NT) for h in hs]
        s_c = [jnp.where(ok_c, s_c[h] * scale - slope[h] * bias_c, NEG) for h in hs]
        s_p = [jnp.where(ok_p, s_p[h] * scale - slope[h] * bias_p, NEG) for h in hs]
        m = [jnp.max(jnp.maximum(s_c[h], s_p[h]), axis=1, keepdims=True) for h in hs]
        p_c = [jnp.exp(s_c[h] - m[h]).astype(BF16) for h in hs]
        p_p = [jnp.exp(s_p[h] - m[h]).astype(BF16) for h in hs]
        den = [_dot(p_c[h], ones, NN) + _dot(p_p[h], ones, NN) for h in hs]
        o = [_dot(p_c[h], vc_ref[:, sl[h]], NN) + _dot(p_p[h], vp_ref[:, sl[h]], NN) for h in hs]
        l_ref[...] = jnp.zeros_like(l_ref)
        for h in hs:
            o_ref[:, sl[h]] = (o[h] / den[h]).astype(BF16)
            l_ref[:, h:h + 1] = m[h] + jnp.log(den[h][:, 0:1])

    def zspec(col, prev):
        if prev:
            return pl.BlockSpec((CHUNK, w), lambda r, n: (jnp.maximum(n - 1, 0), r * mult + col_q + col))
        return pl.BlockSpec((CHUNK, w), lambda r, n: (n, r * mult + col_q + col))

    res = _pcall(
        body, (zv, zv, zv, zv, zv), name=name, grid=(d, nb),
        in_specs=[zspec(0, False), zspec(1, True), zspec(1, False), zspec(2, True), zspec(2, False)],
        out_specs=[pl.BlockSpec((CHUNK, w), lambda r, n: (n, r)), pl.BlockSpec((CHUNK, HEAD_DIM), lambda r, n: (n, r))],
        out_shape=[jax.ShapeDtypeStruct((t // d, d * w), BF16), jax.ShapeDtypeStruct((t // d, d * HEAD_DIM), F32)],
        sem=("parallel", "parallel"), comms=comms)
    (o, lse), rws = res if comms else (res, None)
    outs = (o.reshape(t, w), lse.reshape(t, HEAD_DIM))
    return (outs, rws) if comms else outs


def _dil_merge(a_out, outs, lses, name, comms=()):
    t, a = a_out.shape
    w = outs[0].shape[1]
    heads = w // HEAD_DIM
    nbr = len(outs)

    def body(a_ref, *rest):
        o_refs, l_refs, (ab_ref, lt_ref) = rest[:nbr], rest[nbr:2 * nbr], rest[2 * nbr:]
        ls = [r[...] for r in l_refs]
        m = functools.reduce(jnp.maximum, ls)
        ws = [jnp.exp(l - m) for l in ls]
        tot = functools.reduce(jnp.add, ws)
        ws = [wt / tot for wt in ws]
        ab_ref[:, :a] = a_ref[...]
        for h in range(heads):
            sl = slice(h * HEAD_DIM, (h + 1) * HEAD_DIM)
            mix = functools.reduce(jnp.add, [wt[:, h:h + 1] * r[:, sl].astype(F32) for wt, r in zip(ws, o_refs)])
            ab_ref[:, a + h * HEAD_DIM:a + (h + 1) * HEAD_DIM] = mix.astype(BF16)
        lt_ref[...] = m + jnp.log(tot)

    return _pcall(
        body, (a_out, *outs, *lses), name=name, grid=(t // NORM_ROWS,),
        in_specs=[_row_spec(a)] + [_row_spec(w)] * nbr + [_row_spec(HEAD_DIM)] * nbr,
        out_specs=[_row_spec(a + w), _row_spec(HEAD_DIM)],
        out_shape=[jax.ShapeDtypeStruct((t, a + w), BF16), jax.ShapeDtypeStruct((t, HEAD_DIM), F32)],
        sem=("parallel",), comms=comms)


def _dil_delta(ab, dab, name):
    t, aw = ab.shape
    w = aw // 2
    heads = w // HEAD_DIM

    def body(o_ref, do_ref, dl_ref):
        dl_ref[...] = jnp.zeros_like(dl_ref)
        for h in range(heads):
            sl = slice(h * HEAD_DIM, (h + 1) * HEAD_DIM)
            dl_ref[:, h:h + 1] = jnp.sum(do_ref[:, sl].astype(F32) * o_ref[:, sl].astype(F32), axis=1, keepdims=True)

    half = pl.BlockSpec((NORM_ROWS, w), lambda i: (i, 1))
    return pl.pallas_call(body, name=name, grid=(t // NORM_ROWS,), in_specs=[half, half], out_specs=_row_spec(HEAD_DIM),
                          out_shape=jax.ShapeDtypeStruct((t, HEAD_DIM), F32), compiler_params=_params("parallel"))(ab, dab)


def _dil_bwd(z, dab, ltot, delta, d, name, comms=()):
    t = z.shape[0]
    w = z.shape[1] // 5
    heads = w // HEAD_DIM
    nb = t // d // CHUNK
    scale = HEAD_DIM ** -0.5

    def body(q_ref, qn_ref, kp_ref, kc_ref, vp_ref, vc_ref, do_ref, don_ref, l_ref, ln_ref, dl_ref, dln_ref,
             dq_ref, dk_ref, dv_ref):
        n = pl.program_id(1)
        ok_c, ok_p0, bias_c, bias_p = _dil_masks(d)
        ok_p = ok_p0 & (n > 0)
        ok_n = ok_p0 & (n < nb - 1)
        hs = range(heads)
        sl = [slice(h * HEAD_DIM, (h + 1) * HEAD_DIM) for h in hs]
        slope = [_alibi_slope(h, heads) for h in hs]
        q, qn = [q_ref[:, s] for s in sl], [qn_ref[:, s] for s in sl]
        kp, kc = [kp_ref[:, s] for s in sl], [kc_ref[:, s] for s in sl]
        vp, vc = [vp_ref[:, s] for s in sl], [vc_ref[:, s] for s in sl]
        do, don = [do_ref[:, s] for s in sl], [don_ref[:, s] for s in sl]
        s_c = [_dot(q[h], kc[h], NT) for h in hs]
        s_p = [_dot(q[h], kp[h], NT) for h in hs]
        s_n = [_dot(qn[h], kc[h], NT) for h in hs]
        dp_c = [_dot(do[h], vc[h], NT) for h in hs]
        dp_p = [_dot(do[h], vp[h], NT) for h in hs]
        dp_n = [_dot(don[h], vc[h], NT) for h in hs]
        delta = [dl_ref[:, h:h + 1] for h in hs]
        delta_n = [dln_ref[:, h:h + 1] for h in hs]
        p_c = [jnp.exp(jnp.where(ok_c, s_c[h] * scale - slope[h] * bias_c, NEG) - l_ref[:, h:h + 1]) for h in hs]
        p_p = [jnp.exp(jnp.where(ok_p, s_p[h] * scale - slope[h] * bias_p, NEG) - l_ref[:, h:h + 1]) for h in hs]
        p_n = [jnp.exp(jnp.where(ok_n, s_n[h] * scale - slope[h] * bias_p, NEG) - ln_ref[:, h:h + 1]) for h in hs]
        ds_c = [(p_c[h] * (dp_c[h] - delta[h])).astype(BF16) for h in hs]
        ds_p = [(p_p[h] * (dp_p[h] - delta[h])).astype(BF16) for h in hs]
        ds_n = [(p_n[h] * (dp_n[h] - delta_n[h])).astype(BF16) for h in hs]
        dq = [_dot(ds_c[h], kc[h], NN) + _dot(ds_p[h], kp[h], NN) for h in hs]
        dk = [_dot(ds_c[h], q[h], TN) + _dot(ds_n[h], qn[h], TN) for h in hs]
        dv = [_dot(p_c[h].astype(BF16), do[h], TN) + _dot(p_n[h].astype(BF16), don[h], TN) for h in hs]
        for h in hs:
            dq_ref[:, sl[h]] = (dq[h] * scale).astype(BF16)
            dk_ref[:, sl[h]] = (dk[h] * scale).astype(BF16)
            dv_ref[:, sl[h]] = dv[h].astype(BF16)

    def spec(mult, col, shift, width=w):
        if shift < 0:
            return pl.BlockSpec((CHUNK, width), lambda r, n: (jnp.maximum(n - 1, 0), r * mult + col))
        if shift > 0:
            return pl.BlockSpec((CHUNK, width), lambda r, n: (jnp.minimum(n + 1, nb - 1), r * mult + col))
        return pl.BlockSpec((CHUNK, width), lambda r, n: (n, r * mult + col))

    zv, mult, cq = _dil_view(z, d)
    dov = dab[:, w:].reshape(t // d, d * w)
    lv = ltot.reshape(t // d, d * HEAD_DIM)
    dlv = delta.reshape(t // d, d * HEAD_DIM)
    ospec = spec(1, 0, 0)
    res = _pcall(
        body, (zv, zv, zv, zv, zv, zv, dov, dov, lv, lv, dlv, dlv), name=name, grid=(d, nb),
        in_specs=[spec(mult, cq, 0), spec(mult, cq, 1), spec(mult, cq + 1, -1), spec(mult, cq + 1, 0),
                  spec(mult, cq + 2, -1), spec(mult, cq + 2, 0), spec(1, 0, 0), spec(1, 0, 1),
                  spec(1, 0, 0, HEAD_DIM), spec(1, 0, 1, HEAD_DIM), spec(1, 0, 0, HEAD_DIM), spec(1, 0, 1, HEAD_DIM)],
        out_specs=[ospec, ospec, ospec], out_shape=[jax.ShapeDtypeStruct((t // d, d * w), BF16)] * 3,
        sem=("parallel", "parallel"), comms=comms)
    outs, rws = res if comms else (res, None)
    outs = [o.reshape(t, w) for o in outs]
    return (outs, rws) if comms else outs


def _dz_assemble(duv, parts, name):
    t, a2 = duv.shape
    w = parts[0][0].shape[1]
    nbr = len(parts)

    def body(duv_ref, *rest):
        refs, dz_ref = rest[:-1], rest[-1]
        dz_ref[:, :a2] = duv_ref[...]
        for i in range(3):
            tot = functools.reduce(jnp.add, [refs[b * 3 + i][...].astype(F32) for b in range(nbr)])
            dz_ref[:, a2 + i * w:a2 + (i + 1) * w] = tot.astype(BF16)

    flat = [p for branch in parts for p in branch]
    return pl.pallas_call(
        body, name=name, grid=(t // NORM_ROWS,), in_specs=[_row_spec(a2)] + [_row_spec(w)] * len(flat),
        out_specs=_row_spec(a2 + 3 * w), out_shape=jax.ShapeDtypeStruct((t, a2 + 3 * w), BF16),
        compiler_params=_params("parallel"),
    )(duv, *flat)


def _split_dot(x, m16):
    hi = x.astype(BF16)
    lo = (x - hi.astype(F32)).astype(BF16)
    return _dot(hi, m16, NN) + _dot(lo, m16, NN)


SB_DEAD = -110.0


def _sb_scaled(q):
    return (q.astype(F32) * (HEAD_DIM ** -0.5)).astype(BF16)


SB_GROUP = 4
SB_PAIR = 2


def _sb_logs(qs, kj, below):
    zt = [_dot(q, k, NT) for q, k in zip(qs, kj)]
    sp = [jnp.maximum(z, 0.0) + jnp.log(1.0 + jnp.exp(-jnp.abs(z))) for z in zt]
    return [z - s for z, s in zip(zt, sp)], [(-s if below is None else jnp.where(below, -s, 0.0)) for s in sp]


def _sb_alive(s, i, c_run):
    return (s <= i) & (jnp.max(c_run) > SB_DEAD)


def _sb_fwd(zc, name, comms=()):
    t = zc.shape[0]
    c = zc.shape[1] // 3
    heads = c // HEAD_DIM
    blk = min(SB_BLOCK, t)
    grp = SB_GROUP if heads % SB_GROUP == 0 else SB_PAIR

    def body(q_ref, k_ref, v_ref, o_ref, ct_ref, nb_ref):
        i = pl.program_id(1)
        sl = [slice(p * HEAD_DIM, (p + 1) * HEAD_DIM) for p in range(grp)]
        qs = [_sb_scaled(q_ref[:, s]) for s in sl]
        rows = lax.broadcasted_iota(jnp.int32, (blk, blk), 0)
        cols = lax.broadcasted_iota(jnp.int32, (blk, blk), 1)
        below = rows > cols
        m_right = below.astype(BF16)

        def tile(carry, diagonal):
            s, acc, c_run = carry[0], carry[1:1 + grp], carry[1 + grp:]
            off = pl.multiple_of((i - s) * blk, blk)
            log_beta, l = _sb_logs(qs, [k_ref[pl.ds(off, blk), p] for p in sl], below if diagonal else None)
            right = [_split_dot(x, m_right) for x in l]
            a = [jnp.exp(lb + (c + r)) for lb, c, r in zip(log_beta, c_run, right)]
            if diagonal:
                a = [jnp.where(below, x, 0.0) for x in a]
            acc = [o + _dot(x.astype(BF16), v_ref[pl.ds(off, blk), p], NN) for o, x, p in zip(acc, a, sl)]
            return (s + 1, *acc, *[c + jnp.sum(x, axis=1, keepdims=True) for c, x in zip(c_run, l)])

        zeros = [jnp.zeros((blk, HEAD_DIM), F32)] * grp + [jnp.zeros((blk, 1), F32)] * grp
        out = lax.while_loop(lambda carry: _sb_alive(carry[0], i, functools.reduce(jnp.maximum, carry[1 + grp:])),
                             lambda carry: tile(carry, False), tile((jnp.int32(0), *zeros), True))
        for p, s in enumerate(sl):
            o_ref[:, s] = out[1 + p].astype(BF16)
            ct_ref[:, s] = jnp.broadcast_to(out[1 + grp + p], (blk, HEAD_DIM))
        nb_ref[...] = jnp.zeros(nb_ref.shape, F32) + out[0].astype(F32)

    groups = heads // grp
    qspec = pl.BlockSpec((blk, grp * HEAD_DIM), lambda h, i: (i, h))
    return _pcall(body, (zc, zc, zc), name=name, grid=(groups, t // blk),
                  in_specs=[qspec, pl.BlockSpec((t, grp * HEAD_DIM), lambda h, i: (0, groups + h)),
                            pl.BlockSpec((t, grp * HEAD_DIM), lambda h, i: (0, 2 * groups + h))],
                  out_specs=[qspec, qspec, qspec],
                  out_shape=[jax.ShapeDtypeStruct((t, c), BF16), jax.ShapeDtypeStruct((t, c), F32), jax.ShapeDtypeStruct((t, c), F32)],
                  sem=("parallel", "parallel"), comms=comms)


def _sb_bwd(zc, ctot, swept, do, name, comms=()):
    t = zc.shape[0]
    c = zc.shape[1] // 3
    heads = c // HEAD_DIM
    blk = min(SB_BLOCK, t)
    scale = HEAD_DIM ** -0.5
    grp = SB_GROUP if heads % SB_GROUP == 0 else SB_PAIR

    def body(q_ref, k_ref, v_ref, ct_ref, nb_ref, do_ref, dq_ref, dk_ref, dv_ref):
        i = pl.program_id(1)

        @pl.when(i == 0)
        def _():
            dk_ref[...] = jnp.zeros_like(dk_ref)
            dv_ref[...] = jnp.zeros_like(dv_ref)

        ps = range(grp)
        sl = [slice(p * HEAD_DIM, (p + 1) * HEAD_DIM) for p in ps]
        qs = [_sb_scaled(q_ref[:, s]) for s in sl]
        dov = [do_ref[:, s] for s in sl]
        c_tot = [ct_ref[:, p * HEAD_DIM:p * HEAD_DIM + 1] for p in ps]
        n_blocks = jnp.clip(jnp.max(nb_ref[0:8, :]).astype(jnp.int32), 1, i + 1)
        rows = lax.broadcasted_iota(jnp.int32, (blk, blk), 0)
        cols = lax.broadcasted_iota(jnp.int32, (blk, blk), 1)
        below = rows > cols
        m_upto = (rows <= cols).astype(BF16)
        m_left = (rows < cols).astype(BF16)

        def tile(j, carry, diagonal):
            dq, l_run, w_run = carry[:grp], carry[grp:2 * grp], carry[2 * grp:]
            off = pl.multiple_of(j * blk, blk)
            kj = [k_ref[pl.ds(off, blk), s] for s in sl]
            vj = [v_ref[pl.ds(off, blk), s] for s in sl]
            log_beta, l = _sb_logs(qs, kj, below if diagonal else None)
            d_a = [_dot(dov[p], vj[p], NT) for p in ps]
            upto = [_split_dot(x, m_upto) for x in l]
            a = [jnp.exp(log_beta[p] + (c_tot[p] - l_run[p] - upto[p])) for p in ps]
            if diagonal:
                a = [jnp.where(below, x, 0.0) for x in a]
            wgt = [a[p] * d_a[p] for p in ps]
            before = [w_run[p] + _split_dot(wgt[p], m_left) for p in ps]
            dz = [wgt[p] * jnp.exp(l[p]) - jnp.exp(log_beta[p]) * before[p] for p in ps]
            if diagonal:
                dz = [jnp.where(below, x, 0.0) for x in dz]
            dz16 = [x.astype(BF16) for x in dz]
            dk = [_dot(dz16[p], qs[p], TN) for p in ps]
            dv = [_dot(a[p].astype(BF16), dov[p], TN) for p in ps]
            dq = [dq[p] + _dot(dz16[p], kj[p], NN) for p in ps]
            for p in ps:
                dk_ref[pl.ds(off, blk), sl[p]] += dk[p]
                dv_ref[pl.ds(off, blk), sl[p]] += dv[p]
            return (*dq, *[l_run[p] + jnp.sum(l[p], axis=1, keepdims=True) for p in ps],
                    *[w_run[p] + jnp.sum(wgt[p], axis=1, keepdims=True) for p in ps])

        zeros = [jnp.zeros((blk, HEAD_DIM), F32)] * grp + [jnp.zeros((blk, 1), F32)] * (2 * grp)
        carry = lax.fori_loop(i + 1 - n_blocks, i, lambda j, carry: tile(j, carry, False), tuple(zeros))
        out = tile(i, carry, True)
        for p in ps:
            dq_ref[:, sl[p]] = (out[p] * scale).astype(BF16)

    groups = heads // grp
    qspec = pl.BlockSpec((blk, grp * HEAD_DIM), lambda h, i: (i, h))
    once = dict(pipeline_mode=pl.Buffered(1))
    full = pl.BlockSpec((t, grp * HEAD_DIM), lambda h, i: (0, h), **once)
    return _pcall(body, (zc, zc, zc, ctot, swept, do), name=name, grid=(groups, t // blk),
                  in_specs=[qspec, pl.BlockSpec((t, grp * HEAD_DIM), lambda h, i: (0, groups + h), **once),
                            pl.BlockSpec((t, grp * HEAD_DIM), lambda h, i: (0, 2 * groups + h), **once), qspec, qspec, qspec],
                  out_specs=[qspec, full, full],
                  out_shape=[jax.ShapeDtypeStruct((t, c), BF16), jax.ShapeDtypeStruct((t, c), F32), jax.ShapeDtypeStruct((t, c), F32)],
                  sem=("arbitrary", "arbitrary"), comms=comms)


def _concat_bf16(parts, name, comms=()):
    t, c = parts[0].shape

    def body(*refs):
        for k, r in enumerate(refs[:-1]):
            refs[-1][:, k * c:(k + 1) * c] = r[...].astype(BF16)

    res = _pcall(body, tuple(parts), name=name, grid=(t // NORM_ROWS,), in_specs=[_row_spec(c)] * len(parts),
                 out_specs=[_row_spec(c * len(parts))], out_shape=[jax.ShapeDtypeStruct((t, c * len(parts)), BF16)],
                 sem=("parallel",), comms=comms)
    return (res[0][0], res[1]) if comms else res[0]


KIND = {"ab_w_in": "col", "ab_w_out": "row", "sb_w_in": "col", "sb_w_out": "row",
        "ffn_w1_0": "col", "ffn_w1_1": "col", "ffn_w2_0": "row", "ffn_w2_1": "row"}
X_Y, DIAG, CHIPS = (2, 4), (6,), (2, 4, 6)


def _local_step(x, target, norms, sgu, big, bufs=None):
    g = {k: [v[l:l + 1] for l in range(2)] for k, v in norms.items()}
    ln_g, ln_b, sgu_w, sgu_b = sgu
    groups = sgu_w.shape[0]
    w16 = sgu_w.astype(BF16)
    bias_b = jnp.broadcast_to(sgu_b[:, :, None], (groups, CHUNK, CHUNK))
    big, dws, psum, dist = dict(big), {}, {}, bufs is not None
    pair, got = (dict(bufs[0]), dict(bufs[1])) if dist else ({}, {})

    def run(fn, *args, ops=(), **kw):
        if not dist or not ops:
            return fn(*args, **kw)
        make = {"gs": lambda k, p, *part: _GatherSend(big[k], KIND[k], p, *part), "gf": lambda k, p: _GatherFwd(big[k], KIND[k], p),
                "swap": lambda k, p: _PairSwap(dws[k], pair[k], KIND[k]),
                "chips": lambda k, p, *part: _ChipScatter(psum[k], got[k], p, *part)}
        out, rws = fn(*args, comms=[make[op[0]](*op[1:]) for op in ops], **kw)
        for (op, k, *_), r in zip(ops, rws):
            if op in ("gs", "gf"):
                big[k] = r[0]
            elif op == "swap":
                psum[k] = _pair_sum(dws[k], r[0], KIND[k], f"pair_sum_{k}")
            else:
                got[k] = r[0]
        return out

    h1_0 = _rms_fwd(x, g["pre_mix"][0], "rms_in")
    z0 = run(_matmul, h1_0, big["ab_w_in"], "nn", BF16, "ab_in", ops=[("gs", "ffn_w1_0", X_Y)])
    a_out = run(_sgu_fwd, z0, ln_g, ln_b, w16, bias_b, "sgu_fwd", ops=[("gf", "ffn_w1_0", X_Y), ("gs", "ab_w_out", CHIPS)])
    branches = [run(_dil_fwd, z0, 1, "dil_fwd_1", ops=[("gs", "ffn_w1_0", DIAG, (0, 2)), ("gf", "ab_w_out", CHIPS)]),
                run(_dil_fwd, z0, 4, "dil_fwd_4", ops=[("gs", "ffn_w1_0", DIAG, (1, 2))]),
                run(_dil_fwd, z0, 16, "dil_fwd_16", ops=[("gf", "ffn_w1_0", DIAG), ("gs", "ffn_w2_0", X_Y, (0, 2))])]
    ab, ltot = run(_dil_merge, a_out, [b[0] for b in branches], [b[1] for b in branches], "dil_merge",
                   ops=[("gs", "ffn_w2_0", X_Y, (1, 2))])
    y_0 = run(_matmul, ab, big["ab_w_out"], "nn", BF16, "ab_out", ops=[("gs", "ffn_w2_0", DIAG, (0, 2))])
    x1, h2_0 = run(_post_pre_fwd, y_0, g["post_mix"][0], x, g["pre_ffn"][0], "norm_mix0", ops=[("gs", "ffn_w2_0", DIAG, (1, 2))])
    r_0 = run(_matmul, h2_0, big["ffn_w1_0"], "nn", BF16, "ffn_up_0", relu_out=True,
              ops=[("gf", "ffn_w2_0", CHIPS), ("gs", "sb_w_in", CHIPS)])
    y2_0 = run(_matmul, r_0, big["ffn_w2_0"], "nn", BF16, "ffn_down_0", a_square=True,
               ops=[("gf", "sb_w_in", CHIPS), ("gs", "sb_w_out", CHIPS), ("gs", "ffn_w1_1", X_Y)])
    x2, h1_1 = run(_post_pre_fwd, y2_0, g["post_ffn"][0], x1, g["pre_mix"][1], "norm_ffn0",
                   ops=[("gf", "ffn_w1_1", X_Y), ("gf", "sb_w_out", CHIPS)])
    zc = run(_matmul, h1_1, big["sb_w_in"], "nn", BF16, "sb_in", ops=[("gs", "ffn_w1_1", DIAG)])
    o_sb, ct_sb, nb_sb = run(_sb_fwd, zc, "sb_fwd", ops=[("gf", "ffn_w1_1", DIAG), ("gs", "ffn_w2_1", CHIPS)])
    y_1 = run(_matmul, o_sb, big["sb_w_out"], "nn", BF16, "sb_out", ops=[("gf", "ffn_w2_1", CHIPS)])
    x3, h2_1 = _post_pre_fwd(y_1, g["post_mix"][1], x2, g["pre_ffn"][1], "norm_mix1")
    r_1 = _matmul(h2_1, big["ffn_w1_1"], "nn", BF16, "ffn_up_1", relu_out=True)
    y2_1 = _matmul(r_1, big["ffn_w2_1"], "nn", BF16, "ffn_down_1", a_square=True)
    loss, dx4, dy2_1, dg_post_ffn1 = _final_fwd_bwd(y2_1, g["post_ffn"][1], x3, target, "loss")

    da = _matmul(dy2_1, big["ffn_w2_1"], "nt", BF16, "ffn_da_1", mul2=r_1)
    dws["ffn_w2_1"] = _matmul(r_1, dy2_1, "tn", BF16, "ffn_dw2_1", a_square=True)
    dh2 = run(_matmul, da, big["ffn_w1_1"], "nt", BF16, "ffn_dh_1", ops=[("swap", "ffn_w2_1", None)])
    dws["ffn_w1_1"] = run(_matmul, h2_1, da, "tn", BF16, "ffn_dw1_1", ops=[("chips", "ffn_w2_1", X_Y)])
    dx3, dy_1, dg_pre_ffn1, dg_post_mix1 = run(_pre_post_bwd, x3, g["pre_ffn"][1], dh2, dx4, y_1, g["post_mix"][1], "norm_bwd_mix1",
                                               ops=[("swap", "ffn_w1_1", None)])
    do_sb = _matmul(dy_1, big["sb_w_out"], "nt", BF16, "sb_out_dx")
    dws["sb_w_out"] = _matmul(o_sb, dy_1, "tn", BF16, "sb_out_dw")
    dqkv = run(_sb_bwd, zc, ct_sb, nb_sb, do_sb, "sb_bwd",
               ops=[("chips", "ffn_w2_1", DIAG), ("chips", "ffn_w1_1", CHIPS), ("swap", "sb_w_out", None)])
    dzc = run(_concat_bf16, dqkv, "sb_dz", ops=[("chips", "sb_w_out", X_Y)])
    dh1 = run(_matmul, dzc, big["sb_w_in"], "nt", BF16, "sb_in_dx", ops=[("chips", "sb_w_out", DIAG)])
    dws["sb_w_in"] = _matmul(h1_1, dzc, "tn", BF16, "sb_in_dw")
    dx2, dy2_0, dg_pre_mix1, dg_post_ffn0 = run(_pre_post_bwd, x2, g["pre_mix"][1], dh1, dx3, y2_0, g["post_ffn"][0], "norm_bwd_ffn0",
                                                ops=[("swap", "sb_w_in", None)])
    da = run(_matmul, dy2_0, big["ffn_w2_0"], "nt", BF16, "ffn_da_0", mul2=r_0, ops=[("chips", "sb_w_in", X_Y)])
    dws["ffn_w2_0"] = run(_matmul, r_0, dy2_0, "tn", BF16, "ffn_dw2_0", a_square=True, ops=[("chips", "sb_w_in", DIAG)])
    dws["ffn_w1_0"] = run(_matmul, h2_0, da, "tn", BF16, "ffn_dw1_0", ops=[("swap", "ffn_w2_0", None)])
    dh2 = run(_matmul, da, big["ffn_w1_0"], "nt", BF16, "ffn_dh_0", ops=[("chips", "ffn_w2_0", X_Y), ("swap", "ffn_w1_0", None)])
    dx1, dy_0, dg_pre_ffn0, dg_post_mix0 = run(_pre_post_bwd, x1, g["pre_ffn"][0], dh2, dx2, y_0, g["post_mix"][0], "norm_bwd_mix0",
                                               ops=[("chips", "ffn_w2_0", DIAG, (0, 2))])
    dab = run(_matmul, dy_0, big["ab_w_out"], "nt", BF16, "ab_out_dx", ops=[("chips", "ffn_w2_0", DIAG, (1, 2))])
    dws["ab_w_out"] = run(_matmul, ab, dy_0, "tn", BF16, "ab_out_dw", ops=[("chips", "ffn_w1_0", X_Y, (0, 2))])
    duv, d_ln_g, d_ln_b, d_sgu_w, d_sgu_b = run(_sgu_bwd, z0, dab, ln_g, ln_b, w16, bias_b, "sgu_bwd",
                                                ops=[("chips", "ffn_w1_0", X_Y, (1, 2))])
    delta = _dil_delta(ab, dab, "dil_delta")
    parts = [run(_dil_bwd, z0, dab, ltot, delta, 1, "dil_bwd_1", ops=[("chips", "ffn_w1_0", DIAG, (0, 2)), ("swap", "ab_w_out", None)]),
             run(_dil_bwd, z0, dab, ltot, delta, 4, "dil_bwd_4", ops=[("chips", "ffn_w1_0", DIAG, (1, 2))]),
             run(_dil_bwd, z0, dab, ltot, delta, 16, "dil_bwd_16", ops=[("chips", "ab_w_out", CHIPS)])]
    dz0 = _dz_assemble(duv, parts, "dz_assemble")
    dws["ab_w_in"] = _matmul(h1_0, dz0, "tn", BF16, "ab_in_dw")
    dh1 = run(_matmul, dz0, big["ab_w_in"], "nt", BF16, "ab_in_dx", ops=[("swap", "ab_w_in", None)])
    grad_x, dg_pre_mix0 = run(_pre_post_bwd, x, g["pre_mix"][0], dh1, dx1, None, None, "norm_bwd_in", ops=[("chips", "ab_w_in", X_Y)])

    d_norms = {
        "pre_mix": jnp.concatenate([dg_pre_mix0, dg_pre_mix1]), "post_mix": jnp.concatenate([dg_post_mix0, dg_post_mix1]),
        "pre_ffn": jnp.concatenate([dg_pre_ffn0, dg_pre_ffn1]), "post_ffn": jnp.concatenate([dg_post_ffn0, dg_post_ffn1]),
    }
    return loss, grad_x, d_norms, (d_ln_g, d_ln_b, d_sgu_w, d_sgu_b), (psum, got) if dist else dws


def _to_bf16_full(w, layer, kind, name):
    _, rows, cols = w.shape
    tr = _tile(rows, 512)
    nblk = rows // tr
    full = (rows, 4 * cols) if kind == "col" else (4 * rows, cols)

    def body(w_ref, o_ref):
        o_ref[...] = w_ref[...].astype(BF16)

    def place(i):
        mine = 2 * lax.axis_index("x") + lax.axis_index("y")
        return (i, mine) if kind == "col" else (mine * nblk + i, 0)

    return pl.pallas_call(
        body, name=name, grid=(nblk,), in_specs=[pl.BlockSpec((None, tr, cols), lambda i: (layer, i, 0))],
        out_specs=pl.BlockSpec((tr, cols), place), out_shape=jax.ShapeDtypeStruct(full, BF16), compiler_params=_params("parallel"),
    )(w)


def _pair_sum(dw16, pair, kind, name):
    rh, cs = _half_shape(dw16.shape, kind)
    tr = _tile(rh, 512)
    nblk = rh // tr

    def body(dw_ref, pair_ref, o_ref):
        o_ref[...] = (dw_ref[...].astype(F32) + pair_ref[...].astype(F32)).astype(BF16)

    def own(s, i):
        c = lax.axis_index("c")
        return (c * nblk + i, s) if kind == "col" else ((2 * s + c) * nblk + i, 0)

    spec3 = pl.BlockSpec((None, tr, cs), lambda s, i: (s, i, 0))
    return pl.pallas_call(
        body, name=name, grid=(4, nblk), in_specs=[pl.BlockSpec((tr, cs), own), spec3], out_specs=spec3,
        out_shape=jax.ShapeDtypeStruct((4, rh, cs), BF16), compiler_params=_params("parallel", "parallel"),
    )(dw16, pair)


def _owner_sum(psum, got, buf, layer, name, comms=()):
    _, rh, cs = psum.shape
    tr = _tile(rh, 512)

    def body(p_ref, got_ref, buf_ref, o_ref):
        tot = p_ref[...].astype(F32)
        for j in range(3):
            tot = tot + got_ref[j].astype(F32)
        o_ref[...] = tot

    res = _pcall(
        body, (psum, got, buf), name=name, grid=(rh // tr,),
        in_specs=[pl.BlockSpec((None, tr, cs), lambda i: (2 * lax.axis_index("x") + lax.axis_index("y"), i, 0)),
                  pl.BlockSpec((3, tr, cs), lambda i: (0, i, 0)), ANY],
        out_specs=[pl.BlockSpec((None, None, tr, cs), lambda i: (layer, lax.axis_index("c"), i, 0))],
        out_shape=[jax.ShapeDtypeStruct(buf.shape, F32)], sem=("parallel",), comms=comms, aliases={2: 0})
    return (res[0][0], res[1]) if comms else res[0]


def _adamw_math(w, g, m, v):
    m = ADAM_B1 * m + (1.0 - ADAM_B1) * g
    v = ADAM_B2 * v + (1.0 - ADAM_B2) * (g * g)
    m_hat = m / (1.0 - ADAM_B1 ** ADAM_STEP)
    v_hat = v / (1.0 - ADAM_B2 ** ADAM_STEP)
    return -ADAM_LR * (m_hat / (jnp.sqrt(v_hat) + ADAM_EPS) + ADAM_WD * w), m, v


def _adamw(w, g, m, v, name):
    layers, rows, cols = w.shape
    tr = _tile(rows, 256)

    def body(w_ref, g_ref, m_ref, v_ref, go_ref, d_ref, mo_ref, vo_ref):
        g = g_ref[...]
        go_ref[...] = g
        d_ref[...], mo_ref[...], vo_ref[...] = _adamw_math(w_ref[...], g, m_ref[...], v_ref[...])

    spec = pl.BlockSpec((None, tr, cols), lambda l, i: (l, i, 0))
    return _pcall(body, (w, g, m, v), name=name, grid=(layers, rows // tr), in_specs=[spec] * 4, out_specs=[spec] * 4,
                  out_shape=[jax.ShapeDtypeStruct(w.shape, F32)] * 4, sem=("parallel", "parallel"))


def _pack(arrays):
    flat = jnp.concatenate([a.reshape(-1) for a in arrays])
    pad = (-flat.shape[0]) % 1024
    return jnp.pad(flat, (0, pad)).reshape(-1, 128)


def _unpack(packed, like):
    flat = packed.reshape(-1)
    out, off = [], 0
    for a in like:
        out.append(flat[off:off + a.size].reshape(a.shape))
        off += a.size
    return out


class _SmallGather:
    def __init__(self, g, parts, patterns):
        self.ro, self.rw, self.patterns, self.n_sems = [g], [parts], patterns, len(patterns)

    def start(self, ro, rw, send, recv):
        x, y, c, _ = _place()
        for k, j in enumerate(self.patterns):
            _remote(ro[0], rw[0].at[4 * x + 2 * y + c], send(k), recv(k), _flip(x, y, c, j)).start()

    def finish(self, ro, rw, send, recv):
        x, y, c, _ = _place()
        for k, j in enumerate(self.patterns):
            px, py, pc = _flip(x, y, c, j)
            slot = rw[0].at[4 * px + 2 * py + pc]
            cp = _remote(slot, slot, send(k), recv(k), (x, y, c))
            cp.wait_recv()
            cp.wait_send()


def _small_update(own, parts, w, m, v, name):
    rows = w.shape[0]

    def body(own_ref, p_ref, w_ref, m_ref, v_ref, g_ref, d_ref, mo_ref, vo_ref):
        me = 4 * lax.axis_index("x") + 2 * lax.axis_index("y") + lax.axis_index("c")
        g = jnp.where(me == 0, own_ref[...], p_ref[0])
        for k in range(1, 8):
            g = g + jnp.where(me == k, own_ref[...], p_ref[k])
        g_ref[...] = g
        d_ref[...], mo_ref[...], vo_ref[...] = _adamw_math(w_ref[...], g, m_ref[...], v_ref[...])

    return pl.pallas_call(body, name=name, out_shape=[jax.ShapeDtypeStruct((rows, 128), F32)] * 4,
                          compiler_params=_params())(own, parts, w, m, v)


SMALL = ("norm_pre_mix", "norm_post_mix", "norm_pre_ffn", "norm_post_ffn", "sgu_ln_g", "sgu_ln_b", "sgu_w", "sgu_b")
BIG = (("ab_w_in", ("ab_w_in",)), ("ab_w_out", ("ab_w_out",)), ("sb_w_in", ("sb_w_in",)), ("sb_w_out", ("sb_w_out",)),
       ("ffn_w1", ("ffn_w1_0", "ffn_w1_1")), ("ffn_w2", ("ffn_w2_0", "ffn_w2_1")))
WEIGHTS = ("norm_pre_mix", "norm_post_mix", "norm_pre_ffn", "norm_post_ffn", "ab_w_in", "sgu_ln_g", "sgu_ln_b", "sgu_w", "sgu_b",
           "ab_w_out", "sb_w_in", "sb_w_out", "ffn_w1", "ffn_w2")


def kernel(x, norm_pre_mix, norm_post_mix, norm_pre_ffn, norm_post_ffn, ab_w_in, sgu_ln_g, sgu_ln_b, sgu_w, sgu_b, ab_w_out, sb_w_in, sb_w_out, ffn_w1, ffn_w2, loss_target, m_norm_pre_mix, m_norm_post_mix, m_norm_pre_ffn, m_norm_post_ffn, m_ab_w_in, m_sgu_ln_g, m_sgu_ln_b, m_sgu_w, m_sgu_b, m_ab_w_out, m_sb_w_in, m_sb_w_out, m_ffn_w1, m_ffn_w2, v_norm_pre_mix, v_norm_post_mix, v_norm_pre_ffn, v_norm_post_ffn, v_ab_w_in, v_sgu_ln_g, v_sgu_ln_b, v_sgu_w, v_sgu_b, v_ab_w_out, v_sb_w_in, v_sb_w_out, v_ffn_w1, v_ffn_w2):
    w = dict(norm_pre_mix=norm_pre_mix, norm_post_mix=norm_post_mix, norm_pre_ffn=norm_pre_ffn, norm_post_ffn=norm_post_ffn,
             ab_w_in=ab_w_in, sgu_ln_g=sgu_ln_g, sgu_ln_b=sgu_ln_b, sgu_w=sgu_w, sgu_b=sgu_b, ab_w_out=ab_w_out, sb_w_in=sb_w_in,
             sb_w_out=sb_w_out, ffn_w1=ffn_w1, ffn_w2=ffn_w2)
    m = dict(norm_pre_mix=m_norm_pre_mix, norm_post_mix=m_norm_post_mix, norm_pre_ffn=m_norm_pre_ffn, norm_post_ffn=m_norm_post_ffn,
             ab_w_in=m_ab_w_in, sgu_ln_g=m_sgu_ln_g, sgu_ln_b=m_sgu_ln_b, sgu_w=m_sgu_w, sgu_b=m_sgu_b, ab_w_out=m_ab_w_out,
             sb_w_in=m_sb_w_in, sb_w_out=m_sb_w_out, ffn_w1=m_ffn_w1, ffn_w2=m_ffn_w2)
    v = dict(norm_pre_mix=v_norm_pre_mix, norm_post_mix=v_norm_post_mix, norm_pre_ffn=v_norm_pre_ffn, norm_post_ffn=v_norm_post_ffn,
             ab_w_in=v_ab_w_in, sgu_ln_g=v_sgu_ln_g, sgu_ln_b=v_sgu_ln_b, sgu_w=v_sgu_w, sgu_b=v_sgu_b, ab_w_out=v_ab_w_out,
             sb_w_in=v_sb_w_in, sb_w_out=v_sb_w_out, ffn_w1=v_ffn_w1, ffn_w2=v_ffn_w2)
    big, pair, got = {}, {}, {}
    for name, keys in BIG:
        for layer, key in enumerate(keys):
            big[key] = _to_bf16_full(w[name], layer, KIND[key], f"bf16_{key}")
            half = _half_shape(big[key].shape, KIND[key])
            pair[key], got[key] = lax.empty((4,) + half, BF16), lax.empty((3,) + half, BF16)
    big["ab_w_in"] = _comm_call([_Gather(big["ab_w_in"], KIND["ab_w_in"])], "gather_first")[0][0]

    norms = {k: w["norm_" + k] for k in ("pre_mix", "post_mix", "pre_ffn", "post_ffn")}
    sgu = (sgu_ln_g, sgu_ln_b, sgu_w[0], sgu_b[0])
    loss_blk, grad_x, d_norms, d_sgu, (psum, got) = _local_step(x[0], loss_target[0], norms, sgu, big, (pair, got))
    loss = lax.psum(loss_blk[0, 0], ("x", "y", "c"))

    grads, deltas, new_m, new_v = {}, {}, {}, {}
    keys_of = dict(BIG)
    small_g = _pack([d_norms["pre_mix"], d_norms["post_mix"], d_norms["pre_ffn"], d_norms["post_ffn"],
                     d_sgu[0], d_sgu[1], d_sgu[2][None], d_sgu[3][None]])
    parts = lax.empty((8,) + small_g.shape, F32)
    small_todo = [(1, 2, 4, 6), (3, 5, 7)]
    bufs, pending = {}, None
    for name in ("ffn_w2", "ffn_w1", "sb_w_in", "sb_w_out", "ab_w_out"):
        buf = lax.empty((len(keys_of[name]), 2) + psum[keys_of[name][0]].shape[1:], F32)
        for layer, key in enumerate(keys_of[name]):
            if pending is not None:
                buf, rws = _owner_sum(psum[key], got[key], buf, layer, f"sum_{key}", comms=[_Join([bufs[pending]])])
                bufs[pending], pending = rws[0][0], None
            elif small_todo:
                buf, rws = _owner_sum(psum[key], got[key], buf, layer, f"sum_{key}",
                                      comms=[_SmallGather(small_g, parts, small_todo.pop(0))])
                parts = rws[0][0]
            else:
                buf = _owner_sum(psum[key], got[key], buf, layer, f"sum_{key}")
        bufs[name], pending = buf, name
    assert not small_todo

    rws = _comm_call([_Join([bufs["ab_w_out"]]), _ChipScatter(psum["ab_w_in"], got["ab_w_in"], DIAG)], "tail_comm")
    bufs["ab_w_out"], got["ab_w_in"] = rws[0][0], rws[1][0]
    bufs["ab_w_in"] = _owner_sum(psum["ab_w_in"], got["ab_w_in"], lax.empty((1, 2) + psum["ab_w_in"].shape[1:], F32), 0, "sum_ab_w_in")
    bufs["ab_w_in"] = _comm_call([_Join([bufs["ab_w_in"]])], "join_last")[0][0]
    for name, _ in BIG:
        grads[name], deltas[name], new_m[name], new_v[name] = _adamw(w[name], bufs[name].reshape(w[name].shape), m[name], v[name], f"adamw_{name}")

    outs = _small_update(small_g, parts, _pack([w[k] for k in SMALL]), _pack([m[k] for k in SMALL]), _pack([v[k] for k in SMALL]), "small_update")
    like = [w[k] for k in SMALL]
    for dst, packed in zip((grads, deltas, new_m, new_v), outs):
        for k, a in zip(SMALL, _unpack(packed, like)):
            dst[k] = a

    return (loss, grad_x[None], *[grads[k] for k in WEIGHTS], *[deltas[k] for k in WEIGHTS],
            *[new_m[k] for k in WEIGHTS], *[new_v[k] for k in WEIGHTS])
```
